```python
import math
import jax, jax.numpy as jnp
from jax import lax
import numpy as np

D_MODEL = 1024
BATCH = 8
SEQ = 4096
DEPTH = 4

HEAD_DIM = 64
LRU_WIDTH = D_MODEL // 2
LRU_BLOCKS = LRU_WIDTH // HEAD_DIM
LRU_CONV = 4
LRU_C = 8.0
LRU_MIN_RAD = 0.9
LRU_MAX_RAD = 0.999
FOX_HEADS = (D_MODEL // 2) // HEAD_DIM
FOX_DIM = FOX_HEADS * HEAD_DIM
SWA_HEADS = (D_MODEL // 2) // HEAD_DIM
SWA_KV_HEADS = max(1, SWA_HEADS // 4)
SWA_DIM = SWA_HEADS * HEAD_DIM
SWA_WINDOW = 128
S5_WIDTH = D_MODEL // 2
S5_GROUP = 16
S5_GROUPS = S5_WIDTH // S5_GROUP
S5_STATE = 64
D_FF = 256 * ((8 * D_MODEL // 3 + 255) // 256)
PLE_DIM = 256
ROPE_THETA = 10000.0
QBLOCK = 128
EPS = 1e-6
MACARON = 0.5
OUT_SCALE = 0.5
N_EVEN = (DEPTH + 1) // 2
N_ODD = DEPTH // 2
EV_IN = 2 * LRU_WIDTH + 3 * FOX_DIM + FOX_HEADS
OD_IN = SWA_DIM + 2 * SWA_KV_HEADS * HEAD_DIM + S5_WIDTH
MIX_WIDTH = LRU_WIDTH + FOX_DIM

kernel_name = "hybrid_rglru_fox_swa_s5_macaron"


def rms_norm(x, g):
    x32 = x.astype(jnp.float32)
    y = x32 * lax.rsqrt(jnp.mean(x32 * x32, axis=-1, keepdims=True) + EPS)
    return (y * g.astype(jnp.float32)).astype(x.dtype)


def swiglu(x, wg, wu, wd):
    return (jax.nn.silu(x @ wg) * (x @ wu)) @ wd


def rope(x, pos):
    half = x.shape[-1] // 2
    inv = jnp.power(ROPE_THETA, -jnp.arange(half, dtype=jnp.float32) / half)
    ang = pos.astype(jnp.float32)[:, None] * inv[None, :]
    cos = jnp.cos(ang)[None, :, None, :]
    sin = jnp.sin(ang)[None, :, None, :]
    x32 = x.astype(jnp.float32)
    x1, x2 = x32[..., :half], x32[..., half:]
    return jnp.concatenate([x1 * cos - x2 * sin, x2 * cos + x1 * sin], axis=-1).astype(x.dtype)


def linear_scan_combine(left, right):
    a1, b1 = left
    a2, b2 = right
    return a1 * a2, a2 * b1 + b2


def rg_lru(xa, conv_w, conv_b, wa, ba, wx, bx, lam):
    B_, S_, W = xa.shape
    xp = jnp.pad(xa, ((0, 0), (LRU_CONV - 1, 0), (0, 0)))
    xc = conv_b
    for tap in range(LRU_CONV):
        xc = xc + xp[:, tap:tap + S_] * conv_w[tap]
    xh = xc.reshape(B_, S_, LRU_BLOCKS, W // LRU_BLOCKS)
    r = jax.nn.sigmoid((jnp.einsum('bshi,hij->bshj', xh, wa).reshape(B_, S_, W) + ba).astype(jnp.float32))
    i = jax.nn.sigmoid((jnp.einsum('bshi,hij->bshj', xh, wx).reshape(B_, S_, W) + bx).astype(jnp.float32))
    log_a = -LRU_C * r * jax.nn.softplus(lam.astype(jnp.float32))
    a = jnp.exp(log_a)
    b = jnp.sqrt(-jnp.expm1(2.0 * log_a)) * (i * xc.astype(jnp.float32))
    _, h = lax.associative_scan(linear_scan_combine, (a, b), axis=1)
    return h.astype(xa.dtype)


def fox_attention(q, k, v, f_logit, b_f, qn, kn):
    B_, S_, H, Dh = q.shape
    q = rms_norm(q, qn)
    k = rms_norm(k, kn)
    log_f = jax.nn.log_sigmoid(f_logit.astype(jnp.float32) + b_f.astype(jnp.float32))
    c = jnp.cumsum(log_f, axis=1).transpose(0, 2, 1)
    nb = S_ // QBLOCK
    qb = q.transpose(0, 2, 1, 3).reshape(B_, H, nb, QBLOCK, Dh).transpose(2, 0, 1, 3, 4)
    cb = c.reshape(B_, H, nb, QBLOCK).transpose(2, 0, 1, 3)
    kh = k.transpose(0, 2, 1, 3)
    vh = v.transpose(0, 2, 1, 3)
    kpos = jnp.arange(S_)
    scale = Dh ** -0.5

    def block(args):
        qi, ci, n = args
        s = jnp.einsum('bhqd,bhkd->bhqk', qi, kh).astype(jnp.float32) * scale
        s = s + ci[..., None] - c[:, :, None, :]
        qpos = n * QBLOCK + jnp.arange(QBLOCK)
        mask = kpos[None, :] <= qpos[:, None]
        s = jnp.where(mask, s, -jnp.inf)
        pr = jax.nn.softmax(s, axis=-1)
        return jnp.einsum('bhqk,bhkd->bhqd', pr.astype(vh.dtype), vh)

    o = lax.map(block, (qb, cb, jnp.arange(nb)))
    return o.transpose(1, 0, 3, 2, 4).reshape(B_, S_, H * Dh)


def swa_sink_attention(q, k, v, sinks, qn, kn):
    B_, S_, H, Dh = q.shape
    KVH = k.shape[2]
    G = H // KVH
    W = SWA_WINDOW
    nb = S_ // W
    pos = jnp.arange(S_)
    q = rope(rms_norm(q, qn), pos)
    k = rope(rms_norm(k, kn), pos)
    qb = q.reshape(B_, nb, W, KVH, G, Dh)

    def band(t):
        tp = jnp.pad(t, ((0, 0), (W, 0), (0, 0), (0, 0))).reshape(B_, nb + 1, W, KVH, Dh)
        return jnp.concatenate([tp[:, :-1], tp[:, 1:]], axis=2)

    kb, vb = band(k), band(v)
    s = jnp.einsum('bnqkgd,bnjkd->bnkgqj', qb, kb).astype(jnp.float32) * Dh ** -0.5
    qi = jnp.arange(W)[:, None]
    kj = jnp.arange(2 * W)[None, :]
    diff = qi + W - kj
    key_pos = jnp.arange(nb)[:, None, None] * W - W + kj[None]
    mask = (diff >= 0)[None] & (diff < SWA_WINDOW)[None] & (key_pos >= 0)
    s = jnp.where(mask[None, :, None, None], s, -jnp.inf)
    sink = sinks.astype(jnp.float32).reshape(KVH, G)[None, None, :, :, None, None]
    m = jnp.maximum(jnp.max(s, axis=-1, keepdims=True), sink)
    e = jnp.exp(s - m)
    pr = e / (jnp.sum(e, axis=-1, keepdims=True) + jnp.exp(sink - m))
    o = jnp.einsum('bnkgqj,bnjkd->bnqkgd', pr.astype(vb.dtype), vb)
    return o.reshape(B_, S_, H * Dh)


def s5_glu(u, lam_re, lam_im, log_dt, b_re, b_im, c_re, c_im, d, glu_w, glu_b):
    B_, S_, _ = u.shape
    f32 = jnp.float32
    u32 = u.astype(f32)
    ug = u32.reshape(B_, S_, S5_GROUPS, S5_GROUP)
    lam = lax.complex(lam_re.astype(f32), lam_im.astype(f32))
    dt = jnp.exp(log_dt.astype(f32))[:, None]
    lam_bar = jnp.exp(lam * dt)
    bmat = lax.complex(b_re.astype(f32), b_im.astype(f32))
    b_bar = ((lam_bar - 1.0) / lam)[..., None] * bmat
    bu = jnp.einsum('gpc,bsgc->bsgp', b_bar, ug.astype(jnp.complex64))
    a = jnp.broadcast_to(lam_bar[None, None], (1, S_, S5_GROUPS, S5_STATE))
    _, h = lax.associative_scan(linear_scan_combine, (a, bu), axis=1)
    cmat = lax.complex(c_re.astype(f32), c_im.astype(f32))
    y = jnp.real(jnp.einsum('gcp,bsgp->bsgc', cmat, h)).reshape(B_, S_, S5_WIDTH)
    y = y + d.astype(f32) * u32
    z = jax.nn.gelu(y).astype(u.dtype)
    return z * jax.nn.sigmoid(z @ glu_w + glu_b)


def even_mixer(h, w_in, conv_w, conv_b, wa, ba, wx, bx, lam, b_f, qn, kn, w_out):
    B_, S_, _ = h.shape
    z = h @ w_in
    o1 = LRU_WIDTH
    o2 = o1 + LRU_WIDTH
    o3 = o2 + FOX_DIM
    o4 = o3 + FOX_DIM
    o5 = o4 + FOX_DIM
    xa, ya, q, k, v, f = jnp.split(z, [o1, o2, o3, o4, o5], axis=-1)
    a_out = jax.nn.gelu(ya) * rg_lru(xa, conv_w, conv_b, wa, ba, wx, bx, lam)
    hd = (B_, S_, FOX_HEADS, HEAD_DIM)
    b_out = fox_attention(q.reshape(hd), k.reshape(hd), v.reshape(hd), f, b_f, qn, kn)
    return jnp.concatenate([a_out, b_out], axis=-1) @ w_out


def odd_mixer(h, w_in, qn, kn, sinks, lam_re, lam_im, log_dt, b_re, b_im, c_re, c_im, d,
              glu_w, glu_b, w_out):
    B_, S_, _ = h.shape
    z = h @ w_in
    kvd = SWA_KV_HEADS * HEAD_DIM
    o1 = SWA_DIM
    o2 = o1 + kvd
    o3 = o2 + kvd
    q, k, v, u = jnp.split(z, [o1, o2, o3], axis=-1)
    c_out = swa_sink_attention(q.reshape(B_, S_, SWA_HEADS, HEAD_DIM),
                               k.reshape(B_, S_, SWA_KV_HEADS, HEAD_DIM),
                               v.reshape(B_, S_, SWA_KV_HEADS, HEAD_DIM), sinks, qn, kn)
    d_out = s5_glu(u, lam_re, lam_im, log_dt, b_re, b_im, c_re, c_im, d, glu_w, glu_b)
    return jnp.concatenate([c_out, d_out], axis=-1) @ w_out


def _fwd_setup_inputs(seed: int = 0) -> dict:
    key = jax.random.key(seed)
    ks = iter(jax.random.split(key, 64))
    f32 = jnp.float32

    def nrm(shape, scale):
        return jax.random.normal(next(ks), shape, f32) * scale

    def gain(shape):
        return 1.0 + 0.02 * jax.random.normal(next(ks), shape, f32)

    D, F, NE, NO = D_MODEL, D_FF, N_EVEN, N_ODD
    x = nrm((BATCH, SEQ, D), 1.0)
    p = nrm((DEPTH, BATCH, SEQ, PLE_DIM), 1.0)
    ffn1_norm = gain((DEPTH, D))
    ffn1_wg = nrm((DEPTH, D, F), D ** -0.5)
    ffn1_wu = nrm((DEPTH, D, F), D ** -0.5)
    ffn1_wd = nrm((DEPTH, F, D), F ** -0.5)
    mix_norm = gain((DEPTH, D))
    ffn2_norm = gain((DEPTH, D))
    ffn2_wg = nrm((DEPTH, D, F), D ** -0.5)
    ffn2_wu = nrm((DEPTH, D, F), D ** -0.5)
    ffn2_wd = nrm((DEPTH, F, D), F ** -0.5)
    ple_w = nrm((DEPTH, PLE_DIM, D), PLE_DIM ** -0.5)
    ple_norm = gain((DEPTH, D))
    ple_gate_norm = gain((DEPTH, D))
    ple_gate_w = nrm((DEPTH, D, D), D ** -0.5)
    ev_w_in = nrm((NE, D, EV_IN), D ** -0.5)
    lru_conv_w = nrm((NE, LRU_CONV, LRU_WIDTH), LRU_CONV ** -0.5)
    lru_conv_b = nrm((NE, LRU_WIDTH), 0.01)
    bs = LRU_WIDTH // LRU_BLOCKS
    lru_wa = nrm((NE, LRU_BLOCKS, bs, bs), bs ** -0.5)
    lru_ba = nrm((NE, LRU_WIDTH), 0.01)
    lru_wx = nrm((NE, LRU_BLOCKS, bs, bs), bs ** -0.5)
    lru_bx = nrm((NE, LRU_WIDTH), 0.01)
    unif = jax.random.uniform(next(ks), (NE, LRU_WIDTH), f32, LRU_MIN_RAD ** 2, LRU_MAX_RAD ** 2)
    lru_lambda = jnp.log(jnp.expm1(-0.5 * jnp.log(unif)))
    fox_bf = jax.random.uniform(next(ks), (NE, FOX_HEADS), f32, 1.0, 5.0)
    fox_q_norm = gain((NE, HEAD_DIM))
    fox_k_norm = gain((NE, HEAD_DIM))
    ev_w_out = nrm((NE, MIX_WIDTH, D), MIX_WIDTH ** -0.5 * OUT_SCALE)
    od_w_in = nrm((NO, D, OD_IN), D ** -0.5)
    swa_q_norm = gain((NO, HEAD_DIM))
    swa_k_norm = gain((NO, HEAD_DIM))
    swa_sinks = nrm((NO, SWA_HEADS), 0.5)
    n_idx = jnp.arange(S5_STATE, dtype=f32)
    s5_lambda_re = -0.5 + nrm((NO, S5_GROUPS, S5_STATE), 0.01)
    s5_lambda_im = jnp.pi * n_idx + nrm((NO, S5_GROUPS, S5_STATE), 0.01)
    s5_log_dt = jax.random.uniform(next(ks), (NO, S5_GROUPS), f32, math.log(1e-3), math.log(1e-1))
    s5_b_re = nrm((NO, S5_GROUPS, S5_STATE, S5_GROUP), (2 * S5_GROUP) ** -0.5)
    s5_b_im = nrm((NO, S5_GROUPS, S5_STATE, S5_GROUP), (2 * S5_GROUP) ** -0.5)
    s5_c_re = nrm((NO, S5_GROUPS, S5_GROUP, S5_STATE), S5_STATE ** -0.5)
    s5_c_im = nrm((NO, S5_GROUPS, S5_GROUP, S5_STATE), S5_STATE ** -0.5)
    s5_d = nrm((NO, S5_WIDTH), 1.0)
    s5_glu_w = nrm((NO, S5_WIDTH, S5_WIDTH), S5_WIDTH ** -0.5)
    s5_glu_b = nrm((NO, S5_WIDTH), 0.01)
    od_w_out = nrm((NO, MIX_WIDTH, D), MIX_WIDTH ** -0.5 * OUT_SCALE)
    return {
        "x": x, "p": p,
        "ffn1_norm": ffn1_norm, "ffn1_wg": ffn1_wg, "ffn1_wu": ffn1_wu, "ffn1_wd": ffn1_wd,
        "mix_norm": mix_norm,
        "ffn2_norm": ffn2_norm, "ffn2_wg": ffn2_wg, "ffn2_wu": ffn2_wu, "ffn2_wd": ffn2_wd,
        "ple_w": ple_w, "ple_norm": ple_norm, "ple_gate_norm": ple_gate_norm, "ple_gate_w": ple_gate_w,
        "ev_w_in": ev_w_in, "lru_conv_w": lru_conv_w, "lru_conv_b": lru_conv_b,
        "lru_wa": lru_wa, "lru_ba": lru_ba, "lru_wx": lru_wx, "lru_bx": lru_bx,
        "lru_lambda": lru_lambda, "fox_bf": fox_bf, "fox_q_norm": fox_q_norm,
        "fox_k_norm": fox_k_norm, "ev_w_out": ev_w_out,
        "od_w_in": od_w_in, "swa_q_norm": swa_q_norm, "swa_k_norm": swa_k_norm,
        "swa_sinks": swa_sinks, "s5_lambda_re": s5_lambda_re, "s5_lambda_im": s5_lambda_im,
        "s5_log_dt": s5_log_dt, "s5_b_re": s5_b_re, "s5_b_im": s5_b_im,
        "s5_c_re": s5_c_re, "s5_c_im": s5_c_im, "s5_d": s5_d,
        "s5_glu_w": s5_glu_w, "s5_glu_b": s5_glu_b, "od_w_out": od_w_out,
    }


def _fwd_reference(x, p, ffn1_norm, ffn1_wg, ffn1_wu, ffn1_wd, mix_norm,
              ffn2_norm, ffn2_wg, ffn2_wu, ffn2_wd,
              ple_w, ple_norm, ple_gate_norm, ple_gate_w,
              ev_w_in, lru_conv_w, lru_conv_b, lru_wa, lru_ba, lru_wx, lru_bx,
              lru_lambda, fox_bf, fox_q_norm, fox_k_norm, ev_w_out,
              od_w_in, swa_q_norm, swa_k_norm, swa_sinks, s5_lambda_re, s5_lambda_im,
              s5_log_dt, s5_b_re, s5_b_im, s5_c_re, s5_c_im, s5_d,
              s5_glu_w, s5_glu_b, od_w_out):
    for i in range(DEPTH):
        x = x + MACARON * swiglu(rms_norm(x, ffn1_norm[i]), ffn1_wg[i], ffn1_wu[i], ffn1_wd[i])
        h = rms_norm(x, mix_norm[i])
        if i % 2 == 0:
            j = i // 2
            x = x + even_mixer(h, ev_w_in[j], lru_conv_w[j], lru_conv_b[j], lru_wa[j], lru_ba[j],
                               lru_wx[j], lru_bx[j], lru_lambda[j], fox_bf[j],
                               fox_q_norm[j], fox_k_norm[j], ev_w_out[j])
        else:
            j = i // 2
            x = x + odd_mixer(h, od_w_in[j], swa_q_norm[j], swa_k_norm[j], swa_sinks[j],
                              s5_lambda_re[j], s5_lambda_im[j], s5_log_dt[j], s5_b_re[j],
                              s5_b_im[j], s5_c_re[j], s5_c_im[j], s5_d[j],
                              s5_glu_w[j], s5_glu_b[j], od_w_out[j])
        x = x + MACARON * swiglu(rms_norm(x, ffn2_norm[i]), ffn2_wg[i], ffn2_wu[i], ffn2_wd[i])
        e = rms_norm(p[i] @ ple_w[i], ple_norm[i])
        g = jax.nn.sigmoid(rms_norm(x, ple_gate_norm[i]) @ ple_gate_w[i])
        x = x + g * e
    return x


import jax as _jax
import jax.numpy as _jnp

TWIN_FORMAT = 'train_step'
FWD_PARAMS = ['x', 'p', 'ffn1_norm', 'ffn1_wg', 'ffn1_wu', 'ffn1_wd', 'mix_norm', 'ffn2_norm', 'ffn2_wg', 'ffn2_wu', 'ffn2_wd', 'ple_w', 'ple_norm', 'ple_gate_norm', 'ple_gate_w', 'ev_w_in', 'lru_conv_w', 'lru_conv_b', 'lru_wa', 'lru_ba', 'lru_wx', 'lru_bx', 'lru_lambda', 'fox_bf', 'fox_q_norm', 'fox_k_norm', 'ev_w_out', 'od_w_in', 'swa_q_norm', 'swa_k_norm', 'swa_sinks', 's5_lambda_re', 's5_lambda_im', 's5_log_dt', 's5_b_re', 's5_b_im', 's5_c_re', 's5_c_im', 's5_d', 's5_glu_w', 's5_glu_b', 'od_w_out']
TWIN_WEIGHTS = ['ffn1_norm', 'ffn1_wg', 'ffn1_wu', 'ffn1_wd', 'mix_norm', 'ffn2_norm', 'ffn2_wg', 'ffn2_wu', 'ffn2_wd', 'ple_w', 'ple_norm', 'ple_gate_norm', 'ple_gate_w', 'ev_w_in', 'lru_conv_w', 'lru_conv_b', 'lru_wa', 'lru_ba', 'lru_wx', 'lru_bx', 'lru_lambda', 'fox_bf', 'fox_q_norm', 'fox_k_norm', 'ev_w_out', 'od_w_in', 'swa_q_norm', 'swa_k_norm', 'swa_sinks', 's5_lambda_re', 's5_lambda_im', 's5_log_dt', 's5_b_re', 's5_b_im', 's5_c_re', 's5_c_im', 's5_d', 's5_glu_w', 's5_glu_b', 'od_w_out']
TWIN_DIFF_INPUT = 'x'
TWIN_INPUTS = ['x', 'p', 'ffn1_norm', 'ffn1_wg', 'ffn1_wu', 'ffn1_wd', 'mix_norm', 'ffn2_norm', 'ffn2_wg', 'ffn2_wu', 'ffn2_wd', 'ple_w', 'ple_norm', 'ple_gate_norm', 'ple_gate_w', 'ev_w_in', 'lru_conv_w', 'lru_conv_b', 'lru_wa', 'lru_ba', 'lru_wx', 'lru_bx', 'lru_lambda', 'fox_bf', 'fox_q_norm', 'fox_k_norm', 'ev_w_out', 'od_w_in', 'swa_q_norm', 'swa_k_norm', 'swa_sinks', 's5_lambda_re', 's5_lambda_im', 's5_log_dt', 's5_b_re', 's5_b_im', 's5_c_re', 's5_c_im', 's5_d', 's5_glu_w', 's5_glu_b', 'od_w_out', 'loss_target', 'm_ffn1_norm', 'm_ffn1_wg', 'm_ffn1_wu', 'm_ffn1_wd', 'm_mix_norm', 'm_ffn2_norm', 'm_ffn2_wg', 'm_ffn2_wu', 'm_ffn2_wd', 'm_ple_w', 'm_ple_norm', 'm_ple_gate_norm', 'm_ple_gate_w', 'm_ev_w_in', 'm_lru_conv_w', 'm_lru_conv_b', 'm_lru_wa', 'm_lru_ba', 'm_lru_wx', 'm_lru_bx', 'm_lru_lambda', 'm_fox_bf', 'm_fox_q_norm', 'm_fox_k_norm', 'm_ev_w_out', 'm_od_w_in', 'm_swa_q_norm', 'm_swa_k_norm', 'm_swa_sinks', 'm_s5_lambda_re', 'm_s5_lambda_im', 'm_s5_log_dt', 'm_s5_b_re', 'm_s5_b_im', 'm_s5_c_re', 'm_s5_c_im', 'm_s5_d', 'm_s5_glu_w', 'm_s5_glu_b', 'm_od_w_out', 'v_ffn1_norm', 'v_ffn1_wg', 'v_ffn1_wu', 'v_ffn1_wd', 'v_mix_norm', 'v_ffn2_norm', 'v_ffn2_wg', 'v_ffn2_wu', 'v_ffn2_wd', 'v_ple_w', 'v_ple_norm', 'v_ple_gate_norm', 'v_ple_gate_w', 'v_ev_w_in', 'v_lru_conv_w', 'v_lru_conv_b', 'v_lru_wa', 'v_lru_ba', 'v_lru_wx', 'v_lru_bx', 'v_lru_lambda', 'v_fox_bf', 'v_fox_q_norm', 'v_fox_k_norm', 'v_ev_w_out', 'v_od_w_in', 'v_swa_q_norm', 'v_swa_k_norm', 'v_swa_sinks', 'v_s5_lambda_re', 'v_s5_lambda_im', 'v_s5_log_dt', 'v_s5_b_re', 'v_s5_b_im', 'v_s5_c_re', 'v_s5_c_im', 'v_s5_d', 'v_s5_glu_w', 'v_s5_glu_b', 'v_od_w_out']
TWIN_OUTPUTS = ['loss', 'grad_x', 'grad_ffn1_norm', 'grad_ffn1_wg', 'grad_ffn1_wu', 'grad_ffn1_wd', 'grad_mix_norm', 'grad_ffn2_norm', 'grad_ffn2_wg', 'grad_ffn2_wu', 'grad_ffn2_wd', 'grad_ple_w', 'grad_ple_norm', 'grad_ple_gate_norm', 'grad_ple_gate_w', 'grad_ev_w_in', 'grad_lru_conv_w', 'grad_lru_conv_b', 'grad_lru_wa', 'grad_lru_ba', 'grad_lru_wx', 'grad_lru_bx', 'grad_lru_lambda', 'grad_fox_bf', 'grad_fox_q_norm', 'grad_fox_k_norm', 'grad_ev_w_out', 'grad_od_w_in', 'grad_swa_q_norm', 'grad_swa_k_norm', 'grad_swa_sinks', 'grad_s5_lambda_re', 'grad_s5_lambda_im', 'grad_s5_log_dt', 'grad_s5_b_re', 'grad_s5_b_im', 'grad_s5_c_re', 'grad_s5_c_im', 'grad_s5_d', 'grad_s5_glu_w', 'grad_s5_glu_b', 'grad_od_w_out', 'delta_ffn1_norm', 'delta_ffn1_wg', 'delta_ffn1_wu', 'delta_ffn1_wd', 'delta_mix_norm', 'delta_ffn2_norm', 'delta_ffn2_wg', 'delta_ffn2_wu', 'delta_ffn2_wd', 'delta_ple_w', 'delta_ple_norm', 'delta_ple_gate_norm', 'delta_ple_gate_w', 'delta_ev_w_in', 'delta_lru_conv_w', 'delta_lru_conv_b', 'delta_lru_wa', 'delta_lru_ba', 'delta_lru_wx', 'delta_lru_bx', 'delta_lru_lambda', 'delta_fox_bf', 'delta_fox_q_norm', 'delta_fox_k_norm', 'delta_ev_w_out', 'delta_od_w_in', 'delta_swa_q_norm', 'delta_swa_k_norm', 'delta_swa_sinks', 'delta_s5_lambda_re', 'delta_s5_lambda_im', 'delta_s5_log_dt', 'delta_s5_b_re', 'delta_s5_b_im', 'delta_s5_c_re', 'delta_s5_c_im', 'delta_s5_d', 'delta_s5_glu_w', 'delta_s5_glu_b', 'delta_od_w_out', 'new_m_ffn1_norm', 'new_m_ffn1_wg', 'new_m_ffn1_wu', 'new_m_ffn1_wd', 'new_m_mix_norm', 'new_m_ffn2_norm', 'new_m_ffn2_wg', 'new_m_ffn2_wu', 'new_m_ffn2_wd', 'new_m_ple_w', 'new_m_ple_norm', 'new_m_ple_gate_norm', 'new_m_ple_gate_w', 'new_m_ev_w_in', 'new_m_lru_conv_w', 'new_m_lru_conv_b', 'new_m_lru_wa', 'new_m_lru_ba', 'new_m_lru_wx', 'new_m_lru_bx', 'new_m_lru_lambda', 'new_m_fox_bf', 'new_m_fox_q_norm', 'new_m_fox_k_norm', 'new_m_ev_w_out', 'new_m_od_w_in', 'new_m_swa_q_norm', 'new_m_swa_k_norm', 'new_m_swa_sinks', 'new_m_s5_lambda_re', 'new_m_s5_lambda_im', 'new_m_s5_log_dt', 'new_m_s5_b_re', 'new_m_s5_b_im', 'new_m_s5_c_re', 'new_m_s5_c_im', 'new_m_s5_d', 'new_m_s5_glu_w', 'new_m_s5_glu_b', 'new_m_od_w_out', 'new_v_ffn1_norm', 'new_v_ffn1_wg', 'new_v_ffn1_wu', 'new_v_ffn1_wd', 'new_v_mix_norm', 'new_v_ffn2_norm', 'new_v_ffn2_wg', 'new_v_ffn2_wu', 'new_v_ffn2_wd', 'new_v_ple_w', 'new_v_ple_norm', 'new_v_ple_gate_norm', 'new_v_ple_gate_w', 'new_v_ev_w_in', 'new_v_lru_conv_w', 'new_v_lru_conv_b', 'new_v_lru_wa', 'new_v_lru_ba', 'new_v_lru_wx', 'new_v_lru_bx', 'new_v_lru_lambda', 'new_v_fox_bf', 'new_v_fox_q_norm', 'new_v_fox_k_norm', 'new_v_ev_w_out', 'new_v_od_w_in', 'new_v_swa_q_norm', 'new_v_swa_k_norm', 'new_v_swa_sinks', 'new_v_s5_lambda_re', 'new_v_s5_lambda_im', 'new_v_s5_log_dt', 'new_v_s5_b_re', 'new_v_s5_b_im', 'new_v_s5_c_re', 'new_v_s5_c_im', 'new_v_s5_d', 'new_v_s5_glu_w', 'new_v_s5_glu_b', 'new_v_od_w_out']
TWIN_LEAF_KINDS = {'loss': 'loss', 'grad_x': 'grad_x', 'grad_ffn1_norm': 'grad_w', 'grad_ffn1_wg': 'grad_w', 'grad_ffn1_wu': 'grad_w', 'grad_ffn1_wd': 'grad_w', 'grad_mix_norm': 'grad_w', 'grad_ffn2_norm': 'grad_w', 'grad_ffn2_wg': 'grad_w', 'grad_ffn2_wu': 'grad_w', 'grad_ffn2_wd': 'grad_w', 'grad_ple_w': 'grad_w', 'grad_ple_norm': 'grad_w', 'grad_ple_gate_norm': 'grad_w', 'grad_ple_gate_w': 'grad_w', 'grad_ev_w_in': 'grad_w', 'grad_lru_conv_w': 'grad_w', 'grad_lru_conv_b': 'grad_w', 'grad_lru_wa': 'grad_w', 'grad_lru_ba': 'grad_w', 'grad_lru_wx': 'grad_w', 'grad_lru_bx': 'grad_w', 'grad_lru_lambda': 'grad_w', 'grad_fox_bf': 'grad_w', 'grad_fox_q_norm': 'grad_w', 'grad_fox_k_norm': 'grad_w', 'grad_ev_w_out': 'grad_w', 'grad_od_w_in': 'grad_w', 'grad_swa_q_norm': 'grad_w', 'grad_swa_k_norm': 'grad_w', 'grad_swa_sinks': 'grad_w', 'grad_s5_lambda_re': 'grad_w', 'grad_s5_lambda_im': 'grad_w', 'grad_s5_log_dt': 'grad_w', 'grad_s5_b_re': 'grad_w', 'grad_s5_b_im': 'grad_w', 'grad_s5_c_re': 'grad_w', 'grad_s5_c_im': 'grad_w', 'grad_s5_d': 'grad_w', 'grad_s5_glu_w': 'grad_w', 'grad_s5_glu_b': 'grad_w', 'grad_od_w_out': 'grad_w', 'delta_ffn1_norm': 'delta_w', 'delta_ffn1_wg': 'delta_w', 'delta_ffn1_wu': 'delta_w', 'delta_ffn1_wd': 'delta_w', 'delta_mix_norm': 'delta_w', 'delta_ffn2_norm': 'delta_w', 'delta_ffn2_wg': 'delta_w', 'delta_ffn2_wu': 'delta_w', 'delta_ffn2_wd': 'delta_w', 'delta_ple_w': 'delta_w', 'delta_ple_norm': 'delta_w', 'delta_ple_gate_norm': 'delta_w', 'delta_ple_gate_w': 'delta_w', 'delta_ev_w_in': 'delta_w', 'delta_lru_conv_w': 'delta_w', 'delta_lru_conv_b': 'delta_w', 'delta_lru_wa': 'delta_w', 'delta_lru_ba': 'delta_w', 'delta_lru_wx': 'delta_w', 'delta_lru_bx': 'delta_w', 'delta_lru_lambda': 'delta_w', 'delta_fox_bf': 'delta_w', 'delta_fox_q_norm': 'delta_w', 'delta_fox_k_norm': 'delta_w', 'delta_ev_w_out': 'delta_w', 'delta_od_w_in': 'delta_w', 'delta_swa_q_norm': 'delta_w', 'delta_swa_k_norm': 'delta_w', 'delta_swa_sinks': 'delta_w', 'delta_s5_lambda_re': 'delta_w', 'delta_s5_lambda_im': 'delta_w', 'delta_s5_log_dt': 'delta_w', 'delta_s5_b_re': 'delta_w', 'delta_s5_b_im': 'delta_w', 'delta_s5_c_re': 'delta_w', 'delta_s5_c_im': 'delta_w', 'delta_s5_d': 'delta_w', 'delta_s5_glu_w': 'delta_w', 'delta_s5_glu_b': 'delta_w', 'delta_od_w_out': 'delta_w', 'new_m_ffn1_norm': 'new_m', 'new_m_ffn1_wg': 'new_m', 'new_m_ffn1_wu': 'new_m', 'new_m_ffn1_wd': 'new_m', 'new_m_mix_norm': 'new_m', 'new_m_ffn2_norm': 'new_m', 'new_m_ffn2_wg': 'new_m', 'new_m_ffn2_wu': 'new_m', 'new_m_ffn2_wd': 'new_m', 'new_m_ple_w': 'new_m', 'new_m_ple_norm': 'new_m', 'new_m_ple_gate_norm': 'new_m', 'new_m_ple_gate_w': 'new_m', 'new_m_ev_w_in': 'new_m', 'new_m_lru_conv_w': 'new_m', 'new_m_lru_conv_b': 'new_m', 'new_m_lru_wa': 'new_m', 'new_m_lru_ba': 'new_m', 'new_m_lru_wx': 'new_m', 'new_m_lru_bx': 'new_m', 'new_m_lru_lambda': 'new_m', 'new_m_fox_bf': 'new_m', 'new_m_fox_q_norm': 'new_m', 'new_m_fox_k_norm': 'new_m', 'new_m_ev_w_out': 'new_m', 'new_m_od_w_in': 'new_m', 'new_m_swa_q_norm': 'new_m', 'new_m_swa_k_norm': 'new_m', 'new_m_swa_sinks': 'new_m', 'new_m_s5_lambda_re': 'new_m', 'new_m_s5_lambda_im': 'new_m', 'new_m_s5_log_dt': 'new_m', 'new_m_s5_b_re': 'new_m', 'new_m_s5_b_im': 'new_m', 'new_m_s5_c_re': 'new_m', 'new_m_s5_c_im': 'new_m', 'new_m_s5_d': 'new_m', 'new_m_s5_glu_w': 'new_m', 'new_m_s5_glu_b': 'new_m', 'new_m_od_w_out': 'new_m', 'new_v_ffn1_norm': 'new_v', 'new_v_ffn1_wg': 'new_v', 'new_v_ffn1_wu': 'new_v', 'new_v_ffn1_wd': 'new_v', 'new_v_mix_norm': 'new_v', 'new_v_ffn2_norm': 'new_v', 'new_v_ffn2_wg': 'new_v', 'new_v_ffn2_wu': 'new_v', 'new_v_ffn2_wd': 'new_v', 'new_v_ple_w': 'new_v', 'new_v_ple_norm': 'new_v', 'new_v_ple_gate_norm': 'new_v', 'new_v_ple_gate_w': 'new_v', 'new_v_ev_w_in': 'new_v', 'new_v_lru_conv_w': 'new_v', 'new_v_lru_conv_b': 'new_v', 'new_v_lru_wa': 'new_v', 'new_v_lru_ba': 'new_v', 'new_v_lru_wx': 'new_v', 'new_v_lru_bx': 'new_v', 'new_v_lru_lambda': 'new_v', 'new_v_fox_bf': 'new_v', 'new_v_fox_q_norm': 'new_v', 'new_v_fox_k_norm': 'new_v', 'new_v_ev_w_out': 'new_v', 'new_v_od_w_in': 'new_v', 'new_v_swa_q_norm': 'new_v', 'new_v_swa_k_norm': 'new_v', 'new_v_swa_sinks': 'new_v', 'new_v_s5_lambda_re': 'new_v', 'new_v_s5_lambda_im': 'new_v', 'new_v_s5_log_dt': 'new_v', 'new_v_s5_b_re': 'new_v', 'new_v_s5_b_im': 'new_v', 'new_v_s5_c_re': 'new_v', 'new_v_s5_c_im': 'new_v', 'new_v_s5_d': 'new_v', 'new_v_s5_glu_w': 'new_v', 'new_v_s5_glu_b': 'new_v', 'new_v_od_w_out': 'new_v'}


def _forward(args):
    return _fwd_reference(*[args[k] for k in FWD_PARAMS])


def _output_shape():
    out = _jax.eval_shape(lambda: _forward(_fwd_setup_inputs(0)))
    return out.shape, out.dtype

N_MICROBATCH = 1
ADAM_LR = 0.001
ADAM_B1 = 0.9
ADAM_B2 = 0.999
ADAM_EPS = 1e-08
ADAM_WD = 0.01
ADAM_STEP = 10
PER_EXAMPLE_BATCH_AXIS = {'x': 0, 'p': 1, 'loss_target': 0}
SHARED_INPUTS = []
_WEIGHT_DTYPES = {'ffn1_norm': _jnp.float32, 'ffn1_wg': _jnp.float32, 'ffn1_wu': _jnp.float32, 'ffn1_wd': _jnp.float32, 'mix_norm': _jnp.float32, 'ffn2_norm': _jnp.float32, 'ffn2_wg': _jnp.float32, 'ffn2_wu': _jnp.float32, 'ffn2_wd': _jnp.float32, 'ple_w': _jnp.float32, 'ple_norm': _jnp.float32, 'ple_gate_norm': _jnp.float32, 'ple_gate_w': _jnp.float32, 'ev_w_in': _jnp.float32, 'lru_conv_w': _jnp.float32, 'lru_conv_b': _jnp.float32, 'lru_wa': _jnp.float32, 'lru_ba': _jnp.float32, 'lru_wx': _jnp.float32, 'lru_bx': _jnp.float32, 'lru_lambda': _jnp.float32, 'fox_bf': _jnp.float32, 'fox_q_norm': _jnp.float32, 'fox_k_norm': _jnp.float32, 'ev_w_out': _jnp.float32, 'od_w_in': _jnp.float32, 'swa_q_norm': _jnp.float32, 'swa_k_norm': _jnp.float32, 'swa_sinks': _jnp.float32, 's5_lambda_re': _jnp.float32, 's5_lambda_im': _jnp.float32, 's5_log_dt': _jnp.float32, 's5_b_re': _jnp.float32, 's5_b_im': _jnp.float32, 's5_c_re': _jnp.float32, 's5_c_im': _jnp.float32, 's5_d': _jnp.float32, 's5_glu_w': _jnp.float32, 's5_glu_b': _jnp.float32, 'od_w_out': _jnp.float32}
MOMENT_SCALE = {'ffn1_norm': 6.080689e+00, 'ffn1_wg': 9.326105e-02, 'ffn1_wu': 1.013839e-01, 'ffn1_wd': 1.663584e-01, 'mix_norm': 1.257249e+00, 'ffn2_norm': 6.148340e+00, 'ffn2_wg': 8.836914e-02, 'ffn2_wu': 9.747113e-02, 'ffn2_wd': 1.595398e-01, 'ple_w': 2.329565e-01, 'ple_norm': 9.294599e+00, 'ple_gate_norm': 9.094459e-01, 'ple_gate_w': 1.288499e-01, 'ev_w_in': 1.092153e-01, 'lru_conv_w': 1.036966e+00, 'lru_conv_b': 3.506556e+00, 'lru_wa': 1.912833e-01, 'lru_ba': 1.546177e-01, 'lru_wx': 3.499508e-01, 'lru_bx': 5.707911e-01, 'lru_lambda': 3.526698e-01, 'fox_bf': 1.653587e+01, 'fox_q_norm': 2.121651e+00, 'fox_k_norm': 2.117300e+00, 'ev_w_out': 2.510998e-01, 'od_w_in': 9.604478e-02, 'swa_q_norm': 7.568199e-01, 'swa_k_norm': 7.314427e-01, 'swa_sinks': 1.807487e-01, 's5_lambda_re': 9.816725e-03, 's5_lambda_im': 6.640309e-03, 's5_log_dt': 5.742200e+00, 's5_b_re': 5.052703e-03, 's5_b_im': 4.636990e-03, 's5_c_re': 7.251996e-03, 's5_c_im': 7.035938e-03, 's5_d': 1.359040e+00, 's5_glu_w': 2.451079e-01, 's5_glu_b': 7.522088e-01, 'od_w_out': 5.550759e-01}


def _to_microbatches(a, axis):
    t = _jnp.moveaxis(a, axis, 0)
    t = t.reshape((N_MICROBATCH, t.shape[0] // N_MICROBATCH) + t.shape[1:])
    return _jnp.moveaxis(t, 1, axis + 1)


def setup_inputs(seed: int = 0) -> dict:
    inp = _fwd_setup_inputs(seed)
    key = _jax.random.fold_in(_jax.random.key(seed), 7919)
    shape, _ = _output_shape()
    out = dict(inp)
    out["loss_target"] = _jax.random.normal(_jax.random.fold_in(key, 0), shape, _jnp.float32)
    for i, name in enumerate(TWIN_WEIGHTS):
        w = inp[name].astype(_jnp.float32)
        if MOMENT_SCALE is None:
            s = _jnp.sqrt(_jnp.mean(_jnp.square(w)) + 1e-30)
        else:
            s = MOMENT_SCALE[name]
        km, kv = _jax.random.split(_jax.random.fold_in(key, i + 1))
        out[name] = w
        out["m_" + name] = s * _jax.random.normal(km, w.shape, _jnp.float32)
        out["v_" + name] = (s * s) * _jax.random.uniform(kv, w.shape, _jnp.float32, 0.5, 1.5)
    if N_MICROBATCH > 1:
        for name, axis in PER_EXAMPLE_BATCH_AXIS.items():
            out[name] = _to_microbatches(out[name], axis)
    return {'x': out['x'], 'p': out['p'], 'ffn1_norm': out['ffn1_norm'], 'ffn1_wg': out['ffn1_wg'], 'ffn1_wu': out['ffn1_wu'], 'ffn1_wd': out['ffn1_wd'], 'mix_norm': out['mix_norm'], 'ffn2_norm': out['ffn2_norm'], 'ffn2_wg': out['ffn2_wg'], 'ffn2_wu': out['ffn2_wu'], 'ffn2_wd': out['ffn2_wd'], 'ple_w': out['ple_w'], 'ple_norm': out['ple_norm'], 'ple_gate_norm': out['ple_gate_norm'], 'ple_gate_w': out['ple_gate_w'], 'ev_w_in': out['ev_w_in'], 'lru_conv_w': out['lru_conv_w'], 'lru_conv_b': out['lru_conv_b'], 'lru_wa': out['lru_wa'], 'lru_ba': out['lru_ba'], 'lru_wx': out['lru_wx'], 'lru_bx': out['lru_bx'], 'lru_lambda': out['lru_lambda'], 'fox_bf': out['fox_bf'], 'fox_q_norm': out['fox_q_norm'], 'fox_k_norm': out['fox_k_norm'], 'ev_w_out': out['ev_w_out'], 'od_w_in': out['od_w_in'], 'swa_q_norm': out['swa_q_norm'], 'swa_k_norm': out['swa_k_norm'], 'swa_sinks': out['swa_sinks'], 's5_lambda_re': out['s5_lambda_re'], 's5_lambda_im': out['s5_lambda_im'], 's5_log_dt': out['s5_log_dt'], 's5_b_re': out['s5_b_re'], 's5_b_im': out['s5_b_im'], 's5_c_re': out['s5_c_re'], 's5_c_im': out['s5_c_im'], 's5_d': out['s5_d'], 's5_glu_w': out['s5_glu_w'], 's5_glu_b': out['s5_glu_b'], 'od_w_out': out['od_w_out'], 'loss_target': out['loss_target'], 'm_ffn1_norm': out['m_ffn1_norm'], 'm_ffn1_wg': out['m_ffn1_wg'], 'm_ffn1_wu': out['m_ffn1_wu'], 'm_ffn1_wd': out['m_ffn1_wd'], 'm_mix_norm': out['m_mix_norm'], 'm_ffn2_norm': out['m_ffn2_norm'], 'm_ffn2_wg': out['m_ffn2_wg'], 'm_ffn2_wu': out['m_ffn2_wu'], 'm_ffn2_wd': out['m_ffn2_wd'], 'm_ple_w': out['m_ple_w'], 'm_ple_norm': out['m_ple_norm'], 'm_ple_gate_norm': out['m_ple_gate_norm'], 'm_ple_gate_w': out['m_ple_gate_w'], 'm_ev_w_in': out['m_ev_w_in'], 'm_lru_conv_w': out['m_lru_conv_w'], 'm_lru_conv_b': out['m_lru_conv_b'], 'm_lru_wa': out['m_lru_wa'], 'm_lru_ba': out['m_lru_ba'], 'm_lru_wx': out['m_lru_wx'], 'm_lru_bx': out['m_lru_bx'], 'm_lru_lambda': out['m_lru_lambda'], 'm_fox_bf': out['m_fox_bf'], 'm_fox_q_norm': out['m_fox_q_norm'], 'm_fox_k_norm': out['m_fox_k_norm'], 'm_ev_w_out': out['m_ev_w_out'], 'm_od_w_in': out['m_od_w_in'], 'm_swa_q_norm': out['m_swa_q_norm'], 'm_swa_k_norm': out['m_swa_k_norm'], 'm_swa_sinks': out['m_swa_sinks'], 'm_s5_lambda_re': out['m_s5_lambda_re'], 'm_s5_lambda_im': out['m_s5_lambda_im'], 'm_s5_log_dt': out['m_s5_log_dt'], 'm_s5_b_re': out['m_s5_b_re'], 'm_s5_b_im': out['m_s5_b_im'], 'm_s5_c_re': out['m_s5_c_re'], 'm_s5_c_im': out['m_s5_c_im'], 'm_s5_d': out['m_s5_d'], 'm_s5_glu_w': out['m_s5_glu_w'], 'm_s5_glu_b': out['m_s5_glu_b'], 'm_od_w_out': out['m_od_w_out'], 'v_ffn1_norm': out['v_ffn1_norm'], 'v_ffn1_wg': out['v_ffn1_wg'], 'v_ffn1_wu': out['v_ffn1_wu'], 'v_ffn1_wd': out['v_ffn1_wd'], 'v_mix_norm': out['v_mix_norm'], 'v_ffn2_norm': out['v_ffn2_norm'], 'v_ffn2_wg': out['v_ffn2_wg'], 'v_ffn2_wu': out['v_ffn2_wu'], 'v_ffn2_wd': out['v_ffn2_wd'], 'v_ple_w': out['v_ple_w'], 'v_ple_norm': out['v_ple_norm'], 'v_ple_gate_norm': out['v_ple_gate_norm'], 'v_ple_gate_w': out['v_ple_gate_w'], 'v_ev_w_in': out['v_ev_w_in'], 'v_lru_conv_w': out['v_lru_conv_w'], 'v_lru_conv_b': out['v_lru_conv_b'], 'v_lru_wa': out['v_lru_wa'], 'v_lru_ba': out['v_lru_ba'], 'v_lru_wx': out['v_lru_wx'], 'v_lru_bx': out['v_lru_bx'], 'v_lru_lambda': out['v_lru_lambda'], 'v_fox_bf': out['v_fox_bf'], 'v_fox_q_norm': out['v_fox_q_norm'], 'v_fox_k_norm': out['v_fox_k_norm'], 'v_ev_w_out': out['v_ev_w_out'], 'v_od_w_in': out['v_od_w_in'], 'v_swa_q_norm': out['v_swa_q_norm'], 'v_swa_k_norm': out['v_swa_k_norm'], 'v_swa_sinks': out['v_swa_sinks'], 'v_s5_lambda_re': out['v_s5_lambda_re'], 'v_s5_lambda_im': out['v_s5_lambda_im'], 'v_s5_log_dt': out['v_s5_log_dt'], 'v_s5_b_re': out['v_s5_b_re'], 'v_s5_b_im': out['v_s5_b_im'], 'v_s5_c_re': out['v_s5_c_re'], 'v_s5_c_im': out['v_s5_c_im'], 'v_s5_d': out['v_s5_d'], 'v_s5_glu_w': out['v_s5_glu_w'], 'v_s5_glu_b': out['v_s5_glu_b'], 'v_od_w_out': out['v_od_w_out']}


def _loss(weights, diff, rest, loss_target):
    with _jax.named_scope("forward"):
        args = {**rest, TWIN_DIFF_INPUT: diff, **{k: w.astype(_WEIGHT_DTYPES[k]) for k, w in weights.items()}}
        y = _forward(args)
    with _jax.named_scope("loss_head"):
        err = _jnp.square(y.astype(_jnp.float32) - loss_target)
        return 0.5 * _jnp.sum(_jnp.mean(err, axis=-1)) if err.ndim else 0.5 * err


def _adamw(w, g, m, v):
    m = ADAM_B1 * m + (1.0 - ADAM_B1) * g
    v = ADAM_B2 * v + (1.0 - ADAM_B2) * _jnp.square(g)
    m_hat = m / (1.0 - ADAM_B1 ** ADAM_STEP)
    v_hat = v / (1.0 - ADAM_B2 ** ADAM_STEP)
    delta = -ADAM_LR * (m_hat / (_jnp.sqrt(v_hat) + ADAM_EPS) + ADAM_WD * w)
    return delta, m, v


def reference(x, p, ffn1_norm, ffn1_wg, ffn1_wu, ffn1_wd, mix_norm, ffn2_norm, ffn2_wg, ffn2_wu, ffn2_wd, ple_w, ple_norm, ple_gate_norm, ple_gate_w, ev_w_in, lru_conv_w, lru_conv_b, lru_wa, lru_ba, lru_wx, lru_bx, lru_lambda, fox_bf, fox_q_norm, fox_k_norm, ev_w_out, od_w_in, swa_q_norm, swa_k_norm, swa_sinks, s5_lambda_re, s5_lambda_im, s5_log_dt, s5_b_re, s5_b_im, s5_c_re, s5_c_im, s5_d, s5_glu_w, s5_glu_b, od_w_out, loss_target, m_ffn1_norm, m_ffn1_wg, m_ffn1_wu, m_ffn1_wd, m_mix_norm, m_ffn2_norm, m_ffn2_wg, m_ffn2_wu, m_ffn2_wd, m_ple_w, m_ple_norm, m_ple_gate_norm, m_ple_gate_w, m_ev_w_in, m_lru_conv_w, m_lru_conv_b, m_lru_wa, m_lru_ba, m_lru_wx, m_lru_bx, m_lru_lambda, m_fox_bf, m_fox_q_norm, m_fox_k_norm, m_ev_w_out, m_od_w_in, m_swa_q_norm, m_swa_k_norm, m_swa_sinks, m_s5_lambda_re, m_s5_lambda_im, m_s5_log_dt, m_s5_b_re, m_s5_b_im, m_s5_c_re, m_s5_c_im, m_s5_d, m_s5_glu_w, m_s5_glu_b, m_od_w_out, v_ffn1_norm, v_ffn1_wg, v_ffn1_wu, v_ffn1_wd, v_mix_norm, v_ffn2_norm, v_ffn2_wg, v_ffn2_wu, v_ffn2_wd, v_ple_w, v_ple_norm, v_ple_gate_norm, v_ple_gate_w, v_ev_w_in, v_lru_conv_w, v_lru_conv_b, v_lru_wa, v_lru_ba, v_lru_wx, v_lru_bx, v_lru_lambda, v_fox_bf, v_fox_q_norm, v_fox_k_norm, v_ev_w_out, v_od_w_in, v_swa_q_norm, v_swa_k_norm, v_swa_sinks, v_s5_lambda_re, v_s5_lambda_im, v_s5_log_dt, v_s5_b_re, v_s5_b_im, v_s5_c_re, v_s5_c_im, v_s5_d, v_s5_glu_w, v_s5_glu_b, v_od_w_out):
    given = dict(x=x, p=p, ffn1_norm=ffn1_norm, ffn1_wg=ffn1_wg, ffn1_wu=ffn1_wu, ffn1_wd=ffn1_wd, mix_norm=mix_norm, ffn2_norm=ffn2_norm, ffn2_wg=ffn2_wg, ffn2_wu=ffn2_wu, ffn2_wd=ffn2_wd, ple_w=ple_w, ple_norm=ple_norm, ple_gate_norm=ple_gate_norm, ple_gate_w=ple_gate_w, ev_w_in=ev_w_in, lru_conv_w=lru_conv_w, lru_conv_b=lru_conv_b, lru_wa=lru_wa, lru_ba=lru_ba, lru_wx=lru_wx, lru_bx=lru_bx, lru_lambda=lru_lambda, fox_bf=fox_bf, fox_q_norm=fox_q_norm, fox_k_norm=fox_k_norm, ev_w_out=ev_w_out, od_w_in=od_w_in, swa_q_norm=swa_q_norm, swa_k_norm=swa_k_norm, swa_sinks=swa_sinks, s5_lambda_re=s5_lambda_re, s5_lambda_im=s5_lambda_im, s5_log_dt=s5_log_dt, s5_b_re=s5_b_re, s5_b_im=s5_b_im, s5_c_re=s5_c_re, s5_c_im=s5_c_im, s5_d=s5_d, s5_glu_w=s5_glu_w, s5_glu_b=s5_glu_b, od_w_out=od_w_out, loss_target=loss_target, m_ffn1_norm=m_ffn1_norm, m_ffn1_wg=m_ffn1_wg, m_ffn1_wu=m_ffn1_wu, m_ffn1_wd=m_ffn1_wd, m_mix_norm=m_mix_norm, m_ffn2_norm=m_ffn2_norm, m_ffn2_wg=m_ffn2_wg, m_ffn2_wu=m_ffn2_wu, m_ffn2_wd=m_ffn2_wd, m_ple_w=m_ple_w, m_ple_norm=m_ple_norm, m_ple_gate_norm=m_ple_gate_norm, m_ple_gate_w=m_ple_gate_w, m_ev_w_in=m_ev_w_in, m_lru_conv_w=m_lru_conv_w, m_lru_conv_b=m_lru_conv_b, m_lru_wa=m_lru_wa, m_lru_ba=m_lru_ba, m_lru_wx=m_lru_wx, m_lru_bx=m_lru_bx, m_lru_lambda=m_lru_lambda, m_fox_bf=m_fox_bf, m_fox_q_norm=m_fox_q_norm, m_fox_k_norm=m_fox_k_norm, m_ev_w_out=m_ev_w_out, m_od_w_in=m_od_w_in, m_swa_q_norm=m_swa_q_norm, m_swa_k_norm=m_swa_k_norm, m_swa_sinks=m_swa_sinks, m_s5_lambda_re=m_s5_lambda_re, m_s5_lambda_im=m_s5_lambda_im, m_s5_log_dt=m_s5_log_dt, m_s5_b_re=m_s5_b_re, m_s5_b_im=m_s5_b_im, m_s5_c_re=m_s5_c_re, m_s5_c_im=m_s5_c_im, m_s5_d=m_s5_d, m_s5_glu_w=m_s5_glu_w, m_s5_glu_b=m_s5_glu_b, m_od_w_out=m_od_w_out, v_ffn1_norm=v_ffn1_norm, v_ffn1_wg=v_ffn1_wg, v_ffn1_wu=v_ffn1_wu, v_ffn1_wd=v_ffn1_wd, v_mix_norm=v_mix_norm, v_ffn2_norm=v_ffn2_norm, v_ffn2_wg=v_ffn2_wg, v_ffn2_wu=v_ffn2_wu, v_ffn2_wd=v_ffn2_wd, v_ple_w=v_ple_w, v_ple_norm=v_ple_norm, v_ple_gate_norm=v_ple_gate_norm, v_ple_gate_w=v_ple_gate_w, v_ev_w_in=v_ev_w_in, v_lru_conv_w=v_lru_conv_w, v_lru_conv_b=v_lru_conv_b, v_lru_wa=v_lru_wa, v_lru_ba=v_lru_ba, v_lru_wx=v_lru_wx, v_lru_bx=v_lru_bx, v_lru_lambda=v_lru_lambda, v_fox_bf=v_fox_bf, v_fox_q_norm=v_fox_q_norm, v_fox_k_norm=v_fox_k_norm, v_ev_w_out=v_ev_w_out, v_od_w_in=v_od_w_in, v_swa_q_norm=v_swa_q_norm, v_swa_k_norm=v_swa_k_norm, v_swa_sinks=v_swa_sinks, v_s5_lambda_re=v_s5_lambda_re, v_s5_lambda_im=v_s5_lambda_im, v_s5_log_dt=v_s5_log_dt, v_s5_b_re=v_s5_b_re, v_s5_b_im=v_s5_b_im, v_s5_c_re=v_s5_c_re, v_s5_c_im=v_s5_c_im, v_s5_d=v_s5_d, v_s5_glu_w=v_s5_glu_w, v_s5_glu_b=v_s5_glu_b, v_od_w_out=v_od_w_out)
    weights = {n: given[n] for n in TWIN_WEIGHTS}
    shared = {n: given[n] for n in SHARED_INPUTS}
    per_example = {n: given[n] for n in ['x', 'p']}
    grad_fn = _jax.value_and_grad(_loss, argnums=(0, 1))

    def one_microbatch(ex, loss_target):
        ex = dict(ex)
        diff = ex.pop(TWIN_DIFF_INPUT)
        return grad_fn(weights, diff, {**shared, **ex}, loss_target)

    if N_MICROBATCH == 1:
        loss, (grad_w, grad_x) = one_microbatch(per_example, given["loss_target"])
    else:
        def body(carry, xs):
            loss_sum, grad_sum = carry
            l_k, (gw_k, gx_k) = one_microbatch(xs[0], xs[1])
            with _jax.named_scope("update"):
                return (loss_sum + l_k, _jax.tree.map(_jnp.add, grad_sum, gw_k)), gx_k

        init = (_jnp.zeros((), _jnp.float32), _jax.tree.map(_jnp.zeros_like, weights))
        (loss, grad_w), grad_x = _jax.lax.scan(body, init, (per_example, given["loss_target"]))
    with _jax.named_scope("update"):
        delta_w, new_m, new_v = {}, {}, {}
        for n in TWIN_WEIGHTS:
            delta_w[n], new_m[n], new_v[n] = _adamw(weights[n], grad_w[n], given["m_" + n], given["v_" + n])
    return (loss, grad_x, *[grad_w[n] for n in TWIN_WEIGHTS], *[delta_w[n] for n in TWIN_WEIGHTS],
            *[new_m[n] for n in TWIN_WEIGHTS], *[new_v[n] for n in TWIN_WEIGHTS])
```

```python
import functools
import math

import jax
import jax.numpy as jnp
from jax import lax
from jax.experimental import pallas as pl
from jax.experimental.pallas import tpu as pltpu

F32 = jnp.float32
BF16 = jnp.bfloat16
MXU_DTYPE = BF16
HI = lax.Precision.HIGHEST
MESH = pl.DeviceIdType.MESH

VMEM_LIMIT_BYTES = 56 * 1024 * 1024
ROW_TILE_BYTES = 5 * 1024 * 1024
LANES = 128
SUBLANES = 8

HEAD_DIM = 64
LRU_BLOCKS = 8
LRU_CONV = 4
LRU_C = 8.0
SWA_WINDOW = 128
SWA_GROUP = 4
S5_GROUP = 16
S5_GROUPS = 32
S5_STATE = 64
ROPE_THETA = 10000.0
EPS = 1e-6
MACARON = 0.5
NEG = -1e30

ADAM_LR = 0.001
ADAM_B1 = 0.9
ADAM_B2 = 0.999
ADAM_EPS = 1e-08
ADAM_WD = 0.01
ADAM_STEP = 10

N_CHIPS = 4
N_DEV = 8


def _pick(n, cands):
    for c in cands:
        if n % c == 0:
            return c
    return n


def _params(sem=None):
    return pltpu.CompilerParams(dimension_semantics=sem, vmem_limit_bytes=VMEM_LIMIT_BYTES)


def rowwise(fn, rows, consts=(), outs=(), accs=(), name="rowwise", periods=None):
    rows, consts = list(rows), list(consts)
    n_r, n_c, n_o, n_a = len(rows), len(consts), len(outs), len(accs)
    R = rows[0].shape[0]
    periods = list(periods) if periods is not None else [None] * n_r
    per_row = sum(max(r.shape[1], LANES) * 4 for r in rows) + sum(max(f, LANES) * 4 for f, _ in outs)
    limit = min([R] + [p for p in periods if p is not None])
    tr = limit
    for c in (1024, 512, 256, 128, 64, 32, 16):
        if c <= limit and limit % c == 0 and R % c == 0 and c * per_row <= ROW_TILE_BYTES:
            tr = c
            break

    def row_map(period):
        if period is None:
            return lambda i: (i, 0)
        nb = period // tr
        return lambda i: (i % nb, 0)

    in_specs = [pl.BlockSpec((tr, r.shape[1]), row_map(p)) for r, p in zip(rows, periods)]
    in_specs += [pl.BlockSpec(c.shape, lambda i: (0, 0)) for c in consts]
    out_shape = [jax.ShapeDtypeStruct((R, f), dt) for f, dt in outs]
    out_shape += [jax.ShapeDtypeStruct(tuple(s), F32) for s in accs]
    out_specs = [pl.BlockSpec((tr, f), lambda i: (i, 0)) for f, _ in outs]
    out_specs += [pl.BlockSpec(tuple(s), lambda i: (0, 0)) for s in accs]

    def body(*refs):
        ins = [r[...] for r in refs[:n_r + n_c]]
        o_refs = refs[n_r + n_c:n_r + n_c + n_o]
        a_refs = refs[n_r + n_c + n_o:]
        ro, ra = fn(*ins)
        for ref, val in zip(o_refs, ro):
            ref[...] = val.astype(ref.dtype)
        if n_a:
            @pl.when(pl.program_id(0) == 0)
            def _():
                for ref in a_refs:
                    ref[...] = jnp.zeros(ref.shape, ref.dtype)
            for ref, val in zip(a_refs, ra):
                ref[...] += val.astype(F32)

    res = pl.pallas_call(
        body, grid=(R // tr,), in_specs=in_specs, out_specs=out_specs, out_shape=out_shape,
        name=name, compiler_params=_params(("arbitrary",)),
    )(*rows, *consts)
    return list(res)


def whole(fn, ins, outs, name="whole"):
    n_i = len(ins)

    def body(*refs):
        vals = fn(*[r[...] for r in refs[:n_i]])
        for ref, val in zip(refs[n_i:], vals):
            ref[...] = val.astype(ref.dtype)

    res = pl.pallas_call(
        body, out_shape=[jax.ShapeDtypeStruct(tuple(s), dt) for s, dt in outs],
        in_specs=[pl.BlockSpec(memory_space=pltpu.VMEM)] * n_i,
        out_specs=[pl.BlockSpec(memory_space=pltpu.VMEM)] * len(outs),
        name=name, compiler_params=_params(),
    )(*ins)
    return list(res)


_DOT_DIMS = {
    "nn": (((1,), (0,)), ((), ())),
    "nt": (((1,), (1,)), ((), ())),
    "tn": (((0,), (0,)), ((), ())),
}


def mm(a, b, mode="nn", out_dtype=F32, name="mm"):
    lane, sub = (512, 256, 128), (512, 256, 128, 64, 32, 16, 8)
    if mode == "nn":
        (M, K), (K2, N) = a.shape, b.shape
    elif mode == "nt":
        (M, K), (N, K2) = a.shape, b.shape
    else:
        (K, M), (K2, N) = a.shape, b.shape
    assert K == K2, (mode, a.shape, b.shape)
    tn = _pick(N, lane)
    if mode == "tn":
        tm, tk = _pick(M, lane), _pick(K, sub)
        a_spec = pl.BlockSpec((tk, tm), lambda i, j, k: (k, i))
    else:
        tm, tk = _pick(M, sub), _pick(K, lane)
        a_spec = pl.BlockSpec((tm, tk), lambda i, j, k: (i, k))
    if mode == "nt":
        b_spec = pl.BlockSpec((tn, tk), lambda i, j, k: (j, k))
    else:
        b_spec = pl.BlockSpec((tk, tn), lambda i, j, k: (k, j))
    nk = K // tk
    dims = _DOT_DIMS[mode]

    def body(a_ref, b_ref, o_ref, acc_ref):
        k = pl.program_id(2)

        @pl.when(k == 0)
        def _():
            acc_ref[...] = jnp.zeros(acc_ref.shape, F32)

        acc_ref[...] += lax.dot_general(a_ref[...].astype(MXU_DTYPE), b_ref[...].astype(MXU_DTYPE), dims,
                                        preferred_element_type=F32)

        @pl.when(k == nk - 1)
        def _():
            o_ref[...] = acc_ref[...].astype(o_ref.dtype)

    return pl.pallas_call(
        body, grid=(M // tm, N // tn, nk), in_specs=[a_spec, b_spec],
        out_specs=pl.BlockSpec((tm, tn), lambda i, j, k: (i, j)),
        out_shape=jax.ShapeDtypeStruct((M, N), out_dtype),
        scratch_shapes=[pltpu.VMEM((tm, tn), F32)],
        name=name, compiler_params=_params(("parallel", "parallel", "arbitrary")),
    )(a, b)


def _roll_rows(x, d, reverse):
    return pltpu.roll(x, (SUBLANES - d) if reverse else d, 0)


def scan_real(a, b, reverse=False, name="scan_real"):
    S, W = b.shape
    cw = _pick(W, (256, 128))
    n_tiles = S // SUBLANES

    def body(a_ref, b_ref, o_ref):
        row = lax.broadcasted_iota(jnp.int32, (SUBLANES, cw), 0)
        edge = 0 if reverse else SUBLANES - 1

        def step(i, carry):
            t = (n_tiles - 1 - i) if reverse else i
            off = pl.multiple_of(t * SUBLANES, SUBLANES)
            A = a_ref[pl.ds(off, SUBLANES), :]
            B = b_ref[pl.ds(off, SUBLANES), :]
            for d in (1, 2, 4):
                m = (row < SUBLANES - d) if reverse else (row >= d)
                B = jnp.where(m, A * _roll_rows(B, d, reverse) + B, B)
                A = jnp.where(m, A * _roll_rows(A, d, reverse), A)
            h = B + A * carry
            o_ref[pl.ds(off, SUBLANES), :] = h
            return jnp.sum(jnp.where(row == edge, h, 0.0), axis=0, keepdims=True)

        lax.fori_loop(0, n_tiles, step, jnp.zeros((1, cw), F32))

    spec = pl.BlockSpec((S, cw), lambda j: (0, j))
    return pl.pallas_call(
        body, grid=(W // cw,), in_specs=[spec, spec], out_specs=spec,
        out_shape=jax.ShapeDtypeStruct((S, W), F32), name=name, compiler_params=_params(("parallel",)),
    )(a, b)


def scan_cplx(lam, bu, reverse=False, name="scan_cplx"):
    S, C = bu.shape
    n_tiles = S // SUBLANES
    half = LANES

    def cmul(ar, ai, br, bi):
        return ar * br - ai * bi, ar * bi + ai * br

    def body(lam_ref, bu_ref, o_ref):
        row = lax.broadcasted_iota(jnp.int32, (SUBLANES, half), 0)
        lr = lam_ref[:, :half]
        li = lam_ref[:, half:]
        if reverse:
            li = -li
        l1 = (lr, li)
        l2 = cmul(*l1, *l1)
        l4 = cmul(*l2, *l2)
        pr = jnp.zeros((SUBLANES, half), F32)
        pi = jnp.zeros((SUBLANES, half), F32)
        p = l1
        for r in range(SUBLANES):
            sel = row == ((SUBLANES - 1 - r) if reverse else r)
            pr = jnp.where(sel, p[0], pr)
            pi = jnp.where(sel, p[1], pi)
            p = cmul(*p, *l1)

        def step(i, carry):
            cr, ci = carry
            t = (n_tiles - 1 - i) if reverse else i
            off = pl.multiple_of(t * SUBLANES, SUBLANES)
            Br = bu_ref[pl.ds(off, SUBLANES), :half]
            Bi = bu_ref[pl.ds(off, SUBLANES), half:]
            for d, (qr, qi) in ((1, l1), (2, l2), (4, l4)):
                m = (row < SUBLANES - d) if reverse else (row >= d)
                sr, si = _roll_rows(Br, d, reverse), _roll_rows(Bi, d, reverse)
                nr = jnp.where(m, Br + qr * sr - qi * si, Br)
                ni = jnp.where(m, Bi + qr * si + qi * sr, Bi)
                Br, Bi = nr, ni
            hr = Br + pr * cr - pi * ci
            hi = Bi + pr * ci + pi * cr
            o_ref[pl.ds(off, SUBLANES), :half] = hr
            o_ref[pl.ds(off, SUBLANES), half:] = hi
            at_edge = row == (0 if reverse else SUBLANES - 1)
            return (jnp.sum(jnp.where(at_edge, hr, 0.0), axis=0, keepdims=True),
                    jnp.sum(jnp.where(at_edge, hi, 0.0), axis=0, keepdims=True))

        z = jnp.zeros((1, half), F32)
        lax.fori_loop(0, n_tiles, step, (z, z))

    spec = pl.BlockSpec((S, 2 * half), lambda j: (0, j))
    return pl.pallas_call(
        body, grid=(C // (2 * half),), in_specs=[pl.BlockSpec((1, 2 * half), lambda j: (0, j)), spec],
        out_specs=spec, out_shape=jax.ShapeDtypeStruct((S, C), F32), name=name,
        compiler_params=_params(("parallel",)),
    )(lam, bu)


def _attn_tile(S, window):
    return window if window is not None else _pick(S, (512, 256, 128))


def attn_fwd(q, k, v, sink, cq=None, ck=None, window=None, name="attn_fwd"):
    H, S, Dh = q.shape
    G = H // k.shape[0]
    T = _attn_tile(S, window)
    nq = S // T
    nks = nq if window is None else 2
    scale = Dh ** -0.5
    bias = cq is not None

    def kv_block(i, j):
        return jnp.minimum(j, i) if window is None else jnp.maximum(i - 1 + j, 0)

    def body(*refs):
        if bias:
            q_ref, k_ref, v_ref, s_ref, cq_ref, ck_ref, o_ref, lse_ref, m_scr, l_scr, acc_scr = refs
        else:
            q_ref, k_ref, v_ref, s_ref, o_ref, lse_ref, m_scr, l_scr, acc_scr = refs
        i, j = pl.program_id(1), pl.program_id(2)

        @pl.when(j == 0)
        def _():
            m_scr[...] = jnp.zeros(m_scr.shape, F32) + s_ref[0]
            l_scr[...] = jnp.ones(l_scr.shape, F32)
            acc_scr[...] = jnp.zeros(acc_scr.shape, F32)

        active = (j <= i) if window is None else (i - 1 + j >= 0)

        @pl.when(active)
        def _():
            kb = kv_block(i, j)
            s = lax.dot_general(q_ref[0].astype(MXU_DTYPE), k_ref[0].astype(MXU_DTYPE), _DOT_DIMS["nt"],
                                preferred_element_type=F32) * scale
            if bias:
                s = s + cq_ref[0] - ck_ref[0]
            qpos = i * T + lax.broadcasted_iota(jnp.int32, (T, T), 0)
            kpos = kb * T + lax.broadcasted_iota(jnp.int32, (T, T), 1)
            valid = kpos <= qpos
            if window is not None:
                valid = valid & (qpos - kpos < window)
            s = jnp.where(valid, s, NEG)
            m_old = m_scr[...]
            m_new = jnp.maximum(m_old, jnp.max(s, axis=-1, keepdims=True))
            alpha = jnp.exp(m_old - m_new)
            p = jnp.where(valid, jnp.exp(s - m_new), 0.0)
            l_scr[...] = alpha * l_scr[...] + jnp.sum(p, axis=-1, keepdims=True)
            acc_scr[...] = alpha * acc_scr[...] + jnp.dot(p.astype(MXU_DTYPE), v_ref[0].astype(MXU_DTYPE),
                                                          preferred_element_type=F32)
            m_scr[...] = m_new

        @pl.when(j == nks - 1)
        def _():
            o_ref[0] = acc_scr[...] / l_scr[...]
            lse_ref[0] = m_scr[...] + jnp.log(l_scr[...])

    in_specs = [
        pl.BlockSpec((1, T, Dh), lambda h, i, j: (h, i, 0)),
        pl.BlockSpec((1, T, Dh), lambda h, i, j: (h // G, kv_block(i, j), 0)),
        pl.BlockSpec((1, T, Dh), lambda h, i, j: (h // G, kv_block(i, j), 0)),
        pl.BlockSpec((1, 1, 1), lambda h, i, j: (h, 0, 0)),
    ]
    args = [q, k, v, sink]
    if bias:
        in_specs += [pl.BlockSpec((1, T, 1), lambda h, i, j: (h, i, 0)),
                     pl.BlockSpec((1, 1, T), lambda h, i, j: (h, 0, kv_block(i, j)))]
        args += [cq, ck]
    return pl.pallas_call(
        body, grid=(H, nq, nks), in_specs=in_specs,
        out_specs=[pl.BlockSpec((1, T, Dh), lambda h, i, j: (h, i, 0)),
                   pl.BlockSpec((1, T, 1), lambda h, i, j: (h, i, 0))],
        out_shape=[jax.ShapeDtypeStruct((H, S, Dh), F32), jax.ShapeDtypeStruct((H, S, 1), F32)],
        scratch_shapes=[pltpu.VMEM((T, 1), F32), pltpu.VMEM((T, 1), F32), pltpu.VMEM((T, Dh), F32)],
        name=name, compiler_params=_params(("parallel", "parallel", "arbitrary")),
    )(*args)


def attn_bwd(q, k, v, lse, do, delta, cq=None, ck=None, window=None, name="attn_bwd"):
    H, S, Dh = q.shape
    KVH = k.shape[0]
    G = H // KVH
    T = _attn_tile(S, window)
    nq = S // T
    nqs = nq if window is None else 2
    scale = Dh ** -0.5
    bias = cq is not None
    assert not bias or G == 1

    def q_block(kb, j):
        return jnp.maximum(j, kb) if window is None else jnp.minimum(kb + j, nq - 1)

    def body(*refs):
        if bias:
            (q_ref, k_ref, v_ref, lse_ref, do_ref, dl_ref, cq_ref, ck_ref,
             dq_ref, dk_ref, dv_ref, dcq_ref, dck_ref) = refs
        else:
            q_ref, k_ref, v_ref, lse_ref, do_ref, dl_ref, dq_ref, dk_ref, dv_ref = refs
        kb, g, j = pl.program_id(1), pl.program_id(2), pl.program_id(3)

        @pl.when((g == 0) & (j == 0))
        def _():
            dk_ref[...] = jnp.zeros(dk_ref.shape, F32)
            dv_ref[...] = jnp.zeros(dv_ref.shape, F32)
            if bias:
                dck_ref[...] = jnp.zeros(dck_ref.shape, F32)

        @pl.when((kb == 0) & (g == 0) & (j == 0))
        def _():
            dq_ref[...] = jnp.zeros(dq_ref.shape, F32)
            if bias:
                dcq_ref[...] = jnp.zeros(dcq_ref.shape, F32)

        active = (j >= kb) if window is None else (kb + j <= nq - 1)

        @pl.when(active)
        def _():
            qi = q_block(kb, j)
            off = pl.multiple_of(qi * T, T)
            qb, kk, vv = q_ref[0].astype(MXU_DTYPE), k_ref[0].astype(MXU_DTYPE), v_ref[0].astype(MXU_DTYPE)
            dob = do_ref[0].astype(MXU_DTYPE)
            s = lax.dot_general(qb, kk, _DOT_DIMS["nt"], preferred_element_type=F32) * scale
            if bias:
                s = s + cq_ref[0] - ck_ref[0]
            qpos = qi * T + lax.broadcasted_iota(jnp.int32, (T, T), 0)
            kpos = kb * T + lax.broadcasted_iota(jnp.int32, (T, T), 1)
            valid = kpos <= qpos
            if window is not None:
                valid = valid & (qpos - kpos < window)
            p = jnp.where(valid, jnp.exp(jnp.where(valid, s, NEG) - lse_ref[0]), 0.0)
            dv_ref[0] += lax.dot_general(p.astype(MXU_DTYPE), dob, _DOT_DIMS["tn"], preferred_element_type=F32)
            dp = lax.dot_general(dob, vv, _DOT_DIMS["nt"], preferred_element_type=F32)
            ds = p * (dp - dl_ref[0])
            dsb = ds.astype(MXU_DTYPE)
            dq_ref[0, g, pl.ds(off, T), :] += scale * jnp.dot(dsb, kk, preferred_element_type=F32)
            dk_ref[0] += scale * lax.dot_general(dsb, qb, _DOT_DIMS["tn"], preferred_element_type=F32)
            if bias:
                dcq_ref[0, g, pl.ds(off, T), :] += jnp.sum(ds, axis=1, keepdims=True)
                dck_ref[0] -= jnp.sum(ds, axis=0, keepdims=True)

    def qmap(kvh, kb, g, j):
        return (kvh * G + g, q_block(kb, j), 0)

    in_specs = [
        pl.BlockSpec((1, T, Dh), qmap),
        pl.BlockSpec((1, T, Dh), lambda kvh, kb, g, j: (kvh, kb, 0)),
        pl.BlockSpec((1, T, Dh), lambda kvh, kb, g, j: (kvh, kb, 0)),
        pl.BlockSpec((1, T, 1), qmap),
        pl.BlockSpec((1, T, Dh), qmap),
        pl.BlockSpec((1, T, 1), qmap),
    ]
    args = [q, k, v, lse, do, delta]
    out_specs = [
        pl.BlockSpec((1, G, S, Dh), lambda kvh, kb, g, j: (kvh, 0, 0, 0)),
        pl.BlockSpec((1, T, Dh), lambda kvh, kb, g, j: (kvh, kb, 0)),
        pl.BlockSpec((1, T, Dh), lambda kvh, kb, g, j: (kvh, kb, 0)),
    ]
    out_shape = [jax.ShapeDtypeStruct((KVH, G, S, Dh), F32), jax.ShapeDtypeStruct((KVH, S, Dh), F32),
                 jax.ShapeDtypeStruct((KVH, S, Dh), F32)]
    if bias:
        in_specs += [pl.BlockSpec((1, T, 1), qmap),
                     pl.BlockSpec((1, 1, T), lambda kvh, kb, g, j: (kvh, 0, kb))]
        args += [cq, ck]
        out_specs += [pl.BlockSpec((1, G, S, 1), lambda kvh, kb, g, j: (kvh, 0, 0, 0)),
                      pl.BlockSpec((1, 1, T), lambda kvh, kb, g, j: (kvh, 0, kb))]
        out_shape += [jax.ShapeDtypeStruct((KVH, G, S, 1), F32), jax.ShapeDtypeStruct((KVH, 1, S), F32)]
    res = pl.pallas_call(
        body, grid=(KVH, nq, G, nqs), in_specs=in_specs, out_specs=out_specs, out_shape=out_shape,
        name=name, compiler_params=_params(("arbitrary", "arbitrary", "arbitrary", "arbitrary")),
    )(*args)
    dq = res[0].reshape(H, S, Dh)
    if bias:
        return dq, res[1], res[2], res[3].reshape(H, S, 1), res[4]
    return dq, res[1], res[2]


def _rms(x, g):
    return x * lax.rsqrt(jnp.mean(x * x, axis=-1, keepdims=True) + EPS) * g


def _sigmoid(x):
    return 1.0 / (1.0 + jnp.exp(-x))


def _softplus(x):
    return jnp.maximum(x, 0.0) + jnp.log(1.0 + jnp.exp(-jnp.abs(x)))


def _log_sigmoid(x):
    return jnp.minimum(x, 0.0) - jnp.log(1.0 + jnp.exp(-jnp.abs(x)))


def _gelu(x):
    return 0.5 * x * (1.0 + jnp.tanh(math.sqrt(2.0 / math.pi) * (x + 0.044715 * (x * x * x))))


def _silu(x):
    return x * _sigmoid(x)


def _ffn_act(gu):
    f = gu.shape[1] // 2
    return MACARON * _silu(gu[:, :f]) * gu[:, f:]


def _qk_prep(rope):
    def f(x, *rest):
        if rope:
            cos, sin, g, rot = rest
        else:
            (g,) = rest
        y = _rms(x, g)
        if rope:
            y = y * cos + jnp.dot(y, rot, precision=HI, preferred_element_type=F32) * sin
        return y
    return f


def _lru_gates(pre, xc, ba, bx, lam):
    w = xc.shape[1]
    r = _sigmoid(pre[:, :w] + ba)
    i = _sigmoid(pre[:, w:] + bx)
    log_a = -LRU_C * r * _softplus(lam)
    a = jnp.exp(log_a)
    b = jnp.sqrt(1.0 - jnp.exp(2.0 * log_a)) * (i * xc)
    return a, b


def _lru_conv(x0, x1, x2, x3, w0, w1, w2, w3, cb):
    return cb + x0 * w0 + x1 * w1 + x2 * w2 + x3 * w3


def _s5_params(lre, lim, ldt, gsel, bre, bim):
    dt = jnp.sum(gsel * jnp.exp(ldt), axis=1, keepdims=True)
    er = jnp.exp(lre * dt)
    ang = lim * dt
    lbr, lbi = er * jnp.cos(ang), er * jnp.sin(ang)
    nr, ni = lbr - 1.0, lbi
    den = lre * lre + lim * lim
    fr, fi = (nr * lre + ni * lim) / den, (ni * lre - nr * lim) / den
    return lbr, lbi, fr * bre - fi * bim, fr * bim + fi * bre


def _s5_out(yssm, u, d):
    return _gelu(yssm + d * u)


def _glu(z, gl, gb):
    return z * _sigmoid(gl + gb)


def _ple_out(x, gpre, epre, pn):
    return x + _sigmoid(gpre) * _rms(epre, pn)


def _vjp(fn, args, cots):
    _, pull = jax.vjp(fn, *args)
    return pull(cots)


def add_norm(x, y, g, name):
    D = x.shape[1]
    if y is None:
        return x, rowwise(lambda xv, gv: ([_rms(xv, gv)], []), [x], [g], outs=[(D, BF16)], name=name)[0]
    xn, n = rowwise(lambda xv, yv, gv: ([xv + yv, _rms(xv + yv, gv)], []), [x, y], [g],
                    outs=[(D, F32), (D, BF16)], name=name)
    return xn, n


def norm_bwd(x, g, dn, dx_res, name):
    D = x.shape[1]

    def f(xv, dnv, dxv, gv):
        dx, dg = _vjp(_rms, (xv, gv), dnv)
        return [dxv + dx], [dg]

    return rowwise(f, [x, dn, dx_res], [g], outs=[(D, F32)], accs=[(1, D)], name=name)


def ffn_fwd(n, wgu, wd, tag):
    gu = mm(n, wgu, name=f"ffn_gu_{tag}")
    F = wd.shape[0]
    act = rowwise(lambda g: ([_ffn_act(g)], []), [gu], outs=[(F, BF16)], name=f"ffn_act_{tag}")[0]
    y = mm(act, wd, name=f"ffn_down_{tag}")
    return y, (n, gu, act)


def ffn_bwd(dy, saved, wgu, wd, tag):
    n, gu, act = saved
    dact = mm(dy, wd, "nt", name=f"ffn_dact_{tag}")
    dwd = mm(act, dy, "tn", name=f"ffn_dwd_{tag}")
    dgu = rowwise(lambda g, d: (list(_vjp(_ffn_act, (g,), d)), []), [gu, dact],
                  outs=[(gu.shape[1], BF16)], name=f"ffn_dgu_{tag}")[0]
    dn = mm(dgu, wgu, "nt", name=f"ffn_dn_{tag}")
    dwgu = mm(n, dgu, "tn", name=f"ffn_dwgu_{tag}")
    return dn, dwgu, dwd


def _heads(x, H):
    S = x.shape[0]
    return x.reshape(S, H, HEAD_DIM).transpose(1, 0, 2)


def _unheads(x):
    H, S, _ = x.shape
    return x.transpose(1, 0, 2).reshape(S, H * HEAD_DIM)


def _shift_down(x, n=1):
    return jnp.pad(x, ((n, 0), (0, 0)))[:x.shape[0]]


def _shift_up(x, n=1):
    return jnp.pad(x, ((0, n), (0, 0)))[n:]


def _block_diag(w):
    B, I, J = w.shape
    eye = jnp.eye(B, dtype=w.dtype)
    return (w[:, :, None, :] * eye[:, None, :, None]).reshape(B * I, B * J)


def _block_diag_take(x, B):
    I, J = x.shape[0] // B, x.shape[1] // B
    eye = jnp.eye(B, dtype=x.dtype)
    return jnp.sum(x.reshape(B, I, B, J) * eye[:, None, :, None], axis=2)


def _rope_tables(S):
    half = HEAD_DIM // 2
    inv = jnp.power(ROPE_THETA, -jnp.arange(half, dtype=F32) / half)
    ang = jnp.arange(S, dtype=F32)[:, None] * inv[None, :]
    cos = jnp.concatenate([jnp.cos(ang), jnp.cos(ang)], axis=1)
    sin = jnp.concatenate([jnp.sin(ang), jnp.sin(ang)], axis=1)
    idx = jnp.arange(half)
    rot = jnp.zeros((HEAD_DIM, HEAD_DIM), F32).at[idx + half, idx].set(-1.0).at[idx, idx + half].set(1.0)
    return cos, sin, rot


def qk_prep_fwd(x_hm, g, rope_tabs, name):
    H, S, Dh = x_hm.shape
    rows = [x_hm.reshape(H * S, Dh)]
    consts = [g.reshape(1, Dh)]
    periods = [None]
    if rope_tabs is not None:
        rows += [rope_tabs[0], rope_tabs[1]]
        consts += [rope_tabs[2]]
        periods += [S, S]
    fn = _qk_prep(rope_tabs is not None)
    y = rowwise(lambda *a: ([fn(*a)], []), rows, consts, outs=[(Dh, F32)], name=name, periods=periods)[0]
    return y.reshape(H, S, Dh)


def qk_prep_bwd(x_hm, g, rope_tabs, dy_hm, name):
    H, S, Dh = x_hm.shape
    rope = rope_tabs is not None
    rows = [x_hm.reshape(H * S, Dh), dy_hm.reshape(H * S, Dh)]
    consts = [g.reshape(1, Dh)]
    periods = [None, None]
    if rope:
        rows += [rope_tabs[0], rope_tabs[1]]
        consts += [rope_tabs[2]]
        periods += [S, S]
    fn = _qk_prep(rope)

    def f(xv, dyv, *rest):
        if rope:
            cos, sin, gv, rot = rest
            dx, dg = _vjp(lambda a, b: fn(a, cos, sin, b, rot), (xv, gv), dyv)
        else:
            (gv,) = rest
            dx, dg = _vjp(fn, (xv, gv), dyv)
        return [dx], [dg]

    dx, dg = rowwise(f, rows, consts, outs=[(Dh, F32)], accs=[(1, Dh)], name=name, periods=periods)
    return dx.reshape(H, S, Dh), dg.reshape(Dh)


def attn_delta(do_hm, o_hm, name):
    H, S, Dh = o_hm.shape
    d = rowwise(lambda a, b: ([jnp.sum(a * b, axis=-1, keepdims=True)], []),
                [do_hm.reshape(H * S, Dh), o_hm.reshape(H * S, Dh)], outs=[(1, F32)], name=name)[0]
    return d.reshape(H, S, 1)


def even_mixer_fwd(h, w, tag):
    S = h.shape[0]
    W = 512
    H = 8
    z = mm(h, w["w_in"], name=f"ev_in_{tag}")
    xa, ya, q, k, v, f = (z[:, 0:512], z[:, 512:1024], z[:, 1024:1536], z[:, 1536:2048], z[:, 2048:2560],
                          z[:, 2560:2688])
    xs = [_shift_down(xa, LRU_CONV - 1 - tap) for tap in range(LRU_CONV)]
    taps = [w["conv_w"][tap][None] for tap in range(LRU_CONV)]
    xc = rowwise(lambda *a: ([_lru_conv(*a)], []), xs, taps + [w["conv_b"]], outs=[(W, F32)],
                 name=f"lru_conv_{tag}")[0]
    pre = mm(xc, w["w_ax"], name=f"lru_gates_mm_{tag}")
    a, b = rowwise(lambda p_, x_, ba, bx, lam: (list(_lru_gates(p_, x_, ba, bx, lam)), []), [pre, xc],
                   [w["ba"], w["bx"], w["lam"]], outs=[(W, F32), (W, F32)], name=f"lru_gates_{tag}")
    hs = scan_real(a, b, name=f"lru_scan_{tag}")
    a_out = rowwise(lambda y_, h_: ([_gelu(y_) * h_], []), [ya, hs], outs=[(W, F32)], name=f"lru_out_{tag}")[0]
    lf = rowwise(lambda f_, bf: ([_log_sigmoid(f_ + bf)], []), [f], [w["bf"]], outs=[(LANES, F32)],
                 name=f"fox_logf_{tag}")[0]
    c = scan_real(jnp.ones_like(lf), lf, name=f"fox_cumsum_{tag}")
    c_hm = c[:, :H].T
    q_hm, k_hm, v_hm = _heads(q, H), _heads(k, H), _heads(v, H)
    qn = qk_prep_fwd(q_hm, w["qn"], None, f"fox_qprep_{tag}")
    kn = qk_prep_fwd(k_hm, w["kn"], None, f"fox_kprep_{tag}")
    sink = jnp.full((H, 1, 1), NEG, F32)
    o_hm, lse = attn_fwd(qn, kn, v_hm, sink, c_hm[:, :, None], c_hm[:, None, :], name=f"fox_attn_{tag}")
    mo = jnp.concatenate([a_out, _unheads(o_hm)], axis=1)
    y = mm(mo, w["w_out"], name=f"ev_out_{tag}")
    saved = dict(h=h, xs=xs, xc=xc, pre=pre, a=a, hs=hs, ya=ya, f=f, c_hm=c_hm, q_hm=q_hm, k_hm=k_hm,
                 v_hm=v_hm, qn=qn, kn=kn, o_hm=o_hm, lse=lse, mo=mo)
    return y, saved


def even_mixer_bwd(dy, sv, w, tag):
    W = 512
    H = 8
    S = dy.shape[0]
    g = {}
    dmo = mm(dy, w["w_out"], "nt", name=f"ev_dmo_{tag}")
    g["w_out"] = mm(sv["mo"], dy, "tn", name=f"ev_dwout_{tag}")
    da_out, do = dmo[:, :W], dmo[:, W:]
    do_hm = _heads(do, H)
    delta = attn_delta(do_hm, sv["o_hm"], f"fox_delta_{tag}")
    c_hm = sv["c_hm"]
    dqn, dkn, dv_hm, dcq, dck = attn_bwd(sv["qn"], sv["kn"], sv["v_hm"], sv["lse"], do_hm, delta,
                                          c_hm[:, :, None], c_hm[:, None, :], name=f"fox_attn_bwd_{tag}")
    dq_hm, g["qn"] = qk_prep_bwd(sv["q_hm"], w["qn"], None, dqn, f"fox_qprep_bwd_{tag}")
    dk_hm, g["kn"] = qk_prep_bwd(sv["k_hm"], w["kn"], None, dkn, f"fox_kprep_bwd_{tag}")
    dc = (dcq[:, :, 0] + dck[:, 0, :]).T
    dc = jnp.pad(dc, ((0, 0), (0, LANES - H)))
    dlf = scan_real(jnp.ones_like(dc), dc, reverse=True, name=f"fox_cumsum_bwd_{tag}")

    def f_logf(f_, d_, bf):
        df, dbf = _vjp(lambda a_, b_: _log_sigmoid(a_ + b_), (f_, bf), d_)
        return [df], [dbf]

    df, dbf = rowwise(f_logf, [sv["f"], dlf], [w["bf"]], outs=[(LANES, F32)], accs=[(1, LANES)],
                      name=f"fox_logf_bwd_{tag}")
    g["bf"] = dbf[0, :H]
    def f_out(y_, h_, d_):
        dyv, dhv = _vjp(lambda a_, b_: _gelu(a_) * b_, (y_, h_), d_)
        return [dyv, dhv], []

    dya, dhs = rowwise(f_out, [sv["ya"], sv["hs"], da_out], outs=[(W, F32), (W, F32)], name=f"lru_out_bwd_{tag}")
    gs = scan_real(_shift_up(sv["a"]), dhs, reverse=True, name=f"lru_scan_bwd_{tag}")

    def f_gates(p_, x_, g_, hp_, ba, bx, lam):
        dp, dx, dba, dbx, dlam = _vjp(_lru_gates, (p_, x_, ba, bx, lam), (g_ * hp_, g_))
        return [dp, dx], [dba, dbx, dlam]

    dpre, dxc, dba, dbx, dlam = rowwise(f_gates, [sv["pre"], sv["xc"], gs, _shift_down(sv["hs"])],
                                        [w["ba"], w["bx"], w["lam"]], outs=[(2 * W, BF16), (W, F32)],
                                        accs=[(1, W)] * 3, name=f"lru_gates_bwd_{tag}")
    g["ba"], g["bx"], g["lam"] = dba[0], dbx[0], dlam[0]
    dxc2 = mm(dpre, w["w_ax"], "nt", name=f"lru_gates_mm_dx_{tag}")
    g["w_ax"] = mm(sv["xc"], dpre, "tn", name=f"lru_gates_mm_dw_{tag}")

    def f_conv(d1, d2, x0, x1, x2, x3):
        d = d1 + d2
        return [d], [jnp.sum(d, axis=0, keepdims=True)] + [jnp.sum(d * xv, axis=0, keepdims=True)
                                                           for xv in (x0, x1, x2, x3)]

    dxc_t, dcb, dw0, dw1, dw2, dw3 = rowwise(f_conv, [dxc, dxc2] + sv["xs"], outs=[(W, F32)],
                                             accs=[(1, W)] * 5, name=f"lru_conv_bwd_{tag}")
    g["conv_b"] = dcb[0]
    g["conv_w"] = jnp.concatenate([dw0, dw1, dw2, dw3], axis=0)
    ds_ = [_shift_up(dxc_t, LRU_CONV - 1 - tap) for tap in range(LRU_CONV)]
    taps = [w["conv_w"][tap][None] for tap in range(LRU_CONV)]
    dxa = rowwise(lambda a, b, c, d, w0, w1, w2, w3: ([a * w0 + b * w1 + c * w2 + d * w3], []), ds_, taps,
                  outs=[(W, F32)], name=f"lru_conv_dx_{tag}")[0]
    dz = jnp.concatenate([dxa, dya, _unheads(dq_hm), _unheads(dk_hm), _unheads(dv_hm), df], axis=1)
    g["w_in"] = mm(sv["h"], dz, "tn", name=f"ev_dwin_{tag}")
    dh = mm(dz, w["w_in"], "nt", name=f"ev_dh_{tag}")
    return dh, g


def odd_mixer_fwd(h, w, tag):
    S = h.shape[0]
    H, KVH = 8, 2
    z = mm(h, w["w_in"], name=f"od_in_{tag}")
    q, k, v, u = z[:, 0:512], z[:, 512:640], z[:, 640:768], z[:, 768:1280]
    tabs = _rope_tables(S)
    q_hm, k_hm, v_hm = _heads(q, H), _heads(k, KVH), _heads(v, KVH)
    qn = qk_prep_fwd(q_hm, w["qn"], tabs, f"swa_qprep_{tag}")
    kn = qk_prep_fwd(k_hm, w["kn"], tabs, f"swa_kprep_{tag}")
    sink = w["sinks"].reshape(H, 1, 1)
    o_hm, lse = attn_fwd(qn, kn, v_hm, sink, window=SWA_WINDOW, name=f"swa_attn_{tag}")
    lam, bexp = w["s5_lam"], w["s5_bexp"]
    bu = mm(u, bexp, name=f"s5_bu_{tag}")
    hs = scan_cplx(lam, bu, name=f"s5_scan_{tag}")
    yssm = mm(hs, w["s5_cexp"], name=f"s5_y_{tag}")
    zz = rowwise(lambda y_, u_, d_: ([_s5_out(y_, u_, d_)], []), [yssm, u], [w["s5_d"]], outs=[(512, F32)],
                 name=f"s5_gelu_{tag}")[0]
    gl = mm(zz, w["glu_w"], name=f"s5_glu_mm_{tag}")
    d_out = rowwise(lambda z_, g_, b_: ([_glu(z_, g_, b_)], []), [zz, gl], [w["glu_b"]], outs=[(512, F32)],
                    name=f"s5_glu_{tag}")[0]
    mo = jnp.concatenate([_unheads(o_hm), d_out], axis=1)
    y = mm(mo, w["w_out"], name=f"od_out_{tag}")
    saved = dict(h=h, q_hm=q_hm, k_hm=k_hm, v_hm=v_hm, qn=qn, kn=kn, o_hm=o_hm, lse=lse, u=u, hs=hs, yssm=yssm,
                 zz=zz, gl=gl, mo=mo, tabs=tabs)
    return y, saved


def odd_mixer_bwd(dy, sv, w, tag):
    H, KVH = 8, 2
    g = {}
    dmo = mm(dy, w["w_out"], "nt", name=f"od_dmo_{tag}")
    g["w_out"] = mm(sv["mo"], dy, "tn", name=f"od_dwout_{tag}")
    do, dd = dmo[:, :512], dmo[:, 512:]
    do_hm = _heads(do, H)
    delta = attn_delta(do_hm, sv["o_hm"], f"swa_delta_{tag}")
    dqn, dkn, dv_hm = attn_bwd(sv["qn"], sv["kn"], sv["v_hm"], sv["lse"], do_hm, delta, window=SWA_WINDOW,
                               name=f"swa_attn_bwd_{tag}")
    dq_hm, g["qn"] = qk_prep_bwd(sv["q_hm"], w["qn"], sv["tabs"], dqn, f"swa_qprep_bwd_{tag}")
    dk_hm, g["kn"] = qk_prep_bwd(sv["k_hm"], w["kn"], sv["tabs"], dkn, f"swa_kprep_bwd_{tag}")
    lse_t, delta_t = sv["lse"][:, :, 0].T, delta[:, :, 0].T
    g["sinks"] = rowwise(lambda l_, d_, s_: ([], [jnp.sum(-jnp.exp(s_ - l_) * d_, axis=0, keepdims=True)]),
                         [lse_t, delta_t], [w["sinks"].reshape(1, H)], accs=[(1, H)], name=f"swa_dsink_{tag}")[0][0]
    def f_glu(z_, g_, d_, b_):
        dz_, dg_, db_ = _vjp(_glu, (z_, g_, b_), d_)
        return [dz_, dg_], [db_]

    dzz1, dgl, dglb = rowwise(f_glu, [sv["zz"], sv["gl"], dd], [w["glu_b"]], outs=[(512, F32), (512, BF16)],
                              accs=[(1, 512)], name=f"s5_glu_bwd_{tag}")
    g["glu_b"] = dglb[0]
    g["glu_w"] = mm(sv["zz"], dgl, "tn", name=f"s5_glu_dw_{tag}")
    dzz2 = mm(dgl, w["glu_w"], "nt", name=f"s5_glu_dz_{tag}")

    def f_gelu(y_, u_, d1, d2, dpar):
        dy_, du_, dd_ = _vjp(_s5_out, (y_, u_, dpar), d1 + d2)
        return [dy_, du_], [dd_]

    dyssm, du1, dsd = rowwise(f_gelu, [sv["yssm"], sv["u"], dzz1, dzz2], [w["s5_d"]],
                              outs=[(512, F32), (512, F32)], accs=[(1, 512)], name=f"s5_gelu_bwd_{tag}")
    g["s5_d"] = dsd[0]
    dhs = mm(dyssm, w["s5_cexp"], "nt", name=f"s5_dh_{tag}")
    g["s5_cexp"] = mm(sv["hs"], dyssm, "tn", name=f"s5_dc_{tag}")
    gs = scan_cplx(w["s5_lam"], dhs, reverse=True, name=f"s5_scan_bwd_{tag}")
    g["s5_bexp"] = mm(sv["u"], gs, "tn", name=f"s5_db_{tag}")
    du2 = mm(gs, w["s5_bexp"], "nt", name=f"s5_du_{tag}")

    def f_dlam(g_, hp_):
        C = g_.shape[1]
        outs_r, outs_i = [], []
        for j in range(C // (2 * LANES)):
            gr, gi = g_[:, 2 * LANES * j:2 * LANES * j + LANES], g_[:, 2 * LANES * j + LANES:2 * LANES * (j + 1)]
            hr, hi = hp_[:, 2 * LANES * j:2 * LANES * j + LANES], hp_[:, 2 * LANES * j + LANES:2 * LANES * (j + 1)]
            outs_r.append(jnp.sum(gr * hr + gi * hi, axis=0, keepdims=True))
            outs_i.append(jnp.sum(gi * hr - gr * hi, axis=0, keepdims=True))
        return [], [jnp.concatenate([x for pair in zip(outs_r, outs_i) for x in pair], axis=1)]

    g["s5_lam"] = rowwise(f_dlam, [gs, _shift_down(sv["hs"])], accs=[(1, gs.shape[1])], name=f"s5_dlam_{tag}")[0]
    du = rowwise(lambda a_, b_: ([a_ + b_], []), [du1, du2], outs=[(512, F32)], name=f"s5_du_add_{tag}")[0]
    dz = jnp.concatenate([_unheads(dq_hm), _unheads(dk_hm), _unheads(dv_hm), du], axis=1)
    g["w_in"] = mm(sv["h"], dz, "tn", name=f"od_dwin_{tag}")
    dh = mm(dz, w["w_in"], "nt", name=f"od_dh_{tag}")
    return dh, g


def _s5_cols(x_re, x_im):
    n = x_re.shape[0] // LANES
    return jnp.stack([x_re.reshape(n, LANES), x_im.reshape(n, LANES)], axis=1).reshape(1, 2 * n * LANES)


def _s5_uncols(x):
    n = x.shape[1] // (2 * LANES)
    y = x.reshape(n, 2, LANES)
    return y[:, 0].reshape(-1), y[:, 1].reshape(-1)


def _s5_gsel():
    return jnp.repeat(jnp.eye(S5_GROUPS, dtype=F32), S5_STATE, axis=0)


def s5_prep_fwd(lre, lim, ldt, bre, bim, cre, cim, tag):
    GP = S5_GROUPS * S5_STATE
    ins = [lre.reshape(GP, 1), lim.reshape(GP, 1), ldt.reshape(1, S5_GROUPS), _s5_gsel(),
           bre.reshape(GP, S5_GROUP), bim.reshape(GP, S5_GROUP)]
    lbr, lbi, bbr, bbi = whole(_s5_params, ins, [((GP, 1), F32)] * 2 + [((GP, S5_GROUP), F32)] * 2,
                               name=f"s5_params_{tag}")
    lam = _s5_cols(lbr[:, 0], lbi[:, 0])

    def expand_b(bb):
        return _block_diag(bb.reshape(S5_GROUPS, S5_STATE, S5_GROUP).transpose(0, 2, 1))

    n = GP // LANES
    bexp = jnp.stack([expand_b(bbr).reshape(-1, n, LANES), expand_b(bbi).reshape(-1, n, LANES)],
                     axis=2).reshape(-1, 2 * GP)
    c_r = _block_diag(cre.transpose(0, 2, 1))
    c_i = _block_diag(cim.transpose(0, 2, 1))
    cexp = jnp.stack([c_r.reshape(n, LANES, -1), -c_i.reshape(n, LANES, -1)], axis=1).reshape(2 * GP, -1)
    return lam, bexp, cexp, ins


def s5_prep_bwd(ins, dlam, dbexp, dcexp, tag):
    GP = S5_GROUPS * S5_STATE
    n = GP // LANES
    dlr, dli = _s5_uncols(dlam)
    db = dbexp.reshape(-1, n, 2, LANES)

    def take_b(x):
        return _block_diag_take(x, S5_GROUPS).transpose(0, 2, 1).reshape(GP, S5_GROUP)

    dbbr, dbbi = take_b(db[:, :, 0].reshape(-1, GP)), take_b(db[:, :, 1].reshape(-1, GP))
    dc = dcexp.reshape(n, 2, LANES, -1)
    dcre = _block_diag_take(dc[:, 0].reshape(GP, -1), S5_GROUPS).transpose(0, 2, 1)
    dcim = -_block_diag_take(dc[:, 1].reshape(GP, -1), S5_GROUPS).transpose(0, 2, 1)

    def f(lre, lim, ldt, gsel, bre, bim, c1, c2, c3, c4):
        d = _vjp(lambda a, b, c, e, f_: _s5_params(a, b, c, gsel, e, f_), (lre, lim, ldt, bre, bim), (c1, c2, c3, c4))
        return d

    outs = [((GP, 1), F32)] * 2 + [((1, S5_GROUPS), F32)] + [((GP, S5_GROUP), F32)] * 2
    dlre, dlim, dldt, dbre, dbim = whole(f, ins + [dlr.reshape(GP, 1), dli.reshape(GP, 1), dbbr, dbbi], outs,
                                          name=f"s5_params_bwd_{tag}")
    shp = (S5_GROUPS, S5_STATE)
    return dict(lre=dlre.reshape(shp), lim=dlim.reshape(shp), ldt=dldt.reshape(S5_GROUPS),
                bre=dbre.reshape(S5_GROUPS, S5_STATE, S5_GROUP), bim=dbim.reshape(S5_GROUPS, S5_STATE, S5_GROUP),
                cre=dcre, cim=dcim)


def _place():
    return lax.axis_index("x"), lax.axis_index("y"), lax.axis_index("c")


def _other_chips(x, y):
    return [(1 - x, y), (x, 1 - y), (1 - x, 1 - y)]


def gather_chips(w):
    def body(w_ref, out_ref, send_sems, recv_sems, local_sem):
        x, y, c = _place()
        me, sibling = (x, y, c), (x, y, 1 - c)
        chips = _other_chips(x, y)
        mine = 2 * x + y

        def copy(k, src, dst, to):
            return pltpu.make_async_remote_copy(src_ref=src, dst_ref=dst, send_sem=send_sems.at[k],
                                                recv_sem=recv_sems.at[k], device_id=to, device_id_type=MESH)

        local = pltpu.make_async_copy(w_ref, out_ref.at[mine], local_sem)
        local.start()
        first = [copy(j, w_ref.at[c], out_ref.at[mine, c], (*chip, c)) for j, chip in enumerate(chips)]
        for cp in first:
            cp.start()
        passed = [copy(3 + j, out_ref.at[2 * chip[0] + chip[1], c], out_ref.at[2 * chip[0] + chip[1], c], sibling)
                  for j, chip in enumerate(chips)]
        for j, chip in enumerate(chips):
            copy(j, w_ref.at[c], out_ref.at[2 * chip[0] + chip[1], c], me).wait_recv()
            passed[j].start()
        for j, chip in enumerate(chips):
            copy(3 + j, w_ref.at[c], out_ref.at[2 * chip[0] + chip[1], 1 - c], me).wait_recv()
        for cp in first + passed:
            cp.wait_send()
        local.wait()

    return pl.pallas_call(
        body, out_shape=jax.ShapeDtypeStruct((N_CHIPS,) + w.shape, w.dtype),
        in_specs=[pl.BlockSpec(memory_space=pl.ANY)], out_specs=pl.BlockSpec(memory_space=pl.ANY),
        scratch_shapes=[pltpu.SemaphoreType.DMA((6,)), pltpu.SemaphoreType.DMA((6,)), pltpu.SemaphoreType.DMA],
        name="gather_chips",
    )(w)


def sibling_halves(g):
    def body(g_ref, out_ref, send_sems, recv_sems):
        x, y, c = _place()
        me, sibling = (x, y, c), (x, y, 1 - c)

        def copy(k, to):
            return pltpu.make_async_remote_copy(src_ref=g_ref.at[k, 1 - c], dst_ref=out_ref.at[k],
                                                send_sem=send_sems.at[k], recv_sem=recv_sems.at[k],
                                                device_id=to, device_id_type=MESH)

        cps = [copy(k, sibling) for k in range(N_CHIPS)]
        for cp in cps:
            cp.start()
        for k in range(N_CHIPS):
            copy(k, me).wait_recv()
        for cp in cps:
            cp.wait_send()

    return pl.pallas_call(
        body, out_shape=jax.ShapeDtypeStruct((N_CHIPS,) + g.shape[2:], g.dtype),
        in_specs=[pl.BlockSpec(memory_space=pl.ANY)], out_specs=pl.BlockSpec(memory_space=pl.ANY),
        scratch_shapes=[pltpu.SemaphoreType.DMA((N_CHIPS,)), pltpu.SemaphoreType.DMA((N_CHIPS,))],
        name="sibling_halves",
    )(g)


def exchange_chips(p):
    def body(p_ref, out_ref, send_sems, recv_sems):
        x, y, c = _place()
        me = (x, y, c)
        chips = _other_chips(x, y)

        def copy(j, chip, to):
            return pltpu.make_async_remote_copy(src_ref=p_ref.at[2 * chip[0] + chip[1]], dst_ref=out_ref.at[j],
                                                send_sem=send_sems.at[j], recv_sem=recv_sems.at[j],
                                                device_id=to, device_id_type=MESH)

        cps = [copy(j, chip, (*chip, c)) for j, chip in enumerate(chips)]
        for cp in cps:
            cp.start()
        for j, chip in enumerate(chips):
            copy(j, chip, me).wait_recv()
        for cp in cps:
            cp.wait_send()

    return pl.pallas_call(
        body, out_shape=jax.ShapeDtypeStruct((3,) + p.shape[1:], p.dtype),
        in_specs=[pl.BlockSpec(memory_space=pl.ANY)], out_specs=pl.BlockSpec(memory_space=pl.ANY),
        scratch_shapes=[pltpu.SemaphoreType.DMA((3,)), pltpu.SemaphoreType.DMA((3,))],
        name="exchange_chips",
    )(p)


def sibling_join(r):
    def body(r_ref, out_ref, send_sem, recv_sem, local_sem):
        x, y, c = _place()
        local = pltpu.make_async_copy(r_ref, out_ref.at[c], local_sem)
        local.start()
        cp = pltpu.make_async_remote_copy(src_ref=r_ref, dst_ref=out_ref.at[c], send_sem=send_sem,
                                          recv_sem=recv_sem, device_id=(x, y, 1 - c), device_id_type=MESH)
        cp.start()
        pltpu.make_async_remote_copy(src_ref=r_ref, dst_ref=out_ref.at[1 - c], send_sem=send_sem,
                                     recv_sem=recv_sem, device_id=(x, y, c), device_id_type=MESH).wait_recv()
        cp.wait_send()
        local.wait()

    return pl.pallas_call(
        body, out_shape=jax.ShapeDtypeStruct((2,) + r.shape, r.dtype),
        in_specs=[pl.BlockSpec(memory_space=pl.ANY)], out_specs=pl.BlockSpec(memory_space=pl.ANY),
        scratch_shapes=[pltpu.SemaphoreType.DMA, pltpu.SemaphoreType.DMA, pltpu.SemaphoreType.DMA],
        name="sibling_join",
    )(r)


def gather_devices(v, name):
    R = v.shape[0]

    def body(v_ref, out_ref, send_sems, recv_sems, local_sem):
        x, y, c = _place()
        me, sibling = (x, y, c), (x, y, 1 - c)
        chips = _other_chips(x, y)

        def rows(px, py, pc):
            return out_ref.at[pl.ds((4 * px + 2 * py + pc) * R, R), :]

        def copy(k, block, to, src=None):
            return pltpu.make_async_remote_copy(src_ref=rows(*block) if src is None else src, dst_ref=rows(*block),
                                                send_sem=send_sems.at[k], recv_sem=recv_sems.at[k],
                                                device_id=to, device_id_type=MESH)

        mine = pltpu.make_async_copy(v_ref, rows(*me), local_sem)
        mine.start()
        first = [copy(0, me, sibling, src=v_ref)]
        first += [copy(1 + j, me, (*chip, c), src=v_ref) for j, chip in enumerate(chips)]
        for cp in first:
            cp.start()
        passed = [copy(4 + j, (*chip, c), sibling) for j, chip in enumerate(chips)]
        for j, chip in enumerate(chips):
            copy(1 + j, (*chip, c), me).wait_recv()
            passed[j].start()
        copy(0, sibling, me).wait_recv()
        for j, chip in enumerate(chips):
            copy(4 + j, (*chip, 1 - c), me).wait_recv()
        for cp in first + passed:
            cp.wait_send()
        mine.wait()

    return pl.pallas_call(
        body, out_shape=jax.ShapeDtypeStruct((N_DEV * R, LANES), v.dtype),
        in_specs=[pl.BlockSpec(memory_space=pltpu.VMEM)], out_specs=pl.BlockSpec(memory_space=pltpu.VMEM),
        scratch_shapes=[pltpu.SemaphoreType.DMA((7,)), pltpu.SemaphoreType.DMA((7,)), pltpu.SemaphoreType.DMA],
        name=name, compiler_params=_params(),
    )(v)


def _flat_rows(n, mult):
    return -(-n // (LANES * mult)) * mult


def _adam(w, g, m, v):
    m = ADAM_B1 * m + (1.0 - ADAM_B1) * g
    v = ADAM_B2 * v + (1.0 - ADAM_B2) * (g * g)
    m_hat = m / (1.0 - ADAM_B1 ** ADAM_STEP)
    v_hat = v / (1.0 - ADAM_B2 ** ADAM_STEP)
    return -ADAM_LR * (m_hat / (jnp.sqrt(v_hat) + ADAM_EPS) + ADAM_WD * w), m, v


def adam_2d(w, g, m, v, name):
    shape = w.shape
    F = shape[-1]
    a = [t.reshape(-1, F) for t in (w, g, m, v)]
    d, m2, v2 = rowwise(lambda w_, g_, m_, v_: (list(_adam(w_, g_, m_, v_)), []), a, outs=[(F, F32)] * 3, name=name)
    return d.reshape(shape), m2.reshape(shape), v2.reshape(shape)


WEIGHTS = ['ffn1_norm', 'ffn1_wg', 'ffn1_wu', 'ffn1_wd', 'mix_norm', 'ffn2_norm', 'ffn2_wg', 'ffn2_wu', 'ffn2_wd',
           'ple_w', 'ple_norm', 'ple_gate_norm', 'ple_gate_w', 'ev_w_in', 'lru_conv_w', 'lru_conv_b', 'lru_wa',
           'lru_ba', 'lru_wx', 'lru_bx', 'lru_lambda', 'fox_bf', 'fox_q_norm', 'fox_k_norm', 'ev_w_out', 'od_w_in',
           'swa_q_norm', 'swa_k_norm', 'swa_sinks', 's5_lambda_re', 's5_lambda_im', 's5_log_dt', 's5_b_re',
           's5_b_im', 's5_c_re', 's5_c_im', 's5_d', 's5_glu_w', 's5_glu_b', 'od_w_out']
SHARD_AXIS = {'ffn1_wg': 2, 'ffn1_wu': 2, 'ffn1_wd': 1, 'ffn2_wg': 2, 'ffn2_wu': 2, 'ffn2_wd': 1, 'ple_w': 2,
              'ple_gate_w': 1, 'ev_w_in': 2, 'lru_conv_w': 2, 'ev_w_out': 1, 'od_w_in': 2, 's5_d': 1,
              's5_glu_w': 1, 's5_glu_b': 1, 'od_w_out': 1}
EXACT_SHARDED = ('lru_conv_w', 's5_d', 's5_glu_b')
SHARDED = [n for n in WEIGHTS if n in SHARD_AXIS]
REPLICATED = [n for n in WEIGHTS if n not in SHARD_AXIS]


def gather_weights(shards):
    def unpack(allw, names):
        full, off = {}, 0
        for n in names:
            shp = shards[n].shape
            size = math.prod(shp)
            parts = allw[:, off:off + size].reshape((N_CHIPS,) + shp)
            full[n] = jnp.concatenate([parts[i] for i in range(N_CHIPS)], axis=SHARD_AXIS[n])
            off += size
        return full

    names = [n for n in SHARDED if n not in EXACT_SHARDED]
    flat = jnp.concatenate([shards[n].astype(BF16).reshape(-1) for n in names])
    rows = _flat_rows(flat.shape[0], 2 * 16)
    flat = jnp.pad(flat, (0, rows * LANES - flat.shape[0])).reshape(2, rows // 2, LANES)
    full = unpack(gather_chips(flat).reshape(N_CHIPS, rows * LANES), names)
    flat = jnp.concatenate([shards[n].astype(F32).reshape(-1) for n in EXACT_SHARDED])
    rows = _flat_rows(flat.shape[0], SUBLANES)
    flat = jnp.pad(flat, (0, rows * LANES - flat.shape[0])).reshape(rows, LANES)
    every = gather_devices(flat, "gather_exact_weights").reshape(N_CHIPS, 2, rows * LANES)
    full.update(unpack(every[:, 0], EXACT_SHARDED))
    return full


def reduce_sharded(grads, shapes):
    cols = []
    for n in SHARDED:
        parts = jnp.split(grads[n].astype(F32), N_CHIPS, axis=SHARD_AXIS[n])
        cols.append(jnp.stack([p_.reshape(-1) for p_ in parts]))
    flat = jnp.concatenate(cols, axis=1)
    L = flat.shape[1]
    rows = _flat_rows(L, 2 * 16)
    flat = jnp.pad(flat, ((0, 0), (0, rows * LANES - L))).reshape(N_CHIPS, 2, rows // 2, LANES)
    c = lax.axis_index("c")
    chip = 2 * lax.axis_index("x") + lax.axis_index("y")
    theirs = sibling_halves(flat)
    mine = lax.dynamic_index_in_dim(flat, c, axis=1, keepdims=False)
    R = rows // 2
    pair = rowwise(lambda a, b: ([a + b], []), [mine.reshape(N_CHIPS * R, LANES), theirs.reshape(N_CHIPS * R, LANES)],
                   outs=[(LANES, F32)], name="reduce_pair")[0].reshape(N_CHIPS, R, LANES)
    got = exchange_chips(pair.astype(BF16))
    own = lax.dynamic_index_in_dim(pair, chip, axis=0, keepdims=False)
    half = rowwise(lambda o, a, b, d: ([o + a.astype(F32) + b.astype(F32) + d.astype(F32)], []),
                   [own, got[0], got[1], got[2]], outs=[(LANES, F32)], name="reduce_chips")[0]
    red = sibling_join(half).reshape(-1)
    out, off = {}, 0
    for n in SHARDED:
        size = math.prod(shapes[n])
        out[n] = red[off:off + size].reshape(shapes[n])
        off += size
    return out


def reduce_replicated(grads):
    flat = jnp.concatenate([grads[n].astype(F32).reshape(-1) for n in REPLICATED])
    rows = _flat_rows(flat.shape[0], SUBLANES)
    flat = jnp.pad(flat, (0, rows * LANES - flat.shape[0])).reshape(rows, LANES)
    return gather_devices(flat, "gather_small_grads").reshape(N_DEV, rows, LANES)


def _flatten_small(tensors):
    flat = jnp.concatenate([tensors[n].astype(F32).reshape(-1) for n in REPLICATED])
    rows = _flat_rows(flat.shape[0], SUBLANES)
    return jnp.pad(flat, (0, rows * LANES - flat.shape[0])).reshape(rows, LANES)


def _unflatten_small(flat, shapes):
    flat = flat.reshape(-1)
    out, off = {}, 0
    for n in REPLICATED:
        size = math.prod(shapes[n])
        out[n] = flat[off:off + size].reshape(shapes[n])
        off += size
    return out


def _layer_weights(full, small, i):
    j = i // 2
    w = dict(
        g1=small['ffn1_norm'][i][None], gm=small['mix_norm'][i][None], g2=small['ffn2_norm'][i][None],
        gp=small['ple_norm'][i][None], gg=small['ple_gate_norm'][i][None],
        wgu1=jnp.concatenate([full['ffn1_wg'][i], full['ffn1_wu'][i]], axis=1), wd1=full['ffn1_wd'][i],
        wgu2=jnp.concatenate([full['ffn2_wg'][i], full['ffn2_wu'][i]], axis=1), wd2=full['ffn2_wd'][i],
        ple_w=full['ple_w'][i], ple_gate_w=full['ple_gate_w'][i],
    )
    if i % 2 == 0:
        w_in = full['ev_w_in'][j]
        w['mix'] = dict(
            w_in=jnp.pad(w_in, ((0, 0), (0, 2688 - w_in.shape[1]))), w_out=full['ev_w_out'][j],
            conv_w=full['lru_conv_w'][j].astype(F32), conv_b=small['lru_conv_b'][j][None],
            w_ax=jnp.concatenate([_block_diag(small['lru_wa'][j]), _block_diag(small['lru_wx'][j])], axis=1),
            ba=small['lru_ba'][j][None], bx=small['lru_bx'][j][None], lam=small['lru_lambda'][j][None],
            bf=jnp.pad(small['fox_bf'][j], (0, LANES - 8))[None], qn=small['fox_q_norm'][j],
            kn=small['fox_k_norm'][j])
    else:
        lam, bexp, cexp, ins = s5_prep_fwd(small['s5_lambda_re'][j], small['s5_lambda_im'][j], small['s5_log_dt'][j],
                                           small['s5_b_re'][j], small['s5_b_im'][j], small['s5_c_re'][j],
                                           small['s5_c_im'][j], f"L{i}")
        w['mix'] = dict(
            w_in=full['od_w_in'][j], w_out=full['od_w_out'][j], qn=small['swa_q_norm'][j], kn=small['swa_k_norm'][j],
            sinks=small['swa_sinks'][j], s5_lam=lam, s5_bexp=bexp, s5_cexp=cexp, s5_ins=ins,
            s5_d=full['s5_d'][j].astype(F32)[None], glu_w=full['s5_glu_w'][j], glu_b=full['s5_glu_b'][j].astype(F32)[None])
    return w


def layer_fwd(x, p_i, w, i):
    tag = f"L{i}"
    sv = {}
    x0, n1 = add_norm(x, None, w['g1'], f"norm1_{tag}")
    y1, sv['ffn1'] = ffn_fwd(n1, w['wgu1'], w['wd1'], f"1_{tag}")
    x1, hm = add_norm(x0, y1, w['gm'], f"normm_{tag}")
    if i % 2 == 0:
        ym, sv['mix'] = even_mixer_fwd(hm, w['mix'], tag)
    else:
        ym, sv['mix'] = odd_mixer_fwd(hm, w['mix'], tag)
    x2, n2 = add_norm(x1, ym, w['g2'], f"norm2_{tag}")
    y2, sv['ffn2'] = ffn_fwd(n2, w['wgu2'], w['wd2'], f"2_{tag}")
    x3, ng = add_norm(x2, y2, w['gg'], f"normg_{tag}")
    gpre = mm(ng, w['ple_gate_w'], name=f"ple_gate_{tag}")
    epre = mm(p_i, w['ple_w'], name=f"ple_emb_{tag}")
    D = x.shape[1]
    x4 = rowwise(lambda a, b, c, pn: ([_ple_out(a, b, c, pn)], []), [x3, gpre, epre], [w['gp']], outs=[(D, F32)],
                 name=f"ple_out_{tag}")[0]
    sv.update(x0=x0, x1=x1, x2=x2, x3=x3, ng=ng, gpre=gpre, epre=epre, p=p_i)
    return x4, sv


def layer_bwd(dx4, sv, w, i):
    tag = f"L{i}"
    D = dx4.shape[1]
    g = {}

    def f_ple(a, b, c, d, pn):
        da, db, dc, dpn = _vjp(_ple_out, (a, b, c, pn), d)
        return [db, dc], [dpn]

    dgpre, depre, dgp = rowwise(f_ple, [sv['x3'], sv['gpre'], sv['epre'], dx4], [w['gp']],
                                outs=[(D, BF16), (D, BF16)], accs=[(1, D)], name=f"ple_out_bwd_{tag}")
    g['gp'] = dgp[0]
    g['ple_w'] = mm(sv['p'], depre, "tn", name=f"ple_emb_dw_{tag}")
    g['ple_gate_w'] = mm(sv['ng'], dgpre, "tn", name=f"ple_gate_dw_{tag}")
    dng = mm(dgpre, w['ple_gate_w'], "nt", name=f"ple_gate_dx_{tag}")
    dx3, dgg = norm_bwd(sv['x3'], w['gg'], dng, dx4, f"normg_bwd_{tag}")
    g['gg'] = dgg[0]
    dn2, g['wgu2'], g['wd2'] = ffn_bwd(dx3, sv['ffn2'], w['wgu2'], w['wd2'], f"2_{tag}")
    dx2, dg2 = norm_bwd(sv['x2'], w['g2'], dn2, dx3, f"norm2_bwd_{tag}")
    g['g2'] = dg2[0]
    if i % 2 == 0:
        dhm, g['mix'] = even_mixer_bwd(dx2, sv['mix'], w['mix'], tag)
    else:
        dhm, g['mix'] = odd_mixer_bwd(dx2, sv['mix'], w['mix'], tag)
    dx1, dgm = norm_bwd(sv['x1'], w['gm'], dhm, dx2, f"normm_bwd_{tag}")
    g['gm'] = dgm[0]
    dn1, g['wgu1'], g['wd1'] = ffn_bwd(dx1, sv['ffn1'], w['wgu1'], w['wd1'], f"1_{tag}")
    dx0, dg1 = norm_bwd(sv['x0'], w['g1'], dn1, dx1, f"norm1_bwd_{tag}")
    g['g1'] = dg1[0]
    return dx0, g


def _collect_grads(layer_grads, depth):
    F = layer_grads[0]['wd1'].shape[0]
    st = lambda xs: jnp.stack(xs)
    G = {}
    L = layer_grads
    G['ffn1_norm'] = st([g['g1'] for g in L])
    G['mix_norm'] = st([g['gm'] for g in L])
    G['ffn2_norm'] = st([g['g2'] for g in L])
    G['ple_norm'] = st([g['gp'] for g in L])
    G['ple_gate_norm'] = st([g['gg'] for g in L])
    G['ffn1_wg'] = st([g['wgu1'][:, :F] for g in L])
    G['ffn1_wu'] = st([g['wgu1'][:, F:] for g in L])
    G['ffn1_wd'] = st([g['wd1'] for g in L])
    G['ffn2_wg'] = st([g['wgu2'][:, :F] for g in L])
    G['ffn2_wu'] = st([g['wgu2'][:, F:] for g in L])
    G['ffn2_wd'] = st([g['wd2'] for g in L])
    G['ple_w'] = st([g['ple_w'] for g in L])
    G['ple_gate_w'] = st([g['ple_gate_w'] for g in L])
    ev = [L[i]['mix'] for i in range(0, depth, 2)]
    od = [L[i]['mix'] for i in range(1, depth, 2)]
    G['ev_w_in'] = st([m['w_in'][:, :2568] for m in ev])
    G['ev_w_out'] = st([m['w_out'] for m in ev])
    G['lru_conv_w'] = st([m['conv_w'] for m in ev])
    G['lru_conv_b'] = st([m['conv_b'] for m in ev])
    G['lru_wa'] = st([_block_diag_take(m['w_ax'][:, :512], LRU_BLOCKS) for m in ev])
    G['lru_wx'] = st([_block_diag_take(m['w_ax'][:, 512:], LRU_BLOCKS) for m in ev])
    G['lru_ba'] = st([m['ba'] for m in ev])
    G['lru_bx'] = st([m['bx'] for m in ev])
    G['lru_lambda'] = st([m['lam'] for m in ev])
    G['fox_bf'] = st([m['bf'] for m in ev])
    G['fox_q_norm'] = st([m['qn'] for m in ev])
    G['fox_k_norm'] = st([m['kn'] for m in ev])
    G['od_w_in'] = st([m['w_in'] for m in od])
    G['od_w_out'] = st([m['w_out'] for m in od])
    G['swa_q_norm'] = st([m['qn'] for m in od])
    G['swa_k_norm'] = st([m['kn'] for m in od])
    G['swa_sinks'] = st([m['sinks'] for m in od])
    G['s5_lambda_re'] = st([m['s5']['lre'] for m in od])
    G['s5_lambda_im'] = st([m['s5']['lim'] for m in od])
    G['s5_log_dt'] = st([m['s5']['ldt'] for m in od])
    G['s5_b_re'] = st([m['s5']['bre'] for m in od])
    G['s5_b_im'] = st([m['s5']['bim'] for m in od])
    G['s5_c_re'] = st([m['s5']['cre'] for m in od])
    G['s5_c_im'] = st([m['s5']['cim'] for m in od])
    G['s5_d'] = st([m['s5_d'] for m in od])
    G['s5_glu_w'] = st([m['glu_w'] for m in od])
    G['s5_glu_b'] = st([m['glu_b'] for m in od])
    return G


def local_step(x, p, target, full, small):
    depth = p.shape[0]
    S, D = x.shape
    ws = [_layer_weights(full, small, i) for i in range(depth)]
    saved = []
    xi = x
    for i in range(depth):
        xi, sv = layer_fwd(xi, p[i], ws[i], i)
        saved.append(sv)

    def f_loss(y, t):
        e = y - t
        return [e * (1.0 / D)], [0.5 * jnp.sum(jnp.mean(e * e, axis=-1, keepdims=True), axis=0, keepdims=True)]

    dx, loss = rowwise(f_loss, [xi, target], outs=[(D, F32)], accs=[(1, 1)], name="loss")
    grads = [None] * depth
    for i in reversed(range(depth)):
        dx, grads[i] = layer_bwd(dx, saved[i], ws[i], i)
        if i % 2 == 1:
            m = grads[i]['mix']
            m['s5'] = s5_prep_bwd(ws[i]['mix']['s5_ins'], m['s5_lam'], m['s5_bexp'], m['s5_cexp'], f"L{i}")
    return loss[0, 0], dx, _collect_grads(grads, depth)


def kernel(x, p, ffn1_norm, ffn1_wg, ffn1_wu, ffn1_wd, mix_norm, ffn2_norm, ffn2_wg, ffn2_wu, ffn2_wd, ple_w, ple_norm, ple_gate_norm, ple_gate_w, ev_w_in, lru_conv_w, lru_conv_b, lru_wa, lru_ba, lru_wx, lru_bx, lru_lambda, fox_bf, fox_q_norm, fox_k_norm, ev_w_out, od_w_in, swa_q_norm, swa_k_norm, swa_sinks, s5_lambda_re, s5_lambda_im, s5_log_dt, s5_b_re, s5_b_im, s5_c_re, s5_c_im, s5_d, s5_glu_w, s5_glu_b, od_w_out, loss_target, m_ffn1_norm, m_ffn1_wg, m_ffn1_wu, m_ffn1_wd, m_mix_norm, m_ffn2_norm, m_ffn2_wg, m_ffn2_wu, m_ffn2_wd, m_ple_w, m_ple_norm, m_ple_gate_norm, m_ple_gate_w, m_ev_w_in, m_lru_conv_w, m_lru_conv_b, m_lru_wa, m_lru_ba, m_lru_wx, m_lru_bx, m_lru_lambda, m_fox_bf, m_fox_q_norm, m_fox_k_norm, m_ev_w_out, m_od_w_in, m_swa_q_norm, m_swa_k_norm, m_swa_sinks, m_s5_lambda_re, m_s5_lambda_im, m_s5_log_dt, m_s5_b_re, m_s5_b_im, m_s5_c_re, m_s5_c_im, m_s5_d, m_s5_glu_w, m_s5_glu_b, m_od_w_out, v_ffn1_norm, v_ffn1_wg, v_ffn1_wu, v_ffn1_wd, v_mix_norm, v_ffn2_norm, v_ffn2_wg, v_ffn2_wu, v_ffn2_wd, v_ple_w, v_ple_norm, v_ple_gate_norm, v_ple_gate_w, v_ev_w_in, v_lru_conv_w, v_lru_conv_b, v_lru_wa, v_lru_ba, v_lru_wx, v_lru_bx, v_lru_lambda, v_fox_bf, v_fox_q_norm, v_fox_k_norm, v_ev_w_out, v_od_w_in, v_swa_q_norm, v_swa_k_norm, v_swa_sinks, v_s5_lambda_re, v_s5_lambda_im, v_s5_log_dt, v_s5_b_re, v_s5_b_im, v_s5_c_re, v_s5_c_im, v_s5_d, v_s5_glu_w, v_s5_glu_b, v_od_w_out):
    args = locals()
    wts = {n: args[n] for n in WEIGHTS}
    ms = {n: args["m_" + n] for n in WEIGHTS}
    vs = {n: args["v_" + n] for n in WEIGHTS}
    shapes = {n: wts[n].shape for n in WEIGHTS}

    full = gather_weights(wts)
    small = {n: wts[n] for n in REPLICATED}
    loss, dx, G = local_step(x[0], p[:, 0], loss_target[0], full, small)
    loss = lax.psum(loss, ("x", "y", "c"))

    gsh = reduce_sharded(G, shapes)
    g8 = reduce_replicated(G)
    wf, mf, vf = _flatten_small(wts), _flatten_small(ms), _flatten_small(vs)

    def f_small(g0, g1, g2, g3, g4, g5, g6, g7, w_, m_, v_):
        gsum = ((g0 + g1) + (g2 + g3)) + ((g4 + g5) + (g6 + g7))
        return [gsum] + list(_adam(w_, gsum, m_, v_)), []

    gs_f, ds_f, ms_f, vs_f = rowwise(f_small, [g8[d] for d in range(N_DEV)] + [wf, mf, vf],
                                     outs=[(LANES, F32)] * 4, name="adam_small")
    out_g, out_d, out_m, out_v = {}, {}, {}, {}
    for dst, flat in ((out_g, gs_f), (out_d, ds_f), (out_m, ms_f), (out_v, vs_f)):
        dst.update(_unflatten_small(flat, shapes))
    for n in SHARDED:
        out_g[n] = gsh[n]
        out_d[n], out_m[n], out_v[n] = adam_2d(wts[n], gsh[n], ms[n], vs[n], f"adam_{n}")
    return (loss, dx[None], *[out_g[n] for n in WEIGHTS], *[out_d[n] for n in WEIGHTS],
            *[out_m[n] for n in WEIGHTS], *[out_v[n] for n in WEIGHTS])
```

```python
import functools
import math

import jax
import jax.numpy as jnp
from jax import lax
from jax.experimental import pallas as pl
from jax.experimental.pallas import tpu as pltpu

F32 = jnp.float32
BF16 = jnp.bfloat16
MXU_DTYPE = BF16
HI = lax.Precision.HIGHEST
MESH = pl.DeviceIdType.MESH

VMEM_LIMIT_BYTES = 56 * 1024 * 1024
ROW_TILE_BYTES = 5 * 1024 * 1024
MM_VMEM_BYTES = 40 * 1024 * 1024
MM_TILE_M = 1024
MM_TILE_N = 1408
FLAT_W = 2048
LANES = 128
SUBLANES = 8

HEAD_DIM = 64
LRU_BLOCKS = 8
LRU_CONV = 4
LRU_C = 8.0
SWA_WINDOW = 128
SWA_GROUP = 4
S5_GROUP = 16
S5_GROUPS = 32
S5_STATE = 64
ROPE_THETA = 10000.0
EPS = 1e-6
MACARON = 0.5
NEG = -1e30

ADAM_LR = 0.001
ADAM_B1 = 0.9
ADAM_B2 = 0.999
ADAM_EPS = 1e-08
ADAM_WD = 0.01
ADAM_STEP = 10

N_CHIPS = 4
N_DEV = 8


def _pick(n, cands):
    for c in cands:
        if n % c == 0:
            return c
    return n


def _tile(n, cap, unit):
    best = None
    for t in range(unit, min(n, cap) + 1, unit):
        if n % t == 0:
            best = t
    return n if best is None else best


def _params(sem=None):
    return pltpu.CompilerParams(dimension_semantics=sem, vmem_limit_bytes=VMEM_LIMIT_BYTES)


def rowwise(fn, rows, consts=(), outs=(), accs=(), name="rowwise", periods=None):
    rows, consts = list(rows), list(consts)
    n_r, n_c, n_o, n_a = len(rows), len(consts), len(outs), len(accs)
    R = rows[0].shape[0]
    periods = list(periods) if periods is not None else [None] * n_r
    per_row = sum(max(r.shape[1], LANES) * 4 for r in rows) + sum(max(f, LANES) * 4 for f, _ in outs)
    limit = min([R] + [p for p in periods if p is not None])
    tr = limit
    for c in (1024, 512, 256, 128, 64, 32, 16):
        if c <= limit and limit % c == 0 and R % c == 0 and c * per_row <= ROW_TILE_BYTES:
            tr = c
            break

    def row_map(period):
        if period is None:
            return lambda i: (i, 0)
        nb = period // tr
        return lambda i: (i % nb, 0)

    in_specs = [pl.BlockSpec((tr, r.shape[1]), row_map(p)) for r, p in zip(rows, periods)]
    in_specs += [pl.BlockSpec(c.shape, lambda i: (0, 0)) for c in consts]
    out_shape = [jax.ShapeDtypeStruct((R, f), dt) for f, dt in outs]
    out_shape += [jax.ShapeDtypeStruct(tuple(s), F32) for s in accs]
    out_specs = [pl.BlockSpec((tr, f), lambda i: (i, 0)) for f, _ in outs]
    out_specs += [pl.BlockSpec(tuple(s), lambda i: (0, 0)) for s in accs]

    def body(*refs):
        ins = [r[...] for r in refs[:n_r + n_c]]
        o_refs = refs[n_r + n_c:n_r + n_c + n_o]
        a_refs = refs[n_r + n_c + n_o:]
        ro, ra = fn(*ins)
        for ref, val in zip(o_refs, ro):
            ref[...] = val.astype(ref.dtype)
        if n_a:
            @pl.when(pl.program_id(0) == 0)
            def _():
                for ref in a_refs:
                    ref[...] = jnp.zeros(ref.shape, ref.dtype)
            for ref, val in zip(a_refs, ra):
                ref[...] += val.astype(F32)

    res = pl.pallas_call(
        body, grid=(R // tr,), in_specs=in_specs, out_specs=out_specs, out_shape=out_shape,
        name=name, compiler_params=_params(("arbitrary",)),
    )(*rows, *consts)
    return list(res)


def whole(fn, ins, outs, name="whole"):
    n_i = len(ins)

    def body(*refs):
        vals = fn(*[r[...] for r in refs[:n_i]])
        for ref, val in zip(refs[n_i:], vals):
            ref[...] = val.astype(ref.dtype)

    res = pl.pallas_call(
        body, out_shape=[jax.ShapeDtypeStruct(tuple(s), dt) for s, dt in outs],
        in_specs=[pl.BlockSpec(memory_space=pltpu.VMEM)] * n_i,
        out_specs=[pl.BlockSpec(memory_space=pltpu.VMEM)] * len(outs),
        name=name, compiler_params=_params(),
    )(*ins)
    return list(res)


_DOT_DIMS = {
    "nn": (((1,), (0,)), ((), ())),
    "nt": (((1,), (1,)), ((), ())),
    "tn": (((0,), (0,)), ((), ())),
}


def mm(a, b, mode="nn", out_dtype=F32, name="mm"):
    if mode == "nn":
        (M, K), (K2, N) = a.shape, b.shape
    elif mode == "nt":
        (M, K), (N, K2) = a.shape, b.shape
    else:
        (K, M), (K2, N) = a.shape, b.shape
    assert K == K2, (mode, a.shape, b.shape)
    tn = _tile(N, MM_TILE_N, LANES)
    if mode == "tn":
        tm, tk = _tile(M, MM_TILE_M, LANES), _tile(K, MM_TILE_M, 2 * SUBLANES)
    else:
        tm, tk = _tile(M, MM_TILE_M, 2 * SUBLANES), _tile(K, MM_TILE_N, LANES)

    def vmem_bytes(tm_, tk_):
        return (2 * (tm_ * tk_ * a.dtype.itemsize + tk_ * tn * b.dtype.itemsize
                     + tm_ * tn * jnp.dtype(out_dtype).itemsize) + tm_ * tn * 4)

    while vmem_bytes(tm, tk) > MM_VMEM_BYTES and tk % (2 * LANES) == 0 and K % (tk // 2) == 0:
        tk //= 2
    while vmem_bytes(tm, tk) > MM_VMEM_BYTES and tm % (2 * LANES) == 0 and M % (tm // 2) == 0:
        tm //= 2
    if mode == "tn":
        a_spec = pl.BlockSpec((tk, tm), lambda i, j, k: (k, i))
    else:
        a_spec = pl.BlockSpec((tm, tk), lambda i, j, k: (i, k))
    if mode == "nt":
        b_spec = pl.BlockSpec((tn, tk), lambda i, j, k: (j, k))
    else:
        b_spec = pl.BlockSpec((tk, tn), lambda i, j, k: (k, j))
    nk = K // tk
    dims = _DOT_DIMS[mode]

    def dot(a_ref, b_ref):
        return lax.dot_general(a_ref[...].astype(MXU_DTYPE), b_ref[...].astype(MXU_DTYPE), dims,
                               preferred_element_type=F32)

    def body_one(a_ref, b_ref, o_ref):
        o_ref[...] = dot(a_ref, b_ref).astype(o_ref.dtype)

    def body_acc(a_ref, b_ref, o_ref, acc_ref):
        k = pl.program_id(2)

        @pl.when(k == 0)
        def _():
            acc_ref[...] = dot(a_ref, b_ref)

        @pl.when(k > 0)
        def _():
            acc_ref[...] += dot(a_ref, b_ref)

        @pl.when(k == nk - 1)
        def _():
            o_ref[...] = acc_ref[...].astype(o_ref.dtype)

    return pl.pallas_call(
        body_one if nk == 1 else body_acc, grid=(M // tm, N // tn, nk), in_specs=[a_spec, b_spec],
        out_specs=pl.BlockSpec((tm, tn), lambda i, j, k: (i, j)),
        out_shape=jax.ShapeDtypeStruct((M, N), out_dtype),
        scratch_shapes=[] if nk == 1 else [pltpu.VMEM((tm, tn), F32)],
        name=name, compiler_params=_params(("parallel", "parallel", "arbitrary")),
    )(a, b)


def _roll_rows(x, d, reverse):
    return pltpu.roll(x, (SUBLANES - d) if reverse else d, 0)


def scan_real(a, b, reverse=False, name="scan_real"):
    S, W = b.shape
    cw = _pick(W, (256, 128))
    n_tiles = S // SUBLANES

    def body(a_ref, b_ref, o_ref):
        row = lax.broadcasted_iota(jnp.int32, (SUBLANES, cw), 0)
        edge = 0 if reverse else SUBLANES - 1

        def step(i, carry):
            t = (n_tiles - 1 - i) if reverse else i
            off = pl.multiple_of(t * SUBLANES, SUBLANES)
            A = a_ref[pl.ds(off, SUBLANES), :]
            B = b_ref[pl.ds(off, SUBLANES), :]
            for d in (1, 2, 4):
                m = (row < SUBLANES - d) if reverse else (row >= d)
                B = jnp.where(m, A * _roll_rows(B, d, reverse) + B, B)
                A = jnp.where(m, A * _roll_rows(A, d, reverse), A)
            h = B + A * carry
            o_ref[pl.ds(off, SUBLANES), :] = h
            return jnp.sum(jnp.where(row == edge, h, 0.0), axis=0, keepdims=True)

        lax.fori_loop(0, n_tiles, step, jnp.zeros((1, cw), F32))

    spec = pl.BlockSpec((S, cw), lambda j: (0, j))
    return pl.pallas_call(
        body, grid=(W // cw,), in_specs=[spec, spec], out_specs=spec,
        out_shape=jax.ShapeDtypeStruct((S, W), F32), name=name, compiler_params=_params(("parallel",)),
    )(a, b)


def scan_cplx(lam, bu, reverse=False, name="scan_cplx"):
    S, C = bu.shape
    n_tiles = S // SUBLANES
    half = LANES

    def cmul(ar, ai, br, bi):
        return ar * br - ai * bi, ar * bi + ai * br

    def body(lam_ref, bu_ref, o_ref):
        row = lax.broadcasted_iota(jnp.int32, (SUBLANES, half), 0)
        lr = lam_ref[:, :half]
        li = lam_ref[:, half:]
        if reverse:
            li = -li
        l1 = (lr, li)
        l2 = cmul(*l1, *l1)
        l4 = cmul(*l2, *l2)
        pr = jnp.zeros((SUBLANES, half), F32)
        pi = jnp.zeros((SUBLANES, half), F32)
        p = l1
        for r in range(SUBLANES):
            sel = row == ((SUBLANES - 1 - r) if reverse else r)
            pr = jnp.where(sel, p[0], pr)
            pi = jnp.where(sel, p[1], pi)
            p = cmul(*p, *l1)

        def step(i, carry):
            cr, ci = carry
            t = (n_tiles - 1 - i) if reverse else i
            off = pl.multiple_of(t * SUBLANES, SUBLANES)
            Br = bu_ref[pl.ds(off, SUBLANES), :half]
            Bi = bu_ref[pl.ds(off, SUBLANES), half:]
            for d, (qr, qi) in ((1, l1), (2, l2), (4, l4)):
                m = (row < SUBLANES - d) if reverse else (row >= d)
                sr, si = _roll_rows(Br, d, reverse), _roll_rows(Bi, d, reverse)
                nr = jnp.where(m, Br + qr * sr - qi * si, Br)
                ni = jnp.where(m, Bi + qr * si + qi * sr, Bi)
                Br, Bi = nr, ni
            hr = Br + pr * cr - pi * ci
            hi = Bi + pr * ci + pi * cr
            o_ref[pl.ds(off, SUBLANES), :half] = hr
            o_ref[pl.ds(off, SUBLANES), half:] = hi
            at_edge = row == (0 if reverse else SUBLANES - 1)
            return (jnp.sum(jnp.where(at_edge, hr, 0.0), axis=0, keepdims=True),
                    jnp.sum(jnp.where(at_edge, hi, 0.0), axis=0, keepdims=True))

        z = jnp.zeros((1, half), F32)
        lax.fori_loop(0, n_tiles, step, (z, z))

    spec = pl.BlockSpec((S, 2 * half), lambda j: (0, j))
    return pl.pallas_call(
        body, grid=(C // (2 * half),), in_specs=[pl.BlockSpec((1, 2 * half), lambda j: (0, j)), spec],
        out_specs=spec, out_shape=jax.ShapeDtypeStruct((S, C), F32), name=name,
        compiler_params=_params(("parallel",)),
    )(lam, bu)


def _attn_tile(S, window):
    return window if window is not None else _pick(S, (512, 256, 128))


def attn_fwd(q, k, v, sink, cq=None, ck=None, window=None, name="attn_fwd"):
    H, S, Dh = q.shape
    G = H // k.shape[0]
    T = _attn_tile(S, window)
    nq = S // T
    nks = nq if window is None else 2
    scale = Dh ** -0.5
    bias = cq is not None

    def kv_block(i, j):
        return jnp.minimum(j, i) if window is None else jnp.maximum(i - 1 + j, 0)

    def body(*refs):
        if bias:
            q_ref, k_ref, v_ref, s_ref, cq_ref, ck_ref, o_ref, lse_ref, m_scr, l_scr, acc_scr = refs
        else:
            q_ref, k_ref, v_ref, s_ref, o_ref, lse_ref, m_scr, l_scr, acc_scr = refs
        i, j = pl.program_id(1), pl.program_id(2)

        @pl.when(j == 0)
        def _():
            m_scr[...] = jnp.zeros(m_scr.shape, F32) + s_ref[0]
            l_scr[...] = jnp.ones(l_scr.shape, F32)
            acc_scr[...] = jnp.zeros(acc_scr.shape, F32)

        active = (j <= i) if window is None else (i - 1 + j >= 0)

        @pl.when(active)
        def _():
            kb = kv_block(i, j)
            s = lax.dot_general(q_ref[0].astype(MXU_DTYPE), k_ref[0].astype(MXU_DTYPE), _DOT_DIMS["nt"],
                                preferred_element_type=F32) * scale
            if bias:
                s = s + cq_ref[0] - ck_ref[0]
            qpos = i * T + lax.broadcasted_iota(jnp.int32, (T, T), 0)
            kpos = kb * T + lax.broadcasted_iota(jnp.int32, (T, T), 1)
            valid = kpos <= qpos
            if window is not None:
                valid = valid & (qpos - kpos < window)
            s = jnp.where(valid, s, NEG)
            m_old = m_scr[...]
            m_new = jnp.maximum(m_old, jnp.max(s, axis=-1, keepdims=True))
            alpha = jnp.exp(m_old - m_new)
            p = jnp.where(valid, jnp.exp(s - m_new), 0.0)
            l_scr[...] = alpha * l_scr[...] + jnp.sum(p, axis=-1, keepdims=True)
            acc_scr[...] = alpha * acc_scr[...] + jnp.dot(p.astype(MXU_DTYPE), v_ref[0].astype(MXU_DTYPE),
                                                          preferred_element_type=F32)
            m_scr[...] = m_new

        @pl.when(j == nks - 1)
        def _():
            o_ref[0] = acc_scr[...] / l_scr[...]
            lse_ref[0] = m_scr[...] + jnp.log(l_scr[...])

    in_specs = [
        pl.BlockSpec((1, T, Dh), lambda h, i, j: (h, i, 0)),
        pl.BlockSpec((1, T, Dh), lambda h, i, j: (h // G, kv_block(i, j), 0)),
        pl.BlockSpec((1, T, Dh), lambda h, i, j: (h // G, kv_block(i, j), 0)),
        pl.BlockSpec((1, 1, 1), lambda h, i, j: (h, 0, 0)),
    ]
    args = [q, k, v, sink]
    if bias:
        in_specs += [pl.BlockSpec((1, T, 1), lambda h, i, j: (h, i, 0)),
                     pl.BlockSpec((1, 1, T), lambda h, i, j: (h, 0, kv_block(i, j)))]
        args += [cq, ck]
    return pl.pallas_call(
        body, grid=(H, nq, nks), in_specs=in_specs,
        out_specs=[pl.BlockSpec((1, T, Dh), lambda h, i, j: (h, i, 0)),
                   pl.BlockSpec((1, T, 1), lambda h, i, j: (h, i, 0))],
        out_shape=[jax.ShapeDtypeStruct((H, S, Dh), F32), jax.ShapeDtypeStruct((H, S, 1), F32)],
        scratch_shapes=[pltpu.VMEM((T, 1), F32), pltpu.VMEM((T, 1), F32), pltpu.VMEM((T, Dh), F32)],
        name=name, compiler_params=_params(("parallel", "parallel", "arbitrary")),
    )(*args)


def attn_bwd(q, k, v, lse, do, delta, cq=None, ck=None, window=None, name="attn_bwd"):
    H, S, Dh = q.shape
    KVH = k.shape[0]
    G = H // KVH
    T = _attn_tile(S, window)
    nq = S // T
    nqs = nq if window is None else 2
    scale = Dh ** -0.5
    bias = cq is not None
    assert not bias or G == 1

    def q_block(kb, j):
        return jnp.maximum(j, kb) if window is None else jnp.minimum(kb + j, nq - 1)

    def body(*refs):
        if bias:
            (q_ref, k_ref, v_ref, lse_ref, do_ref, dl_ref, cq_ref, ck_ref,
             dq_ref, dk_ref, dv_ref, dcq_ref, dck_ref) = refs
        else:
            q_ref, k_ref, v_ref, lse_ref, do_ref, dl_ref, dq_ref, dk_ref, dv_ref = refs
        kb, g, j = pl.program_id(1), pl.program_id(2), pl.program_id(3)

        @pl.when((g == 0) & (j == 0))
        def _():
            dk_ref[...] = jnp.zeros(dk_ref.shape, F32)
            dv_ref[...] = jnp.zeros(dv_ref.shape, F32)
            if bias:
                dck_ref[...] = jnp.zeros(dck_ref.shape, F32)

        @pl.when((kb == 0) & (g == 0) & (j == 0))
        def _():
            dq_ref[...] = jnp.zeros(dq_ref.shape, F32)
            if bias:
                dcq_ref[...] = jnp.zeros(dcq_ref.shape, F32)

        active = (j >= kb) if window is None else (kb + j <= nq - 1)

        @pl.when(active)
        def _():
            qi = q_block(kb, j)
            off = pl.multiple_of(qi * T, T)
            qb, kk, vv = q_ref[0].astype(MXU_DTYPE), k_ref[0].astype(MXU_DTYPE), v_ref[0].astype(MXU_DTYPE)
            dob = do_ref[0].astype(MXU_DTYPE)
            s = lax.dot_general(qb, kk, _DOT_DIMS["nt"], preferred_element_type=F32) * scale
            if bias:
                s = s + cq_ref[0] - ck_ref[0]
            qpos = qi * T + lax.broadcasted_iota(jnp.int32, (T, T), 0)
            kpos = kb * T + lax.broadcasted_iota(jnp.int32, (T, T), 1)
            valid = kpos <= qpos
            if window is not None:
                valid = valid & (qpos - kpos < window)
            p = jnp.where(valid, jnp.exp(jnp.where(valid, s, NEG) - lse_ref[0]), 0.0)
            dv_ref[0] += lax.dot_general(p.astype(MXU_DTYPE), dob, _DOT_DIMS["tn"], preferred_element_type=F32)
            dp = lax.dot_general(dob, vv, _DOT_DIMS["nt"], preferred_element_type=F32)
            ds = p * (dp - dl_ref[0])
            dsb = ds.astype(MXU_DTYPE)
            dq_ref[0, g, pl.ds(off, T), :] += scale * jnp.dot(dsb, kk, preferred_element_type=F32)
            dk_ref[0] += scale * lax.dot_general(dsb, qb, _DOT_DIMS["tn"], preferred_element_type=F32)
            if bias:
                dcq_ref[0, g, pl.ds(off, T), :] += jnp.sum(ds, axis=1, keepdims=True)
                dck_ref[0] -= jnp.sum(ds, axis=0, keepdims=True)

    def qmap(kvh, kb, g, j):
        return (kvh * G + g, q_block(kb, j), 0)

    in_specs = [
        pl.BlockSpec((1, T, Dh), qmap),
        pl.BlockSpec((1, T, Dh), lambda kvh, kb, g, j: (kvh, kb, 0)),
        pl.BlockSpec((1, T, Dh), lambda kvh, kb, g, j: (kvh, kb, 0)),
        pl.BlockSpec((1, T, 1), qmap),
        pl.BlockSpec((1, T, Dh), qmap),
        pl.BlockSpec((1, T, 1), qmap),
    ]
    args = [q, k, v, lse, do, delta]
    out_specs = [
        pl.BlockSpec((1, G, S, Dh), lambda kvh, kb, g, j: (kvh, 0, 0, 0)),
        pl.BlockSpec((1, T, Dh), lambda kvh, kb, g, j: (kvh, kb, 0)),
        pl.BlockSpec((1, T, Dh), lambda kvh, kb, g, j: (kvh, kb, 0)),
    ]
    out_shape = [jax.ShapeDtypeStruct((KVH, G, S, Dh), F32), jax.ShapeDtypeStruct((KVH, S, Dh), F32),
                 jax.ShapeDtypeStruct((KVH, S, Dh), F32)]
    if bias:
        in_specs += [pl.BlockSpec((1, T, 1), qmap),
                     pl.BlockSpec((1, 1, T), lambda kvh, kb, g, j: (kvh, 0, kb))]
        args += [cq, ck]
        out_specs += [pl.BlockSpec((1, G, S, 1), lambda kvh, kb, g, j: (kvh, 0, 0, 0)),
                      pl.BlockSpec((1, 1, T), lambda kvh, kb, g, j: (kvh, 0, kb))]
        out_shape += [jax.ShapeDtypeStruct((KVH, G, S, 1), F32), jax.ShapeDtypeStruct((KVH, 1, S), F32)]
    res = pl.pallas_call(
        body, grid=(KVH, nq, G, nqs), in_specs=in_specs, out_specs=out_specs, out_shape=out_shape,
        name=name, compiler_params=_params(("arbitrary", "arbitrary", "arbitrary", "arbitrary")),
    )(*args)
    dq = res[0].reshape(H, S, Dh)
    if bias:
        return dq, res[1], res[2], res[3].reshape(H, S, 1), res[4]
    return dq, res[1], res[2]


def _rms(x, g):
    return x * lax.rsqrt(jnp.mean(x * x, axis=-1, keepdims=True) + EPS) * g


def _sigmoid(x):
    return 1.0 / (1.0 + jnp.exp(-x))


def _softplus(x):
    return jnp.maximum(x, 0.0) + jnp.log(1.0 + jnp.exp(-jnp.abs(x)))


def _log_sigmoid(x):
    return jnp.minimum(x, 0.0) - jnp.log(1.0 + jnp.exp(-jnp.abs(x)))


def _gelu(x):
    return 0.5 * x * (1.0 + jnp.tanh(math.sqrt(2.0 / math.pi) * (x + 0.044715 * (x * x * x))))


def _silu(x):
    return x * _sigmoid(x)


def _ffn_act(gu):
    f = gu.shape[1] // 2
    return MACARON * _silu(gu[:, :f]) * gu[:, f:]


def _qk_prep(rope):
    def f(x, *rest):
        if rope:
            cos, sin, g, rot = rest
        else:
            (g,) = rest
        y = _rms(x, g)
        if rope:
            y = y * cos + jnp.dot(y, rot, precision=HI, preferred_element_type=F32) * sin
        return y
    return f


def _lru_gates(pre, xc, ba, bx, lam):
    w = xc.shape[1]
    r = _sigmoid(pre[:, :w] + ba)
    i = _sigmoid(pre[:, w:] + bx)
    log_a = -LRU_C * r * _softplus(lam)
    a = jnp.exp(log_a)
    b = jnp.sqrt(1.0 - jnp.exp(2.0 * log_a)) * (i * xc)
    return a, b


def _lru_conv(x0, x1, x2, x3, w0, w1, w2, w3, cb):
    return cb + x0 * w0 + x1 * w1 + x2 * w2 + x3 * w3


def _s5_params(lre, lim, ldt, gsel, bre, bim):
    dt = jnp.sum(gsel * jnp.exp(ldt), axis=1, keepdims=True)
    er = jnp.exp(lre * dt)
    ang = lim * dt
    lbr, lbi = er * jnp.cos(ang), er * jnp.sin(ang)
    nr, ni = lbr - 1.0, lbi
    den = lre * lre + lim * lim
    fr, fi = (nr * lre + ni * lim) / den, (ni * lre - nr * lim) / den
    return lbr, lbi, fr * bre - fi * bim, fr * bim + fi * bre


def _s5_out(yssm, u, d):
    return _gelu(yssm + d * u)


def _glu(z, gl, gb):
    return z * _sigmoid(gl + gb)


def _ple_out(x, gpre, epre, pn):
    return x + _sigmoid(gpre) * _rms(epre, pn)


def _vjp(fn, args, cots):
    _, pull = jax.vjp(fn, *args)
    return pull(cots)


def add_norm(x, y, g, name):
    D = x.shape[1]
    if y is None:
        return x, rowwise(lambda xv, gv: ([_rms(xv, gv)], []), [x], [g], outs=[(D, BF16)], name=name)[0]
    xn, n = rowwise(lambda xv, yv, gv: ([xv + yv, _rms(xv + yv, gv)], []), [x, y], [g],
                    outs=[(D, F32), (D, BF16)], name=name)
    return xn, n


def norm_bwd(x, g, dn, dx_res, name):
    D = x.shape[1]

    def f(xv, dnv, dxv, gv):
        dx, dg = _vjp(_rms, (xv, gv), dnv)
        return [dxv + dx], [dg]

    return rowwise(f, [x, dn, dx_res], [g], outs=[(D, F32)], accs=[(1, D)], name=name)


def ffn_fwd(n, wgu, wd, tag):
    gu = mm(n, wgu, out_dtype=BF16, name=f"ffn_gu_{tag}")
    F = wd.shape[0]
    act = rowwise(lambda g: ([_ffn_act(g.astype(F32))], []), [gu], outs=[(F, BF16)], name=f"ffn_act_{tag}")[0]
    y = mm(act, wd, name=f"ffn_down_{tag}")
    return y, (n, gu, act)


def ffn_bwd(dy, saved, wgu, wd, tag):
    n, gu, act = saved
    dact = mm(dy, wd, "nt", out_dtype=BF16, name=f"ffn_dact_{tag}")
    dwd = mm(act, dy, "tn", name=f"ffn_dwd_{tag}")
    dgu = rowwise(lambda g, d: (list(_vjp(_ffn_act, (g.astype(F32),), d.astype(F32))), []), [gu, dact],
                  outs=[(gu.shape[1], BF16)], name=f"ffn_dgu_{tag}")[0]
    dn = mm(dgu, wgu, "nt", name=f"ffn_dn_{tag}")
    dwgu = mm(n, dgu, "tn", name=f"ffn_dwgu_{tag}")
    return dn, dwgu, dwd


def _heads(x, H):
    S = x.shape[0]
    return x.reshape(S, H, HEAD_DIM).transpose(1, 0, 2)


def _unheads(x):
    H, S, _ = x.shape
    return x.transpose(1, 0, 2).reshape(S, H * HEAD_DIM)


def _shift_down(x, n=1):
    return jnp.pad(x, ((n, 0), (0, 0)))[:x.shape[0]]


def _shift_up(x, n=1):
    return jnp.pad(x, ((0, n), (0, 0)))[n:]


def _block_diag(w):
    B, I, J = w.shape
    eye = jnp.eye(B, dtype=w.dtype)
    return (w[:, :, None, :] * eye[:, None, :, None]).reshape(B * I, B * J)


def _block_diag_take(x, B):
    I, J = x.shape[0] // B, x.shape[1] // B
    eye = jnp.eye(B, dtype=x.dtype)
    return jnp.sum(x.reshape(B, I, B, J) * eye[:, None, :, None], axis=2)


def _rope_tables(S):
    half = HEAD_DIM // 2
    inv = jnp.power(ROPE_THETA, -jnp.arange(half, dtype=F32) / half)
    ang = jnp.arange(S, dtype=F32)[:, None] * inv[None, :]
    cos = jnp.concatenate([jnp.cos(ang), jnp.cos(ang)], axis=1)
    sin = jnp.concatenate([jnp.sin(ang), jnp.sin(ang)], axis=1)
    r = jnp.arange(HEAD_DIM)[:, None]
    c = jnp.arange(HEAD_DIM)[None, :]
    rot = jnp.where(r == c + half, -1.0, 0.0) + jnp.where(c == r + half, 1.0, 0.0)
    return cos, sin, rot.astype(F32)


def qk_prep_fwd(x_hm, g, rope_tabs, name):
    H, S, Dh = x_hm.shape
    rows = [x_hm.reshape(H * S, Dh)]
    consts = [g.reshape(1, Dh)]
    periods = [None]
    if rope_tabs is not None:
        rows += [rope_tabs[0], rope_tabs[1]]
        consts += [rope_tabs[2]]
        periods += [S, S]
    fn = _qk_prep(rope_tabs is not None)
    y = rowwise(lambda *a: ([fn(*a)], []), rows, consts, outs=[(Dh, F32)], name=name, periods=periods)[0]
    return y.reshape(H, S, Dh)


def qk_prep_bwd(x_hm, g, rope_tabs, dy_hm, name):
    H, S, Dh = x_hm.shape
    rope = rope_tabs is not None
    rows = [x_hm.reshape(H * S, Dh), dy_hm.reshape(H * S, Dh)]
    consts = [g.reshape(1, Dh)]
    periods = [None, None]
    if rope:
        rows += [rope_tabs[0], rope_tabs[1]]
        consts += [rope_tabs[2]]
        periods += [S, S]
    fn = _qk_prep(rope)

    def f(xv, dyv, *rest):
        if rope:
            cos, sin, gv, rot = rest
            dx, dg = _vjp(lambda a, b: fn(a, cos, sin, b, rot), (xv, gv), dyv)
        else:
            (gv,) = rest
            dx, dg = _vjp(fn, (xv, gv), dyv)
        return [dx], [dg]

    dx, dg = rowwise(f, rows, consts, outs=[(Dh, F32)], accs=[(1, Dh)], name=name, periods=periods)
    return dx.reshape(H, S, Dh), dg.reshape(Dh)


def attn_delta(do_hm, o_hm, name):
    H, S, Dh = o_hm.shape
    d = rowwise(lambda a, b: ([jnp.sum(a * b, axis=-1, keepdims=True)], []),
                [do_hm.reshape(H * S, Dh), o_hm.reshape(H * S, Dh)], outs=[(1, F32)], name=name)[0]
    return d.reshape(H, S, 1)


def even_mixer_fwd(h, w, tag):
    S = h.shape[0]
    W = 512
    H = 8
    z = mm(h, w["w_in"], name=f"ev_in_{tag}")
    xa, ya, q, k, v, f = (z[:, 0:512], z[:, 512:1024], z[:, 1024:1536], z[:, 1536:2048], z[:, 2048:2560],
                          z[:, 2560:2688])
    xs = [_shift_down(xa, LRU_CONV - 1 - tap) for tap in range(LRU_CONV)]
    taps = [w["conv_w"][tap][None] for tap in range(LRU_CONV)]
    xc = rowwise(lambda *a: ([_lru_conv(*a)], []), xs, taps + [w["conv_b"]], outs=[(W, F32)],
                 name=f"lru_conv_{tag}")[0]
    pre = mm(xc, w["w_ax"], name=f"lru_gates_mm_{tag}")
    a, b = rowwise(lambda p_, x_, ba, bx, lam: (list(_lru_gates(p_, x_, ba, bx, lam)), []), [pre, xc],
                   [w["ba"], w["bx"], w["lam"]], outs=[(W, F32), (W, F32)], name=f"lru_gates_{tag}")
    hs = scan_real(a, b, name=f"lru_scan_{tag}")
    a_out = rowwise(lambda y_, h_: ([_gelu(y_) * h_], []), [ya, hs], outs=[(W, F32)], name=f"lru_out_{tag}")[0]
    lf = rowwise(lambda f_, bf: ([_log_sigmoid(f_ + bf)], []), [f], [w["bf"]], outs=[(LANES, F32)],
                 name=f"fox_logf_{tag}")[0]
    c = scan_real(jnp.ones_like(lf), lf, name=f"fox_cumsum_{tag}")
    c_hm = c[:, :H].T
    q_hm, k_hm, v_hm = _heads(q, H), _heads(k, H), _heads(v, H)
    qn = qk_prep_fwd(q_hm, w["qn"], None, f"fox_qprep_{tag}")
    kn = qk_prep_fwd(k_hm, w["kn"], None, f"fox_kprep_{tag}")
    sink = jnp.full((H, 1, 1), NEG, F32)
    o_hm, lse = attn_fwd(qn, kn, v_hm, sink, c_hm[:, :, None], c_hm[:, None, :], name=f"fox_attn_{tag}")
    mo = jnp.concatenate([a_out, _unheads(o_hm)], axis=1).astype(BF16)
    y = mm(mo, w["w_out"], name=f"ev_out_{tag}")
    saved = dict(h=h, xs=xs, xc=xc, pre=pre, a=a, hs=hs, ya=ya, f=f, c_hm=c_hm, q_hm=q_hm, k_hm=k_hm,
                 v_hm=v_hm, qn=qn, kn=kn, o_hm=o_hm, lse=lse, mo=mo)
    return y, saved


def even_mixer_bwd(dy, sv, w, tag):
    W = 512
    H = 8
    S = dy.shape[0]
    g = {}
    dmo = mm(dy, w["w_out"], "nt", name=f"ev_dmo_{tag}")
    g["w_out"] = mm(sv["mo"], dy, "tn", name=f"ev_dwout_{tag}")
    da_out, do = dmo[:, :W], dmo[:, W:]
    do_hm = _heads(do, H)
    delta = attn_delta(do_hm, sv["o_hm"], f"fox_delta_{tag}")
    c_hm = sv["c_hm"]
    dqn, dkn, dv_hm, dcq, dck = attn_bwd(sv["qn"], sv["kn"], sv["v_hm"], sv["lse"], do_hm, delta,
                                          c_hm[:, :, None], c_hm[:, None, :], name=f"fox_attn_bwd_{tag}")
    dq_hm, g["qn"] = qk_prep_bwd(sv["q_hm"], w["qn"], None, dqn, f"fox_qprep_bwd_{tag}")
    dk_hm, g["kn"] = qk_prep_bwd(sv["k_hm"], w["kn"], None, dkn, f"fox_kprep_bwd_{tag}")
    dc = (dcq[:, :, 0] + dck[:, 0, :]).T
    dc = jnp.pad(dc, ((0, 0), (0, LANES - H)))
    dlf = scan_real(jnp.ones_like(dc), dc, reverse=True, name=f"fox_cumsum_bwd_{tag}")

    def f_logf(f_, d_, bf):
        df, dbf = _vjp(lambda a_, b_: _log_sigmoid(a_ + b_), (f_, bf), d_)
        return [df], [dbf]

    df, dbf = rowwise(f_logf, [sv["f"], dlf], [w["bf"]], outs=[(LANES, F32)], accs=[(1, LANES)],
                      name=f"fox_logf_bwd_{tag}")
    g["bf"] = dbf[0, :H]
    def f_out(y_, h_, d_):
        dyv, dhv = _vjp(lambda a_, b_: _gelu(a_) * b_, (y_, h_), d_)
        return [dyv, dhv], []

    dya, dhs = rowwise(f_out, [sv["ya"], sv["hs"], da_out], outs=[(W, F32), (W, F32)], name=f"lru_out_bwd_{tag}")
    gs = scan_real(_shift_up(sv["a"]), dhs, reverse=True, name=f"lru_scan_bwd_{tag}")

    def f_gates(p_, x_, g_, hp_, ba, bx, lam):
        dp, dx, dba, dbx, dlam = _vjp(_lru_gates, (p_, x_, ba, bx, lam), (g_ * hp_, g_))
        return [dp, dx], [dba, dbx, dlam]

    dpre, dxc, dba, dbx, dlam = rowwise(f_gates, [sv["pre"], sv["xc"], gs, _shift_down(sv["hs"])],
                                        [w["ba"], w["bx"], w["lam"]], outs=[(2 * W, BF16), (W, F32)],
                                        accs=[(1, W)] * 3, name=f"lru_gates_bwd_{tag}")
    g["ba"], g["bx"], g["lam"] = dba[0], dbx[0], dlam[0]
    dxc2 = mm(dpre, w["w_ax"], "nt", name=f"lru_gates_mm_dx_{tag}")
    g["w_ax"] = mm(sv["xc"], dpre, "tn", name=f"lru_gates_mm_dw_{tag}")

    def f_conv(d1, d2, x0, x1, x2, x3):
        d = d1 + d2
        return [d], [jnp.sum(d, axis=0, keepdims=True)] + [jnp.sum(d * xv, axis=0, keepdims=True)
                                                           for xv in (x0, x1, x2, x3)]

    dxc_t, dcb, dw0, dw1, dw2, dw3 = rowwise(f_conv, [dxc, dxc2] + sv["xs"], outs=[(W, F32)],
                                             accs=[(1, W)] * 5, name=f"lru_conv_bwd_{tag}")
    g["conv_b"] = dcb[0]
    g["conv_w"] = jnp.concatenate([dw0, dw1, dw2, dw3], axis=0)
    ds_ = [_shift_up(dxc_t, LRU_CONV - 1 - tap) for tap in range(LRU_CONV)]
    taps = [w["conv_w"][tap][None] for tap in range(LRU_CONV)]
    dxa = rowwise(lambda a, b, c, d, w0, w1, w2, w3: ([a * w0 + b * w1 + c * w2 + d * w3], []), ds_, taps,
                  outs=[(W, F32)], name=f"lru_conv_dx_{tag}")[0]
    dz = jnp.concatenate([dxa, dya, _unheads(dq_hm), _unheads(dk_hm), _unheads(dv_hm), df], axis=1).astype(BF16)
    g["w_in"] = mm(sv["h"], dz, "tn", name=f"ev_dwin_{tag}")
    dh = mm(dz, w["w_in"], "nt", name=f"ev_dh_{tag}")
    return dh, g


def odd_mixer_fwd(h, w, tag):
    S = h.shape[0]
    H, KVH = 8, 2
    z = mm(h, w["w_in"], name=f"od_in_{tag}")
    q, k, v, u = z[:, 0:512], z[:, 512:640], z[:, 640:768], z[:, 768:1280]
    tabs = _rope_tables(S)
    q_hm, k_hm, v_hm = _heads(q, H), _heads(k, KVH), _heads(v, KVH)
    qn = qk_prep_fwd(q_hm, w["qn"], tabs, f"swa_qprep_{tag}")
    kn = qk_prep_fwd(k_hm, w["kn"], tabs, f"swa_kprep_{tag}")
    sink = w["sinks"].reshape(H, 1, 1)
    o_hm, lse = attn_fwd(qn, kn, v_hm, sink, window=SWA_WINDOW, name=f"swa_attn_{tag}")
    lam, bexp = w["s5_lam"], w["s5_bexp"]
    bu = mm(u, bexp, name=f"s5_bu_{tag}")
    hs = scan_cplx(lam, bu, name=f"s5_scan_{tag}")
    yssm = mm(hs, w["s5_cexp"], name=f"s5_y_{tag}")
    zz = rowwise(lambda y_, u_, d_: ([_s5_out(y_, u_, d_)], []), [yssm, u], [w["s5_d"]], outs=[(512, F32)],
                 name=f"s5_gelu_{tag}")[0]
    gl = mm(zz, w["glu_w"], name=f"s5_glu_mm_{tag}")
    d_out = rowwise(lambda z_, g_, b_: ([_glu(z_, g_, b_)], []), [zz, gl], [w["glu_b"]], outs=[(512, F32)],
                    name=f"s5_glu_{tag}")[0]
    mo = jnp.concatenate([_unheads(o_hm), d_out], axis=1).astype(BF16)
    y = mm(mo, w["w_out"], name=f"od_out_{tag}")
    saved = dict(h=h, q_hm=q_hm, k_hm=k_hm, v_hm=v_hm, qn=qn, kn=kn, o_hm=o_hm, lse=lse, u=u, hs=hs, yssm=yssm,
                 zz=zz, gl=gl, mo=mo, tabs=tabs)
    return y, saved


def odd_mixer_bwd(dy, sv, w, tag):
    H, KVH = 8, 2
    g = {}
    dmo = mm(dy, w["w_out"], "nt", name=f"od_dmo_{tag}")
    g["w_out"] = mm(sv["mo"], dy, "tn", name=f"od_dwout_{tag}")
    do, dd = dmo[:, :512], dmo[:, 512:]
    do_hm = _heads(do, H)
    delta = attn_delta(do_hm, sv["o_hm"], f"swa_delta_{tag}")
    dqn, dkn, dv_hm = attn_bwd(sv["qn"], sv["kn"], sv["v_hm"], sv["lse"], do_hm, delta, window=SWA_WINDOW,
                               name=f"swa_attn_bwd_{tag}")
    dq_hm, g["qn"] = qk_prep_bwd(sv["q_hm"], w["qn"], sv["tabs"], dqn, f"swa_qprep_bwd_{tag}")
    dk_hm, g["kn"] = qk_prep_bwd(sv["k_hm"], w["kn"], sv["tabs"], dkn, f"swa_kprep_bwd_{tag}")
    lse_t, delta_t = sv["lse"][:, :, 0].T, delta[:, :, 0].T
    g["sinks"] = rowwise(lambda l_, d_, s_: ([], [jnp.sum(-jnp.exp(s_ - l_) * d_, axis=0, keepdims=True)]),
                         [lse_t, delta_t], [w["sinks"].reshape(1, H)], accs=[(1, H)], name=f"swa_dsink_{tag}")[0][0]
    def f_glu(z_, g_, d_, b_):
        dz_, dg_, db_ = _vjp(_glu, (z_, g_, b_), d_)
        return [dz_, dg_], [db_]

    dzz1, dgl, dglb = rowwise(f_glu, [sv["zz"], sv["gl"], dd], [w["glu_b"]], outs=[(512, F32), (512, BF16)],
                              accs=[(1, 512)], name=f"s5_glu_bwd_{tag}")
    g["glu_b"] = dglb[0]
    g["glu_w"] = mm(sv["zz"], dgl, "tn", name=f"s5_glu_dw_{tag}")
    dzz2 = mm(dgl, w["glu_w"], "nt", name=f"s5_glu_dz_{tag}")

    def f_gelu(y_, u_, d1, d2, dpar):
        dy_, du_, dd_ = _vjp(_s5_out, (y_, u_, dpar), d1 + d2)
        return [dy_, du_], [dd_]

    dyssm, du1, dsd = rowwise(f_gelu, [sv["yssm"], sv["u"], dzz1, dzz2], [w["s5_d"]],
                              outs=[(512, F32), (512, F32)], accs=[(1, 512)], name=f"s5_gelu_bwd_{tag}")
    g["s5_d"] = dsd[0]
    dhs = mm(dyssm, w["s5_cexp"], "nt", name=f"s5_dh_{tag}")
    g["s5_cexp"] = mm(sv["hs"], dyssm, "tn", name=f"s5_dc_{tag}")
    gs = scan_cplx(w["s5_lam"], dhs, reverse=True, name=f"s5_scan_bwd_{tag}")
    g["s5_bexp"] = mm(sv["u"], gs, "tn", name=f"s5_db_{tag}")
    du2 = mm(gs, w["s5_bexp"], "nt", name=f"s5_du_{tag}")

    def f_dlam(g_, hp_):
        C = g_.shape[1]
        outs_r, outs_i = [], []
        for j in range(C // (2 * LANES)):
            gr, gi = g_[:, 2 * LANES * j:2 * LANES * j + LANES], g_[:, 2 * LANES * j + LANES:2 * LANES * (j + 1)]
            hr, hi = hp_[:, 2 * LANES * j:2 * LANES * j + LANES], hp_[:, 2 * LANES * j + LANES:2 * LANES * (j + 1)]
            outs_r.append(jnp.sum(gr * hr + gi * hi, axis=0, keepdims=True))
            outs_i.append(jnp.sum(gi * hr - gr * hi, axis=0, keepdims=True))
        return [], [jnp.concatenate([x for pair in zip(outs_r, outs_i) for x in pair], axis=1)]

    g["s5_lam"] = rowwise(f_dlam, [gs, _shift_down(sv["hs"])], accs=[(1, gs.shape[1])], name=f"s5_dlam_{tag}")[0]
    du = rowwise(lambda a_, b_: ([a_ + b_], []), [du1, du2], outs=[(512, F32)], name=f"s5_du_add_{tag}")[0]
    dz = jnp.concatenate([_unheads(dq_hm), _unheads(dk_hm), _unheads(dv_hm), du], axis=1).astype(BF16)
    g["w_in"] = mm(sv["h"], dz, "tn", name=f"od_dwin_{tag}")
    dh = mm(dz, w["w_in"], "nt", name=f"od_dh_{tag}")
    return dh, g


def _s5_cols(x_re, x_im):
    n = x_re.shape[0] // LANES
    return jnp.stack([x_re.reshape(n, LANES), x_im.reshape(n, LANES)], axis=1).reshape(1, 2 * n * LANES)


def _s5_uncols(x):
    n = x.shape[1] // (2 * LANES)
    y = x.reshape(n, 2, LANES)
    return y[:, 0].reshape(-1), y[:, 1].reshape(-1)


def _s5_gsel():
    return jnp.repeat(jnp.eye(S5_GROUPS, dtype=F32), S5_STATE, axis=0)


def s5_prep_fwd(lre, lim, ldt, bre, bim, cre, cim, tag):
    GP = S5_GROUPS * S5_STATE
    ins = [lre.reshape(GP, 1), lim.reshape(GP, 1), ldt.reshape(1, S5_GROUPS), _s5_gsel(),
           bre.reshape(GP, S5_GROUP), bim.reshape(GP, S5_GROUP)]
    lbr, lbi, bbr, bbi = whole(_s5_params, ins, [((GP, 1), F32)] * 2 + [((GP, S5_GROUP), F32)] * 2,
                               name=f"s5_params_{tag}")
    lam = _s5_cols(lbr[:, 0], lbi[:, 0])

    def expand_b(bb):
        return _block_diag(bb.reshape(S5_GROUPS, S5_STATE, S5_GROUP).transpose(0, 2, 1))

    n = GP // LANES
    bexp = jnp.stack([expand_b(bbr).reshape(-1, n, LANES), expand_b(bbi).reshape(-1, n, LANES)],
                     axis=2).reshape(-1, 2 * GP)
    c_r = _block_diag(cre.transpose(0, 2, 1))
    c_i = _block_diag(cim.transpose(0, 2, 1))
    cexp = jnp.stack([c_r.reshape(n, LANES, -1), -c_i.reshape(n, LANES, -1)], axis=1).reshape(2 * GP, -1)
    return lam, bexp.astype(BF16), cexp.astype(BF16), ins


def s5_prep_bwd(ins, dlam, dbexp, dcexp, tag):
    GP = S5_GROUPS * S5_STATE
    n = GP // LANES
    dlr, dli = _s5_uncols(dlam)
    db = dbexp.reshape(-1, n, 2, LANES)

    def take_b(x):
        return _block_diag_take(x, S5_GROUPS).transpose(0, 2, 1).reshape(GP, S5_GROUP)

    dbbr, dbbi = take_b(db[:, :, 0].reshape(-1, GP)), take_b(db[:, :, 1].reshape(-1, GP))
    dc = dcexp.reshape(n, 2, LANES, -1)
    dcre = _block_diag_take(dc[:, 0].reshape(GP, -1), S5_GROUPS).transpose(0, 2, 1)
    dcim = -_block_diag_take(dc[:, 1].reshape(GP, -1), S5_GROUPS).transpose(0, 2, 1)

    def f(lre, lim, ldt, gsel, bre, bim, c1, c2, c3, c4):
        d = _vjp(lambda a, b, c, e, f_: _s5_params(a, b, c, gsel, e, f_), (lre, lim, ldt, bre, bim), (c1, c2, c3, c4))
        return d

    outs = [((GP, 1), F32)] * 2 + [((1, S5_GROUPS), F32)] + [((GP, S5_GROUP), F32)] * 2
    dlre, dlim, dldt, dbre, dbim = whole(f, ins + [dlr.reshape(GP, 1), dli.reshape(GP, 1), dbbr, dbbi], outs,
                                          name=f"s5_params_bwd_{tag}")
    shp = (S5_GROUPS, S5_STATE)
    return dict(lre=dlre.reshape(shp), lim=dlim.reshape(shp), ldt=dldt.reshape(S5_GROUPS),
                bre=dbre.reshape(S5_GROUPS, S5_STATE, S5_GROUP), bim=dbim.reshape(S5_GROUPS, S5_STATE, S5_GROUP),
                cre=dcre, cim=dcim)


def _place():
    return lax.axis_index("x"), lax.axis_index("y"), lax.axis_index("c")


def _other_chips(x, y):
    return [(1 - x, y), (x, 1 - y), (1 - x, 1 - y)]


def gather_chips(w):
    def body(w_ref, out_ref, send_sems, recv_sems, local_sem):
        x, y, c = _place()
        me, sibling = (x, y, c), (x, y, 1 - c)
        chips = _other_chips(x, y)
        mine = 2 * x + y

        def copy(k, src, dst, to):
            return pltpu.make_async_remote_copy(src_ref=src, dst_ref=dst, send_sem=send_sems.at[k],
                                                recv_sem=recv_sems.at[k], device_id=to, device_id_type=MESH)

        local = pltpu.make_async_copy(w_ref, out_ref.at[mine], local_sem)
        local.start()
        first = [copy(j, w_ref.at[c], out_ref.at[mine, c], (*chip, c)) for j, chip in enumerate(chips)]
        for cp in first:
            cp.start()
        passed = [copy(3 + j, out_ref.at[2 * chip[0] + chip[1], c], out_ref.at[2 * chip[0] + chip[1], c], sibling)
                  for j, chip in enumerate(chips)]
        for j, chip in enumerate(chips):
            copy(j, w_ref.at[c], out_ref.at[2 * chip[0] + chip[1], c], me).wait_recv()
            passed[j].start()
        for j, chip in enumerate(chips):
            copy(3 + j, w_ref.at[c], out_ref.at[2 * chip[0] + chip[1], 1 - c], me).wait_recv()
        for cp in first + passed:
            cp.wait_send()
        local.wait()

    return pl.pallas_call(
        body, out_shape=jax.ShapeDtypeStruct((N_CHIPS,) + w.shape, w.dtype),
        in_specs=[pl.BlockSpec(memory_space=pl.ANY)], out_specs=pl.BlockSpec(memory_space=pl.ANY),
        scratch_shapes=[pltpu.SemaphoreType.DMA((6,)), pltpu.SemaphoreType.DMA((6,)), pltpu.SemaphoreType.DMA],
        name="gather_chips",
    )(w)


def sibling_halves(g):
    def body(g_ref, out_ref, send_sems, recv_sems):
        x, y, c = _place()
        me, sibling = (x, y, c), (x, y, 1 - c)

        def copy(k, to):
            return pltpu.make_async_remote_copy(src_ref=g_ref.at[k, 1 - c], dst_ref=out_ref.at[k],
                                                send_sem=send_sems.at[k], recv_sem=recv_sems.at[k],
                                                device_id=to, device_id_type=MESH)

        cps = [copy(k, sibling) for k in range(N_CHIPS)]
        for cp in cps:
            cp.start()
        for k in range(N_CHIPS):
            copy(k, me).wait_recv()
        for cp in cps:
            cp.wait_send()

    return pl.pallas_call(
        body, out_shape=jax.ShapeDtypeStruct((N_CHIPS,) + g.shape[2:], g.dtype),
        in_specs=[pl.BlockSpec(memory_space=pl.ANY)], out_specs=pl.BlockSpec(memory_space=pl.ANY),
        scratch_shapes=[pltpu.SemaphoreType.DMA((N_CHIPS,)), pltpu.SemaphoreType.DMA((N_CHIPS,))],
        name="sibling_halves",
    )(g)


def exchange_chips(p):
    def body(p_ref, out_ref, send_sems, recv_sems):
        x, y, c = _place()
        me = (x, y, c)
        chips = _other_chips(x, y)

        def copy(j, chip, to):
            return pltpu.make_async_remote_copy(src_ref=p_ref.at[2 * chip[0] + chip[1]], dst_ref=out_ref.at[j],
                                                send_sem=send_sems.at[j], recv_sem=recv_sems.at[j],
                                                device_id=to, device_id_type=MESH)

        cps = [copy(j, chip, (*chip, c)) for j, chip in enumerate(chips)]
        for cp in cps:
            cp.start()
        for j, chip in enumerate(chips):
            copy(j, chip, me).wait_recv()
        for cp in cps:
            cp.wait_send()

    return pl.pallas_call(
        body, out_shape=jax.ShapeDtypeStruct((3,) + p.shape[1:], p.dtype),
        in_specs=[pl.BlockSpec(memory_space=pl.ANY)], out_specs=pl.BlockSpec(memory_space=pl.ANY),
        scratch_shapes=[pltpu.SemaphoreType.DMA((3,)), pltpu.SemaphoreType.DMA((3,))],
        name="exchange_chips",
    )(p)


def sibling_join(r):
    def body(r_ref, out_ref, send_sem, recv_sem, local_sem):
        x, y, c = _place()
        local = pltpu.make_async_copy(r_ref, out_ref.at[c], local_sem)
        local.start()
        cp = pltpu.make_async_remote_copy(src_ref=r_ref, dst_ref=out_ref.at[c], send_sem=send_sem,
                                          recv_sem=recv_sem, device_id=(x, y, 1 - c), device_id_type=MESH)
        cp.start()
        pltpu.make_async_remote_copy(src_ref=r_ref, dst_ref=out_ref.at[1 - c], send_sem=send_sem,
                                     recv_sem=recv_sem, device_id=(x, y, c), device_id_type=MESH).wait_recv()
        cp.wait_send()
        local.wait()

    return pl.pallas_call(
        body, out_shape=jax.ShapeDtypeStruct((2,) + r.shape, r.dtype),
        in_specs=[pl.BlockSpec(memory_space=pl.ANY)], out_specs=pl.BlockSpec(memory_space=pl.ANY),
        scratch_shapes=[pltpu.SemaphoreType.DMA, pltpu.SemaphoreType.DMA, pltpu.SemaphoreType.DMA],
        name="sibling_join",
    )(r)


def gather_devices(v, name):
    R = v.shape[0]

    def body(v_ref, out_ref, send_sems, recv_sems, local_sem):
        x, y, c = _place()
        me, sibling = (x, y, c), (x, y, 1 - c)
        chips = _other_chips(x, y)

        def rows(px, py, pc):
            return out_ref.at[pl.ds((4 * px + 2 * py + pc) * R, R), :]

        def copy(k, block, to, src=None):
            return pltpu.make_async_remote_copy(src_ref=rows(*block) if src is None else src, dst_ref=rows(*block),
                                                send_sem=send_sems.at[k], recv_sem=recv_sems.at[k],
                                                device_id=to, device_id_type=MESH)

        mine = pltpu.make_async_copy(v_ref, rows(*me), local_sem)
        mine.start()
        first = [copy(0, me, sibling, src=v_ref)]
        first += [copy(1 + j, me, (*chip, c), src=v_ref) for j, chip in enumerate(chips)]
        for cp in first:
            cp.start()
        passed = [copy(4 + j, (*chip, c), sibling) for j, chip in enumerate(chips)]
        for j, chip in enumerate(chips):
            copy(1 + j, (*chip, c), me).wait_recv()
            passed[j].start()
        copy(0, sibling, me).wait_recv()
        for j, chip in enumerate(chips):
            copy(4 + j, (*chip, 1 - c), me).wait_recv()
        for cp in first + passed:
            cp.wait_send()
        mine.wait()

    return pl.pallas_call(
        body, out_shape=jax.ShapeDtypeStruct((N_DEV * R, LANES), v.dtype),
        in_specs=[pl.BlockSpec(memory_space=pltpu.VMEM)], out_specs=pl.BlockSpec(memory_space=pltpu.VMEM),
        scratch_shapes=[pltpu.SemaphoreType.DMA((7,)), pltpu.SemaphoreType.DMA((7,)), pltpu.SemaphoreType.DMA],
        name=name, compiler_params=_params(),
    )(v)


def _flat_rows(n, mult):
    return -(-n // (LANES * mult)) * mult


def _adam(w, g, m, v):
    m = ADAM_B1 * m + (1.0 - ADAM_B1) * g
    v = ADAM_B2 * v + (1.0 - ADAM_B2) * (g * g)
    m_hat = m / (1.0 - ADAM_B1 ** ADAM_STEP)
    v_hat = v / (1.0 - ADAM_B2 ** ADAM_STEP)
    return -ADAM_LR * (m_hat / (jnp.sqrt(v_hat) + ADAM_EPS) + ADAM_WD * w), m, v


def adam_2d(w, g, m, v, name):
    shape = w.shape
    F = shape[-1]
    a = [t.reshape(-1, F) for t in (w, g, m, v)]
    d, m2, v2 = rowwise(lambda w_, g_, m_, v_: (list(_adam(w_, g_, m_, v_)), []), a, outs=[(F, F32)] * 3, name=name)
    return d.reshape(shape), m2.reshape(shape), v2.reshape(shape)


WEIGHTS = ['ffn1_norm', 'ffn1_wg', 'ffn1_wu', 'ffn1_wd', 'mix_norm', 'ffn2_norm', 'ffn2_wg', 'ffn2_wu', 'ffn2_wd',
           'ple_w', 'ple_norm', 'ple_gate_norm', 'ple_gate_w', 'ev_w_in', 'lru_conv_w', 'lru_conv_b', 'lru_wa',
           'lru_ba', 'lru_wx', 'lru_bx', 'lru_lambda', 'fox_bf', 'fox_q_norm', 'fox_k_norm', 'ev_w_out', 'od_w_in',
           'swa_q_norm', 'swa_k_norm', 'swa_sinks', 's5_lambda_re', 's5_lambda_im', 's5_log_dt', 's5_b_re',
           's5_b_im', 's5_c_re', 's5_c_im', 's5_d', 's5_glu_w', 's5_glu_b', 'od_w_out']
SHARD_AXIS = {'ffn1_wg': 2, 'ffn1_wu': 2, 'ffn1_wd': 1, 'ffn2_wg': 2, 'ffn2_wu': 2, 'ffn2_wd': 1, 'ple_w': 2,
              'ple_gate_w': 1, 'ev_w_in': 2, 'lru_conv_w': 2, 'ev_w_out': 1, 'od_w_in': 2, 's5_d': 1,
              's5_glu_w': 1, 's5_glu_b': 1, 'od_w_out': 1}
EXACT_SHARDED = ('lru_conv_w', 's5_d', 's5_glu_b')
SHARDED = [n for n in WEIGHTS if n in SHARD_AXIS]
REPLICATED = [n for n in WEIGHTS if n not in SHARD_AXIS]


def gather_weights(shards):
    def unpack(allw, names):
        full, off = {}, 0
        for n in names:
            shp = shards[n].shape
            size = math.prod(shp)
            parts = allw[:, off:off + size].reshape((N_CHIPS,) + shp)
            full[n] = jnp.concatenate([parts[i] for i in range(N_CHIPS)], axis=SHARD_AXIS[n])
            off += size
        return full

    names = [n for n in SHARDED if n not in EXACT_SHARDED]
    flat = jnp.concatenate([shards[n].astype(BF16).reshape(-1) for n in names])
    rows = _flat_rows(flat.shape[0], 2 * 16)
    flat = jnp.pad(flat, (0, rows * LANES - flat.shape[0])).reshape(2, rows // 2, LANES)
    full = unpack(gather_chips(flat).reshape(N_CHIPS, rows * LANES), names)
    flat = jnp.concatenate([shards[n].astype(F32).reshape(-1) for n in EXACT_SHARDED])
    rows = _flat_rows(flat.shape[0], SUBLANES)
    flat = jnp.pad(flat, (0, rows * LANES - flat.shape[0])).reshape(rows, LANES)
    every = gather_devices(flat, "gather_exact_weights").reshape(N_CHIPS, 2, rows * LANES)
    full.update(unpack(every[:, 0], EXACT_SHARDED))
    return full


def reduce_sharded(grads, shapes):
    cols = []
    for n in SHARDED:
        parts = jnp.split(grads[n], N_CHIPS, axis=SHARD_AXIS[n])
        cols.append(jnp.stack([p_.reshape(-1) for p_ in parts]).astype(BF16))
    flat = jnp.concatenate(cols, axis=1)
    L = flat.shape[1]
    W = FLAT_W
    R = -(-L // (2 * W * 16)) * 16
    flat = jnp.pad(flat, ((0, 0), (0, 2 * R * W - L))).reshape(N_CHIPS, 2, R, W)
    c = lax.axis_index("c")
    chip = 2 * lax.axis_index("x") + lax.axis_index("y")
    theirs = sibling_halves(flat)
    mine = lax.dynamic_index_in_dim(flat, c, axis=1, keepdims=False)
    pair = rowwise(lambda a, b: ([a.astype(F32) + b.astype(F32)], []),
                   [mine.reshape(N_CHIPS * R, W), theirs.reshape(N_CHIPS * R, W)],
                   outs=[(W, BF16)], name="reduce_pair")[0].reshape(N_CHIPS, R, W)
    got = exchange_chips(pair)
    own = lax.dynamic_index_in_dim(pair, chip, axis=0, keepdims=False)
    half = rowwise(lambda o, a, b, d: ([(o.astype(F32) + a.astype(F32)) + (b.astype(F32) + d.astype(F32))], []),
                   [own, got[0], got[1], got[2]], outs=[(W, F32)], name="reduce_chips")[0]
    red = sibling_join(half).reshape(-1)
    out, off = {}, 0
    for n in SHARDED:
        size = math.prod(shapes[n])
        out[n] = red[off:off + size].reshape(shapes[n])
        off += size
    return out


def reduce_replicated(grads):
    flat = jnp.concatenate([grads[n].astype(F32).reshape(-1) for n in REPLICATED])
    rows = _flat_rows(flat.shape[0], SUBLANES)
    flat = jnp.pad(flat, (0, rows * LANES - flat.shape[0])).reshape(rows, LANES)
    return gather_devices(flat, "gather_small_grads").reshape(N_DEV, rows, LANES)


def _flatten_small(tensors):
    flat = jnp.concatenate([tensors[n].astype(F32).reshape(-1) for n in REPLICATED])
    rows = _flat_rows(flat.shape[0], SUBLANES)
    return jnp.pad(flat, (0, rows * LANES - flat.shape[0])).reshape(rows, LANES)


def _unflatten_small(flat, shapes):
    flat = flat.reshape(-1)
    out, off = {}, 0
    for n in REPLICATED:
        size = math.prod(shapes[n])
        out[n] = flat[off:off + size].reshape(shapes[n])
        off += size
    return out


def _layer_weights(full, small, i):
    j = i // 2
    w = dict(
        g1=small['ffn1_norm'][i][None], gm=small['mix_norm'][i][None], g2=small['ffn2_norm'][i][None],
        gp=small['ple_norm'][i][None], gg=small['ple_gate_norm'][i][None],
        wgu1=jnp.concatenate([full['ffn1_wg'][i], full['ffn1_wu'][i]], axis=1), wd1=full['ffn1_wd'][i],
        wgu2=jnp.concatenate([full['ffn2_wg'][i], full['ffn2_wu'][i]], axis=1), wd2=full['ffn2_wd'][i],
        ple_w=full['ple_w'][i], ple_gate_w=full['ple_gate_w'][i],
    )
    if i % 2 == 0:
        w_in = full['ev_w_in'][j]
        w['mix'] = dict(
            w_in=jnp.pad(w_in, ((0, 0), (0, 2688 - w_in.shape[1]))), w_out=full['ev_w_out'][j],
            conv_w=full['lru_conv_w'][j].astype(F32), conv_b=small['lru_conv_b'][j][None],
            w_ax=jnp.concatenate([_block_diag(small['lru_wa'][j]), _block_diag(small['lru_wx'][j])],
                                 axis=1).astype(BF16),
            ba=small['lru_ba'][j][None], bx=small['lru_bx'][j][None], lam=small['lru_lambda'][j][None],
            bf=jnp.pad(small['fox_bf'][j], (0, LANES - 8))[None], qn=small['fox_q_norm'][j],
            kn=small['fox_k_norm'][j])
    else:
        lam, bexp, cexp, ins = s5_prep_fwd(small['s5_lambda_re'][j], small['s5_lambda_im'][j], small['s5_log_dt'][j],
                                           small['s5_b_re'][j], small['s5_b_im'][j], small['s5_c_re'][j],
                                           small['s5_c_im'][j], f"L{i}")
        w['mix'] = dict(
            w_in=full['od_w_in'][j], w_out=full['od_w_out'][j], qn=small['swa_q_norm'][j], kn=small['swa_k_norm'][j],
            sinks=small['swa_sinks'][j], s5_lam=lam, s5_bexp=bexp, s5_cexp=cexp, s5_ins=ins,
            s5_d=full['s5_d'][j].astype(F32)[None], glu_w=full['s5_glu_w'][j], glu_b=full['s5_glu_b'][j].astype(F32)[None])
    return w


def layer_fwd(x, p_i, w, i):
    tag = f"L{i}"
    sv = {}
    x0, n1 = add_norm(x, None, w['g1'], f"norm1_{tag}")
    y1, sv['ffn1'] = ffn_fwd(n1, w['wgu1'], w['wd1'], f"1_{tag}")
    x1, hm = add_norm(x0, y1, w['gm'], f"normm_{tag}")
    if i % 2 == 0:
        ym, sv['mix'] = even_mixer_fwd(hm, w['mix'], tag)
    else:
        ym, sv['mix'] = odd_mixer_fwd(hm, w['mix'], tag)
    x2, n2 = add_norm(x1, ym, w['g2'], f"norm2_{tag}")
    y2, sv['ffn2'] = ffn_fwd(n2, w['wgu2'], w['wd2'], f"2_{tag}")
    x3, ng = add_norm(x2, y2, w['gg'], f"normg_{tag}")
    gpre = mm(ng, w['ple_gate_w'], name=f"ple_gate_{tag}")
    epre = mm(p_i, w['ple_w'], name=f"ple_emb_{tag}")
    D = x.shape[1]
    x4 = rowwise(lambda a, b, c, pn: ([_ple_out(a, b, c, pn)], []), [x3, gpre, epre], [w['gp']], outs=[(D, F32)],
                 name=f"ple_out_{tag}")[0]
    sv.update(x0=x0, x1=x1, x2=x2, x3=x3, ng=ng, gpre=gpre, epre=epre, p=p_i)
    return x4, sv


def layer_bwd(dx4, sv, w, i):
    tag = f"L{i}"
    D = dx4.shape[1]
    g = {}

    def f_ple(a, b, c, d, pn):
        da, db, dc, dpn = _vjp(_ple_out, (a, b, c, pn), d)
        return [db, dc], [dpn]

    dgpre, depre, dgp = rowwise(f_ple, [sv['x3'], sv['gpre'], sv['epre'], dx4], [w['gp']],
                                outs=[(D, BF16), (D, BF16)], accs=[(1, D)], name=f"ple_out_bwd_{tag}")
    g['gp'] = dgp[0]
    g['ple_w'] = mm(sv['p'], depre, "tn", name=f"ple_emb_dw_{tag}")
    g['ple_gate_w'] = mm(sv['ng'], dgpre, "tn", name=f"ple_gate_dw_{tag}")
    dng = mm(dgpre, w['ple_gate_w'], "nt", name=f"ple_gate_dx_{tag}")
    dx3, dgg = norm_bwd(sv['x3'], w['gg'], dng, dx4, f"normg_bwd_{tag}")
    g['gg'] = dgg[0]
    dn2, g['wgu2'], g['wd2'] = ffn_bwd(dx3, sv['ffn2'], w['wgu2'], w['wd2'], f"2_{tag}")
    dx2, dg2 = norm_bwd(sv['x2'], w['g2'], dn2, dx3, f"norm2_bwd_{tag}")
    g['g2'] = dg2[0]
    if i % 2 == 0:
        dhm, g['mix'] = even_mixer_bwd(dx2, sv['mix'], w['mix'], tag)
    else:
        dhm, g['mix'] = odd_mixer_bwd(dx2, sv['mix'], w['mix'], tag)
    dx1, dgm = norm_bwd(sv['x1'], w['gm'], dhm, dx2, f"normm_bwd_{tag}")
    g['gm'] = dgm[0]
    dn1, g['wgu1'], g['wd1'] = ffn_bwd(dx1, sv['ffn1'], w['wgu1'], w['wd1'], f"1_{tag}")
    dx0, dg1 = norm_bwd(sv['x0'], w['g1'], dn1, dx1, f"norm1_bwd_{tag}")
    g['g1'] = dg1[0]
    return dx0, g


def _collect_grads(layer_grads, depth):
    F = layer_grads[0]['wd1'].shape[0]
    st = lambda xs: jnp.stack(xs)
    G = {}
    L = layer_grads
    G['ffn1_norm'] = st([g['g1'] for g in L])
    G['mix_norm'] = st([g['gm'] for g in L])
    G['ffn2_norm'] = st([g['g2'] for g in L])
    G['ple_norm'] = st([g['gp'] for g in L])
    G['ple_gate_norm'] = st([g['gg'] for g in L])
    G['ffn1_wg'] = st([g['wgu1'][:, :F] for g in L])
    G['ffn1_wu'] = st([g['wgu1'][:, F:] for g in L])
    G['ffn1_wd'] = st([g['wd1'] for g in L])
    G['ffn2_wg'] = st([g['wgu2'][:, :F] for g in L])
    G['ffn2_wu'] = st([g['wgu2'][:, F:] for g in L])
    G['ffn2_wd'] = st([g['wd2'] for g in L])
    G['ple_w'] = st([g['ple_w'] for g in L])
    G['ple_gate_w'] = st([g['ple_gate_w'] for g in L])
    ev = [L[i]['mix'] for i in range(0, depth, 2)]
    od = [L[i]['mix'] for i in range(1, depth, 2)]
    G['ev_w_in'] = st([m['w_in'][:, :2568] for m in ev])
    G['ev_w_out'] = st([m['w_out'] for m in ev])
    G['lru_conv_w'] = st([m['conv_w'] for m in ev])
    G['lru_conv_b'] = st([m['conv_b'] for m in ev])
    G['lru_wa'] = st([_block_diag_take(m['w_ax'][:, :512], LRU_BLOCKS) for m in ev])
    G['lru_wx'] = st([_block_diag_take(m['w_ax'][:, 512:], LRU_BLOCKS) for m in ev])
    G['lru_ba'] = st([m['ba'] for m in ev])
    G['lru_bx'] = st([m['bx'] for m in ev])
    G['lru_lambda'] = st([m['lam'] for m in ev])
    G['fox_bf'] = st([m['bf'] for m in ev])
    G['fox_q_norm'] = st([m['qn'] for m in ev])
    G['fox_k_norm'] = st([m['kn'] for m in ev])
    G['od_w_in'] = st([m['w_in'] for m in od])
    G['od_w_out'] = st([m['w_out'] for m in od])
    G['swa_q_norm'] = st([m['qn'] for m in od])
    G['swa_k_norm'] = st([m['kn'] for m in od])
    G['swa_sinks'] = st([m['sinks'] for m in od])
    G['s5_lambda_re'] = st([m['s5']['lre'] for m in od])
    G['s5_lambda_im'] = st([m['s5']['lim'] for m in od])
    G['s5_log_dt'] = st([m['s5']['ldt'] for m in od])
    G['s5_b_re'] = st([m['s5']['bre'] for m in od])
    G['s5_b_im'] = st([m['s5']['bim'] for m in od])
    G['s5_c_re'] = st([m['s5']['cre'] for m in od])
    G['s5_c_im'] = st([m['s5']['cim'] for m in od])
    G['s5_d'] = st([m['s5_d'] for m in od])
    G['s5_glu_w'] = st([m['glu_w'] for m in od])
    G['s5_glu_b'] = st([m['glu_b'] for m in od])
    return G


def local_step(x, p, target, full, small):
    depth = p.shape[0]
    S, D = x.shape
    ws = [_layer_weights(full, small, i) for i in range(depth)]
    saved = []
    xi = x
    for i in range(depth):
        xi, sv = layer_fwd(xi, p[i], ws[i], i)
        saved.append(sv)

    def f_loss(y, t):
        e = y - t
        return [e * (1.0 / D)], [0.5 * jnp.sum(jnp.mean(e * e, axis=-1, keepdims=True), axis=0, keepdims=True)]

    dx, loss = rowwise(f_loss, [xi, target], outs=[(D, F32)], accs=[(1, 1)], name="loss")
    grads = [None] * depth
    for i in reversed(range(depth)):
        dx, grads[i] = layer_bwd(dx, saved[i], ws[i], i)
        if i % 2 == 1:
            m = grads[i]['mix']
            m['s5'] = s5_prep_bwd(ws[i]['mix']['s5_ins'], m['s5_lam'], m['s5_bexp'], m['s5_cexp'], f"L{i}")
    return loss[0, 0], dx, _collect_grads(grads, depth)


def kernel(x, p, ffn1_norm, ffn1_wg, ffn1_wu, ffn1_wd, mix_norm, ffn2_norm, ffn2_wg, ffn2_wu, ffn2_wd, ple_w, ple_norm, ple_gate_norm, ple_gate_w, ev_w_in, lru_conv_w, lru_conv_b, lru_wa, lru_ba, lru_wx, lru_bx, lru_lambda, fox_bf, fox_q_norm, fox_k_norm, ev_w_out, od_w_in, swa_q_norm, swa_k_norm, swa_sinks, s5_lambda_re, s5_lambda_im, s5_log_dt, s5_b_re, s5_b_im, s5_c_re, s5_c_im, s5_d, s5_glu_w, s5_glu_b, od_w_out, loss_target, m_ffn1_norm, m_ffn1_wg, m_ffn1_wu, m_ffn1_wd, m_mix_norm, m_ffn2_norm, m_ffn2_wg, m_ffn2_wu, m_ffn2_wd, m_ple_w, m_ple_norm, m_ple_gate_norm, m_ple_gate_w, m_ev_w_in, m_lru_conv_w, m_lru_conv_b, m_lru_wa, m_lru_ba, m_lru_wx, m_lru_bx, m_lru_lambda, m_fox_bf, m_fox_q_norm, m_fox_k_norm, m_ev_w_out, m_od_w_in, m_swa_q_norm, m_swa_k_norm, m_swa_sinks, m_s5_lambda_re, m_s5_lambda_im, m_s5_log_dt, m_s5_b_re, m_s5_b_im, m_s5_c_re, m_s5_c_im, m_s5_d, m_s5_glu_w, m_s5_glu_b, m_od_w_out, v_ffn1_norm, v_ffn1_wg, v_ffn1_wu, v_ffn1_wd, v_mix_norm, v_ffn2_norm, v_ffn2_wg, v_ffn2_wu, v_ffn2_wd, v_ple_w, v_ple_norm, v_ple_gate_norm, v_ple_gate_w, v_ev_w_in, v_lru_conv_w, v_lru_conv_b, v_lru_wa, v_lru_ba, v_lru_wx, v_lru_bx, v_lru_lambda, v_fox_bf, v_fox_q_norm, v_fox_k_norm, v_ev_w_out, v_od_w_in, v_swa_q_norm, v_swa_k_norm, v_swa_sinks, v_s5_lambda_re, v_s5_lambda_im, v_s5_log_dt, v_s5_b_re, v_s5_b_im, v_s5_c_re, v_s5_c_im, v_s5_d, v_s5_glu_w, v_s5_glu_b, v_od_w_out):
    args = locals()
    wts = {n: args[n] for n in WEIGHTS}
    ms = {n: args["m_" + n] for n in WEIGHTS}
    vs = {n: args["v_" + n] for n in WEIGHTS}
    shapes = {n: wts[n].shape for n in WEIGHTS}

    full = gather_weights(wts)
    small = {n: wts[n] for n in REPLICATED}
    loss, dx, G = local_step(x[0], p[:, 0], loss_target[0], full, small)
    loss = lax.psum(loss, ("x", "y", "c"))

    gsh = reduce_sharded(G, shapes)
    g8 = reduce_replicated(G)
    wf, mf, vf = _flatten_small(wts), _flatten_small(ms), _flatten_small(vs)

    def f_small(g0, g1, g2, g3, g4, g5, g6, g7, w_, m_, v_):
        gsum = ((g0 + g1) + (g2 + g3)) + ((g4 + g5) + (g6 + g7))
        return [gsum] + list(_adam(w_, gsum, m_, v_)), []

    gs_f, ds_f, ms_f, vs_f = rowwise(f_small, [g8[d] for d in range(N_DEV)] + [wf, mf, vf],
                                     outs=[(LANES, F32)] * 4, name="adam_small")
    out_g, out_d, out_m, out_v = {}, {}, {}, {}
    for dst, flat in ((out_g, gs_f), (out_d, ds_f), (out_m, ms_f), (out_v, vs_f)):
        dst.update(_unflatten_small(flat, shapes))
    for n in SHARDED:
        out_g[n] = gsh[n]
        out_d[n], out_m[n], out_v[n] = adam_2d(wts[n], gsh[n], ms[n], vs[n], f"adam_{n}")
    return (loss, dx[None], *[out_g[n] for n in WEIGHTS], *[out_d[n] for n in WEIGHTS],
            *[out_m[n] for n in WEIGHTS], *[out_v[n] for n in WEIGHTS])
```

```python
import functools
import math

import jax
import jax.numpy as jnp
from jax import lax
from jax.experimental import pallas as pl
from jax.experimental.pallas import tpu as pltpu

F32 = jnp.float32
BF16 = jnp.bfloat16
MXU_DTYPE = BF16
HI = lax.Precision.HIGHEST
MESH = pl.DeviceIdType.MESH

VMEM_LIMIT_BYTES = 56 * 1024 * 1024
ROW_TILE_BYTES = 5 * 1024 * 1024
MM_VMEM_BYTES = 40 * 1024 * 1024
MM_TILE_M = 1024
MM_TILE_N = 1408
FLAT_W = 2048
LANES = 128
SUBLANES = 8

HEAD_DIM = 64
LRU_BLOCKS = 8
LRU_CONV = 4
LRU_C = 8.0
SWA_WINDOW = 128
SWA_GROUP = 4
S5_GROUP = 16
S5_GROUPS = 32
S5_STATE = 64
ROPE_THETA = 10000.0
EPS = 1e-6
MACARON = 0.5
NEG = -1e30

ADAM_LR = 0.001
ADAM_B1 = 0.9
ADAM_B2 = 0.999
ADAM_EPS = 1e-08
ADAM_WD = 0.01
ADAM_STEP = 10

N_CHIPS = 4
N_DEV = 8


def _pick(n, cands):
    for c in cands:
        if n % c == 0:
            return c
    return n


def _tile(n, cap, unit):
    best = None
    for t in range(unit, min(n, cap) + 1, unit):
        if n % t == 0:
            best = t
    return n if best is None else best


def _params(sem=None):
    return pltpu.CompilerParams(dimension_semantics=sem, vmem_limit_bytes=VMEM_LIMIT_BYTES)


def rowwise(fn, rows, consts=(), outs=(), accs=(), name="rowwise", periods=None):
    rows, consts = list(rows), list(consts)
    n_r, n_c, n_o, n_a = len(rows), len(consts), len(outs), len(accs)
    R = rows[0].shape[0]
    periods = list(periods) if periods is not None else [None] * n_r
    per_row = sum(max(r.shape[1], LANES) * 4 for r in rows) + sum(max(f, LANES) * 4 for f, _ in outs)
    limit = min([R] + [p for p in periods if p is not None])
    tr = limit
    for c in (1024, 512, 256, 128, 64, 32, 16):
        if c <= limit and limit % c == 0 and R % c == 0 and c * per_row <= ROW_TILE_BYTES:
            tr = c
            break

    def row_map(period):
        if period is None:
            return lambda i: (i, 0)
        nb = period // tr
        return lambda i: (i % nb, 0)

    in_specs = [pl.BlockSpec((tr, r.shape[1]), row_map(p)) for r, p in zip(rows, periods)]
    in_specs += [pl.BlockSpec(c.shape, lambda i: (0, 0)) for c in consts]
    out_shape = [jax.ShapeDtypeStruct((R, f), dt) for f, dt in outs]
    out_shape += [jax.ShapeDtypeStruct(tuple(s), F32) for s in accs]
    out_specs = [pl.BlockSpec((tr, f), lambda i: (i, 0)) for f, _ in outs]
    out_specs += [pl.BlockSpec(tuple(s), lambda i: (0, 0)) for s in accs]

    def body(*refs):
        ins = [r[...] for r in refs[:n_r + n_c]]
        o_refs = refs[n_r + n_c:n_r + n_c + n_o]
        a_refs = refs[n_r + n_c + n_o:]
        ro, ra = fn(*ins)
        for ref, val in zip(o_refs, ro):
            ref[...] = val.astype(ref.dtype)
        if n_a:
            @pl.when(pl.program_id(0) == 0)
            def _():
                for ref in a_refs:
                    ref[...] = jnp.zeros(ref.shape, ref.dtype)
            for ref, val in zip(a_refs, ra):
                ref[...] += val.astype(F32)

    res = pl.pallas_call(
        body, grid=(R // tr,), in_specs=in_specs, out_specs=out_specs, out_shape=out_shape,
        name=name, compiler_params=_params(("arbitrary",)),
    )(*rows, *consts)
    return list(res)


def whole(fn, ins, outs, name="whole"):
    n_i = len(ins)

    def body(*refs):
        vals = fn(*[r[...] for r in refs[:n_i]])
        for ref, val in zip(refs[n_i:], vals):
            ref[...] = val.astype(ref.dtype)

    res = pl.pallas_call(
        body, out_shape=[jax.ShapeDtypeStruct(tuple(s), dt) for s, dt in outs],
        in_specs=[pl.BlockSpec(memory_space=pltpu.VMEM)] * n_i,
        out_specs=[pl.BlockSpec(memory_space=pltpu.VMEM)] * len(outs),
        name=name, compiler_params=_params(),
    )(*ins)
    return list(res)


_DOT_DIMS = {
    "nn": (((1,), (0,)), ((), ())),
    "nt": (((1,), (1,)), ((), ())),
    "tn": (((0,), (0,)), ((), ())),
}


def mm(a, b, mode="nn", out_dtype=F32, name="mm"):
    if mode == "nn":
        (M, K), (K2, N) = a.shape, b.shape
    elif mode == "nt":
        (M, K), (N, K2) = a.shape, b.shape
    else:
        (K, M), (K2, N) = a.shape, b.shape
    assert K == K2, (mode, a.shape, b.shape)
    tn = _tile(N, MM_TILE_N, LANES)
    if mode == "tn":
        tm, tk = _tile(M, MM_TILE_M, LANES), _tile(K, MM_TILE_M, 2 * SUBLANES)
    else:
        tm, tk = _tile(M, MM_TILE_M, 2 * SUBLANES), _tile(K, MM_TILE_N, LANES)

    def vmem_bytes(tm_, tk_):
        return (2 * (tm_ * tk_ * a.dtype.itemsize + tk_ * tn * b.dtype.itemsize
                     + tm_ * tn * jnp.dtype(out_dtype).itemsize) + tm_ * tn * 4)

    while vmem_bytes(tm, tk) > MM_VMEM_BYTES and tk % (2 * LANES) == 0 and K % (tk // 2) == 0:
        tk //= 2
    while vmem_bytes(tm, tk) > MM_VMEM_BYTES and tm % (2 * LANES) == 0 and M % (tm // 2) == 0:
        tm //= 2
    if mode == "tn":
        a_spec = pl.BlockSpec((tk, tm), lambda i, j, k: (k, i))
    else:
        a_spec = pl.BlockSpec((tm, tk), lambda i, j, k: (i, k))
    if mode == "nt":
        b_spec = pl.BlockSpec((tn, tk), lambda i, j, k: (j, k))
    else:
        b_spec = pl.BlockSpec((tk, tn), lambda i, j, k: (k, j))
    nk = K // tk
    dims = _DOT_DIMS[mode]

    def dot(a_ref, b_ref):
        return lax.dot_general(a_ref[...].astype(MXU_DTYPE), b_ref[...].astype(MXU_DTYPE), dims,
                               preferred_element_type=F32)

    def body_one(a_ref, b_ref, o_ref):
        o_ref[...] = dot(a_ref, b_ref).astype(o_ref.dtype)

    def body_acc(a_ref, b_ref, o_ref, acc_ref):
        k = pl.program_id(2)

        @pl.when(k == 0)
        def _():
            acc_ref[...] = dot(a_ref, b_ref)

        @pl.when(k > 0)
        def _():
            acc_ref[...] += dot(a_ref, b_ref)

        @pl.when(k == nk - 1)
        def _():
            o_ref[...] = acc_ref[...].astype(o_ref.dtype)

    return pl.pallas_call(
        body_one if nk == 1 else body_acc, grid=(M // tm, N // tn, nk), in_specs=[a_spec, b_spec],
        out_specs=pl.BlockSpec((tm, tn), lambda i, j, k: (i, j)),
        out_shape=jax.ShapeDtypeStruct((M, N), out_dtype),
        scratch_shapes=[] if nk == 1 else [pltpu.VMEM((tm, tn), F32)],
        name=name, compiler_params=_params(("parallel", "parallel", "arbitrary")),
    )(a, b)


def _roll_rows(x, d, reverse):
    return pltpu.roll(x, (SUBLANES - d) if reverse else d, 0)


def scan_real(a, b, reverse=False, name="scan_real"):
    S, W = b.shape
    cw = _pick(W, (256, 128))
    n_tiles = S // SUBLANES

    def body(a_ref, b_ref, o_ref):
        row = lax.broadcasted_iota(jnp.int32, (SUBLANES, cw), 0)
        edge = 0 if reverse else SUBLANES - 1

        def step(i, carry):
            t = (n_tiles - 1 - i) if reverse else i
            off = pl.multiple_of(t * SUBLANES, SUBLANES)
            A = a_ref[pl.ds(off, SUBLANES), :]
            B = b_ref[pl.ds(off, SUBLANES), :]
            for d in (1, 2, 4):
                m = (row < SUBLANES - d) if reverse else (row >= d)
                B = jnp.where(m, A * _roll_rows(B, d, reverse) + B, B)
                A = jnp.where(m, A * _roll_rows(A, d, reverse), A)
            h = B + A * carry
            o_ref[pl.ds(off, SUBLANES), :] = h
            return jnp.sum(jnp.where(row == edge, h, 0.0), axis=0, keepdims=True)

        lax.fori_loop(0, n_tiles, step, jnp.zeros((1, cw), F32))

    spec = pl.BlockSpec((S, cw), lambda j: (0, j))
    return pl.pallas_call(
        body, grid=(W // cw,), in_specs=[spec, spec], out_specs=spec,
        out_shape=jax.ShapeDtypeStruct((S, W), F32), name=name, compiler_params=_params(("parallel",)),
    )(a, b)


def scan_cplx(lam, bu, reverse=False, name="scan_cplx"):
    S, C = bu.shape
    n_tiles = S // SUBLANES
    half = LANES

    def cmul(ar, ai, br, bi):
        return ar * br - ai * bi, ar * bi + ai * br

    def body(lam_ref, bu_ref, o_ref):
        row = lax.broadcasted_iota(jnp.int32, (SUBLANES, half), 0)
        lr = lam_ref[:, :half]
        li = lam_ref[:, half:]
        if reverse:
            li = -li
        l1 = (lr, li)
        l2 = cmul(*l1, *l1)
        l4 = cmul(*l2, *l2)
        pr = jnp.zeros((SUBLANES, half), F32)
        pi = jnp.zeros((SUBLANES, half), F32)
        p = l1
        for r in range(SUBLANES):
            sel = row == ((SUBLANES - 1 - r) if reverse else r)
            pr = jnp.where(sel, p[0], pr)
            pi = jnp.where(sel, p[1], pi)
            p = cmul(*p, *l1)

        def step(i, carry):
            cr, ci = carry
            t = (n_tiles - 1 - i) if reverse else i
            off = pl.multiple_of(t * SUBLANES, SUBLANES)
            Br = bu_ref[pl.ds(off, SUBLANES), :half]
            Bi = bu_ref[pl.ds(off, SUBLANES), half:]
            for d, (qr, qi) in ((1, l1), (2, l2), (4, l4)):
                m = (row < SUBLANES - d) if reverse else (row >= d)
                sr, si = _roll_rows(Br, d, reverse), _roll_rows(Bi, d, reverse)
                nr = jnp.where(m, Br + qr * sr - qi * si, Br)
                ni = jnp.where(m, Bi + qr * si + qi * sr, Bi)
                Br, Bi = nr, ni
            hr = Br + pr * cr - pi * ci
            hi = Bi + pr * ci + pi * cr
            o_ref[pl.ds(off, SUBLANES), :half] = hr
            o_ref[pl.ds(off, SUBLANES), half:] = hi
            at_edge = row == (0 if reverse else SUBLANES - 1)
            return (jnp.sum(jnp.where(at_edge, hr, 0.0), axis=0, keepdims=True),
                    jnp.sum(jnp.where(at_edge, hi, 0.0), axis=0, keepdims=True))

        z = jnp.zeros((1, half), F32)
        lax.fori_loop(0, n_tiles, step, (z, z))

    spec = pl.BlockSpec((S, 2 * half), lambda j: (0, j))
    return pl.pallas_call(
        body, grid=(C // (2 * half),), in_specs=[pl.BlockSpec((1, 2 * half), lambda j: (0, j)), spec],
        out_specs=spec, out_shape=jax.ShapeDtypeStruct((S, C), F32), name=name,
        compiler_params=_params(("parallel",)),
    )(lam, bu)


def _attn_tile(S, window):
    return window if window is not None else _pick(S, (512, 256, 128))


def attn_fwd(q, k, v, sink, cq=None, ck=None, window=None, name="attn_fwd"):
    H, S, Dh = q.shape
    G = H // k.shape[0]
    T = _attn_tile(S, window)
    nq = S // T
    nks = nq if window is None else 2
    scale = Dh ** -0.5
    bias = cq is not None

    def kv_block(i, j):
        return jnp.minimum(j, i) if window is None else jnp.maximum(i - 1 + j, 0)

    def body(*refs):
        if bias:
            q_ref, k_ref, v_ref, s_ref, cq_ref, ck_ref, o_ref, lse_ref, m_scr, l_scr, acc_scr = refs
        else:
            q_ref, k_ref, v_ref, s_ref, o_ref, lse_ref, m_scr, l_scr, acc_scr = refs
        i, j = pl.program_id(1), pl.program_id(2)

        @pl.when(j == 0)
        def _():
            m_scr[...] = jnp.zeros(m_scr.shape, F32) + s_ref[0]
            l_scr[...] = jnp.ones(l_scr.shape, F32)
            acc_scr[...] = jnp.zeros(acc_scr.shape, F32)

        active = (j <= i) if window is None else (i - 1 + j >= 0)

        @pl.when(active)
        def _():
            kb = kv_block(i, j)
            s = lax.dot_general(q_ref[0].astype(MXU_DTYPE), k_ref[0].astype(MXU_DTYPE), _DOT_DIMS["nt"],
                                preferred_element_type=F32) * scale
            if bias:
                s = s + cq_ref[0] - ck_ref[0]
            qpos = i * T + lax.broadcasted_iota(jnp.int32, (T, T), 0)
            kpos = kb * T + lax.broadcasted_iota(jnp.int32, (T, T), 1)
            valid = kpos <= qpos
            if window is not None:
                valid = valid & (qpos - kpos < window)
            s = jnp.where(valid, s, NEG)
            m_old = m_scr[...]
            m_new = jnp.maximum(m_old, jnp.max(s, axis=-1, keepdims=True))
            alpha = jnp.exp(m_old - m_new)
            p = jnp.where(valid, jnp.exp(s - m_new), 0.0)
            l_scr[...] = alpha * l_scr[...] + jnp.sum(p, axis=-1, keepdims=True)
            acc_scr[...] = alpha * acc_scr[...] + jnp.dot(p.astype(MXU_DTYPE), v_ref[0].astype(MXU_DTYPE),
                                                          preferred_element_type=F32)
            m_scr[...] = m_new

        @pl.when(j == nks - 1)
        def _():
            o_ref[0] = acc_scr[...] / l_scr[...]
            lse_ref[0] = m_scr[...] + jnp.log(l_scr[...])

    in_specs = [
        pl.BlockSpec((1, T, Dh), lambda h, i, j: (h, i, 0)),
        pl.BlockSpec((1, T, Dh), lambda h, i, j: (h // G, kv_block(i, j), 0)),
        pl.BlockSpec((1, T, Dh), lambda h, i, j: (h // G, kv_block(i, j), 0)),
        pl.BlockSpec((1, 1, 1), lambda h, i, j: (h, 0, 0)),
    ]
    args = [q, k, v, sink]
    if bias:
        in_specs += [pl.BlockSpec((1, T, 1), lambda h, i, j: (h, i, 0)),
                     pl.BlockSpec((1, 1, T), lambda h, i, j: (h, 0, kv_block(i, j)))]
        args += [cq, ck]
    return pl.pallas_call(
        body, grid=(H, nq, nks), in_specs=in_specs,
        out_specs=[pl.BlockSpec((1, T, Dh), lambda h, i, j: (h, i, 0)),
                   pl.BlockSpec((1, T, 1), lambda h, i, j: (h, i, 0))],
        out_shape=[jax.ShapeDtypeStruct((H, S, Dh), F32), jax.ShapeDtypeStruct((H, S, 1), F32)],
        scratch_shapes=[pltpu.VMEM((T, 1), F32), pltpu.VMEM((T, 1), F32), pltpu.VMEM((T, Dh), F32)],
        name=name, compiler_params=_params(("parallel", "parallel", "arbitrary")),
    )(*args)


def attn_bwd(q, k, v, lse, do, delta, cq=None, ck=None, window=None, name="attn_bwd"):
    H, S, Dh = q.shape
    KVH = k.shape[0]
    G = H // KVH
    T = _attn_tile(S, window)
    nq = S // T
    nqs = nq if window is None else 2
    scale = Dh ** -0.5
    bias = cq is not None
    assert not bias or G == 1

    def q_block(kb, j):
        return jnp.maximum(j, kb) if window is None else jnp.minimum(kb + j, nq - 1)

    def body(*refs):
        if bias:
            (q_ref, k_ref, v_ref, lse_ref, do_ref, dl_ref, cq_ref, ck_ref,
             dq_ref, dk_ref, dv_ref, dcq_ref, dck_ref) = refs
        else:
            q_ref, k_ref, v_ref, lse_ref, do_ref, dl_ref, dq_ref, dk_ref, dv_ref = refs
        kb, g, j = pl.program_id(1), pl.program_id(2), pl.program_id(3)

        @pl.when((g == 0) & (j == 0))
        def _():
            dk_ref[...] = jnp.zeros(dk_ref.shape, F32)
            dv_ref[...] = jnp.zeros(dv_ref.shape, F32)
            if bias:
                dck_ref[...] = jnp.zeros(dck_ref.shape, F32)

        @pl.when((kb == 0) & (g == 0) & (j == 0))
        def _():
            dq_ref[...] = jnp.zeros(dq_ref.shape, F32)
            if bias:
                dcq_ref[...] = jnp.zeros(dcq_ref.shape, F32)

        active = (j >= kb) if window is None else (kb + j <= nq - 1)

        @pl.when(active)
        def _():
            qi = q_block(kb, j)
            off = pl.multiple_of(qi * T, T)
            qb, kk, vv = q_ref[0].astype(MXU_DTYPE), k_ref[0].astype(MXU_DTYPE), v_ref[0].astype(MXU_DTYPE)
            dob = do_ref[0].astype(MXU_DTYPE)
            s = lax.dot_general(qb, kk, _DOT_DIMS["nt"], preferred_element_type=F32) * scale
            if bias:
                s = s + cq_ref[0] - ck_ref[0]
            qpos = qi * T + lax.broadcasted_iota(jnp.int32, (T, T), 0)
            kpos = kb * T + lax.broadcasted_iota(jnp.int32, (T, T), 1)
            valid = kpos <= qpos
            if window is not None:
                valid = valid & (qpos - kpos < window)
            p = jnp.where(valid, jnp.exp(jnp.where(valid, s, NEG) - lse_ref[0]), 0.0)
            dv_ref[0] += lax.dot_general(p.astype(MXU_DTYPE), dob, _DOT_DIMS["tn"], preferred_element_type=F32)
            dp = lax.dot_general(dob, vv, _DOT_DIMS["nt"], preferred_element_type=F32)
            ds = p * (dp - dl_ref[0])
            dsb = ds.astype(MXU_DTYPE)
            dq_ref[0, g, pl.ds(off, T), :] += scale * jnp.dot(dsb, kk, preferred_element_type=F32)
            dk_ref[0] += scale * lax.dot_general(dsb, qb, _DOT_DIMS["tn"], preferred_element_type=F32)
            if bias:
                dcq_ref[0, g, pl.ds(off, T), :] += jnp.sum(ds, axis=1, keepdims=True)
                dck_ref[0] -= jnp.sum(ds, axis=0, keepdims=True)

    def qmap(kvh, kb, g, j):
        return (kvh * G + g, q_block(kb, j), 0)

    in_specs = [
        pl.BlockSpec((1, T, Dh), qmap),
        pl.BlockSpec((1, T, Dh), lambda kvh, kb, g, j: (kvh, kb, 0)),
        pl.BlockSpec((1, T, Dh), lambda kvh, kb, g, j: (kvh, kb, 0)),
        pl.BlockSpec((1, T, 1), qmap),
        pl.BlockSpec((1, T, Dh), qmap),
        pl.BlockSpec((1, T, 1), qmap),
    ]
    args = [q, k, v, lse, do, delta]
    out_specs = [
        pl.BlockSpec((1, G, S, Dh), lambda kvh, kb, g, j: (kvh, 0, 0, 0)),
        pl.BlockSpec((1, T, Dh), lambda kvh, kb, g, j: (kvh, kb, 0)),
        pl.BlockSpec((1, T, Dh), lambda kvh, kb, g, j: (kvh, kb, 0)),
    ]
    out_shape = [jax.ShapeDtypeStruct((KVH, G, S, Dh), F32), jax.ShapeDtypeStruct((KVH, S, Dh), F32),
                 jax.ShapeDtypeStruct((KVH, S, Dh), F32)]
    if bias:
        in_specs += [pl.BlockSpec((1, T, 1), qmap),
                     pl.BlockSpec((1, 1, T), lambda kvh, kb, g, j: (kvh, 0, kb))]
        args += [cq, ck]
        out_specs += [pl.BlockSpec((1, G, S, 1), lambda kvh, kb, g, j: (kvh, 0, 0, 0)),
                      pl.BlockSpec((1, 1, T), lambda kvh, kb, g, j: (kvh, 0, kb))]
        out_shape += [jax.ShapeDtypeStruct((KVH, G, S, 1), F32), jax.ShapeDtypeStruct((KVH, 1, S), F32)]
    res = pl.pallas_call(
        body, grid=(KVH, nq, G, nqs), in_specs=in_specs, out_specs=out_specs, out_shape=out_shape,
        name=name, compiler_params=_params(("arbitrary", "arbitrary", "arbitrary", "arbitrary")),
    )(*args)
    dq = res[0].reshape(H, S, Dh)
    if bias:
        return dq, res[1], res[2], res[3].reshape(H, S, 1), res[4]
    return dq, res[1], res[2]


def _rms(x, g):
    return x * lax.rsqrt(jnp.mean(x * x, axis=-1, keepdims=True) + EPS) * g


def _sigmoid(x):
    return 1.0 / (1.0 + jnp.exp(-x))


def _softplus(x):
    return jnp.maximum(x, 0.0) + jnp.log(1.0 + jnp.exp(-jnp.abs(x)))


def _log_sigmoid(x):
    return jnp.minimum(x, 0.0) - jnp.log(1.0 + jnp.exp(-jnp.abs(x)))


def _gelu(x):
    return 0.5 * x * (1.0 + jnp.tanh(math.sqrt(2.0 / math.pi) * (x + 0.044715 * (x * x * x))))


def _silu(x):
    return x * _sigmoid(x)


def _ffn_act(gu):
    f = gu.shape[1] // 2
    return MACARON * _silu(gu[:, :f]) * gu[:, f:]


def _qk_prep(rope):
    def f(x, *rest):
        if rope:
            cos, sin, g, rot = rest
        else:
            (g,) = rest
        y = _rms(x, g)
        if rope:
            y = y * cos + jnp.dot(y, rot, precision=HI, preferred_element_type=F32) * sin
        return y
    return f


def _lru_gates(pre, xc, ba, bx, lam):
    w = xc.shape[1]
    r = _sigmoid(pre[:, :w] + ba)
    i = _sigmoid(pre[:, w:] + bx)
    log_a = -LRU_C * r * _softplus(lam)
    a = jnp.exp(log_a)
    b = jnp.sqrt(1.0 - jnp.exp(2.0 * log_a)) * (i * xc)
    return a, b


def _lru_conv(x0, x1, x2, x3, w0, w1, w2, w3, cb):
    return cb + x0 * w0 + x1 * w1 + x2 * w2 + x3 * w3


def _s5_params(lre, lim, ldt, gsel, bre, bim):
    dt = jnp.sum(gsel * jnp.exp(ldt), axis=1, keepdims=True)
    er = jnp.exp(lre * dt)
    ang = lim * dt
    lbr, lbi = er * jnp.cos(ang), er * jnp.sin(ang)
    nr, ni = lbr - 1.0, lbi
    den = lre * lre + lim * lim
    fr, fi = (nr * lre + ni * lim) / den, (ni * lre - nr * lim) / den
    return lbr, lbi, fr * bre - fi * bim, fr * bim + fi * bre


def _s5_out(yssm, u, d):
    return _gelu(yssm + d * u)


def _glu(z, gl, gb):
    return z * _sigmoid(gl + gb)


def _ple_out(x, gpre, epre, pn):
    return x + _sigmoid(gpre) * _rms(epre, pn)


def _vjp(fn, args, cots):
    _, pull = jax.vjp(fn, *args)
    return pull(cots)


def add_norm(x, y, g, name):
    D = x.shape[1]
    if y is None:
        return x, rowwise(lambda xv, gv: ([_rms(xv, gv)], []), [x], [g], outs=[(D, BF16)], name=name)[0]
    xn, n = rowwise(lambda xv, yv, gv: ([xv + yv, _rms(xv + yv, gv)], []), [x, y], [g],
                    outs=[(D, F32), (D, BF16)], name=name)
    return xn, n


def norm_bwd(x, g, dn, dx_res, name):
    D = x.shape[1]

    def f(xv, dnv, dxv, gv):
        dx, dg = _vjp(_rms, (xv, gv), dnv)
        return [dxv + dx], [dg]

    return rowwise(f, [x, dn, dx_res], [g], outs=[(D, F32)], accs=[(1, D)], name=name)


def _swiglu(g, u):
    return MACARON * _silu(g) * u


def _dotf(a, b, mode="nn"):
    return lax.dot_general(a.astype(MXU_DTYPE), b.astype(MXU_DTYPE), _DOT_DIMS[mode], preferred_element_type=F32)


def ffn_up(n, wgu, ig, iu, name):
    S, D = n.shape
    C, _, _, Fc = wgu.shape
    tm = _tile(S, MM_TILE_M, 2 * SUBLANES)

    def body(n_ref, wg_ref, wu_ref, g_ref, u_ref, a_ref):
        g = _dotf(n_ref[...], wg_ref[...])
        u = _dotf(n_ref[...], wu_ref[...])
        g_ref[...] = g.astype(g_ref.dtype)
        u_ref[...] = u.astype(u_ref.dtype)
        a_ref[...] = _swiglu(g, u).astype(a_ref.dtype)

    hid = pl.BlockSpec((None, tm, Fc), lambda s, i: (s, i, 0))
    return pl.pallas_call(
        body, grid=(C, S // tm),
        in_specs=[pl.BlockSpec((tm, D), lambda s, i: (i, 0)),
                  pl.BlockSpec((None, None, D, Fc), lambda s, i: (s, ig, 0, 0)),
                  pl.BlockSpec((None, None, D, Fc), lambda s, i: (s, iu, 0, 0))],
        out_specs=[hid, hid, hid], out_shape=[jax.ShapeDtypeStruct((C, S, Fc), BF16)] * 3,
        name=name, compiler_params=_params(("parallel", "parallel")),
    )(n, wgu, wgu)


def ffn_down(act, wd, iw, name):
    C, S, Fc = act.shape
    D = wd.shape[-1]
    tm = _tile(S, MM_TILE_M, 2 * SUBLANES)

    def body(a_ref, w_ref, o_ref):
        s = pl.program_id(1)
        r = _dotf(a_ref[...], w_ref[...])

        @pl.when(s == 0)
        def _():
            o_ref[...] = r

        @pl.when(s > 0)
        def _():
            o_ref[...] += r

    return pl.pallas_call(
        body, grid=(S // tm, C),
        in_specs=[pl.BlockSpec((None, tm, Fc), lambda i, s: (s, i, 0)),
                  pl.BlockSpec((None, None, Fc, D), lambda i, s: (s, iw, 0, 0))],
        out_specs=pl.BlockSpec((tm, D), lambda i, s: (i, 0)), out_shape=jax.ShapeDtypeStruct((S, D), F32),
        name=name, compiler_params=_params(("parallel", "arbitrary")),
    )(act, wd)


def ffn_down_bwd(dy, wd, iw, g, u, name):
    C, S, Fc = g.shape
    D = dy.shape[1]
    tm = _tile(S, MM_TILE_M, 2 * SUBLANES)

    def body(dy_ref, w_ref, g_ref, u_ref, dg_ref, du_ref):
        dact = _dotf(dy_ref[...], w_ref[...], "nt")
        dg, du = _vjp(_swiglu, (g_ref[...].astype(F32), u_ref[...].astype(F32)), dact)
        dg_ref[...] = dg.astype(dg_ref.dtype)
        du_ref[...] = du.astype(du_ref.dtype)

    hid = pl.BlockSpec((None, tm, Fc), lambda s, i: (s, i, 0))
    return pl.pallas_call(
        body, grid=(C, S // tm),
        in_specs=[pl.BlockSpec((tm, D), lambda s, i: (i, 0)),
                  pl.BlockSpec((None, None, Fc, D), lambda s, i: (s, iw, 0, 0)), hid, hid],
        out_specs=[hid, hid], out_shape=[jax.ShapeDtypeStruct((C, S, Fc), BF16)] * 2,
        name=name, compiler_params=_params(("parallel", "parallel")),
    )(dy, wd, g, u)


def ffn_dn(dg, du, wgu, ig, iu, name):
    C, S, Fc = dg.shape
    D = wgu.shape[2]
    tm = _tile(S, MM_TILE_M, 2 * SUBLANES)

    def body(dg_ref, du_ref, wg_ref, wu_ref, o_ref):
        s = pl.program_id(1)
        r = _dotf(dg_ref[...], wg_ref[...], "nt") + _dotf(du_ref[...], wu_ref[...], "nt")

        @pl.when(s == 0)
        def _():
            o_ref[...] = r

        @pl.when(s > 0)
        def _():
            o_ref[...] += r

    hid = pl.BlockSpec((None, tm, Fc), lambda i, s: (s, i, 0))
    return pl.pallas_call(
        body, grid=(S // tm, C),
        in_specs=[hid, hid, pl.BlockSpec((None, None, D, Fc), lambda i, s: (s, ig, 0, 0)),
                  pl.BlockSpec((None, None, D, Fc), lambda i, s: (s, iu, 0, 0))],
        out_specs=pl.BlockSpec((tm, D), lambda i, s: (i, 0)), out_shape=jax.ShapeDtypeStruct((S, D), F32),
        name=name, compiler_params=_params(("parallel", "arbitrary")),
    )(dg, du, wgu, wgu)


def ffn_dw(a, d, buf, idx, shape, blocked, name):
    C, P, M, N = shape
    S = d.shape[-2]
    tk = _tile(S, MM_TILE_M, 2 * SUBLANES)

    def body(*refs):
        a_ref, d_ref, o_ref = refs[0], refs[1], refs[-1]
        k = pl.program_id(1)
        r = _dotf(a_ref[...], d_ref[...], "tn")

        @pl.when(k == 0)
        def _():
            o_ref[...] = r

        @pl.when(k > 0)
        def _():
            o_ref[...] += r

    if blocked == "a":
        a_spec = pl.BlockSpec((None, tk, M), lambda s, k: (s, k, 0))
        d_spec = pl.BlockSpec((tk, N), lambda s, k: (k, 0))
    else:
        a_spec = pl.BlockSpec((tk, M), lambda s, k: (k, 0))
        d_spec = pl.BlockSpec((None, tk, N), lambda s, k: (s, k, 0))
    out_spec = pl.BlockSpec((None, None, M, N), lambda s, k: (s, idx, 0, 0))
    out_shape = jax.ShapeDtypeStruct(tuple(shape), F32)
    if buf is None:
        return pl.pallas_call(body, grid=(C, S // tk), in_specs=[a_spec, d_spec], out_specs=out_spec,
                              out_shape=out_shape, name=name, compiler_params=_params(("parallel", "arbitrary")))(a, d)
    return pl.pallas_call(body, grid=(C, S // tk), in_specs=[a_spec, d_spec, pl.BlockSpec(memory_space=pl.ANY)],
                          out_specs=out_spec, out_shape=out_shape, input_output_aliases={2: 0}, name=name,
                          compiler_params=_params(("parallel", "arbitrary")))(a, d, buf)


def ffn_fwd(n, wgu, wd, ig, iu, iw, tag):
    g, u, act = ffn_up(n, wgu, ig, iu, f"ffn_up_{tag}")
    return ffn_down(act, wd, iw, f"ffn_down_{tag}"), (n, g, u, act)


def ffn_bwd(dy, saved, wgu, wd, ig, iu, iw, gbuf, tag):
    n, g, u, act = saved
    gwgu, gwd = gbuf
    dg, du = ffn_down_bwd(dy, wd, iw, g, u, f"ffn_down_bwd_{tag}")
    gwd = ffn_dw(act, dy, gwd, iw, (N_CHIPS,) + wd.shape[1:], "a", f"ffn_dwd_{tag}")
    dn = ffn_dn(dg, du, wgu, ig, iu, f"ffn_dn_{tag}")
    gwgu = ffn_dw(n, dg, gwgu, ig, (N_CHIPS,) + wgu.shape[1:], "d", f"ffn_dwg_{tag}")
    gwgu = ffn_dw(n, du, gwgu, iu, (N_CHIPS,) + wgu.shape[1:], "d", f"ffn_dwu_{tag}")
    return dn, (gwgu, gwd)


def _heads(x, H):
    S = x.shape[0]
    return x.reshape(S, H, HEAD_DIM).transpose(1, 0, 2)


def _unheads(x):
    H, S, _ = x.shape
    return x.transpose(1, 0, 2).reshape(S, H * HEAD_DIM)


def _shift_down(x, n=1):
    return jnp.pad(x, ((n, 0), (0, 0)))[:x.shape[0]]


def _shift_up(x, n=1):
    return jnp.pad(x, ((0, n), (0, 0)))[n:]


def _block_diag(w):
    B, I, J = w.shape
    eye = jnp.eye(B, dtype=w.dtype)
    return (w[:, :, None, :] * eye[:, None, :, None]).reshape(B * I, B * J)


def _block_diag_take(x, B):
    I, J = x.shape[0] // B, x.shape[1] // B
    eye = jnp.eye(B, dtype=x.dtype)
    return jnp.sum(x.reshape(B, I, B, J) * eye[:, None, :, None], axis=2)


def _rope_tables(S):
    half = HEAD_DIM // 2
    inv = jnp.power(ROPE_THETA, -jnp.arange(half, dtype=F32) / half)
    ang = jnp.arange(S, dtype=F32)[:, None] * inv[None, :]
    cos = jnp.concatenate([jnp.cos(ang), jnp.cos(ang)], axis=1)
    sin = jnp.concatenate([jnp.sin(ang), jnp.sin(ang)], axis=1)
    r = jnp.arange(HEAD_DIM)[:, None]
    c = jnp.arange(HEAD_DIM)[None, :]
    rot = jnp.where(r == c + half, -1.0, 0.0) + jnp.where(c == r + half, 1.0, 0.0)
    return cos, sin, rot.astype(F32)


def qk_prep_fwd(x_hm, g, rope_tabs, name):
    H, S, Dh = x_hm.shape
    rows = [x_hm.reshape(H * S, Dh)]
    consts = [g.reshape(1, Dh)]
    periods = [None]
    if rope_tabs is not None:
        rows += [rope_tabs[0], rope_tabs[1]]
        consts += [rope_tabs[2]]
        periods += [S, S]
    fn = _qk_prep(rope_tabs is not None)
    y = rowwise(lambda *a: ([fn(*a)], []), rows, consts, outs=[(Dh, F32)], name=name, periods=periods)[0]
    return y.reshape(H, S, Dh)


def qk_prep_bwd(x_hm, g, rope_tabs, dy_hm, name):
    H, S, Dh = x_hm.shape
    rope = rope_tabs is not None
    rows = [x_hm.reshape(H * S, Dh), dy_hm.reshape(H * S, Dh)]
    consts = [g.reshape(1, Dh)]
    periods = [None, None]
    if rope:
        rows += [rope_tabs[0], rope_tabs[1]]
        consts += [rope_tabs[2]]
        periods += [S, S]
    fn = _qk_prep(rope)

    def f(xv, dyv, *rest):
        if rope:
            cos, sin, gv, rot = rest
            dx, dg = _vjp(lambda a, b: fn(a, cos, sin, b, rot), (xv, gv), dyv)
        else:
            (gv,) = rest
            dx, dg = _vjp(fn, (xv, gv), dyv)
        return [dx], [dg]

    dx, dg = rowwise(f, rows, consts, outs=[(Dh, F32)], accs=[(1, Dh)], name=name, periods=periods)
    return dx.reshape(H, S, Dh), dg.reshape(Dh)


def attn_delta(do_hm, o_hm, name):
    H, S, Dh = o_hm.shape
    d = rowwise(lambda a, b: ([jnp.sum(a * b, axis=-1, keepdims=True)], []),
                [do_hm.reshape(H * S, Dh), o_hm.reshape(H * S, Dh)], outs=[(1, F32)], name=name)[0]
    return d.reshape(H, S, 1)


def even_mixer_fwd(h, w, tag):
    S = h.shape[0]
    W = 512
    H = 8
    z = mm(h, w["w_in"], name=f"ev_in_{tag}")
    xa, ya, q, k, v, f = (z[:, 0:512], z[:, 512:1024], z[:, 1024:1536], z[:, 1536:2048], z[:, 2048:2560],
                          z[:, 2560:2688])
    xs = [_shift_down(xa, LRU_CONV - 1 - tap) for tap in range(LRU_CONV)]
    taps = [w["conv_w"][tap][None] for tap in range(LRU_CONV)]
    xc = rowwise(lambda *a: ([_lru_conv(*a)], []), xs, taps + [w["conv_b"]], outs=[(W, F32)],
                 name=f"lru_conv_{tag}")[0]
    pre = mm(xc, w["w_ax"], name=f"lru_gates_mm_{tag}")
    a, b = rowwise(lambda p_, x_, ba, bx, lam: (list(_lru_gates(p_, x_, ba, bx, lam)), []), [pre, xc],
                   [w["ba"], w["bx"], w["lam"]], outs=[(W, F32), (W, F32)], name=f"lru_gates_{tag}")
    hs = scan_real(a, b, name=f"lru_scan_{tag}")
    a_out = rowwise(lambda y_, h_: ([_gelu(y_) * h_], []), [ya, hs], outs=[(W, F32)], name=f"lru_out_{tag}")[0]
    lf = rowwise(lambda f_, bf: ([_log_sigmoid(f_ + bf)], []), [f], [w["bf"]], outs=[(LANES, F32)],
                 name=f"fox_logf_{tag}")[0]
    c = scan_real(jnp.ones_like(lf), lf, name=f"fox_cumsum_{tag}")
    c_hm = c[:, :H].T
    q_hm, k_hm, v_hm = _heads(q, H), _heads(k, H), _heads(v, H)
    qn = qk_prep_fwd(q_hm, w["qn"], None, f"fox_qprep_{tag}")
    kn = qk_prep_fwd(k_hm, w["kn"], None, f"fox_kprep_{tag}")
    sink = jnp.full((H, 1, 1), NEG, F32)
    o_hm, lse = attn_fwd(qn, kn, v_hm, sink, c_hm[:, :, None], c_hm[:, None, :], name=f"fox_attn_{tag}")
    mo = jnp.concatenate([a_out, _unheads(o_hm)], axis=1).astype(BF16)
    y = mm(mo, w["w_out"], name=f"ev_out_{tag}")
    saved = dict(h=h, xs=xs, xc=xc, pre=pre, a=a, hs=hs, ya=ya, f=f, c_hm=c_hm, q_hm=q_hm, k_hm=k_hm,
                 v_hm=v_hm, qn=qn, kn=kn, o_hm=o_hm, lse=lse, mo=mo)
    return y, saved


def even_mixer_bwd(dy, sv, w, tag):
    W = 512
    H = 8
    S = dy.shape[0]
    g = {}
    dmo = mm(dy, w["w_out"], "nt", name=f"ev_dmo_{tag}")
    g["w_out"] = mm(sv["mo"], dy, "tn", name=f"ev_dwout_{tag}")
    da_out, do = dmo[:, :W], dmo[:, W:]
    do_hm = _heads(do, H)
    delta = attn_delta(do_hm, sv["o_hm"], f"fox_delta_{tag}")
    c_hm = sv["c_hm"]
    dqn, dkn, dv_hm, dcq, dck = attn_bwd(sv["qn"], sv["kn"], sv["v_hm"], sv["lse"], do_hm, delta,
                                          c_hm[:, :, None], c_hm[:, None, :], name=f"fox_attn_bwd_{tag}")
    dq_hm, g["qn"] = qk_prep_bwd(sv["q_hm"], w["qn"], None, dqn, f"fox_qprep_bwd_{tag}")
    dk_hm, g["kn"] = qk_prep_bwd(sv["k_hm"], w["kn"], None, dkn, f"fox_kprep_bwd_{tag}")
    dc = (dcq[:, :, 0] + dck[:, 0, :]).T
    dc = jnp.pad(dc, ((0, 0), (0, LANES - H)))
    dlf = scan_real(jnp.ones_like(dc), dc, reverse=True, name=f"fox_cumsum_bwd_{tag}")

    def f_logf(f_, d_, bf):
        df, dbf = _vjp(lambda a_, b_: _log_sigmoid(a_ + b_), (f_, bf), d_)
        return [df], [dbf]

    df, dbf = rowwise(f_logf, [sv["f"], dlf], [w["bf"]], outs=[(LANES, F32)], accs=[(1, LANES)],
                      name=f"fox_logf_bwd_{tag}")
    g["bf"] = dbf[0, :H]
    def f_out(y_, h_, d_):
        dyv, dhv = _vjp(lambda a_, b_: _gelu(a_) * b_, (y_, h_), d_)
        return [dyv, dhv], []

    dya, dhs = rowwise(f_out, [sv["ya"], sv["hs"], da_out], outs=[(W, F32), (W, F32)], name=f"lru_out_bwd_{tag}")
    gs = scan_real(_shift_up(sv["a"]), dhs, reverse=True, name=f"lru_scan_bwd_{tag}")

    def f_gates(p_, x_, g_, hp_, ba, bx, lam):
        dp, dx, dba, dbx, dlam = _vjp(_lru_gates, (p_, x_, ba, bx, lam), (g_ * hp_, g_))
        return [dp, dx], [dba, dbx, dlam]

    dpre, dxc, dba, dbx, dlam = rowwise(f_gates, [sv["pre"], sv["xc"], gs, _shift_down(sv["hs"])],
                                        [w["ba"], w["bx"], w["lam"]], outs=[(2 * W, BF16), (W, F32)],
                                        accs=[(1, W)] * 3, name=f"lru_gates_bwd_{tag}")
    g["ba"], g["bx"], g["lam"] = dba[0], dbx[0], dlam[0]
    dxc2 = mm(dpre, w["w_ax"], "nt", name=f"lru_gates_mm_dx_{tag}")
    g["w_ax"] = mm(sv["xc"], dpre, "tn", name=f"lru_gates_mm_dw_{tag}")

    def f_conv(d1, d2, x0, x1, x2, x3):
        d = d1 + d2
        return [d], [jnp.sum(d, axis=0, keepdims=True)] + [jnp.sum(d * xv, axis=0, keepdims=True)
                                                           for xv in (x0, x1, x2, x3)]

    dxc_t, dcb, dw0, dw1, dw2, dw3 = rowwise(f_conv, [dxc, dxc2] + sv["xs"], outs=[(W, F32)],
                                             accs=[(1, W)] * 5, name=f"lru_conv_bwd_{tag}")
    g["conv_b"] = dcb[0]
    g["conv_w"] = jnp.concatenate([dw0, dw1, dw2, dw3], axis=0)
    ds_ = [_shift_up(dxc_t, LRU_CONV - 1 - tap) for tap in range(LRU_CONV)]
    taps = [w["conv_w"][tap][None] for tap in range(LRU_CONV)]
    dxa = rowwise(lambda a, b, c, d, w0, w1, w2, w3: ([a * w0 + b * w1 + c * w2 + d * w3], []), ds_, taps,
                  outs=[(W, F32)], name=f"lru_conv_dx_{tag}")[0]
    dz = jnp.concatenate([dxa, dya, _unheads(dq_hm), _unheads(dk_hm), _unheads(dv_hm), df], axis=1).astype(BF16)
    g["w_in"] = mm(sv["h"], dz, "tn", name=f"ev_dwin_{tag}")
    dh = mm(dz, w["w_in"], "nt", name=f"ev_dh_{tag}")
    return dh, g


def odd_mixer_fwd(h, w, tag):
    S = h.shape[0]
    H, KVH = 8, 2
    z = mm(h, w["w_in"], name=f"od_in_{tag}")
    q, k, v, u = z[:, 0:512], z[:, 512:640], z[:, 640:768], z[:, 768:1280]
    tabs = _rope_tables(S)
    q_hm, k_hm, v_hm = _heads(q, H), _heads(k, KVH), _heads(v, KVH)
    qn = qk_prep_fwd(q_hm, w["qn"], tabs, f"swa_qprep_{tag}")
    kn = qk_prep_fwd(k_hm, w["kn"], tabs, f"swa_kprep_{tag}")
    sink = w["sinks"].reshape(H, 1, 1)
    o_hm, lse = attn_fwd(qn, kn, v_hm, sink, window=SWA_WINDOW, name=f"swa_attn_{tag}")
    lam, bexp = w["s5_lam"], w["s5_bexp"]
    bu = mm(u, bexp, name=f"s5_bu_{tag}")
    hs = scan_cplx(lam, bu, name=f"s5_scan_{tag}")
    yssm = mm(hs, w["s5_cexp"], name=f"s5_y_{tag}")
    zz = rowwise(lambda y_, u_, d_: ([_s5_out(y_, u_, d_)], []), [yssm, u], [w["s5_d"]], outs=[(512, F32)],
                 name=f"s5_gelu_{tag}")[0]
    gl = mm(zz, w["glu_w"], name=f"s5_glu_mm_{tag}")
    d_out = rowwise(lambda z_, g_, b_: ([_glu(z_, g_, b_)], []), [zz, gl], [w["glu_b"]], outs=[(512, F32)],
                    name=f"s5_glu_{tag}")[0]
    mo = jnp.concatenate([_unheads(o_hm), d_out], axis=1).astype(BF16)
    y = mm(mo, w["w_out"], name=f"od_out_{tag}")
    saved = dict(h=h, q_hm=q_hm, k_hm=k_hm, v_hm=v_hm, qn=qn, kn=kn, o_hm=o_hm, lse=lse, u=u, hs=hs, yssm=yssm,
                 zz=zz, gl=gl, mo=mo, tabs=tabs)
    return y, saved


def odd_mixer_bwd(dy, sv, w, tag):
    H, KVH = 8, 2
    g = {}
    dmo = mm(dy, w["w_out"], "nt", name=f"od_dmo_{tag}")
    g["w_out"] = mm(sv["mo"], dy, "tn", name=f"od_dwout_{tag}")
    do, dd = dmo[:, :512], dmo[:, 512:]
    do_hm = _heads(do, H)
    delta = attn_delta(do_hm, sv["o_hm"], f"swa_delta_{tag}")
    dqn, dkn, dv_hm = attn_bwd(sv["qn"], sv["kn"], sv["v_hm"], sv["lse"], do_hm, delta, window=SWA_WINDOW,
                               name=f"swa_attn_bwd_{tag}")
    dq_hm, g["qn"] = qk_prep_bwd(sv["q_hm"], w["qn"], sv["tabs"], dqn, f"swa_qprep_bwd_{tag}")
    dk_hm, g["kn"] = qk_prep_bwd(sv["k_hm"], w["kn"], sv["tabs"], dkn, f"swa_kprep_bwd_{tag}")
    lse_t, delta_t = sv["lse"][:, :, 0].T, delta[:, :, 0].T
    g["sinks"] = rowwise(lambda l_, d_, s_: ([], [jnp.sum(-jnp.exp(s_ - l_) * d_, axis=0, keepdims=True)]),
                         [lse_t, delta_t], [w["sinks"].reshape(1, H)], accs=[(1, H)], name=f"swa_dsink_{tag}")[0][0]
    def f_glu(z_, g_, d_, b_):
        dz_, dg_, db_ = _vjp(_glu, (z_, g_, b_), d_)
        return [dz_, dg_], [db_]

    dzz1, dgl, dglb = rowwise(f_glu, [sv["zz"], sv["gl"], dd], [w["glu_b"]], outs=[(512, F32), (512, BF16)],
                              accs=[(1, 512)], name=f"s5_glu_bwd_{tag}")
    g["glu_b"] = dglb[0]
    g["glu_w"] = mm(sv["zz"], dgl, "tn", name=f"s5_glu_dw_{tag}")
    dzz2 = mm(dgl, w["glu_w"], "nt", name=f"s5_glu_dz_{tag}")

    def f_gelu(y_, u_, d1, d2, dpar):
        dy_, du_, dd_ = _vjp(_s5_out, (y_, u_, dpar), d1 + d2)
        return [dy_, du_], [dd_]

    dyssm, du1, dsd = rowwise(f_gelu, [sv["yssm"], sv["u"], dzz1, dzz2], [w["s5_d"]],
                              outs=[(512, F32), (512, F32)], accs=[(1, 512)], name=f"s5_gelu_bwd_{tag}")
    g["s5_d"] = dsd[0]
    dhs = mm(dyssm, w["s5_cexp"], "nt", name=f"s5_dh_{tag}")
    g["s5_cexp"] = mm(sv["hs"], dyssm, "tn", name=f"s5_dc_{tag}")
    gs = scan_cplx(w["s5_lam"], dhs, reverse=True, name=f"s5_scan_bwd_{tag}")
    g["s5_bexp"] = mm(sv["u"], gs, "tn", name=f"s5_db_{tag}")
    du2 = mm(gs, w["s5_bexp"], "nt", name=f"s5_du_{tag}")

    def f_dlam(g_, hp_):
        C = g_.shape[1]
        outs_r, outs_i = [], []
        for j in range(C // (2 * LANES)):
            gr, gi = g_[:, 2 * LANES * j:2 * LANES * j + LANES], g_[:, 2 * LANES * j + LANES:2 * LANES * (j + 1)]
            hr, hi = hp_[:, 2 * LANES * j:2 * LANES * j + LANES], hp_[:, 2 * LANES * j + LANES:2 * LANES * (j + 1)]
            outs_r.append(jnp.sum(gr * hr + gi * hi, axis=0, keepdims=True))
            outs_i.append(jnp.sum(gi * hr - gr * hi, axis=0, keepdims=True))
        return [], [jnp.concatenate([x for pair in zip(outs_r, outs_i) for x in pair], axis=1)]

    g["s5_lam"] = rowwise(f_dlam, [gs, _shift_down(sv["hs"])], accs=[(1, gs.shape[1])], name=f"s5_dlam_{tag}")[0]
    du = rowwise(lambda a_, b_: ([a_ + b_], []), [du1, du2], outs=[(512, F32)], name=f"s5_du_add_{tag}")[0]
    dz = jnp.concatenate([_unheads(dq_hm), _unheads(dk_hm), _unheads(dv_hm), du], axis=1).astype(BF16)
    g["w_in"] = mm(sv["h"], dz, "tn", name=f"od_dwin_{tag}")
    dh = mm(dz, w["w_in"], "nt", name=f"od_dh_{tag}")
    return dh, g


def _s5_cols(x_re, x_im):
    n = x_re.shape[0] // LANES
    return jnp.stack([x_re.reshape(n, LANES), x_im.reshape(n, LANES)], axis=1).reshape(1, 2 * n * LANES)


def _s5_uncols(x):
    n = x.shape[1] // (2 * LANES)
    y = x.reshape(n, 2, LANES)
    return y[:, 0].reshape(-1), y[:, 1].reshape(-1)


def _s5_gsel():
    return jnp.repeat(jnp.eye(S5_GROUPS, dtype=F32), S5_STATE, axis=0)


def s5_prep_fwd(lre, lim, ldt, bre, bim, cre, cim, tag):
    GP = S5_GROUPS * S5_STATE
    ins = [lre.reshape(GP, 1), lim.reshape(GP, 1), ldt.reshape(1, S5_GROUPS), _s5_gsel(),
           bre.reshape(GP, S5_GROUP), bim.reshape(GP, S5_GROUP)]
    lbr, lbi, bbr, bbi = whole(_s5_params, ins, [((GP, 1), F32)] * 2 + [((GP, S5_GROUP), F32)] * 2,
                               name=f"s5_params_{tag}")
    lam = _s5_cols(lbr[:, 0], lbi[:, 0])

    def expand_b(bb):
        return _block_diag(bb.reshape(S5_GROUPS, S5_STATE, S5_GROUP).transpose(0, 2, 1))

    n = GP // LANES
    bexp = jnp.stack([expand_b(bbr).reshape(-1, n, LANES), expand_b(bbi).reshape(-1, n, LANES)],
                     axis=2).reshape(-1, 2 * GP)
    c_r = _block_diag(cre.transpose(0, 2, 1))
    c_i = _block_diag(cim.transpose(0, 2, 1))
    cexp = jnp.stack([c_r.reshape(n, LANES, -1), -c_i.reshape(n, LANES, -1)], axis=1).reshape(2 * GP, -1)
    return lam, bexp.astype(BF16), cexp.astype(BF16), ins


def s5_prep_bwd(ins, dlam, dbexp, dcexp, tag):
    GP = S5_GROUPS * S5_STATE
    n = GP // LANES
    dlr, dli = _s5_uncols(dlam)
    db = dbexp.reshape(-1, n, 2, LANES)

    def take_b(x):
        return _block_diag_take(x, S5_GROUPS).transpose(0, 2, 1).reshape(GP, S5_GROUP)

    dbbr, dbbi = take_b(db[:, :, 0].reshape(-1, GP)), take_b(db[:, :, 1].reshape(-1, GP))
    dc = dcexp.reshape(n, 2, LANES, -1)
    dcre = _block_diag_take(dc[:, 0].reshape(GP, -1), S5_GROUPS).transpose(0, 2, 1)
    dcim = -_block_diag_take(dc[:, 1].reshape(GP, -1), S5_GROUPS).transpose(0, 2, 1)

    def f(lre, lim, ldt, gsel, bre, bim, c1, c2, c3, c4):
        d = _vjp(lambda a, b, c, e, f_: _s5_params(a, b, c, gsel, e, f_), (lre, lim, ldt, bre, bim), (c1, c2, c3, c4))
        return d

    outs = [((GP, 1), F32)] * 2 + [((1, S5_GROUPS), F32)] + [((GP, S5_GROUP), F32)] * 2
    dlre, dlim, dldt, dbre, dbim = whole(f, ins + [dlr.reshape(GP, 1), dli.reshape(GP, 1), dbbr, dbbi], outs,
                                          name=f"s5_params_bwd_{tag}")
    shp = (S5_GROUPS, S5_STATE)
    return dict(lre=dlre.reshape(shp), lim=dlim.reshape(shp), ldt=dldt.reshape(S5_GROUPS),
                bre=dbre.reshape(S5_GROUPS, S5_STATE, S5_GROUP), bim=dbim.reshape(S5_GROUPS, S5_STATE, S5_GROUP),
                cre=dcre, cim=dcim)


def _place():
    return lax.axis_index("x"), lax.axis_index("y"), lax.axis_index("c")


def _other_chips(x, y):
    return [(1 - x, y), (x, 1 - y), (1 - x, 1 - y)]


def _half(ref, h):
    n = ref.shape[0] // 2
    return ref.at[pl.ds(h * n, n)]


def _hbm_specs(n):
    return [pl.BlockSpec(memory_space=pl.ANY)] * n


def gather_chips(ws):
    n = len(ws)

    def body(*refs):
        w_refs, out_refs, (send_sems, recv_sems) = refs[:n], refs[n:2 * n], refs[2 * n:]
        x, y, c = _place()
        me, sibling = (x, y, c), (x, y, 1 - c)
        chips = _other_chips(x, y)
        mine = 2 * x + y

        def copy(k, src, dst, to):
            return pltpu.make_async_remote_copy(src_ref=src, dst_ref=dst, send_sem=send_sems.at[k],
                                                recv_sem=recv_sems.at[k], device_id=to, device_id_type=MESH)

        first, passed = [], []
        for p in range(n):
            for j, chip in enumerate(chips):
                first.append(copy(6 * p + j, _half(w_refs[p], c), _half(out_refs[p].at[mine], c), (*chip, c)))
                first[-1].start()
        for p in range(n):
            for j, chip in enumerate(chips):
                block = out_refs[p].at[2 * chip[0] + chip[1]]
                copy(6 * p + j, _half(w_refs[p], c), _half(block, c), me).wait_recv()
                passed.append(copy(6 * p + 3 + j, _half(block, c), _half(block, c), sibling))
                passed[-1].start()
        for p in range(n):
            for j, chip in enumerate(chips):
                block = out_refs[p].at[2 * chip[0] + chip[1]]
                copy(6 * p + 3 + j, _half(w_refs[p], c), _half(block, 1 - c), me).wait_recv()
        for cp in first + passed:
            cp.wait_send()

    return pl.pallas_call(
        body, out_shape=[jax.ShapeDtypeStruct((N_CHIPS,) + w.shape, w.dtype) for w in ws],
        in_specs=_hbm_specs(n), out_specs=_hbm_specs(n),
        scratch_shapes=[pltpu.SemaphoreType.DMA((6 * n,)), pltpu.SemaphoreType.DMA((6 * n,))],
        name="gather_chips",
    )(*ws)


def sibling_halves(gs):
    n = len(gs)

    def body(*refs):
        g_refs, out_refs, (send_sems, recv_sems) = refs[:n], refs[n:2 * n], refs[2 * n:]
        x, y, c = _place()
        me, sibling = (x, y, c), (x, y, 1 - c)

        def copy(p, k, to):
            return pltpu.make_async_remote_copy(src_ref=_half(g_refs[p].at[k], 1 - c), dst_ref=out_refs[p].at[k],
                                                send_sem=send_sems.at[N_CHIPS * p + k],
                                                recv_sem=recv_sems.at[N_CHIPS * p + k],
                                                device_id=to, device_id_type=MESH)

        cps = [copy(p, k, sibling) for p in range(n) for k in range(N_CHIPS)]
        for cp in cps:
            cp.start()
        for p in range(n):
            for k in range(N_CHIPS):
                copy(p, k, me).wait_recv()
        for cp in cps:
            cp.wait_send()

    return pl.pallas_call(
        body, out_shape=[jax.ShapeDtypeStruct((N_CHIPS, g.shape[1] // 2) + g.shape[2:], g.dtype) for g in gs],
        in_specs=_hbm_specs(n), out_specs=_hbm_specs(n),
        scratch_shapes=[pltpu.SemaphoreType.DMA((N_CHIPS * n,)), pltpu.SemaphoreType.DMA((N_CHIPS * n,))],
        name="sibling_halves",
    )(*gs)


def exchange_chips(ps):
    n = len(ps)

    def body(*refs):
        p_refs, out_refs, (send_sems, recv_sems) = refs[:n], refs[n:2 * n], refs[2 * n:]
        x, y, c = _place()
        me = (x, y, c)
        chips = _other_chips(x, y)

        def copy(p, j, chip, to):
            return pltpu.make_async_remote_copy(src_ref=p_refs[p].at[2 * chip[0] + chip[1]], dst_ref=out_refs[p].at[j],
                                                send_sem=send_sems.at[3 * p + j], recv_sem=recv_sems.at[3 * p + j],
                                                device_id=to, device_id_type=MESH)

        cps = [copy(p, j, chip, (*chip, c)) for p in range(n) for j, chip in enumerate(chips)]
        for cp in cps:
            cp.start()
        for p in range(n):
            for j, chip in enumerate(chips):
                copy(p, j, chip, me).wait_recv()
        for cp in cps:
            cp.wait_send()

    return pl.pallas_call(
        body, out_shape=[jax.ShapeDtypeStruct((3,) + p_.shape[1:], p_.dtype) for p_ in ps],
        in_specs=_hbm_specs(n), out_specs=_hbm_specs(n),
        scratch_shapes=[pltpu.SemaphoreType.DMA((3 * n,)), pltpu.SemaphoreType.DMA((3 * n,))],
        name="exchange_chips",
    )(*ps)


def sibling_join(rs):
    n = len(rs)

    def body(*refs):
        r_refs, out_refs, (send_sems, recv_sems) = refs[:n], refs[n:2 * n], refs[2 * n:]
        x, y, c = _place()

        def copy(p, h, to):
            return pltpu.make_async_remote_copy(src_ref=r_refs[p], dst_ref=_half(out_refs[p], h),
                                                send_sem=send_sems.at[p], recv_sem=recv_sems.at[p],
                                                device_id=to, device_id_type=MESH)

        cps = [copy(p, c, (x, y, 1 - c)) for p in range(n)]
        for cp in cps:
            cp.start()
        for p in range(n):
            copy(p, 1 - c, (x, y, c)).wait_recv()
        for cp in cps:
            cp.wait_send()

    return pl.pallas_call(
        body, out_shape=[jax.ShapeDtypeStruct((2 * r.shape[0],) + r.shape[1:], r.dtype) for r in rs],
        in_specs=_hbm_specs(n), out_specs=_hbm_specs(n),
        scratch_shapes=[pltpu.SemaphoreType.DMA((n,)), pltpu.SemaphoreType.DMA((n,))],
        name="sibling_join",
    )(*rs)


def gather_devices(v, name):
    R = v.shape[0]

    def body(v_ref, out_ref, send_sems, recv_sems, local_sem):
        x, y, c = _place()
        me, sibling = (x, y, c), (x, y, 1 - c)
        chips = _other_chips(x, y)

        def rows(px, py, pc):
            return out_ref.at[pl.ds((4 * px + 2 * py + pc) * R, R), :]

        def copy(k, block, to, src=None):
            return pltpu.make_async_remote_copy(src_ref=rows(*block) if src is None else src, dst_ref=rows(*block),
                                                send_sem=send_sems.at[k], recv_sem=recv_sems.at[k],
                                                device_id=to, device_id_type=MESH)

        mine = pltpu.make_async_copy(v_ref, rows(*me), local_sem)
        mine.start()
        first = [copy(0, me, sibling, src=v_ref)]
        first += [copy(1 + j, me, (*chip, c), src=v_ref) for j, chip in enumerate(chips)]
        for cp in first:
            cp.start()
        passed = [copy(4 + j, (*chip, c), sibling) for j, chip in enumerate(chips)]
        for j, chip in enumerate(chips):
            copy(1 + j, (*chip, c), me).wait_recv()
            passed[j].start()
        copy(0, sibling, me).wait_recv()
        for j, chip in enumerate(chips):
            copy(4 + j, (*chip, 1 - c), me).wait_recv()
        for cp in first + passed:
            cp.wait_send()
        mine.wait()

    return pl.pallas_call(
        body, out_shape=jax.ShapeDtypeStruct((N_DEV * R, LANES), v.dtype),
        in_specs=[pl.BlockSpec(memory_space=pltpu.VMEM)], out_specs=pl.BlockSpec(memory_space=pltpu.VMEM),
        scratch_shapes=[pltpu.SemaphoreType.DMA((7,)), pltpu.SemaphoreType.DMA((7,)), pltpu.SemaphoreType.DMA],
        name=name, compiler_params=_params(),
    )(v)


def _flat_rows(n, mult):
    return -(-n // (LANES * mult)) * mult


def _adam(w, g, m, v):
    m = ADAM_B1 * m + (1.0 - ADAM_B1) * g
    v = ADAM_B2 * v + (1.0 - ADAM_B2) * (g * g)
    m_hat = m / (1.0 - ADAM_B1 ** ADAM_STEP)
    v_hat = v / (1.0 - ADAM_B2 ** ADAM_STEP)
    return -ADAM_LR * (m_hat / (jnp.sqrt(v_hat) + ADAM_EPS) + ADAM_WD * w), m, v


def adam_2d(w, g, m, v, name):
    shape = w.shape
    F = shape[-1]
    a = [t.reshape(-1, F) for t in (w, g, m, v)]
    d, m2, v2 = rowwise(lambda w_, g_, m_, v_: (list(_adam(w_, g_, m_, v_)), []), a, outs=[(F, F32)] * 3, name=name)
    return d.reshape(shape), m2.reshape(shape), v2.reshape(shape)


WEIGHTS = ['ffn1_norm', 'ffn1_wg', 'ffn1_wu', 'ffn1_wd', 'mix_norm', 'ffn2_norm', 'ffn2_wg', 'ffn2_wu', 'ffn2_wd',
           'ple_w', 'ple_norm', 'ple_gate_norm', 'ple_gate_w', 'ev_w_in', 'lru_conv_w', 'lru_conv_b', 'lru_wa',
           'lru_ba', 'lru_wx', 'lru_bx', 'lru_lambda', 'fox_bf', 'fox_q_norm', 'fox_k_norm', 'ev_w_out', 'od_w_in',
           'swa_q_norm', 'swa_k_norm', 'swa_sinks', 's5_lambda_re', 's5_lambda_im', 's5_log_dt', 's5_b_re',
           's5_b_im', 's5_c_re', 's5_c_im', 's5_d', 's5_glu_w', 's5_glu_b', 'od_w_out']
SHARD_AXIS = {'ffn1_wg': 2, 'ffn1_wu': 2, 'ffn1_wd': 1, 'ffn2_wg': 2, 'ffn2_wu': 2, 'ffn2_wd': 1, 'ple_w': 2,
              'ple_gate_w': 1, 'ev_w_in': 2, 'lru_conv_w': 2, 'ev_w_out': 1, 'od_w_in': 2, 's5_d': 1,
              's5_glu_w': 1, 's5_glu_b': 1, 'od_w_out': 1}
EXACT_SHARDED = ('lru_conv_w', 's5_d', 's5_glu_b')
SHARDED = [n for n in WEIGHTS if n in SHARD_AXIS]
REPLICATED = [n for n in WEIGHTS if n not in SHARD_AXIS]


GROUPS = {
    'wgu': ['ffn1_wg', 'ffn1_wu', 'ffn2_wg', 'ffn2_wu'],
    'wd': ['ffn1_wd', 'ffn2_wd'],
    'w_rows': ['ple_gate_w', 'ev_w_out', 'od_w_out'],
    'ple_w': ['ple_w'], 'ev_w_in': ['ev_w_in'], 'od_w_in': ['od_w_in'], 's5_glu_w': ['s5_glu_w'],
}
REDUCED_GROUPS = list(GROUPS)


def _chip():
    return 2 * lax.axis_index("x") + lax.axis_index("y")


def gather_weights(shards):
    own = {k: jnp.concatenate([shards[n] for n in names], axis=0).astype(BF16) for k, names in GROUPS.items()}
    own['exact'] = jnp.concatenate([shards['lru_conv_w'], shards['s5_d'][:, None], shards['s5_glu_b'][:, None]], axis=1)
    keys = list(own)
    got = gather_chips([own[k] for k in keys])
    return {k: lax.dynamic_update_index_in_dim(g, own[k], _chip(), 0) for k, g in zip(keys, got)}


def _rows_by_chip(w):
    return w.reshape(w.shape[0] * w.shape[1], w.shape[2])


def _cols_by_chip(w):
    return w.transpose(1, 0, 2).reshape(w.shape[1], w.shape[0] * w.shape[2])


def _chip_rows(g):
    return g.reshape(N_CHIPS, g.shape[0] // N_CHIPS, g.shape[1])


def _chip_cols(g):
    return g.reshape(g.shape[0], N_CHIPS, g.shape[1] // N_CHIPS).transpose(1, 0, 2)


def full_weights(gw, depth):
    n_ev = (depth + 1) // 2
    ex = gw['exact']
    return dict(
        ple_gate_w=[_rows_by_chip(gw['w_rows'][:, l]) for l in range(depth)],
        ev_w_out=[_rows_by_chip(gw['w_rows'][:, depth + j]) for j in range(n_ev)],
        od_w_out=[_rows_by_chip(gw['w_rows'][:, depth + n_ev + j]) for j in range(depth // 2)],
        ple_w=[_cols_by_chip(gw['ple_w'][:, l]) for l in range(depth)],
        ev_w_in=[_cols_by_chip(gw['ev_w_in'][:, j]) for j in range(n_ev)],
        od_w_in=[_cols_by_chip(gw['od_w_in'][:, j]) for j in range(depth // 2)],
        s5_glu_w=[_rows_by_chip(gw['s5_glu_w'][:, j]) for j in range(depth // 2)],
        lru_conv_w=[_cols_by_chip(ex[:, j, 0:LRU_CONV]) for j in range(n_ev)],
        s5_d=[ex[:, j, LRU_CONV].reshape(-1) for j in range(depth // 2)],
        s5_glu_b=[ex[:, j, LRU_CONV + 1].reshape(-1) for j in range(depth // 2)],
    )


def _add_tile(rows, width):
    for t in (1024, 512, 256, 128, 64, 32, 16):
        if rows % t == 0 and 3 * t * width * 4 <= ROW_TILE_BYTES:
            return t
    return rows


def pair_add(g, t, c, name):
    C, F = g.shape[0], g.shape[-1]
    rows = math.prod(t.shape[1:-1])
    tr = _add_tile(rows, F)
    nb = rows // tr

    def body(c_ref, g_ref, t_ref, o_ref):
        o_ref[...] = (g_ref[...] + t_ref[...]).astype(o_ref.dtype)

    spec = pl.BlockSpec((None, tr, F), lambda k, i, c_ref: (k, i, 0))
    out = pl.pallas_call(
        body, out_shape=jax.ShapeDtypeStruct((C, rows, F), BF16),
        grid_spec=pltpu.PrefetchScalarGridSpec(
            num_scalar_prefetch=1, grid=(C, nb),
            in_specs=[pl.BlockSpec((None, tr, F), lambda k, i, c_ref: (k, c_ref[0] * nb + i, 0)), spec],
            out_specs=spec),
        name=name, compiler_params=_params(("parallel", "parallel")),
    )(c.reshape(1).astype(jnp.int32), g.reshape(C, 2 * rows, F), t.reshape(C, rows, F))
    return out.reshape(t.shape)


def chips_add(p, xs, chip, name):
    F = p.shape[-1]
    rows = math.prod(p.shape[1:-1])
    tr = _add_tile(rows, F)

    def body(m_ref, p_ref, a_ref, b_ref, d_ref, o_ref):
        o_ref[...] = ((p_ref[...].astype(F32) + a_ref[...].astype(F32))
                      + (b_ref[...].astype(F32) + d_ref[...].astype(F32)))

    def other(j):
        return pl.BlockSpec((None, tr, F), lambda i, m_ref: (j, i, 0))

    x3 = xs.reshape(3, rows, F)
    out = pl.pallas_call(
        body, out_shape=jax.ShapeDtypeStruct((rows, F), F32),
        grid_spec=pltpu.PrefetchScalarGridSpec(
            num_scalar_prefetch=1, grid=(rows // tr,),
            in_specs=[pl.BlockSpec((None, tr, F), lambda i, m_ref: (m_ref[0], i, 0)), other(0), other(1), other(2)],
            out_specs=pl.BlockSpec((tr, F), lambda i, m_ref: (i, 0))),
        name=name, compiler_params=_params(("parallel",)),
    )(chip.reshape(1).astype(jnp.int32), p.reshape(N_CHIPS, rows, F), x3, x3, x3)
    return out.reshape(p.shape[1:])


def reduce_sharded(groups):
    keys = list(groups)
    c = lax.axis_index("c")
    gs = [groups[k] for k in keys]
    theirs = sibling_halves(gs)
    pairs = [pair_add(g, t, c, f"pair_add_{k}") for k, g, t in zip(keys, gs, theirs)]
    got = exchange_chips(pairs)
    halves = [chips_add(p_, x_, _chip(), f"chips_add_{k}") for k, p_, x_ in zip(keys, pairs, got)]
    joined = sibling_join(halves)
    out = {}
    for k, h, j in zip(keys, halves, joined):
        out[k] = lax.dynamic_update_slice_in_dim(j, h, c * h.shape[0], axis=0)
    return out


SMALL_GRADS = REPLICATED + list(EXACT_SHARDED)


def _flatten_small(tensors, shapes):
    parts = [tensors[n].astype(F32).reshape(-1) if n in tensors else jnp.zeros((math.prod(shapes[n]),), F32)
             for n in SMALL_GRADS]
    flat = jnp.concatenate(parts)
    rows = _flat_rows(flat.shape[0], SUBLANES)
    return jnp.pad(flat, (0, rows * LANES - flat.shape[0])).reshape(rows, LANES)


def _unflatten_small(flat, shapes):
    flat = flat.reshape(-1)
    out, off = {}, 0
    for n in SMALL_GRADS:
        size = math.prod(shapes[n])
        out[n] = flat[off:off + size].reshape(shapes[n])
        off += size
    return out


def grad_groups(gwgu, gwd, G):
    def st(xs):
        return jnp.stack(xs, axis=1)

    return {
        'wgu': gwgu, 'wd': gwd,
        'w_rows': st([_chip_rows(g) for n in GROUPS['w_rows'] for g in G[n]]),
        'ple_w': st([_chip_cols(g) for g in G['ple_w']]),
        'ev_w_in': st([_chip_cols(g) for g in G['ev_w_in']]),
        'od_w_in': st([_chip_cols(g) for g in G['od_w_in']]),
        's5_glu_w': st([_chip_rows(g) for g in G['s5_glu_w']]),
    }


def ungroup(red, shapes):
    out = {}
    for k, names in GROUPS.items():
        off = 0
        for n in names:
            out[n] = red[k][off:off + shapes[n][0]]
            off += shapes[n][0]
    return out


def _layer_weights(full, small, i, depth):
    j = i // 2
    w = dict(
        g1=small['ffn1_norm'][i][None], gm=small['mix_norm'][i][None], g2=small['ffn2_norm'][i][None],
        gp=small['ple_norm'][i][None], gg=small['ple_gate_norm'][i][None],
        ffn1=(i, depth + i, i), ffn2=(2 * depth + i, 3 * depth + i, depth + i),
        ple_w=full['ple_w'][i], ple_gate_w=full['ple_gate_w'][i],
    )
    if i % 2 == 0:
        w_in = full['ev_w_in'][j]
        w['mix'] = dict(
            w_in=jnp.pad(w_in, ((0, 0), (0, 2688 - w_in.shape[1]))), w_out=full['ev_w_out'][j],
            conv_w=full['lru_conv_w'][j].astype(F32), conv_b=small['lru_conv_b'][j][None],
            w_ax=jnp.concatenate([_block_diag(small['lru_wa'][j]), _block_diag(small['lru_wx'][j])],
                                 axis=1).astype(BF16),
            ba=small['lru_ba'][j][None], bx=small['lru_bx'][j][None], lam=small['lru_lambda'][j][None],
            bf=jnp.pad(small['fox_bf'][j], (0, LANES - 8))[None], qn=small['fox_q_norm'][j],
            kn=small['fox_k_norm'][j])
    else:
        lam, bexp, cexp, ins = s5_prep_fwd(small['s5_lambda_re'][j], small['s5_lambda_im'][j], small['s5_log_dt'][j],
                                           small['s5_b_re'][j], small['s5_b_im'][j], small['s5_c_re'][j],
                                           small['s5_c_im'][j], f"L{i}")
        w['mix'] = dict(
            w_in=full['od_w_in'][j], w_out=full['od_w_out'][j], qn=small['swa_q_norm'][j], kn=small['swa_k_norm'][j],
            sinks=small['swa_sinks'][j], s5_lam=lam, s5_bexp=bexp, s5_cexp=cexp, s5_ins=ins,
            s5_d=full['s5_d'][j].astype(F32)[None], glu_w=full['s5_glu_w'][j], glu_b=full['s5_glu_b'][j].astype(F32)[None])
    return w


def layer_fwd(x, p_i, w, ffnw, i):
    tag = f"L{i}"
    sv = {}
    wgu, wd = ffnw
    x0, n1 = add_norm(x, None, w['g1'], f"norm1_{tag}")
    y1, sv['ffn1'] = ffn_fwd(n1, wgu, wd, *w['ffn1'], f"1_{tag}")
    x1, hm = add_norm(x0, y1, w['gm'], f"normm_{tag}")
    if i % 2 == 0:
        ym, sv['mix'] = even_mixer_fwd(hm, w['mix'], tag)
    else:
        ym, sv['mix'] = odd_mixer_fwd(hm, w['mix'], tag)
    x2, n2 = add_norm(x1, ym, w['g2'], f"norm2_{tag}")
    y2, sv['ffn2'] = ffn_fwd(n2, wgu, wd, *w['ffn2'], f"2_{tag}")
    x3, ng = add_norm(x2, y2, w['gg'], f"normg_{tag}")
    gpre = mm(ng, w['ple_gate_w'], name=f"ple_gate_{tag}")
    epre = mm(p_i, w['ple_w'], name=f"ple_emb_{tag}")
    D = x.shape[1]
    x4 = rowwise(lambda a, b, c, pn: ([_ple_out(a, b, c, pn)], []), [x3, gpre, epre], [w['gp']], outs=[(D, F32)],
                 name=f"ple_out_{tag}")[0]
    sv.update(x0=x0, x1=x1, x2=x2, x3=x3, ng=ng, gpre=gpre, epre=epre, p=p_i)
    return x4, sv


def layer_bwd(dx4, sv, w, ffnw, gbuf, i):
    tag = f"L{i}"
    D = dx4.shape[1]
    g = {}
    wgu, wd = ffnw

    def f_ple(a, b, c, d, pn):
        da, db, dc, dpn = _vjp(_ple_out, (a, b, c, pn), d)
        return [db, dc], [dpn]

    dgpre, depre, dgp = rowwise(f_ple, [sv['x3'], sv['gpre'], sv['epre'], dx4], [w['gp']],
                                outs=[(D, BF16), (D, BF16)], accs=[(1, D)], name=f"ple_out_bwd_{tag}")
    g['gp'] = dgp[0]
    g['ple_w'] = mm(sv['p'], depre, "tn", name=f"ple_emb_dw_{tag}")
    g['ple_gate_w'] = mm(sv['ng'], dgpre, "tn", name=f"ple_gate_dw_{tag}")
    dng = mm(dgpre, w['ple_gate_w'], "nt", name=f"ple_gate_dx_{tag}")
    dx3, dgg = norm_bwd(sv['x3'], w['gg'], dng, dx4, f"normg_bwd_{tag}")
    g['gg'] = dgg[0]
    dn2, gbuf = ffn_bwd(dx3, sv['ffn2'], wgu, wd, *w['ffn2'], gbuf, f"2_{tag}")
    dx2, dg2 = norm_bwd(sv['x2'], w['g2'], dn2, dx3, f"norm2_bwd_{tag}")
    g['g2'] = dg2[0]
    if i % 2 == 0:
        dhm, g['mix'] = even_mixer_bwd(dx2, sv['mix'], w['mix'], tag)
    else:
        dhm, g['mix'] = odd_mixer_bwd(dx2, sv['mix'], w['mix'], tag)
    dx1, dgm = norm_bwd(sv['x1'], w['gm'], dhm, dx2, f"normm_bwd_{tag}")
    g['gm'] = dgm[0]
    dn1, gbuf = ffn_bwd(dx1, sv['ffn1'], wgu, wd, *w['ffn1'], gbuf, f"1_{tag}")
    dx0, dg1 = norm_bwd(sv['x0'], w['g1'], dn1, dx1, f"norm1_bwd_{tag}")
    g['g1'] = dg1[0]
    return dx0, g, gbuf


def _collect_grads(layer_grads, depth):
    st = lambda xs: jnp.stack(xs)
    G = {}
    L = layer_grads
    G['ffn1_norm'] = st([g['g1'] for g in L])
    G['mix_norm'] = st([g['gm'] for g in L])
    G['ffn2_norm'] = st([g['g2'] for g in L])
    G['ple_norm'] = st([g['gp'] for g in L])
    G['ple_gate_norm'] = st([g['gg'] for g in L])
    G['ple_w'] = st([g['ple_w'] for g in L])
    G['ple_gate_w'] = st([g['ple_gate_w'] for g in L])
    ev = [L[i]['mix'] for i in range(0, depth, 2)]
    od = [L[i]['mix'] for i in range(1, depth, 2)]
    G['ev_w_in'] = st([m['w_in'][:, :2568] for m in ev])
    G['ev_w_out'] = st([m['w_out'] for m in ev])
    G['lru_conv_w'] = st([m['conv_w'] for m in ev])
    G['lru_conv_b'] = st([m['conv_b'] for m in ev])
    G['lru_wa'] = st([_block_diag_take(m['w_ax'][:, :512], LRU_BLOCKS) for m in ev])
    G['lru_wx'] = st([_block_diag_take(m['w_ax'][:, 512:], LRU_BLOCKS) for m in ev])
    G['lru_ba'] = st([m['ba'] for m in ev])
    G['lru_bx'] = st([m['bx'] for m in ev])
    G['lru_lambda'] = st([m['lam'] for m in ev])
    G['fox_bf'] = st([m['bf'] for m in ev])
    G['fox_q_norm'] = st([m['qn'] for m in ev])
    G['fox_k_norm'] = st([m['kn'] for m in ev])
    G['od_w_in'] = st([m['w_in'] for m in od])
    G['od_w_out'] = st([m['w_out'] for m in od])
    G['swa_q_norm'] = st([m['qn'] for m in od])
    G['swa_k_norm'] = st([m['kn'] for m in od])
    G['swa_sinks'] = st([m['sinks'] for m in od])
    G['s5_lambda_re'] = st([m['s5']['lre'] for m in od])
    G['s5_lambda_im'] = st([m['s5']['lim'] for m in od])
    G['s5_log_dt'] = st([m['s5']['ldt'] for m in od])
    G['s5_b_re'] = st([m['s5']['bre'] for m in od])
    G['s5_b_im'] = st([m['s5']['bim'] for m in od])
    G['s5_c_re'] = st([m['s5']['cre'] for m in od])
    G['s5_c_im'] = st([m['s5']['cim'] for m in od])
    G['s5_d'] = st([m['s5_d'] for m in od])
    G['s5_glu_w'] = st([m['glu_w'] for m in od])
    G['s5_glu_b'] = st([m['glu_b'] for m in od])
    return G


def local_step(x, p, target, ffnw, full, small):
    depth = p.shape[0]
    S, D = x.shape
    ws = [_layer_weights(full, small, i, depth) for i in range(depth)]
    saved = []
    xi = x
    for i in range(depth):
        xi, sv = layer_fwd(xi, p[i], ws[i], ffnw, i)
        saved.append(sv)

    def f_loss(y, t):
        e = y - t
        return [e * (1.0 / D)], [0.5 * jnp.sum(jnp.mean(e * e, axis=-1, keepdims=True), axis=0, keepdims=True)]

    dx, loss = rowwise(f_loss, [xi, target], outs=[(D, F32)], accs=[(1, 1)], name="loss")
    grads = [None] * depth
    gbuf = (None, None)
    for i in reversed(range(depth)):
        dx, grads[i], gbuf = layer_bwd(dx, saved[i], ws[i], ffnw, gbuf, i)
        if i % 2 == 1:
            m = grads[i]['mix']
            m['s5'] = s5_prep_bwd(ws[i]['mix']['s5_ins'], m['s5_lam'], m['s5_bexp'], m['s5_cexp'], f"L{i}")
    return loss[0, 0], dx, gbuf, _collect_grads(grads, depth)


def kernel(x, p, ffn1_norm, ffn1_wg, ffn1_wu, ffn1_wd, mix_norm, ffn2_norm, ffn2_wg, ffn2_wu, ffn2_wd, ple_w, ple_norm, ple_gate_norm, ple_gate_w, ev_w_in, lru_conv_w, lru_conv_b, lru_wa, lru_ba, lru_wx, lru_bx, lru_lambda, fox_bf, fox_q_norm, fox_k_norm, ev_w_out, od_w_in, swa_q_norm, swa_k_norm, swa_sinks, s5_lambda_re, s5_lambda_im, s5_log_dt, s5_b_re, s5_b_im, s5_c_re, s5_c_im, s5_d, s5_glu_w, s5_glu_b, od_w_out, loss_target, m_ffn1_norm, m_ffn1_wg, m_ffn1_wu, m_ffn1_wd, m_mix_norm, m_ffn2_norm, m_ffn2_wg, m_ffn2_wu, m_ffn2_wd, m_ple_w, m_ple_norm, m_ple_gate_norm, m_ple_gate_w, m_ev_w_in, m_lru_conv_w, m_lru_conv_b, m_lru_wa, m_lru_ba, m_lru_wx, m_lru_bx, m_lru_lambda, m_fox_bf, m_fox_q_norm, m_fox_k_norm, m_ev_w_out, m_od_w_in, m_swa_q_norm, m_swa_k_norm, m_swa_sinks, m_s5_lambda_re, m_s5_lambda_im, m_s5_log_dt, m_s5_b_re, m_s5_b_im, m_s5_c_re, m_s5_c_im, m_s5_d, m_s5_glu_w, m_s5_glu_b, m_od_w_out, v_ffn1_norm, v_ffn1_wg, v_ffn1_wu, v_ffn1_wd, v_mix_norm, v_ffn2_norm, v_ffn2_wg, v_ffn2_wu, v_ffn2_wd, v_ple_w, v_ple_norm, v_ple_gate_norm, v_ple_gate_w, v_ev_w_in, v_lru_conv_w, v_lru_conv_b, v_lru_wa, v_lru_ba, v_lru_wx, v_lru_bx, v_lru_lambda, v_fox_bf, v_fox_q_norm, v_fox_k_norm, v_ev_w_out, v_od_w_in, v_swa_q_norm, v_swa_k_norm, v_swa_sinks, v_s5_lambda_re, v_s5_lambda_im, v_s5_log_dt, v_s5_b_re, v_s5_b_im, v_s5_c_re, v_s5_c_im, v_s5_d, v_s5_glu_w, v_s5_glu_b, v_od_w_out):
    args = locals()
    wts = {n: args[n] for n in WEIGHTS}
    ms = {n: args["m_" + n] for n in WEIGHTS}
    vs = {n: args["v_" + n] for n in WEIGHTS}
    shapes = {n: wts[n].shape for n in WEIGHTS}

    depth = p.shape[0]
    gw = gather_weights({n: wts[n] for n in SHARDED})
    small = {n: wts[n] for n in REPLICATED}
    loss, dx, (gwgu, gwd), G = local_step(x[0], p[:, 0], loss_target[0], (gw['wgu'], gw['wd']),
                                          full_weights(gw, depth), small)
    loss = lax.psum(loss, ("x", "y", "c"))

    gsh = ungroup(reduce_sharded(grad_groups(gwgu, gwd, G)), shapes)
    full_shapes = {n: (G[n].shape if n in EXACT_SHARDED else shapes[n]) for n in SMALL_GRADS}
    flat_g = _flatten_small(G, full_shapes)
    g8 = gather_devices(flat_g, "gather_small_grads").reshape((N_DEV,) + flat_g.shape)
    wf, mf, vf = (_flatten_small({n: t[n] for n in REPLICATED}, full_shapes) for t in (wts, ms, vs))

    def f_small(g0, g1, g2, g3, g4, g5, g6, g7, w_, m_, v_):
        gsum = ((g0 + g1) + (g2 + g3)) + ((g4 + g5) + (g6 + g7))
        return [gsum] + list(_adam(w_, gsum, m_, v_)), []

    gs_f, ds_f, ms_f, vs_f = rowwise(f_small, [g8[d] for d in range(N_DEV)] + [wf, mf, vf],
                                     outs=[(LANES, F32)] * 4, name="adam_small")
    out_g, out_d, out_m, out_v = {}, {}, {}, {}
    for dst, flat in ((out_g, gs_f), (out_d, ds_f), (out_m, ms_f), (out_v, vs_f)):
        dst.update(_unflatten_small(flat, full_shapes))
    for n in EXACT_SHARDED:
        width = shapes[n][SHARD_AXIS[n]]
        gsh[n] = lax.dynamic_slice_in_dim(out_g[n], _chip() * width, width, axis=SHARD_AXIS[n])
    for n in SHARDED:
        out_g[n] = gsh[n]
        out_d[n], out_m[n], out_v[n] = adam_2d(wts[n], gsh[n], ms[n], vs[n], f"adam_{n}")
    return (loss, dx[None], *[out_g[n] for n in WEIGHTS], *[out_d[n] for n in WEIGHTS],
            *[out_m[n] for n in WEIGHTS], *[out_v[n] for n in WEIGHTS])
```

```python
import functools
import math

import jax
import jax.numpy as jnp
from jax import lax
from jax.experimental import pallas as pl
from jax.experimental.pallas import tpu as pltpu

F32 = jnp.float32
BF16 = jnp.bfloat16
MXU_DTYPE = BF16
HI = lax.Precision.HIGHEST
MESH = pl.DeviceIdType.MESH

VMEM_LIMIT_BYTES = 56 * 1024 * 1024
ROW_TILE_BYTES = 5 * 1024 * 1024
MM_VMEM_BYTES = 40 * 1024 * 1024
MM_TILE_M = 1024
MM_TILE_N = 1408
FLAT_W = 2048
LANES = 128
SUBLANES = 8

HEAD_DIM = 64
LRU_BLOCKS = 8
LRU_CONV = 4
LRU_C = 8.0
SWA_WINDOW = 128
SWA_GROUP = 4
S5_GROUP = 16
S5_GROUPS = 32
S5_STATE = 64
ROPE_THETA = 10000.0
EPS = 1e-6
MACARON = 0.5
NEG = -1e30

ADAM_LR = 0.001
ADAM_B1 = 0.9
ADAM_B2 = 0.999
ADAM_EPS = 1e-08
ADAM_WD = 0.01
ADAM_STEP = 10

N_CHIPS = 4
N_DEV = 8


def _pick(n, cands):
    for c in cands:
        if n % c == 0:
            return c
    return n


def _tile(n, cap, unit):
    best = None
    for t in range(unit, min(n, cap) + 1, unit):
        if n % t == 0:
            best = t
    return n if best is None else best


def _params(sem=None):
    return pltpu.CompilerParams(dimension_semantics=sem, vmem_limit_bytes=VMEM_LIMIT_BYTES)


def rowwise(fn, rows, consts=(), outs=(), accs=(), name="rowwise", periods=None):
    rows, consts = list(rows), list(consts)
    n_r, n_c, n_o, n_a = len(rows), len(consts), len(outs), len(accs)
    R = rows[0].shape[0]
    periods = list(periods) if periods is not None else [None] * n_r
    per_row = sum(max(r.shape[1], LANES) * 4 for r in rows) + sum(max(f, LANES) * 4 for f, _ in outs)
    limit = min([R] + [p for p in periods if p is not None])
    tr = limit
    for c in (1024, 512, 256, 128, 64, 32, 16):
        if c <= limit and limit % c == 0 and R % c == 0 and c * per_row <= ROW_TILE_BYTES:
            tr = c
            break

    def row_map(period):
        if period is None:
            return lambda i: (i, 0)
        nb = period // tr
        return lambda i: (i % nb, 0)

    in_specs = [pl.BlockSpec((tr, r.shape[1]), row_map(p)) for r, p in zip(rows, periods)]
    in_specs += [pl.BlockSpec(c.shape, lambda i: (0, 0)) for c in consts]
    out_shape = [jax.ShapeDtypeStruct((R, f), dt) for f, dt in outs]
    out_shape += [jax.ShapeDtypeStruct(tuple(s), F32) for s in accs]
    out_specs = [pl.BlockSpec((tr, f), lambda i: (i, 0)) for f, _ in outs]
    out_specs += [pl.BlockSpec(tuple(s), lambda i: (0, 0)) for s in accs]

    def body(*refs):
        ins = [r[...] for r in refs[:n_r + n_c]]
        o_refs = refs[n_r + n_c:n_r + n_c + n_o]
        a_refs = refs[n_r + n_c + n_o:]
        ro, ra = fn(*ins)
        for ref, val in zip(o_refs, ro):
            ref[...] = val.astype(ref.dtype)
        if n_a:
            @pl.when(pl.program_id(0) == 0)
            def _():
                for ref in a_refs:
                    ref[...] = jnp.zeros(ref.shape, ref.dtype)
            for ref, val in zip(a_refs, ra):
                ref[...] += val.astype(F32)

    res = pl.pallas_call(
        body, grid=(R // tr,), in_specs=in_specs, out_specs=out_specs, out_shape=out_shape,
        name=name, compiler_params=_params(("arbitrary",)),
    )(*rows, *consts)
    return list(res)


def whole(fn, ins, outs, name="whole"):
    n_i = len(ins)

    def body(*refs):
        vals = fn(*[r[...] for r in refs[:n_i]])
        for ref, val in zip(refs[n_i:], vals):
            ref[...] = val.astype(ref.dtype)

    res = pl.pallas_call(
        body, out_shape=[jax.ShapeDtypeStruct(tuple(s), dt) for s, dt in outs],
        in_specs=[pl.BlockSpec(memory_space=pltpu.VMEM)] * n_i,
        out_specs=[pl.BlockSpec(memory_space=pltpu.VMEM)] * len(outs),
        name=name, compiler_params=_params(),
    )(*ins)
    return list(res)


_DOT_DIMS = {
    "nn": (((1,), (0,)), ((), ())),
    "nt": (((1,), (1,)), ((), ())),
    "tn": (((0,), (0,)), ((), ())),
}


def mm(a, b, mode="nn", out_dtype=F32, name="mm"):
    if mode == "nn":
        (M, K), (K2, N) = a.shape, b.shape
    elif mode == "nt":
        (M, K), (N, K2) = a.shape, b.shape
    else:
        (K, M), (K2, N) = a.shape, b.shape
    assert K == K2, (mode, a.shape, b.shape)
    tn = _tile(N, MM_TILE_N, LANES)
    if mode == "tn":
        tm, tk = _tile(M, MM_TILE_M, LANES), _tile(K, MM_TILE_M, 2 * SUBLANES)
    else:
        tm, tk = _tile(M, MM_TILE_M, 2 * SUBLANES), _tile(K, MM_TILE_N, LANES)

    def vmem_bytes(tm_, tk_):
        return (2 * (tm_ * tk_ * a.dtype.itemsize + tk_ * tn * b.dtype.itemsize
                     + tm_ * tn * jnp.dtype(out_dtype).itemsize) + tm_ * tn * 4)

    while vmem_bytes(tm, tk) > MM_VMEM_BYTES and tk % (2 * LANES) == 0 and K % (tk // 2) == 0:
        tk //= 2
    while vmem_bytes(tm, tk) > MM_VMEM_BYTES and tm % (2 * LANES) == 0 and M % (tm // 2) == 0:
        tm //= 2
    if mode == "tn":
        a_spec = pl.BlockSpec((tk, tm), lambda i, j, k: (k, i))
    else:
        a_spec = pl.BlockSpec((tm, tk), lambda i, j, k: (i, k))
    if mode == "nt":
        b_spec = pl.BlockSpec((tn, tk), lambda i, j, k: (j, k))
    else:
        b_spec = pl.BlockSpec((tk, tn), lambda i, j, k: (k, j))
    nk = K // tk
    dims = _DOT_DIMS[mode]

    def dot(a_ref, b_ref):
        return lax.dot_general(a_ref[...].astype(MXU_DTYPE), b_ref[...].astype(MXU_DTYPE), dims,
                               preferred_element_type=F32)

    def body_one(a_ref, b_ref, o_ref):
        o_ref[...] = dot(a_ref, b_ref).astype(o_ref.dtype)

    def body_acc(a_ref, b_ref, o_ref, acc_ref):
        k = pl.program_id(2)

        @pl.when(k == 0)
        def _():
            acc_ref[...] = dot(a_ref, b_ref)

        @pl.when(k > 0)
        def _():
            acc_ref[...] += dot(a_ref, b_ref)

        @pl.when(k == nk - 1)
        def _():
            o_ref[...] = acc_ref[...].astype(o_ref.dtype)

    return pl.pallas_call(
        body_one if nk == 1 else body_acc, grid=(M // tm, N // tn, nk), in_specs=[a_spec, b_spec],
        out_specs=pl.BlockSpec((tm, tn), lambda i, j, k: (i, j)),
        out_shape=jax.ShapeDtypeStruct((M, N), out_dtype),
        scratch_shapes=[] if nk == 1 else [pltpu.VMEM((tm, tn), F32)],
        name=name, compiler_params=_params(("parallel", "parallel", "arbitrary")),
    )(a, b)


def _roll_rows(x, d, reverse):
    return pltpu.roll(x, (SUBLANES - d) if reverse else d, 0)


def scan_real(a, b, reverse=False, name="scan_real"):
    S, W = b.shape
    cw = _pick(W, (256, 128))
    n_tiles = S // SUBLANES

    def body(a_ref, b_ref, o_ref):
        row = lax.broadcasted_iota(jnp.int32, (SUBLANES, cw), 0)
        edge = 0 if reverse else SUBLANES - 1

        def step(i, carry):
            t = (n_tiles - 1 - i) if reverse else i
            off = pl.multiple_of(t * SUBLANES, SUBLANES)
            A = a_ref[pl.ds(off, SUBLANES), :]
            B = b_ref[pl.ds(off, SUBLANES), :]
            for d in (1, 2, 4):
                m = (row < SUBLANES - d) if reverse else (row >= d)
                B = jnp.where(m, A * _roll_rows(B, d, reverse) + B, B)
                A = jnp.where(m, A * _roll_rows(A, d, reverse), A)
            o_ref[pl.ds(off, SUBLANES), :] = B + A * carry
            at_edge = row == edge
            return (jnp.sum(jnp.where(at_edge, B, 0.0), axis=0, keepdims=True)
                    + jnp.sum(jnp.where(at_edge, A, 0.0), axis=0, keepdims=True) * carry)

        lax.fori_loop(0, n_tiles, step, jnp.zeros((1, cw), F32), unroll=2)

    spec = pl.BlockSpec((S, cw), lambda j: (0, j))
    return pl.pallas_call(
        body, grid=(W // cw,), in_specs=[spec, spec], out_specs=spec,
        out_shape=jax.ShapeDtypeStruct((S, W), F32), name=name, compiler_params=_params(("parallel",)),
    )(a, b)


def scan_cplx(lam, bu, reverse=False, name="scan_cplx"):
    S, C = bu.shape
    n_tiles = S // SUBLANES
    half = LANES

    def cmul(ar, ai, br, bi):
        return ar * br - ai * bi, ar * bi + ai * br

    def body(lam_ref, bu_ref, o_ref):
        row = lax.broadcasted_iota(jnp.int32, (SUBLANES, half), 0)
        lr = lam_ref[:, :half]
        li = lam_ref[:, half:]
        if reverse:
            li = -li
        l1 = (lr, li)
        l2 = cmul(*l1, *l1)
        l4 = cmul(*l2, *l2)
        pr = jnp.zeros((SUBLANES, half), F32)
        pi = jnp.zeros((SUBLANES, half), F32)
        p = l1
        for r in range(SUBLANES):
            sel = row == ((SUBLANES - 1 - r) if reverse else r)
            pr = jnp.where(sel, p[0], pr)
            pi = jnp.where(sel, p[1], pi)
            p = cmul(*p, *l1)

        def step(i, carry):
            cr, ci = carry
            t = (n_tiles - 1 - i) if reverse else i
            off = pl.multiple_of(t * SUBLANES, SUBLANES)
            Br = bu_ref[pl.ds(off, SUBLANES), :half]
            Bi = bu_ref[pl.ds(off, SUBLANES), half:]
            for d, (qr, qi) in ((1, l1), (2, l2), (4, l4)):
                m = (row < SUBLANES - d) if reverse else (row >= d)
                sr, si = _roll_rows(Br, d, reverse), _roll_rows(Bi, d, reverse)
                nr = jnp.where(m, Br + qr * sr - qi * si, Br)
                ni = jnp.where(m, Bi + qr * si + qi * sr, Bi)
                Br, Bi = nr, ni
            o_ref[pl.ds(off, SUBLANES), :half] = Br + pr * cr - pi * ci
            o_ref[pl.ds(off, SUBLANES), half:] = Bi + pr * ci + pi * cr
            er, ei = edge_row(Br), edge_row(Bi)
            return er + p8r * cr - p8i * ci, ei + p8r * ci + p8i * cr

        def edge_row(v):
            return jnp.sum(jnp.where(row == (0 if reverse else SUBLANES - 1), v, 0.0), axis=0, keepdims=True)

        p8r, p8i = edge_row(pr), edge_row(pi)
        z = jnp.zeros((1, half), F32)
        lax.fori_loop(0, n_tiles, step, (z, z), unroll=2)

    spec = pl.BlockSpec((S, 2 * half), lambda j: (0, j))
    return pl.pallas_call(
        body, grid=(C // (2 * half),), in_specs=[pl.BlockSpec((1, 2 * half), lambda j: (0, j)), spec],
        out_specs=spec, out_shape=jax.ShapeDtypeStruct((S, C), F32), name=name,
        compiler_params=_params(("parallel",)),
    )(lam, bu)


ATTN_HEADS_PER_STEP = 2


def _attn_tile(S, window):
    if window is None:
        return _pick(S, (512, 256, 128))
    return max(window, _pick(S, (256, 128)))


def _attn_valid(q_blk, k_blk, T, window):
    qpos = q_blk * T + lax.broadcasted_iota(jnp.int32, (T, T), 0)
    kpos = k_blk * T + lax.broadcasted_iota(jnp.int32, (T, T), 1)
    valid = kpos <= qpos
    if window is not None:
        valid = valid & (qpos - kpos < window)
    return valid


def attn_fwd(q, k, v, sink, cq=None, ck=None, window=None, name="attn_fwd"):
    H, S, Dh = q.shape
    G = H // k.shape[0]
    HP = ATTN_HEADS_PER_STEP
    assert H % HP == 0 and (G == 1 or G % HP == 0)
    KP = HP if G == 1 else 1
    T = _attn_tile(S, window)
    nq = S // T
    nks = nq if window is None else 2
    scale = Dh ** -0.5
    bias = cq is not None

    def kv_block(i, j):
        return jnp.minimum(j, i) if window is None else jnp.maximum(i - 1 + j, 0)

    def body(*refs):
        if bias:
            q_ref, k_ref, v_ref, s_ref, cq_ref, ck_ref, o_ref, lse_ref, m_scr, l_scr, acc_scr = refs
        else:
            q_ref, k_ref, v_ref, s_ref, o_ref, lse_ref, m_scr, l_scr, acc_scr = refs
        i, j = pl.program_id(1), pl.program_id(2)

        @pl.when(j == 0)
        def _():
            m_scr[...] = jnp.zeros(m_scr.shape, F32) + s_ref[...]
            l_scr[...] = jnp.ones(l_scr.shape, F32)
            acc_scr[...] = jnp.zeros(acc_scr.shape, F32)

        def block(masked):
            valid = _attn_valid(i, kv_block(i, j), T, window) if masked else None
            for b in range(HP):
                kvb = b if G == 1 else 0
                s = _dotf(q_ref[b], k_ref[kvb], "nt") * scale
                if bias:
                    s = s + cq_ref[b] - ck_ref[b]
                if masked:
                    s = jnp.where(valid, s, NEG)
                m_old = m_scr[b]
                m_new = jnp.maximum(m_old, jnp.max(s, axis=-1, keepdims=True))
                alpha = jnp.exp(m_old - m_new)
                p = jnp.exp(s - m_new)
                l_scr[b] = alpha * l_scr[b] + jnp.sum(p, axis=-1, keepdims=True)
                acc_scr[b] = alpha * acc_scr[b] + _dotf(p, v_ref[kvb])
                m_scr[b] = m_new

        if window is None:
            pl.when(j < i)(lambda: block(False))
            pl.when(j == i)(lambda: block(True))
        else:
            pl.when(i - 1 + j >= 0)(lambda: block(True))

        @pl.when(j == nks - 1)
        def _():
            o_ref[...] = acc_scr[...] / l_scr[...]
            lse_ref[...] = m_scr[...] + jnp.log(l_scr[...])

    def kv_map(hp, i, j):
        return (hp if G == 1 else (hp * HP) // G, kv_block(i, j), 0)

    in_specs = [
        pl.BlockSpec((HP, T, Dh), lambda hp, i, j: (hp, i, 0)),
        pl.BlockSpec((KP, T, Dh), kv_map),
        pl.BlockSpec((KP, T, Dh), kv_map),
        pl.BlockSpec((HP, 1, 1), lambda hp, i, j: (hp, 0, 0)),
    ]
    args = [q, k, v, sink]
    if bias:
        in_specs += [pl.BlockSpec((HP, T, 1), lambda hp, i, j: (hp, i, 0)),
                     pl.BlockSpec((HP, 1, T), lambda hp, i, j: (hp, 0, kv_block(i, j)))]
        args += [cq, ck]
    return pl.pallas_call(
        body, grid=(H // HP, nq, nks), in_specs=in_specs,
        out_specs=[pl.BlockSpec((HP, T, Dh), lambda hp, i, j: (hp, i, 0)),
                   pl.BlockSpec((HP, T, 1), lambda hp, i, j: (hp, i, 0))],
        out_shape=[jax.ShapeDtypeStruct((H, S, Dh), F32), jax.ShapeDtypeStruct((H, S, 1), F32)],
        scratch_shapes=[pltpu.VMEM((HP, T, 1), F32), pltpu.VMEM((HP, T, 1), F32), pltpu.VMEM((HP, T, Dh), F32)],
        name=name, compiler_params=_params(("parallel", "parallel", "arbitrary")),
    )(*args)


def attn_bwd(q, k, v, lse, do, delta, cq=None, ck=None, window=None, name="attn_bwd"):
    H, S, Dh = q.shape
    KVH = k.shape[0]
    G = H // KVH
    HP = ATTN_HEADS_PER_STEP
    assert H % HP == 0 and (G == 1 or G % HP == 0)
    pair_kv = G == 1
    KP = HP if pair_kv else 1
    T = _attn_tile(S, window)
    nq = S // T
    nqs = nq if window is None else 2
    scale = Dh ** -0.5
    bias = cq is not None
    assert not bias or G == 1

    def q_block(kb, j):
        return jnp.maximum(j, kb) if window is None else jnp.minimum(kb + j, nq - 1)

    def body(*refs):
        if bias:
            (q_ref, k_ref, v_ref, lse_ref, do_ref, dl_ref, cq_ref, ck_ref,
             dq_ref, dk_ref, dv_ref, dcq_ref, dck_ref) = refs
        else:
            q_ref, k_ref, v_ref, lse_ref, do_ref, dl_ref, dq_ref, dk_ref, dv_ref = refs
        kb, gp, j = pl.program_id(1), pl.program_id(2), pl.program_id(3)

        @pl.when((gp == 0) & (j == 0))
        def _():
            dk_ref[...] = jnp.zeros(dk_ref.shape, F32)
            dv_ref[...] = jnp.zeros(dv_ref.shape, F32)
            if bias:
                dck_ref[...] = jnp.zeros(dck_ref.shape, F32)

        @pl.when((kb == 0) & (gp == 0) & (j == 0))
        def _():
            dq_ref[...] = jnp.zeros(dq_ref.shape, F32)
            if bias:
                dcq_ref[...] = jnp.zeros(dcq_ref.shape, F32)

        def block(masked):
            qi = q_block(kb, j)
            off = pl.multiple_of(qi * T, T)
            valid = _attn_valid(qi, kb, T, window) if masked else None
            for b in range(HP):
                kvb = b if pair_kv else 0
                g = 0 if pair_kv else gp * HP + b
                qb, kk, vv = q_ref[b].astype(MXU_DTYPE), k_ref[kvb].astype(MXU_DTYPE), v_ref[kvb].astype(MXU_DTYPE)
                dob = do_ref[b].astype(MXU_DTYPE)
                s = _dotf(qb, kk, "nt") * scale
                if bias:
                    s = s + cq_ref[b] - ck_ref[b]
                if masked:
                    s = jnp.where(valid, s, NEG)
                p = jnp.exp(s - lse_ref[b])
                dv_ref[kvb] += _dotf(p, dob, "tn")
                ds = p * (_dotf(dob, vv, "nt") - dl_ref[b])
                dsb = ds.astype(MXU_DTYPE)
                dq_ref[kvb, g, pl.ds(off, T), :] += scale * _dotf(dsb, kk)
                dk_ref[kvb] += scale * _dotf(dsb, qb, "tn")
                if bias:
                    dcq_ref[kvb, g, pl.ds(off, T), :] += jnp.sum(ds, axis=1, keepdims=True)
                    dck_ref[kvb] -= jnp.sum(ds, axis=0, keepdims=True)

        if window is None:
            pl.when(j > kb)(lambda: block(False))
            pl.when(j == kb)(lambda: block(True))
        else:
            pl.when(kb + j <= nq - 1)(lambda: block(True))

    def qmap(kvp, kb, gp, j):
        return (kvp if pair_kv else (kvp * G) // HP + gp, q_block(kb, j), 0)

    in_specs = [
        pl.BlockSpec((HP, T, Dh), qmap),
        pl.BlockSpec((KP, T, Dh), lambda kvp, kb, gp, j: (kvp, kb, 0)),
        pl.BlockSpec((KP, T, Dh), lambda kvp, kb, gp, j: (kvp, kb, 0)),
        pl.BlockSpec((HP, T, 1), qmap),
        pl.BlockSpec((HP, T, Dh), qmap),
        pl.BlockSpec((HP, T, 1), qmap),
    ]
    args = [q, k, v, lse, do, delta]
    out_specs = [
        pl.BlockSpec((KP, G, S, Dh), lambda kvp, kb, gp, j: (kvp, 0, 0, 0)),
        pl.BlockSpec((KP, T, Dh), lambda kvp, kb, gp, j: (kvp, kb, 0)),
        pl.BlockSpec((KP, T, Dh), lambda kvp, kb, gp, j: (kvp, kb, 0)),
    ]
    out_shape = [jax.ShapeDtypeStruct((KVH, G, S, Dh), F32), jax.ShapeDtypeStruct((KVH, S, Dh), F32),
                 jax.ShapeDtypeStruct((KVH, S, Dh), F32)]
    if bias:
        in_specs += [pl.BlockSpec((HP, T, 1), qmap),
                     pl.BlockSpec((HP, 1, T), lambda kvp, kb, gp, j: (kvp, 0, kb))]
        args += [cq, ck]
        out_specs += [pl.BlockSpec((KP, G, S, 1), lambda kvp, kb, gp, j: (kvp, 0, 0, 0)),
                      pl.BlockSpec((KP, 1, T), lambda kvp, kb, gp, j: (kvp, 0, kb))]
        out_shape += [jax.ShapeDtypeStruct((KVH, G, S, 1), F32), jax.ShapeDtypeStruct((KVH, 1, S), F32)]
    res = pl.pallas_call(
        body, grid=(KVH // KP, nq, 1 if pair_kv else G // HP, nqs), in_specs=in_specs, out_specs=out_specs,
        out_shape=out_shape, name=name, compiler_params=_params(("arbitrary", "arbitrary", "arbitrary", "arbitrary")),
    )(*args)
    dq = res[0].reshape(H, S, Dh)
    if bias:
        return dq, res[1], res[2], res[3].reshape(H, S, 1), res[4]
    return dq, res[1], res[2]


def _rms(x, g):
    return x * lax.rsqrt(jnp.mean(x * x, axis=-1, keepdims=True) + EPS) * g


def _sigmoid(x):
    return 1.0 / (1.0 + jnp.exp(-x))


def _softplus(x):
    return jnp.maximum(x, 0.0) + jnp.log(1.0 + jnp.exp(-jnp.abs(x)))


def _log_sigmoid(x):
    return jnp.minimum(x, 0.0) - jnp.log(1.0 + jnp.exp(-jnp.abs(x)))


def _gelu(x):
    return 0.5 * x * (1.0 + jnp.tanh(math.sqrt(2.0 / math.pi) * (x + 0.044715 * (x * x * x))))


def _silu(x):
    return x * _sigmoid(x)


def _ffn_act(gu):
    f = gu.shape[1] // 2
    return MACARON * _silu(gu[:, :f]) * gu[:, f:]


def _qk_prep(rope):
    def f(x, *rest):
        if rope:
            cos, sin, g, rot = rest
        else:
            (g,) = rest
        y = _rms(x, g)
        if rope:
            y = y * cos + jnp.dot(y, rot, precision=HI, preferred_element_type=F32) * sin
        return y
    return f


def _lru_gates(pre, xc, ba, bx, lam):
    w = xc.shape[1]
    r = _sigmoid(pre[:, :w] + ba)
    i = _sigmoid(pre[:, w:] + bx)
    log_a = -LRU_C * r * _softplus(lam)
    a = jnp.exp(log_a)
    b = jnp.sqrt(1.0 - jnp.exp(2.0 * log_a)) * (i * xc)
    return a, b


def _lru_conv(x0, x1, x2, x3, w0, w1, w2, w3, cb):
    return cb + x0 * w0 + x1 * w1 + x2 * w2 + x3 * w3


def _s5_params(lre, lim, ldt, gsel, bre, bim):
    dt = jnp.sum(gsel * jnp.exp(ldt), axis=1, keepdims=True)
    er = jnp.exp(lre * dt)
    ang = lim * dt
    lbr, lbi = er * jnp.cos(ang), er * jnp.sin(ang)
    nr, ni = lbr - 1.0, lbi
    den = lre * lre + lim * lim
    fr, fi = (nr * lre + ni * lim) / den, (ni * lre - nr * lim) / den
    return lbr, lbi, fr * bre - fi * bim, fr * bim + fi * bre


def _s5_out(yssm, u, d):
    return _gelu(yssm + d * u)


def _glu(z, gl, gb):
    return z * _sigmoid(gl + gb)


def _ple_out(x, gpre, epre, pn):
    return x + _sigmoid(gpre) * _rms(epre, pn)


def _vjp(fn, args, cots):
    _, pull = jax.vjp(fn, *args)
    return pull(cots)


def add_norm(x, y, g, name):
    D = x.shape[1]
    if y is None:
        return x, rowwise(lambda xv, gv: ([_rms(xv, gv)], []), [x], [g], outs=[(D, BF16)], name=name)[0]
    xn, n = rowwise(lambda xv, yv, gv: ([xv + yv, _rms(xv + yv, gv)], []), [x, y], [g],
                    outs=[(D, F32), (D, BF16)], name=name)
    return xn, n


def norm_bwd(x, g, dn, dx_res, name):
    D = x.shape[1]

    def f(xv, dnv, dxv, gv):
        dx, dg = _vjp(_rms, (xv, gv), dnv)
        return [dxv + dx], [dg]

    return rowwise(f, [x, dn, dx_res], [g], outs=[(D, F32)], accs=[(1, D)], name=name)


def _swiglu(g, u):
    return MACARON * _silu(g) * u


def _dotf(a, b, mode="nn"):
    return lax.dot_general(a.astype(MXU_DTYPE), b.astype(MXU_DTYPE), _DOT_DIMS[mode], preferred_element_type=F32)


def ffn_up(n, wgu, ig, iu, name):
    S, D = n.shape
    C, _, _, Fc = wgu.shape
    tm = _tile(S, MM_TILE_M, 2 * SUBLANES)

    def body(n_ref, wg_ref, wu_ref, g_ref, u_ref, a_ref):
        g = _dotf(n_ref[...], wg_ref[...])
        u = _dotf(n_ref[...], wu_ref[...])
        g_ref[...] = g.astype(g_ref.dtype)
        u_ref[...] = u.astype(u_ref.dtype)
        a_ref[...] = _swiglu(g, u).astype(a_ref.dtype)

    hid = pl.BlockSpec((None, tm, Fc), lambda s, i: (s, i, 0))
    return pl.pallas_call(
        body, grid=(C, S // tm),
        in_specs=[pl.BlockSpec((tm, D), lambda s, i: (i, 0)),
                  pl.BlockSpec((None, None, D, Fc), lambda s, i: (s, ig, 0, 0)),
                  pl.BlockSpec((None, None, D, Fc), lambda s, i: (s, iu, 0, 0))],
        out_specs=[hid, hid, hid], out_shape=[jax.ShapeDtypeStruct((C, S, Fc), BF16)] * 3,
        name=name, compiler_params=_params(("parallel", "parallel")),
    )(n, wgu, wgu)


def ffn_down(act, wd, iw, name):
    C, S, Fc = act.shape
    D = wd.shape[-1]
    tm = _tile(S, MM_TILE_M, 2 * SUBLANES)

    def body(a_ref, w_ref, o_ref):
        s = pl.program_id(1)
        r = _dotf(a_ref[...], w_ref[...])

        @pl.when(s == 0)
        def _():
            o_ref[...] = r

        @pl.when(s > 0)
        def _():
            o_ref[...] += r

    return pl.pallas_call(
        body, grid=(S // tm, C),
        in_specs=[pl.BlockSpec((None, tm, Fc), lambda i, s: (s, i, 0)),
                  pl.BlockSpec((None, None, Fc, D), lambda i, s: (s, iw, 0, 0))],
        out_specs=pl.BlockSpec((tm, D), lambda i, s: (i, 0)), out_shape=jax.ShapeDtypeStruct((S, D), F32),
        name=name, compiler_params=_params(("parallel", "arbitrary")),
    )(act, wd)


def ffn_down_bwd(dy, wd, iw, g, u, name):
    C, S, Fc = g.shape
    D = dy.shape[1]
    tm = _tile(S, MM_TILE_M, 2 * SUBLANES)

    def body(dy_ref, w_ref, g_ref, u_ref, dg_ref, du_ref):
        dact = _dotf(dy_ref[...], w_ref[...], "nt")
        dg, du = _vjp(_swiglu, (g_ref[...].astype(F32), u_ref[...].astype(F32)), dact)
        dg_ref[...] = dg.astype(dg_ref.dtype)
        du_ref[...] = du.astype(du_ref.dtype)

    hid = pl.BlockSpec((None, tm, Fc), lambda s, i: (s, i, 0))
    return pl.pallas_call(
        body, grid=(C, S // tm),
        in_specs=[pl.BlockSpec((tm, D), lambda s, i: (i, 0)),
                  pl.BlockSpec((None, None, Fc, D), lambda s, i: (s, iw, 0, 0)), hid, hid],
        out_specs=[hid, hid], out_shape=[jax.ShapeDtypeStruct((C, S, Fc), BF16)] * 2,
        name=name, compiler_params=_params(("parallel", "parallel")),
    )(dy, wd, g, u)


def ffn_dn(dg, du, wgu, ig, iu, name):
    C, S, Fc = dg.shape
    D = wgu.shape[2]
    tm = _tile(S, MM_TILE_M, 2 * SUBLANES)

    def body(dg_ref, du_ref, wg_ref, wu_ref, o_ref):
        s = pl.program_id(1)
        r = _dotf(dg_ref[...], wg_ref[...], "nt") + _dotf(du_ref[...], wu_ref[...], "nt")

        @pl.when(s == 0)
        def _():
            o_ref[...] = r

        @pl.when(s > 0)
        def _():
            o_ref[...] += r

    hid = pl.BlockSpec((None, tm, Fc), lambda i, s: (s, i, 0))
    return pl.pallas_call(
        body, grid=(S // tm, C),
        in_specs=[hid, hid, pl.BlockSpec((None, None, D, Fc), lambda i, s: (s, ig, 0, 0)),
                  pl.BlockSpec((None, None, D, Fc), lambda i, s: (s, iu, 0, 0))],
        out_specs=pl.BlockSpec((tm, D), lambda i, s: (i, 0)), out_shape=jax.ShapeDtypeStruct((S, D), F32),
        name=name, compiler_params=_params(("parallel", "arbitrary")),
    )(dg, du, wgu, wgu)


def ffn_dw(a, d, buf, idx, shape, blocked, name):
    C, P, M, N = shape
    S = d.shape[-2]
    tk = _tile(S, MM_TILE_M, 2 * SUBLANES)

    def body(*refs):
        a_ref, d_ref, o_ref = refs[0], refs[1], refs[-1]
        k = pl.program_id(1)
        r = _dotf(a_ref[...], d_ref[...], "tn")

        @pl.when(k == 0)
        def _():
            o_ref[...] = r

        @pl.when(k > 0)
        def _():
            o_ref[...] += r

    if blocked == "a":
        a_spec = pl.BlockSpec((None, tk, M), lambda s, k: (s, k, 0))
        d_spec = pl.BlockSpec((tk, N), lambda s, k: (k, 0))
    else:
        a_spec = pl.BlockSpec((tk, M), lambda s, k: (k, 0))
        d_spec = pl.BlockSpec((None, tk, N), lambda s, k: (s, k, 0))
    out_spec = pl.BlockSpec((None, None, M, N), lambda s, k: (s, idx, 0, 0))
    out_shape = jax.ShapeDtypeStruct(tuple(shape), F32)
    if buf is None:
        return pl.pallas_call(body, grid=(C, S // tk), in_specs=[a_spec, d_spec], out_specs=out_spec,
                              out_shape=out_shape, name=name, compiler_params=_params(("parallel", "arbitrary")))(a, d)
    return pl.pallas_call(body, grid=(C, S // tk), in_specs=[a_spec, d_spec, pl.BlockSpec(memory_space=pl.ANY)],
                          out_specs=out_spec, out_shape=out_shape, input_output_aliases={2: 0}, name=name,
                          compiler_params=_params(("parallel", "arbitrary")))(a, d, buf)


def ffn_fwd(n, wgu, wd, ig, iu, iw, tag):
    g, u, act = ffn_up(n, wgu, ig, iu, f"ffn_up_{tag}")
    return ffn_down(act, wd, iw, f"ffn_down_{tag}"), (n, g, u, act)


def ffn_bwd(dy, saved, wgu, wd, ig, iu, iw, gbuf, tag):
    n, g, u, act = saved
    gwgu, gwd = gbuf
    dg, du = ffn_down_bwd(dy, wd, iw, g, u, f"ffn_down_bwd_{tag}")
    gwd = ffn_dw(act, dy, gwd, iw, (N_CHIPS,) + wd.shape[1:], "a", f"ffn_dwd_{tag}")
    dn = ffn_dn(dg, du, wgu, ig, iu, f"ffn_dn_{tag}")
    gwgu = ffn_dw(n, dg, gwgu, ig, (N_CHIPS,) + wgu.shape[1:], "d", f"ffn_dwg_{tag}")
    gwgu = ffn_dw(n, du, gwgu, iu, (N_CHIPS,) + wgu.shape[1:], "d", f"ffn_dwu_{tag}")
    return dn, (gwgu, gwd)


def _heads(x, H):
    S = x.shape[0]
    return x.reshape(S, H, HEAD_DIM).transpose(1, 0, 2)


def _unheads(x):
    H, S, _ = x.shape
    return x.transpose(1, 0, 2).reshape(S, H * HEAD_DIM)


def _shift_down(x, n=1):
    return jnp.pad(x, ((n, 0), (0, 0)))[:x.shape[0]]


def _shift_up(x, n=1):
    return jnp.pad(x, ((0, n), (0, 0)))[n:]


def _block_diag(w):
    B, I, J = w.shape
    eye = jnp.eye(B, dtype=w.dtype)
    return (w[:, :, None, :] * eye[:, None, :, None]).reshape(B * I, B * J)


def _block_diag_take(x, B):
    I, J = x.shape[0] // B, x.shape[1] // B
    eye = jnp.eye(B, dtype=x.dtype)
    return jnp.sum(x.reshape(B, I, B, J) * eye[:, None, :, None], axis=2)


def _rope_tables(S):
    half = HEAD_DIM // 2
    inv = jnp.power(ROPE_THETA, -jnp.arange(half, dtype=F32) / half)
    ang = jnp.arange(S, dtype=F32)[:, None] * inv[None, :]
    cos = jnp.concatenate([jnp.cos(ang), jnp.cos(ang)], axis=1)
    sin = jnp.concatenate([jnp.sin(ang), jnp.sin(ang)], axis=1)
    r = jnp.arange(HEAD_DIM)[:, None]
    c = jnp.arange(HEAD_DIM)[None, :]
    rot = jnp.where(r == c + half, -1.0, 0.0) + jnp.where(c == r + half, 1.0, 0.0)
    return cos, sin, rot.astype(F32)


def qk_prep_fwd(x_hm, g, rope_tabs, name):
    H, S, Dh = x_hm.shape
    rows = [x_hm.reshape(H * S, Dh)]
    consts = [g.reshape(1, Dh)]
    periods = [None]
    if rope_tabs is not None:
        rows += [rope_tabs[0], rope_tabs[1]]
        consts += [rope_tabs[2]]
        periods += [S, S]
    fn = _qk_prep(rope_tabs is not None)
    y = rowwise(lambda *a: ([fn(*a)], []), rows, consts, outs=[(Dh, F32)], name=name, periods=periods)[0]
    return y.reshape(H, S, Dh)


def qk_prep_bwd(x_hm, g, rope_tabs, dy_hm, name):
    H, S, Dh = x_hm.shape
    rope = rope_tabs is not None
    rows = [x_hm.reshape(H * S, Dh), dy_hm.reshape(H * S, Dh)]
    consts = [g.reshape(1, Dh)]
    periods = [None, None]
    if rope:
        rows += [rope_tabs[0], rope_tabs[1]]
        consts += [rope_tabs[2]]
        periods += [S, S]
    fn = _qk_prep(rope)

    def f(xv, dyv, *rest):
        if rope:
            cos, sin, gv, rot = rest
            dx, dg = _vjp(lambda a, b: fn(a, cos, sin, b, rot), (xv, gv), dyv)
        else:
            (gv,) = rest
            dx, dg = _vjp(fn, (xv, gv), dyv)
        return [dx], [dg]

    dx, dg = rowwise(f, rows, consts, outs=[(Dh, F32)], accs=[(1, Dh)], name=name, periods=periods)
    return dx.reshape(H, S, Dh), dg.reshape(Dh)


def attn_delta(do_hm, o_hm, name):
    H, S, Dh = o_hm.shape
    d = rowwise(lambda a, b: ([jnp.sum(a * b, axis=-1, keepdims=True)], []),
                [do_hm.reshape(H * S, Dh), o_hm.reshape(H * S, Dh)], outs=[(1, F32)], name=name)[0]
    return d.reshape(H, S, 1)


def even_mixer_fwd(h, w, tag):
    S = h.shape[0]
    W = 512
    H = 8
    z = mm(h, w["w_in"], name=f"ev_in_{tag}")
    xa, ya, q, k, v, f = (z[:, 0:512], z[:, 512:1024], z[:, 1024:1536], z[:, 1536:2048], z[:, 2048:2560],
                          z[:, 2560:2688])
    xs = [_shift_down(xa, LRU_CONV - 1 - tap) for tap in range(LRU_CONV)]
    taps = [w["conv_w"][tap][None] for tap in range(LRU_CONV)]
    xc = rowwise(lambda *a: ([_lru_conv(*a)], []), xs, taps + [w["conv_b"]], outs=[(W, F32)],
                 name=f"lru_conv_{tag}")[0]
    pre = mm(xc, w["w_ax"], name=f"lru_gates_mm_{tag}")
    a, b = rowwise(lambda p_, x_, ba, bx, lam: (list(_lru_gates(p_, x_, ba, bx, lam)), []), [pre, xc],
                   [w["ba"], w["bx"], w["lam"]], outs=[(W, F32), (W, F32)], name=f"lru_gates_{tag}")
    hs = scan_real(a, b, name=f"lru_scan_{tag}")
    a_out = rowwise(lambda y_, h_: ([_gelu(y_) * h_], []), [ya, hs], outs=[(W, F32)], name=f"lru_out_{tag}")[0]
    lf = rowwise(lambda f_, bf: ([_log_sigmoid(f_ + bf)], []), [f], [w["bf"]], outs=[(LANES, F32)],
                 name=f"fox_logf_{tag}")[0]
    c = scan_real(jnp.ones_like(lf), lf, name=f"fox_cumsum_{tag}")
    c_hm = c[:, :H].T
    q_hm, k_hm, v_hm = _heads(q, H), _heads(k, H), _heads(v, H)
    qn = qk_prep_fwd(q_hm, w["qn"], None, f"fox_qprep_{tag}")
    kn = qk_prep_fwd(k_hm, w["kn"], None, f"fox_kprep_{tag}")
    sink = jnp.full((H, 1, 1), NEG, F32)
    o_hm, lse = attn_fwd(qn, kn, v_hm, sink, c_hm[:, :, None], c_hm[:, None, :], name=f"fox_attn_{tag}")
    mo = jnp.concatenate([a_out, _unheads(o_hm)], axis=1).astype(BF16)
    y = mm(mo, w["w_out"], name=f"ev_out_{tag}")
    saved = dict(h=h, xs=xs, xc=xc, pre=pre, a=a, hs=hs, ya=ya, f=f, c_hm=c_hm, q_hm=q_hm, k_hm=k_hm,
                 v_hm=v_hm, qn=qn, kn=kn, o_hm=o_hm, lse=lse, mo=mo)
    return y, saved


def even_mixer_bwd(dy, sv, w, tag):
    W = 512
    H = 8
    S = dy.shape[0]
    g = {}
    dmo = mm(dy, w["w_out"], "nt", name=f"ev_dmo_{tag}")
    g["w_out"] = mm(sv["mo"], dy, "tn", name=f"ev_dwout_{tag}")
    da_out, do = dmo[:, :W], dmo[:, W:]
    do_hm = _heads(do, H)
    delta = attn_delta(do_hm, sv["o_hm"], f"fox_delta_{tag}")
    c_hm = sv["c_hm"]
    dqn, dkn, dv_hm, dcq, dck = attn_bwd(sv["qn"], sv["kn"], sv["v_hm"], sv["lse"], do_hm, delta,
                                          c_hm[:, :, None], c_hm[:, None, :], name=f"fox_attn_bwd_{tag}")
    dq_hm, g["qn"] = qk_prep_bwd(sv["q_hm"], w["qn"], None, dqn, f"fox_qprep_bwd_{tag}")
    dk_hm, g["kn"] = qk_prep_bwd(sv["k_hm"], w["kn"], None, dkn, f"fox_kprep_bwd_{tag}")
    dc = (dcq[:, :, 0] + dck[:, 0, :]).T
    dc = jnp.pad(dc, ((0, 0), (0, LANES - H)))
    dlf = scan_real(jnp.ones_like(dc), dc, reverse=True, name=f"fox_cumsum_bwd_{tag}")

    def f_logf(f_, d_, bf):
        df, dbf = _vjp(lambda a_, b_: _log_sigmoid(a_ + b_), (f_, bf), d_)
        return [df], [dbf]

    df, dbf = rowwise(f_logf, [sv["f"], dlf], [w["bf"]], outs=[(LANES, F32)], accs=[(1, LANES)],
                      name=f"fox_logf_bwd_{tag}")
    g["bf"] = dbf[0, :H]
    def f_out(y_, h_, d_):
        dyv, dhv = _vjp(lambda a_, b_: _gelu(a_) * b_, (y_, h_), d_)
        return [dyv, dhv], []

    dya, dhs = rowwise(f_out, [sv["ya"], sv["hs"], da_out], outs=[(W, F32), (W, F32)], name=f"lru_out_bwd_{tag}")
    gs = scan_real(_shift_up(sv["a"]), dhs, reverse=True, name=f"lru_scan_bwd_{tag}")

    def f_gates(p_, x_, g_, hp_, ba, bx, lam):
        dp, dx, dba, dbx, dlam = _vjp(_lru_gates, (p_, x_, ba, bx, lam), (g_ * hp_, g_))
        return [dp, dx], [dba, dbx, dlam]

    dpre, dxc, dba, dbx, dlam = rowwise(f_gates, [sv["pre"], sv["xc"], gs, _shift_down(sv["hs"])],
                                        [w["ba"], w["bx"], w["lam"]], outs=[(2 * W, BF16), (W, F32)],
                                        accs=[(1, W)] * 3, name=f"lru_gates_bwd_{tag}")
    g["ba"], g["bx"], g["lam"] = dba[0], dbx[0], dlam[0]
    dxc2 = mm(dpre, w["w_ax"], "nt", name=f"lru_gates_mm_dx_{tag}")
    g["w_ax"] = mm(sv["xc"], dpre, "tn", name=f"lru_gates_mm_dw_{tag}")

    def f_conv(d1, d2, x0, x1, x2, x3):
        d = d1 + d2
        return [d], [jnp.sum(d, axis=0, keepdims=True)] + [jnp.sum(d * xv, axis=0, keepdims=True)
                                                           for xv in (x0, x1, x2, x3)]

    dxc_t, dcb, dw0, dw1, dw2, dw3 = rowwise(f_conv, [dxc, dxc2] + sv["xs"], outs=[(W, F32)],
                                             accs=[(1, W)] * 5, name=f"lru_conv_bwd_{tag}")
    g["conv_b"] = dcb[0]
    g["conv_w"] = jnp.concatenate([dw0, dw1, dw2, dw3], axis=0)
    ds_ = [_shift_up(dxc_t, LRU_CONV - 1 - tap) for tap in range(LRU_CONV)]
    taps = [w["conv_w"][tap][None] for tap in range(LRU_CONV)]
    dxa = rowwise(lambda a, b, c, d, w0, w1, w2, w3: ([a * w0 + b * w1 + c * w2 + d * w3], []), ds_, taps,
                  outs=[(W, F32)], name=f"lru_conv_dx_{tag}")[0]
    dz = jnp.concatenate([dxa, dya, _unheads(dq_hm), _unheads(dk_hm), _unheads(dv_hm), df], axis=1).astype(BF16)
    g["w_in"] = mm(sv["h"], dz, "tn", name=f"ev_dwin_{tag}")
    dh = mm(dz, w["w_in"], "nt", name=f"ev_dh_{tag}")
    return dh, g


def odd_mixer_fwd(h, w, tag):
    S = h.shape[0]
    H, KVH = 8, 2
    z = mm(h, w["w_in"], name=f"od_in_{tag}")
    q, k, v, u = z[:, 0:512], z[:, 512:640], z[:, 640:768], z[:, 768:1280]
    tabs = _rope_tables(S)
    q_hm, k_hm, v_hm = _heads(q, H), _heads(k, KVH), _heads(v, KVH)
    qn = qk_prep_fwd(q_hm, w["qn"], tabs, f"swa_qprep_{tag}")
    kn = qk_prep_fwd(k_hm, w["kn"], tabs, f"swa_kprep_{tag}")
    sink = w["sinks"].reshape(H, 1, 1)
    o_hm, lse = attn_fwd(qn, kn, v_hm, sink, window=SWA_WINDOW, name=f"swa_attn_{tag}")
    lam, bexp = w["s5_lam"], w["s5_bexp"]
    bu = mm(u, bexp, name=f"s5_bu_{tag}")
    hs = scan_cplx(lam, bu, name=f"s5_scan_{tag}")
    yssm = mm(hs, w["s5_cexp"], name=f"s5_y_{tag}")
    zz = rowwise(lambda y_, u_, d_: ([_s5_out(y_, u_, d_)], []), [yssm, u], [w["s5_d"]], outs=[(512, F32)],
                 name=f"s5_gelu_{tag}")[0]
    gl = mm(zz, w["glu_w"], name=f"s5_glu_mm_{tag}")
    d_out = rowwise(lambda z_, g_, b_: ([_glu(z_, g_, b_)], []), [zz, gl], [w["glu_b"]], outs=[(512, F32)],
                    name=f"s5_glu_{tag}")[0]
    mo = jnp.concatenate([_unheads(o_hm), d_out], axis=1).astype(BF16)
    y = mm(mo, w["w_out"], name=f"od_out_{tag}")
    saved = dict(h=h, q_hm=q_hm, k_hm=k_hm, v_hm=v_hm, qn=qn, kn=kn, o_hm=o_hm, lse=lse, u=u, hs=hs, yssm=yssm,
                 zz=zz, gl=gl, mo=mo, tabs=tabs)
    return y, saved


def odd_mixer_bwd(dy, sv, w, tag):
    H, KVH = 8, 2
    g = {}
    dmo = mm(dy, w["w_out"], "nt", name=f"od_dmo_{tag}")
    g["w_out"] = mm(sv["mo"], dy, "tn", name=f"od_dwout_{tag}")
    do, dd = dmo[:, :512], dmo[:, 512:]
    do_hm = _heads(do, H)
    delta = attn_delta(do_hm, sv["o_hm"], f"swa_delta_{tag}")
    dqn, dkn, dv_hm = attn_bwd(sv["qn"], sv["kn"], sv["v_hm"], sv["lse"], do_hm, delta, window=SWA_WINDOW,
                               name=f"swa_attn_bwd_{tag}")
    dq_hm, g["qn"] = qk_prep_bwd(sv["q_hm"], w["qn"], sv["tabs"], dqn, f"swa_qprep_bwd_{tag}")
    dk_hm, g["kn"] = qk_prep_bwd(sv["k_hm"], w["kn"], sv["tabs"], dkn, f"swa_kprep_bwd_{tag}")
    lse_t, delta_t = sv["lse"][:, :, 0].T, delta[:, :, 0].T
    g["sinks"] = rowwise(lambda l_, d_, s_: ([], [jnp.sum(-jnp.exp(s_ - l_) * d_, axis=0, keepdims=True)]),
                         [lse_t, delta_t], [w["sinks"].reshape(1, H)], accs=[(1, H)], name=f"swa_dsink_{tag}")[0][0]
    def f_glu(z_, g_, d_, b_):
        dz_, dg_, db_ = _vjp(_glu, (z_, g_, b_), d_)
        return [dz_, dg_], [db_]

    dzz1, dgl, dglb = rowwise(f_glu, [sv["zz"], sv["gl"], dd], [w["glu_b"]], outs=[(512, F32), (512, BF16)],
                              accs=[(1, 512)], name=f"s5_glu_bwd_{tag}")
    g["glu_b"] = dglb[0]
    g["glu_w"] = mm(sv["zz"], dgl, "tn", name=f"s5_glu_dw_{tag}")
    dzz2 = mm(dgl, w["glu_w"], "nt", name=f"s5_glu_dz_{tag}")

    def f_gelu(y_, u_, d1, d2, dpar):
        dy_, du_, dd_ = _vjp(_s5_out, (y_, u_, dpar), d1 + d2)
        return [dy_, du_], [dd_]

    dyssm, du1, dsd = rowwise(f_gelu, [sv["yssm"], sv["u"], dzz1, dzz2], [w["s5_d"]],
                              outs=[(512, F32), (512, F32)], accs=[(1, 512)], name=f"s5_gelu_bwd_{tag}")
    g["s5_d"] = dsd[0]
    dhs = mm(dyssm, w["s5_cexp"], "nt", name=f"s5_dh_{tag}")
    g["s5_cexp"] = mm(sv["hs"], dyssm, "tn", name=f"s5_dc_{tag}")
    gs = scan_cplx(w["s5_lam"], dhs, reverse=True, name=f"s5_scan_bwd_{tag}")
    g["s5_bexp"] = mm(sv["u"], gs, "tn", name=f"s5_db_{tag}")
    du2 = mm(gs, w["s5_bexp"], "nt", name=f"s5_du_{tag}")

    def f_dlam(g_, hp_):
        C = g_.shape[1]
        outs_r, outs_i = [], []
        for j in range(C // (2 * LANES)):
            gr, gi = g_[:, 2 * LANES * j:2 * LANES * j + LANES], g_[:, 2 * LANES * j + LANES:2 * LANES * (j + 1)]
            hr, hi = hp_[:, 2 * LANES * j:2 * LANES * j + LANES], hp_[:, 2 * LANES * j + LANES:2 * LANES * (j + 1)]
            outs_r.append(jnp.sum(gr * hr + gi * hi, axis=0, keepdims=True))
            outs_i.append(jnp.sum(gi * hr - gr * hi, axis=0, keepdims=True))
        return [], [jnp.concatenate([x for pair in zip(outs_r, outs_i) for x in pair], axis=1)]

    g["s5_lam"] = rowwise(f_dlam, [gs, _shift_down(sv["hs"])], accs=[(1, gs.shape[1])], name=f"s5_dlam_{tag}")[0]
    du = rowwise(lambda a_, b_: ([a_ + b_], []), [du1, du2], outs=[(512, F32)], name=f"s5_du_add_{tag}")[0]
    dz = jnp.concatenate([_unheads(dq_hm), _unheads(dk_hm), _unheads(dv_hm), du], axis=1).astype(BF16)
    g["w_in"] = mm(sv["h"], dz, "tn", name=f"od_dwin_{tag}")
    dh = mm(dz, w["w_in"], "nt", name=f"od_dh_{tag}")
    return dh, g


def _s5_cols(x_re, x_im):
    n = x_re.shape[0] // LANES
    return jnp.stack([x_re.reshape(n, LANES), x_im.reshape(n, LANES)], axis=1).reshape(1, 2 * n * LANES)


def _s5_uncols(x):
    n = x.shape[1] // (2 * LANES)
    y = x.reshape(n, 2, LANES)
    return y[:, 0].reshape(-1), y[:, 1].reshape(-1)


def _s5_gsel():
    return jnp.repeat(jnp.eye(S5_GROUPS, dtype=F32), S5_STATE, axis=0)


def s5_prep_fwd(lre, lim, ldt, bre, bim, cre, cim, tag):
    GP = S5_GROUPS * S5_STATE
    ins = [lre.reshape(GP, 1), lim.reshape(GP, 1), ldt.reshape(1, S5_GROUPS), _s5_gsel(),
           bre.reshape(GP, S5_GROUP), bim.reshape(GP, S5_GROUP)]
    lbr, lbi, bbr, bbi = whole(_s5_params, ins, [((GP, 1), F32)] * 2 + [((GP, S5_GROUP), F32)] * 2,
                               name=f"s5_params_{tag}")
    lam = _s5_cols(lbr[:, 0], lbi[:, 0])

    def expand_b(bb):
        return _block_diag(bb.reshape(S5_GROUPS, S5_STATE, S5_GROUP).transpose(0, 2, 1))

    n = GP // LANES
    bexp = jnp.stack([expand_b(bbr).reshape(-1, n, LANES), expand_b(bbi).reshape(-1, n, LANES)],
                     axis=2).reshape(-1, 2 * GP)
    c_r = _block_diag(cre.transpose(0, 2, 1))
    c_i = _block_diag(cim.transpose(0, 2, 1))
    cexp = jnp.stack([c_r.reshape(n, LANES, -1), -c_i.reshape(n, LANES, -1)], axis=1).reshape(2 * GP, -1)
    return lam, bexp.astype(BF16), cexp.astype(BF16), ins


def s5_prep_bwd(ins, dlam, dbexp, dcexp, tag):
    GP = S5_GROUPS * S5_STATE
    n = GP // LANES
    dlr, dli = _s5_uncols(dlam)
    db = dbexp.reshape(-1, n, 2, LANES)

    def take_b(x):
        return _block_diag_take(x, S5_GROUPS).transpose(0, 2, 1).reshape(GP, S5_GROUP)

    dbbr, dbbi = take_b(db[:, :, 0].reshape(-1, GP)), take_b(db[:, :, 1].reshape(-1, GP))
    dc = dcexp.reshape(n, 2, LANES, -1)
    dcre = _block_diag_take(dc[:, 0].reshape(GP, -1), S5_GROUPS).transpose(0, 2, 1)
    dcim = -_block_diag_take(dc[:, 1].reshape(GP, -1), S5_GROUPS).transpose(0, 2, 1)

    def f(lre, lim, ldt, gsel, bre, bim, c1, c2, c3, c4):
        d = _vjp(lambda a, b, c, e, f_: _s5_params(a, b, c, gsel, e, f_), (lre, lim, ldt, bre, bim), (c1, c2, c3, c4))
        return d

    outs = [((GP, 1), F32)] * 2 + [((1, S5_GROUPS), F32)] + [((GP, S5_GROUP), F32)] * 2
    dlre, dlim, dldt, dbre, dbim = whole(f, ins + [dlr.reshape(GP, 1), dli.reshape(GP, 1), dbbr, dbbi], outs,
                                          name=f"s5_params_bwd_{tag}")
    shp = (S5_GROUPS, S5_STATE)
    return dict(lre=dlre.reshape(shp), lim=dlim.reshape(shp), ldt=dldt.reshape(S5_GROUPS),
                bre=dbre.reshape(S5_GROUPS, S5_STATE, S5_GROUP), bim=dbim.reshape(S5_GROUPS, S5_STATE, S5_GROUP),
                cre=dcre, cim=dcim)


def _place():
    return lax.axis_index("x"), lax.axis_index("y"), lax.axis_index("c")


def _other_chips(x, y):
    return [(1 - x, y), (x, 1 - y), (1 - x, 1 - y)]


def _half(ref, h):
    n = ref.shape[0] // 2
    return ref.at[pl.ds(h * n, n)]


def _hbm_specs(n):
    return [pl.BlockSpec(memory_space=pl.ANY)] * n


def gather_chips(ws):
    n = len(ws)

    def body(*refs):
        w_refs, out_refs, (send_sems, recv_sems) = refs[:n], refs[n:2 * n], refs[2 * n:]
        x, y, c = _place()
        me, sibling = (x, y, c), (x, y, 1 - c)
        chips = _other_chips(x, y)
        mine = 2 * x + y

        def copy(k, src, dst, to):
            return pltpu.make_async_remote_copy(src_ref=src, dst_ref=dst, send_sem=send_sems.at[k],
                                                recv_sem=recv_sems.at[k], device_id=to, device_id_type=MESH)

        first, passed = [], []
        for p in range(n):
            for j, chip in enumerate(chips):
                first.append(copy(6 * p + j, _half(w_refs[p], c), _half(out_refs[p].at[mine], c), (*chip, c)))
                first[-1].start()
        for p in range(n):
            for j, chip in enumerate(chips):
                block = out_refs[p].at[2 * chip[0] + chip[1]]
                copy(6 * p + j, _half(w_refs[p], c), _half(block, c), me).wait_recv()
                passed.append(copy(6 * p + 3 + j, _half(block, c), _half(block, c), sibling))
                passed[-1].start()
        for p in range(n):
            for j, chip in enumerate(chips):
                block = out_refs[p].at[2 * chip[0] + chip[1]]
                copy(6 * p + 3 + j, _half(w_refs[p], c), _half(block, 1 - c), me).wait_recv()
        for cp in first + passed:
            cp.wait_send()

    return pl.pallas_call(
        body, out_shape=[jax.ShapeDtypeStruct((N_CHIPS,) + w.shape, w.dtype) for w in ws],
        in_specs=_hbm_specs(n), out_specs=_hbm_specs(n),
        scratch_shapes=[pltpu.SemaphoreType.DMA((6 * n,)), pltpu.SemaphoreType.DMA((6 * n,))],
        name="gather_chips",
    )(*ws)


def sibling_halves(gs):
    n = len(gs)

    def body(*refs):
        g_refs, out_refs, (send_sems, recv_sems) = refs[:n], refs[n:2 * n], refs[2 * n:]
        x, y, c = _place()
        me, sibling = (x, y, c), (x, y, 1 - c)

        def copy(p, k, to):
            return pltpu.make_async_remote_copy(src_ref=_half(g_refs[p].at[k], 1 - c), dst_ref=out_refs[p].at[k],
                                                send_sem=send_sems.at[N_CHIPS * p + k],
                                                recv_sem=recv_sems.at[N_CHIPS * p + k],
                                                device_id=to, device_id_type=MESH)

        cps = [copy(p, k, sibling) for p in range(n) for k in range(N_CHIPS)]
        for cp in cps:
            cp.start()
        for p in range(n):
            for k in range(N_CHIPS):
                copy(p, k, me).wait_recv()
        for cp in cps:
            cp.wait_send()

    return pl.pallas_call(
        body, out_shape=[jax.ShapeDtypeStruct((N_CHIPS, g.shape[1] // 2) + g.shape[2:], g.dtype) for g in gs],
        in_specs=_hbm_specs(n), out_specs=_hbm_specs(n),
        scratch_shapes=[pltpu.SemaphoreType.DMA((N_CHIPS * n,)), pltpu.SemaphoreType.DMA((N_CHIPS * n,))],
        name="sibling_halves",
    )(*gs)


def exchange_chips(ps):
    n = len(ps)

    def body(*refs):
        p_refs, out_refs, (send_sems, recv_sems) = refs[:n], refs[n:2 * n], refs[2 * n:]
        x, y, c = _place()
        me = (x, y, c)
        chips = _other_chips(x, y)

        def copy(p, j, chip, to):
            return pltpu.make_async_remote_copy(src_ref=p_refs[p].at[2 * chip[0] + chip[1]], dst_ref=out_refs[p].at[j],
                                                send_sem=send_sems.at[3 * p + j], recv_sem=recv_sems.at[3 * p + j],
                                                device_id=to, device_id_type=MESH)

        cps = [copy(p, j, chip, (*chip, c)) for p in range(n) for j, chip in enumerate(chips)]
        for cp in cps:
            cp.start()
        for p in range(n):
            for j, chip in enumerate(chips):
                copy(p, j, chip, me).wait_recv()
        for cp in cps:
            cp.wait_send()

    return pl.pallas_call(
        body, out_shape=[jax.ShapeDtypeStruct((3,) + p_.shape[1:], p_.dtype) for p_ in ps],
        in_specs=_hbm_specs(n), out_specs=_hbm_specs(n),
        scratch_shapes=[pltpu.SemaphoreType.DMA((3 * n,)), pltpu.SemaphoreType.DMA((3 * n,))],
        name="exchange_chips",
    )(*ps)


def sibling_join(rs):
    n = len(rs)

    def body(*refs):
        r_refs, out_refs, (send_sems, recv_sems) = refs[:n], refs[n:2 * n], refs[2 * n:]
        x, y, c = _place()

        def copy(p, h, to):
            return pltpu.make_async_remote_copy(src_ref=r_refs[p], dst_ref=_half(out_refs[p], h),
                                                send_sem=send_sems.at[p], recv_sem=recv_sems.at[p],
                                                device_id=to, device_id_type=MESH)

        cps = [copy(p, c, (x, y, 1 - c)) for p in range(n)]
        for cp in cps:
            cp.start()
        for p in range(n):
            copy(p, 1 - c, (x, y, c)).wait_recv()
        for cp in cps:
            cp.wait_send()

    return pl.pallas_call(
        body, out_shape=[jax.ShapeDtypeStruct((2 * r.shape[0],) + r.shape[1:], r.dtype) for r in rs],
        in_specs=_hbm_specs(n), out_specs=_hbm_specs(n),
        scratch_shapes=[pltpu.SemaphoreType.DMA((n,)), pltpu.SemaphoreType.DMA((n,))],
        name="sibling_join",
    )(*rs)


def gather_devices(v, name):
    R = v.shape[0]

    def body(v_ref, out_ref, send_sems, recv_sems, local_sem):
        x, y, c = _place()
        me, sibling = (x, y, c), (x, y, 1 - c)
        chips = _other_chips(x, y)

        def rows(px, py, pc):
            return out_ref.at[pl.ds((4 * px + 2 * py + pc) * R, R), :]

        def copy(k, block, to, src=None):
            return pltpu.make_async_remote_copy(src_ref=rows(*block) if src is None else src, dst_ref=rows(*block),
                                                send_sem=send_sems.at[k], recv_sem=recv_sems.at[k],
                                                device_id=to, device_id_type=MESH)

        mine = pltpu.make_async_copy(v_ref, rows(*me), local_sem)
        mine.start()
        first = [copy(0, me, sibling, src=v_ref)]
        first += [copy(1 + j, me, (*chip, c), src=v_ref) for j, chip in enumerate(chips)]
        for cp in first:
            cp.start()
        passed = [copy(4 + j, (*chip, c), sibling) for j, chip in enumerate(chips)]
        for j, chip in enumerate(chips):
            copy(1 + j, (*chip, c), me).wait_recv()
            passed[j].start()
        copy(0, sibling, me).wait_recv()
        for j, chip in enumerate(chips):
            copy(4 + j, (*chip, 1 - c), me).wait_recv()
        for cp in first + passed:
            cp.wait_send()
        mine.wait()

    return pl.pallas_call(
        body, out_shape=jax.ShapeDtypeStruct((N_DEV * R, LANES), v.dtype),
        in_specs=[pl.BlockSpec(memory_space=pltpu.VMEM)], out_specs=pl.BlockSpec(memory_space=pltpu.VMEM),
        scratch_shapes=[pltpu.SemaphoreType.DMA((7,)), pltpu.SemaphoreType.DMA((7,)), pltpu.SemaphoreType.DMA],
        name=name, compiler_params=_params(),
    )(v)


def _flat_rows(n, mult):
    return -(-n // (LANES * mult)) * mult


def _adam(w, g, m, v):
    m = ADAM_B1 * m + (1.0 - ADAM_B1) * g
    v = ADAM_B2 * v + (1.0 - ADAM_B2) * (g * g)
    m_hat = m / (1.0 - ADAM_B1 ** ADAM_STEP)
    v_hat = v / (1.0 - ADAM_B2 ** ADAM_STEP)
    return -ADAM_LR * (m_hat / (jnp.sqrt(v_hat) + ADAM_EPS) + ADAM_WD * w), m, v


def adam_2d(w, g, m, v, name):
    shape = w.shape
    F = shape[-1]
    a = [t.reshape(-1, F) for t in (w, g, m, v)]
    d, m2, v2 = rowwise(lambda w_, g_, m_, v_: (list(_adam(w_, g_, m_, v_)), []), a, outs=[(F, F32)] * 3, name=name)
    return d.reshape(shape), m2.reshape(shape), v2.reshape(shape)


WEIGHTS = ['ffn1_norm', 'ffn1_wg', 'ffn1_wu', 'ffn1_wd', 'mix_norm', 'ffn2_norm', 'ffn2_wg', 'ffn2_wu', 'ffn2_wd',
           'ple_w', 'ple_norm', 'ple_gate_norm', 'ple_gate_w', 'ev_w_in', 'lru_conv_w', 'lru_conv_b', 'lru_wa',
           'lru_ba', 'lru_wx', 'lru_bx', 'lru_lambda', 'fox_bf', 'fox_q_norm', 'fox_k_norm', 'ev_w_out', 'od_w_in',
           'swa_q_norm', 'swa_k_norm', 'swa_sinks', 's5_lambda_re', 's5_lambda_im', 's5_log_dt', 's5_b_re',
           's5_b_im', 's5_c_re', 's5_c_im', 's5_d', 's5_glu_w', 's5_glu_b', 'od_w_out']
SHARD_AXIS = {'ffn1_wg': 2, 'ffn1_wu': 2, 'ffn1_wd': 1, 'ffn2_wg': 2, 'ffn2_wu': 2, 'ffn2_wd': 1, 'ple_w': 2,
              'ple_gate_w': 1, 'ev_w_in': 2, 'lru_conv_w': 2, 'ev_w_out': 1, 'od_w_in': 2, 's5_d': 1,
              's5_glu_w': 1, 's5_glu_b': 1, 'od_w_out': 1}
EXACT_SHARDED = ('lru_conv_w', 's5_d', 's5_glu_b')
SHARDED = [n for n in WEIGHTS if n in SHARD_AXIS]
REPLICATED = [n for n in WEIGHTS if n not in SHARD_AXIS]


GROUPS = {
    'wgu': ['ffn1_wg', 'ffn1_wu', 'ffn2_wg', 'ffn2_wu'],
    'wd': ['ffn1_wd', 'ffn2_wd'],
    'w_rows': ['ple_gate_w', 'ev_w_out', 'od_w_out'],
    'ple_w': ['ple_w'], 'ev_w_in': ['ev_w_in'], 'od_w_in': ['od_w_in'], 's5_glu_w': ['s5_glu_w'],
}
REDUCED_GROUPS = list(GROUPS)


def _chip():
    return 2 * lax.axis_index("x") + lax.axis_index("y")


def gather_weights(shards):
    own = {k: jnp.concatenate([shards[n] for n in names], axis=0).astype(BF16) for k, names in GROUPS.items()}
    own['exact'] = jnp.concatenate([shards['lru_conv_w'], shards['s5_d'][:, None], shards['s5_glu_b'][:, None]], axis=1)
    keys = list(own)
    got = gather_chips([own[k] for k in keys])
    return {k: lax.dynamic_update_index_in_dim(g, own[k], _chip(), 0) for k, g in zip(keys, got)}


def _rows_by_chip(w):
    return w.reshape(w.shape[0] * w.shape[1], w.shape[2])


def _cols_by_chip(w):
    return w.transpose(1, 0, 2).reshape(w.shape[1], w.shape[0] * w.shape[2])


def _chip_rows(g):
    return g.reshape(N_CHIPS, g.shape[0] // N_CHIPS, g.shape[1])


def _chip_cols(g):
    return g.reshape(g.shape[0], N_CHIPS, g.shape[1] // N_CHIPS).transpose(1, 0, 2)


def full_weights(gw, depth):
    n_ev = (depth + 1) // 2
    ex = gw['exact']
    return dict(
        ple_gate_w=[_rows_by_chip(gw['w_rows'][:, l]) for l in range(depth)],
        ev_w_out=[_rows_by_chip(gw['w_rows'][:, depth + j]) for j in range(n_ev)],
        od_w_out=[_rows_by_chip(gw['w_rows'][:, depth + n_ev + j]) for j in range(depth // 2)],
        ple_w=[_cols_by_chip(gw['ple_w'][:, l]) for l in range(depth)],
        ev_w_in=[_cols_by_chip(gw['ev_w_in'][:, j]) for j in range(n_ev)],
        od_w_in=[_cols_by_chip(gw['od_w_in'][:, j]) for j in range(depth // 2)],
        s5_glu_w=[_rows_by_chip(gw['s5_glu_w'][:, j]) for j in range(depth // 2)],
        lru_conv_w=[_cols_by_chip(ex[:, j, 0:LRU_CONV]) for j in range(n_ev)],
        s5_d=[ex[:, j, LRU_CONV].reshape(-1) for j in range(depth // 2)],
        s5_glu_b=[ex[:, j, LRU_CONV + 1].reshape(-1) for j in range(depth // 2)],
    )


def _add_tile(rows, width):
    for t in (1024, 512, 256, 128, 64, 32, 16):
        if rows % t == 0 and 3 * t * width * 4 <= ROW_TILE_BYTES:
            return t
    return rows


def pair_add(g, t, c, name):
    C, F = g.shape[0], g.shape[-1]
    rows = math.prod(t.shape[1:-1])
    tr = _add_tile(rows, F)
    nb = rows // tr

    def body(c_ref, g_ref, t_ref, o_ref):
        o_ref[...] = (g_ref[...] + t_ref[...]).astype(o_ref.dtype)

    spec = pl.BlockSpec((None, tr, F), lambda k, i, c_ref: (k, i, 0))
    out = pl.pallas_call(
        body, out_shape=jax.ShapeDtypeStruct((C, rows, F), BF16),
        grid_spec=pltpu.PrefetchScalarGridSpec(
            num_scalar_prefetch=1, grid=(C, nb),
            in_specs=[pl.BlockSpec((None, tr, F), lambda k, i, c_ref: (k, c_ref[0] * nb + i, 0)), spec],
            out_specs=spec),
        name=name, compiler_params=_params(("parallel", "parallel")),
    )(c.reshape(1).astype(jnp.int32), g.reshape(C, 2 * rows, F), t.reshape(C, rows, F))
    return out.reshape(t.shape)


def chips_add(p, xs, chip, name):
    F = p.shape[-1]
    rows = math.prod(p.shape[1:-1])
    tr = _add_tile(rows, F)

    def body(m_ref, p_ref, a_ref, b_ref, d_ref, o_ref):
        o_ref[...] = ((p_ref[...].astype(F32) + a_ref[...].astype(F32))
                      + (b_ref[...].astype(F32) + d_ref[...].astype(F32)))

    def other(j):
        return pl.BlockSpec((None, tr, F), lambda i, m_ref: (j, i, 0))

    x3 = xs.reshape(3, rows, F)
    out = pl.pallas_call(
        body, out_shape=jax.ShapeDtypeStruct((rows, F), F32),
        grid_spec=pltpu.PrefetchScalarGridSpec(
            num_scalar_prefetch=1, grid=(rows // tr,),
            in_specs=[pl.BlockSpec((None, tr, F), lambda i, m_ref: (m_ref[0], i, 0)), other(0), other(1), other(2)],
            out_specs=pl.BlockSpec((tr, F), lambda i, m_ref: (i, 0))),
        name=name, compiler_params=_params(("parallel",)),
    )(chip.reshape(1).astype(jnp.int32), p.reshape(N_CHIPS, rows, F), x3, x3, x3)
    return out.reshape(p.shape[1:])


def reduce_sharded(groups):
    keys = list(groups)
    c = lax.axis_index("c")
    gs = [groups[k] for k in keys]
    theirs = sibling_halves(gs)
    pairs = [pair_add(g, t, c, f"pair_add_{k}") for k, g, t in zip(keys, gs, theirs)]
    got = exchange_chips(pairs)
    halves = [chips_add(p_, x_, _chip(), f"chips_add_{k}") for k, p_, x_ in zip(keys, pairs, got)]
    joined = sibling_join(halves)
    out = {}
    for k, h, j in zip(keys, halves, joined):
        out[k] = lax.dynamic_update_slice_in_dim(j, h, c * h.shape[0], axis=0)
    return out


SMALL_GRADS = REPLICATED + list(EXACT_SHARDED)


def _flatten_small(tensors, shapes):
    parts = [tensors[n].astype(F32).reshape(-1) if n in tensors else jnp.zeros((math.prod(shapes[n]),), F32)
             for n in SMALL_GRADS]
    flat = jnp.concatenate(parts)
    rows = _flat_rows(flat.shape[0], SUBLANES)
    return jnp.pad(flat, (0, rows * LANES - flat.shape[0])).reshape(rows, LANES)


def _unflatten_small(flat, shapes):
    flat = flat.reshape(-1)
    out, off = {}, 0
    for n in SMALL_GRADS:
        size = math.prod(shapes[n])
        out[n] = flat[off:off + size].reshape(shapes[n])
        off += size
    return out


def grad_groups(gwgu, gwd, G):
    def st(xs):
        return jnp.stack(xs, axis=1)

    return {
        'wgu': gwgu, 'wd': gwd,
        'w_rows': st([_chip_rows(g) for n in GROUPS['w_rows'] for g in G[n]]),
        'ple_w': st([_chip_cols(g) for g in G['ple_w']]),
        'ev_w_in': st([_chip_cols(g) for g in G['ev_w_in']]),
        'od_w_in': st([_chip_cols(g) for g in G['od_w_in']]),
        's5_glu_w': st([_chip_rows(g) for g in G['s5_glu_w']]),
    }


def ungroup(red, shapes):
    out = {}
    for k, names in GROUPS.items():
        off = 0
        for n in names:
            out[n] = red[k][off:off + shapes[n][0]]
            off += shapes[n][0]
    return out


def _layer_weights(full, small, i, depth):
    j = i // 2
    w = dict(
        g1=small['ffn1_norm'][i][None], gm=small['mix_norm'][i][None], g2=small['ffn2_norm'][i][None],
        gp=small['ple_norm'][i][None], gg=small['ple_gate_norm'][i][None],
        ffn1=(i, depth + i, i), ffn2=(2 * depth + i, 3 * depth + i, depth + i),
        ple_w=full['ple_w'][i], ple_gate_w=full['ple_gate_w'][i],
    )
    if i % 2 == 0:
        w_in = full['ev_w_in'][j]
        w['mix'] = dict(
            w_in=jnp.pad(w_in, ((0, 0), (0, 2688 - w_in.shape[1]))), w_out=full['ev_w_out'][j],
            conv_w=full['lru_conv_w'][j].astype(F32), conv_b=small['lru_conv_b'][j][None],
            w_ax=jnp.concatenate([_block_diag(small['lru_wa'][j]), _block_diag(small['lru_wx'][j])],
                                 axis=1).astype(BF16),
            ba=small['lru_ba'][j][None], bx=small['lru_bx'][j][None], lam=small['lru_lambda'][j][None],
            bf=jnp.pad(small['fox_bf'][j], (0, LANES - 8))[None], qn=small['fox_q_norm'][j],
            kn=small['fox_k_norm'][j])
    else:
        lam, bexp, cexp, ins = s5_prep_fwd(small['s5_lambda_re'][j], small['s5_lambda_im'][j], small['s5_log_dt'][j],
                                           small['s5_b_re'][j], small['s5_b_im'][j], small['s5_c_re'][j],
                                           small['s5_c_im'][j], f"L{i}")
        w['mix'] = dict(
            w_in=full['od_w_in'][j], w_out=full['od_w_out'][j], qn=small['swa_q_norm'][j], kn=small['swa_k_norm'][j],
            sinks=small['swa_sinks'][j], s5_lam=lam, s5_bexp=bexp, s5_cexp=cexp, s5_ins=ins,
            s5_d=full['s5_d'][j].astype(F32)[None], glu_w=full['s5_glu_w'][j], glu_b=full['s5_glu_b'][j].astype(F32)[None])
    return w


def layer_fwd(x, p_i, w, ffnw, i):
    tag = f"L{i}"
    sv = {}
    wgu, wd = ffnw
    x0, n1 = add_norm(x, None, w['g1'], f"norm1_{tag}")
    y1, sv['ffn1'] = ffn_fwd(n1, wgu, wd, *w['ffn1'], f"1_{tag}")
    x1, hm = add_norm(x0, y1, w['gm'], f"normm_{tag}")
    if i % 2 == 0:
        ym, sv['mix'] = even_mixer_fwd(hm, w['mix'], tag)
    else:
        ym, sv['mix'] = odd_mixer_fwd(hm, w['mix'], tag)
    x2, n2 = add_norm(x1, ym, w['g2'], f"norm2_{tag}")
    y2, sv['ffn2'] = ffn_fwd(n2, wgu, wd, *w['ffn2'], f"2_{tag}")
    x3, ng = add_norm(x2, y2, w['gg'], f"normg_{tag}")
    gpre = mm(ng, w['ple_gate_w'], name=f"ple_gate_{tag}")
    epre = mm(p_i, w['ple_w'], name=f"ple_emb_{tag}")
    D = x.shape[1]
    x4 = rowwise(lambda a, b, c, pn: ([_ple_out(a, b, c, pn)], []), [x3, gpre, epre], [w['gp']], outs=[(D, F32)],
                 name=f"ple_out_{tag}")[0]
    sv.update(x0=x0, x1=x1, x2=x2, x3=x3, ng=ng, gpre=gpre, epre=epre, p=p_i)
    return x4, sv


def layer_bwd(dx4, sv, w, ffnw, gbuf, i):
    tag = f"L{i}"
    D = dx4.shape[1]
    g = {}
    wgu, wd = ffnw

    def f_ple(a, b, c, d, pn):
        da, db, dc, dpn = _vjp(_ple_out, (a, b, c, pn), d)
        return [db, dc], [dpn]

    dgpre, depre, dgp = rowwise(f_ple, [sv['x3'], sv['gpre'], sv['epre'], dx4], [w['gp']],
                                outs=[(D, BF16), (D, BF16)], accs=[(1, D)], name=f"ple_out_bwd_{tag}")
    g['gp'] = dgp[0]
    g['ple_w'] = mm(sv['p'], depre, "tn", name=f"ple_emb_dw_{tag}")
    g['ple_gate_w'] = mm(sv['ng'], dgpre, "tn", name=f"ple_gate_dw_{tag}")
    dng = mm(dgpre, w['ple_gate_w'], "nt", name=f"ple_gate_dx_{tag}")
    dx3, dgg = norm_bwd(sv['x3'], w['gg'], dng, dx4, f"normg_bwd_{tag}")
    g['gg'] = dgg[0]
    dn2, gbuf = ffn_bwd(dx3, sv['ffn2'], wgu, wd, *w['ffn2'], gbuf, f"2_{tag}")
    dx2, dg2 = norm_bwd(sv['x2'], w['g2'], dn2, dx3, f"norm2_bwd_{tag}")
    g['g2'] = dg2[0]
    if i % 2 == 0:
        dhm, g['mix'] = even_mixer_bwd(dx2, sv['mix'], w['mix'], tag)
    else:
        dhm, g['mix'] = odd_mixer_bwd(dx2, sv['mix'], w['mix'], tag)
    dx1, dgm = norm_bwd(sv['x1'], w['gm'], dhm, dx2, f"normm_bwd_{tag}")
    g['gm'] = dgm[0]
    dn1, gbuf = ffn_bwd(dx1, sv['ffn1'], wgu, wd, *w['ffn1'], gbuf, f"1_{tag}")
    dx0, dg1 = norm_bwd(sv['x0'], w['g1'], dn1, dx1, f"norm1_bwd_{tag}")
    g['g1'] = dg1[0]
    return dx0, g, gbuf


def _collect_grads(layer_grads, depth):
    st = lambda xs: jnp.stack(xs)
    G = {}
    L = layer_grads
    G['ffn1_norm'] = st([g['g1'] for g in L])
    G['mix_norm'] = st([g['gm'] for g in L])
    G['ffn2_norm'] = st([g['g2'] for g in L])
    G['ple_norm'] = st([g['gp'] for g in L])
    G['ple_gate_norm'] = st([g['gg'] for g in L])
    G['ple_w'] = st([g['ple_w'] for g in L])
    G['ple_gate_w'] = st([g['ple_gate_w'] for g in L])
    ev = [L[i]['mix'] for i in range(0, depth, 2)]
    od = [L[i]['mix'] for i in range(1, depth, 2)]
    G['ev_w_in'] = st([m['w_in'][:, :2568] for m in ev])
    G['ev_w_out'] = st([m['w_out'] for m in ev])
    G['lru_conv_w'] = st([m['conv_w'] for m in ev])
    G['lru_conv_b'] = st([m['conv_b'] for m in ev])
    G['lru_wa'] = st([_block_diag_take(m['w_ax'][:, :512], LRU_BLOCKS) for m in ev])
    G['lru_wx'] = st([_block_diag_take(m['w_ax'][:, 512:], LRU_BLOCKS) for m in ev])
    G['lru_ba'] = st([m['ba'] for m in ev])
    G['lru_bx'] = st([m['bx'] for m in ev])
    G['lru_lambda'] = st([m['lam'] for m in ev])
    G['fox_bf'] = st([m['bf'] for m in ev])
    G['fox_q_norm'] = st([m['qn'] for m in ev])
    G['fox_k_norm'] = st([m['kn'] for m in ev])
    G['od_w_in'] = st([m['w_in'] for m in od])
    G['od_w_out'] = st([m['w_out'] for m in od])
    G['swa_q_norm'] = st([m['qn'] for m in od])
    G['swa_k_norm'] = st([m['kn'] for m in od])
    G['swa_sinks'] = st([m['sinks'] for m in od])
    G['s5_lambda_re'] = st([m['s5']['lre'] for m in od])
    G['s5_lambda_im'] = st([m['s5']['lim'] for m in od])
    G['s5_log_dt'] = st([m['s5']['ldt'] for m in od])
    G['s5_b_re'] = st([m['s5']['bre'] for m in od])
    G['s5_b_im'] = st([m['s5']['bim'] for m in od])
    G['s5_c_re'] = st([m['s5']['cre'] for m in od])
    G['s5_c_im'] = st([m['s5']['cim'] for m in od])
    G['s5_d'] = st([m['s5_d'] for m in od])
    G['s5_glu_w'] = st([m['glu_w'] for m in od])
    G['s5_glu_b'] = st([m['glu_b'] for m in od])
    return G


def local_step(x, p, target, ffnw, full, small):
    depth = p.shape[0]
    S, D = x.shape
    ws = [_layer_weights(full, small, i, depth) for i in range(depth)]
    saved = []
    xi = x
    for i in range(depth):
        xi, sv = layer_fwd(xi, p[i], ws[i], ffnw, i)
        saved.append(sv)

    def f_loss(y, t):
        e = y - t
        return [e * (1.0 / D)], [0.5 * jnp.sum(jnp.mean(e * e, axis=-1, keepdims=True), axis=0, keepdims=True)]

    dx, loss = rowwise(f_loss, [xi, target], outs=[(D, F32)], accs=[(1, 1)], name="loss")
    grads = [None] * depth
    gbuf = (None, None)
    for i in reversed(range(depth)):
        dx, grads[i], gbuf = layer_bwd(dx, saved[i], ws[i], ffnw, gbuf, i)
        if i % 2 == 1:
            m = grads[i]['mix']
            m['s5'] = s5_prep_bwd(ws[i]['mix']['s5_ins'], m['s5_lam'], m['s5_bexp'], m['s5_cexp'], f"L{i}")
    return loss[0, 0], dx, gbuf, _collect_grads(grads, depth)


def kernel(x, p, ffn1_norm, ffn1_wg, ffn1_wu, ffn1_wd, mix_norm, ffn2_norm, ffn2_wg, ffn2_wu, ffn2_wd, ple_w, ple_norm, ple_gate_norm, ple_gate_w, ev_w_in, lru_conv_w, lru_conv_b, lru_wa, lru_ba, lru_wx, lru_bx, lru_lambda, fox_bf, fox_q_norm, fox_k_norm, ev_w_out, od_w_in, swa_q_norm, swa_k_norm, swa_sinks, s5_lambda_re, s5_lambda_im, s5_log_dt, s5_b_re, s5_b_im, s5_c_re, s5_c_im, s5_d, s5_glu_w, s5_glu_b, od_w_out, loss_target, m_ffn1_norm, m_ffn1_wg, m_ffn1_wu, m_ffn1_wd, m_mix_norm, m_ffn2_norm, m_ffn2_wg, m_ffn2_wu, m_ffn2_wd, m_ple_w, m_ple_norm, m_ple_gate_norm, m_ple_gate_w, m_ev_w_in, m_lru_conv_w, m_lru_conv_b, m_lru_wa, m_lru_ba, m_lru_wx, m_lru_bx, m_lru_lambda, m_fox_bf, m_fox_q_norm, m_fox_k_norm, m_ev_w_out, m_od_w_in, m_swa_q_norm, m_swa_k_norm, m_swa_sinks, m_s5_lambda_re, m_s5_lambda_im, m_s5_log_dt, m_s5_b_re, m_s5_b_im, m_s5_c_re, m_s5_c_im, m_s5_d, m_s5_glu_w, m_s5_glu_b, m_od_w_out, v_ffn1_norm, v_ffn1_wg, v_ffn1_wu, v_ffn1_wd, v_mix_norm, v_ffn2_norm, v_ffn2_wg, v_ffn2_wu, v_ffn2_wd, v_ple_w, v_ple_norm, v_ple_gate_norm, v_ple_gate_w, v_ev_w_in, v_lru_conv_w, v_lru_conv_b, v_lru_wa, v_lru_ba, v_lru_wx, v_lru_bx, v_lru_lambda, v_fox_bf, v_fox_q_norm, v_fox_k_norm, v_ev_w_out, v_od_w_in, v_swa_q_norm, v_swa_k_norm, v_swa_sinks, v_s5_lambda_re, v_s5_lambda_im, v_s5_log_dt, v_s5_b_re, v_s5_b_im, v_s5_c_re, v_s5_c_im, v_s5_d, v_s5_glu_w, v_s5_glu_b, v_od_w_out):
    args = locals()
    wts = {n: args[n] for n in WEIGHTS}
    ms = {n: args["m_" + n] for n in WEIGHTS}
    vs = {n: args["v_" + n] for n in WEIGHTS}
    shapes = {n: wts[n].shape for n in WEIGHTS}

    depth = p.shape[0]
    gw = gather_weights({n: wts[n] for n in SHARDED})
    small = {n: wts[n] for n in REPLICATED}
    loss, dx, (gwgu, gwd), G = local_step(x[0], p[:, 0], loss_target[0], (gw['wgu'], gw['wd']),
                                          full_weights(gw, depth), small)
    loss = lax.psum(loss, ("x", "y", "c"))

    gsh = ungroup(reduce_sharded(grad_groups(gwgu, gwd, G)), shapes)
    full_shapes = {n: (G[n].shape if n in EXACT_SHARDED else shapes[n]) for n in SMALL_GRADS}
    flat_g = _flatten_small(G, full_shapes)
    g8 = gather_devices(flat_g, "gather_small_grads").reshape((N_DEV,) + flat_g.shape)
    wf, mf, vf = (_flatten_small({n: t[n] for n in REPLICATED}, full_shapes) for t in (wts, ms, vs))

    def f_small(g0, g1, g2, g3, g4, g5, g6, g7, w_, m_, v_):
        gsum = ((g0 + g1) + (g2 + g3)) + ((g4 + g5) + (g6 + g7))
        return [gsum] + list(_adam(w_, gsum, m_, v_)), []

    gs_f, ds_f, ms_f, vs_f = rowwise(f_small, [g8[d] for d in range(N_DEV)] + [wf, mf, vf],
                                     outs=[(LANES, F32)] * 4, name="adam_small")
    out_g, out_d, out_m, out_v = {}, {}, {}, {}
    for dst, flat in ((out_g, gs_f), (out_d, ds_f), (out_m, ms_f), (out_v, vs_f)):
        dst.update(_unflatten_small(flat, full_shapes))
    for n in EXACT_SHARDED:
        width = shapes[n][SHARD_AXIS[n]]
        gsh[n] = lax.dynamic_slice_in_dim(out_g[n], _chip() * width, width, axis=SHARD_AXIS[n])
    for n in SHARDED:
        out_g[n] = gsh[n]
        out_d[n], out_m[n], out_v[n] = adam_2d(wts[n], gsh[n], ms[n], vs[n], f"adam_{n}")
    return (loss, dx[None], *[out_g[n] for n in WEIGHTS], *[out_d[n] for n in WEIGHTS],
            *[out_m[n] for n in WEIGHTS], *[out_v[n] for n in WEIGHTS])
```

```python
import functools
import math

import jax
import jax.numpy as jnp
from jax import lax
from jax.experimental import pallas as pl
from jax.experimental.pallas import tpu as pltpu

F32 = jnp.float32
BF16 = jnp.bfloat16
MXU_DTYPE = BF16
HI = lax.Precision.HIGHEST
MESH = pl.DeviceIdType.MESH

VMEM_LIMIT_BYTES = 56 * 1024 * 1024
ROW_TILE_BYTES = 5 * 1024 * 1024
MM_VMEM_BYTES = 40 * 1024 * 1024
MM_TILE_M = 1024
MM_TILE_N = 1408
FLAT_W = 2048
LANES = 128
SUBLANES = 8

HEAD_DIM = 64
LRU_BLOCKS = 8
LRU_CONV = 4
LRU_C = 8.0
SWA_WINDOW = 128
SWA_GROUP = 4
S5_GROUP = 16
S5_GROUPS = 32
S5_STATE = 64
ROPE_THETA = 10000.0
EPS = 1e-6
MACARON = 0.5
NEG = -1e30

ADAM_LR = 0.001
ADAM_B1 = 0.9
ADAM_B2 = 0.999
ADAM_EPS = 1e-08
ADAM_WD = 0.01
ADAM_STEP = 10

N_CHIPS = 4
N_DEV = 8


def _pick(n, cands):
    for c in cands:
        if n % c == 0:
            return c
    return n


def _tile(n, cap, unit):
    best = None
    for t in range(unit, min(n, cap) + 1, unit):
        if n % t == 0:
            best = t
    return n if best is None else best


def _params(sem=None):
    return pltpu.CompilerParams(dimension_semantics=sem, vmem_limit_bytes=VMEM_LIMIT_BYTES)


def rowwise(fn, rows, consts=(), outs=(), accs=(), name="rowwise", periods=None):
    rows, consts = list(rows), list(consts)
    n_r, n_c, n_o, n_a = len(rows), len(consts), len(outs), len(accs)
    R = rows[0].shape[0]
    periods = list(periods) if periods is not None else [None] * n_r
    per_row = sum(max(r.shape[1], LANES) * 4 for r in rows) + sum(max(f, LANES) * 4 for f, _ in outs)
    limit = min([R] + [p for p in periods if p is not None])
    tr = limit
    for c in (1024, 512, 256, 128, 64, 32, 16):
        if c <= limit and limit % c == 0 and R % c == 0 and c * per_row <= ROW_TILE_BYTES:
            tr = c
            break

    def row_map(period):
        if period is None:
            return lambda i: (i, 0)
        nb = period // tr
        return lambda i: (i % nb, 0)

    in_specs = [pl.BlockSpec((tr, r.shape[1]), row_map(p)) for r, p in zip(rows, periods)]
    in_specs += [pl.BlockSpec(c.shape, lambda i: (0, 0)) for c in consts]
    out_shape = [jax.ShapeDtypeStruct((R, f), dt) for f, dt in outs]
    out_shape += [jax.ShapeDtypeStruct(tuple(s), F32) for s in accs]
    out_specs = [pl.BlockSpec((tr, f), lambda i: (i, 0)) for f, _ in outs]
    out_specs += [pl.BlockSpec(tuple(s), lambda i: (0, 0)) for s in accs]

    def body(*refs):
        ins = [r[...] for r in refs[:n_r + n_c]]
        o_refs = refs[n_r + n_c:n_r + n_c + n_o]
        a_refs = refs[n_r + n_c + n_o:]
        ro, ra = fn(*ins)
        for ref, val in zip(o_refs, ro):
            ref[...] = val.astype(ref.dtype)
        if n_a:
            @pl.when(pl.program_id(0) == 0)
            def _():
                for ref in a_refs:
                    ref[...] = jnp.zeros(ref.shape, ref.dtype)
            for ref, val in zip(a_refs, ra):
                ref[...] += val.astype(F32)

    res = pl.pallas_call(
        body, grid=(R // tr,), in_specs=in_specs, out_specs=out_specs, out_shape=out_shape,
        name=name, compiler_params=_params(("arbitrary",)),
    )(*rows, *consts)
    return list(res)


def whole(fn, ins, outs, name="whole"):
    n_i = len(ins)

    def body(*refs):
        vals = fn(*[r[...] for r in refs[:n_i]])
        for ref, val in zip(refs[n_i:], vals):
            ref[...] = val.astype(ref.dtype)

    res = pl.pallas_call(
        body, out_shape=[jax.ShapeDtypeStruct(tuple(s), dt) for s, dt in outs],
        in_specs=[pl.BlockSpec(memory_space=pltpu.VMEM)] * n_i,
        out_specs=[pl.BlockSpec(memory_space=pltpu.VMEM)] * len(outs),
        name=name, compiler_params=_params(),
    )(*ins)
    return list(res)


_DOT_DIMS = {
    "nn": (((1,), (0,)), ((), ())),
    "nt": (((1,), (1,)), ((), ())),
    "tn": (((0,), (0,)), ((), ())),
}


def mm(a, b, mode="nn", out_dtype=F32, name="mm"):
    if mode == "nn":
        (M, K), (K2, N) = a.shape, b.shape
    elif mode == "nt":
        (M, K), (N, K2) = a.shape, b.shape
    else:
        (K, M), (K2, N) = a.shape, b.shape
    assert K == K2, (mode, a.shape, b.shape)
    tn = _tile(N, MM_TILE_N, LANES)
    if mode == "tn":
        tm, tk = _tile(M, MM_TILE_M, LANES), _tile(K, MM_TILE_M, 2 * SUBLANES)
    else:
        tm, tk = _tile(M, MM_TILE_M, 2 * SUBLANES), _tile(K, MM_TILE_N, LANES)

    def vmem_bytes(tm_, tk_):
        return (2 * (tm_ * tk_ * a.dtype.itemsize + tk_ * tn * b.dtype.itemsize
                     + tm_ * tn * jnp.dtype(out_dtype).itemsize) + tm_ * tn * 4)

    while vmem_bytes(tm, tk) > MM_VMEM_BYTES and tk % (2 * LANES) == 0 and K % (tk // 2) == 0:
        tk //= 2
    while vmem_bytes(tm, tk) > MM_VMEM_BYTES and tm % (2 * LANES) == 0 and M % (tm // 2) == 0:
        tm //= 2
    if mode == "tn":
        a_spec = pl.BlockSpec((tk, tm), lambda i, j, k: (k, i))
    else:
        a_spec = pl.BlockSpec((tm, tk), lambda i, j, k: (i, k))
    if mode == "nt":
        b_spec = pl.BlockSpec((tn, tk), lambda i, j, k: (j, k))
    else:
        b_spec = pl.BlockSpec((tk, tn), lambda i, j, k: (k, j))
    nk = K // tk
    dims = _DOT_DIMS[mode]

    def dot(a_ref, b_ref):
        return lax.dot_general(a_ref[...].astype(MXU_DTYPE), b_ref[...].astype(MXU_DTYPE), dims,
                               preferred_element_type=F32)

    def body_one(a_ref, b_ref, o_ref):
        o_ref[...] = dot(a_ref, b_ref).astype(o_ref.dtype)

    def body_acc(a_ref, b_ref, o_ref, acc_ref):
        k = pl.program_id(2)

        @pl.when(k == 0)
        def _():
            acc_ref[...] = dot(a_ref, b_ref)

        @pl.when(k > 0)
        def _():
            acc_ref[...] += dot(a_ref, b_ref)

        @pl.when(k == nk - 1)
        def _():
            o_ref[...] = acc_ref[...].astype(o_ref.dtype)

    return pl.pallas_call(
        body_one if nk == 1 else body_acc, grid=(M // tm, N // tn, nk), in_specs=[a_spec, b_spec],
        out_specs=pl.BlockSpec((tm, tn), lambda i, j, k: (i, j)),
        out_shape=jax.ShapeDtypeStruct((M, N), out_dtype),
        scratch_shapes=[] if nk == 1 else [pltpu.VMEM((tm, tn), F32)],
        name=name, compiler_params=_params(("parallel", "parallel", "arbitrary")),
    )(a, b)


def _roll_rows(x, d, reverse):
    return pltpu.roll(x, (SUBLANES - d) if reverse else d, 0)


def scan_real(a, b, reverse=False, name="scan_real"):
    S, W = b.shape
    cw = _pick(W, (256, 128))
    n_tiles = S // SUBLANES

    def body(a_ref, b_ref, o_ref):
        row = lax.broadcasted_iota(jnp.int32, (SUBLANES, cw), 0)
        edge = 0 if reverse else SUBLANES - 1

        def step(i, carry):
            t = (n_tiles - 1 - i) if reverse else i
            off = pl.multiple_of(t * SUBLANES, SUBLANES)
            A = a_ref[pl.ds(off, SUBLANES), :]
            B = b_ref[pl.ds(off, SUBLANES), :]
            for d in (1, 2, 4):
                m = (row < SUBLANES - d) if reverse else (row >= d)
                B = jnp.where(m, A * _roll_rows(B, d, reverse) + B, B)
                A = jnp.where(m, A * _roll_rows(A, d, reverse), A)
            o_ref[pl.ds(off, SUBLANES), :] = B + A * carry
            at_edge = row == edge
            return (jnp.sum(jnp.where(at_edge, B, 0.0), axis=0, keepdims=True)
                    + jnp.sum(jnp.where(at_edge, A, 0.0), axis=0, keepdims=True) * carry)

        lax.fori_loop(0, n_tiles, step, jnp.zeros((1, cw), F32), unroll=2)

    spec = pl.BlockSpec((S, cw), lambda j: (0, j))
    return pl.pallas_call(
        body, grid=(W // cw,), in_specs=[spec, spec], out_specs=spec,
        out_shape=jax.ShapeDtypeStruct((S, W), F32), name=name, compiler_params=_params(("parallel",)),
    )(a, b)


def scan_cplx(lam, bu, reverse=False, name="scan_cplx"):
    S, C = bu.shape
    half = LANES
    CB = _pick(C, (1024, 512, 256))
    TS = _pick(S, (1024, 512, 256, 128, 64, 32, 16, 8))
    groups = CB // (2 * half)
    n_blocks, n_tiles = S // TS, TS // SUBLANES

    def cmul(ar, ai, br, bi):
        return ar * br - ai * bi, ar * bi + ai * br

    def body(lam_ref, bu_ref, o_ref, carry_ref):
        row = lax.broadcasted_iota(jnp.int32, (SUBLANES, half), 0)

        def edge_row(v):
            return jnp.sum(jnp.where(row == (0 if reverse else SUBLANES - 1), v, 0.0), axis=0, keepdims=True)

        @pl.when(pl.program_id(1) == 0)
        def _():
            carry_ref[...] = jnp.zeros(carry_ref.shape, F32)

        consts = []
        for g in range(groups):
            lr = lam_ref[:, 2 * half * g:2 * half * g + half]
            li = lam_ref[:, 2 * half * g + half:2 * half * (g + 1)]
            if reverse:
                li = -li
            l1 = (lr, li)
            l2 = cmul(*l1, *l1)
            l4 = cmul(*l2, *l2)
            pr = jnp.zeros((SUBLANES, half), F32)
            pi = jnp.zeros((SUBLANES, half), F32)
            p = l1
            for r in range(SUBLANES):
                sel = row == ((SUBLANES - 1 - r) if reverse else r)
                pr = jnp.where(sel, p[0], pr)
                pi = jnp.where(sel, p[1], pi)
                p = cmul(*p, *l1)
            consts.append((l1, l2, l4, pr, pi, edge_row(pr), edge_row(pi)))

        def step(i, carry):
            t = (n_tiles - 1 - i) if reverse else i
            off = pl.multiple_of(t * SUBLANES, SUBLANES)
            out = []
            for g in range(groups):
                l1, l2, l4, pr, pi, p8r, p8i = consts[g]
                cr, ci = carry[2 * g], carry[2 * g + 1]
                re, im = pl.ds(2 * half * g, half), pl.ds(2 * half * g + half, half)
                Br = bu_ref[pl.ds(off, SUBLANES), re]
                Bi = bu_ref[pl.ds(off, SUBLANES), im]
                for d, (qr, qi) in ((1, l1), (2, l2), (4, l4)):
                    m = (row < SUBLANES - d) if reverse else (row >= d)
                    sr, si = _roll_rows(Br, d, reverse), _roll_rows(Bi, d, reverse)
                    nr = jnp.where(m, Br + qr * sr - qi * si, Br)
                    ni = jnp.where(m, Bi + qr * si + qi * sr, Bi)
                    Br, Bi = nr, ni
                o_ref[pl.ds(off, SUBLANES), re] = Br + pr * cr - pi * ci
                o_ref[pl.ds(off, SUBLANES), im] = Bi + pr * ci + pi * cr
                er, ei = edge_row(Br), edge_row(Bi)
                out += [er + p8r * cr - p8i * ci, ei + p8r * ci + p8i * cr]
            return tuple(out)

        carry0 = tuple(carry_ref[:, pl.ds(half * k, half)] for k in range(2 * groups))
        carry1 = lax.fori_loop(0, n_tiles, step, carry0, unroll=2)
        for k in range(2 * groups):
            carry_ref[:, pl.ds(half * k, half)] = carry1[k]

    def rows(j, t):
        return ((n_blocks - 1 - t) if reverse else t, j)

    spec = pl.BlockSpec((TS, CB), rows)
    return pl.pallas_call(
        body, grid=(C // CB, n_blocks), in_specs=[pl.BlockSpec((1, CB), lambda j, t: (0, j)), spec],
        out_specs=spec, out_shape=jax.ShapeDtypeStruct((S, C), F32), scratch_shapes=[pltpu.VMEM((1, CB), F32)],
        name=name, compiler_params=_params(("parallel", "arbitrary")),
    )(lam, bu)


ATTN_HEADS_PER_STEP = 2


def _attn_tile(S, window):
    if window is None:
        return _pick(S, (512, 256, 128))
    return max(window, _pick(S, (256, 128)))


def _attn_valid(q_blk, k_blk, T, window):
    kpos = k_blk * T + lax.broadcasted_iota(jnp.int32, (T, T), 0)
    qpos = q_blk * T + lax.broadcasted_iota(jnp.int32, (T, T), 1)
    valid = kpos <= qpos
    if window is not None:
        valid = valid & (qpos - kpos < window)
    return valid


def attn_fwd(q, k, v, sink, cq=None, ck=None, window=None, name="attn_fwd"):
    H, S, Dh = q.shape
    G = H // k.shape[0]
    HP = ATTN_HEADS_PER_STEP
    assert H % HP == 0 and (G == 1 or G % HP == 0)
    KP = HP if G == 1 else 1
    T = _attn_tile(S, window)
    nq = S // T
    nks = nq if window is None else 2
    scale = Dh ** -0.5
    bias = cq is not None

    def kv_block(i, j):
        return jnp.minimum(j, i) if window is None else jnp.maximum(i - 1 + j, 0)

    def body(*refs):
        if bias:
            q_ref, k_ref, v_ref, s_ref, cq_ref, ck_ref, o_ref, lse_ref, m_scr, l_scr, acc_scr = refs
        else:
            q_ref, k_ref, v_ref, s_ref, o_ref, lse_ref, m_scr, l_scr, acc_scr = refs
        i, j = pl.program_id(1), pl.program_id(2)

        @pl.when(j == 0)
        def _():
            m_scr[...] = jnp.zeros(m_scr.shape, F32) + s_ref[...]
            l_scr[...] = jnp.ones(l_scr.shape, F32)
            acc_scr[...] = jnp.zeros(acc_scr.shape, F32)

        def block(masked):
            valid = _attn_valid(i, kv_block(i, j), T, window) if masked else None
            for b in range(HP):
                kvb = b if G == 1 else 0
                s = _dotf(k_ref[kvb], q_ref[b], "nt") * scale
                if bias:
                    s = s + cq_ref[b] - ck_ref[b]
                if masked:
                    s = jnp.where(valid, s, NEG)
                m_old = m_scr[b]
                m_new = jnp.maximum(m_old, jnp.max(s, axis=0, keepdims=True))
                alpha = jnp.exp(m_old - m_new)
                p = jnp.exp(s - m_new)
                l_scr[b] = alpha * l_scr[b] + jnp.sum(p, axis=0, keepdims=True)
                acc_scr[b] = alpha * acc_scr[b] + _dotf(v_ref[kvb], p, "tn")
                m_scr[b] = m_new

        if window is None:
            pl.when(j < i)(lambda: block(False))
            pl.when(j == i)(lambda: block(True))
        else:
            pl.when(i - 1 + j >= 0)(lambda: block(True))

        @pl.when(j == nks - 1)
        def _():
            o_ref[...] = acc_scr[...] / l_scr[...]
            lse_ref[...] = m_scr[...] + jnp.log(l_scr[...])

    def kv_map(hp, i, j):
        return (hp if G == 1 else (hp * HP) // G, kv_block(i, j), 0)

    in_specs = [
        pl.BlockSpec((HP, T, Dh), lambda hp, i, j: (hp, i, 0)),
        pl.BlockSpec((KP, T, Dh), kv_map),
        pl.BlockSpec((KP, T, Dh), kv_map),
        pl.BlockSpec((HP, 1, 1), lambda hp, i, j: (hp, 0, 0)),
    ]
    args = [q, k, v, sink]
    if bias:
        in_specs += [pl.BlockSpec((HP, 1, T), lambda hp, i, j: (hp, 0, i)),
                     pl.BlockSpec((HP, T, 1), lambda hp, i, j: (hp, kv_block(i, j), 0))]
        args += [cq, ck]
    return pl.pallas_call(
        body, grid=(H // HP, nq, nks), in_specs=in_specs,
        out_specs=[pl.BlockSpec((HP, Dh, T), lambda hp, i, j: (hp, 0, i)),
                   pl.BlockSpec((HP, 1, T), lambda hp, i, j: (hp, 0, i))],
        out_shape=[jax.ShapeDtypeStruct((H, Dh, S), F32), jax.ShapeDtypeStruct((H, 1, S), F32)],
        scratch_shapes=[pltpu.VMEM((HP, 1, T), F32), pltpu.VMEM((HP, 1, T), F32), pltpu.VMEM((HP, Dh, T), F32)],
        name=name, compiler_params=_params(("parallel", "parallel", "arbitrary")),
    )(*args)


def attn_bwd(q, k, v, lse, do, delta, cq=None, ck=None, window=None, name="attn_bwd"):
    H, S, Dh = q.shape
    KVH = k.shape[0]
    G = H // KVH
    HP = ATTN_HEADS_PER_STEP
    assert H % HP == 0 and (G == 1 or G % HP == 0)
    pair_kv = G == 1
    KP = HP if pair_kv else 1
    T = _attn_tile(S, window)
    nq = S // T
    nqs = nq if window is None else 2
    scale = Dh ** -0.5
    bias = cq is not None
    assert not bias or G == 1

    def q_block(kb, j):
        return jnp.maximum(j, kb) if window is None else jnp.minimum(kb + j, nq - 1)

    def body(*refs):
        if bias:
            (q_ref, k_ref, v_ref, lse_ref, do_ref, dl_ref, cq_ref, ck_ref,
             dq_ref, dk_ref, dv_ref, dcq_ref, dck_ref) = refs
        else:
            q_ref, k_ref, v_ref, lse_ref, do_ref, dl_ref, dq_ref, dk_ref, dv_ref = refs
        kb, gp, j = pl.program_id(1), pl.program_id(2), pl.program_id(3)

        @pl.when((gp == 0) & (j == 0))
        def _():
            dk_ref[...] = jnp.zeros(dk_ref.shape, F32)
            dv_ref[...] = jnp.zeros(dv_ref.shape, F32)
            if bias:
                dck_ref[...] = jnp.zeros(dck_ref.shape, F32)

        @pl.when((kb == 0) & (gp == 0) & (j == 0))
        def _():
            dq_ref[...] = jnp.zeros(dq_ref.shape, F32)
            if bias:
                dcq_ref[...] = jnp.zeros(dcq_ref.shape, F32)

        def block(masked):
            qi = q_block(kb, j)
            off = pl.multiple_of(qi * T, T)
            valid = _attn_valid(qi, kb, T, window) if masked else None
            for b in range(HP):
                kvb = b if pair_kv else 0
                g = 0 if pair_kv else gp * HP + b
                qb, kk, vv = q_ref[b].astype(MXU_DTYPE), k_ref[kvb].astype(MXU_DTYPE), v_ref[kvb].astype(MXU_DTYPE)
                dob = do_ref[b].astype(MXU_DTYPE)
                s = _dotf(kk, qb, "nt") * scale
                if bias:
                    s = s + cq_ref[b] - ck_ref[b]
                if masked:
                    s = jnp.where(valid, s, NEG)
                p = jnp.exp(s - lse_ref[b])
                dv_ref[kvb] += _dotf(p, dob, "nt")
                ds = p * (_dotf(vv, dob) - dl_ref[b])
                dsb = ds.astype(MXU_DTYPE)
                dk_ref[kvb] += scale * _dotf(dsb, qb)
                dq_ref[kvb, g, pl.ds(off, T), :] += scale * _dotf(dsb, kk, "tn")
                if bias:
                    dcq_ref[kvb, g, :, pl.ds(off, T)] += jnp.sum(ds, axis=0, keepdims=True)
                    dck_ref[kvb] -= jnp.sum(ds, axis=1, keepdims=True)

        if window is None:
            pl.when(j > kb)(lambda: block(False))
            pl.when(j == kb)(lambda: block(True))
        else:
            pl.when(kb + j <= nq - 1)(lambda: block(True))

    def qmap(kvp, kb, gp, j):
        return (kvp if pair_kv else (kvp * G) // HP + gp, q_block(kb, j), 0)

    def qmap_t(kvp, kb, gp, j):
        return (kvp if pair_kv else (kvp * G) // HP + gp, 0, q_block(kb, j))

    in_specs = [
        pl.BlockSpec((HP, T, Dh), qmap),
        pl.BlockSpec((KP, T, Dh), lambda kvp, kb, gp, j: (kvp, kb, 0)),
        pl.BlockSpec((KP, T, Dh), lambda kvp, kb, gp, j: (kvp, kb, 0)),
        pl.BlockSpec((HP, 1, T), qmap_t),
        pl.BlockSpec((HP, Dh, T), qmap_t),
        pl.BlockSpec((HP, 1, T), qmap_t),
    ]
    args = [q, k, v, lse, do, delta]
    out_specs = [
        pl.BlockSpec((KP, G, S, Dh), lambda kvp, kb, gp, j: (kvp, 0, 0, 0)),
        pl.BlockSpec((KP, T, Dh), lambda kvp, kb, gp, j: (kvp, kb, 0)),
        pl.BlockSpec((KP, T, Dh), lambda kvp, kb, gp, j: (kvp, kb, 0)),
    ]
    out_shape = [jax.ShapeDtypeStruct((KVH, G, S, Dh), F32), jax.ShapeDtypeStruct((KVH, S, Dh), F32),
                 jax.ShapeDtypeStruct((KVH, S, Dh), F32)]
    if bias:
        in_specs += [pl.BlockSpec((HP, 1, T), qmap_t),
                     pl.BlockSpec((HP, T, 1), lambda kvp, kb, gp, j: (kvp, kb, 0))]
        args += [cq, ck]
        out_specs += [pl.BlockSpec((KP, G, 1, S), lambda kvp, kb, gp, j: (kvp, 0, 0, 0)),
                      pl.BlockSpec((KP, T, 1), lambda kvp, kb, gp, j: (kvp, kb, 0))]
        out_shape += [jax.ShapeDtypeStruct((KVH, G, 1, S), F32), jax.ShapeDtypeStruct((KVH, S, 1), F32)]
    res = pl.pallas_call(
        body, grid=(KVH // KP, nq, 1 if pair_kv else G // HP, nqs), in_specs=in_specs, out_specs=out_specs,
        out_shape=out_shape, name=name, compiler_params=_params(("arbitrary", "arbitrary", "arbitrary", "arbitrary")),
    )(*args)
    dq = res[0].reshape(H, S, Dh)
    if bias:
        return dq, res[1], res[2], res[3].reshape(H, 1, S), res[4]
    return dq, res[1], res[2]


def _rms(x, g):
    return x * lax.rsqrt(jnp.mean(x * x, axis=-1, keepdims=True) + EPS) * g


def _sigmoid(x):
    return 1.0 / (1.0 + jnp.exp(-x))


def _softplus(x):
    return jnp.maximum(x, 0.0) + jnp.log(1.0 + jnp.exp(-jnp.abs(x)))


def _log_sigmoid(x):
    return jnp.minimum(x, 0.0) - jnp.log(1.0 + jnp.exp(-jnp.abs(x)))


def _gelu(x):
    return 0.5 * x * (1.0 + jnp.tanh(math.sqrt(2.0 / math.pi) * (x + 0.044715 * (x * x * x))))


def _silu(x):
    return x * _sigmoid(x)


def _ffn_act(gu):
    f = gu.shape[1] // 2
    return MACARON * _silu(gu[:, :f]) * gu[:, f:]


def _qk_prep(rope):
    def f(x, *rest):
        if rope:
            cos, sin, g, rot = rest
        else:
            (g,) = rest
        y = _rms(x, g)
        if rope:
            y = y * cos + jnp.dot(y, rot, precision=HI, preferred_element_type=F32) * sin
        return y
    return f


def _lru_gates(pre, xc, ba, bx, lam):
    w = xc.shape[1]
    r = _sigmoid(pre[:, :w] + ba)
    i = _sigmoid(pre[:, w:] + bx)
    log_a = -LRU_C * r * _softplus(lam)
    a = jnp.exp(log_a)
    b = jnp.sqrt(1.0 - jnp.exp(2.0 * log_a)) * (i * xc)
    return a, b


def _lru_conv(x0, x1, x2, x3, w0, w1, w2, w3, cb):
    return cb + x0 * w0 + x1 * w1 + x2 * w2 + x3 * w3


def _s5_params(lre, lim, ldt, gsel, bre, bim):
    dt = jnp.sum(gsel * jnp.exp(ldt), axis=1, keepdims=True)
    er = jnp.exp(lre * dt)
    ang = lim * dt
    lbr, lbi = er * jnp.cos(ang), er * jnp.sin(ang)
    nr, ni = lbr - 1.0, lbi
    den = lre * lre + lim * lim
    fr, fi = (nr * lre + ni * lim) / den, (ni * lre - nr * lim) / den
    return lbr, lbi, fr * bre - fi * bim, fr * bim + fi * bre


def _s5_out(yssm, u, d):
    return _gelu(yssm + d * u)


def _glu(z, gl, gb):
    return z * _sigmoid(gl + gb)


def _ple_out(x, gpre, epre, pn):
    return x + _sigmoid(gpre) * _rms(epre, pn)


def _vjp(fn, args, cots):
    _, pull = jax.vjp(fn, *args)
    return pull(cots)


def add_norm(x, y, g, name):
    D = x.shape[1]
    if y is None:
        return x, rowwise(lambda xv, gv: ([_rms(xv, gv)], []), [x], [g], outs=[(D, BF16)], name=name)[0]
    xn, n = rowwise(lambda xv, yv, gv: ([xv + yv, _rms(xv + yv, gv)], []), [x, y], [g],
                    outs=[(D, F32), (D, BF16)], name=name)
    return xn, n


def norm_bwd(x, g, dn, dx_res, name):
    D = x.shape[1]

    def f(xv, dnv, dxv, gv):
        dx, dg = _vjp(_rms, (xv, gv), dnv)
        return [dxv + dx], [dg]

    return rowwise(f, [x, dn, dx_res], [g], outs=[(D, F32)], accs=[(1, D)], name=name)


def _swiglu(g, u):
    return MACARON * _silu(g) * u


def _dotf(a, b, mode="nn"):
    return lax.dot_general(a.astype(MXU_DTYPE), b.astype(MXU_DTYPE), _DOT_DIMS[mode], preferred_element_type=F32)


def ffn_up(n, wgu, ig, iu, name):
    S, D = n.shape
    C, _, _, Fc = wgu.shape
    tm = _tile(S, MM_TILE_M, 2 * SUBLANES)

    def body(n_ref, wg_ref, wu_ref, g_ref, u_ref, a_ref):
        g = _dotf(n_ref[...], wg_ref[...])
        u = _dotf(n_ref[...], wu_ref[...])
        g_ref[...] = g.astype(g_ref.dtype)
        u_ref[...] = u.astype(u_ref.dtype)
        a_ref[...] = _swiglu(g, u).astype(a_ref.dtype)

    hid = pl.BlockSpec((None, tm, Fc), lambda s, i: (s, i, 0))
    return pl.pallas_call(
        body, grid=(C, S // tm),
        in_specs=[pl.BlockSpec((tm, D), lambda s, i: (i, 0)),
                  pl.BlockSpec((None, None, D, Fc), lambda s, i: (s, ig, 0, 0)),
                  pl.BlockSpec((None, None, D, Fc), lambda s, i: (s, iu, 0, 0))],
        out_specs=[hid, hid, hid], out_shape=[jax.ShapeDtypeStruct((C, S, Fc), BF16)] * 3,
        name=name, compiler_params=_params(("parallel", "parallel")),
    )(n, wgu, wgu)


def ffn_down(act, wd, iw, name):
    C, S, Fc = act.shape
    D = wd.shape[-1]
    tm = _tile(S, MM_TILE_M, 2 * SUBLANES)

    def body(a_ref, w_ref, o_ref):
        s = pl.program_id(1)
        r = _dotf(a_ref[...], w_ref[...])

        @pl.when(s == 0)
        def _():
            o_ref[...] = r

        @pl.when(s > 0)
        def _():
            o_ref[...] += r

    return pl.pallas_call(
        body, grid=(S // tm, C),
        in_specs=[pl.BlockSpec((None, tm, Fc), lambda i, s: (s, i, 0)),
                  pl.BlockSpec((None, None, Fc, D), lambda i, s: (s, iw, 0, 0))],
        out_specs=pl.BlockSpec((tm, D), lambda i, s: (i, 0)), out_shape=jax.ShapeDtypeStruct((S, D), F32),
        name=name, compiler_params=_params(("parallel", "arbitrary")),
    )(act, wd)


def ffn_down_bwd(dy, wd, iw, g, u, name):
    C, S, Fc = g.shape
    D = dy.shape[1]
    tm = _tile(S, MM_TILE_M, 2 * SUBLANES)

    def body(dy_ref, w_ref, g_ref, u_ref, dg_ref, du_ref):
        dact = MACARON * _dotf(dy_ref[...], w_ref[...], "nt")
        g, u = g_ref[...].astype(F32), u_ref[...].astype(F32)
        sg = _sigmoid(g)
        gs = g * sg
        dg_ref[...] = (dact * u * (sg + gs * (1.0 - sg))).astype(dg_ref.dtype)
        du_ref[...] = (dact * gs).astype(du_ref.dtype)

    hid = pl.BlockSpec((None, tm, Fc), lambda s, i: (s, i, 0))
    return pl.pallas_call(
        body, grid=(C, S // tm),
        in_specs=[pl.BlockSpec((tm, D), lambda s, i: (i, 0)),
                  pl.BlockSpec((None, None, Fc, D), lambda s, i: (s, iw, 0, 0)), hid, hid],
        out_specs=[hid, hid], out_shape=[jax.ShapeDtypeStruct((C, S, Fc), BF16)] * 2,
        name=name, compiler_params=_params(("parallel", "parallel")),
    )(dy, wd, g, u)


def ffn_dn(dg, du, wgu, ig, iu, name):
    C, S, Fc = dg.shape
    D = wgu.shape[2]
    tm = _tile(S, MM_TILE_M, 2 * SUBLANES)

    def body(dg_ref, du_ref, wg_ref, wu_ref, o_ref):
        s = pl.program_id(1)
        r = _dotf(dg_ref[...], wg_ref[...], "nt") + _dotf(du_ref[...], wu_ref[...], "nt")

        @pl.when(s == 0)
        def _():
            o_ref[...] = r

        @pl.when(s > 0)
        def _():
            o_ref[...] += r

    hid = pl.BlockSpec((None, tm, Fc), lambda i, s: (s, i, 0))
    return pl.pallas_call(
        body, grid=(S // tm, C),
        in_specs=[hid, hid, pl.BlockSpec((None, None, D, Fc), lambda i, s: (s, ig, 0, 0)),
                  pl.BlockSpec((None, None, D, Fc), lambda i, s: (s, iu, 0, 0))],
        out_specs=pl.BlockSpec((tm, D), lambda i, s: (i, 0)), out_shape=jax.ShapeDtypeStruct((S, D), F32),
        name=name, compiler_params=_params(("parallel", "arbitrary")),
    )(dg, du, wgu, wgu)


def ffn_dw(a, d, buf, idx, shape, blocked, name):
    C, P, M, N = shape
    S = d.shape[-2]
    tk = _tile(S, MM_TILE_M, 2 * SUBLANES)

    def body(*refs):
        a_ref, d_ref, o_ref = refs[0], refs[1], refs[-1]
        k = pl.program_id(1)
        r = _dotf(a_ref[...], d_ref[...], "tn")

        @pl.when(k == 0)
        def _():
            o_ref[...] = r

        @pl.when(k > 0)
        def _():
            o_ref[...] += r

    if blocked == "a":
        a_spec = pl.BlockSpec((None, tk, M), lambda s, k: (s, k, 0))
        d_spec = pl.BlockSpec((tk, N), lambda s, k: (k, 0))
    else:
        a_spec = pl.BlockSpec((tk, M), lambda s, k: (k, 0))
        d_spec = pl.BlockSpec((None, tk, N), lambda s, k: (s, k, 0))
    out_spec = pl.BlockSpec((None, None, M, N), lambda s, k: (s, idx, 0, 0))
    out_shape = jax.ShapeDtypeStruct(tuple(shape), F32)
    if buf is None:
        return pl.pallas_call(body, grid=(C, S // tk), in_specs=[a_spec, d_spec], out_specs=out_spec,
                              out_shape=out_shape, name=name, compiler_params=_params(("parallel", "arbitrary")))(a, d)
    return pl.pallas_call(body, grid=(C, S // tk), in_specs=[a_spec, d_spec, pl.BlockSpec(memory_space=pl.ANY)],
                          out_specs=out_spec, out_shape=out_shape, input_output_aliases={2: 0}, name=name,
                          compiler_params=_params(("parallel", "arbitrary")))(a, d, buf)


def ffn_fwd(n, wgu, wd, ig, iu, iw, tag):
    g, u, act = ffn_up(n, wgu, ig, iu, f"ffn_up_{tag}")
    return ffn_down(act, wd, iw, f"ffn_down_{tag}"), (n, g, u, act)


def ffn_bwd(dy, saved, wgu, wd, ig, iu, iw, gbuf, tag):
    n, g, u, act = saved
    gwgu, gwd = gbuf
    dg, du = ffn_down_bwd(dy, wd, iw, g, u, f"ffn_down_bwd_{tag}")
    gwd = ffn_dw(act, dy, gwd, iw, (N_CHIPS,) + wd.shape[1:], "a", f"ffn_dwd_{tag}")
    dn = ffn_dn(dg, du, wgu, ig, iu, f"ffn_dn_{tag}")
    gwgu = ffn_dw(n, dg, gwgu, ig, (N_CHIPS,) + wgu.shape[1:], "d", f"ffn_dwg_{tag}")
    gwgu = ffn_dw(n, du, gwgu, iu, (N_CHIPS,) + wgu.shape[1:], "d", f"ffn_dwu_{tag}")
    return dn, (gwgu, gwd)


def _heads(x, H):
    S = x.shape[0]
    return x.reshape(S, H, HEAD_DIM).transpose(1, 0, 2)


def _unheads(x):
    H, S, _ = x.shape
    return x.transpose(1, 0, 2).reshape(S, H * HEAD_DIM)


def _heads_t(x, H):
    return x.T.reshape(H, HEAD_DIM, x.shape[0])


def _unheads_t(x):
    return x.reshape(x.shape[0] * x.shape[1], x.shape[2]).T


def _shift_down(x, n=1):
    return jnp.pad(x, ((n, 0), (0, 0)))[:x.shape[0]]


def _shift_up(x, n=1):
    return jnp.pad(x, ((0, n), (0, 0)))[n:]


def _block_diag(w):
    B, I, J = w.shape
    eye = jnp.eye(B, dtype=w.dtype)
    return (w[:, :, None, :] * eye[:, None, :, None]).reshape(B * I, B * J)


def _block_diag_take(x, B):
    I, J = x.shape[0] // B, x.shape[1] // B
    eye = jnp.eye(B, dtype=x.dtype)
    return jnp.sum(x.reshape(B, I, B, J) * eye[:, None, :, None], axis=2)


def _rope_tables(S):
    half = HEAD_DIM // 2
    inv = jnp.power(ROPE_THETA, -jnp.arange(half, dtype=F32) / half)
    ang = jnp.arange(S, dtype=F32)[:, None] * inv[None, :]
    cos = jnp.concatenate([jnp.cos(ang), jnp.cos(ang)], axis=1)
    sin = jnp.concatenate([jnp.sin(ang), jnp.sin(ang)], axis=1)
    r = jnp.arange(HEAD_DIM)[:, None]
    c = jnp.arange(HEAD_DIM)[None, :]
    rot = jnp.where(r == c + half, -1.0, 0.0) + jnp.where(c == r + half, 1.0, 0.0)
    return cos, sin, rot.astype(F32)


def qk_prep_fwd(x_hm, g, rope_tabs, name):
    H, S, Dh = x_hm.shape
    rows = [x_hm.reshape(H * S, Dh)]
    consts = [g.reshape(1, Dh)]
    periods = [None]
    if rope_tabs is not None:
        rows += [rope_tabs[0], rope_tabs[1]]
        consts += [rope_tabs[2]]
        periods += [S, S]
    fn = _qk_prep(rope_tabs is not None)
    y = rowwise(lambda *a: ([fn(*a)], []), rows, consts, outs=[(Dh, F32)], name=name, periods=periods)[0]
    return y.reshape(H, S, Dh)


def qk_prep_bwd(x_hm, g, rope_tabs, dy_hm, name):
    H, S, Dh = x_hm.shape
    rope = rope_tabs is not None
    rows = [x_hm.reshape(H * S, Dh), dy_hm.reshape(H * S, Dh)]
    consts = [g.reshape(1, Dh)]
    periods = [None, None]
    if rope:
        rows += [rope_tabs[0], rope_tabs[1]]
        consts += [rope_tabs[2]]
        periods += [S, S]
    fn = _qk_prep(rope)

    def f(xv, dyv, *rest):
        if rope:
            cos, sin, gv, rot = rest
            dx, dg = _vjp(lambda a, b: fn(a, cos, sin, b, rot), (xv, gv), dyv)
        else:
            (gv,) = rest
            dx, dg = _vjp(fn, (xv, gv), dyv)
        return [dx], [dg]

    dx, dg = rowwise(f, rows, consts, outs=[(Dh, F32)], accs=[(1, Dh)], name=name, periods=periods)
    return dx.reshape(H, S, Dh), dg.reshape(Dh)


def attn_delta(do_t, o_t, name):
    H, Dh, S = o_t.shape

    def body(a_ref, b_ref, o_ref):
        o_ref[...] = jnp.sum(a_ref[...] * b_ref[...], axis=0, keepdims=True)

    spec = pl.BlockSpec((None, Dh, S), lambda h: (h, 0, 0))
    return pl.pallas_call(
        body, grid=(H,), in_specs=[spec, spec], out_specs=pl.BlockSpec((None, 1, S), lambda h: (h, 0, 0)),
        out_shape=jax.ShapeDtypeStruct((H, 1, S), F32), name=name, compiler_params=_params(("parallel",)),
    )(do_t, o_t)


def even_mixer_fwd(h, w, tag):
    S = h.shape[0]
    W = 512
    H = 8
    z = mm(h, w["w_in"], name=f"ev_in_{tag}")
    xa, ya, q, k, v, f = (z[:, 0:512], z[:, 512:1024], z[:, 1024:1536], z[:, 1536:2048], z[:, 2048:2560],
                          z[:, 2560:2688])
    xs = [_shift_down(xa, LRU_CONV - 1 - tap) for tap in range(LRU_CONV)]
    taps = [w["conv_w"][tap][None] for tap in range(LRU_CONV)]
    xc = rowwise(lambda *a: ([_lru_conv(*a)], []), xs, taps + [w["conv_b"]], outs=[(W, F32)],
                 name=f"lru_conv_{tag}")[0]
    pre = mm(xc, w["w_ax"], name=f"lru_gates_mm_{tag}")
    a, b = rowwise(lambda p_, x_, ba, bx, lam: (list(_lru_gates(p_, x_, ba, bx, lam)), []), [pre, xc],
                   [w["ba"], w["bx"], w["lam"]], outs=[(W, F32), (W, F32)], name=f"lru_gates_{tag}")
    hs = scan_real(a, b, name=f"lru_scan_{tag}")
    a_out = rowwise(lambda y_, h_: ([_gelu(y_) * h_], []), [ya, hs], outs=[(W, F32)], name=f"lru_out_{tag}")[0]
    lf = rowwise(lambda f_, bf: ([_log_sigmoid(f_ + bf)], []), [f], [w["bf"]], outs=[(LANES, F32)],
                 name=f"fox_logf_{tag}")[0]
    c = scan_real(jnp.ones_like(lf), lf, name=f"fox_cumsum_{tag}")
    c_hm = c[:, :H].T
    q_hm, k_hm, v_hm = _heads(q, H), _heads(k, H), _heads(v, H)
    qn = qk_prep_fwd(q_hm, w["qn"], None, f"fox_qprep_{tag}")
    kn = qk_prep_fwd(k_hm, w["kn"], None, f"fox_kprep_{tag}")
    sink = jnp.full((H, 1, 1), NEG, F32)
    o_hm, lse = attn_fwd(qn, kn, v_hm, sink, c_hm[:, None, :], c_hm[:, :, None], name=f"fox_attn_{tag}")
    mo = jnp.concatenate([a_out, _unheads_t(o_hm)], axis=1).astype(BF16)
    y = mm(mo, w["w_out"], name=f"ev_out_{tag}")
    saved = dict(h=h, xs=xs, xc=xc, pre=pre, a=a, hs=hs, ya=ya, f=f, c_hm=c_hm, q_hm=q_hm, k_hm=k_hm,
                 v_hm=v_hm, qn=qn, kn=kn, o_hm=o_hm, lse=lse, mo=mo)
    return y, saved


def even_mixer_bwd(dy, sv, w, tag):
    W = 512
    H = 8
    S = dy.shape[0]
    g = {}
    dmo = mm(dy, w["w_out"], "nt", name=f"ev_dmo_{tag}")
    g["w_out"] = mm(sv["mo"], dy, "tn", name=f"ev_dwout_{tag}")
    da_out, do = dmo[:, :W], dmo[:, W:]
    do_hm = _heads_t(do, H)
    delta = attn_delta(do_hm, sv["o_hm"], f"fox_delta_{tag}")
    c_hm = sv["c_hm"]
    dqn, dkn, dv_hm, dcq, dck = attn_bwd(sv["qn"], sv["kn"], sv["v_hm"], sv["lse"], do_hm, delta,
                                          c_hm[:, None, :], c_hm[:, :, None], name=f"fox_attn_bwd_{tag}")
    dq_hm, g["qn"] = qk_prep_bwd(sv["q_hm"], w["qn"], None, dqn, f"fox_qprep_bwd_{tag}")
    dk_hm, g["kn"] = qk_prep_bwd(sv["k_hm"], w["kn"], None, dkn, f"fox_kprep_bwd_{tag}")
    dc = (dcq[:, 0, :] + dck[:, :, 0]).T
    dc = jnp.pad(dc, ((0, 0), (0, LANES - H)))
    dlf = scan_real(jnp.ones_like(dc), dc, reverse=True, name=f"fox_cumsum_bwd_{tag}")

    def f_logf(f_, d_, bf):
        df, dbf = _vjp(lambda a_, b_: _log_sigmoid(a_ + b_), (f_, bf), d_)
        return [df], [dbf]

    df, dbf = rowwise(f_logf, [sv["f"], dlf], [w["bf"]], outs=[(LANES, F32)], accs=[(1, LANES)],
                      name=f"fox_logf_bwd_{tag}")
    g["bf"] = dbf[0, :H]
    def f_out(y_, h_, d_):
        dyv, dhv = _vjp(lambda a_, b_: _gelu(a_) * b_, (y_, h_), d_)
        return [dyv, dhv], []

    dya, dhs = rowwise(f_out, [sv["ya"], sv["hs"], da_out], outs=[(W, F32), (W, F32)], name=f"lru_out_bwd_{tag}")
    gs = scan_real(_shift_up(sv["a"]), dhs, reverse=True, name=f"lru_scan_bwd_{tag}")

    def f_gates(p_, x_, g_, hp_, ba, bx, lam):
        dp, dx, dba, dbx, dlam = _vjp(_lru_gates, (p_, x_, ba, bx, lam), (g_ * hp_, g_))
        return [dp, dx], [dba, dbx, dlam]

    dpre, dxc, dba, dbx, dlam = rowwise(f_gates, [sv["pre"], sv["xc"], gs, _shift_down(sv["hs"])],
                                        [w["ba"], w["bx"], w["lam"]], outs=[(2 * W, BF16), (W, F32)],
                                        accs=[(1, W)] * 3, name=f"lru_gates_bwd_{tag}")
    g["ba"], g["bx"], g["lam"] = dba[0], dbx[0], dlam[0]
    dxc2 = mm(dpre, w["w_ax"], "nt", name=f"lru_gates_mm_dx_{tag}")
    g["w_ax"] = mm(sv["xc"], dpre, "tn", name=f"lru_gates_mm_dw_{tag}")

    def f_conv(d1, d2, x0, x1, x2, x3):
        d = d1 + d2
        return [d], [jnp.sum(d, axis=0, keepdims=True)] + [jnp.sum(d * xv, axis=0, keepdims=True)
                                                           for xv in (x0, x1, x2, x3)]

    dxc_t, dcb, dw0, dw1, dw2, dw3 = rowwise(f_conv, [dxc, dxc2] + sv["xs"], outs=[(W, F32)],
                                             accs=[(1, W)] * 5, name=f"lru_conv_bwd_{tag}")
    g["conv_b"] = dcb[0]
    g["conv_w"] = jnp.concatenate([dw0, dw1, dw2, dw3], axis=0)
    ds_ = [_shift_up(dxc_t, LRU_CONV - 1 - tap) for tap in range(LRU_CONV)]
    taps = [w["conv_w"][tap][None] for tap in range(LRU_CONV)]
    dxa = rowwise(lambda a, b, c, d, w0, w1, w2, w3: ([a * w0 + b * w1 + c * w2 + d * w3], []), ds_, taps,
                  outs=[(W, F32)], name=f"lru_conv_dx_{tag}")[0]
    dz = jnp.concatenate([dxa, dya, _unheads(dq_hm), _unheads(dk_hm), _unheads(dv_hm), df], axis=1).astype(BF16)
    g["w_in"] = mm(sv["h"], dz, "tn", name=f"ev_dwin_{tag}")
    dh = mm(dz, w["w_in"], "nt", name=f"ev_dh_{tag}")
    return dh, g


def odd_mixer_fwd(h, w, tag):
    S = h.shape[0]
    H, KVH = 8, 2
    z = mm(h, w["w_in"], name=f"od_in_{tag}")
    q, k, v, u = z[:, 0:512], z[:, 512:640], z[:, 640:768], z[:, 768:1280]
    tabs = _rope_tables(S)
    q_hm, k_hm, v_hm = _heads(q, H), _heads(k, KVH), _heads(v, KVH)
    qn = qk_prep_fwd(q_hm, w["qn"], tabs, f"swa_qprep_{tag}")
    kn = qk_prep_fwd(k_hm, w["kn"], tabs, f"swa_kprep_{tag}")
    sink = w["sinks"].reshape(H, 1, 1)
    o_hm, lse = attn_fwd(qn, kn, v_hm, sink, window=SWA_WINDOW, name=f"swa_attn_{tag}")
    lam, bexp = w["s5_lam"], w["s5_bexp"]
    bu = mm(u, bexp, name=f"s5_bu_{tag}")
    hs = scan_cplx(lam, bu, name=f"s5_scan_{tag}")
    yssm = mm(hs, w["s5_cexp"], name=f"s5_y_{tag}")
    zz = rowwise(lambda y_, u_, d_: ([_s5_out(y_, u_, d_)], []), [yssm, u], [w["s5_d"]], outs=[(512, F32)],
                 name=f"s5_gelu_{tag}")[0]
    gl = mm(zz, w["glu_w"], name=f"s5_glu_mm_{tag}")
    d_out = rowwise(lambda z_, g_, b_: ([_glu(z_, g_, b_)], []), [zz, gl], [w["glu_b"]], outs=[(512, F32)],
                    name=f"s5_glu_{tag}")[0]
    mo = jnp.concatenate([_unheads_t(o_hm), d_out], axis=1).astype(BF16)
    y = mm(mo, w["w_out"], name=f"od_out_{tag}")
    saved = dict(h=h, q_hm=q_hm, k_hm=k_hm, v_hm=v_hm, qn=qn, kn=kn, o_hm=o_hm, lse=lse, u=u, hs=hs, yssm=yssm,
                 zz=zz, gl=gl, mo=mo, tabs=tabs)
    return y, saved


def odd_mixer_bwd(dy, sv, w, tag):
    H, KVH = 8, 2
    g = {}
    dmo = mm(dy, w["w_out"], "nt", name=f"od_dmo_{tag}")
    g["w_out"] = mm(sv["mo"], dy, "tn", name=f"od_dwout_{tag}")
    do, dd = dmo[:, :512], dmo[:, 512:]
    do_hm = _heads_t(do, H)
    delta = attn_delta(do_hm, sv["o_hm"], f"swa_delta_{tag}")
    dqn, dkn, dv_hm = attn_bwd(sv["qn"], sv["kn"], sv["v_hm"], sv["lse"], do_hm, delta, window=SWA_WINDOW,
                               name=f"swa_attn_bwd_{tag}")
    dq_hm, g["qn"] = qk_prep_bwd(sv["q_hm"], w["qn"], sv["tabs"], dqn, f"swa_qprep_bwd_{tag}")
    dk_hm, g["kn"] = qk_prep_bwd(sv["k_hm"], w["kn"], sv["tabs"], dkn, f"swa_kprep_bwd_{tag}")
    lse_t, delta_t = sv["lse"][:, 0, :].T, delta[:, 0, :].T
    g["sinks"] = rowwise(lambda l_, d_, s_: ([], [jnp.sum(-jnp.exp(s_ - l_) * d_, axis=0, keepdims=True)]),
                         [lse_t, delta_t], [w["sinks"].reshape(1, H)], accs=[(1, H)], name=f"swa_dsink_{tag}")[0][0]
    def f_glu(z_, g_, d_, b_):
        dz_, dg_, db_ = _vjp(_glu, (z_, g_, b_), d_)
        return [dz_, dg_], [db_]

    dzz1, dgl, dglb = rowwise(f_glu, [sv["zz"], sv["gl"], dd], [w["glu_b"]], outs=[(512, F32), (512, BF16)],
                              accs=[(1, 512)], name=f"s5_glu_bwd_{tag}")
    g["glu_b"] = dglb[0]
    g["glu_w"] = mm(sv["zz"], dgl, "tn", name=f"s5_glu_dw_{tag}")
    dzz2 = mm(dgl, w["glu_w"], "nt", name=f"s5_glu_dz_{tag}")

    def f_gelu(y_, u_, d1, d2, dpar):
        dy_, du_, dd_ = _vjp(_s5_out, (y_, u_, dpar), d1 + d2)
        return [dy_, du_], [dd_]

    dyssm, du1, dsd = rowwise(f_gelu, [sv["yssm"], sv["u"], dzz1, dzz2], [w["s5_d"]],
                              outs=[(512, F32), (512, F32)], accs=[(1, 512)], name=f"s5_gelu_bwd_{tag}")
    g["s5_d"] = dsd[0]
    dhs = mm(dyssm, w["s5_cexp"], "nt", name=f"s5_dh_{tag}")
    g["s5_cexp"] = mm(sv["hs"], dyssm, "tn", name=f"s5_dc_{tag}")
    gs = scan_cplx(w["s5_lam"], dhs, reverse=True, name=f"s5_scan_bwd_{tag}")
    g["s5_bexp"] = mm(sv["u"], gs, "tn", name=f"s5_db_{tag}")
    du2 = mm(gs, w["s5_bexp"], "nt", name=f"s5_du_{tag}")

    def f_dlam(g_, hp_):
        C = g_.shape[1]
        outs_r, outs_i = [], []
        for j in range(C // (2 * LANES)):
            gr, gi = g_[:, 2 * LANES * j:2 * LANES * j + LANES], g_[:, 2 * LANES * j + LANES:2 * LANES * (j + 1)]
            hr, hi = hp_[:, 2 * LANES * j:2 * LANES * j + LANES], hp_[:, 2 * LANES * j + LANES:2 * LANES * (j + 1)]
            outs_r.append(jnp.sum(gr * hr + gi * hi, axis=0, keepdims=True))
            outs_i.append(jnp.sum(gi * hr - gr * hi, axis=0, keepdims=True))
        return [], [jnp.concatenate([x for pair in zip(outs_r, outs_i) for x in pair], axis=1)]

    g["s5_lam"] = rowwise(f_dlam, [gs, _shift_down(sv["hs"])], accs=[(1, gs.shape[1])], name=f"s5_dlam_{tag}")[0]
    du = rowwise(lambda a_, b_: ([a_ + b_], []), [du1, du2], outs=[(512, F32)], name=f"s5_du_add_{tag}")[0]
    dz = jnp.concatenate([_unheads(dq_hm), _unheads(dk_hm), _unheads(dv_hm), du], axis=1).astype(BF16)
    g["w_in"] = mm(sv["h"], dz, "tn", name=f"od_dwin_{tag}")
    dh = mm(dz, w["w_in"], "nt", name=f"od_dh_{tag}")
    return dh, g


def _s5_cols(x_re, x_im):
    n = x_re.shape[0] // LANES
    return jnp.stack([x_re.reshape(n, LANES), x_im.reshape(n, LANES)], axis=1).reshape(1, 2 * n * LANES)


def _s5_uncols(x):
    n = x.shape[1] // (2 * LANES)
    y = x.reshape(n, 2, LANES)
    return y[:, 0].reshape(-1), y[:, 1].reshape(-1)


def _s5_gsel():
    return jnp.repeat(jnp.eye(S5_GROUPS, dtype=F32), S5_STATE, axis=0)


def s5_prep_fwd(lre, lim, ldt, bre, bim, cre, cim, tag):
    GP = S5_GROUPS * S5_STATE
    ins = [lre.reshape(GP, 1), lim.reshape(GP, 1), ldt.reshape(1, S5_GROUPS), _s5_gsel(),
           bre.reshape(GP, S5_GROUP), bim.reshape(GP, S5_GROUP)]
    lbr, lbi, bbr, bbi = whole(_s5_params, ins, [((GP, 1), F32)] * 2 + [((GP, S5_GROUP), F32)] * 2,
                               name=f"s5_params_{tag}")
    lam = _s5_cols(lbr[:, 0], lbi[:, 0])

    def expand_b(bb):
        return _block_diag(bb.reshape(S5_GROUPS, S5_STATE, S5_GROUP).transpose(0, 2, 1))

    n = GP // LANES
    bexp = jnp.stack([expand_b(bbr).reshape(-1, n, LANES), expand_b(bbi).reshape(-1, n, LANES)],
                     axis=2).reshape(-1, 2 * GP)
    c_r = _block_diag(cre.transpose(0, 2, 1))
    c_i = _block_diag(cim.transpose(0, 2, 1))
    cexp = jnp.stack([c_r.reshape(n, LANES, -1), -c_i.reshape(n, LANES, -1)], axis=1).reshape(2 * GP, -1)
    return lam, bexp.astype(BF16), cexp.astype(BF16), ins


def s5_prep_bwd(ins, dlam, dbexp, dcexp, tag):
    GP = S5_GROUPS * S5_STATE
    n = GP // LANES
    dlr, dli = _s5_uncols(dlam)
    db = dbexp.reshape(-1, n, 2, LANES)

    def take_b(x):
        return _block_diag_take(x, S5_GROUPS).transpose(0, 2, 1).reshape(GP, S5_GROUP)

    dbbr, dbbi = take_b(db[:, :, 0].reshape(-1, GP)), take_b(db[:, :, 1].reshape(-1, GP))
    dc = dcexp.reshape(n, 2, LANES, -1)
    dcre = _block_diag_take(dc[:, 0].reshape(GP, -1), S5_GROUPS).transpose(0, 2, 1)
    dcim = -_block_diag_take(dc[:, 1].reshape(GP, -1), S5_GROUPS).transpose(0, 2, 1)

    def f(lre, lim, ldt, gsel, bre, bim, c1, c2, c3, c4):
        d = _vjp(lambda a, b, c, e, f_: _s5_params(a, b, c, gsel, e, f_), (lre, lim, ldt, bre, bim), (c1, c2, c3, c4))
        return d

    outs = [((GP, 1), F32)] * 2 + [((1, S5_GROUPS), F32)] + [((GP, S5_GROUP), F32)] * 2
    dlre, dlim, dldt, dbre, dbim = whole(f, ins + [dlr.reshape(GP, 1), dli.reshape(GP, 1), dbbr, dbbi], outs,
                                          name=f"s5_params_bwd_{tag}")
    shp = (S5_GROUPS, S5_STATE)
    return dict(lre=dlre.reshape(shp), lim=dlim.reshape(shp), ldt=dldt.reshape(S5_GROUPS),
                bre=dbre.reshape(S5_GROUPS, S5_STATE, S5_GROUP), bim=dbim.reshape(S5_GROUPS, S5_STATE, S5_GROUP),
                cre=dcre, cim=dcim)


def _place():
    return lax.axis_index("x"), lax.axis_index("y"), lax.axis_index("c")


def _other_chips(x, y):
    return [(1 - x, y), (x, 1 - y), (1 - x, 1 - y)]


def _half(ref, h):
    n = ref.shape[0] // 2
    return ref.at[pl.ds(h * n, n)]


def _hbm_specs(n):
    return [pl.BlockSpec(memory_space=pl.ANY)] * n


def gather_chips(ws):
    n = len(ws)

    def body(*refs):
        w_refs, out_refs, (send_sems, recv_sems) = refs[:n], refs[n:2 * n], refs[2 * n:]
        x, y, c = _place()
        me, sibling = (x, y, c), (x, y, 1 - c)
        chips = _other_chips(x, y)
        mine = 2 * x + y

        def copy(k, src, dst, to):
            return pltpu.make_async_remote_copy(src_ref=src, dst_ref=dst, send_sem=send_sems.at[k],
                                                recv_sem=recv_sems.at[k], device_id=to, device_id_type=MESH)

        first, passed = [], []
        for p in range(n):
            for j, chip in enumerate(chips):
                first.append(copy(6 * p + j, _half(w_refs[p], c), _half(out_refs[p].at[mine], c), (*chip, c)))
                first[-1].start()
        for p in range(n):
            for j, chip in enumerate(chips):
                block = out_refs[p].at[2 * chip[0] + chip[1]]
                copy(6 * p + j, _half(w_refs[p], c), _half(block, c), me).wait_recv()
                passed.append(copy(6 * p + 3 + j, _half(block, c), _half(block, c), sibling))
                passed[-1].start()
        for p in range(n):
            for j, chip in enumerate(chips):
                block = out_refs[p].at[2 * chip[0] + chip[1]]
                copy(6 * p + 3 + j, _half(w_refs[p], c), _half(block, 1 - c), me).wait_recv()
        for cp in first + passed:
            cp.wait_send()

    return pl.pallas_call(
        body, out_shape=[jax.ShapeDtypeStruct((N_CHIPS,) + w.shape, w.dtype) for w in ws],
        in_specs=_hbm_specs(n), out_specs=_hbm_specs(n),
        scratch_shapes=[pltpu.SemaphoreType.DMA((6 * n,)), pltpu.SemaphoreType.DMA((6 * n,))],
        name="gather_chips",
    )(*ws)


def sibling_halves(gs):
    n = len(gs)

    def body(*refs):
        g_refs, out_refs, (send_sems, recv_sems) = refs[:n], refs[n:2 * n], refs[2 * n:]
        x, y, c = _place()
        me, sibling = (x, y, c), (x, y, 1 - c)

        def copy(p, k, to):
            return pltpu.make_async_remote_copy(src_ref=_half(g_refs[p].at[k], 1 - c), dst_ref=out_refs[p].at[k],
                                                send_sem=send_sems.at[N_CHIPS * p + k],
                                                recv_sem=recv_sems.at[N_CHIPS * p + k],
                                                device_id=to, device_id_type=MESH)

        cps = [copy(p, k, sibling) for p in range(n) for k in range(N_CHIPS)]
        for cp in cps:
            cp.start()
        for p in range(n):
            for k in range(N_CHIPS):
                copy(p, k, me).wait_recv()
        for cp in cps:
            cp.wait_send()

    return pl.pallas_call(
        body, out_shape=[jax.ShapeDtypeStruct((N_CHIPS, g.shape[1] // 2) + g.shape[2:], g.dtype) for g in gs],
        in_specs=_hbm_specs(n), out_specs=_hbm_specs(n),
        scratch_shapes=[pltpu.SemaphoreType.DMA((N_CHIPS * n,)), pltpu.SemaphoreType.DMA((N_CHIPS * n,))],
        name="sibling_halves",
    )(*gs)


def exchange_chips(ps):
    n = len(ps)

    def body(*refs):
        p_refs, out_refs, (send_sems, recv_sems) = refs[:n], refs[n:2 * n], refs[2 * n:]
        x, y, c = _place()
        me = (x, y, c)
        chips = _other_chips(x, y)

        def copy(p, j, chip, to):
            return pltpu.make_async_remote_copy(src_ref=p_refs[p].at[2 * chip[0] + chip[1]], dst_ref=out_refs[p].at[j],
                                                send_sem=send_sems.at[3 * p + j], recv_sem=recv_sems.at[3 * p + j],
                                                device_id=to, device_id_type=MESH)

        cps = [copy(p, j, chip, (*chip, c)) for p in range(n) for j, chip in enumerate(chips)]
        for cp in cps:
            cp.start()
        for p in range(n):
            for j, chip in enumerate(chips):
                copy(p, j, chip, me).wait_recv()
        for cp in cps:
            cp.wait_send()

    return pl.pallas_call(
        body, out_shape=[jax.ShapeDtypeStruct((3,) + p_.shape[1:], p_.dtype) for p_ in ps],
        in_specs=_hbm_specs(n), out_specs=_hbm_specs(n),
        scratch_shapes=[pltpu.SemaphoreType.DMA((3 * n,)), pltpu.SemaphoreType.DMA((3 * n,))],
        name="exchange_chips",
    )(*ps)


def sibling_join(rs):
    n = len(rs)

    def body(*refs):
        r_refs, out_refs, (send_sems, recv_sems) = refs[:n], refs[n:2 * n], refs[2 * n:]
        x, y, c = _place()

        def copy(p, h, to):
            return pltpu.make_async_remote_copy(src_ref=r_refs[p], dst_ref=_half(out_refs[p], h),
                                                send_sem=send_sems.at[p], recv_sem=recv_sems.at[p],
                                                device_id=to, device_id_type=MESH)

        cps = [copy(p, c, (x, y, 1 - c)) for p in range(n)]
        for cp in cps:
            cp.start()
        for p in range(n):
            copy(p, 1 - c, (x, y, c)).wait_recv()
        for cp in cps:
            cp.wait_send()

    return pl.pallas_call(
        body, out_shape=[jax.ShapeDtypeStruct((2 * r.shape[0],) + r.shape[1:], r.dtype) for r in rs],
        in_specs=_hbm_specs(n), out_specs=_hbm_specs(n),
        scratch_shapes=[pltpu.SemaphoreType.DMA((n,)), pltpu.SemaphoreType.DMA((n,))],
        name="sibling_join",
    )(*rs)


def gather_devices(v, name):
    R = v.shape[0]

    def body(v_ref, out_ref, send_sems, recv_sems, local_sem):
        x, y, c = _place()
        me, sibling = (x, y, c), (x, y, 1 - c)
        chips = _other_chips(x, y)

        def rows(px, py, pc):
            return out_ref.at[pl.ds((4 * px + 2 * py + pc) * R, R), :]

        def copy(k, block, to, src=None):
            return pltpu.make_async_remote_copy(src_ref=rows(*block) if src is None else src, dst_ref=rows(*block),
                                                send_sem=send_sems.at[k], recv_sem=recv_sems.at[k],
                                                device_id=to, device_id_type=MESH)

        mine = pltpu.make_async_copy(v_ref, rows(*me), local_sem)
        mine.start()
        first = [copy(0, me, sibling, src=v_ref)]
        first += [copy(1 + j, me, (*chip, c), src=v_ref) for j, chip in enumerate(chips)]
        for cp in first:
            cp.start()
        passed = [copy(4 + j, (*chip, c), sibling) for j, chip in enumerate(chips)]
        for j, chip in enumerate(chips):
            copy(1 + j, (*chip, c), me).wait_recv()
            passed[j].start()
        copy(0, sibling, me).wait_recv()
        for j, chip in enumerate(chips):
            copy(4 + j, (*chip, 1 - c), me).wait_recv()
        for cp in first + passed:
            cp.wait_send()
        mine.wait()

    return pl.pallas_call(
        body, out_shape=jax.ShapeDtypeStruct((N_DEV * R, LANES), v.dtype),
        in_specs=[pl.BlockSpec(memory_space=pltpu.VMEM)], out_specs=pl.BlockSpec(memory_space=pltpu.VMEM),
        scratch_shapes=[pltpu.SemaphoreType.DMA((7,)), pltpu.SemaphoreType.DMA((7,)), pltpu.SemaphoreType.DMA],
        name=name, compiler_params=_params(),
    )(v)


def _flat_rows(n, mult):
    return -(-n // (LANES * mult)) * mult


def _adam(w, g, m, v):
    m = ADAM_B1 * m + (1.0 - ADAM_B1) * g
    v = ADAM_B2 * v + (1.0 - ADAM_B2) * (g * g)
    m_hat = m / (1.0 - ADAM_B1 ** ADAM_STEP)
    v_hat = v / (1.0 - ADAM_B2 ** ADAM_STEP)
    return -ADAM_LR * (m_hat / (jnp.sqrt(v_hat) + ADAM_EPS) + ADAM_WD * w), m, v


def adam_2d(w, g, m, v, name):
    shape = w.shape
    F = shape[-1]
    a = [t.reshape(-1, F) for t in (w, g, m, v)]
    d, m2, v2 = rowwise(lambda w_, g_, m_, v_: (list(_adam(w_, g_, m_, v_)), []), a, outs=[(F, F32)] * 3, name=name)
    return d.reshape(shape), m2.reshape(shape), v2.reshape(shape)


WEIGHTS = ['ffn1_norm', 'ffn1_wg', 'ffn1_wu', 'ffn1_wd', 'mix_norm', 'ffn2_norm', 'ffn2_wg', 'ffn2_wu', 'ffn2_wd',
           'ple_w', 'ple_norm', 'ple_gate_norm', 'ple_gate_w', 'ev_w_in', 'lru_conv_w', 'lru_conv_b', 'lru_wa',
           'lru_ba', 'lru_wx', 'lru_bx', 'lru_lambda', 'fox_bf', 'fox_q_norm', 'fox_k_norm', 'ev_w_out', 'od_w_in',
           'swa_q_norm', 'swa_k_norm', 'swa_sinks', 's5_lambda_re', 's5_lambda_im', 's5_log_dt', 's5_b_re',
           's5_b_im', 's5_c_re', 's5_c_im', 's5_d', 's5_glu_w', 's5_glu_b', 'od_w_out']
SHARD_AXIS = {'ffn1_wg': 2, 'ffn1_wu': 2, 'ffn1_wd': 1, 'ffn2_wg': 2, 'ffn2_wu': 2, 'ffn2_wd': 1, 'ple_w': 2,
              'ple_gate_w': 1, 'ev_w_in': 2, 'lru_conv_w': 2, 'ev_w_out': 1, 'od_w_in': 2, 's5_d': 1,
              's5_glu_w': 1, 's5_glu_b': 1, 'od_w_out': 1}
EXACT_SHARDED = ('lru_conv_w', 's5_d', 's5_glu_b')
SHARDED = [n for n in WEIGHTS if n in SHARD_AXIS]
REPLICATED = [n for n in WEIGHTS if n not in SHARD_AXIS]


GROUPS = {
    'wgu': ['ffn1_wg', 'ffn1_wu', 'ffn2_wg', 'ffn2_wu'],
    'wd': ['ffn1_wd', 'ffn2_wd'],
    'w_rows': ['ple_gate_w', 'ev_w_out', 'od_w_out'],
    'ple_w': ['ple_w'], 'ev_w_in': ['ev_w_in'], 'od_w_in': ['od_w_in'], 's5_glu_w': ['s5_glu_w'],
}
REDUCED_GROUPS = list(GROUPS)


def _chip():
    return 2 * lax.axis_index("x") + lax.axis_index("y")


def gather_weights(shards):
    own = {k: jnp.concatenate([shards[n] for n in names], axis=0).astype(BF16) for k, names in GROUPS.items()}
    own['exact'] = jnp.concatenate([shards['lru_conv_w'], shards['s5_d'][:, None], shards['s5_glu_b'][:, None]], axis=1)
    keys = list(own)
    got = gather_chips([own[k] for k in keys])
    return {k: lax.dynamic_update_index_in_dim(g, own[k], _chip(), 0) for k, g in zip(keys, got)}


def _rows_by_chip(w):
    return w.reshape(w.shape[0] * w.shape[1], w.shape[2])


def _cols_by_chip(w):
    return w.transpose(1, 0, 2).reshape(w.shape[1], w.shape[0] * w.shape[2])


def _chip_rows(g):
    return g.reshape(N_CHIPS, g.shape[0] // N_CHIPS, g.shape[1])


def _chip_cols(g):
    return g.reshape(g.shape[0], N_CHIPS, g.shape[1] // N_CHIPS).transpose(1, 0, 2)


def full_weights(gw, depth):
    n_ev = (depth + 1) // 2
    ex = gw['exact']
    return dict(
        ple_gate_w=[_rows_by_chip(gw['w_rows'][:, l]) for l in range(depth)],
        ev_w_out=[_rows_by_chip(gw['w_rows'][:, depth + j]) for j in range(n_ev)],
        od_w_out=[_rows_by_chip(gw['w_rows'][:, depth + n_ev + j]) for j in range(depth // 2)],
        ple_w=[_cols_by_chip(gw['ple_w'][:, l]) for l in range(depth)],
        ev_w_in=[_cols_by_chip(gw['ev_w_in'][:, j]) for j in range(n_ev)],
        od_w_in=[_cols_by_chip(gw['od_w_in'][:, j]) for j in range(depth // 2)],
        s5_glu_w=[_rows_by_chip(gw['s5_glu_w'][:, j]) for j in range(depth // 2)],
        lru_conv_w=[_cols_by_chip(ex[:, j, 0:LRU_CONV]) for j in range(n_ev)],
        s5_d=[ex[:, j, LRU_CONV].reshape(-1) for j in range(depth // 2)],
        s5_glu_b=[ex[:, j, LRU_CONV + 1].reshape(-1) for j in range(depth // 2)],
    )


def _add_tile(rows, width):
    for t in (1024, 512, 256, 128, 64, 32, 16):
        if rows % t == 0 and 3 * t * width * 4 <= ROW_TILE_BYTES:
            return t
    return rows


def pair_add(g, t, c, name):
    C, F = g.shape[0], g.shape[-1]
    rows = math.prod(t.shape[1:-1])
    tr = _add_tile(rows, F)
    nb = rows // tr

    def body(c_ref, g_ref, t_ref, o_ref):
        o_ref[...] = (g_ref[...] + t_ref[...]).astype(o_ref.dtype)

    spec = pl.BlockSpec((None, tr, F), lambda k, i, c_ref: (k, i, 0))
    out = pl.pallas_call(
        body, out_shape=jax.ShapeDtypeStruct((C, rows, F), BF16),
        grid_spec=pltpu.PrefetchScalarGridSpec(
            num_scalar_prefetch=1, grid=(C, nb),
            in_specs=[pl.BlockSpec((None, tr, F), lambda k, i, c_ref: (k, c_ref[0] * nb + i, 0)), spec],
            out_specs=spec),
        name=name, compiler_params=_params(("parallel", "parallel")),
    )(c.reshape(1).astype(jnp.int32), g.reshape(C, 2 * rows, F), t.reshape(C, rows, F))
    return out.reshape(t.shape)


def chips_add(p, xs, chip, name):
    F = p.shape[-1]
    rows = math.prod(p.shape[1:-1])
    tr = _add_tile(rows, F)

    def body(m_ref, p_ref, a_ref, b_ref, d_ref, o_ref):
        o_ref[...] = ((p_ref[...].astype(F32) + a_ref[...].astype(F32))
                      + (b_ref[...].astype(F32) + d_ref[...].astype(F32)))

    def other(j):
        return pl.BlockSpec((None, tr, F), lambda i, m_ref: (j, i, 0))

    x3 = xs.reshape(3, rows, F)
    out = pl.pallas_call(
        body, out_shape=jax.ShapeDtypeStruct((rows, F), F32),
        grid_spec=pltpu.PrefetchScalarGridSpec(
            num_scalar_prefetch=1, grid=(rows // tr,),
            in_specs=[pl.BlockSpec((None, tr, F), lambda i, m_ref: (m_ref[0], i, 0)), other(0), other(1), other(2)],
            out_specs=pl.BlockSpec((tr, F), lambda i, m_ref: (i, 0))),
        name=name, compiler_params=_params(("parallel",)),
    )(chip.reshape(1).astype(jnp.int32), p.reshape(N_CHIPS, rows, F), x3, x3, x3)
    return out.reshape(p.shape[1:])


def reduce_sharded(groups):
    keys = list(groups)
    c = lax.axis_index("c")
    gs = [groups[k] for k in keys]
    theirs = sibling_halves(gs)
    pairs = [pair_add(g, t, c, f"pair_add_{k}") for k, g, t in zip(keys, gs, theirs)]
    got = exchange_chips(pairs)
    halves = [chips_add(p_, x_, _chip(), f"chips_add_{k}") for k, p_, x_ in zip(keys, pairs, got)]
    joined = sibling_join(halves)
    out = {}
    for k, h, j in zip(keys, halves, joined):
        out[k] = lax.dynamic_update_slice_in_dim(j, h, c * h.shape[0], axis=0)
    return out


SMALL_GRADS = REPLICATED + list(EXACT_SHARDED)


def _flatten_small(tensors, shapes):
    parts = [tensors[n].astype(F32).reshape(-1) if n in tensors else jnp.zeros((math.prod(shapes[n]),), F32)
             for n in SMALL_GRADS]
    flat = jnp.concatenate(parts)
    rows = _flat_rows(flat.shape[0], SUBLANES)
    return jnp.pad(flat, (0, rows * LANES - flat.shape[0])).reshape(rows, LANES)


def _unflatten_small(flat, shapes):
    flat = flat.reshape(-1)
    out, off = {}, 0
    for n in SMALL_GRADS:
        size = math.prod(shapes[n])
        out[n] = flat[off:off + size].reshape(shapes[n])
        off += size
    return out


def grad_groups(gwgu, gwd, G):
    def st(xs):
        return jnp.stack(xs, axis=1)

    return {
        'wgu': gwgu, 'wd': gwd,
        'w_rows': st([_chip_rows(g) for n in GROUPS['w_rows'] for g in G[n]]),
        'ple_w': st([_chip_cols(g) for g in G['ple_w']]),
        'ev_w_in': st([_chip_cols(g) for g in G['ev_w_in']]),
        'od_w_in': st([_chip_cols(g) for g in G['od_w_in']]),
        's5_glu_w': st([_chip_rows(g) for g in G['s5_glu_w']]),
    }


def ungroup(red, shapes):
    out = {}
    for k, names in GROUPS.items():
        off = 0
        for n in names:
            out[n] = red[k][off:off + shapes[n][0]]
            off += shapes[n][0]
    return out


def _layer_weights(full, small, i, depth):
    j = i // 2
    w = dict(
        g1=small['ffn1_norm'][i][None], gm=small['mix_norm'][i][None], g2=small['ffn2_norm'][i][None],
        gp=small['ple_norm'][i][None], gg=small['ple_gate_norm'][i][None],
        ffn1=(i, depth + i, i), ffn2=(2 * depth + i, 3 * depth + i, depth + i),
        ple_w=full['ple_w'][i], ple_gate_w=full['ple_gate_w'][i],
    )
    if i % 2 == 0:
        w_in = full['ev_w_in'][j]
        w['mix'] = dict(
            w_in=jnp.pad(w_in, ((0, 0), (0, 2688 - w_in.shape[1]))), w_out=full['ev_w_out'][j],
            conv_w=full['lru_conv_w'][j].astype(F32), conv_b=small['lru_conv_b'][j][None],
            w_ax=jnp.concatenate([_block_diag(small['lru_wa'][j]), _block_diag(small['lru_wx'][j])],
                                 axis=1).astype(BF16),
            ba=small['lru_ba'][j][None], bx=small['lru_bx'][j][None], lam=small['lru_lambda'][j][None],
            bf=jnp.pad(small['fox_bf'][j], (0, LANES - 8))[None], qn=small['fox_q_norm'][j],
            kn=small['fox_k_norm'][j])
    else:
        lam, bexp, cexp, ins = s5_prep_fwd(small['s5_lambda_re'][j], small['s5_lambda_im'][j], small['s5_log_dt'][j],
                                           small['s5_b_re'][j], small['s5_b_im'][j], small['s5_c_re'][j],
                                           small['s5_c_im'][j], f"L{i}")
        w['mix'] = dict(
            w_in=full['od_w_in'][j], w_out=full['od_w_out'][j], qn=small['swa_q_norm'][j], kn=small['swa_k_norm'][j],
            sinks=small['swa_sinks'][j], s5_lam=lam, s5_bexp=bexp, s5_cexp=cexp, s5_ins=ins,
            s5_d=full['s5_d'][j].astype(F32)[None], glu_w=full['s5_glu_w'][j], glu_b=full['s5_glu_b'][j].astype(F32)[None])
    return w


def layer_fwd(x, p_i, w, ffnw, i):
    tag = f"L{i}"
    sv = {}
    wgu, wd = ffnw
    x0, n1 = add_norm(x, None, w['g1'], f"norm1_{tag}")
    y1, sv['ffn1'] = ffn_fwd(n1, wgu, wd, *w['ffn1'], f"1_{tag}")
    x1, hm = add_norm(x0, y1, w['gm'], f"normm_{tag}")
    if i % 2 == 0:
        ym, sv['mix'] = even_mixer_fwd(hm, w['mix'], tag)
    else:
        ym, sv['mix'] = odd_mixer_fwd(hm, w['mix'], tag)
    x2, n2 = add_norm(x1, ym, w['g2'], f"norm2_{tag}")
    y2, sv['ffn2'] = ffn_fwd(n2, wgu, wd, *w['ffn2'], f"2_{tag}")
    x3, ng = add_norm(x2, y2, w['gg'], f"normg_{tag}")
    gpre = mm(ng, w['ple_gate_w'], name=f"ple_gate_{tag}")
    epre = mm(p_i, w['ple_w'], name=f"ple_emb_{tag}")
    D = x.shape[1]
    x4 = rowwise(lambda a, b, c, pn: ([_ple_out(a, b, c, pn)], []), [x3, gpre, epre], [w['gp']], outs=[(D, F32)],
                 name=f"ple_out_{tag}")[0]
    sv.update(x0=x0, x1=x1, x2=x2, x3=x3, ng=ng, gpre=gpre, epre=epre, p=p_i)
    return x4, sv


def layer_bwd(dx4, sv, w, ffnw, gbuf, i):
    tag = f"L{i}"
    D = dx4.shape[1]
    g = {}
    wgu, wd = ffnw

    def f_ple(a, b, c, d, pn):
        da, db, dc, dpn = _vjp(_ple_out, (a, b, c, pn), d)
        return [db, dc], [dpn]

    dgpre, depre, dgp = rowwise(f_ple, [sv['x3'], sv['gpre'], sv['epre'], dx4], [w['gp']],
                                outs=[(D, BF16), (D, BF16)], accs=[(1, D)], name=f"ple_out_bwd_{tag}")
    g['gp'] = dgp[0]
    g['ple_w'] = mm(sv['p'], depre, "tn", name=f"ple_emb_dw_{tag}")
    g['ple_gate_w'] = mm(sv['ng'], dgpre, "tn", name=f"ple_gate_dw_{tag}")
    dng = mm(dgpre, w['ple_gate_w'], "nt", name=f"ple_gate_dx_{tag}")
    dx3, dgg = norm_bwd(sv['x3'], w['gg'], dng, dx4, f"normg_bwd_{tag}")
    g['gg'] = dgg[0]
    dn2, gbuf = ffn_bwd(dx3, sv['ffn2'], wgu, wd, *w['ffn2'], gbuf, f"2_{tag}")
    dx2, dg2 = norm_bwd(sv['x2'], w['g2'], dn2, dx3, f"norm2_bwd_{tag}")
    g['g2'] = dg2[0]
    if i % 2 == 0:
        dhm, g['mix'] = even_mixer_bwd(dx2, sv['mix'], w['mix'], tag)
    else:
        dhm, g['mix'] = odd_mixer_bwd(dx2, sv['mix'], w['mix'], tag)
    dx1, dgm = norm_bwd(sv['x1'], w['gm'], dhm, dx2, f"normm_bwd_{tag}")
    g['gm'] = dgm[0]
    dn1, gbuf = ffn_bwd(dx1, sv['ffn1'], wgu, wd, *w['ffn1'], gbuf, f"1_{tag}")
    dx0, dg1 = norm_bwd(sv['x0'], w['g1'], dn1, dx1, f"norm1_bwd_{tag}")
    g['g1'] = dg1[0]
    return dx0, g, gbuf


def _collect_grads(layer_grads, depth):
    st = lambda xs: jnp.stack(xs)
    G = {}
    L = layer_grads
    G['ffn1_norm'] = st([g['g1'] for g in L])
    G['mix_norm'] = st([g['gm'] for g in L])
    G['ffn2_norm'] = st([g['g2'] for g in L])
    G['ple_norm'] = st([g['gp'] for g in L])
    G['ple_gate_norm'] = st([g['gg'] for g in L])
    G['ple_w'] = st([g['ple_w'] for g in L])
    G['ple_gate_w'] = st([g['ple_gate_w'] for g in L])
    ev = [L[i]['mix'] for i in range(0, depth, 2)]
    od = [L[i]['mix'] for i in range(1, depth, 2)]
    G['ev_w_in'] = st([m['w_in'][:, :2568] for m in ev])
    G['ev_w_out'] = st([m['w_out'] for m in ev])
    G['lru_conv_w'] = st([m['conv_w'] for m in ev])
    G['lru_conv_b'] = st([m['conv_b'] for m in ev])
    G['lru_wa'] = st([_block_diag_take(m['w_ax'][:, :512], LRU_BLOCKS) for m in ev])
    G['lru_wx'] = st([_block_diag_take(m['w_ax'][:, 512:], LRU_BLOCKS) for m in ev])
    G['lru_ba'] = st([m['ba'] for m in ev])
    G['lru_bx'] = st([m['bx'] for m in ev])
    G['lru_lambda'] = st([m['lam'] for m in ev])
    G['fox_bf'] = st([m['bf'] for m in ev])
    G['fox_q_norm'] = st([m['qn'] for m in ev])
    G['fox_k_norm'] = st([m['kn'] for m in ev])
    G['od_w_in'] = st([m['w_in'] for m in od])
    G['od_w_out'] = st([m['w_out'] for m in od])
    G['swa_q_norm'] = st([m['qn'] for m in od])
    G['swa_k_norm'] = st([m['kn'] for m in od])
    G['swa_sinks'] = st([m['sinks'] for m in od])
    G['s5_lambda_re'] = st([m['s5']['lre'] for m in od])
    G['s5_lambda_im'] = st([m['s5']['lim'] for m in od])
    G['s5_log_dt'] = st([m['s5']['ldt'] for m in od])
    G['s5_b_re'] = st([m['s5']['bre'] for m in od])
    G['s5_b_im'] = st([m['s5']['bim'] for m in od])
    G['s5_c_re'] = st([m['s5']['cre'] for m in od])
    G['s5_c_im'] = st([m['s5']['cim'] for m in od])
    G['s5_d'] = st([m['s5_d'] for m in od])
    G['s5_glu_w'] = st([m['glu_w'] for m in od])
    G['s5_glu_b'] = st([m['glu_b'] for m in od])
    return G


def local_step(x, p, target, ffnw, full, small):
    depth = p.shape[0]
    S, D = x.shape
    ws = [_layer_weights(full, small, i, depth) for i in range(depth)]
    saved = []
    xi = x
    for i in range(depth):
        xi, sv = layer_fwd(xi, p[i], ws[i], ffnw, i)
        saved.append(sv)

    def f_loss(y, t):
        e = y - t
        return [e * (1.0 / D)], [0.5 * jnp.sum(jnp.mean(e * e, axis=-1, keepdims=True), axis=0, keepdims=True)]

    dx, loss = rowwise(f_loss, [xi, target], outs=[(D, F32)], accs=[(1, 1)], name="loss")
    grads = [None] * depth
    gbuf = (None, None)
    for i in reversed(range(depth)):
        dx, grads[i], gbuf = layer_bwd(dx, saved[i], ws[i], ffnw, gbuf, i)
        if i % 2 == 1:
            m = grads[i]['mix']
            m['s5'] = s5_prep_bwd(ws[i]['mix']['s5_ins'], m['s5_lam'], m['s5_bexp'], m['s5_cexp'], f"L{i}")
    return loss[0, 0], dx, gbuf, _collect_grads(grads, depth)


def kernel(x, p, ffn1_norm, ffn1_wg, ffn1_wu, ffn1_wd, mix_norm, ffn2_norm, ffn2_wg, ffn2_wu, ffn2_wd, ple_w, ple_norm, ple_gate_norm, ple_gate_w, ev_w_in, lru_conv_w, lru_conv_b, lru_wa, lru_ba, lru_wx, lru_bx, lru_lambda, fox_bf, fox_q_norm, fox_k_norm, ev_w_out, od_w_in, swa_q_norm, swa_k_norm, swa_sinks, s5_lambda_re, s5_lambda_im, s5_log_dt, s5_b_re, s5_b_im, s5_c_re, s5_c_im, s5_d, s5_glu_w, s5_glu_b, od_w_out, loss_target, m_ffn1_norm, m_ffn1_wg, m_ffn1_wu, m_ffn1_wd, m_mix_norm, m_ffn2_norm, m_ffn2_wg, m_ffn2_wu, m_ffn2_wd, m_ple_w, m_ple_norm, m_ple_gate_norm, m_ple_gate_w, m_ev_w_in, m_lru_conv_w, m_lru_conv_b, m_lru_wa, m_lru_ba, m_lru_wx, m_lru_bx, m_lru_lambda, m_fox_bf, m_fox_q_norm, m_fox_k_norm, m_ev_w_out, m_od_w_in, m_swa_q_norm, m_swa_k_norm, m_swa_sinks, m_s5_lambda_re, m_s5_lambda_im, m_s5_log_dt, m_s5_b_re, m_s5_b_im, m_s5_c_re, m_s5_c_im, m_s5_d, m_s5_glu_w, m_s5_glu_b, m_od_w_out, v_ffn1_norm, v_ffn1_wg, v_ffn1_wu, v_ffn1_wd, v_mix_norm, v_ffn2_norm, v_ffn2_wg, v_ffn2_wu, v_ffn2_wd, v_ple_w, v_ple_norm, v_ple_gate_norm, v_ple_gate_w, v_ev_w_in, v_lru_conv_w, v_lru_conv_b, v_lru_wa, v_lru_ba, v_lru_wx, v_lru_bx, v_lru_lambda, v_fox_bf, v_fox_q_norm, v_fox_k_norm, v_ev_w_out, v_od_w_in, v_swa_q_norm, v_swa_k_norm, v_swa_sinks, v_s5_lambda_re, v_s5_lambda_im, v_s5_log_dt, v_s5_b_re, v_s5_b_im, v_s5_c_re, v_s5_c_im, v_s5_d, v_s5_glu_w, v_s5_glu_b, v_od_w_out):
    args = locals()
    wts = {n: args[n] for n in WEIGHTS}
    ms = {n: args["m_" + n] for n in WEIGHTS}
    vs = {n: args["v_" + n] for n in WEIGHTS}
    shapes = {n: wts[n].shape for n in WEIGHTS}

    depth = p.shape[0]
    gw = gather_weights({n: wts[n] for n in SHARDED})
    small = {n: wts[n] for n in REPLICATED}
    loss, dx, (gwgu, gwd), G = local_step(x[0], p[:, 0], loss_target[0], (gw['wgu'], gw['wd']),
                                          full_weights(gw, depth), small)
    loss = lax.psum(loss, ("x", "y", "c"))

    gsh = ungroup(reduce_sharded(grad_groups(gwgu, gwd, G)), shapes)
    full_shapes = {n: (G[n].shape if n in EXACT_SHARDED else shapes[n]) for n in SMALL_GRADS}
    flat_g = _flatten_small(G, full_shapes)
    g8 = gather_devices(flat_g, "gather_small_grads").reshape((N_DEV,) + flat_g.shape)
    wf, mf, vf = (_flatten_small({n: t[n] for n in REPLICATED}, full_shapes) for t in (wts, ms, vs))

    def f_small(g0, g1, g2, g3, g4, g5, g6, g7, w_, m_, v_):
        gsum = ((g0 + g1) + (g2 + g3)) + ((g4 + g5) + (g6 + g7))
        return [gsum] + list(_adam(w_, gsum, m_, v_)), []

    gs_f, ds_f, ms_f, vs_f = rowwise(f_small, [g8[d] for d in range(N_DEV)] + [wf, mf, vf],
                                     outs=[(LANES, F32)] * 4, name="adam_small")
    out_g, out_d, out_m, out_v = {}, {}, {}, {}
    for dst, flat in ((out_g, gs_f), (out_d, ds_f), (out_m, ms_f), (out_v, vs_f)):
        dst.update(_unflatten_small(flat, full_shapes))
    for n in EXACT_SHARDED:
        width = shapes[n][SHARD_AXIS[n]]
        gsh[n] = lax.dynamic_slice_in_dim(out_g[n], _chip() * width, width, axis=SHARD_AXIS[n])
    for n in SHARDED:
        out_g[n] = gsh[n]
        out_d[n], out_m[n], out_v[n] = adam_2d(wts[n], gsh[n], ms[n], vs[n], f"adam_{n}")
    return (loss, dx[None], *[out_g[n] for n in WEIGHTS], *[out_d[n] for n in WEIGHTS],
            *[out_m[n] for n in WEIGHTS], *[out_v[n] for n in WEIGHTS])
```

```python
import functools
import math

import jax
import jax.numpy as jnp
from jax import lax
from jax.experimental import pallas as pl
from jax.experimental.pallas import tpu as pltpu

F32 = jnp.float32
BF16 = jnp.bfloat16
MXU_DTYPE = BF16
HI = lax.Precision.HIGHEST
MESH = pl.DeviceIdType.MESH

VMEM_LIMIT_BYTES = 56 * 1024 * 1024
ROW_TILE_BYTES = 5 * 1024 * 1024
MM_VMEM_BYTES = 40 * 1024 * 1024
MM_TILE_M = 1024
MM_TILE_N = 1408
FLAT_W = 2048
LANES = 128
SUBLANES = 8

HEAD_DIM = 64
LRU_BLOCKS = 8
LRU_CONV = 4
LRU_C = 8.0
SWA_WINDOW = 128
SWA_GROUP = 4
S5_GROUP = 16
S5_GROUPS = 32
S5_STATE = 64
ROPE_THETA = 10000.0
EPS = 1e-6
MACARON = 0.5
NEG = -1e30

ADAM_LR = 0.001
ADAM_B1 = 0.9
ADAM_B2 = 0.999
ADAM_EPS = 1e-08
ADAM_WD = 0.01
ADAM_STEP = 10

N_CHIPS = 4
N_DEV = 8


def _pick(n, cands):
    for c in cands:
        if n % c == 0:
            return c
    return n


def _tile(n, cap, unit):
    best = None
    for t in range(unit, min(n, cap) + 1, unit):
        if n % t == 0:
            best = t
    return n if best is None else best


def _params(sem=None):
    return pltpu.CompilerParams(dimension_semantics=sem, vmem_limit_bytes=VMEM_LIMIT_BYTES)


def rowwise(fn, rows, consts=(), outs=(), accs=(), name="rowwise", periods=None):
    rows, consts = list(rows), list(consts)
    n_r, n_c, n_o, n_a = len(rows), len(consts), len(outs), len(accs)
    R = rows[0].shape[0]
    periods = list(periods) if periods is not None else [None] * n_r
    per_row = sum(max(r.shape[1], LANES) * 4 for r in rows) + sum(max(f, LANES) * 4 for f, _ in outs)
    limit = min([R] + [p for p in periods if p is not None])
    tr = limit
    for c in (1024, 512, 256, 128, 64, 32, 16):
        if c <= limit and limit % c == 0 and R % c == 0 and c * per_row <= ROW_TILE_BYTES:
            tr = c
            break

    def row_map(period):
        if period is None:
            return lambda i: (i, 0)
        nb = period // tr
        return lambda i: (i % nb, 0)

    in_specs = [pl.BlockSpec((tr, r.shape[1]), row_map(p)) for r, p in zip(rows, periods)]
    in_specs += [pl.BlockSpec(c.shape, lambda i: (0, 0)) for c in consts]
    out_shape = [jax.ShapeDtypeStruct((R, f), dt) for f, dt in outs]
    out_shape += [jax.ShapeDtypeStruct(tuple(s), F32) for s in accs]
    out_specs = [pl.BlockSpec((tr, f), lambda i: (i, 0)) for f, _ in outs]
    out_specs += [pl.BlockSpec(tuple(s), lambda i: (0, 0)) for s in accs]

    def body(*refs):
        ins = [r[...] for r in refs[:n_r + n_c]]
        o_refs = refs[n_r + n_c:n_r + n_c + n_o]
        a_refs = refs[n_r + n_c + n_o:]
        ro, ra = fn(*ins)
        for ref, val in zip(o_refs, ro):
            ref[...] = val.astype(ref.dtype)
        if n_a:
            @pl.when(pl.program_id(0) == 0)
            def _():
                for ref in a_refs:
                    ref[...] = jnp.zeros(ref.shape, ref.dtype)
            for ref, val in zip(a_refs, ra):
                ref[...] += val.astype(F32)

    res = pl.pallas_call(
        body, grid=(R // tr,), in_specs=in_specs, out_specs=out_specs, out_shape=out_shape,
        name=name, compiler_params=_params(("arbitrary",)),
    )(*rows, *consts)
    return list(res)


def whole(fn, ins, outs, name="whole"):
    n_i = len(ins)

    def body(*refs):
        vals = fn(*[r[...] for r in refs[:n_i]])
        for ref, val in zip(refs[n_i:], vals):
            ref[...] = val.astype(ref.dtype)

    res = pl.pallas_call(
        body, out_shape=[jax.ShapeDtypeStruct(tuple(s), dt) for s, dt in outs],
        in_specs=[pl.BlockSpec(memory_space=pltpu.VMEM)] * n_i,
        out_specs=[pl.BlockSpec(memory_space=pltpu.VMEM)] * len(outs),
        name=name, compiler_params=_params(),
    )(*ins)
    return list(res)


_DOT_DIMS = {
    "nn": (((1,), (0,)), ((), ())),
    "nt": (((1,), (1,)), ((), ())),
    "tn": (((0,), (0,)), ((), ())),
}


def mm(a, b, mode="nn", out_dtype=F32, name="mm"):
    if mode == "nn":
        (M, K), (K2, N) = a.shape, b.shape
    elif mode == "nt":
        (M, K), (N, K2) = a.shape, b.shape
    else:
        (K, M), (K2, N) = a.shape, b.shape
    assert K == K2, (mode, a.shape, b.shape)
    tn = _tile(N, MM_TILE_N, LANES)
    if mode == "tn":
        tm, tk = _tile(M, MM_TILE_M, LANES), _tile(K, MM_TILE_M, 2 * SUBLANES)
    else:
        tm, tk = _tile(M, MM_TILE_M, 2 * SUBLANES), _tile(K, MM_TILE_N, LANES)

    def vmem_bytes(tm_, tk_):
        return (2 * (tm_ * tk_ * a.dtype.itemsize + tk_ * tn * b.dtype.itemsize
                     + tm_ * tn * jnp.dtype(out_dtype).itemsize) + tm_ * tn * 4)

    while vmem_bytes(tm, tk) > MM_VMEM_BYTES and tk % (2 * LANES) == 0 and K % (tk // 2) == 0:
        tk //= 2
    while vmem_bytes(tm, tk) > MM_VMEM_BYTES and tm % (2 * LANES) == 0 and M % (tm // 2) == 0:
        tm //= 2
    if mode == "tn":
        a_spec = pl.BlockSpec((tk, tm), lambda i, j, k: (k, i))
    else:
        a_spec = pl.BlockSpec((tm, tk), lambda i, j, k: (i, k))
    if mode == "nt":
        b_spec = pl.BlockSpec((tn, tk), lambda i, j, k: (j, k))
    else:
        b_spec = pl.BlockSpec((tk, tn), lambda i, j, k: (k, j))
    nk = K // tk
    dims = _DOT_DIMS[mode]

    def dot(a_ref, b_ref):
        return lax.dot_general(a_ref[...].astype(MXU_DTYPE), b_ref[...].astype(MXU_DTYPE), dims,
                               preferred_element_type=F32)

    def body_one(a_ref, b_ref, o_ref):
        o_ref[...] = dot(a_ref, b_ref).astype(o_ref.dtype)

    def body_acc(a_ref, b_ref, o_ref, acc_ref):
        k = pl.program_id(2)

        @pl.when(k == 0)
        def _():
            acc_ref[...] = dot(a_ref, b_ref)

        @pl.when(k > 0)
        def _():
            acc_ref[...] += dot(a_ref, b_ref)

        @pl.when(k == nk - 1)
        def _():
            o_ref[...] = acc_ref[...].astype(o_ref.dtype)

    return pl.pallas_call(
        body_one if nk == 1 else body_acc, grid=(M // tm, N // tn, nk), in_specs=[a_spec, b_spec],
        out_specs=pl.BlockSpec((tm, tn), lambda i, j, k: (i, j)),
        out_shape=jax.ShapeDtypeStruct((M, N), out_dtype),
        scratch_shapes=[] if nk == 1 else [pltpu.VMEM((tm, tn), F32)],
        name=name, compiler_params=_params(("parallel", "parallel", "arbitrary")),
    )(a, b)


def _roll_rows(x, d, reverse):
    return pltpu.roll(x, (SUBLANES - d) if reverse else d, 0)


def scan_real(a, b, reverse=False, name="scan_real"):
    S, W = b.shape
    cw = _pick(W, (256, 128))
    n_tiles = S // SUBLANES

    def body(a_ref, b_ref, o_ref):
        row = lax.broadcasted_iota(jnp.int32, (SUBLANES, cw), 0)
        edge = 0 if reverse else SUBLANES - 1

        def step(i, carry):
            t = (n_tiles - 1 - i) if reverse else i
            off = pl.multiple_of(t * SUBLANES, SUBLANES)
            A = a_ref[pl.ds(off, SUBLANES), :]
            B = b_ref[pl.ds(off, SUBLANES), :]
            for d in (1, 2, 4):
                m = (row < SUBLANES - d) if reverse else (row >= d)
                B = jnp.where(m, A * _roll_rows(B, d, reverse) + B, B)
                A = jnp.where(m, A * _roll_rows(A, d, reverse), A)
            o_ref[pl.ds(off, SUBLANES), :] = B + A * carry
            at_edge = row == edge
            return (jnp.sum(jnp.where(at_edge, B, 0.0), axis=0, keepdims=True)
                    + jnp.sum(jnp.where(at_edge, A, 0.0), axis=0, keepdims=True) * carry)

        lax.fori_loop(0, n_tiles, step, jnp.zeros((1, cw), F32), unroll=2)

    spec = pl.BlockSpec((S, cw), lambda j: (0, j))
    return pl.pallas_call(
        body, grid=(W // cw,), in_specs=[spec, spec], out_specs=spec,
        out_shape=jax.ShapeDtypeStruct((S, W), F32), name=name, compiler_params=_params(("parallel",)),
    )(a, b)


def scan_cplx(lam, bu, reverse=False, name="scan_cplx"):
    S, C = bu.shape
    half = LANES
    CB = _pick(C, (1024, 512, 256))
    TS = _pick(S, (1024, 512, 256, 128, 64, 32, 16, 8))
    groups = CB // (2 * half)
    n_blocks, n_tiles = S // TS, TS // SUBLANES

    def cmul(ar, ai, br, bi):
        return ar * br - ai * bi, ar * bi + ai * br

    def body(lam_ref, bu_ref, o_ref, carry_ref):
        row = lax.broadcasted_iota(jnp.int32, (SUBLANES, half), 0)

        def edge_row(v):
            return jnp.sum(jnp.where(row == (0 if reverse else SUBLANES - 1), v, 0.0), axis=0, keepdims=True)

        @pl.when(pl.program_id(1) == 0)
        def _():
            carry_ref[...] = jnp.zeros(carry_ref.shape, F32)

        consts = []
        for g in range(groups):
            lr = lam_ref[:, 2 * half * g:2 * half * g + half]
            li = lam_ref[:, 2 * half * g + half:2 * half * (g + 1)]
            if reverse:
                li = -li
            l1 = (lr, li)
            l2 = cmul(*l1, *l1)
            l4 = cmul(*l2, *l2)
            pr = jnp.zeros((SUBLANES, half), F32)
            pi = jnp.zeros((SUBLANES, half), F32)
            p = l1
            for r in range(SUBLANES):
                sel = row == ((SUBLANES - 1 - r) if reverse else r)
                pr = jnp.where(sel, p[0], pr)
                pi = jnp.where(sel, p[1], pi)
                p = cmul(*p, *l1)
            consts.append((l1, l2, l4, pr, pi, edge_row(pr), edge_row(pi)))

        def step(i, carry):
            t = (n_tiles - 1 - i) if reverse else i
            off = pl.multiple_of(t * SUBLANES, SUBLANES)
            out = []
            for g in range(groups):
                l1, l2, l4, pr, pi, p8r, p8i = consts[g]
                cr, ci = carry[2 * g], carry[2 * g + 1]
                re, im = pl.ds(2 * half * g, half), pl.ds(2 * half * g + half, half)
                Br = bu_ref[pl.ds(off, SUBLANES), re]
                Bi = bu_ref[pl.ds(off, SUBLANES), im]
                for d, (qr, qi) in ((1, l1), (2, l2), (4, l4)):
                    m = (row < SUBLANES - d) if reverse else (row >= d)
                    sr, si = _roll_rows(Br, d, reverse), _roll_rows(Bi, d, reverse)
                    nr = jnp.where(m, Br + qr * sr - qi * si, Br)
                    ni = jnp.where(m, Bi + qr * si + qi * sr, Bi)
                    Br, Bi = nr, ni
                o_ref[pl.ds(off, SUBLANES), re] = Br + pr * cr - pi * ci
                o_ref[pl.ds(off, SUBLANES), im] = Bi + pr * ci + pi * cr
                er, ei = edge_row(Br), edge_row(Bi)
                out += [er + p8r * cr - p8i * ci, ei + p8r * ci + p8i * cr]
            return tuple(out)

        carry0 = tuple(carry_ref[:, pl.ds(half * k, half)] for k in range(2 * groups))
        carry1 = lax.fori_loop(0, n_tiles, step, carry0, unroll=2)
        for k in range(2 * groups):
            carry_ref[:, pl.ds(half * k, half)] = carry1[k]

    def rows(j, t):
        return ((n_blocks - 1 - t) if reverse else t, j)

    spec = pl.BlockSpec((TS, CB), rows)
    return pl.pallas_call(
        body, grid=(C // CB, n_blocks), in_specs=[pl.BlockSpec((1, CB), lambda j, t: (0, j)), spec],
        out_specs=spec, out_shape=jax.ShapeDtypeStruct((S, C), F32), scratch_shapes=[pltpu.VMEM((1, CB), F32)],
        name=name, compiler_params=_params(("parallel", "arbitrary")),
    )(lam, bu)


ATTN_HEADS_PER_STEP = 2


def _attn_tile(S, window):
    if window is None:
        return _pick(S, (512, 256, 128))
    return max(window, _pick(S, (256, 128)))


def _attn_valid(q_blk, k_blk, T, window):
    kpos = k_blk * T + lax.broadcasted_iota(jnp.int32, (T, T), 0)
    qpos = q_blk * T + lax.broadcasted_iota(jnp.int32, (T, T), 1)
    valid = kpos <= qpos
    if window is not None:
        valid = valid & (qpos - kpos < window)
    return valid


def attn_fwd(q, k, v, sink, cq=None, ck=None, window=None, name="attn_fwd"):
    H, S, Dh = q.shape
    G = H // k.shape[0]
    HP = ATTN_HEADS_PER_STEP
    assert H % HP == 0 and (G == 1 or G % HP == 0)
    KP = HP if G == 1 else 1
    T = _attn_tile(S, window)
    nq = S // T
    nks = nq if window is None else 2
    scale = Dh ** -0.5
    bias = cq is not None

    def kv_block(i, j):
        return jnp.minimum(j, i) if window is None else jnp.maximum(i - 1 + j, 0)

    def body(*refs):
        if bias:
            q_ref, k_ref, v_ref, s_ref, cq_ref, ck_ref, o_ref, lse_ref, m_scr, l_scr, acc_scr = refs
        else:
            q_ref, k_ref, v_ref, s_ref, o_ref, lse_ref, m_scr, l_scr, acc_scr = refs
        i, j = pl.program_id(1), pl.program_id(2)

        @pl.when(j == 0)
        def _():
            m_scr[...] = jnp.zeros(m_scr.shape, F32) + s_ref[...]
            l_scr[...] = jnp.ones(l_scr.shape, F32)
            acc_scr[...] = jnp.zeros(acc_scr.shape, F32)

        def block(masked):
            valid = _attn_valid(i, kv_block(i, j), T, window) if masked else None
            for b in range(HP):
                kvb = b if G == 1 else 0
                s = _dotf(k_ref[kvb], q_ref[b], "nt") * scale
                if bias:
                    s = s + cq_ref[b] - ck_ref[b]
                if masked:
                    s = jnp.where(valid, s, NEG)
                m_old = m_scr[b]
                m_new = jnp.maximum(m_old, jnp.max(s, axis=0, keepdims=True))
                alpha = jnp.exp(m_old - m_new)
                p = jnp.exp(s - m_new)
                l_scr[b] = alpha * l_scr[b] + jnp.sum(p, axis=0, keepdims=True)
                acc_scr[b] = alpha * acc_scr[b] + _dotf(v_ref[kvb], p, "tn")
                m_scr[b] = m_new

        if window is None:
            pl.when(j < i)(lambda: block(False))
            pl.when(j == i)(lambda: block(True))
        else:
            pl.when(i - 1 + j >= 0)(lambda: block(True))

        @pl.when(j == nks - 1)
        def _():
            o_ref[...] = acc_scr[...] / l_scr[...]
            lse_ref[...] = m_scr[...] + jnp.log(l_scr[...])

    def kv_map(hp, i, j):
        return (hp if G == 1 else (hp * HP) // G, kv_block(i, j), 0)

    in_specs = [
        pl.BlockSpec((HP, T, Dh), lambda hp, i, j: (hp, i, 0)),
        pl.BlockSpec((KP, T, Dh), kv_map),
        pl.BlockSpec((KP, T, Dh), kv_map),
        pl.BlockSpec((HP, 1, 1), lambda hp, i, j: (hp, 0, 0)),
    ]
    args = [q, k, v, sink]
    if bias:
        in_specs += [pl.BlockSpec((HP, 1, T), lambda hp, i, j: (hp, 0, i)),
                     pl.BlockSpec((HP, T, 1), lambda hp, i, j: (hp, kv_block(i, j), 0))]
        args += [cq, ck]
    return pl.pallas_call(
        body, grid=(H // HP, nq, nks), in_specs=in_specs,
        out_specs=[pl.BlockSpec((HP, Dh, T), lambda hp, i, j: (hp, 0, i)),
                   pl.BlockSpec((HP, 1, T), lambda hp, i, j: (hp, 0, i))],
        out_shape=[jax.ShapeDtypeStruct((H, Dh, S), F32), jax.ShapeDtypeStruct((H, 1, S), F32)],
        scratch_shapes=[pltpu.VMEM((HP, 1, T), F32), pltpu.VMEM((HP, 1, T), F32), pltpu.VMEM((HP, Dh, T), F32)],
        name=name, compiler_params=_params(("parallel", "parallel", "arbitrary")),
    )(*args)


def attn_bwd(q, k, v, lse, do, delta, cq=None, ck=None, window=None, name="attn_bwd"):
    H, S, Dh = q.shape
    KVH = k.shape[0]
    G = H // KVH
    HP = ATTN_HEADS_PER_STEP
    assert H % HP == 0 and (G == 1 or G % HP == 0)
    pair_kv = G == 1
    KP = HP if pair_kv else 1
    T = _attn_tile(S, window)
    nq = S // T
    nqs = nq if window is None else 2
    scale = Dh ** -0.5
    bias = cq is not None
    assert not bias or G == 1

    def q_block(kb, j):
        return jnp.maximum(j, kb) if window is None else jnp.minimum(kb + j, nq - 1)

    def body(*refs):
        if bias:
            (q_ref, k_ref, v_ref, lse_ref, do_ref, dl_ref, cq_ref, ck_ref,
             dq_ref, dk_ref, dv_ref, dcq_ref, dck_ref) = refs
        else:
            q_ref, k_ref, v_ref, lse_ref, do_ref, dl_ref, dq_ref, dk_ref, dv_ref = refs
        kb, gp, j = pl.program_id(1), pl.program_id(2), pl.program_id(3)

        @pl.when((gp == 0) & (j == 0))
        def _():
            dk_ref[...] = jnp.zeros(dk_ref.shape, F32)
            dv_ref[...] = jnp.zeros(dv_ref.shape, F32)
            if bias:
                dck_ref[...] = jnp.zeros(dck_ref.shape, F32)

        @pl.when((kb == 0) & (gp == 0) & (j == 0))
        def _():
            dq_ref[...] = jnp.zeros(dq_ref.shape, F32)
            if bias:
                dcq_ref[...] = jnp.zeros(dcq_ref.shape, F32)

        def block(masked):
            qi = q_block(kb, j)
            off = pl.multiple_of(qi * T, T)
            valid = _attn_valid(qi, kb, T, window) if masked else None
            for b in range(HP):
                kvb = b if pair_kv else 0
                g = 0 if pair_kv else gp * HP + b
                qb, kk, vv = q_ref[b].astype(MXU_DTYPE), k_ref[kvb].astype(MXU_DTYPE), v_ref[kvb].astype(MXU_DTYPE)
                dob = do_ref[b].astype(MXU_DTYPE)
                s = _dotf(kk, qb, "nt") * scale
                if bias:
                    s = s + cq_ref[b] - ck_ref[b]
                if masked:
                    s = jnp.where(valid, s, NEG)
                p = jnp.exp(s - lse_ref[b])
                dv_ref[kvb] += _dotf(p, dob, "nt")
                ds = p * (_dotf(vv, dob) - dl_ref[b])
                dsb = ds.astype(MXU_DTYPE)
                dk_ref[kvb] += scale * _dotf(dsb, qb)
                dq_ref[kvb, g, pl.ds(off, T), :] += scale * _dotf(dsb, kk, "tn")
                if bias:
                    dcq_ref[kvb, g, :, pl.ds(off, T)] += jnp.sum(ds, axis=0, keepdims=True)
                    dck_ref[kvb] -= jnp.sum(ds, axis=1, keepdims=True)

        if window is None:
            pl.when(j > kb)(lambda: block(False))
            pl.when(j == kb)(lambda: block(True))
        else:
            pl.when(kb + j <= nq - 1)(lambda: block(True))

    def qmap(kvp, kb, gp, j):
        return (kvp if pair_kv else (kvp * G) // HP + gp, q_block(kb, j), 0)

    def qmap_t(kvp, kb, gp, j):
        return (kvp if pair_kv else (kvp * G) // HP + gp, 0, q_block(kb, j))

    in_specs = [
        pl.BlockSpec((HP, T, Dh), qmap),
        pl.BlockSpec((KP, T, Dh), lambda kvp, kb, gp, j: (kvp, kb, 0)),
        pl.BlockSpec((KP, T, Dh), lambda kvp, kb, gp, j: (kvp, kb, 0)),
        pl.BlockSpec((HP, 1, T), qmap_t),
        pl.BlockSpec((HP, Dh, T), qmap_t),
        pl.BlockSpec((HP, 1, T), qmap_t),
    ]
    args = [q, k, v, lse, do, delta]
    out_specs = [
        pl.BlockSpec((KP, G, S, Dh), lambda kvp, kb, gp, j: (kvp, 0, 0, 0)),
        pl.BlockSpec((KP, T, Dh), lambda kvp, kb, gp, j: (kvp, kb, 0)),
        pl.BlockSpec((KP, T, Dh), lambda kvp, kb, gp, j: (kvp, kb, 0)),
    ]
    out_shape = [jax.ShapeDtypeStruct((KVH, G, S, Dh), F32), jax.ShapeDtypeStruct((KVH, S, Dh), F32),
                 jax.ShapeDtypeStruct((KVH, S, Dh), F32)]
    if bias:
        in_specs += [pl.BlockSpec((HP, 1, T), qmap_t),
                     pl.BlockSpec((HP, T, 1), lambda kvp, kb, gp, j: (kvp, kb, 0))]
        args += [cq, ck]
        out_specs += [pl.BlockSpec((KP, G, 1, S), lambda kvp, kb, gp, j: (kvp, 0, 0, 0)),
                      pl.BlockSpec((KP, T, 1), lambda kvp, kb, gp, j: (kvp, kb, 0))]
        out_shape += [jax.ShapeDtypeStruct((KVH, G, 1, S), F32), jax.ShapeDtypeStruct((KVH, S, 1), F32)]
    res = pl.pallas_call(
        body, grid=(KVH // KP, nq, 1 if pair_kv else G // HP, nqs), in_specs=in_specs, out_specs=out_specs,
        out_shape=out_shape, name=name, compiler_params=_params(("arbitrary", "arbitrary", "arbitrary", "arbitrary")),
    )(*args)
    dq = res[0].reshape(H, S, Dh)
    if bias:
        return dq, res[1], res[2], res[3].reshape(H, 1, S), res[4]
    return dq, res[1], res[2]


def _rms(x, g):
    return x * lax.rsqrt(jnp.mean(x * x, axis=-1, keepdims=True) + EPS) * g


def _sigmoid(x):
    return 1.0 / (1.0 + jnp.exp(-x))


def _softplus(x):
    return jnp.maximum(x, 0.0) + jnp.log(1.0 + jnp.exp(-jnp.abs(x)))


def _log_sigmoid(x):
    return jnp.minimum(x, 0.0) - jnp.log(1.0 + jnp.exp(-jnp.abs(x)))


def _gelu(x):
    return 0.5 * x * (1.0 + jnp.tanh(math.sqrt(2.0 / math.pi) * (x + 0.044715 * (x * x * x))))


def _silu(x):
    return x * _sigmoid(x)


def _ffn_act(gu):
    f = gu.shape[1] // 2
    return MACARON * _silu(gu[:, :f]) * gu[:, f:]


def _qk_prep(rope):
    def f(x, *rest):
        if rope:
            cos, sin, g, rot = rest
        else:
            (g,) = rest
        y = _rms(x, g)
        if rope:
            y = y * cos + jnp.dot(y, rot, precision=HI, preferred_element_type=F32) * sin
        return y
    return f


def _lru_gates(pre, xc, ba, bx, lam):
    w = xc.shape[1]
    r = _sigmoid(pre[:, :w] + ba)
    i = _sigmoid(pre[:, w:] + bx)
    log_a = -LRU_C * r * _softplus(lam)
    a = jnp.exp(log_a)
    b = jnp.sqrt(1.0 - jnp.exp(2.0 * log_a)) * (i * xc)
    return a, b


def _lru_conv(x0, x1, x2, x3, w0, w1, w2, w3, cb):
    return cb + x0 * w0 + x1 * w1 + x2 * w2 + x3 * w3


def _s5_params(lre, lim, ldt, gsel, bre, bim):
    dt = jnp.sum(gsel * jnp.exp(ldt), axis=1, keepdims=True)
    er = jnp.exp(lre * dt)
    ang = lim * dt
    lbr, lbi = er * jnp.cos(ang), er * jnp.sin(ang)
    nr, ni = lbr - 1.0, lbi
    den = lre * lre + lim * lim
    fr, fi = (nr * lre + ni * lim) / den, (ni * lre - nr * lim) / den
    return lbr, lbi, fr * bre - fi * bim, fr * bim + fi * bre


def _s5_out(yssm, u, d):
    return _gelu(yssm + d * u)


def _glu(z, gl, gb):
    return z * _sigmoid(gl + gb)


def _ple_out(x, gpre, epre, pn):
    return x + _sigmoid(gpre) * _rms(epre, pn)


def _vjp(fn, args, cots):
    _, pull = jax.vjp(fn, *args)
    return pull(cots)


def add_norm(x, y, g, name):
    D = x.shape[1]
    if y is None:
        return x, rowwise(lambda xv, gv: ([_rms(xv, gv)], []), [x], [g], outs=[(D, BF16)], name=name)[0]
    xn, n = rowwise(lambda xv, yv, gv: ([xv + yv, _rms(xv + yv, gv)], []), [x, y], [g],
                    outs=[(D, F32), (D, BF16)], name=name)
    return xn, n


def norm_bwd(x, g, dn, dx_res, name):
    D = x.shape[1]

    def f(xv, dnv, dxv, gv):
        dx, dg = _vjp(_rms, (xv, gv), dnv)
        return [dxv + dx], [dg]

    return rowwise(f, [x, dn, dx_res], [g], outs=[(D, F32)], accs=[(1, D)], name=name)


def _swiglu(g, u):
    return MACARON * _silu(g) * u


def _dotf(a, b, mode="nn"):
    return lax.dot_general(a.astype(MXU_DTYPE), b.astype(MXU_DTYPE), _DOT_DIMS[mode], preferred_element_type=F32)


def ffn_up(n, wgu, ig, iu, name):
    S, D = n.shape
    C, _, _, Fc = wgu.shape
    tm = _tile(S, MM_TILE_M, 2 * SUBLANES)

    def body(n_ref, wg_ref, wu_ref, g_ref, u_ref, a_ref):
        g = _dotf(n_ref[...], wg_ref[...])
        u = _dotf(n_ref[...], wu_ref[...])
        g_ref[...] = g.astype(g_ref.dtype)
        u_ref[...] = u.astype(u_ref.dtype)
        a_ref[...] = _swiglu(g, u).astype(a_ref.dtype)

    hid = pl.BlockSpec((None, tm, Fc), lambda s, i: (s, i, 0))
    return pl.pallas_call(
        body, grid=(C, S // tm),
        in_specs=[pl.BlockSpec((tm, D), lambda s, i: (i, 0)),
                  pl.BlockSpec((None, None, D, Fc), lambda s, i: (s, ig, 0, 0)),
                  pl.BlockSpec((None, None, D, Fc), lambda s, i: (s, iu, 0, 0))],
        out_specs=[hid, hid, hid], out_shape=[jax.ShapeDtypeStruct((C, S, Fc), BF16)] * 3,
        name=name, compiler_params=_params(("parallel", "parallel")),
    )(n, wgu, wgu)


def ffn_down(act, wd, iw, x, gain, name):
    C, S, Fc = act.shape
    D = wd.shape[-1]
    tm = _tile(S, MM_TILE_M, 2 * SUBLANES)

    def body(a_ref, w_ref, x_ref, g_ref, xo_ref, n_ref, acc_ref):
        s = pl.program_id(1)
        r = _dotf(a_ref[...], w_ref[...])

        @pl.when(s == 0)
        def _():
            acc_ref[...] = r

        @pl.when(s > 0)
        def _():
            acc_ref[...] += r

        @pl.when(s == C - 1)
        def _():
            xn = x_ref[...] + acc_ref[...]
            xo_ref[...] = xn
            n_ref[...] = _rms(xn, g_ref[...]).astype(n_ref.dtype)

    row = pl.BlockSpec((tm, D), lambda i, s: (i, 0))
    return pl.pallas_call(
        body, grid=(S // tm, C),
        in_specs=[pl.BlockSpec((None, tm, Fc), lambda i, s: (s, i, 0)),
                  pl.BlockSpec((None, None, Fc, D), lambda i, s: (s, iw, 0, 0)), row,
                  pl.BlockSpec((1, D), lambda i, s: (0, 0))],
        out_specs=[row, row], out_shape=[jax.ShapeDtypeStruct((S, D), F32), jax.ShapeDtypeStruct((S, D), BF16)],
        scratch_shapes=[pltpu.VMEM((tm, D), F32)], name=name, compiler_params=_params(("parallel", "arbitrary")),
    )(act, wd, x, gain)


def ffn_down_bwd(dy, wd, iw, g, u, name):
    C, S, Fc = g.shape
    D = dy.shape[1]
    tm = _tile(S, MM_TILE_M, 2 * SUBLANES)

    def body(dy_ref, w_ref, g_ref, u_ref, dg_ref, du_ref):
        dact = MACARON * _dotf(dy_ref[...], w_ref[...], "nt")
        g, u = g_ref[...].astype(F32), u_ref[...].astype(F32)
        sg = _sigmoid(g)
        gs = g * sg
        dg_ref[...] = (dact * u * (sg + gs * (1.0 - sg))).astype(dg_ref.dtype)
        du_ref[...] = (dact * gs).astype(du_ref.dtype)

    hid = pl.BlockSpec((None, tm, Fc), lambda s, i: (s, i, 0))
    return pl.pallas_call(
        body, grid=(C, S // tm),
        in_specs=[pl.BlockSpec((tm, D), lambda s, i: (i, 0)),
                  pl.BlockSpec((None, None, Fc, D), lambda s, i: (s, iw, 0, 0)), hid, hid],
        out_specs=[hid, hid], out_shape=[jax.ShapeDtypeStruct((C, S, Fc), BF16)] * 2,
        name=name, compiler_params=_params(("parallel", "parallel")),
    )(dy, wd, g, u)


def ffn_dn(dg, du, wgu, ig, iu, x, gain, dx_res, name):
    C, S, Fc = dg.shape
    D = wgu.shape[2]
    tm = _tile(S, MM_TILE_M // 2, 2 * SUBLANES)

    def body(dg_ref, du_ref, wg_ref, wu_ref, x_ref, g_ref, r_ref, dx_ref, dgain_ref, acc_ref):
        i, s = pl.program_id(0), pl.program_id(1)
        r = _dotf(dg_ref[...], wg_ref[...], "nt") + _dotf(du_ref[...], wu_ref[...], "nt")

        @pl.when((i == 0) & (s == 0))
        def _():
            dgain_ref[...] = jnp.zeros(dgain_ref.shape, F32)

        @pl.when(s == 0)
        def _():
            acc_ref[...] = r

        @pl.when(s > 0)
        def _():
            acc_ref[...] += r

        @pl.when(s == C - 1)
        def _():
            dx, dgain = _vjp(_rms, (x_ref[...], g_ref[...]), acc_ref[...])
            dx_ref[...] = r_ref[...] + dx
            dgain_ref[...] += dgain

    hid = pl.BlockSpec((None, tm, Fc), lambda i, s: (s, i, 0))
    row = pl.BlockSpec((tm, D), lambda i, s: (i, 0))
    vec = pl.BlockSpec((1, D), lambda i, s: (0, 0))
    return pl.pallas_call(
        body, grid=(S // tm, C),
        in_specs=[hid, hid, pl.BlockSpec((None, None, D, Fc), lambda i, s: (s, ig, 0, 0)),
                  pl.BlockSpec((None, None, D, Fc), lambda i, s: (s, iu, 0, 0)), row, vec, row],
        out_specs=[row, vec], out_shape=[jax.ShapeDtypeStruct((S, D), F32), jax.ShapeDtypeStruct((1, D), F32)],
        scratch_shapes=[pltpu.VMEM((tm, D), F32)], name=name, compiler_params=_params(("arbitrary", "arbitrary")),
    )(dg, du, wgu, wgu, x, gain, dx_res)


def ffn_dw(a, d, buf, idx, shape, blocked, name):
    C, P, M, N = shape
    S = d.shape[-2]
    tk = _tile(S, MM_TILE_M, 2 * SUBLANES)
    nk = S // tk

    def body(*refs):
        a_ref, d_ref, o_ref, acc_ref = refs[0], refs[1], refs[-2], refs[-1]
        k = pl.program_id(1)
        r = _dotf(a_ref[...], d_ref[...], "tn")

        @pl.when(k == 0)
        def _():
            acc_ref[...] = r

        @pl.when(k > 0)
        def _():
            acc_ref[...] += r

        @pl.when(k == nk - 1)
        def _():
            o_ref[...] = acc_ref[...].astype(o_ref.dtype)

    if blocked == "a":
        a_spec = pl.BlockSpec((None, tk, M), lambda s, k: (s, k, 0))
        d_spec = pl.BlockSpec((tk, N), lambda s, k: (k, 0))
    else:
        a_spec = pl.BlockSpec((tk, M), lambda s, k: (k, 0))
        d_spec = pl.BlockSpec((None, tk, N), lambda s, k: (s, k, 0))
    out_spec = pl.BlockSpec((None, None, M, N), lambda s, k: (s, idx, 0, 0))
    out_shape = jax.ShapeDtypeStruct(tuple(shape), BF16)
    scratch = [pltpu.VMEM((M, N), F32)]
    if buf is None:
        return pl.pallas_call(body, grid=(C, nk), in_specs=[a_spec, d_spec], out_specs=out_spec, out_shape=out_shape,
                              scratch_shapes=scratch, name=name,
                              compiler_params=_params(("parallel", "arbitrary")))(a, d)
    return pl.pallas_call(body, grid=(C, nk), in_specs=[a_spec, d_spec, pl.BlockSpec(memory_space=pl.ANY)],
                          out_specs=out_spec, out_shape=out_shape, input_output_aliases={2: 0},
                          scratch_shapes=scratch, name=name,
                          compiler_params=_params(("parallel", "arbitrary")))(a, d, buf)


def ffn_fwd(n, x, gain, wgu, wd, ig, iu, iw, tag):
    g, u, act = ffn_up(n, wgu, ig, iu, f"ffn_up_{tag}")
    x_new, n_new = ffn_down(act, wd, iw, x, gain, f"ffn_down_{tag}")
    return x_new, n_new, (n, g, u, act)


def ffn_bwd(dy, saved, x, gain, wgu, wd, ig, iu, iw, gbuf, tag):
    n, g, u, act = saved
    gwgu, gwd = gbuf
    dg, du = ffn_down_bwd(dy, wd, iw, g, u, f"ffn_down_bwd_{tag}")
    gwd = ffn_dw(act, dy, gwd, iw, (N_CHIPS,) + wd.shape[1:], "a", f"ffn_dwd_{tag}")
    dx, dgain = ffn_dn(dg, du, wgu, ig, iu, x, gain, dy, f"ffn_dn_{tag}")
    gwgu = ffn_dw(n, dg, gwgu, ig, (N_CHIPS,) + wgu.shape[1:], "d", f"ffn_dwg_{tag}")
    gwgu = ffn_dw(n, du, gwgu, iu, (N_CHIPS,) + wgu.shape[1:], "d", f"ffn_dwu_{tag}")
    return dx, dgain, (gwgu, gwd)


def _heads(x, H):
    S = x.shape[0]
    return x.reshape(S, H, HEAD_DIM).transpose(1, 0, 2)


def _unheads(x):
    H, S, _ = x.shape
    return x.transpose(1, 0, 2).reshape(S, H * HEAD_DIM)


def _heads_t(x, H):
    return x.T.reshape(H, HEAD_DIM, x.shape[0])


def _unheads_t(x):
    return x.reshape(x.shape[0] * x.shape[1], x.shape[2]).T


def _shift_down(x, n=1):
    return jnp.pad(x, ((n, 0), (0, 0)))[:x.shape[0]]


def _shift_up(x, n=1):
    return jnp.pad(x, ((0, n), (0, 0)))[n:]


def _block_diag(w):
    B, I, J = w.shape
    eye = jnp.eye(B, dtype=w.dtype)
    return (w[:, :, None, :] * eye[:, None, :, None]).reshape(B * I, B * J)


def _block_diag_take(x, B):
    I, J = x.shape[0] // B, x.shape[1] // B
    eye = jnp.eye(B, dtype=x.dtype)
    return jnp.sum(x.reshape(B, I, B, J) * eye[:, None, :, None], axis=2)


def _rope_tables(S):
    half = HEAD_DIM // 2
    inv = jnp.power(ROPE_THETA, -jnp.arange(half, dtype=F32) / half)
    ang = jnp.arange(S, dtype=F32)[:, None] * inv[None, :]
    cos = jnp.concatenate([jnp.cos(ang), jnp.cos(ang)], axis=1)
    sin = jnp.concatenate([jnp.sin(ang), jnp.sin(ang)], axis=1)
    r = jnp.arange(HEAD_DIM)[:, None]
    c = jnp.arange(HEAD_DIM)[None, :]
    rot = jnp.where(r == c + half, -1.0, 0.0) + jnp.where(c == r + half, 1.0, 0.0)
    return cos, sin, rot.astype(F32)


def qk_prep_fwd(x_hm, g, rope_tabs, name):
    H, S, Dh = x_hm.shape
    rows = [x_hm.reshape(H * S, Dh)]
    consts = [g.reshape(1, Dh)]
    periods = [None]
    if rope_tabs is not None:
        rows += [rope_tabs[0], rope_tabs[1]]
        consts += [rope_tabs[2]]
        periods += [S, S]
    fn = _qk_prep(rope_tabs is not None)
    y = rowwise(lambda *a: ([fn(*a)], []), rows, consts, outs=[(Dh, F32)], name=name, periods=periods)[0]
    return y.reshape(H, S, Dh)


def qk_prep_bwd(x_hm, g, rope_tabs, dy_hm, name):
    H, S, Dh = x_hm.shape
    rope = rope_tabs is not None
    rows = [x_hm.reshape(H * S, Dh), dy_hm.reshape(H * S, Dh)]
    consts = [g.reshape(1, Dh)]
    periods = [None, None]
    if rope:
        rows += [rope_tabs[0], rope_tabs[1]]
        consts += [rope_tabs[2]]
        periods += [S, S]
    fn = _qk_prep(rope)

    def f(xv, dyv, *rest):
        if rope:
            cos, sin, gv, rot = rest
            dx, dg = _vjp(lambda a, b: fn(a, cos, sin, b, rot), (xv, gv), dyv)
        else:
            (gv,) = rest
            dx, dg = _vjp(fn, (xv, gv), dyv)
        return [dx], [dg]

    dx, dg = rowwise(f, rows, consts, outs=[(Dh, F32)], accs=[(1, Dh)], name=name, periods=periods)
    return dx.reshape(H, S, Dh), dg.reshape(Dh)


def attn_delta(do_t, o_t, name):
    H, Dh, S = o_t.shape

    def body(a_ref, b_ref, o_ref):
        o_ref[...] = jnp.sum(a_ref[...] * b_ref[...], axis=0, keepdims=True)

    spec = pl.BlockSpec((None, Dh, S), lambda h: (h, 0, 0))
    return pl.pallas_call(
        body, grid=(H,), in_specs=[spec, spec], out_specs=pl.BlockSpec((None, 1, S), lambda h: (h, 0, 0)),
        out_shape=jax.ShapeDtypeStruct((H, 1, S), F32), name=name, compiler_params=_params(("parallel",)),
    )(do_t, o_t)


def even_mixer_fwd(h, w, tag):
    S = h.shape[0]
    W = 512
    H = 8
    z = mm(h, w["w_in"], name=f"ev_in_{tag}")
    xa, ya, q, k, v, f = (z[:, 0:512], z[:, 512:1024], z[:, 1024:1536], z[:, 1536:2048], z[:, 2048:2560],
                          z[:, 2560:2688])
    xs = [_shift_down(xa, LRU_CONV - 1 - tap) for tap in range(LRU_CONV)]
    taps = [w["conv_w"][tap][None] for tap in range(LRU_CONV)]
    xc = rowwise(lambda *a: ([_lru_conv(*a)], []), xs, taps + [w["conv_b"]], outs=[(W, F32)],
                 name=f"lru_conv_{tag}")[0]
    pre = mm(xc, w["w_ax"], name=f"lru_gates_mm_{tag}")
    a, b = rowwise(lambda p_, x_, ba, bx, lam: (list(_lru_gates(p_, x_, ba, bx, lam)), []), [pre, xc],
                   [w["ba"], w["bx"], w["lam"]], outs=[(W, F32), (W, F32)], name=f"lru_gates_{tag}")
    hs = scan_real(a, b, name=f"lru_scan_{tag}")
    a_out = rowwise(lambda y_, h_: ([_gelu(y_) * h_], []), [ya, hs], outs=[(W, F32)], name=f"lru_out_{tag}")[0]
    lf = rowwise(lambda f_, bf: ([_log_sigmoid(f_ + bf)], []), [f], [w["bf"]], outs=[(LANES, F32)],
                 name=f"fox_logf_{tag}")[0]
    c = scan_real(jnp.ones_like(lf), lf, name=f"fox_cumsum_{tag}")
    c_hm = c[:, :H].T
    q_hm, k_hm, v_hm = _heads(q, H), _heads(k, H), _heads(v, H)
    qn = qk_prep_fwd(q_hm, w["qn"], None, f"fox_qprep_{tag}")
    kn = qk_prep_fwd(k_hm, w["kn"], None, f"fox_kprep_{tag}")
    sink = jnp.full((H, 1, 1), NEG, F32)
    o_hm, lse = attn_fwd(qn, kn, v_hm, sink, c_hm[:, None, :], c_hm[:, :, None], name=f"fox_attn_{tag}")
    mo = jnp.concatenate([a_out, _unheads_t(o_hm)], axis=1).astype(BF16)
    y = mm(mo, w["w_out"], name=f"ev_out_{tag}")
    saved = dict(h=h, xs=xs, xc=xc, pre=pre, a=a, hs=hs, ya=ya, f=f, c_hm=c_hm, q_hm=q_hm, k_hm=k_hm,
                 v_hm=v_hm, qn=qn, kn=kn, o_hm=o_hm, lse=lse, mo=mo)
    return y, saved


def even_mixer_bwd(dy, sv, w, tag):
    W = 512
    H = 8
    S = dy.shape[0]
    g = {}
    dmo = mm(dy, w["w_out"], "nt", name=f"ev_dmo_{tag}")
    g["w_out"] = mm(sv["mo"], dy, "tn", name=f"ev_dwout_{tag}")
    da_out, do = dmo[:, :W], dmo[:, W:]
    do_hm = _heads_t(do, H)
    delta = attn_delta(do_hm, sv["o_hm"], f"fox_delta_{tag}")
    c_hm = sv["c_hm"]
    dqn, dkn, dv_hm, dcq, dck = attn_bwd(sv["qn"], sv["kn"], sv["v_hm"], sv["lse"], do_hm, delta,
                                          c_hm[:, None, :], c_hm[:, :, None], name=f"fox_attn_bwd_{tag}")
    dq_hm, g["qn"] = qk_prep_bwd(sv["q_hm"], w["qn"], None, dqn, f"fox_qprep_bwd_{tag}")
    dk_hm, g["kn"] = qk_prep_bwd(sv["k_hm"], w["kn"], None, dkn, f"fox_kprep_bwd_{tag}")
    dc = (dcq[:, 0, :] + dck[:, :, 0]).T
    dc = jnp.pad(dc, ((0, 0), (0, LANES - H)))
    dlf = scan_real(jnp.ones_like(dc), dc, reverse=True, name=f"fox_cumsum_bwd_{tag}")

    def f_logf(f_, d_, bf):
        df, dbf = _vjp(lambda a_, b_: _log_sigmoid(a_ + b_), (f_, bf), d_)
        return [df], [dbf]

    df, dbf = rowwise(f_logf, [sv["f"], dlf], [w["bf"]], outs=[(LANES, F32)], accs=[(1, LANES)],
                      name=f"fox_logf_bwd_{tag}")
    g["bf"] = dbf[0, :H]
    def f_out(y_, h_, d_):
        dyv, dhv = _vjp(lambda a_, b_: _gelu(a_) * b_, (y_, h_), d_)
        return [dyv, dhv], []

    dya, dhs = rowwise(f_out, [sv["ya"], sv["hs"], da_out], outs=[(W, F32), (W, F32)], name=f"lru_out_bwd_{tag}")
    gs = scan_real(_shift_up(sv["a"]), dhs, reverse=True, name=f"lru_scan_bwd_{tag}")

    def f_gates(p_, x_, g_, hp_, ba, bx, lam):
        dp, dx, dba, dbx, dlam = _vjp(_lru_gates, (p_, x_, ba, bx, lam), (g_ * hp_, g_))
        return [dp, dx], [dba, dbx, dlam]

    dpre, dxc, dba, dbx, dlam = rowwise(f_gates, [sv["pre"], sv["xc"], gs, _shift_down(sv["hs"])],
                                        [w["ba"], w["bx"], w["lam"]], outs=[(2 * W, BF16), (W, F32)],
                                        accs=[(1, W)] * 3, name=f"lru_gates_bwd_{tag}")
    g["ba"], g["bx"], g["lam"] = dba[0], dbx[0], dlam[0]
    dxc2 = mm(dpre, w["w_ax"], "nt", name=f"lru_gates_mm_dx_{tag}")
    g["w_ax"] = mm(sv["xc"], dpre, "tn", name=f"lru_gates_mm_dw_{tag}")

    def f_conv(d1, d2, x0, x1, x2, x3):
        d = d1 + d2
        return [d], [jnp.sum(d, axis=0, keepdims=True)] + [jnp.sum(d * xv, axis=0, keepdims=True)
                                                           for xv in (x0, x1, x2, x3)]

    dxc_t, dcb, dw0, dw1, dw2, dw3 = rowwise(f_conv, [dxc, dxc2] + sv["xs"], outs=[(W, F32)],
                                             accs=[(1, W)] * 5, name=f"lru_conv_bwd_{tag}")
    g["conv_b"] = dcb[0]
    g["conv_w"] = jnp.concatenate([dw0, dw1, dw2, dw3], axis=0)
    ds_ = [_shift_up(dxc_t, LRU_CONV - 1 - tap) for tap in range(LRU_CONV)]
    taps = [w["conv_w"][tap][None] for tap in range(LRU_CONV)]
    dxa = rowwise(lambda a, b, c, d, w0, w1, w2, w3: ([a * w0 + b * w1 + c * w2 + d * w3], []), ds_, taps,
                  outs=[(W, F32)], name=f"lru_conv_dx_{tag}")[0]
    dz = jnp.concatenate([dxa, dya, _unheads(dq_hm), _unheads(dk_hm), _unheads(dv_hm), df], axis=1).astype(BF16)
    g["w_in"] = mm(sv["h"], dz, "tn", name=f"ev_dwin_{tag}")
    dh = mm(dz, w["w_in"], "nt", name=f"ev_dh_{tag}")
    return dh, g


def odd_mixer_fwd(h, w, tag):
    S = h.shape[0]
    H, KVH = 8, 2
    z = mm(h, w["w_in"], name=f"od_in_{tag}")
    q, k, v, u = z[:, 0:512], z[:, 512:640], z[:, 640:768], z[:, 768:1280]
    tabs = _rope_tables(S)
    q_hm, k_hm, v_hm = _heads(q, H), _heads(k, KVH), _heads(v, KVH)
    qn = qk_prep_fwd(q_hm, w["qn"], tabs, f"swa_qprep_{tag}")
    kn = qk_prep_fwd(k_hm, w["kn"], tabs, f"swa_kprep_{tag}")
    sink = w["sinks"].reshape(H, 1, 1)
    o_hm, lse = attn_fwd(qn, kn, v_hm, sink, window=SWA_WINDOW, name=f"swa_attn_{tag}")
    lam, bexp = w["s5_lam"], w["s5_bexp"]
    bu = mm(u, bexp, name=f"s5_bu_{tag}")
    hs = scan_cplx(lam, bu, name=f"s5_scan_{tag}")
    yssm = mm(hs, w["s5_cexp"], name=f"s5_y_{tag}")
    zz = rowwise(lambda y_, u_, d_: ([_s5_out(y_, u_, d_)], []), [yssm, u], [w["s5_d"]], outs=[(512, F32)],
                 name=f"s5_gelu_{tag}")[0]
    gl = mm(zz, w["glu_w"], name=f"s5_glu_mm_{tag}")
    d_out = rowwise(lambda z_, g_, b_: ([_glu(z_, g_, b_)], []), [zz, gl], [w["glu_b"]], outs=[(512, F32)],
                    name=f"s5_glu_{tag}")[0]
    mo = jnp.concatenate([_unheads_t(o_hm), d_out], axis=1).astype(BF16)
    y = mm(mo, w["w_out"], name=f"od_out_{tag}")
    saved = dict(h=h, q_hm=q_hm, k_hm=k_hm, v_hm=v_hm, qn=qn, kn=kn, o_hm=o_hm, lse=lse, u=u, hs=hs, yssm=yssm,
                 zz=zz, gl=gl, mo=mo, tabs=tabs)
    return y, saved


def odd_mixer_bwd(dy, sv, w, tag):
    H, KVH = 8, 2
    g = {}
    dmo = mm(dy, w["w_out"], "nt", name=f"od_dmo_{tag}")
    g["w_out"] = mm(sv["mo"], dy, "tn", name=f"od_dwout_{tag}")
    do, dd = dmo[:, :512], dmo[:, 512:]
    do_hm = _heads_t(do, H)
    delta = attn_delta(do_hm, sv["o_hm"], f"swa_delta_{tag}")
    dqn, dkn, dv_hm = attn_bwd(sv["qn"], sv["kn"], sv["v_hm"], sv["lse"], do_hm, delta, window=SWA_WINDOW,
                               name=f"swa_attn_bwd_{tag}")
    dq_hm, g["qn"] = qk_prep_bwd(sv["q_hm"], w["qn"], sv["tabs"], dqn, f"swa_qprep_bwd_{tag}")
    dk_hm, g["kn"] = qk_prep_bwd(sv["k_hm"], w["kn"], sv["tabs"], dkn, f"swa_kprep_bwd_{tag}")
    lse_t, delta_t = sv["lse"][:, 0, :].T, delta[:, 0, :].T
    g["sinks"] = rowwise(lambda l_, d_, s_: ([], [jnp.sum(-jnp.exp(s_ - l_) * d_, axis=0, keepdims=True)]),
                         [lse_t, delta_t], [w["sinks"].reshape(1, H)], accs=[(1, H)], name=f"swa_dsink_{tag}")[0][0]
    def f_glu(z_, g_, d_, b_):
        dz_, dg_, db_ = _vjp(_glu, (z_, g_, b_), d_)
        return [dz_, dg_], [db_]

    dzz1, dgl, dglb = rowwise(f_glu, [sv["zz"], sv["gl"], dd], [w["glu_b"]], outs=[(512, F32), (512, BF16)],
                              accs=[(1, 512)], name=f"s5_glu_bwd_{tag}")
    g["glu_b"] = dglb[0]
    g["glu_w"] = mm(sv["zz"], dgl, "tn", name=f"s5_glu_dw_{tag}")
    dzz2 = mm(dgl, w["glu_w"], "nt", name=f"s5_glu_dz_{tag}")

    def f_gelu(y_, u_, d1, d2, dpar):
        dy_, du_, dd_ = _vjp(_s5_out, (y_, u_, dpar), d1 + d2)
        return [dy_, du_], [dd_]

    dyssm, du1, dsd = rowwise(f_gelu, [sv["yssm"], sv["u"], dzz1, dzz2], [w["s5_d"]],
                              outs=[(512, F32), (512, F32)], accs=[(1, 512)], name=f"s5_gelu_bwd_{tag}")
    g["s5_d"] = dsd[0]
    dhs = mm(dyssm, w["s5_cexp"], "nt", name=f"s5_dh_{tag}")
    g["s5_cexp"] = mm(sv["hs"], dyssm, "tn", name=f"s5_dc_{tag}")
    gs = scan_cplx(w["s5_lam"], dhs, reverse=True, name=f"s5_scan_bwd_{tag}")
    g["s5_bexp"] = mm(sv["u"], gs, "tn", name=f"s5_db_{tag}")
    du2 = mm(gs, w["s5_bexp"], "nt", name=f"s5_du_{tag}")

    def f_dlam(g_, hp_):
        C = g_.shape[1]
        outs_r, outs_i = [], []
        for j in range(C // (2 * LANES)):
            gr, gi = g_[:, 2 * LANES * j:2 * LANES * j + LANES], g_[:, 2 * LANES * j + LANES:2 * LANES * (j + 1)]
            hr, hi = hp_[:, 2 * LANES * j:2 * LANES * j + LANES], hp_[:, 2 * LANES * j + LANES:2 * LANES * (j + 1)]
            outs_r.append(jnp.sum(gr * hr + gi * hi, axis=0, keepdims=True))
            outs_i.append(jnp.sum(gi * hr - gr * hi, axis=0, keepdims=True))
        return [], [jnp.concatenate([x for pair in zip(outs_r, outs_i) for x in pair], axis=1)]

    g["s5_lam"] = rowwise(f_dlam, [gs, _shift_down(sv["hs"])], accs=[(1, gs.shape[1])], name=f"s5_dlam_{tag}")[0]
    du = rowwise(lambda a_, b_: ([a_ + b_], []), [du1, du2], outs=[(512, F32)], name=f"s5_du_add_{tag}")[0]
    dz = jnp.concatenate([_unheads(dq_hm), _unheads(dk_hm), _unheads(dv_hm), du], axis=1).astype(BF16)
    g["w_in"] = mm(sv["h"], dz, "tn", name=f"od_dwin_{tag}")
    dh = mm(dz, w["w_in"], "nt", name=f"od_dh_{tag}")
    return dh, g


def _s5_cols(x_re, x_im):
    n = x_re.shape[0] // LANES
    return jnp.stack([x_re.reshape(n, LANES), x_im.reshape(n, LANES)], axis=1).reshape(1, 2 * n * LANES)


def _s5_uncols(x):
    n = x.shape[1] // (2 * LANES)
    y = x.reshape(n, 2, LANES)
    return y[:, 0].reshape(-1), y[:, 1].reshape(-1)


def _s5_gsel():
    return jnp.repeat(jnp.eye(S5_GROUPS, dtype=F32), S5_STATE, axis=0)


def s5_prep_fwd(lre, lim, ldt, bre, bim, cre, cim, tag):
    GP = S5_GROUPS * S5_STATE
    ins = [lre.reshape(GP, 1), lim.reshape(GP, 1), ldt.reshape(1, S5_GROUPS), _s5_gsel(),
           bre.reshape(GP, S5_GROUP), bim.reshape(GP, S5_GROUP)]
    lbr, lbi, bbr, bbi = whole(_s5_params, ins, [((GP, 1), F32)] * 2 + [((GP, S5_GROUP), F32)] * 2,
                               name=f"s5_params_{tag}")
    lam = _s5_cols(lbr[:, 0], lbi[:, 0])

    def expand_b(bb):
        return _block_diag(bb.reshape(S5_GROUPS, S5_STATE, S5_GROUP).transpose(0, 2, 1))

    n = GP // LANES
    bexp = jnp.stack([expand_b(bbr).reshape(-1, n, LANES), expand_b(bbi).reshape(-1, n, LANES)],
                     axis=2).reshape(-1, 2 * GP)
    c_r = _block_diag(cre.transpose(0, 2, 1))
    c_i = _block_diag(cim.transpose(0, 2, 1))
    cexp = jnp.stack([c_r.reshape(n, LANES, -1), -c_i.reshape(n, LANES, -1)], axis=1).reshape(2 * GP, -1)
    return lam, bexp.astype(BF16), cexp.astype(BF16), ins


def s5_prep_bwd(ins, dlam, dbexp, dcexp, tag):
    GP = S5_GROUPS * S5_STATE
    n = GP // LANES
    dlr, dli = _s5_uncols(dlam)
    db = dbexp.reshape(-1, n, 2, LANES)

    def take_b(x):
        return _block_diag_take(x, S5_GROUPS).transpose(0, 2, 1).reshape(GP, S5_GROUP)

    dbbr, dbbi = take_b(db[:, :, 0].reshape(-1, GP)), take_b(db[:, :, 1].reshape(-1, GP))
    dc = dcexp.reshape(n, 2, LANES, -1)
    dcre = _block_diag_take(dc[:, 0].reshape(GP, -1), S5_GROUPS).transpose(0, 2, 1)
    dcim = -_block_diag_take(dc[:, 1].reshape(GP, -1), S5_GROUPS).transpose(0, 2, 1)

    def f(lre, lim, ldt, gsel, bre, bim, c1, c2, c3, c4):
        d = _vjp(lambda a, b, c, e, f_: _s5_params(a, b, c, gsel, e, f_), (lre, lim, ldt, bre, bim), (c1, c2, c3, c4))
        return d

    outs = [((GP, 1), F32)] * 2 + [((1, S5_GROUPS), F32)] + [((GP, S5_GROUP), F32)] * 2
    dlre, dlim, dldt, dbre, dbim = whole(f, ins + [dlr.reshape(GP, 1), dli.reshape(GP, 1), dbbr, dbbi], outs,
                                          name=f"s5_params_bwd_{tag}")
    shp = (S5_GROUPS, S5_STATE)
    return dict(lre=dlre.reshape(shp), lim=dlim.reshape(shp), ldt=dldt.reshape(S5_GROUPS),
                bre=dbre.reshape(S5_GROUPS, S5_STATE, S5_GROUP), bim=dbim.reshape(S5_GROUPS, S5_STATE, S5_GROUP),
                cre=dcre, cim=dcim)


def _place():
    return lax.axis_index("x"), lax.axis_index("y"), lax.axis_index("c")


def _other_chips(x, y):
    return [(1 - x, y), (x, 1 - y), (1 - x, 1 - y)]


def _half(ref, h):
    n = ref.shape[0] // 2
    return ref.at[pl.ds(h * n, n)]


def _hbm_specs(n):
    return [pl.BlockSpec(memory_space=pl.ANY)] * n


def gather_chips(ws):
    n = len(ws)

    def body(*refs):
        w_refs, out_refs, (send_sems, recv_sems) = refs[:n], refs[n:2 * n], refs[2 * n:]
        x, y, c = _place()
        me, sibling = (x, y, c), (x, y, 1 - c)
        chips = _other_chips(x, y)
        mine = 2 * x + y

        def copy(k, src, dst, to):
            return pltpu.make_async_remote_copy(src_ref=src, dst_ref=dst, send_sem=send_sems.at[k],
                                                recv_sem=recv_sems.at[k], device_id=to, device_id_type=MESH)

        first, passed = [], []
        for p in range(n):
            for j, chip in enumerate(chips):
                first.append(copy(6 * p + j, _half(w_refs[p], c), _half(out_refs[p].at[mine], c), (*chip, c)))
                first[-1].start()
        for p in range(n):
            for j, chip in enumerate(chips):
                block = out_refs[p].at[2 * chip[0] + chip[1]]
                copy(6 * p + j, _half(w_refs[p], c), _half(block, c), me).wait_recv()
                passed.append(copy(6 * p + 3 + j, _half(block, c), _half(block, c), sibling))
                passed[-1].start()
        for p in range(n):
            for j, chip in enumerate(chips):
                block = out_refs[p].at[2 * chip[0] + chip[1]]
                copy(6 * p + 3 + j, _half(w_refs[p], c), _half(block, 1 - c), me).wait_recv()
        for cp in first + passed:
            cp.wait_send()

    return pl.pallas_call(
        body, out_shape=[jax.ShapeDtypeStruct((N_CHIPS,) + w.shape, w.dtype) for w in ws],
        in_specs=_hbm_specs(n), out_specs=_hbm_specs(n),
        scratch_shapes=[pltpu.SemaphoreType.DMA((6 * n,)), pltpu.SemaphoreType.DMA((6 * n,))],
        name="gather_chips",
    )(*ws)


def sibling_halves(gs):
    n = len(gs)

    def body(*refs):
        g_refs, out_refs, (send_sems, recv_sems) = refs[:n], refs[n:2 * n], refs[2 * n:]
        x, y, c = _place()
        me, sibling = (x, y, c), (x, y, 1 - c)

        def copy(p, k, to):
            return pltpu.make_async_remote_copy(src_ref=_half(g_refs[p].at[k], 1 - c), dst_ref=out_refs[p].at[k],
                                                send_sem=send_sems.at[N_CHIPS * p + k],
                                                recv_sem=recv_sems.at[N_CHIPS * p + k],
                                                device_id=to, device_id_type=MESH)

        cps = [copy(p, k, sibling) for p in range(n) for k in range(N_CHIPS)]
        for cp in cps:
            cp.start()
        for p in range(n):
            for k in range(N_CHIPS):
                copy(p, k, me).wait_recv()
        for cp in cps:
            cp.wait_send()

    return pl.pallas_call(
        body, out_shape=[jax.ShapeDtypeStruct((N_CHIPS, g.shape[1] // 2) + g.shape[2:], g.dtype) for g in gs],
        in_specs=_hbm_specs(n), out_specs=_hbm_specs(n),
        scratch_shapes=[pltpu.SemaphoreType.DMA((N_CHIPS * n,)), pltpu.SemaphoreType.DMA((N_CHIPS * n,))],
        name="sibling_halves",
    )(*gs)


def exchange_chips(ps):
    n = len(ps)

    def body(*refs):
        p_refs, out_refs, (send_sems, recv_sems) = refs[:n], refs[n:2 * n], refs[2 * n:]
        x, y, c = _place()
        me = (x, y, c)
        chips = _other_chips(x, y)

        def copy(p, j, chip, to):
            return pltpu.make_async_remote_copy(src_ref=p_refs[p].at[2 * chip[0] + chip[1]], dst_ref=out_refs[p].at[j],
                                                send_sem=send_sems.at[3 * p + j], recv_sem=recv_sems.at[3 * p + j],
                                                device_id=to, device_id_type=MESH)

        cps = [copy(p, j, chip, (*chip, c)) for p in range(n) for j, chip in enumerate(chips)]
        for cp in cps:
            cp.start()
        for p in range(n):
            for j, chip in enumerate(chips):
                copy(p, j, chip, me).wait_recv()
        for cp in cps:
            cp.wait_send()

    return pl.pallas_call(
        body, out_shape=[jax.ShapeDtypeStruct((3,) + p_.shape[1:], p_.dtype) for p_ in ps],
        in_specs=_hbm_specs(n), out_specs=_hbm_specs(n),
        scratch_shapes=[pltpu.SemaphoreType.DMA((3 * n,)), pltpu.SemaphoreType.DMA((3 * n,))],
        name="exchange_chips",
    )(*ps)


def sibling_join(rs):
    n = len(rs)

    def body(*refs):
        r_refs, out_refs, (send_sems, recv_sems) = refs[:n], refs[n:2 * n], refs[2 * n:]
        x, y, c = _place()

        def copy(p, h, to):
            return pltpu.make_async_remote_copy(src_ref=r_refs[p], dst_ref=_half(out_refs[p], h),
                                                send_sem=send_sems.at[p], recv_sem=recv_sems.at[p],
                                                device_id=to, device_id_type=MESH)

        cps = [copy(p, c, (x, y, 1 - c)) for p in range(n)]
        for cp in cps:
            cp.start()
        for p in range(n):
            copy(p, 1 - c, (x, y, c)).wait_recv()
        for cp in cps:
            cp.wait_send()

    return pl.pallas_call(
        body, out_shape=[jax.ShapeDtypeStruct((2 * r.shape[0],) + r.shape[1:], r.dtype) for r in rs],
        in_specs=_hbm_specs(n), out_specs=_hbm_specs(n),
        scratch_shapes=[pltpu.SemaphoreType.DMA((n,)), pltpu.SemaphoreType.DMA((n,))],
        name="sibling_join",
    )(*rs)


def gather_devices(v, name):
    R = v.shape[0]

    def body(v_ref, out_ref, send_sems, recv_sems, local_sem):
        x, y, c = _place()
        me, sibling = (x, y, c), (x, y, 1 - c)
        chips = _other_chips(x, y)

        def rows(px, py, pc):
            return out_ref.at[pl.ds((4 * px + 2 * py + pc) * R, R), :]

        def copy(k, block, to, src=None):
            return pltpu.make_async_remote_copy(src_ref=rows(*block) if src is None else src, dst_ref=rows(*block),
                                                send_sem=send_sems.at[k], recv_sem=recv_sems.at[k],
                                                device_id=to, device_id_type=MESH)

        mine = pltpu.make_async_copy(v_ref, rows(*me), local_sem)
        mine.start()
        first = [copy(0, me, sibling, src=v_ref)]
        first += [copy(1 + j, me, (*chip, c), src=v_ref) for j, chip in enumerate(chips)]
        for cp in first:
            cp.start()
        passed = [copy(4 + j, (*chip, c), sibling) for j, chip in enumerate(chips)]
        for j, chip in enumerate(chips):
            copy(1 + j, (*chip, c), me).wait_recv()
            passed[j].start()
        copy(0, sibling, me).wait_recv()
        for j, chip in enumerate(chips):
            copy(4 + j, (*chip, 1 - c), me).wait_recv()
        for cp in first + passed:
            cp.wait_send()
        mine.wait()

    return pl.pallas_call(
        body, out_shape=jax.ShapeDtypeStruct((N_DEV * R, LANES), v.dtype),
        in_specs=[pl.BlockSpec(memory_space=pltpu.VMEM)], out_specs=pl.BlockSpec(memory_space=pltpu.VMEM),
        scratch_shapes=[pltpu.SemaphoreType.DMA((7,)), pltpu.SemaphoreType.DMA((7,)), pltpu.SemaphoreType.DMA],
        name=name, compiler_params=_params(),
    )(v)


def _flat_rows(n, mult):
    return -(-n // (LANES * mult)) * mult


def _adam(w, g, m, v):
    m = ADAM_B1 * m + (1.0 - ADAM_B1) * g
    v = ADAM_B2 * v + (1.0 - ADAM_B2) * (g * g)
    m_hat = m / (1.0 - ADAM_B1 ** ADAM_STEP)
    v_hat = v / (1.0 - ADAM_B2 ** ADAM_STEP)
    return -ADAM_LR * (m_hat / (jnp.sqrt(v_hat) + ADAM_EPS) + ADAM_WD * w), m, v


def adam_2d(w, g, m, v, name):
    shape = w.shape
    F = shape[-1]
    if w.ndim == 3 and shape[1] % (2 * SUBLANES) == 0:
        L, R, _ = shape
        tr = R
        for t in (512, 256, 128, 64, 32, 16):
            if R % t == 0 and 7 * t * max(F, LANES) * 4 <= ROW_TILE_BYTES:
                tr = t
                break

        def body(w_ref, g_ref, m_ref, v_ref, d_ref, m2_ref, v2_ref):
            d_ref[...], m2_ref[...], v2_ref[...] = _adam(w_ref[...], g_ref[...], m_ref[...], v_ref[...])

        spec = pl.BlockSpec((None, tr, F), lambda l, i: (l, i, 0))
        return pl.pallas_call(
            body, grid=(L, R // tr), in_specs=[spec] * 4, out_specs=[spec] * 3,
            out_shape=[jax.ShapeDtypeStruct(shape, F32)] * 3, name=name, compiler_params=_params(("parallel", "parallel")),
        )(w, g, m, v)
    a = [t.reshape(-1, F) for t in (w, g, m, v)]
    d, m2, v2 = rowwise(lambda w_, g_, m_, v_: (list(_adam(w_, g_, m_, v_)), []), a, outs=[(F, F32)] * 3, name=name)
    return d.reshape(shape), m2.reshape(shape), v2.reshape(shape)


WEIGHTS = ['ffn1_norm', 'ffn1_wg', 'ffn1_wu', 'ffn1_wd', 'mix_norm', 'ffn2_norm', 'ffn2_wg', 'ffn2_wu', 'ffn2_wd',
           'ple_w', 'ple_norm', 'ple_gate_norm', 'ple_gate_w', 'ev_w_in', 'lru_conv_w', 'lru_conv_b', 'lru_wa',
           'lru_ba', 'lru_wx', 'lru_bx', 'lru_lambda', 'fox_bf', 'fox_q_norm', 'fox_k_norm', 'ev_w_out', 'od_w_in',
           'swa_q_norm', 'swa_k_norm', 'swa_sinks', 's5_lambda_re', 's5_lambda_im', 's5_log_dt', 's5_b_re',
           's5_b_im', 's5_c_re', 's5_c_im', 's5_d', 's5_glu_w', 's5_glu_b', 'od_w_out']
SHARD_AXIS = {'ffn1_wg': 2, 'ffn1_wu': 2, 'ffn1_wd': 1, 'ffn2_wg': 2, 'ffn2_wu': 2, 'ffn2_wd': 1, 'ple_w': 2,
              'ple_gate_w': 1, 'ev_w_in': 2, 'lru_conv_w': 2, 'ev_w_out': 1, 'od_w_in': 2, 's5_d': 1,
              's5_glu_w': 1, 's5_glu_b': 1, 'od_w_out': 1}
EXACT_SHARDED = ('lru_conv_w', 's5_d', 's5_glu_b')
SHARDED = [n for n in WEIGHTS if n in SHARD_AXIS]
REPLICATED = [n for n in WEIGHTS if n not in SHARD_AXIS]


GROUPS = {
    'wgu': ['ffn1_wg', 'ffn1_wu', 'ffn2_wg', 'ffn2_wu'],
    'wd': ['ffn1_wd', 'ffn2_wd'],
    'w_rows': ['ple_gate_w', 'ev_w_out', 'od_w_out'],
    'ple_w': ['ple_w'], 'ev_w_in': ['ev_w_in'], 'od_w_in': ['od_w_in'], 's5_glu_w': ['s5_glu_w'],
}
REDUCED_GROUPS = list(GROUPS)


def _chip():
    return 2 * lax.axis_index("x") + lax.axis_index("y")


def gather_weights(shards):
    own = {k: jnp.concatenate([shards[n] for n in names], axis=0).astype(BF16) for k, names in GROUPS.items()}
    own['exact'] = jnp.concatenate([shards['lru_conv_w'], shards['s5_d'][:, None], shards['s5_glu_b'][:, None]], axis=1)
    keys = list(own)
    got = gather_chips([own[k] for k in keys])
    return {k: lax.dynamic_update_index_in_dim(g, own[k], _chip(), 0) for k, g in zip(keys, got)}


def _rows_by_chip(w):
    return w.reshape(w.shape[0] * w.shape[1], w.shape[2])


def _cols_by_chip(w):
    return w.transpose(1, 0, 2).reshape(w.shape[1], w.shape[0] * w.shape[2])


def _chip_rows(g):
    return g.reshape(N_CHIPS, g.shape[0] // N_CHIPS, g.shape[1])


def _chip_cols(g):
    return g.reshape(g.shape[0], N_CHIPS, g.shape[1] // N_CHIPS).transpose(1, 0, 2)


def full_weights(gw, depth):
    n_ev = (depth + 1) // 2
    ex = gw['exact']
    return dict(
        ple_gate_w=[_rows_by_chip(gw['w_rows'][:, l]) for l in range(depth)],
        ev_w_out=[_rows_by_chip(gw['w_rows'][:, depth + j]) for j in range(n_ev)],
        od_w_out=[_rows_by_chip(gw['w_rows'][:, depth + n_ev + j]) for j in range(depth // 2)],
        ple_w=[_cols_by_chip(gw['ple_w'][:, l]) for l in range(depth)],
        ev_w_in=[_cols_by_chip(gw['ev_w_in'][:, j]) for j in range(n_ev)],
        od_w_in=[_cols_by_chip(gw['od_w_in'][:, j]) for j in range(depth // 2)],
        s5_glu_w=[_rows_by_chip(gw['s5_glu_w'][:, j]) for j in range(depth // 2)],
        lru_conv_w=[_cols_by_chip(ex[:, j, 0:LRU_CONV]) for j in range(n_ev)],
        s5_d=[ex[:, j, LRU_CONV].reshape(-1) for j in range(depth // 2)],
        s5_glu_b=[ex[:, j, LRU_CONV + 1].reshape(-1) for j in range(depth // 2)],
    )


def _add_tile(rows, width):
    for t in (1024, 512, 256, 128, 64, 32, 16):
        if rows % t == 0 and 3 * t * width * 4 <= ROW_TILE_BYTES:
            return t
    return rows


def pair_add(g, t, c, name):
    C, F = g.shape[0], g.shape[-1]
    rows = math.prod(t.shape[1:-1])
    tr = _add_tile(rows, F)
    nb = rows // tr

    def body(c_ref, g_ref, t_ref, o_ref):
        o_ref[...] = (g_ref[...].astype(F32) + t_ref[...].astype(F32)).astype(o_ref.dtype)

    spec = pl.BlockSpec((None, tr, F), lambda k, i, c_ref: (k, i, 0))
    out = pl.pallas_call(
        body, out_shape=jax.ShapeDtypeStruct((C, rows, F), BF16),
        grid_spec=pltpu.PrefetchScalarGridSpec(
            num_scalar_prefetch=1, grid=(C, nb),
            in_specs=[pl.BlockSpec((None, tr, F), lambda k, i, c_ref: (k, c_ref[0] * nb + i, 0)), spec],
            out_specs=spec),
        name=name, compiler_params=_params(("parallel", "parallel")),
    )(c.reshape(1).astype(jnp.int32), g.reshape(C, 2 * rows, F), t.reshape(C, rows, F))
    return out.reshape(t.shape)


def chips_add(p, xs, chip, name):
    F = p.shape[-1]
    rows = math.prod(p.shape[1:-1])
    tr = _add_tile(rows, F)

    def body(m_ref, p_ref, a_ref, b_ref, d_ref, o_ref):
        o_ref[...] = ((p_ref[...].astype(F32) + a_ref[...].astype(F32))
                      + (b_ref[...].astype(F32) + d_ref[...].astype(F32)))

    def other(j):
        return pl.BlockSpec((None, tr, F), lambda i, m_ref: (j, i, 0))

    x3 = xs.reshape(3, rows, F)
    out = pl.pallas_call(
        body, out_shape=jax.ShapeDtypeStruct((rows, F), F32),
        grid_spec=pltpu.PrefetchScalarGridSpec(
            num_scalar_prefetch=1, grid=(rows // tr,),
            in_specs=[pl.BlockSpec((None, tr, F), lambda i, m_ref: (m_ref[0], i, 0)), other(0), other(1), other(2)],
            out_specs=pl.BlockSpec((tr, F), lambda i, m_ref: (i, 0))),
        name=name, compiler_params=_params(("parallel",)),
    )(chip.reshape(1).astype(jnp.int32), p.reshape(N_CHIPS, rows, F), x3, x3, x3)
    return out.reshape(p.shape[1:])


def reduce_sharded(groups):
    keys = list(groups)
    c = lax.axis_index("c")
    gs = [groups[k] for k in keys]
    theirs = sibling_halves(gs)
    pairs = [pair_add(g, t, c, f"pair_add_{k}") for k, g, t in zip(keys, gs, theirs)]
    got = exchange_chips(pairs)
    halves = [chips_add(p_, x_, _chip(), f"chips_add_{k}") for k, p_, x_ in zip(keys, pairs, got)]
    joined = sibling_join(halves)
    out = {}
    for k, h, j in zip(keys, halves, joined):
        out[k] = lax.dynamic_update_slice_in_dim(j, h, c * h.shape[0], axis=0)
    return out


SMALL_GRADS = REPLICATED + list(EXACT_SHARDED)


def _flatten_small(tensors, shapes):
    parts = [tensors[n].astype(F32).reshape(-1) if n in tensors else jnp.zeros((math.prod(shapes[n]),), F32)
             for n in SMALL_GRADS]
    flat = jnp.concatenate(parts)
    rows = _flat_rows(flat.shape[0], SUBLANES)
    return jnp.pad(flat, (0, rows * LANES - flat.shape[0])).reshape(rows, LANES)


def _unflatten_small(flat, shapes):
    flat = flat.reshape(-1)
    out, off = {}, 0
    for n in SMALL_GRADS:
        size = math.prod(shapes[n])
        out[n] = flat[off:off + size].reshape(shapes[n])
        off += size
    return out


def grad_groups(gwgu, gwd, G):
    def st(xs):
        return jnp.stack(xs, axis=1).astype(BF16)

    return {
        'wgu': gwgu, 'wd': gwd,
        'w_rows': st([_chip_rows(g) for n in GROUPS['w_rows'] for g in G[n]]),
        'ple_w': st([_chip_cols(g) for g in G['ple_w']]),
        'ev_w_in': st([_chip_cols(g) for g in G['ev_w_in']]),
        'od_w_in': st([_chip_cols(g) for g in G['od_w_in']]),
        's5_glu_w': st([_chip_rows(g) for g in G['s5_glu_w']]),
    }


def ungroup(red, shapes):
    out = {}
    for k, names in GROUPS.items():
        off = 0
        for n in names:
            out[n] = red[k][off:off + shapes[n][0]]
            off += shapes[n][0]
    return out


def _layer_weights(full, small, i, depth):
    j = i // 2
    w = dict(
        g1=small['ffn1_norm'][i][None], gm=small['mix_norm'][i][None], g2=small['ffn2_norm'][i][None],
        gp=small['ple_norm'][i][None], gg=small['ple_gate_norm'][i][None],
        ffn1=(i, depth + i, i), ffn2=(2 * depth + i, 3 * depth + i, depth + i),
        ple_w=full['ple_w'][i], ple_gate_w=full['ple_gate_w'][i],
    )
    if i % 2 == 0:
        w_in = full['ev_w_in'][j]
        w['mix'] = dict(
            w_in=jnp.pad(w_in, ((0, 0), (0, 2688 - w_in.shape[1]))), w_out=full['ev_w_out'][j],
            conv_w=full['lru_conv_w'][j].astype(F32), conv_b=small['lru_conv_b'][j][None],
            w_ax=jnp.concatenate([_block_diag(small['lru_wa'][j]), _block_diag(small['lru_wx'][j])],
                                 axis=1).astype(BF16),
            ba=small['lru_ba'][j][None], bx=small['lru_bx'][j][None], lam=small['lru_lambda'][j][None],
            bf=jnp.pad(small['fox_bf'][j], (0, LANES - 8))[None], qn=small['fox_q_norm'][j],
            kn=small['fox_k_norm'][j])
    else:
        lam, bexp, cexp, ins = s5_prep_fwd(small['s5_lambda_re'][j], small['s5_lambda_im'][j], small['s5_log_dt'][j],
                                           small['s5_b_re'][j], small['s5_b_im'][j], small['s5_c_re'][j],
                                           small['s5_c_im'][j], f"L{i}")
        w['mix'] = dict(
            w_in=full['od_w_in'][j], w_out=full['od_w_out'][j], qn=small['swa_q_norm'][j], kn=small['swa_k_norm'][j],
            sinks=small['swa_sinks'][j], s5_lam=lam, s5_bexp=bexp, s5_cexp=cexp, s5_ins=ins,
            s5_d=full['s5_d'][j].astype(F32)[None], glu_w=full['s5_glu_w'][j], glu_b=full['s5_glu_b'][j].astype(F32)[None])
    return w


def layer_fwd(x0, n1, p_i, w, ffnw, next_g1, i):
    tag = f"L{i}"
    sv = {}
    wgu, wd = ffnw
    x1, hm, sv['ffn1'] = ffn_fwd(n1, x0, w['gm'], wgu, wd, *w['ffn1'], f"1_{tag}")
    if i % 2 == 0:
        ym, sv['mix'] = even_mixer_fwd(hm, w['mix'], tag)
    else:
        ym, sv['mix'] = odd_mixer_fwd(hm, w['mix'], tag)
    x2, n2 = add_norm(x1, ym, w['g2'], f"norm2_{tag}")
    x3, ng, sv['ffn2'] = ffn_fwd(n2, x2, w['gg'], wgu, wd, *w['ffn2'], f"2_{tag}")
    gpre = mm(ng, w['ple_gate_w'], name=f"ple_gate_{tag}")
    epre = mm(p_i, w['ple_w'], name=f"ple_emb_{tag}")
    D = x0.shape[1]
    if next_g1 is None:
        x4 = rowwise(lambda a, b, c, pn: ([_ple_out(a, b, c, pn)], []), [x3, gpre, epre], [w['gp']],
                     outs=[(D, F32)], name=f"ple_out_{tag}")[0]
        n_next = None
    else:
        def f(a, b, c, pn, gn):
            y = _ple_out(a, b, c, pn)
            return [y, _rms(y, gn)], []

        x4, n_next = rowwise(f, [x3, gpre, epre], [w['gp'], next_g1], outs=[(D, F32), (D, BF16)],
                             name=f"ple_out_{tag}")
    sv.update(x0=x0, x1=x1, x2=x2, x3=x3, ng=ng, gpre=gpre, epre=epre, p=p_i)
    return x4, n_next, sv


def layer_bwd(dx4, sv, w, ffnw, gbuf, i):
    tag = f"L{i}"
    D = dx4.shape[1]
    g = {}
    wgu, wd = ffnw

    def f_ple(a, b, c, d, pn):
        da, db, dc, dpn = _vjp(_ple_out, (a, b, c, pn), d)
        return [db, dc], [dpn]

    dgpre, depre, dgp = rowwise(f_ple, [sv['x3'], sv['gpre'], sv['epre'], dx4], [w['gp']],
                                outs=[(D, BF16), (D, BF16)], accs=[(1, D)], name=f"ple_out_bwd_{tag}")
    g['gp'] = dgp[0]
    g['ple_w'] = mm(sv['p'], depre, "tn", name=f"ple_emb_dw_{tag}")
    g['ple_gate_w'] = mm(sv['ng'], dgpre, "tn", name=f"ple_gate_dw_{tag}")
    dng = mm(dgpre, w['ple_gate_w'], "nt", name=f"ple_gate_dx_{tag}")
    dx3, dgg = norm_bwd(sv['x3'], w['gg'], dng, dx4, f"normg_bwd_{tag}")
    g['gg'] = dgg[0]
    dx2, dg2, gbuf = ffn_bwd(dx3, sv['ffn2'], sv['x2'], w['g2'], wgu, wd, *w['ffn2'], gbuf, f"2_{tag}")
    g['g2'] = dg2[0]
    if i % 2 == 0:
        dhm, g['mix'] = even_mixer_bwd(dx2, sv['mix'], w['mix'], tag)
    else:
        dhm, g['mix'] = odd_mixer_bwd(dx2, sv['mix'], w['mix'], tag)
    dx1, dgm = norm_bwd(sv['x1'], w['gm'], dhm, dx2, f"normm_bwd_{tag}")
    g['gm'] = dgm[0]
    dx0, dg1, gbuf = ffn_bwd(dx1, sv['ffn1'], sv['x0'], w['g1'], wgu, wd, *w['ffn1'], gbuf, f"1_{tag}")
    g['g1'] = dg1[0]
    return dx0, g, gbuf


def _collect_grads(layer_grads, depth):
    st = lambda xs: jnp.stack(xs)
    G = {}
    L = layer_grads
    G['ffn1_norm'] = st([g['g1'] for g in L])
    G['mix_norm'] = st([g['gm'] for g in L])
    G['ffn2_norm'] = st([g['g2'] for g in L])
    G['ple_norm'] = st([g['gp'] for g in L])
    G['ple_gate_norm'] = st([g['gg'] for g in L])
    G['ple_w'] = st([g['ple_w'] for g in L])
    G['ple_gate_w'] = st([g['ple_gate_w'] for g in L])
    ev = [L[i]['mix'] for i in range(0, depth, 2)]
    od = [L[i]['mix'] for i in range(1, depth, 2)]
    G['ev_w_in'] = st([m['w_in'][:, :2568] for m in ev])
    G['ev_w_out'] = st([m['w_out'] for m in ev])
    G['lru_conv_w'] = st([m['conv_w'] for m in ev])
    G['lru_conv_b'] = st([m['conv_b'] for m in ev])
    G['lru_wa'] = st([_block_diag_take(m['w_ax'][:, :512], LRU_BLOCKS) for m in ev])
    G['lru_wx'] = st([_block_diag_take(m['w_ax'][:, 512:], LRU_BLOCKS) for m in ev])
    G['lru_ba'] = st([m['ba'] for m in ev])
    G['lru_bx'] = st([m['bx'] for m in ev])
    G['lru_lambda'] = st([m['lam'] for m in ev])
    G['fox_bf'] = st([m['bf'] for m in ev])
    G['fox_q_norm'] = st([m['qn'] for m in ev])
    G['fox_k_norm'] = st([m['kn'] for m in ev])
    G['od_w_in'] = st([m['w_in'] for m in od])
    G['od_w_out'] = st([m['w_out'] for m in od])
    G['swa_q_norm'] = st([m['qn'] for m in od])
    G['swa_k_norm'] = st([m['kn'] for m in od])
    G['swa_sinks'] = st([m['sinks'] for m in od])
    G['s5_lambda_re'] = st([m['s5']['lre'] for m in od])
    G['s5_lambda_im'] = st([m['s5']['lim'] for m in od])
    G['s5_log_dt'] = st([m['s5']['ldt'] for m in od])
    G['s5_b_re'] = st([m['s5']['bre'] for m in od])
    G['s5_b_im'] = st([m['s5']['bim'] for m in od])
    G['s5_c_re'] = st([m['s5']['cre'] for m in od])
    G['s5_c_im'] = st([m['s5']['cim'] for m in od])
    G['s5_d'] = st([m['s5_d'] for m in od])
    G['s5_glu_w'] = st([m['glu_w'] for m in od])
    G['s5_glu_b'] = st([m['glu_b'] for m in od])
    return G


def local_step(x, p, target, ffnw, full, small):
    depth = p.shape[0]
    S, D = x.shape
    ws = [_layer_weights(full, small, i, depth) for i in range(depth)]
    saved = []
    xi, ni = add_norm(x, None, ws[0]['g1'], "norm1_L0")
    for i in range(depth):
        xi, ni, sv = layer_fwd(xi, ni, p[i], ws[i], ffnw, ws[i + 1]['g1'] if i + 1 < depth else None, i)
        saved.append(sv)

    def f_loss(y, t):
        e = y - t
        return [e * (1.0 / D)], [0.5 * jnp.sum(jnp.mean(e * e, axis=-1, keepdims=True), axis=0, keepdims=True)]

    dx, loss = rowwise(f_loss, [xi, target], outs=[(D, F32)], accs=[(1, 1)], name="loss")
    grads = [None] * depth
    gbuf = (None, None)
    for i in reversed(range(depth)):
        dx, grads[i], gbuf = layer_bwd(dx, saved[i], ws[i], ffnw, gbuf, i)
        if i % 2 == 1:
            m = grads[i]['mix']
            m['s5'] = s5_prep_bwd(ws[i]['mix']['s5_ins'], m['s5_lam'], m['s5_bexp'], m['s5_cexp'], f"L{i}")
    return loss[0, 0], dx, gbuf, _collect_grads(grads, depth)


def kernel(x, p, ffn1_norm, ffn1_wg, ffn1_wu, ffn1_wd, mix_norm, ffn2_norm, ffn2_wg, ffn2_wu, ffn2_wd, ple_w, ple_norm, ple_gate_norm, ple_gate_w, ev_w_in, lru_conv_w, lru_conv_b, lru_wa, lru_ba, lru_wx, lru_bx, lru_lambda, fox_bf, fox_q_norm, fox_k_norm, ev_w_out, od_w_in, swa_q_norm, swa_k_norm, swa_sinks, s5_lambda_re, s5_lambda_im, s5_log_dt, s5_b_re, s5_b_im, s5_c_re, s5_c_im, s5_d, s5_glu_w, s5_glu_b, od_w_out, loss_target, m_ffn1_norm, m_ffn1_wg, m_ffn1_wu, m_ffn1_wd, m_mix_norm, m_ffn2_norm, m_ffn2_wg, m_ffn2_wu, m_ffn2_wd, m_ple_w, m_ple_norm, m_ple_gate_norm, m_ple_gate_w, m_ev_w_in, m_lru_conv_w, m_lru_conv_b, m_lru_wa, m_lru_ba, m_lru_wx, m_lru_bx, m_lru_lambda, m_fox_bf, m_fox_q_norm, m_fox_k_norm, m_ev_w_out, m_od_w_in, m_swa_q_norm, m_swa_k_norm, m_swa_sinks, m_s5_lambda_re, m_s5_lambda_im, m_s5_log_dt, m_s5_b_re, m_s5_b_im, m_s5_c_re, m_s5_c_im, m_s5_d, m_s5_glu_w, m_s5_glu_b, m_od_w_out, v_ffn1_norm, v_ffn1_wg, v_ffn1_wu, v_ffn1_wd, v_mix_norm, v_ffn2_norm, v_ffn2_wg, v_ffn2_wu, v_ffn2_wd, v_ple_w, v_ple_norm, v_ple_gate_norm, v_ple_gate_w, v_ev_w_in, v_lru_conv_w, v_lru_conv_b, v_lru_wa, v_lru_ba, v_lru_wx, v_lru_bx, v_lru_lambda, v_fox_bf, v_fox_q_norm, v_fox_k_norm, v_ev_w_out, v_od_w_in, v_swa_q_norm, v_swa_k_norm, v_swa_sinks, v_s5_lambda_re, v_s5_lambda_im, v_s5_log_dt, v_s5_b_re, v_s5_b_im, v_s5_c_re, v_s5_c_im, v_s5_d, v_s5_glu_w, v_s5_glu_b, v_od_w_out):
    args = locals()
    wts = {n: args[n] for n in WEIGHTS}
    ms = {n: args["m_" + n] for n in WEIGHTS}
    vs = {n: args["v_" + n] for n in WEIGHTS}
    shapes = {n: wts[n].shape for n in WEIGHTS}

    depth = p.shape[0]
    gw = gather_weights({n: wts[n] for n in SHARDED})
    small = {n: wts[n] for n in REPLICATED}
    loss, dx, (gwgu, gwd), G = local_step(x[0], p[:, 0], loss_target[0], (gw['wgu'], gw['wd']),
                                          full_weights(gw, depth), small)
    loss = lax.psum(loss, ("x", "y", "c"))

    gsh = ungroup(reduce_sharded(grad_groups(gwgu, gwd, G)), shapes)
    full_shapes = {n: (G[n].shape if n in EXACT_SHARDED else shapes[n]) for n in SMALL_GRADS}
    flat_g = _flatten_small(G, full_shapes)
    g8 = gather_devices(flat_g, "gather_small_grads").reshape((N_DEV,) + flat_g.shape)
    wf, mf, vf = (_flatten_small({n: t[n] for n in REPLICATED}, full_shapes) for t in (wts, ms, vs))

    def f_small(g0, g1, g2, g3, g4, g5, g6, g7, w_, m_, v_):
        gsum = ((g0 + g1) + (g2 + g3)) + ((g4 + g5) + (g6 + g7))
        return [gsum] + list(_adam(w_, gsum, m_, v_)), []

    gs_f, ds_f, ms_f, vs_f = rowwise(f_small, [g8[d] for d in range(N_DEV)] + [wf, mf, vf],
                                     outs=[(LANES, F32)] * 4, name="adam_small")
    out_g, out_d, out_m, out_v = {}, {}, {}, {}
    for dst, flat in ((out_g, gs_f), (out_d, ds_f), (out_m, ms_f), (out_v, vs_f)):
        dst.update(_unflatten_small(flat, full_shapes))
    for n in EXACT_SHARDED:
        width = shapes[n][SHARD_AXIS[n]]
        gsh[n] = lax.dynamic_slice_in_dim(out_g[n], _chip() * width, width, axis=SHARD_AXIS[n])
    for n in SHARDED:
        out_g[n] = gsh[n]
        out_d[n], out_m[n], out_v[n] = adam_2d(wts[n], gsh[n], ms[n], vs[n], f"adam_{n}")
    return (loss, dx[None], *[out_g[n] for n in WEIGHTS], *[out_d[n] for n in WEIGHTS],
            *[out_m[n] for n in WEIGHTS], *[out_v[n] for n in WEIGHTS])
```

```python
import functools
import math

import jax
import jax.numpy as jnp
from jax import lax
from jax.experimental import pallas as pl
from jax.experimental.pallas import tpu as pltpu

F32 = jnp.float32
BF16 = jnp.bfloat16
MXU_DTYPE = BF16
HI = lax.Precision.HIGHEST
MESH = pl.DeviceIdType.MESH

VMEM_LIMIT_BYTES = 56 * 1024 * 1024
ROW_TILE_BYTES = 5 * 1024 * 1024
MM_VMEM_BYTES = 40 * 1024 * 1024
MM_TILE_M = 1024
MM_TILE_N = 1408
FLAT_W = 2048
LANES = 128
SUBLANES = 8

HEAD_DIM = 64
LRU_BLOCKS = 8
LRU_CONV = 4
LRU_C = 8.0
SWA_WINDOW = 128
SWA_GROUP = 4
S5_GROUP = 16
S5_GROUPS = 32
S5_STATE = 64
ROPE_THETA = 10000.0
EPS = 1e-6
MACARON = 0.5
NEG = -1e30

ADAM_LR = 0.001
ADAM_B1 = 0.9
ADAM_B2 = 0.999
ADAM_EPS = 1e-08
ADAM_WD = 0.01
ADAM_STEP = 10

N_CHIPS = 4
N_DEV = 8


def _pick(n, cands):
    for c in cands:
        if n % c == 0:
            return c
    return n


def _tile(n, cap, unit):
    best = None
    for t in range(unit, min(n, cap) + 1, unit):
        if n % t == 0:
            best = t
    return n if best is None else best


def _params(sem=None):
    return pltpu.CompilerParams(dimension_semantics=sem, vmem_limit_bytes=VMEM_LIMIT_BYTES)


def rowwise(fn, rows, consts=(), outs=(), accs=(), name="rowwise", periods=None):
    rows, consts = list(rows), list(consts)
    n_r, n_c, n_o, n_a = len(rows), len(consts), len(outs), len(accs)
    R = rows[0].shape[0]
    periods = list(periods) if periods is not None else [None] * n_r
    per_row = sum(max(r.shape[1], LANES) * 4 for r in rows) + sum(max(f, LANES) * 4 for f, _ in outs)
    limit = min([R] + [p for p in periods if p is not None])
    tr = limit
    for c in (1024, 512, 256, 128, 64, 32, 16):
        if c <= limit and limit % c == 0 and R % c == 0 and c * per_row <= ROW_TILE_BYTES:
            tr = c
            break

    def row_map(period):
        if period is None:
            return lambda i: (i, 0)
        nb = period // tr
        return lambda i: (i % nb, 0)

    in_specs = [pl.BlockSpec((tr, r.shape[1]), row_map(p)) for r, p in zip(rows, periods)]
    in_specs += [pl.BlockSpec(c.shape, lambda i: (0, 0)) for c in consts]
    out_shape = [jax.ShapeDtypeStruct((R, f), dt) for f, dt in outs]
    out_shape += [jax.ShapeDtypeStruct(tuple(s), F32) for s in accs]
    out_specs = [pl.BlockSpec((tr, f), lambda i: (i, 0)) for f, _ in outs]
    out_specs += [pl.BlockSpec(tuple(s), lambda i: (0, 0)) for s in accs]

    def body(*refs):
        ins = [r[...] for r in refs[:n_r + n_c]]
        o_refs = refs[n_r + n_c:n_r + n_c + n_o]
        a_refs = refs[n_r + n_c + n_o:]
        ro, ra = fn(*ins)
        for ref, val in zip(o_refs, ro):
            ref[...] = val.astype(ref.dtype)
        if n_a:
            @pl.when(pl.program_id(0) == 0)
            def _():
                for ref in a_refs:
                    ref[...] = jnp.zeros(ref.shape, ref.dtype)
            for ref, val in zip(a_refs, ra):
                ref[...] += val.astype(F32)

    res = pl.pallas_call(
        body, grid=(R // tr,), in_specs=in_specs, out_specs=out_specs, out_shape=out_shape,
        name=name, compiler_params=_params(("arbitrary",)),
    )(*rows, *consts)
    return list(res)


def whole(fn, ins, outs, name="whole"):
    n_i = len(ins)

    def body(*refs):
        vals = fn(*[r[...] for r in refs[:n_i]])
        for ref, val in zip(refs[n_i:], vals):
            ref[...] = val.astype(ref.dtype)

    res = pl.pallas_call(
        body, out_shape=[jax.ShapeDtypeStruct(tuple(s), dt) for s, dt in outs],
        in_specs=[pl.BlockSpec(memory_space=pltpu.VMEM)] * n_i,
        out_specs=[pl.BlockSpec(memory_space=pltpu.VMEM)] * len(outs),
        name=name, compiler_params=_params(),
    )(*ins)
    return list(res)


_DOT_DIMS = {
    "nn": (((1,), (0,)), ((), ())),
    "nt": (((1,), (1,)), ((), ())),
    "tn": (((0,), (0,)), ((), ())),
}


def mm(a, b, mode="nn", out_dtype=F32, name="mm"):
    if mode == "nn":
        (M, K), (K2, N) = a.shape, b.shape
    elif mode == "nt":
        (M, K), (N, K2) = a.shape, b.shape
    else:
        (K, M), (K2, N) = a.shape, b.shape
    assert K == K2, (mode, a.shape, b.shape)
    tn = _tile(N, MM_TILE_N, LANES)
    if mode == "tn":
        tm, tk = _tile(M, MM_TILE_M, LANES), _tile(K, MM_TILE_M, 2 * SUBLANES)
    else:
        tm, tk = _tile(M, MM_TILE_M, 2 * SUBLANES), _tile(K, MM_TILE_N, LANES)

    def vmem_bytes(tm_, tk_):
        return (2 * (tm_ * tk_ * a.dtype.itemsize + tk_ * tn * b.dtype.itemsize
                     + tm_ * tn * jnp.dtype(out_dtype).itemsize) + tm_ * tn * 4)

    while vmem_bytes(tm, tk) > MM_VMEM_BYTES and tk % (2 * LANES) == 0 and K % (tk // 2) == 0:
        tk //= 2
    while vmem_bytes(tm, tk) > MM_VMEM_BYTES and tm % (2 * LANES) == 0 and M % (tm // 2) == 0:
        tm //= 2
    if mode == "tn":
        a_spec = pl.BlockSpec((tk, tm), lambda i, j, k: (k, i))
    else:
        a_spec = pl.BlockSpec((tm, tk), lambda i, j, k: (i, k))
    if mode == "nt":
        b_spec = pl.BlockSpec((tn, tk), lambda i, j, k: (j, k))
    else:
        b_spec = pl.BlockSpec((tk, tn), lambda i, j, k: (k, j))
    nk = K // tk
    dims = _DOT_DIMS[mode]

    def dot(a_ref, b_ref):
        return lax.dot_general(a_ref[...].astype(MXU_DTYPE), b_ref[...].astype(MXU_DTYPE), dims,
                               preferred_element_type=F32)

    def body_one(a_ref, b_ref, o_ref):
        o_ref[...] = dot(a_ref, b_ref).astype(o_ref.dtype)

    def body_acc(a_ref, b_ref, o_ref, acc_ref):
        k = pl.program_id(2)

        @pl.when(k == 0)
        def _():
            acc_ref[...] = dot(a_ref, b_ref)

        @pl.when(k > 0)
        def _():
            acc_ref[...] += dot(a_ref, b_ref)

        @pl.when(k == nk - 1)
        def _():
            o_ref[...] = acc_ref[...].astype(o_ref.dtype)

    return pl.pallas_call(
        body_one if nk == 1 else body_acc, grid=(M // tm, N // tn, nk), in_specs=[a_spec, b_spec],
        out_specs=pl.BlockSpec((tm, tn), lambda i, j, k: (i, j)),
        out_shape=jax.ShapeDtypeStruct((M, N), out_dtype),
        scratch_shapes=[] if nk == 1 else [pltpu.VMEM((tm, tn), F32)],
        name=name, compiler_params=_params(("parallel", "parallel", "arbitrary")),
    )(a, b)


def _mm_rows_tiles(M, K):
    return _tile(M, MM_TILE_M // 2, 2 * SUBLANES), _tile(K, MM_TILE_N, LANES)


def mm_add_norm(a, b, x, gain, name):
    (M, K), N = a.shape, b.shape[1]
    tm, tk = _mm_rows_tiles(M, K)
    nk = K // tk

    def body(a_ref, b_ref, x_ref, g_ref, xo_ref, n_ref, acc_ref):
        k = pl.program_id(1)
        r = _dotf(a_ref[...], b_ref[...])

        @pl.when(k == 0)
        def _():
            acc_ref[...] = r

        @pl.when(k > 0)
        def _():
            acc_ref[...] += r

        @pl.when(k == nk - 1)
        def _():
            xn = x_ref[...] + acc_ref[...]
            xo_ref[...] = xn
            n_ref[...] = _rms(xn, g_ref[...]).astype(n_ref.dtype)

    row = pl.BlockSpec((tm, N), lambda i, k: (i, 0))
    return pl.pallas_call(
        body, grid=(M // tm, nk),
        in_specs=[pl.BlockSpec((tm, tk), lambda i, k: (i, k)), pl.BlockSpec((tk, N), lambda i, k: (k, 0)), row,
                  pl.BlockSpec((1, N), lambda i, k: (0, 0))],
        out_specs=[row, row], out_shape=[jax.ShapeDtypeStruct((M, N), F32), jax.ShapeDtypeStruct((M, N), BF16)],
        scratch_shapes=[pltpu.VMEM((tm, N), F32)], name=name, compiler_params=_params(("parallel", "arbitrary")),
    )(a, b, x, gain)


def mm_norm_bwd(a, b, x, gain, dx_res, name):
    (M, K), N = a.shape, b.shape[0]
    tm, tk = _mm_rows_tiles(M, K)
    nk = K // tk

    def body(a_ref, b_ref, x_ref, g_ref, r_ref, dx_ref, dgain_ref, acc_ref):
        i, k = pl.program_id(0), pl.program_id(1)
        r = _dotf(a_ref[...], b_ref[...], "nt")

        @pl.when((i == 0) & (k == 0))
        def _():
            dgain_ref[...] = jnp.zeros(dgain_ref.shape, F32)

        @pl.when(k == 0)
        def _():
            acc_ref[...] = r

        @pl.when(k > 0)
        def _():
            acc_ref[...] += r

        @pl.when(k == nk - 1)
        def _():
            dx, dgain = _vjp(_rms, (x_ref[...], g_ref[...]), acc_ref[...])
            dx_ref[...] = r_ref[...] + dx
            dgain_ref[...] += dgain

    row = pl.BlockSpec((tm, N), lambda i, k: (i, 0))
    vec = pl.BlockSpec((1, N), lambda i, k: (0, 0))
    return pl.pallas_call(
        body, grid=(M // tm, nk),
        in_specs=[pl.BlockSpec((tm, tk), lambda i, k: (i, k)), pl.BlockSpec((N, tk), lambda i, k: (0, k)), row, vec, row],
        out_specs=[row, vec], out_shape=[jax.ShapeDtypeStruct((M, N), F32), jax.ShapeDtypeStruct((1, N), F32)],
        scratch_shapes=[pltpu.VMEM((tm, N), F32)], name=name, compiler_params=_params(("arbitrary", "arbitrary")),
    )(a, b, x, gain, dx_res)


def _roll_rows(x, d, reverse):
    return pltpu.roll(x, (SUBLANES - d) if reverse else d, 0)


def scan_real(a, b, reverse=False, name="scan_real"):
    S, W = b.shape
    cw = _pick(W, (256, 128))
    n_tiles = S // SUBLANES

    def body(a_ref, b_ref, o_ref):
        row = lax.broadcasted_iota(jnp.int32, (SUBLANES, cw), 0)
        edge = 0 if reverse else SUBLANES - 1

        def step(i, carry):
            t = (n_tiles - 1 - i) if reverse else i
            off = pl.multiple_of(t * SUBLANES, SUBLANES)
            A = a_ref[pl.ds(off, SUBLANES), :]
            B = b_ref[pl.ds(off, SUBLANES), :]
            for d in (1, 2, 4):
                m = (row < SUBLANES - d) if reverse else (row >= d)
                B = jnp.where(m, A * _roll_rows(B, d, reverse) + B, B)
                A = jnp.where(m, A * _roll_rows(A, d, reverse), A)
            o_ref[pl.ds(off, SUBLANES), :] = B + A * carry
            at_edge = row == edge
            return (jnp.sum(jnp.where(at_edge, B, 0.0), axis=0, keepdims=True)
                    + jnp.sum(jnp.where(at_edge, A, 0.0), axis=0, keepdims=True) * carry)

        lax.fori_loop(0, n_tiles, step, jnp.zeros((1, cw), F32), unroll=2)

    spec = pl.BlockSpec((S, cw), lambda j: (0, j))
    return pl.pallas_call(
        body, grid=(W // cw,), in_specs=[spec, spec], out_specs=spec,
        out_shape=jax.ShapeDtypeStruct((S, W), F32), name=name, compiler_params=_params(("parallel",)),
    )(a, b)


def scan_cplx(lam, bu, reverse=False, name="scan_cplx"):
    S, C = bu.shape
    half = LANES
    CB = _pick(C, (1024, 512, 256))
    TS = _pick(S, (1024, 512, 256, 128, 64, 32, 16, 8))
    groups = CB // (2 * half)
    n_blocks, n_tiles = S // TS, TS // SUBLANES

    def cmul(ar, ai, br, bi):
        return ar * br - ai * bi, ar * bi + ai * br

    def body(lam_ref, bu_ref, o_ref, carry_ref):
        row = lax.broadcasted_iota(jnp.int32, (SUBLANES, half), 0)

        def edge_row(v):
            return jnp.sum(jnp.where(row == (0 if reverse else SUBLANES - 1), v, 0.0), axis=0, keepdims=True)

        @pl.when(pl.program_id(1) == 0)
        def _():
            carry_ref[...] = jnp.zeros(carry_ref.shape, F32)

        consts = []
        for g in range(groups):
            lr = lam_ref[:, 2 * half * g:2 * half * g + half]
            li = lam_ref[:, 2 * half * g + half:2 * half * (g + 1)]
            if reverse:
                li = -li
            l1 = (lr, li)
            l2 = cmul(*l1, *l1)
            l4 = cmul(*l2, *l2)
            pr = jnp.zeros((SUBLANES, half), F32)
            pi = jnp.zeros((SUBLANES, half), F32)
            p = l1
            for r in range(SUBLANES):
                sel = row == ((SUBLANES - 1 - r) if reverse else r)
                pr = jnp.where(sel, p[0], pr)
                pi = jnp.where(sel, p[1], pi)
                p = cmul(*p, *l1)
            consts.append((l1, l2, l4, pr, pi, edge_row(pr), edge_row(pi)))

        def step(i, carry):
            t = (n_tiles - 1 - i) if reverse else i
            off = pl.multiple_of(t * SUBLANES, SUBLANES)
            out = []
            for g in range(groups):
                l1, l2, l4, pr, pi, p8r, p8i = consts[g]
                cr, ci = carry[2 * g], carry[2 * g + 1]
                re, im = pl.ds(2 * half * g, half), pl.ds(2 * half * g + half, half)
                Br = bu_ref[pl.ds(off, SUBLANES), re]
                Bi = bu_ref[pl.ds(off, SUBLANES), im]
                for d, (qr, qi) in ((1, l1), (2, l2), (4, l4)):
                    m = (row < SUBLANES - d) if reverse else (row >= d)
                    sr, si = _roll_rows(Br, d, reverse), _roll_rows(Bi, d, reverse)
                    nr = jnp.where(m, Br + qr * sr - qi * si, Br)
                    ni = jnp.where(m, Bi + qr * si + qi * sr, Bi)
                    Br, Bi = nr, ni
                o_ref[pl.ds(off, SUBLANES), re] = Br + pr * cr - pi * ci
                o_ref[pl.ds(off, SUBLANES), im] = Bi + pr * ci + pi * cr
                er, ei = edge_row(Br), edge_row(Bi)
                out += [er + p8r * cr - p8i * ci, ei + p8r * ci + p8i * cr]
            return tuple(out)

        carry0 = tuple(carry_ref[:, pl.ds(half * k, half)] for k in range(2 * groups))
        carry1 = lax.fori_loop(0, n_tiles, step, carry0, unroll=2)
        for k in range(2 * groups):
            carry_ref[:, pl.ds(half * k, half)] = carry1[k]

    def rows(j, t):
        return ((n_blocks - 1 - t) if reverse else t, j)

    spec = pl.BlockSpec((TS, CB), rows)
    return pl.pallas_call(
        body, grid=(C // CB, n_blocks), in_specs=[pl.BlockSpec((1, CB), lambda j, t: (0, j)), spec],
        out_specs=spec, out_shape=jax.ShapeDtypeStruct((S, C), F32), scratch_shapes=[pltpu.VMEM((1, CB), F32)],
        name=name, compiler_params=_params(("parallel", "arbitrary")),
    )(lam, bu)


ATTN_HEADS_PER_STEP = 2


def _attn_tile(S, window):
    if window is None:
        return _pick(S, (512, 256, 128))
    return max(window, _pick(S, (256, 128)))


def _attn_valid(q_blk, k_blk, T, window):
    kpos = k_blk * T + lax.broadcasted_iota(jnp.int32, (T, T), 0)
    qpos = q_blk * T + lax.broadcasted_iota(jnp.int32, (T, T), 1)
    valid = kpos <= qpos
    if window is not None:
        valid = valid & (qpos - kpos < window)
    return valid


def attn_fwd(q, k, v, sink, cq=None, ck=None, window=None, name="attn_fwd"):
    H, S, Dh = q.shape
    G = H // k.shape[0]
    HP = ATTN_HEADS_PER_STEP
    assert H % HP == 0 and (G == 1 or G % HP == 0)
    KP = HP if G == 1 else 1
    T = _attn_tile(S, window)
    nq = S // T
    nks = nq if window is None else 2
    scale = Dh ** -0.5
    bias = cq is not None

    def kv_block(i, j):
        return jnp.minimum(j, i) if window is None else jnp.maximum(i - 1 + j, 0)

    def body(*refs):
        if bias:
            q_ref, k_ref, v_ref, s_ref, cq_ref, ck_ref, o_ref, lse_ref, m_scr, l_scr, acc_scr = refs
        else:
            q_ref, k_ref, v_ref, s_ref, o_ref, lse_ref, m_scr, l_scr, acc_scr = refs
        i, j = pl.program_id(1), pl.program_id(2)

        @pl.when(j == 0)
        def _():
            m_scr[...] = jnp.zeros(m_scr.shape, F32) + s_ref[...]
            l_scr[...] = jnp.ones(l_scr.shape, F32)
            acc_scr[...] = jnp.zeros(acc_scr.shape, F32)

        def block(masked):
            valid = _attn_valid(i, kv_block(i, j), T, window) if masked else None
            for b in range(HP):
                kvb = b if G == 1 else 0
                s = _dotf(k_ref[kvb], q_ref[b], "nt") * scale
                if bias:
                    s = s + cq_ref[b] - ck_ref[b]
                if masked:
                    s = jnp.where(valid, s, NEG)
                m_old = m_scr[b]
                m_new = jnp.maximum(m_old, jnp.max(s, axis=0, keepdims=True))
                alpha = jnp.exp(m_old - m_new)
                p = jnp.exp(s - m_new)
                l_scr[b] = alpha * l_scr[b] + jnp.sum(p, axis=0, keepdims=True)
                acc_scr[b] = alpha * acc_scr[b] + _dotf(v_ref[kvb], p, "tn")
                m_scr[b] = m_new

        if window is None:
            pl.when(j < i)(lambda: block(False))
            pl.when(j == i)(lambda: block(True))
        else:
            pl.when(i - 1 + j >= 0)(lambda: block(True))

        @pl.when(j == nks - 1)
        def _():
            o_ref[...] = acc_scr[...] / l_scr[...]
            lse_ref[...] = m_scr[...] + jnp.log(l_scr[...])

    def kv_map(hp, i, j):
        return (hp if G == 1 else (hp * HP) // G, kv_block(i, j), 0)

    in_specs = [
        pl.BlockSpec((HP, T, Dh), lambda hp, i, j: (hp, i, 0)),
        pl.BlockSpec((KP, T, Dh), kv_map),
        pl.BlockSpec((KP, T, Dh), kv_map),
        pl.BlockSpec((HP, 1, 1), lambda hp, i, j: (hp, 0, 0)),
    ]
    args = [q, k, v, sink]
    if bias:
        in_specs += [pl.BlockSpec((HP, 1, T), lambda hp, i, j: (hp, 0, i)),
                     pl.BlockSpec((HP, T, 1), lambda hp, i, j: (hp, kv_block(i, j), 0))]
        args += [cq, ck]
    return pl.pallas_call(
        body, grid=(H // HP, nq, nks), in_specs=in_specs,
        out_specs=[pl.BlockSpec((HP, Dh, T), lambda hp, i, j: (hp, 0, i)),
                   pl.BlockSpec((HP, 1, T), lambda hp, i, j: (hp, 0, i))],
        out_shape=[jax.ShapeDtypeStruct((H, Dh, S), F32), jax.ShapeDtypeStruct((H, 1, S), F32)],
        scratch_shapes=[pltpu.VMEM((HP, 1, T), F32), pltpu.VMEM((HP, 1, T), F32), pltpu.VMEM((HP, Dh, T), F32)],
        name=name, compiler_params=_params(("parallel", "parallel", "arbitrary")),
    )(*args)


def attn_bwd(q, k, v, lse, do, delta, cq=None, ck=None, window=None, name="attn_bwd"):
    H, S, Dh = q.shape
    KVH = k.shape[0]
    G = H // KVH
    HP = ATTN_HEADS_PER_STEP
    assert H % HP == 0 and (G == 1 or G % HP == 0)
    pair_kv = G == 1
    KP = HP if pair_kv else 1
    T = _attn_tile(S, window)
    nq = S // T
    nqs = nq if window is None else 2
    scale = Dh ** -0.5
    bias = cq is not None
    assert not bias or G == 1

    def q_block(kb, j):
        return jnp.maximum(j, kb) if window is None else jnp.minimum(kb + j, nq - 1)

    def body(*refs):
        if bias:
            (q_ref, k_ref, v_ref, lse_ref, do_ref, dl_ref, cq_ref, ck_ref,
             dq_ref, dk_ref, dv_ref, dcq_ref, dck_ref) = refs
        else:
            q_ref, k_ref, v_ref, lse_ref, do_ref, dl_ref, dq_ref, dk_ref, dv_ref = refs
        kb, gp, j = pl.program_id(1), pl.program_id(2), pl.program_id(3)

        @pl.when((gp == 0) & (j == 0))
        def _():
            dk_ref[...] = jnp.zeros(dk_ref.shape, F32)
            dv_ref[...] = jnp.zeros(dv_ref.shape, F32)
            if bias:
                dck_ref[...] = jnp.zeros(dck_ref.shape, F32)

        @pl.when((kb == 0) & (gp == 0) & (j == 0))
        def _():
            dq_ref[...] = jnp.zeros(dq_ref.shape, F32)
            if bias:
                dcq_ref[...] = jnp.zeros(dcq_ref.shape, F32)

        def block(masked):
            qi = q_block(kb, j)
            off = pl.multiple_of(qi * T, T)
            valid = _attn_valid(qi, kb, T, window) if masked else None
            for b in range(HP):
                kvb = b if pair_kv else 0
                g = 0 if pair_kv else gp * HP + b
                qb, kk, vv = q_ref[b].astype(MXU_DTYPE), k_ref[kvb].astype(MXU_DTYPE), v_ref[kvb].astype(MXU_DTYPE)
                dob = do_ref[b].astype(MXU_DTYPE)
                s = _dotf(kk, qb, "nt") * scale
                if bias:
                    s = s + cq_ref[b] - ck_ref[b]
                if masked:
                    s = jnp.where(valid, s, NEG)
                p = jnp.exp(s - lse_ref[b])
                dv_ref[kvb] += _dotf(p, dob, "nt")
                ds = p * (_dotf(vv, dob) - dl_ref[b])
                dsb = ds.astype(MXU_DTYPE)
                dk_ref[kvb] += scale * _dotf(dsb, qb)
                dq_ref[kvb, g, pl.ds(off, T), :] += scale * _dotf(dsb, kk, "tn")
                if bias:
                    dcq_ref[kvb, g, :, pl.ds(off, T)] += jnp.sum(ds, axis=0, keepdims=True)
                    dck_ref[kvb] -= jnp.sum(ds, axis=1, keepdims=True)

        if window is None:
            pl.when(j > kb)(lambda: block(False))
            pl.when(j == kb)(lambda: block(True))
        else:
            pl.when(kb + j <= nq - 1)(lambda: block(True))

    def qmap(kvp, kb, gp, j):
        return (kvp if pair_kv else (kvp * G) // HP + gp, q_block(kb, j), 0)

    def qmap_t(kvp, kb, gp, j):
        return (kvp if pair_kv else (kvp * G) // HP + gp, 0, q_block(kb, j))

    in_specs = [
        pl.BlockSpec((HP, T, Dh), qmap),
        pl.BlockSpec((KP, T, Dh), lambda kvp, kb, gp, j: (kvp, kb, 0)),
        pl.BlockSpec((KP, T, Dh), lambda kvp, kb, gp, j: (kvp, kb, 0)),
        pl.BlockSpec((HP, 1, T), qmap_t),
        pl.BlockSpec((HP, Dh, T), qmap_t),
        pl.BlockSpec((HP, 1, T), qmap_t),
    ]
    args = [q, k, v, lse, do, delta]
    out_specs = [
        pl.BlockSpec((KP, G, S, Dh), lambda kvp, kb, gp, j: (kvp, 0, 0, 0)),
        pl.BlockSpec((KP, T, Dh), lambda kvp, kb, gp, j: (kvp, kb, 0)),
        pl.BlockSpec((KP, T, Dh), lambda kvp, kb, gp, j: (kvp, kb, 0)),
    ]
    out_shape = [jax.ShapeDtypeStruct((KVH, G, S, Dh), F32), jax.ShapeDtypeStruct((KVH, S, Dh), F32),
                 jax.ShapeDtypeStruct((KVH, S, Dh), F32)]
    if bias:
        in_specs += [pl.BlockSpec((HP, 1, T), qmap_t),
                     pl.BlockSpec((HP, T, 1), lambda kvp, kb, gp, j: (kvp, kb, 0))]
        args += [cq, ck]
        out_specs += [pl.BlockSpec((KP, G, 1, S), lambda kvp, kb, gp, j: (kvp, 0, 0, 0)),
                      pl.BlockSpec((KP, T, 1), lambda kvp, kb, gp, j: (kvp, kb, 0))]
        out_shape += [jax.ShapeDtypeStruct((KVH, G, 1, S), F32), jax.ShapeDtypeStruct((KVH, S, 1), F32)]
    res = pl.pallas_call(
        body, grid=(KVH // KP, nq, 1 if pair_kv else G // HP, nqs), in_specs=in_specs, out_specs=out_specs,
        out_shape=out_shape, name=name, compiler_params=_params(("arbitrary", "arbitrary", "arbitrary", "arbitrary")),
    )(*args)
    dq = res[0].reshape(H, S, Dh)
    if bias:
        return dq, res[1], res[2], res[3].reshape(H, 1, S), res[4]
    return dq, res[1], res[2]


def _rms(x, g):
    return x * lax.rsqrt(jnp.mean(x * x, axis=-1, keepdims=True) + EPS) * g


def _sigmoid(x):
    return 1.0 / (1.0 + jnp.exp(-x))


def _softplus(x):
    return jnp.maximum(x, 0.0) + jnp.log(1.0 + jnp.exp(-jnp.abs(x)))


def _log_sigmoid(x):
    return jnp.minimum(x, 0.0) - jnp.log(1.0 + jnp.exp(-jnp.abs(x)))


def _gelu(x):
    return 0.5 * x * (1.0 + jnp.tanh(math.sqrt(2.0 / math.pi) * (x + 0.044715 * (x * x * x))))


def _silu(x):
    return x * _sigmoid(x)


def _ffn_act(gu):
    f = gu.shape[1] // 2
    return MACARON * _silu(gu[:, :f]) * gu[:, f:]


def _qk_prep(rope):
    def f(x, *rest):
        if rope:
            cos, sin, g, rot = rest
        else:
            (g,) = rest
        y = _rms(x, g)
        if rope:
            y = y * cos + jnp.dot(y, rot, precision=HI, preferred_element_type=F32) * sin
        return y
    return f


def _lru_gates(pre, xc, ba, bx, lam):
    w = xc.shape[1]
    r = _sigmoid(pre[:, :w] + ba)
    i = _sigmoid(pre[:, w:] + bx)
    log_a = -LRU_C * r * _softplus(lam)
    a = jnp.exp(log_a)
    b = jnp.sqrt(1.0 - jnp.exp(2.0 * log_a)) * (i * xc)
    return a, b


def _lru_conv(x0, x1, x2, x3, w0, w1, w2, w3, cb):
    return cb + x0 * w0 + x1 * w1 + x2 * w2 + x3 * w3


def _s5_params(lre, lim, ldt, gsel, bre, bim):
    dt = jnp.sum(gsel * jnp.exp(ldt), axis=1, keepdims=True)
    er = jnp.exp(lre * dt)
    ang = lim * dt
    lbr, lbi = er * jnp.cos(ang), er * jnp.sin(ang)
    nr, ni = lbr - 1.0, lbi
    den = lre * lre + lim * lim
    fr, fi = (nr * lre + ni * lim) / den, (ni * lre - nr * lim) / den
    return lbr, lbi, fr * bre - fi * bim, fr * bim + fi * bre


def _s5_out(yssm, u, d):
    return _gelu(yssm + d * u)


def _glu(z, gl, gb):
    return z * _sigmoid(gl + gb)


def _ple_out(x, gpre, epre, pn):
    return x + _sigmoid(gpre) * _rms(epre, pn)


def _vjp(fn, args, cots):
    _, pull = jax.vjp(fn, *args)
    return pull(cots)


def add_norm(x, y, g, name):
    D = x.shape[1]
    if y is None:
        return x, rowwise(lambda xv, gv: ([_rms(xv, gv)], []), [x], [g], outs=[(D, BF16)], name=name)[0]
    xn, n = rowwise(lambda xv, yv, gv: ([xv + yv, _rms(xv + yv, gv)], []), [x, y], [g],
                    outs=[(D, F32), (D, BF16)], name=name)
    return xn, n


def norm_bwd(x, g, dn, dx_res, name):
    D = x.shape[1]

    def f(xv, dnv, dxv, gv):
        dx, dg = _vjp(_rms, (xv, gv), dnv)
        return [dxv + dx], [dg]

    return rowwise(f, [x, dn, dx_res], [g], outs=[(D, F32)], accs=[(1, D)], name=name)


def _swiglu(g, u):
    return MACARON * _silu(g) * u


def _dotf(a, b, mode="nn"):
    return lax.dot_general(a.astype(MXU_DTYPE), b.astype(MXU_DTYPE), _DOT_DIMS[mode], preferred_element_type=F32)


def ffn_up(n, wgu, ig, iu, name):
    S, D = n.shape
    C, _, _, Fc = wgu.shape
    tm = _tile(S, MM_TILE_M, 2 * SUBLANES)

    def body(n_ref, wg_ref, wu_ref, g_ref, u_ref, a_ref):
        g = _dotf(n_ref[...], wg_ref[...])
        u = _dotf(n_ref[...], wu_ref[...])
        g_ref[...] = g.astype(g_ref.dtype)
        u_ref[...] = u.astype(u_ref.dtype)
        a_ref[...] = _swiglu(g, u).astype(a_ref.dtype)

    hid = pl.BlockSpec((None, tm, Fc), lambda s, i: (s, i, 0))
    return pl.pallas_call(
        body, grid=(C, S // tm),
        in_specs=[pl.BlockSpec((tm, D), lambda s, i: (i, 0)),
                  pl.BlockSpec((None, None, D, Fc), lambda s, i: (s, ig, 0, 0)),
                  pl.BlockSpec((None, None, D, Fc), lambda s, i: (s, iu, 0, 0))],
        out_specs=[hid, hid, hid], out_shape=[jax.ShapeDtypeStruct((C, S, Fc), BF16)] * 3,
        name=name, compiler_params=_params(("parallel", "parallel")),
    )(n, wgu, wgu)


def ffn_down(act, wd, iw, x, gain, name):
    C, S, Fc = act.shape
    D = wd.shape[-1]
    tm = _tile(S, MM_TILE_M, 2 * SUBLANES)

    def body(a_ref, w_ref, x_ref, g_ref, xo_ref, n_ref, acc_ref):
        s = pl.program_id(1)
        r = _dotf(a_ref[...], w_ref[...])

        @pl.when(s == 0)
        def _():
            acc_ref[...] = r

        @pl.when(s > 0)
        def _():
            acc_ref[...] += r

        @pl.when(s == C - 1)
        def _():
            xn = x_ref[...] + acc_ref[...]
            xo_ref[...] = xn
            n_ref[...] = _rms(xn, g_ref[...]).astype(n_ref.dtype)

    row = pl.BlockSpec((tm, D), lambda i, s: (i, 0))
    return pl.pallas_call(
        body, grid=(S // tm, C),
        in_specs=[pl.BlockSpec((None, tm, Fc), lambda i, s: (s, i, 0)),
                  pl.BlockSpec((None, None, Fc, D), lambda i, s: (s, iw, 0, 0)), row,
                  pl.BlockSpec((1, D), lambda i, s: (0, 0))],
        out_specs=[row, row], out_shape=[jax.ShapeDtypeStruct((S, D), F32), jax.ShapeDtypeStruct((S, D), BF16)],
        scratch_shapes=[pltpu.VMEM((tm, D), F32)], name=name, compiler_params=_params(("parallel", "arbitrary")),
    )(act, wd, x, gain)


def ffn_down_bwd(dy, wd, iw, g, u, name):
    C, S, Fc = g.shape
    D = dy.shape[1]
    tm = _tile(S, MM_TILE_M, 2 * SUBLANES)

    def body(dy_ref, w_ref, g_ref, u_ref, dg_ref, du_ref):
        dact = MACARON * _dotf(dy_ref[...], w_ref[...], "nt")
        g, u = g_ref[...].astype(F32), u_ref[...].astype(F32)
        sg = _sigmoid(g)
        gs = g * sg
        dg_ref[...] = (dact * u * (sg + gs * (1.0 - sg))).astype(dg_ref.dtype)
        du_ref[...] = (dact * gs).astype(du_ref.dtype)

    hid = pl.BlockSpec((None, tm, Fc), lambda s, i: (s, i, 0))
    return pl.pallas_call(
        body, grid=(C, S // tm),
        in_specs=[pl.BlockSpec((tm, D), lambda s, i: (i, 0)),
                  pl.BlockSpec((None, None, Fc, D), lambda s, i: (s, iw, 0, 0)), hid, hid],
        out_specs=[hid, hid], out_shape=[jax.ShapeDtypeStruct((C, S, Fc), BF16)] * 2,
        name=name, compiler_params=_params(("parallel", "parallel")),
    )(dy, wd, g, u)


def ffn_dn(dg, du, wgu, ig, iu, x, gain, dx_res, name):
    C, S, Fc = dg.shape
    D = wgu.shape[2]
    tm = _tile(S, MM_TILE_M // 2, 2 * SUBLANES)

    def body(dg_ref, du_ref, wg_ref, wu_ref, x_ref, g_ref, r_ref, dx_ref, dgain_ref, acc_ref):
        i, s = pl.program_id(0), pl.program_id(1)
        r = _dotf(dg_ref[...], wg_ref[...], "nt") + _dotf(du_ref[...], wu_ref[...], "nt")

        @pl.when((i == 0) & (s == 0))
        def _():
            dgain_ref[...] = jnp.zeros(dgain_ref.shape, F32)

        @pl.when(s == 0)
        def _():
            acc_ref[...] = r

        @pl.when(s > 0)
        def _():
            acc_ref[...] += r

        @pl.when(s == C - 1)
        def _():
            dx, dgain = _vjp(_rms, (x_ref[...], g_ref[...]), acc_ref[...])
            dx_ref[...] = r_ref[...] + dx
            dgain_ref[...] += dgain

    hid = pl.BlockSpec((None, tm, Fc), lambda i, s: (s, i, 0))
    row = pl.BlockSpec((tm, D), lambda i, s: (i, 0))
    vec = pl.BlockSpec((1, D), lambda i, s: (0, 0))
    return pl.pallas_call(
        body, grid=(S // tm, C),
        in_specs=[hid, hid, pl.BlockSpec((None, None, D, Fc), lambda i, s: (s, ig, 0, 0)),
                  pl.BlockSpec((None, None, D, Fc), lambda i, s: (s, iu, 0, 0)), row, vec, row],
        out_specs=[row, vec], out_shape=[jax.ShapeDtypeStruct((S, D), F32), jax.ShapeDtypeStruct((1, D), F32)],
        scratch_shapes=[pltpu.VMEM((tm, D), F32)], name=name, compiler_params=_params(("arbitrary", "arbitrary")),
    )(dg, du, wgu, wgu, x, gain, dx_res)


def ffn_dw(a, d, buf, idx, shape, blocked, name):
    C, P, M, N = shape
    S = d.shape[-2]
    tk = _tile(S, MM_TILE_M, 2 * SUBLANES)
    nk = S // tk

    def body(*refs):
        a_ref, d_ref, o_ref, acc_ref = refs[0], refs[1], refs[-2], refs[-1]
        k = pl.program_id(1)
        r = _dotf(a_ref[...], d_ref[...], "tn")

        @pl.when(k == 0)
        def _():
            acc_ref[...] = r

        @pl.when(k > 0)
        def _():
            acc_ref[...] += r

        @pl.when(k == nk - 1)
        def _():
            o_ref[...] = acc_ref[...].astype(o_ref.dtype)

    if blocked == "a":
        a_spec = pl.BlockSpec((None, tk, M), lambda s, k: (s, k, 0))
        d_spec = pl.BlockSpec((tk, N), lambda s, k: (k, 0))
    else:
        a_spec = pl.BlockSpec((tk, M), lambda s, k: (k, 0))
        d_spec = pl.BlockSpec((None, tk, N), lambda s, k: (s, k, 0))
    out_spec = pl.BlockSpec((None, None, M, N), lambda s, k: (s, idx, 0, 0))
    out_shape = jax.ShapeDtypeStruct(tuple(shape), BF16)
    scratch = [pltpu.VMEM((M, N), F32)]
    if buf is None:
        return pl.pallas_call(body, grid=(C, nk), in_specs=[a_spec, d_spec], out_specs=out_spec, out_shape=out_shape,
                              scratch_shapes=scratch, name=name,
                              compiler_params=_params(("parallel", "arbitrary")))(a, d)
    return pl.pallas_call(body, grid=(C, nk), in_specs=[a_spec, d_spec, pl.BlockSpec(memory_space=pl.ANY)],
                          out_specs=out_spec, out_shape=out_shape, input_output_aliases={2: 0},
                          scratch_shapes=scratch, name=name,
                          compiler_params=_params(("parallel", "arbitrary")))(a, d, buf)


def ffn_fwd(n, x, gain, wgu, wd, ig, iu, iw, tag):
    g, u, act = ffn_up(n, wgu, ig, iu, f"ffn_up_{tag}")
    x_new, n_new = ffn_down(act, wd, iw, x, gain, f"ffn_down_{tag}")
    return x_new, n_new, (n, g, u, act)


def ffn_bwd(dy, saved, x, gain, wgu, wd, ig, iu, iw, gbuf, tag):
    n, g, u, act = saved
    gwgu, gwd = gbuf
    dg, du = ffn_down_bwd(dy, wd, iw, g, u, f"ffn_down_bwd_{tag}")
    gwd = ffn_dw(act, dy, gwd, iw, (N_CHIPS,) + wd.shape[1:], "a", f"ffn_dwd_{tag}")
    dx, dgain = ffn_dn(dg, du, wgu, ig, iu, x, gain, dy, f"ffn_dn_{tag}")
    gwgu = ffn_dw(n, dg, gwgu, ig, (N_CHIPS,) + wgu.shape[1:], "d", f"ffn_dwg_{tag}")
    gwgu = ffn_dw(n, du, gwgu, iu, (N_CHIPS,) + wgu.shape[1:], "d", f"ffn_dwu_{tag}")
    return dx, dgain, (gwgu, gwd)


def _heads(x, H):
    S = x.shape[0]
    return x.reshape(S, H, HEAD_DIM).transpose(1, 0, 2)


def _unheads(x):
    H, S, _ = x.shape
    return x.transpose(1, 0, 2).reshape(S, H * HEAD_DIM)


def _heads_t(x, H):
    return x.T.reshape(H, HEAD_DIM, x.shape[0])


def _unheads_t(x):
    return x.reshape(x.shape[0] * x.shape[1], x.shape[2]).T


def _shift_down(x, n=1):
    return jnp.pad(x, ((n, 0), (0, 0)))[:x.shape[0]]


def _shift_up(x, n=1):
    return jnp.pad(x, ((0, n), (0, 0)))[n:]


def _block_diag(w):
    B, I, J = w.shape
    eye = jnp.eye(B, dtype=w.dtype)
    return (w[:, :, None, :] * eye[:, None, :, None]).reshape(B * I, B * J)


def _block_diag_take(x, B):
    I, J = x.shape[0] // B, x.shape[1] // B
    eye = jnp.eye(B, dtype=x.dtype)
    return jnp.sum(x.reshape(B, I, B, J) * eye[:, None, :, None], axis=2)


def _rope_tables(S):
    half = HEAD_DIM // 2
    inv = jnp.power(ROPE_THETA, -jnp.arange(half, dtype=F32) / half)
    ang = jnp.arange(S, dtype=F32)[:, None] * inv[None, :]
    cos = jnp.concatenate([jnp.cos(ang), jnp.cos(ang)], axis=1)
    sin = jnp.concatenate([jnp.sin(ang), jnp.sin(ang)], axis=1)
    r = jnp.arange(HEAD_DIM)[:, None]
    c = jnp.arange(HEAD_DIM)[None, :]
    rot = jnp.where(r == c + half, -1.0, 0.0) + jnp.where(c == r + half, 1.0, 0.0)
    return cos, sin, rot.astype(F32)


def qk_prep_fwd(x_hm, g, rope_tabs, name):
    H, S, Dh = x_hm.shape
    rows = [x_hm.reshape(H * S, Dh)]
    consts = [g.reshape(1, Dh)]
    periods = [None]
    if rope_tabs is not None:
        rows += [rope_tabs[0], rope_tabs[1]]
        consts += [rope_tabs[2]]
        periods += [S, S]
    fn = _qk_prep(rope_tabs is not None)
    y = rowwise(lambda *a: ([fn(*a)], []), rows, consts, outs=[(Dh, F32)], name=name, periods=periods)[0]
    return y.reshape(H, S, Dh)


def qk_prep_bwd(x_hm, g, rope_tabs, dy_hm, name):
    H, S, Dh = x_hm.shape
    rope = rope_tabs is not None
    rows = [x_hm.reshape(H * S, Dh), dy_hm.reshape(H * S, Dh)]
    consts = [g.reshape(1, Dh)]
    periods = [None, None]
    if rope:
        rows += [rope_tabs[0], rope_tabs[1]]
        consts += [rope_tabs[2]]
        periods += [S, S]
    fn = _qk_prep(rope)

    def f(xv, dyv, *rest):
        if rope:
            cos, sin, gv, rot = rest
            dx, dg = _vjp(lambda a, b: fn(a, cos, sin, b, rot), (xv, gv), dyv)
        else:
            (gv,) = rest
            dx, dg = _vjp(fn, (xv, gv), dyv)
        return [dx], [dg]

    dx, dg = rowwise(f, rows, consts, outs=[(Dh, F32)], accs=[(1, Dh)], name=name, periods=periods)
    return dx.reshape(H, S, Dh), dg.reshape(Dh)


def attn_delta(do_t, o_t, name):
    H, Dh, S = o_t.shape

    def body(a_ref, b_ref, o_ref):
        o_ref[...] = jnp.sum(a_ref[...] * b_ref[...], axis=0, keepdims=True)

    spec = pl.BlockSpec((None, Dh, S), lambda h: (h, 0, 0))
    return pl.pallas_call(
        body, grid=(H,), in_specs=[spec, spec], out_specs=pl.BlockSpec((None, 1, S), lambda h: (h, 0, 0)),
        out_shape=jax.ShapeDtypeStruct((H, 1, S), F32), name=name, compiler_params=_params(("parallel",)),
    )(do_t, o_t)


def even_mixer_fwd(h, w, tag):
    S = h.shape[0]
    W = 512
    H = 8
    z = mm(h, w["w_in"], name=f"ev_in_{tag}")
    xa, ya, q, k, v, f = (z[:, 0:512], z[:, 512:1024], z[:, 1024:1536], z[:, 1536:2048], z[:, 2048:2560],
                          z[:, 2560:2688])
    xs = [_shift_down(xa, LRU_CONV - 1 - tap) for tap in range(LRU_CONV)]
    taps = [w["conv_w"][tap][None] for tap in range(LRU_CONV)]
    xc = rowwise(lambda *a: ([_lru_conv(*a)], []), xs, taps + [w["conv_b"]], outs=[(W, F32)],
                 name=f"lru_conv_{tag}")[0]
    pre = mm(xc, w["w_ax"], name=f"lru_gates_mm_{tag}")
    a, b = rowwise(lambda p_, x_, ba, bx, lam: (list(_lru_gates(p_, x_, ba, bx, lam)), []), [pre, xc],
                   [w["ba"], w["bx"], w["lam"]], outs=[(W, F32), (W, F32)], name=f"lru_gates_{tag}")
    hs = scan_real(a, b, name=f"lru_scan_{tag}")
    a_out = rowwise(lambda y_, h_: ([_gelu(y_) * h_], []), [ya, hs], outs=[(W, F32)], name=f"lru_out_{tag}")[0]
    lf = rowwise(lambda f_, bf: ([_log_sigmoid(f_ + bf)], []), [f], [w["bf"]], outs=[(LANES, F32)],
                 name=f"fox_logf_{tag}")[0]
    c = scan_real(jnp.ones_like(lf), lf, name=f"fox_cumsum_{tag}")
    c_hm = c[:, :H].T
    q_hm, k_hm, v_hm = _heads(q, H), _heads(k, H), _heads(v, H)
    qn = qk_prep_fwd(q_hm, w["qn"], None, f"fox_qprep_{tag}")
    kn = qk_prep_fwd(k_hm, w["kn"], None, f"fox_kprep_{tag}")
    sink = jnp.full((H, 1, 1), NEG, F32)
    o_hm, lse = attn_fwd(qn, kn, v_hm, sink, c_hm[:, None, :], c_hm[:, :, None], name=f"fox_attn_{tag}")
    mo = jnp.concatenate([a_out, _unheads_t(o_hm)], axis=1).astype(BF16)
    saved = dict(h=h, xs=xs, xc=xc, pre=pre, a=a, hs=hs, ya=ya, f=f, c_hm=c_hm, q_hm=q_hm, k_hm=k_hm,
                 v_hm=v_hm, qn=qn, kn=kn, o_hm=o_hm, lse=lse, mo=mo)
    return mo, saved


def even_mixer_bwd(dy, sv, w, tag):
    W = 512
    H = 8
    S = dy.shape[0]
    g = {}
    dmo = mm(dy, w["w_out"], "nt", name=f"ev_dmo_{tag}")
    g["w_out"] = mm(sv["mo"], dy, "tn", name=f"ev_dwout_{tag}")
    da_out, do = dmo[:, :W], dmo[:, W:]
    do_hm = _heads_t(do, H)
    delta = attn_delta(do_hm, sv["o_hm"], f"fox_delta_{tag}")
    c_hm = sv["c_hm"]
    dqn, dkn, dv_hm, dcq, dck = attn_bwd(sv["qn"], sv["kn"], sv["v_hm"], sv["lse"], do_hm, delta,
                                          c_hm[:, None, :], c_hm[:, :, None], name=f"fox_attn_bwd_{tag}")
    dq_hm, g["qn"] = qk_prep_bwd(sv["q_hm"], w["qn"], None, dqn, f"fox_qprep_bwd_{tag}")
    dk_hm, g["kn"] = qk_prep_bwd(sv["k_hm"], w["kn"], None, dkn, f"fox_kprep_bwd_{tag}")
    dc = (dcq[:, 0, :] + dck[:, :, 0]).T
    dc = jnp.pad(dc, ((0, 0), (0, LANES - H)))
    dlf = scan_real(jnp.ones_like(dc), dc, reverse=True, name=f"fox_cumsum_bwd_{tag}")

    def f_logf(f_, d_, bf):
        df, dbf = _vjp(lambda a_, b_: _log_sigmoid(a_ + b_), (f_, bf), d_)
        return [df], [dbf]

    df, dbf = rowwise(f_logf, [sv["f"], dlf], [w["bf"]], outs=[(LANES, F32)], accs=[(1, LANES)],
                      name=f"fox_logf_bwd_{tag}")
    g["bf"] = dbf[0, :H]
    def f_out(y_, h_, d_):
        dyv, dhv = _vjp(lambda a_, b_: _gelu(a_) * b_, (y_, h_), d_)
        return [dyv, dhv], []

    dya, dhs = rowwise(f_out, [sv["ya"], sv["hs"], da_out], outs=[(W, F32), (W, F32)], name=f"lru_out_bwd_{tag}")
    gs = scan_real(_shift_up(sv["a"]), dhs, reverse=True, name=f"lru_scan_bwd_{tag}")

    def f_gates(p_, x_, g_, hp_, ba, bx, lam):
        dp, dx, dba, dbx, dlam = _vjp(_lru_gates, (p_, x_, ba, bx, lam), (g_ * hp_, g_))
        return [dp, dx], [dba, dbx, dlam]

    dpre, dxc, dba, dbx, dlam = rowwise(f_gates, [sv["pre"], sv["xc"], gs, _shift_down(sv["hs"])],
                                        [w["ba"], w["bx"], w["lam"]], outs=[(2 * W, BF16), (W, F32)],
                                        accs=[(1, W)] * 3, name=f"lru_gates_bwd_{tag}")
    g["ba"], g["bx"], g["lam"] = dba[0], dbx[0], dlam[0]
    dxc2 = mm(dpre, w["w_ax"], "nt", name=f"lru_gates_mm_dx_{tag}")
    g["w_ax"] = mm(sv["xc"], dpre, "tn", name=f"lru_gates_mm_dw_{tag}")

    def f_conv(d1, d2, x0, x1, x2, x3):
        d = d1 + d2
        return [d], [jnp.sum(d, axis=0, keepdims=True)] + [jnp.sum(d * xv, axis=0, keepdims=True)
                                                           for xv in (x0, x1, x2, x3)]

    dxc_t, dcb, dw0, dw1, dw2, dw3 = rowwise(f_conv, [dxc, dxc2] + sv["xs"], outs=[(W, F32)],
                                             accs=[(1, W)] * 5, name=f"lru_conv_bwd_{tag}")
    g["conv_b"] = dcb[0]
    g["conv_w"] = jnp.concatenate([dw0, dw1, dw2, dw3], axis=0)
    ds_ = [_shift_up(dxc_t, LRU_CONV - 1 - tap) for tap in range(LRU_CONV)]
    taps = [w["conv_w"][tap][None] for tap in range(LRU_CONV)]
    dxa = rowwise(lambda a, b, c, d, w0, w1, w2, w3: ([a * w0 + b * w1 + c * w2 + d * w3], []), ds_, taps,
                  outs=[(W, F32)], name=f"lru_conv_dx_{tag}")[0]
    dz = jnp.concatenate([dxa, dya, _unheads(dq_hm), _unheads(dk_hm), _unheads(dv_hm), df], axis=1).astype(BF16)
    g["w_in"] = mm(sv["h"], dz, "tn", name=f"ev_dwin_{tag}")
    return dz, g


def odd_mixer_fwd(h, w, tag):
    S = h.shape[0]
    H, KVH = 8, 2
    z = mm(h, w["w_in"], name=f"od_in_{tag}")
    q, k, v, u = z[:, 0:512], z[:, 512:640], z[:, 640:768], z[:, 768:1280]
    tabs = _rope_tables(S)
    q_hm, k_hm, v_hm = _heads(q, H), _heads(k, KVH), _heads(v, KVH)
    qn = qk_prep_fwd(q_hm, w["qn"], tabs, f"swa_qprep_{tag}")
    kn = qk_prep_fwd(k_hm, w["kn"], tabs, f"swa_kprep_{tag}")
    sink = w["sinks"].reshape(H, 1, 1)
    o_hm, lse = attn_fwd(qn, kn, v_hm, sink, window=SWA_WINDOW, name=f"swa_attn_{tag}")
    lam, bexp = w["s5_lam"], w["s5_bexp"]
    bu = mm(u, bexp, name=f"s5_bu_{tag}")
    hs = scan_cplx(lam, bu, name=f"s5_scan_{tag}")
    yssm = mm(hs, w["s5_cexp"], name=f"s5_y_{tag}")
    zz = rowwise(lambda y_, u_, d_: ([_s5_out(y_, u_, d_)], []), [yssm, u], [w["s5_d"]], outs=[(512, F32)],
                 name=f"s5_gelu_{tag}")[0]
    gl = mm(zz, w["glu_w"], name=f"s5_glu_mm_{tag}")
    d_out = rowwise(lambda z_, g_, b_: ([_glu(z_, g_, b_)], []), [zz, gl], [w["glu_b"]], outs=[(512, F32)],
                    name=f"s5_glu_{tag}")[0]
    mo = jnp.concatenate([_unheads_t(o_hm), d_out], axis=1).astype(BF16)
    saved = dict(h=h, q_hm=q_hm, k_hm=k_hm, v_hm=v_hm, qn=qn, kn=kn, o_hm=o_hm, lse=lse, u=u, hs=hs, yssm=yssm,
                 zz=zz, gl=gl, mo=mo, tabs=tabs)
    return mo, saved


def odd_mixer_bwd(dy, sv, w, tag):
    H, KVH = 8, 2
    g = {}
    dmo = mm(dy, w["w_out"], "nt", name=f"od_dmo_{tag}")
    g["w_out"] = mm(sv["mo"], dy, "tn", name=f"od_dwout_{tag}")
    do, dd = dmo[:, :512], dmo[:, 512:]
    do_hm = _heads_t(do, H)
    delta = attn_delta(do_hm, sv["o_hm"], f"swa_delta_{tag}")
    dqn, dkn, dv_hm = attn_bwd(sv["qn"], sv["kn"], sv["v_hm"], sv["lse"], do_hm, delta, window=SWA_WINDOW,
                               name=f"swa_attn_bwd_{tag}")
    dq_hm, g["qn"] = qk_prep_bwd(sv["q_hm"], w["qn"], sv["tabs"], dqn, f"swa_qprep_bwd_{tag}")
    dk_hm, g["kn"] = qk_prep_bwd(sv["k_hm"], w["kn"], sv["tabs"], dkn, f"swa_kprep_bwd_{tag}")
    lse_t, delta_t = sv["lse"][:, 0, :].T, delta[:, 0, :].T
    g["sinks"] = rowwise(lambda l_, d_, s_: ([], [jnp.sum(-jnp.exp(s_ - l_) * d_, axis=0, keepdims=True)]),
                         [lse_t, delta_t], [w["sinks"].reshape(1, H)], accs=[(1, H)], name=f"swa_dsink_{tag}")[0][0]
    def f_glu(z_, g_, d_, b_):
        dz_, dg_, db_ = _vjp(_glu, (z_, g_, b_), d_)
        return [dz_, dg_], [db_]

    dzz1, dgl, dglb = rowwise(f_glu, [sv["zz"], sv["gl"], dd], [w["glu_b"]], outs=[(512, F32), (512, BF16)],
                              accs=[(1, 512)], name=f"s5_glu_bwd_{tag}")
    g["glu_b"] = dglb[0]
    g["glu_w"] = mm(sv["zz"], dgl, "tn", name=f"s5_glu_dw_{tag}")
    dzz2 = mm(dgl, w["glu_w"], "nt", name=f"s5_glu_dz_{tag}")

    def f_gelu(y_, u_, d1, d2, dpar):
        dy_, du_, dd_ = _vjp(_s5_out, (y_, u_, dpar), d1 + d2)
        return [dy_, du_], [dd_]

    dyssm, du1, dsd = rowwise(f_gelu, [sv["yssm"], sv["u"], dzz1, dzz2], [w["s5_d"]],
                              outs=[(512, F32), (512, F32)], accs=[(1, 512)], name=f"s5_gelu_bwd_{tag}")
    g["s5_d"] = dsd[0]
    dhs = mm(dyssm, w["s5_cexp"], "nt", name=f"s5_dh_{tag}")
    g["s5_cexp"] = mm(sv["hs"], dyssm, "tn", name=f"s5_dc_{tag}")
    gs = scan_cplx(w["s5_lam"], dhs, reverse=True, name=f"s5_scan_bwd_{tag}")
    g["s5_bexp"] = mm(sv["u"], gs, "tn", name=f"s5_db_{tag}")
    du2 = mm(gs, w["s5_bexp"], "nt", name=f"s5_du_{tag}")

    def f_dlam(g_, hp_):
        C = g_.shape[1]
        outs_r, outs_i = [], []
        for j in range(C // (2 * LANES)):
            gr, gi = g_[:, 2 * LANES * j:2 * LANES * j + LANES], g_[:, 2 * LANES * j + LANES:2 * LANES * (j + 1)]
            hr, hi = hp_[:, 2 * LANES * j:2 * LANES * j + LANES], hp_[:, 2 * LANES * j + LANES:2 * LANES * (j + 1)]
            outs_r.append(jnp.sum(gr * hr + gi * hi, axis=0, keepdims=True))
            outs_i.append(jnp.sum(gi * hr - gr * hi, axis=0, keepdims=True))
        return [], [jnp.concatenate([x for pair in zip(outs_r, outs_i) for x in pair], axis=1)]

    g["s5_lam"] = rowwise(f_dlam, [gs, _shift_down(sv["hs"])], accs=[(1, gs.shape[1])], name=f"s5_dlam_{tag}")[0]
    du = rowwise(lambda a_, b_: ([a_ + b_], []), [du1, du2], outs=[(512, F32)], name=f"s5_du_add_{tag}")[0]
    dz = jnp.concatenate([_unheads(dq_hm), _unheads(dk_hm), _unheads(dv_hm), du], axis=1).astype(BF16)
    g["w_in"] = mm(sv["h"], dz, "tn", name=f"od_dwin_{tag}")
    return dz, g


def _s5_cols(x_re, x_im):
    n = x_re.shape[0] // LANES
    return jnp.stack([x_re.reshape(n, LANES), x_im.reshape(n, LANES)], axis=1).reshape(1, 2 * n * LANES)


def _s5_uncols(x):
    n = x.shape[1] // (2 * LANES)
    y = x.reshape(n, 2, LANES)
    return y[:, 0].reshape(-1), y[:, 1].reshape(-1)


def _s5_gsel():
    return jnp.repeat(jnp.eye(S5_GROUPS, dtype=F32), S5_STATE, axis=0)


def s5_prep_fwd(lre, lim, ldt, bre, bim, cre, cim, tag):
    GP = S5_GROUPS * S5_STATE
    ins = [lre.reshape(GP, 1), lim.reshape(GP, 1), ldt.reshape(1, S5_GROUPS), _s5_gsel(),
           bre.reshape(GP, S5_GROUP), bim.reshape(GP, S5_GROUP)]
    lbr, lbi, bbr, bbi = whole(_s5_params, ins, [((GP, 1), F32)] * 2 + [((GP, S5_GROUP), F32)] * 2,
                               name=f"s5_params_{tag}")
    lam = _s5_cols(lbr[:, 0], lbi[:, 0])

    def expand_b(bb):
        return _block_diag(bb.reshape(S5_GROUPS, S5_STATE, S5_GROUP).transpose(0, 2, 1))

    n = GP // LANES
    bexp = jnp.stack([expand_b(bbr).reshape(-1, n, LANES), expand_b(bbi).reshape(-1, n, LANES)],
                     axis=2).reshape(-1, 2 * GP)
    c_r = _block_diag(cre.transpose(0, 2, 1))
    c_i = _block_diag(cim.transpose(0, 2, 1))
    cexp = jnp.stack([c_r.reshape(n, LANES, -1), -c_i.reshape(n, LANES, -1)], axis=1).reshape(2 * GP, -1)
    return lam, bexp.astype(BF16), cexp.astype(BF16), ins


def s5_prep_bwd(ins, dlam, dbexp, dcexp, tag):
    GP = S5_GROUPS * S5_STATE
    n = GP // LANES
    dlr, dli = _s5_uncols(dlam)
    db = dbexp.reshape(-1, n, 2, LANES)

    def take_b(x):
        return _block_diag_take(x, S5_GROUPS).transpose(0, 2, 1).reshape(GP, S5_GROUP)

    dbbr, dbbi = take_b(db[:, :, 0].reshape(-1, GP)), take_b(db[:, :, 1].reshape(-1, GP))
    dc = dcexp.reshape(n, 2, LANES, -1)
    dcre = _block_diag_take(dc[:, 0].reshape(GP, -1), S5_GROUPS).transpose(0, 2, 1)
    dcim = -_block_diag_take(dc[:, 1].reshape(GP, -1), S5_GROUPS).transpose(0, 2, 1)

    def f(lre, lim, ldt, gsel, bre, bim, c1, c2, c3, c4):
        d = _vjp(lambda a, b, c, e, f_: _s5_params(a, b, c, gsel, e, f_), (lre, lim, ldt, bre, bim), (c1, c2, c3, c4))
        return d

    outs = [((GP, 1), F32)] * 2 + [((1, S5_GROUPS), F32)] + [((GP, S5_GROUP), F32)] * 2
    dlre, dlim, dldt, dbre, dbim = whole(f, ins + [dlr.reshape(GP, 1), dli.reshape(GP, 1), dbbr, dbbi], outs,
                                          name=f"s5_params_bwd_{tag}")
    shp = (S5_GROUPS, S5_STATE)
    return dict(lre=dlre.reshape(shp), lim=dlim.reshape(shp), ldt=dldt.reshape(S5_GROUPS),
                bre=dbre.reshape(S5_GROUPS, S5_STATE, S5_GROUP), bim=dbim.reshape(S5_GROUPS, S5_STATE, S5_GROUP),
                cre=dcre, cim=dcim)


def _place():
    return lax.axis_index("x"), lax.axis_index("y"), lax.axis_index("c")


def _other_chips(x, y):
    return [(1 - x, y), (x, 1 - y), (1 - x, 1 - y)]


def _half(ref, h):
    n = ref.shape[0] // 2
    return ref.at[pl.ds(h * n, n)]


def _hbm_specs(n):
    return [pl.BlockSpec(memory_space=pl.ANY)] * n


def gather_chips(ws):
    n = len(ws)

    def body(*refs):
        w_refs, out_refs, (send_sems, recv_sems) = refs[:n], refs[n:2 * n], refs[2 * n:]
        x, y, c = _place()
        me, sibling = (x, y, c), (x, y, 1 - c)
        chips = _other_chips(x, y)
        mine = 2 * x + y

        def copy(k, src, dst, to):
            return pltpu.make_async_remote_copy(src_ref=src, dst_ref=dst, send_sem=send_sems.at[k],
                                                recv_sem=recv_sems.at[k], device_id=to, device_id_type=MESH)

        first, passed = [], []
        for p in range(n):
            for j, chip in enumerate(chips):
                first.append(copy(6 * p + j, _half(w_refs[p], c), _half(out_refs[p].at[mine], c), (*chip, c)))
                first[-1].start()
        for p in range(n):
            for j, chip in enumerate(chips):
                block = out_refs[p].at[2 * chip[0] + chip[1]]
                copy(6 * p + j, _half(w_refs[p], c), _half(block, c), me).wait_recv()
                passed.append(copy(6 * p + 3 + j, _half(block, c), _half(block, c), sibling))
                passed[-1].start()
        for p in range(n):
            for j, chip in enumerate(chips):
                block = out_refs[p].at[2 * chip[0] + chip[1]]
                copy(6 * p + 3 + j, _half(w_refs[p], c), _half(block, 1 - c), me).wait_recv()
        for cp in first + passed:
            cp.wait_send()

    return pl.pallas_call(
        body, out_shape=[jax.ShapeDtypeStruct((N_CHIPS,) + w.shape, w.dtype) for w in ws],
        in_specs=_hbm_specs(n), out_specs=_hbm_specs(n),
        scratch_shapes=[pltpu.SemaphoreType.DMA((6 * n,)), pltpu.SemaphoreType.DMA((6 * n,))],
        name="gather_chips",
    )(*ws)


def sibling_halves(gs):
    n = len(gs)

    def body(*refs):
        g_refs, out_refs, (send_sems, recv_sems) = refs[:n], refs[n:2 * n], refs[2 * n:]
        x, y, c = _place()
        me, sibling = (x, y, c), (x, y, 1 - c)

        def copy(p, k, to):
            return pltpu.make_async_remote_copy(src_ref=_half(g_refs[p].at[k], 1 - c), dst_ref=out_refs[p].at[k],
                                                send_sem=send_sems.at[N_CHIPS * p + k],
                                                recv_sem=recv_sems.at[N_CHIPS * p + k],
                                                device_id=to, device_id_type=MESH)

        cps = [copy(p, k, sibling) for p in range(n) for k in range(N_CHIPS)]
        for cp in cps:
            cp.start()
        for p in range(n):
            for k in range(N_CHIPS):
                copy(p, k, me).wait_recv()
        for cp in cps:
            cp.wait_send()

    return pl.pallas_call(
        body, out_shape=[jax.ShapeDtypeStruct((N_CHIPS, g.shape[1] // 2) + g.shape[2:], g.dtype) for g in gs],
        in_specs=_hbm_specs(n), out_specs=_hbm_specs(n),
        scratch_shapes=[pltpu.SemaphoreType.DMA((N_CHIPS * n,)), pltpu.SemaphoreType.DMA((N_CHIPS * n,))],
        name="sibling_halves",
    )(*gs)


def exchange_chips(ps):
    n = len(ps)

    def body(*refs):
        p_refs, out_refs, (send_sems, recv_sems) = refs[:n], refs[n:2 * n], refs[2 * n:]
        x, y, c = _place()
        me = (x, y, c)
        chips = _other_chips(x, y)

        def copy(p, j, chip, to):
            return pltpu.make_async_remote_copy(src_ref=p_refs[p].at[2 * chip[0] + chip[1]], dst_ref=out_refs[p].at[j],
                                                send_sem=send_sems.at[3 * p + j], recv_sem=recv_sems.at[3 * p + j],
                                                device_id=to, device_id_type=MESH)

        cps = [copy(p, j, chip, (*chip, c)) for p in range(n) for j, chip in enumerate(chips)]
        for cp in cps:
            cp.start()
        for p in range(n):
            for j, chip in enumerate(chips):
                copy(p, j, chip, me).wait_recv()
        for cp in cps:
            cp.wait_send()

    return pl.pallas_call(
        body, out_shape=[jax.ShapeDtypeStruct((3,) + p_.shape[1:], p_.dtype) for p_ in ps],
        in_specs=_hbm_specs(n), out_specs=_hbm_specs(n),
        scratch_shapes=[pltpu.SemaphoreType.DMA((3 * n,)), pltpu.SemaphoreType.DMA((3 * n,))],
        name="exchange_chips",
    )(*ps)


def sibling_join(rs):
    n = len(rs)

    def body(*refs):
        r_refs, out_refs, (send_sems, recv_sems) = refs[:n], refs[n:2 * n], refs[2 * n:]
        x, y, c = _place()

        def copy(p, h, to):
            return pltpu.make_async_remote_copy(src_ref=r_refs[p], dst_ref=_half(out_refs[p], h),
                                                send_sem=send_sems.at[p], recv_sem=recv_sems.at[p],
                                                device_id=to, device_id_type=MESH)

        cps = [copy(p, c, (x, y, 1 - c)) for p in range(n)]
        for cp in cps:
            cp.start()
        for p in range(n):
            copy(p, 1 - c, (x, y, c)).wait_recv()
        for cp in cps:
            cp.wait_send()

    return pl.pallas_call(
        body, out_shape=[jax.ShapeDtypeStruct((2 * r.shape[0],) + r.shape[1:], r.dtype) for r in rs],
        in_specs=_hbm_specs(n), out_specs=_hbm_specs(n),
        scratch_shapes=[pltpu.SemaphoreType.DMA((n,)), pltpu.SemaphoreType.DMA((n,))],
        name="sibling_join",
    )(*rs)


def gather_devices(v, name):
    R = v.shape[0]

    def body(v_ref, out_ref, send_sems, recv_sems, local_sem):
        x, y, c = _place()
        me, sibling = (x, y, c), (x, y, 1 - c)
        chips = _other_chips(x, y)

        def rows(px, py, pc):
            return out_ref.at[pl.ds((4 * px + 2 * py + pc) * R, R), :]

        def copy(k, block, to, src=None):
            return pltpu.make_async_remote_copy(src_ref=rows(*block) if src is None else src, dst_ref=rows(*block),
                                                send_sem=send_sems.at[k], recv_sem=recv_sems.at[k],
                                                device_id=to, device_id_type=MESH)

        mine = pltpu.make_async_copy(v_ref, rows(*me), local_sem)
        mine.start()
        first = [copy(0, me, sibling, src=v_ref)]
        first += [copy(1 + j, me, (*chip, c), src=v_ref) for j, chip in enumerate(chips)]
        for cp in first:
            cp.start()
        passed = [copy(4 + j, (*chip, c), sibling) for j, chip in enumerate(chips)]
        for j, chip in enumerate(chips):
            copy(1 + j, (*chip, c), me).wait_recv()
            passed[j].start()
        copy(0, sibling, me).wait_recv()
        for j, chip in enumerate(chips):
            copy(4 + j, (*chip, 1 - c), me).wait_recv()
        for cp in first + passed:
            cp.wait_send()
        mine.wait()

    return pl.pallas_call(
        body, out_shape=jax.ShapeDtypeStruct((N_DEV * R, LANES), v.dtype),
        in_specs=[pl.BlockSpec(memory_space=pltpu.VMEM)], out_specs=pl.BlockSpec(memory_space=pltpu.VMEM),
        scratch_shapes=[pltpu.SemaphoreType.DMA((7,)), pltpu.SemaphoreType.DMA((7,)), pltpu.SemaphoreType.DMA],
        name=name, compiler_params=_params(),
    )(v)


def _flat_rows(n, mult):
    return -(-n // (LANES * mult)) * mult


def _adam(w, g, m, v):
    m = ADAM_B1 * m + (1.0 - ADAM_B1) * g
    v = ADAM_B2 * v + (1.0 - ADAM_B2) * (g * g)
    m_hat = m / (1.0 - ADAM_B1 ** ADAM_STEP)
    v_hat = v / (1.0 - ADAM_B2 ** ADAM_STEP)
    return -ADAM_LR * (m_hat / (jnp.sqrt(v_hat) + ADAM_EPS) + ADAM_WD * w), m, v


def adam_2d(w, g, m, v, name):
    shape = w.shape
    F = shape[-1]
    if w.ndim == 3 and shape[1] % (2 * SUBLANES) == 0:
        L, R, _ = shape
        tr = R
        for t in (512, 256, 128, 64, 32, 16):
            if R % t == 0 and 7 * t * max(F, LANES) * 4 <= ROW_TILE_BYTES:
                tr = t
                break

        def body(w_ref, g_ref, m_ref, v_ref, d_ref, m2_ref, v2_ref):
            d_ref[...], m2_ref[...], v2_ref[...] = _adam(w_ref[...], g_ref[...], m_ref[...], v_ref[...])

        spec = pl.BlockSpec((None, tr, F), lambda l, i: (l, i, 0))
        return pl.pallas_call(
            body, grid=(L, R // tr), in_specs=[spec] * 4, out_specs=[spec] * 3,
            out_shape=[jax.ShapeDtypeStruct(shape, F32)] * 3, name=name, compiler_params=_params(("parallel", "parallel")),
        )(w, g, m, v)
    a = [t.reshape(-1, F) for t in (w, g, m, v)]
    d, m2, v2 = rowwise(lambda w_, g_, m_, v_: (list(_adam(w_, g_, m_, v_)), []), a, outs=[(F, F32)] * 3, name=name)
    return d.reshape(shape), m2.reshape(shape), v2.reshape(shape)


WEIGHTS = ['ffn1_norm', 'ffn1_wg', 'ffn1_wu', 'ffn1_wd', 'mix_norm', 'ffn2_norm', 'ffn2_wg', 'ffn2_wu', 'ffn2_wd',
           'ple_w', 'ple_norm', 'ple_gate_norm', 'ple_gate_w', 'ev_w_in', 'lru_conv_w', 'lru_conv_b', 'lru_wa',
           'lru_ba', 'lru_wx', 'lru_bx', 'lru_lambda', 'fox_bf', 'fox_q_norm', 'fox_k_norm', 'ev_w_out', 'od_w_in',
           'swa_q_norm', 'swa_k_norm', 'swa_sinks', 's5_lambda_re', 's5_lambda_im', 's5_log_dt', 's5_b_re',
           's5_b_im', 's5_c_re', 's5_c_im', 's5_d', 's5_glu_w', 's5_glu_b', 'od_w_out']
SHARD_AXIS = {'ffn1_wg': 2, 'ffn1_wu': 2, 'ffn1_wd': 1, 'ffn2_wg': 2, 'ffn2_wu': 2, 'ffn2_wd': 1, 'ple_w': 2,
              'ple_gate_w': 1, 'ev_w_in': 2, 'lru_conv_w': 2, 'ev_w_out': 1, 'od_w_in': 2, 's5_d': 1,
              's5_glu_w': 1, 's5_glu_b': 1, 'od_w_out': 1}
EXACT_SHARDED = ('lru_conv_w', 's5_d', 's5_glu_b')
ADAM_TRANSPOSED = ('ffn1_wg', 'ffn1_wu', 'ffn2_wg', 'ffn2_wu', 'od_w_in')
SHARDED = [n for n in WEIGHTS if n in SHARD_AXIS]
REPLICATED = [n for n in WEIGHTS if n not in SHARD_AXIS]


GROUPS = {
    'wgu': ['ffn1_wg', 'ffn1_wu', 'ffn2_wg', 'ffn2_wu'],
    'wd': ['ffn1_wd', 'ffn2_wd'],
    'w_rows': ['ple_gate_w', 'ev_w_out', 'od_w_out'],
    'ple_w': ['ple_w'], 'ev_w_in': ['ev_w_in'], 'od_w_in': ['od_w_in'], 's5_glu_w': ['s5_glu_w'],
}
REDUCED_GROUPS = list(GROUPS)


def _chip():
    return 2 * lax.axis_index("x") + lax.axis_index("y")


def gather_weights(shards):
    own = {k: jnp.concatenate([shards[n] for n in names], axis=0).astype(BF16) for k, names in GROUPS.items()}
    own['exact'] = jnp.concatenate([shards['lru_conv_w'], shards['s5_d'][:, None], shards['s5_glu_b'][:, None]], axis=1)
    keys = list(own)
    got = gather_chips([own[k] for k in keys])
    return {k: lax.dynamic_update_index_in_dim(g, own[k], _chip(), 0) for k, g in zip(keys, got)}


def _rows_by_chip(w):
    return w.reshape(w.shape[0] * w.shape[1], w.shape[2])


def _cols_by_chip(w):
    return w.transpose(1, 0, 2).reshape(w.shape[1], w.shape[0] * w.shape[2])


def _chip_rows(g):
    return g.reshape(N_CHIPS, g.shape[0] // N_CHIPS, g.shape[1])


def _chip_cols(g):
    return g.reshape(g.shape[0], N_CHIPS, g.shape[1] // N_CHIPS).transpose(1, 0, 2)


def full_weights(gw, depth):
    n_ev = (depth + 1) // 2
    ex = gw['exact']
    return dict(
        ple_gate_w=[_rows_by_chip(gw['w_rows'][:, l]) for l in range(depth)],
        ev_w_out=[_rows_by_chip(gw['w_rows'][:, depth + j]) for j in range(n_ev)],
        od_w_out=[_rows_by_chip(gw['w_rows'][:, depth + n_ev + j]) for j in range(depth // 2)],
        ple_w=[_cols_by_chip(gw['ple_w'][:, l]) for l in range(depth)],
        ev_w_in=[_cols_by_chip(gw['ev_w_in'][:, j]) for j in range(n_ev)],
        od_w_in=[_cols_by_chip(gw['od_w_in'][:, j]) for j in range(depth // 2)],
        s5_glu_w=[_rows_by_chip(gw['s5_glu_w'][:, j]) for j in range(depth // 2)],
        lru_conv_w=[_cols_by_chip(ex[:, j, 0:LRU_CONV]) for j in range(n_ev)],
        s5_d=[ex[:, j, LRU_CONV].reshape(-1) for j in range(depth // 2)],
        s5_glu_b=[ex[:, j, LRU_CONV + 1].reshape(-1) for j in range(depth // 2)],
    )


def _add_tile(rows, width):
    for t in (1024, 512, 256, 128, 64, 32, 16):
        if rows % t == 0 and 3 * t * width * 4 <= ROW_TILE_BYTES:
            return t
    return rows


def pair_add(g, t, c, name):
    C, F = g.shape[0], g.shape[-1]
    rows = math.prod(t.shape[1:-1])
    tr = _add_tile(rows, F)
    nb = rows // tr

    def body(c_ref, g_ref, t_ref, o_ref):
        o_ref[...] = (g_ref[...].astype(F32) + t_ref[...].astype(F32)).astype(o_ref.dtype)

    spec = pl.BlockSpec((None, tr, F), lambda k, i, c_ref: (k, i, 0))
    out = pl.pallas_call(
        body, out_shape=jax.ShapeDtypeStruct((C, rows, F), BF16),
        grid_spec=pltpu.PrefetchScalarGridSpec(
            num_scalar_prefetch=1, grid=(C, nb),
            in_specs=[pl.BlockSpec((None, tr, F), lambda k, i, c_ref: (k, c_ref[0] * nb + i, 0)), spec],
            out_specs=spec),
        name=name, compiler_params=_params(("parallel", "parallel")),
    )(c.reshape(1).astype(jnp.int32), g.reshape(C, 2 * rows, F), t.reshape(C, rows, F))
    return out.reshape(t.shape)


def chips_add(p, xs, chip, name):
    F = p.shape[-1]
    rows = math.prod(p.shape[1:-1])
    tr = _add_tile(rows, F)

    def body(m_ref, p_ref, a_ref, b_ref, d_ref, o_ref):
        o_ref[...] = ((p_ref[...].astype(F32) + a_ref[...].astype(F32))
                      + (b_ref[...].astype(F32) + d_ref[...].astype(F32)))

    def other(j):
        return pl.BlockSpec((None, tr, F), lambda i, m_ref: (j, i, 0))

    x3 = xs.reshape(3, rows, F)
    out = pl.pallas_call(
        body, out_shape=jax.ShapeDtypeStruct((rows, F), F32),
        grid_spec=pltpu.PrefetchScalarGridSpec(
            num_scalar_prefetch=1, grid=(rows // tr,),
            in_specs=[pl.BlockSpec((None, tr, F), lambda i, m_ref: (m_ref[0], i, 0)), other(0), other(1), other(2)],
            out_specs=pl.BlockSpec((tr, F), lambda i, m_ref: (i, 0))),
        name=name, compiler_params=_params(("parallel",)),
    )(chip.reshape(1).astype(jnp.int32), p.reshape(N_CHIPS, rows, F), x3, x3, x3)
    return out.reshape(p.shape[1:])


def reduce_sharded(groups):
    keys = list(groups)
    c = lax.axis_index("c")
    gs = [groups[k] for k in keys]
    theirs = sibling_halves(gs)
    pairs = [pair_add(g, t, c, f"pair_add_{k}") for k, g, t in zip(keys, gs, theirs)]
    got = exchange_chips(pairs)
    halves = [chips_add(p_, x_, _chip(), f"chips_add_{k}") for k, p_, x_ in zip(keys, pairs, got)]
    joined = sibling_join(halves)
    out = {}
    for k, h, j in zip(keys, halves, joined):
        out[k] = lax.dynamic_update_slice_in_dim(j, h, c * h.shape[0], axis=0)
    return out


SMALL_GRADS = REPLICATED + list(EXACT_SHARDED)


def _flatten_small(tensors, shapes):
    parts = [tensors[n].astype(F32).reshape(-1) if n in tensors else jnp.zeros((math.prod(shapes[n]),), F32)
             for n in SMALL_GRADS]
    flat = jnp.concatenate(parts)
    rows = _flat_rows(flat.shape[0], SUBLANES)
    return jnp.pad(flat, (0, rows * LANES - flat.shape[0])).reshape(rows, LANES)


def _unflatten_small(flat, shapes):
    flat = flat.reshape(-1)
    out, off = {}, 0
    for n in SMALL_GRADS:
        size = math.prod(shapes[n])
        out[n] = flat[off:off + size].reshape(shapes[n])
        off += size
    return out


def grad_groups(gwgu, gwd, G):
    def st(xs):
        return jnp.stack(xs, axis=1).astype(BF16)

    return {
        'wgu': gwgu, 'wd': gwd,
        'w_rows': st([_chip_rows(g) for n in GROUPS['w_rows'] for g in G[n]]),
        'ple_w': st([_chip_cols(g) for g in G['ple_w']]),
        'ev_w_in': st([_chip_cols(g) for g in G['ev_w_in']]),
        'od_w_in': st([_chip_cols(g) for g in G['od_w_in']]),
        's5_glu_w': st([_chip_rows(g) for g in G['s5_glu_w']]),
    }


def ungroup(red, shapes):
    out = {}
    for k, names in GROUPS.items():
        off = 0
        for n in names:
            out[n] = red[k][off:off + shapes[n][0]]
            off += shapes[n][0]
    return out


def _layer_weights(full, small, i, depth):
    j = i // 2
    w = dict(
        g1=small['ffn1_norm'][i][None], gm=small['mix_norm'][i][None], g2=small['ffn2_norm'][i][None],
        gp=small['ple_norm'][i][None], gg=small['ple_gate_norm'][i][None],
        ffn1=(i, depth + i, i), ffn2=(2 * depth + i, 3 * depth + i, depth + i),
        ple_w=full['ple_w'][i], ple_gate_w=full['ple_gate_w'][i],
    )
    if i % 2 == 0:
        w_in = full['ev_w_in'][j]
        w['mix'] = dict(
            w_in=jnp.pad(w_in, ((0, 0), (0, 2688 - w_in.shape[1]))), w_out=full['ev_w_out'][j],
            conv_w=full['lru_conv_w'][j].astype(F32), conv_b=small['lru_conv_b'][j][None],
            w_ax=jnp.concatenate([_block_diag(small['lru_wa'][j]), _block_diag(small['lru_wx'][j])],
                                 axis=1).astype(BF16),
            ba=small['lru_ba'][j][None], bx=small['lru_bx'][j][None], lam=small['lru_lambda'][j][None],
            bf=jnp.pad(small['fox_bf'][j], (0, LANES - 8))[None], qn=small['fox_q_norm'][j],
            kn=small['fox_k_norm'][j])
    else:
        lam, bexp, cexp, ins = s5_prep_fwd(small['s5_lambda_re'][j], small['s5_lambda_im'][j], small['s5_log_dt'][j],
                                           small['s5_b_re'][j], small['s5_b_im'][j], small['s5_c_re'][j],
                                           small['s5_c_im'][j], f"L{i}")
        w['mix'] = dict(
            w_in=full['od_w_in'][j], w_out=full['od_w_out'][j], qn=small['swa_q_norm'][j], kn=small['swa_k_norm'][j],
            sinks=small['swa_sinks'][j], s5_lam=lam, s5_bexp=bexp, s5_cexp=cexp, s5_ins=ins,
            s5_d=full['s5_d'][j].astype(F32)[None], glu_w=full['s5_glu_w'][j], glu_b=full['s5_glu_b'][j].astype(F32)[None])
    return w


def layer_fwd(x0, n1, p_i, w, ffnw, next_g1, i):
    tag = f"L{i}"
    sv = {}
    wgu, wd = ffnw
    x1, hm, sv['ffn1'] = ffn_fwd(n1, x0, w['gm'], wgu, wd, *w['ffn1'], f"1_{tag}")
    if i % 2 == 0:
        mo, sv['mix'] = even_mixer_fwd(hm, w['mix'], tag)
    else:
        mo, sv['mix'] = odd_mixer_fwd(hm, w['mix'], tag)
    x2, n2 = mm_add_norm(mo, w['mix']['w_out'], x1, w['g2'], f"mix_out_{tag}")
    x3, ng, sv['ffn2'] = ffn_fwd(n2, x2, w['gg'], wgu, wd, *w['ffn2'], f"2_{tag}")
    gpre = mm(ng, w['ple_gate_w'], name=f"ple_gate_{tag}")
    epre = mm(p_i, w['ple_w'], name=f"ple_emb_{tag}")
    D = x0.shape[1]
    if next_g1 is None:
        x4 = rowwise(lambda a, b, c, pn: ([_ple_out(a, b, c, pn)], []), [x3, gpre, epre], [w['gp']],
                     outs=[(D, F32)], name=f"ple_out_{tag}")[0]
        n_next = None
    else:
        def f(a, b, c, pn, gn):
            y = _ple_out(a, b, c, pn)
            return [y, _rms(y, gn)], []

        x4, n_next = rowwise(f, [x3, gpre, epre], [w['gp'], next_g1], outs=[(D, F32), (D, BF16)],
                             name=f"ple_out_{tag}")
    sv.update(x0=x0, x1=x1, x2=x2, x3=x3, ng=ng, gpre=gpre, epre=epre, p=p_i)
    return x4, n_next, sv


def layer_bwd(dx4, sv, w, ffnw, gbuf, i):
    tag = f"L{i}"
    D = dx4.shape[1]
    g = {}
    wgu, wd = ffnw

    def f_ple(a, b, c, d, pn):
        da, db, dc, dpn = _vjp(_ple_out, (a, b, c, pn), d)
        return [db, dc], [dpn]

    dgpre, depre, dgp = rowwise(f_ple, [sv['x3'], sv['gpre'], sv['epre'], dx4], [w['gp']],
                                outs=[(D, BF16), (D, BF16)], accs=[(1, D)], name=f"ple_out_bwd_{tag}")
    g['gp'] = dgp[0]
    g['ple_w'] = mm(sv['p'], depre, "tn", name=f"ple_emb_dw_{tag}")
    g['ple_gate_w'] = mm(sv['ng'], dgpre, "tn", name=f"ple_gate_dw_{tag}")
    dx3, dgg = mm_norm_bwd(dgpre, w['ple_gate_w'], sv['x3'], w['gg'], dx4, f"ple_gate_dx_{tag}")
    g['gg'] = dgg[0]
    dx2, dg2, gbuf = ffn_bwd(dx3, sv['ffn2'], sv['x2'], w['g2'], wgu, wd, *w['ffn2'], gbuf, f"2_{tag}")
    g['g2'] = dg2[0]
    if i % 2 == 0:
        dz, g['mix'] = even_mixer_bwd(dx2, sv['mix'], w['mix'], tag)
    else:
        dz, g['mix'] = odd_mixer_bwd(dx2, sv['mix'], w['mix'], tag)
    dx1, dgm = mm_norm_bwd(dz, w['mix']['w_in'], sv['x1'], w['gm'], dx2, f"mix_dh_{tag}")
    g['gm'] = dgm[0]
    dx0, dg1, gbuf = ffn_bwd(dx1, sv['ffn1'], sv['x0'], w['g1'], wgu, wd, *w['ffn1'], gbuf, f"1_{tag}")
    g['g1'] = dg1[0]
    return dx0, g, gbuf


def _collect_grads(layer_grads, depth):
    st = lambda xs: jnp.stack(xs)
    G = {}
    L = layer_grads
    G['ffn1_norm'] = st([g['g1'] for g in L])
    G['mix_norm'] = st([g['gm'] for g in L])
    G['ffn2_norm'] = st([g['g2'] for g in L])
    G['ple_norm'] = st([g['gp'] for g in L])
    G['ple_gate_norm'] = st([g['gg'] for g in L])
    G['ple_w'] = st([g['ple_w'] for g in L])
    G['ple_gate_w'] = st([g['ple_gate_w'] for g in L])
    ev = [L[i]['mix'] for i in range(0, depth, 2)]
    od = [L[i]['mix'] for i in range(1, depth, 2)]
    G['ev_w_in'] = st([m['w_in'][:, :2568] for m in ev])
    G['ev_w_out'] = st([m['w_out'] for m in ev])
    G['lru_conv_w'] = st([m['conv_w'] for m in ev])
    G['lru_conv_b'] = st([m['conv_b'] for m in ev])
    G['lru_wa'] = st([_block_diag_take(m['w_ax'][:, :512], LRU_BLOCKS) for m in ev])
    G['lru_wx'] = st([_block_diag_take(m['w_ax'][:, 512:], LRU_BLOCKS) for m in ev])
    G['lru_ba'] = st([m['ba'] for m in ev])
    G['lru_bx'] = st([m['bx'] for m in ev])
    G['lru_lambda'] = st([m['lam'] for m in ev])
    G['fox_bf'] = st([m['bf'] for m in ev])
    G['fox_q_norm'] = st([m['qn'] for m in ev])
    G['fox_k_norm'] = st([m['kn'] for m in ev])
    G['od_w_in'] = st([m['w_in'] for m in od])
    G['od_w_out'] = st([m['w_out'] for m in od])
    G['swa_q_norm'] = st([m['qn'] for m in od])
    G['swa_k_norm'] = st([m['kn'] for m in od])
    G['swa_sinks'] = st([m['sinks'] for m in od])
    G['s5_lambda_re'] = st([m['s5']['lre'] for m in od])
    G['s5_lambda_im'] = st([m['s5']['lim'] for m in od])
    G['s5_log_dt'] = st([m['s5']['ldt'] for m in od])
    G['s5_b_re'] = st([m['s5']['bre'] for m in od])
    G['s5_b_im'] = st([m['s5']['bim'] for m in od])
    G['s5_c_re'] = st([m['s5']['cre'] for m in od])
    G['s5_c_im'] = st([m['s5']['cim'] for m in od])
    G['s5_d'] = st([m['s5_d'] for m in od])
    G['s5_glu_w'] = st([m['glu_w'] for m in od])
    G['s5_glu_b'] = st([m['glu_b'] for m in od])
    return G


def local_step(x, p, target, ffnw, full, small):
    depth = p.shape[0]
    S, D = x.shape
    ws = [_layer_weights(full, small, i, depth) for i in range(depth)]
    saved = []
    xi, ni = add_norm(x, None, ws[0]['g1'], "norm1_L0")
    for i in range(depth):
        xi, ni, sv = layer_fwd(xi, ni, p[i], ws[i], ffnw, ws[i + 1]['g1'] if i + 1 < depth else None, i)
        saved.append(sv)

    def f_loss(y, t):
        e = y - t
        return [e * (1.0 / D)], [0.5 * jnp.sum(jnp.mean(e * e, axis=-1, keepdims=True), axis=0, keepdims=True)]

    dx, loss = rowwise(f_loss, [xi, target], outs=[(D, F32)], accs=[(1, 1)], name="loss")
    grads = [None] * depth
    gbuf = (None, None)
    for i in reversed(range(depth)):
        dx, grads[i], gbuf = layer_bwd(dx, saved[i], ws[i], ffnw, gbuf, i)
        if i % 2 == 1:
            m = grads[i]['mix']
            m['s5'] = s5_prep_bwd(ws[i]['mix']['s5_ins'], m['s5_lam'], m['s5_bexp'], m['s5_cexp'], f"L{i}")
    return loss[0, 0], dx, gbuf, _collect_grads(grads, depth)


def kernel(x, p, ffn1_norm, ffn1_wg, ffn1_wu, ffn1_wd, mix_norm, ffn2_norm, ffn2_wg, ffn2_wu, ffn2_wd, ple_w, ple_norm, ple_gate_norm, ple_gate_w, ev_w_in, lru_conv_w, lru_conv_b, lru_wa, lru_ba, lru_wx, lru_bx, lru_lambda, fox_bf, fox_q_norm, fox_k_norm, ev_w_out, od_w_in, swa_q_norm, swa_k_norm, swa_sinks, s5_lambda_re, s5_lambda_im, s5_log_dt, s5_b_re, s5_b_im, s5_c_re, s5_c_im, s5_d, s5_glu_w, s5_glu_b, od_w_out, loss_target, m_ffn1_norm, m_ffn1_wg, m_ffn1_wu, m_ffn1_wd, m_mix_norm, m_ffn2_norm, m_ffn2_wg, m_ffn2_wu, m_ffn2_wd, m_ple_w, m_ple_norm, m_ple_gate_norm, m_ple_gate_w, m_ev_w_in, m_lru_conv_w, m_lru_conv_b, m_lru_wa, m_lru_ba, m_lru_wx, m_lru_bx, m_lru_lambda, m_fox_bf, m_fox_q_norm, m_fox_k_norm, m_ev_w_out, m_od_w_in, m_swa_q_norm, m_swa_k_norm, m_swa_sinks, m_s5_lambda_re, m_s5_lambda_im, m_s5_log_dt, m_s5_b_re, m_s5_b_im, m_s5_c_re, m_s5_c_im, m_s5_d, m_s5_glu_w, m_s5_glu_b, m_od_w_out, v_ffn1_norm, v_ffn1_wg, v_ffn1_wu, v_ffn1_wd, v_mix_norm, v_ffn2_norm, v_ffn2_wg, v_ffn2_wu, v_ffn2_wd, v_ple_w, v_ple_norm, v_ple_gate_norm, v_ple_gate_w, v_ev_w_in, v_lru_conv_w, v_lru_conv_b, v_lru_wa, v_lru_ba, v_lru_wx, v_lru_bx, v_lru_lambda, v_fox_bf, v_fox_q_norm, v_fox_k_norm, v_ev_w_out, v_od_w_in, v_swa_q_norm, v_swa_k_norm, v_swa_sinks, v_s5_lambda_re, v_s5_lambda_im, v_s5_log_dt, v_s5_b_re, v_s5_b_im, v_s5_c_re, v_s5_c_im, v_s5_d, v_s5_glu_w, v_s5_glu_b, v_od_w_out):
    args = locals()
    wts = {n: args[n] for n in WEIGHTS}
    ms = {n: args["m_" + n] for n in WEIGHTS}
    vs = {n: args["v_" + n] for n in WEIGHTS}
    shapes = {n: wts[n].shape for n in WEIGHTS}

    depth = p.shape[0]
    gw = gather_weights({n: wts[n] for n in SHARDED})
    small = {n: wts[n] for n in REPLICATED}
    loss, dx, (gwgu, gwd), G = local_step(x[0], p[:, 0], loss_target[0], (gw['wgu'], gw['wd']),
                                          full_weights(gw, depth), small)
    loss = lax.psum(loss, ("x", "y", "c"))

    gsh = ungroup(reduce_sharded(grad_groups(gwgu, gwd, G)), shapes)
    full_shapes = {n: (G[n].shape if n in EXACT_SHARDED else shapes[n]) for n in SMALL_GRADS}
    flat_g = _flatten_small(G, full_shapes)
    g8 = gather_devices(flat_g, "gather_small_grads").reshape((N_DEV,) + flat_g.shape)
    wf, mf, vf = (_flatten_small({n: t[n] for n in REPLICATED}, full_shapes) for t in (wts, ms, vs))

    def f_small(g0, g1, g2, g3, g4, g5, g6, g7, w_, m_, v_):
        gsum = ((g0 + g1) + (g2 + g3)) + ((g4 + g5) + (g6 + g7))
        return [gsum] + list(_adam(w_, gsum, m_, v_)), []

    gs_f, ds_f, ms_f, vs_f = rowwise(f_small, [g8[d] for d in range(N_DEV)] + [wf, mf, vf],
                                     outs=[(LANES, F32)] * 4, name="adam_small")
    out_g, out_d, out_m, out_v = {}, {}, {}, {}
    for dst, flat in ((out_g, gs_f), (out_d, ds_f), (out_m, ms_f), (out_v, vs_f)):
        dst.update(_unflatten_small(flat, full_shapes))
    for n in EXACT_SHARDED:
        width = shapes[n][SHARD_AXIS[n]]
        gsh[n] = lax.dynamic_slice_in_dim(out_g[n], _chip() * width, width, axis=SHARD_AXIS[n])
    for n in SHARDED:
        out_g[n] = gsh[n]
        if n in ADAM_TRANSPOSED:
            def t(a):
                return a.transpose(0, 2, 1)
            d_, m_, v_ = adam_2d(t(wts[n]), t(gsh[n]), t(ms[n]), t(vs[n]), f"adam_{n}")
            out_d[n], out_m[n], out_v[n] = t(d_), t(m_), t(v_)
        else:
            out_d[n], out_m[n], out_v[n] = adam_2d(wts[n], gsh[n], ms[n], vs[n], f"adam_{n}")
    return (loss, dx[None], *[out_g[n] for n in WEIGHTS], *[out_d[n] for n in WEIGHTS],
            *[out_m[n] for n in WEIGHTS], *[out_v[n] for n in WEIGHTS])
```

```python
import functools
import math

import jax
import jax.numpy as jnp
from jax import lax
from jax.experimental import pallas as pl
from jax.experimental.pallas import tpu as pltpu

F32 = jnp.float32
BF16 = jnp.bfloat16
MXU_DTYPE = BF16
HI = lax.Precision.HIGHEST
MESH = pl.DeviceIdType.MESH

VMEM_LIMIT_BYTES = 56 * 1024 * 1024
ROW_TILE_BYTES = 5 * 1024 * 1024
MM_VMEM_BYTES = 40 * 1024 * 1024
MM_TILE_M = 1024
MM_TILE_N = 1408
FLAT_W = 2048
LANES = 128
SUBLANES = 8

HEAD_DIM = 64
LRU_BLOCKS = 8
LRU_CONV = 4
LRU_C = 8.0
SWA_WINDOW = 128
SWA_GROUP = 4
S5_GROUP = 16
S5_GROUPS = 32
S5_STATE = 64
ROPE_THETA = 10000.0
EPS = 1e-6
MACARON = 0.5
NEG = -1e30

ADAM_LR = 0.001
ADAM_B1 = 0.9
ADAM_B2 = 0.999
ADAM_EPS = 1e-08
ADAM_WD = 0.01
ADAM_STEP = 10

N_CHIPS = 4
N_DEV = 8


def _pick(n, cands):
    for c in cands:
        if n % c == 0:
            return c
    return n


def _tile(n, cap, unit):
    best = None
    for t in range(unit, min(n, cap) + 1, unit):
        if n % t == 0:
            best = t
    return n if best is None else best


def _params(sem=None):
    return pltpu.CompilerParams(dimension_semantics=sem, vmem_limit_bytes=VMEM_LIMIT_BYTES)


def rowwise(fn, rows, consts=(), outs=(), accs=(), name="rowwise", periods=None):
    rows, consts = list(rows), list(consts)
    n_r, n_c, n_o, n_a = len(rows), len(consts), len(outs), len(accs)
    R = rows[0].shape[0]
    periods = list(periods) if periods is not None else [None] * n_r
    per_row = sum(max(r.shape[1], LANES) * 4 for r in rows) + sum(max(f, LANES) * 4 for f, _ in outs)
    limit = min([R] + [p for p in periods if p is not None])
    tr = limit
    for c in (1024, 512, 256, 128, 64, 32, 16):
        if c <= limit and limit % c == 0 and R % c == 0 and c * per_row <= ROW_TILE_BYTES:
            tr = c
            break

    def row_map(period):
        if period is None:
            return lambda i: (i, 0)
        nb = period // tr
        return lambda i: (i % nb, 0)

    in_specs = [pl.BlockSpec((tr, r.shape[1]), row_map(p)) for r, p in zip(rows, periods)]
    in_specs += [pl.BlockSpec(c.shape, lambda i: (0, 0)) for c in consts]
    out_shape = [jax.ShapeDtypeStruct((R, f), dt) for f, dt in outs]
    out_shape += [jax.ShapeDtypeStruct(tuple(s), F32) for s in accs]
    out_specs = [pl.BlockSpec((tr, f), lambda i: (i, 0)) for f, _ in outs]
    out_specs += [pl.BlockSpec(tuple(s), lambda i: (0, 0)) for s in accs]

    def body(*refs):
        ins = [r[...] for r in refs[:n_r + n_c]]
        o_refs = refs[n_r + n_c:n_r + n_c + n_o]
        a_refs = refs[n_r + n_c + n_o:]
        ro, ra = fn(*ins)
        for ref, val in zip(o_refs, ro):
            ref[...] = val.astype(ref.dtype)
        if n_a:
            @pl.when(pl.program_id(0) == 0)
            def _():
                for ref in a_refs:
                    ref[...] = jnp.zeros(ref.shape, ref.dtype)
            for ref, val in zip(a_refs, ra):
                ref[...] += val.astype(F32)

    res = pl.pallas_call(
        body, grid=(R // tr,), in_specs=in_specs, out_specs=out_specs, out_shape=out_shape,
        name=name, compiler_params=_params(("arbitrary",)),
    )(*rows, *consts)
    return list(res)


def whole(fn, ins, outs, name="whole"):
    n_i = len(ins)

    def body(*refs):
        vals = fn(*[r[...] for r in refs[:n_i]])
        for ref, val in zip(refs[n_i:], vals):
            ref[...] = val.astype(ref.dtype)

    res = pl.pallas_call(
        body, out_shape=[jax.ShapeDtypeStruct(tuple(s), dt) for s, dt in outs],
        in_specs=[pl.BlockSpec(memory_space=pltpu.VMEM)] * n_i,
        out_specs=[pl.BlockSpec(memory_space=pltpu.VMEM)] * len(outs),
        name=name, compiler_params=_params(),
    )(*ins)
    return list(res)


_DOT_DIMS = {
    "nn": (((1,), (0,)), ((), ())),
    "nt": (((1,), (1,)), ((), ())),
    "tn": (((0,), (0,)), ((), ())),
}


def mm(a, b, mode="nn", out_dtype=F32, name="mm"):
    if mode == "nn":
        (M, K), (K2, N) = a.shape, b.shape
    elif mode == "nt":
        (M, K), (N, K2) = a.shape, b.shape
    else:
        (K, M), (K2, N) = a.shape, b.shape
    assert K == K2, (mode, a.shape, b.shape)
    tn = _tile(N, MM_TILE_N, LANES)
    if mode == "tn":
        tm, tk = _tile(M, MM_TILE_M, LANES), _tile(K, MM_TILE_M, 2 * SUBLANES)
    else:
        tm, tk = _tile(M, MM_TILE_M, 2 * SUBLANES), _tile(K, MM_TILE_N, LANES)

    def vmem_bytes(tm_, tk_):
        return (2 * (tm_ * tk_ * a.dtype.itemsize + tk_ * tn * b.dtype.itemsize
                     + tm_ * tn * jnp.dtype(out_dtype).itemsize) + tm_ * tn * 4)

    while vmem_bytes(tm, tk) > MM_VMEM_BYTES and tk % (2 * LANES) == 0 and K % (tk // 2) == 0:
        tk //= 2
    while vmem_bytes(tm, tk) > MM_VMEM_BYTES and tm % (2 * LANES) == 0 and M % (tm // 2) == 0:
        tm //= 2
    if mode == "tn":
        a_spec = pl.BlockSpec((tk, tm), lambda i, j, k: (k, i))
    else:
        a_spec = pl.BlockSpec((tm, tk), lambda i, j, k: (i, k))
    if mode == "nt":
        b_spec = pl.BlockSpec((tn, tk), lambda i, j, k: (j, k))
    else:
        b_spec = pl.BlockSpec((tk, tn), lambda i, j, k: (k, j))
    nk = K // tk
    dims = _DOT_DIMS[mode]

    def dot(a_ref, b_ref):
        return lax.dot_general(a_ref[...].astype(MXU_DTYPE), b_ref[...].astype(MXU_DTYPE), dims,
                               preferred_element_type=F32)

    def body_one(a_ref, b_ref, o_ref):
        o_ref[...] = dot(a_ref, b_ref).astype(o_ref.dtype)

    def body_acc(a_ref, b_ref, o_ref, acc_ref):
        k = pl.program_id(2)

        @pl.when(k == 0)
        def _():
            acc_ref[...] = dot(a_ref, b_ref)

        @pl.when(k > 0)
        def _():
            acc_ref[...] += dot(a_ref, b_ref)

        @pl.when(k == nk - 1)
        def _():
            o_ref[...] = acc_ref[...].astype(o_ref.dtype)

    return pl.pallas_call(
        body_one if nk == 1 else body_acc, grid=(M // tm, N // tn, nk), in_specs=[a_spec, b_spec],
        out_specs=pl.BlockSpec((tm, tn), lambda i, j, k: (i, j)),
        out_shape=jax.ShapeDtypeStruct((M, N), out_dtype),
        scratch_shapes=[] if nk == 1 else [pltpu.VMEM((tm, tn), F32)],
        name=name, compiler_params=_params(("parallel", "parallel", "arbitrary")),
    )(a, b)


def _mm_rows_tiles(M, K):
    return _tile(M, MM_TILE_M // 2, 2 * SUBLANES), _tile(K, MM_TILE_N, LANES)


def mm_add_norm(a, b, x, gain, name):
    (M, K), N = a.shape, b.shape[1]
    tm, tk = _mm_rows_tiles(M, K)
    nk = K // tk

    def body(a_ref, b_ref, x_ref, g_ref, xo_ref, n_ref, acc_ref):
        k = pl.program_id(1)
        r = _dotf(a_ref[...], b_ref[...])

        @pl.when(k == 0)
        def _():
            acc_ref[...] = r

        @pl.when(k > 0)
        def _():
            acc_ref[...] += r

        @pl.when(k == nk - 1)
        def _():
            xn = x_ref[...] + acc_ref[...]
            xo_ref[...] = xn
            n_ref[...] = _rms(xn, g_ref[...]).astype(n_ref.dtype)

    row = pl.BlockSpec((tm, N), lambda i, k: (i, 0))
    return pl.pallas_call(
        body, grid=(M // tm, nk),
        in_specs=[pl.BlockSpec((tm, tk), lambda i, k: (i, k)), pl.BlockSpec((tk, N), lambda i, k: (k, 0)), row,
                  pl.BlockSpec((1, N), lambda i, k: (0, 0))],
        out_specs=[row, row], out_shape=[jax.ShapeDtypeStruct((M, N), F32), jax.ShapeDtypeStruct((M, N), BF16)],
        scratch_shapes=[pltpu.VMEM((tm, N), F32)], name=name, compiler_params=_params(("parallel", "arbitrary")),
    )(a, b, x, gain)


def mm_norm_bwd(a, b, x, gain, dx_res, name):
    (M, K), N = a.shape, b.shape[0]
    tm, tk = _mm_rows_tiles(M, K)
    nk = K // tk

    def body(a_ref, b_ref, x_ref, g_ref, r_ref, dx_ref, dgain_ref, acc_ref):
        i, k = pl.program_id(0), pl.program_id(1)
        r = _dotf(a_ref[...], b_ref[...], "nt")

        @pl.when((i == 0) & (k == 0))
        def _():
            dgain_ref[...] = jnp.zeros(dgain_ref.shape, F32)

        @pl.when(k == 0)
        def _():
            acc_ref[...] = r

        @pl.when(k > 0)
        def _():
            acc_ref[...] += r

        @pl.when(k == nk - 1)
        def _():
            dx, dgain = _vjp(_rms, (x_ref[...], g_ref[...]), acc_ref[...])
            dx_ref[...] = r_ref[...] + dx
            dgain_ref[...] += dgain

    row = pl.BlockSpec((tm, N), lambda i, k: (i, 0))
    vec = pl.BlockSpec((1, N), lambda i, k: (0, 0))
    return pl.pallas_call(
        body, grid=(M // tm, nk),
        in_specs=[pl.BlockSpec((tm, tk), lambda i, k: (i, k)), pl.BlockSpec((N, tk), lambda i, k: (0, k)), row, vec, row],
        out_specs=[row, vec], out_shape=[jax.ShapeDtypeStruct((M, N), F32), jax.ShapeDtypeStruct((1, N), F32)],
        scratch_shapes=[pltpu.VMEM((tm, N), F32)], name=name, compiler_params=_params(("arbitrary", "arbitrary")),
    )(a, b, x, gain, dx_res)


def _roll_rows(x, d, reverse):
    return pltpu.roll(x, (SUBLANES - d) if reverse else d, 0)


def scan_real(a, b, reverse=False, name="scan_real"):
    S, W = b.shape
    cw = _pick(W, (256, 128))
    n_tiles = S // SUBLANES

    def body(a_ref, b_ref, o_ref):
        row = lax.broadcasted_iota(jnp.int32, (SUBLANES, cw), 0)
        edge = 0 if reverse else SUBLANES - 1

        def step(i, carry):
            t = (n_tiles - 1 - i) if reverse else i
            off = pl.multiple_of(t * SUBLANES, SUBLANES)
            A = a_ref[pl.ds(off, SUBLANES), :]
            B = b_ref[pl.ds(off, SUBLANES), :]
            for d in (1, 2, 4):
                m = (row < SUBLANES - d) if reverse else (row >= d)
                B = jnp.where(m, A * _roll_rows(B, d, reverse) + B, B)
                A = jnp.where(m, A * _roll_rows(A, d, reverse), A)
            o_ref[pl.ds(off, SUBLANES), :] = B + A * carry
            at_edge = row == edge
            return (jnp.sum(jnp.where(at_edge, B, 0.0), axis=0, keepdims=True)
                    + jnp.sum(jnp.where(at_edge, A, 0.0), axis=0, keepdims=True) * carry)

        lax.fori_loop(0, n_tiles, step, jnp.zeros((1, cw), F32), unroll=2)

    spec = pl.BlockSpec((S, cw), lambda j: (0, j))
    return pl.pallas_call(
        body, grid=(W // cw,), in_specs=[spec, spec], out_specs=spec,
        out_shape=jax.ShapeDtypeStruct((S, W), F32), name=name, compiler_params=_params(("parallel",)),
    )(a, b)


def scan_cplx(lam, bu, reverse=False, name="scan_cplx"):
    S, C = bu.shape
    half = LANES
    CB = _pick(C, (1024, 512, 256))
    TS = _pick(S, (1024, 512, 256, 128, 64, 32, 16, 8))
    groups = CB // (2 * half)
    n_blocks, n_tiles = S // TS, TS // SUBLANES

    def cmul(ar, ai, br, bi):
        return ar * br - ai * bi, ar * bi + ai * br

    def body(lam_ref, bu_ref, o_ref, carry_ref):
        row = lax.broadcasted_iota(jnp.int32, (SUBLANES, half), 0)

        def edge_row(v):
            return jnp.sum(jnp.where(row == (0 if reverse else SUBLANES - 1), v, 0.0), axis=0, keepdims=True)

        @pl.when(pl.program_id(1) == 0)
        def _():
            carry_ref[...] = jnp.zeros(carry_ref.shape, F32)

        consts = []
        for g in range(groups):
            lr = lam_ref[:, 2 * half * g:2 * half * g + half]
            li = lam_ref[:, 2 * half * g + half:2 * half * (g + 1)]
            if reverse:
                li = -li
            l1 = (lr, li)
            l2 = cmul(*l1, *l1)
            l4 = cmul(*l2, *l2)
            pr = jnp.zeros((SUBLANES, half), F32)
            pi = jnp.zeros((SUBLANES, half), F32)
            p = l1
            for r in range(SUBLANES):
                sel = row == ((SUBLANES - 1 - r) if reverse else r)
                pr = jnp.where(sel, p[0], pr)
                pi = jnp.where(sel, p[1], pi)
                p = cmul(*p, *l1)
            consts.append((l1, l2, l4, pr, pi, edge_row(pr), edge_row(pi)))

        def step(i, carry):
            t = (n_tiles - 1 - i) if reverse else i
            off = pl.multiple_of(t * SUBLANES, SUBLANES)
            out = []
            for g in range(groups):
                l1, l2, l4, pr, pi, p8r, p8i = consts[g]
                cr, ci = carry[2 * g], carry[2 * g + 1]
                re, im = pl.ds(2 * half * g, half), pl.ds(2 * half * g + half, half)
                Br = bu_ref[pl.ds(off, SUBLANES), re]
                Bi = bu_ref[pl.ds(off, SUBLANES), im]
                for d, (qr, qi) in ((1, l1), (2, l2), (4, l4)):
                    m = (row < SUBLANES - d) if reverse else (row >= d)
                    sr, si = _roll_rows(Br, d, reverse), _roll_rows(Bi, d, reverse)
                    nr = jnp.where(m, Br + qr * sr - qi * si, Br)
                    ni = jnp.where(m, Bi + qr * si + qi * sr, Bi)
                    Br, Bi = nr, ni
                o_ref[pl.ds(off, SUBLANES), re] = Br + pr * cr - pi * ci
                o_ref[pl.ds(off, SUBLANES), im] = Bi + pr * ci + pi * cr
                er, ei = edge_row(Br), edge_row(Bi)
                out += [er + p8r * cr - p8i * ci, ei + p8r * ci + p8i * cr]
            return tuple(out)

        carry0 = tuple(carry_ref[:, pl.ds(half * k, half)] for k in range(2 * groups))
        carry1 = lax.fori_loop(0, n_tiles, step, carry0, unroll=2)
        for k in range(2 * groups):
            carry_ref[:, pl.ds(half * k, half)] = carry1[k]

    def rows(j, t):
        return ((n_blocks - 1 - t) if reverse else t, j)

    spec = pl.BlockSpec((TS, CB), rows)
    return pl.pallas_call(
        body, grid=(C // CB, n_blocks), in_specs=[pl.BlockSpec((1, CB), lambda j, t: (0, j)), spec],
        out_specs=spec, out_shape=jax.ShapeDtypeStruct((S, C), F32), scratch_shapes=[pltpu.VMEM((1, CB), F32)],
        name=name, compiler_params=_params(("parallel", "arbitrary")),
    )(lam, bu)


ATTN_HEADS_PER_STEP = 2


def _attn_tile(S, window):
    if window is None:
        return _pick(S, (512, 256, 128))
    return max(window, _pick(S, (256, 128)))


def _attn_valid(q_blk, k_blk, T, window):
    kpos = k_blk * T + lax.broadcasted_iota(jnp.int32, (T, T), 0)
    qpos = q_blk * T + lax.broadcasted_iota(jnp.int32, (T, T), 1)
    valid = kpos <= qpos
    if window is not None:
        valid = valid & (qpos - kpos < window)
    return valid


def attn_fwd(q, k, v, sink, cq=None, ck=None, window=None, name="attn_fwd"):
    H, S, Dh = q.shape
    G = H // k.shape[0]
    HP = ATTN_HEADS_PER_STEP
    assert H % HP == 0 and (G == 1 or G % HP == 0)
    KP = HP if G == 1 else 1
    T = _attn_tile(S, window)
    nq = S // T
    nks = nq if window is None else 2
    scale = Dh ** -0.5
    bias = cq is not None

    def kv_block(i, j):
        return jnp.minimum(j, i) if window is None else jnp.maximum(i - 1 + j, 0)

    def body(*refs):
        if bias:
            q_ref, k_ref, v_ref, s_ref, cq_ref, ck_ref, o_ref, lse_ref, m_scr, l_scr, acc_scr = refs
        else:
            q_ref, k_ref, v_ref, s_ref, o_ref, lse_ref, m_scr, l_scr, acc_scr = refs
        i, j = pl.program_id(1), pl.program_id(2)

        @pl.when(j == 0)
        def _():
            m_scr[...] = jnp.zeros(m_scr.shape, F32) + s_ref[...]
            l_scr[...] = jnp.ones(l_scr.shape, F32)
            acc_scr[...] = jnp.zeros(acc_scr.shape, F32)

        def block(masked):
            valid = _attn_valid(i, kv_block(i, j), T, window) if masked else None
            for b in range(HP):
                kvb = b if G == 1 else 0
                s = _dotf(k_ref[kvb], q_ref[b], "nt") * scale
                if bias:
                    s = s + cq_ref[b] - ck_ref[b]
                if masked:
                    s = jnp.where(valid, s, NEG)
                m_old = m_scr[b]
                m_new = jnp.maximum(m_old, jnp.max(s, axis=0, keepdims=True))
                alpha = jnp.exp(m_old - m_new)
                p = jnp.exp(s - m_new)
                l_scr[b] = alpha * l_scr[b] + jnp.sum(p, axis=0, keepdims=True)
                acc_scr[b] = alpha * acc_scr[b] + _dotf(v_ref[kvb], p, "tn")
                m_scr[b] = m_new

        if window is None:
            pl.when(j < i)(lambda: block(False))
            pl.when(j == i)(lambda: block(True))
        else:
            pl.when(i - 1 + j >= 0)(lambda: block(True))

        @pl.when(j == nks - 1)
        def _():
            o_ref[...] = acc_scr[...] / l_scr[...]
            lse_ref[...] = m_scr[...] + jnp.log(l_scr[...])

    def kv_map(hp, i, j):
        return (hp if G == 1 else (hp * HP) // G, kv_block(i, j), 0)

    in_specs = [
        pl.BlockSpec((HP, T, Dh), lambda hp, i, j: (hp, i, 0)),
        pl.BlockSpec((KP, T, Dh), kv_map),
        pl.BlockSpec((KP, T, Dh), kv_map),
        pl.BlockSpec((HP, 1, 1), lambda hp, i, j: (hp, 0, 0)),
    ]
    args = [q, k, v, sink]
    if bias:
        in_specs += [pl.BlockSpec((HP, 1, T), lambda hp, i, j: (hp, 0, i)),
                     pl.BlockSpec((HP, T, 1), lambda hp, i, j: (hp, kv_block(i, j), 0))]
        args += [cq, ck]
    return pl.pallas_call(
        body, grid=(H // HP, nq, nks), in_specs=in_specs,
        out_specs=[pl.BlockSpec((HP, Dh, T), lambda hp, i, j: (hp, 0, i)),
                   pl.BlockSpec((HP, 1, T), lambda hp, i, j: (hp, 0, i))],
        out_shape=[jax.ShapeDtypeStruct((H, Dh, S), F32), jax.ShapeDtypeStruct((H, 1, S), F32)],
        scratch_shapes=[pltpu.VMEM((HP, 1, T), F32), pltpu.VMEM((HP, 1, T), F32), pltpu.VMEM((HP, Dh, T), F32)],
        name=name, compiler_params=_params(("parallel", "parallel", "arbitrary")),
    )(*args)


def attn_bwd(q, k, v, lse, do, delta, cq=None, ck=None, window=None, name="attn_bwd"):
    H, S, Dh = q.shape
    KVH = k.shape[0]
    G = H // KVH
    HP = ATTN_HEADS_PER_STEP
    assert H % HP == 0 and (G == 1 or G % HP == 0)
    pair_kv = G == 1
    KP = HP if pair_kv else 1
    T = _attn_tile(S, window)
    nq = S // T
    nqs = nq if window is None else 2
    scale = Dh ** -0.5
    bias = cq is not None
    assert not bias or G == 1

    def q_block(kb, j):
        return jnp.maximum(j, kb) if window is None else jnp.minimum(kb + j, nq - 1)

    def body(*refs):
        if bias:
            (q_ref, k_ref, v_ref, lse_ref, do_ref, dl_ref, cq_ref, ck_ref,
             dq_ref, dk_ref, dv_ref, dcq_ref, dck_ref) = refs
        else:
            q_ref, k_ref, v_ref, lse_ref, do_ref, dl_ref, dq_ref, dk_ref, dv_ref = refs
        kb, gp, j = pl.program_id(1), pl.program_id(2), pl.program_id(3)

        @pl.when((gp == 0) & (j == 0))
        def _():
            dk_ref[...] = jnp.zeros(dk_ref.shape, F32)
            dv_ref[...] = jnp.zeros(dv_ref.shape, F32)
            if bias:
                dck_ref[...] = jnp.zeros(dck_ref.shape, F32)

        @pl.when((kb == 0) & (gp == 0) & (j == 0))
        def _():
            dq_ref[...] = jnp.zeros(dq_ref.shape, F32)
            if bias:
                dcq_ref[...] = jnp.zeros(dcq_ref.shape, F32)

        def block(masked):
            qi = q_block(kb, j)
            off = pl.multiple_of(qi * T, T)
            valid = _attn_valid(qi, kb, T, window) if masked else None
            for b in range(HP):
                kvb = b if pair_kv else 0
                g = 0 if pair_kv else gp * HP + b
                qb, kk, vv = q_ref[b].astype(MXU_DTYPE), k_ref[kvb].astype(MXU_DTYPE), v_ref[kvb].astype(MXU_DTYPE)
                dob = do_ref[b].astype(MXU_DTYPE)
                s = _dotf(kk, qb, "nt") * scale
                if bias:
                    s = s + cq_ref[b] - ck_ref[b]
                if masked:
                    s = jnp.where(valid, s, NEG)
                p = jnp.exp(s - lse_ref[b])
                dv_ref[kvb] += _dotf(p, dob, "nt")
                ds = p * (_dotf(vv, dob) - dl_ref[b])
                dsb = ds.astype(MXU_DTYPE)
                dk_ref[kvb] += scale * _dotf(dsb, qb)
                dq_ref[kvb, g, pl.ds(off, T), :] += scale * _dotf(dsb, kk, "tn")
                if bias:
                    dcq_ref[kvb, g, :, pl.ds(off, T)] += jnp.sum(ds, axis=0, keepdims=True)
                    dck_ref[kvb] -= jnp.sum(ds, axis=1, keepdims=True)

        if window is None:
            pl.when(j > kb)(lambda: block(False))
            pl.when(j == kb)(lambda: block(True))
        else:
            pl.when(kb + j <= nq - 1)(lambda: block(True))

    def qmap(kvp, kb, gp, j):
        return (kvp if pair_kv else (kvp * G) // HP + gp, q_block(kb, j), 0)

    def qmap_t(kvp, kb, gp, j):
        return (kvp if pair_kv else (kvp * G) // HP + gp, 0, q_block(kb, j))

    in_specs = [
        pl.BlockSpec((HP, T, Dh), qmap),
        pl.BlockSpec((KP, T, Dh), lambda kvp, kb, gp, j: (kvp, kb, 0)),
        pl.BlockSpec((KP, T, Dh), lambda kvp, kb, gp, j: (kvp, kb, 0)),
        pl.BlockSpec((HP, 1, T), qmap_t),
        pl.BlockSpec((HP, Dh, T), qmap_t),
        pl.BlockSpec((HP, 1, T), qmap_t),
    ]
    args = [q, k, v, lse, do, delta]
    out_specs = [
        pl.BlockSpec((KP, G, S, Dh), lambda kvp, kb, gp, j: (kvp, 0, 0, 0)),
        pl.BlockSpec((KP, T, Dh), lambda kvp, kb, gp, j: (kvp, kb, 0)),
        pl.BlockSpec((KP, T, Dh), lambda kvp, kb, gp, j: (kvp, kb, 0)),
    ]
    out_shape = [jax.ShapeDtypeStruct((KVH, G, S, Dh), F32), jax.ShapeDtypeStruct((KVH, S, Dh), F32),
                 jax.ShapeDtypeStruct((KVH, S, Dh), F32)]
    if bias:
        in_specs += [pl.BlockSpec((HP, 1, T), qmap_t),
                     pl.BlockSpec((HP, T, 1), lambda kvp, kb, gp, j: (kvp, kb, 0))]
        args += [cq, ck]
        out_specs += [pl.BlockSpec((KP, G, 1, S), lambda kvp, kb, gp, j: (kvp, 0, 0, 0)),
                      pl.BlockSpec((KP, T, 1), lambda kvp, kb, gp, j: (kvp, kb, 0))]
        out_shape += [jax.ShapeDtypeStruct((KVH, G, 1, S), F32), jax.ShapeDtypeStruct((KVH, S, 1), F32)]
    res = pl.pallas_call(
        body, grid=(KVH // KP, nq, 1 if pair_kv else G // HP, nqs), in_specs=in_specs, out_specs=out_specs,
        out_shape=out_shape, name=name, compiler_params=_params(("arbitrary", "arbitrary", "arbitrary", "arbitrary")),
    )(*args)
    dq = res[0].reshape(H, S, Dh)
    if bias:
        return dq, res[1], res[2], res[3].reshape(H, 1, S), res[4]
    return dq, res[1], res[2]


def _rms(x, g):
    return x * lax.rsqrt(jnp.mean(x * x, axis=-1, keepdims=True) + EPS) * g


def _sigmoid(x):
    return 1.0 / (1.0 + jnp.exp(-x))


def _softplus(x):
    return jnp.maximum(x, 0.0) + jnp.log(1.0 + jnp.exp(-jnp.abs(x)))


def _log_sigmoid(x):
    return jnp.minimum(x, 0.0) - jnp.log(1.0 + jnp.exp(-jnp.abs(x)))


def _gelu(x):
    return 0.5 * x * (1.0 + jnp.tanh(math.sqrt(2.0 / math.pi) * (x + 0.044715 * (x * x * x))))


def _silu(x):
    return x * _sigmoid(x)


def _ffn_act(gu):
    f = gu.shape[1] // 2
    return MACARON * _silu(gu[:, :f]) * gu[:, f:]


def _qk_prep(rope):
    def f(x, *rest):
        if rope:
            cos, sin, g, rot = rest
        else:
            (g,) = rest
        y = _rms(x, g)
        if rope:
            y = y * cos + jnp.dot(y, rot, precision=HI, preferred_element_type=F32) * sin
        return y
    return f


def _lru_gates(pre, xc, ba, bx, lam):
    w = xc.shape[1]
    r = _sigmoid(pre[:, :w] + ba)
    i = _sigmoid(pre[:, w:] + bx)
    log_a = -LRU_C * r * _softplus(lam)
    a = jnp.exp(log_a)
    b = jnp.sqrt(1.0 - jnp.exp(2.0 * log_a)) * (i * xc)
    return a, b


def _lru_conv(x0, x1, x2, x3, w0, w1, w2, w3, cb):
    return cb + x0 * w0 + x1 * w1 + x2 * w2 + x3 * w3


def _s5_params(lre, lim, ldt, gsel, bre, bim):
    dt = jnp.sum(gsel * jnp.exp(ldt), axis=1, keepdims=True)
    er = jnp.exp(lre * dt)
    ang = lim * dt
    lbr, lbi = er * jnp.cos(ang), er * jnp.sin(ang)
    nr, ni = lbr - 1.0, lbi
    den = lre * lre + lim * lim
    fr, fi = (nr * lre + ni * lim) / den, (ni * lre - nr * lim) / den
    return lbr, lbi, fr * bre - fi * bim, fr * bim + fi * bre


def _s5_out(yssm, u, d):
    return _gelu(yssm + d * u)


def _glu(z, gl, gb):
    return z * _sigmoid(gl + gb)


def _ple_out(x, gpre, epre, pn):
    return x + _sigmoid(gpre) * _rms(epre, pn)


def _vjp(fn, args, cots):
    _, pull = jax.vjp(fn, *args)
    return pull(cots)


def add_norm(x, y, g, name):
    D = x.shape[1]
    if y is None:
        return x, rowwise(lambda xv, gv: ([_rms(xv, gv)], []), [x], [g], outs=[(D, BF16)], name=name)[0]
    xn, n = rowwise(lambda xv, yv, gv: ([xv + yv, _rms(xv + yv, gv)], []), [x, y], [g],
                    outs=[(D, F32), (D, BF16)], name=name)
    return xn, n


def norm_bwd(x, g, dn, dx_res, name):
    D = x.shape[1]

    def f(xv, dnv, dxv, gv):
        dx, dg = _vjp(_rms, (xv, gv), dnv)
        return [dxv + dx], [dg]

    return rowwise(f, [x, dn, dx_res], [g], outs=[(D, F32)], accs=[(1, D)], name=name)


def _swiglu(g, u):
    return MACARON * _silu(g) * u


def _dotf(a, b, mode="nn"):
    return lax.dot_general(a.astype(MXU_DTYPE), b.astype(MXU_DTYPE), _DOT_DIMS[mode], preferred_element_type=F32)


def ffn_up(n, wgu, ig, iu, name):
    S, D = n.shape
    C, _, _, Fc = wgu.shape
    tm = _tile(S, MM_TILE_M, 2 * SUBLANES)

    def body(n_ref, wg_ref, wu_ref, g_ref, u_ref, a_ref):
        g = _dotf(n_ref[...], wg_ref[...])
        u = _dotf(n_ref[...], wu_ref[...])
        g_ref[...] = g.astype(g_ref.dtype)
        u_ref[...] = u.astype(u_ref.dtype)
        a_ref[...] = _swiglu(g, u).astype(a_ref.dtype)

    hid = pl.BlockSpec((None, tm, Fc), lambda s, i: (s, i, 0))
    return pl.pallas_call(
        body, grid=(C, S // tm),
        in_specs=[pl.BlockSpec((tm, D), lambda s, i: (i, 0)),
                  pl.BlockSpec((None, None, D, Fc), lambda s, i: (s, ig, 0, 0)),
                  pl.BlockSpec((None, None, D, Fc), lambda s, i: (s, iu, 0, 0))],
        out_specs=[hid, hid, hid], out_shape=[jax.ShapeDtypeStruct((C, S, Fc), BF16)] * 3,
        name=name, compiler_params=_params(("parallel", "parallel")),
    )(n, wgu, wgu)


def ffn_down(act, wd, iw, x, gain, name):
    C, S, Fc = act.shape
    D = wd.shape[-1]
    tm = _tile(S, MM_TILE_M, 2 * SUBLANES)

    def body(a_ref, w_ref, x_ref, g_ref, xo_ref, n_ref, acc_ref):
        s = pl.program_id(1)
        r = _dotf(a_ref[...], w_ref[...])

        @pl.when(s == 0)
        def _():
            acc_ref[...] = r

        @pl.when(s > 0)
        def _():
            acc_ref[...] += r

        @pl.when(s == C - 1)
        def _():
            xn = x_ref[...] + acc_ref[...]
            xo_ref[...] = xn
            n_ref[...] = _rms(xn, g_ref[...]).astype(n_ref.dtype)

    row = pl.BlockSpec((tm, D), lambda i, s: (i, 0))
    return pl.pallas_call(
        body, grid=(S // tm, C),
        in_specs=[pl.BlockSpec((None, tm, Fc), lambda i, s: (s, i, 0)),
                  pl.BlockSpec((None, None, Fc, D), lambda i, s: (s, iw, 0, 0)), row,
                  pl.BlockSpec((1, D), lambda i, s: (0, 0))],
        out_specs=[row, row], out_shape=[jax.ShapeDtypeStruct((S, D), F32), jax.ShapeDtypeStruct((S, D), BF16)],
        scratch_shapes=[pltpu.VMEM((tm, D), F32)], name=name, compiler_params=_params(("parallel", "arbitrary")),
    )(act, wd, x, gain)


def ffn_down_bwd(dy, wd, iw, g, u, name):
    C, S, Fc = g.shape
    D = dy.shape[1]
    tm = _tile(S, MM_TILE_M, 2 * SUBLANES)

    def body(dy_ref, w_ref, g_ref, u_ref, dg_ref, du_ref):
        dact = MACARON * _dotf(dy_ref[...], w_ref[...], "nt")
        g, u = g_ref[...].astype(F32), u_ref[...].astype(F32)
        sg = _sigmoid(g)
        gs = g * sg
        dg_ref[...] = (dact * u * (sg + gs * (1.0 - sg))).astype(dg_ref.dtype)
        du_ref[...] = (dact * gs).astype(du_ref.dtype)

    hid = pl.BlockSpec((None, tm, Fc), lambda s, i: (s, i, 0))
    return pl.pallas_call(
        body, grid=(C, S // tm),
        in_specs=[pl.BlockSpec((tm, D), lambda s, i: (i, 0)),
                  pl.BlockSpec((None, None, Fc, D), lambda s, i: (s, iw, 0, 0)), hid, hid],
        out_specs=[hid, hid], out_shape=[jax.ShapeDtypeStruct((C, S, Fc), BF16)] * 2,
        name=name, compiler_params=_params(("parallel", "parallel")),
    )(dy, wd, g, u)


def ffn_dn(dg, du, wgu, ig, iu, x, gain, dx_res, name):
    C, S, Fc = dg.shape
    D = wgu.shape[2]
    tm = _tile(S, MM_TILE_M, 2 * SUBLANES)
    parts = 2 if tm % (4 * SUBLANES) == 0 else 1

    def body(dg_ref, du_ref, wg_ref, wu_ref, x_ref, g_ref, r_ref, dx_ref, dgain_ref, acc_ref):
        i, s = pl.program_id(0), pl.program_id(1)
        r = _dotf(dg_ref[...], wg_ref[...], "nt") + _dotf(du_ref[...], wu_ref[...], "nt")

        @pl.when((i == 0) & (s == 0))
        def _():
            dgain_ref[...] = jnp.zeros(dgain_ref.shape, F32)

        @pl.when(s == 0)
        def _():
            acc_ref[...] = r

        @pl.when(s > 0)
        def _():
            acc_ref[...] += r

        @pl.when(s == C - 1)
        def _():
            for part in range(parts):
                rows = pl.ds(part * (tm // parts), tm // parts)
                dx, dgain = _vjp(_rms, (x_ref[rows, :], g_ref[...]), acc_ref[rows, :])
                dx_ref[rows, :] = r_ref[rows, :] + dx
                dgain_ref[...] += dgain

    hid = pl.BlockSpec((None, tm, Fc), lambda i, s: (s, i, 0))
    row = pl.BlockSpec((tm, D), lambda i, s: (i, 0))
    vec = pl.BlockSpec((1, D), lambda i, s: (0, 0))
    return pl.pallas_call(
        body, grid=(S // tm, C),
        in_specs=[hid, hid, pl.BlockSpec((None, None, D, Fc), lambda i, s: (s, ig, 0, 0)),
                  pl.BlockSpec((None, None, D, Fc), lambda i, s: (s, iu, 0, 0)), row, vec, row],
        out_specs=[row, vec], out_shape=[jax.ShapeDtypeStruct((S, D), F32), jax.ShapeDtypeStruct((1, D), F32)],
        scratch_shapes=[pltpu.VMEM((tm, D), F32)], name=name, compiler_params=_params(("arbitrary", "arbitrary")),
    )(dg, du, wgu, wgu, x, gain, dx_res)


def ffn_dw(a, d, buf, idx, shape, blocked, name):
    C, P, M, N = shape
    S = d.shape[-2]
    tk = _tile(S, MM_TILE_M, 2 * SUBLANES)
    nk = S // tk

    def body(*refs):
        a_ref, d_ref, o_ref, acc_ref = refs[0], refs[1], refs[-2], refs[-1]
        k = pl.program_id(1)
        r = _dotf(a_ref[...], d_ref[...], "tn")

        @pl.when(k == 0)
        def _():
            acc_ref[...] = r

        @pl.when(k > 0)
        def _():
            acc_ref[...] += r

        @pl.when(k == nk - 1)
        def _():
            o_ref[...] = acc_ref[...].astype(o_ref.dtype)

    if blocked == "a":
        a_spec = pl.BlockSpec((None, tk, M), lambda s, k: (s, k, 0))
        d_spec = pl.BlockSpec((tk, N), lambda s, k: (k, 0))
    else:
        a_spec = pl.BlockSpec((tk, M), lambda s, k: (k, 0))
        d_spec = pl.BlockSpec((None, tk, N), lambda s, k: (s, k, 0))
    out_spec = pl.BlockSpec((None, None, M, N), lambda s, k: (s, idx, 0, 0))
    out_shape = jax.ShapeDtypeStruct(tuple(shape), BF16)
    scratch = [pltpu.VMEM((M, N), F32)]
    if buf is None:
        return pl.pallas_call(body, grid=(C, nk), in_specs=[a_spec, d_spec], out_specs=out_spec, out_shape=out_shape,
                              scratch_shapes=scratch, name=name,
                              compiler_params=_params(("parallel", "arbitrary")))(a, d)
    return pl.pallas_call(body, grid=(C, nk), in_specs=[a_spec, d_spec, pl.BlockSpec(memory_space=pl.ANY)],
                          out_specs=out_spec, out_shape=out_shape, input_output_aliases={2: 0},
                          scratch_shapes=scratch, name=name,
                          compiler_params=_params(("parallel", "arbitrary")))(a, d, buf)


def ffn_fwd(n, x, gain, wgu, wd, ig, iu, iw, tag):
    g, u, act = ffn_up(n, wgu, ig, iu, f"ffn_up_{tag}")
    x_new, n_new = ffn_down(act, wd, iw, x, gain, f"ffn_down_{tag}")
    return x_new, n_new, (n, g, u, act)


def ffn_bwd(dy, saved, x, gain, wgu, wd, ig, iu, iw, gbuf, tag):
    n, g, u, act = saved
    gwgu, gwd = gbuf
    dg, du = ffn_down_bwd(dy, wd, iw, g, u, f"ffn_down_bwd_{tag}")
    gwd = ffn_dw(act, dy, gwd, iw, (N_CHIPS,) + wd.shape[1:], "a", f"ffn_dwd_{tag}")
    dx, dgain = ffn_dn(dg, du, wgu, ig, iu, x, gain, dy, f"ffn_dn_{tag}")
    gwgu = ffn_dw(n, dg, gwgu, ig, (N_CHIPS,) + wgu.shape[1:], "d", f"ffn_dwg_{tag}")
    gwgu = ffn_dw(n, du, gwgu, iu, (N_CHIPS,) + wgu.shape[1:], "d", f"ffn_dwu_{tag}")
    return dx, dgain, (gwgu, gwd)


def _heads(x, H):
    S = x.shape[0]
    return x.reshape(S, H, HEAD_DIM).transpose(1, 0, 2)


def _unheads(x):
    H, S, _ = x.shape
    return x.transpose(1, 0, 2).reshape(S, H * HEAD_DIM)


def _heads_t(x, H):
    return x.T.reshape(H, HEAD_DIM, x.shape[0])


def _unheads_t(x):
    return x.reshape(x.shape[0] * x.shape[1], x.shape[2]).T


def _shift_down(x, n=1):
    return jnp.pad(x, ((n, 0), (0, 0)))[:x.shape[0]]


def _shift_up(x, n=1):
    return jnp.pad(x, ((0, n), (0, 0)))[n:]


def _block_diag(w):
    B, I, J = w.shape
    eye = jnp.eye(B, dtype=w.dtype)
    return (w[:, :, None, :] * eye[:, None, :, None]).reshape(B * I, B * J)


def _block_diag_take(x, B):
    I, J = x.shape[0] // B, x.shape[1] // B
    eye = jnp.eye(B, dtype=x.dtype)
    return jnp.sum(x.reshape(B, I, B, J) * eye[:, None, :, None], axis=2)


def _rope_tables(S):
    half = HEAD_DIM // 2
    inv = jnp.power(ROPE_THETA, -jnp.arange(half, dtype=F32) / half)
    ang = jnp.arange(S, dtype=F32)[:, None] * inv[None, :]
    cos = jnp.concatenate([jnp.cos(ang), jnp.cos(ang)], axis=1)
    sin = jnp.concatenate([jnp.sin(ang), jnp.sin(ang)], axis=1)
    r = jnp.arange(HEAD_DIM)[:, None]
    c = jnp.arange(HEAD_DIM)[None, :]
    rot = jnp.where(r == c + half, -1.0, 0.0) + jnp.where(c == r + half, 1.0, 0.0)
    return cos, sin, rot.astype(F32)


def qk_prep_fwd(x_hm, g, rope_tabs, name):
    H, S, Dh = x_hm.shape
    rows = [x_hm.reshape(H * S, Dh)]
    consts = [g.reshape(1, Dh)]
    periods = [None]
    if rope_tabs is not None:
        rows += [rope_tabs[0], rope_tabs[1]]
        consts += [rope_tabs[2]]
        periods += [S, S]
    fn = _qk_prep(rope_tabs is not None)
    y = rowwise(lambda *a: ([fn(*a)], []), rows, consts, outs=[(Dh, F32)], name=name, periods=periods)[0]
    return y.reshape(H, S, Dh)


def qk_prep_bwd(x_hm, g, rope_tabs, dy_hm, name):
    H, S, Dh = x_hm.shape
    rope = rope_tabs is not None
    rows = [x_hm.reshape(H * S, Dh), dy_hm.reshape(H * S, Dh)]
    consts = [g.reshape(1, Dh)]
    periods = [None, None]
    if rope:
        rows += [rope_tabs[0], rope_tabs[1]]
        consts += [rope_tabs[2]]
        periods += [S, S]
    fn = _qk_prep(rope)

    def f(xv, dyv, *rest):
        if rope:
            cos, sin, gv, rot = rest
            dx, dg = _vjp(lambda a, b: fn(a, cos, sin, b, rot), (xv, gv), dyv)
        else:
            (gv,) = rest
            dx, dg = _vjp(fn, (xv, gv), dyv)
        return [dx], [dg]

    dx, dg = rowwise(f, rows, consts, outs=[(Dh, F32)], accs=[(1, Dh)], name=name, periods=periods)
    return dx.reshape(H, S, Dh), dg.reshape(Dh)


def attn_delta(do_t, o_t, name):
    H, Dh, S = o_t.shape

    def body(a_ref, b_ref, o_ref):
        o_ref[...] = jnp.sum(a_ref[...] * b_ref[...], axis=0, keepdims=True)

    spec = pl.BlockSpec((None, Dh, S), lambda h: (h, 0, 0))
    return pl.pallas_call(
        body, grid=(H,), in_specs=[spec, spec], out_specs=pl.BlockSpec((None, 1, S), lambda h: (h, 0, 0)),
        out_shape=jax.ShapeDtypeStruct((H, 1, S), F32), name=name, compiler_params=_params(("parallel",)),
    )(do_t, o_t)


def even_mixer_fwd(h, w, tag):
    S = h.shape[0]
    W = 512
    H = 8
    z = mm(h, w["w_in"], name=f"ev_in_{tag}")
    xa, ya, q, k, v, f = (z[:, 0:512], z[:, 512:1024], z[:, 1024:1536], z[:, 1536:2048], z[:, 2048:2560],
                          z[:, 2560:2688])
    xs = [_shift_down(xa, LRU_CONV - 1 - tap) for tap in range(LRU_CONV)]
    taps = [w["conv_w"][tap][None] for tap in range(LRU_CONV)]
    xc = rowwise(lambda *a: ([_lru_conv(*a)], []), xs, taps + [w["conv_b"]], outs=[(W, F32)],
                 name=f"lru_conv_{tag}")[0]
    pre = mm(xc, w["w_ax"], name=f"lru_gates_mm_{tag}")
    a, b = rowwise(lambda p_, x_, ba, bx, lam: (list(_lru_gates(p_, x_, ba, bx, lam)), []), [pre, xc],
                   [w["ba"], w["bx"], w["lam"]], outs=[(W, F32), (W, F32)], name=f"lru_gates_{tag}")
    hs = scan_real(a, b, name=f"lru_scan_{tag}")
    a_out = rowwise(lambda y_, h_: ([_gelu(y_) * h_], []), [ya, hs], outs=[(W, F32)], name=f"lru_out_{tag}")[0]
    lf = rowwise(lambda f_, bf: ([_log_sigmoid(f_ + bf)], []), [f], [w["bf"]], outs=[(LANES, F32)],
                 name=f"fox_logf_{tag}")[0]
    c = scan_real(jnp.ones_like(lf), lf, name=f"fox_cumsum_{tag}")
    c_hm = c[:, :H].T
    q_hm, k_hm, v_hm = _heads(q, H), _heads(k, H), _heads(v, H)
    qn = qk_prep_fwd(q_hm, w["qn"], None, f"fox_qprep_{tag}")
    kn = qk_prep_fwd(k_hm, w["kn"], None, f"fox_kprep_{tag}")
    sink = jnp.full((H, 1, 1), NEG, F32)
    o_hm, lse = attn_fwd(qn, kn, v_hm, sink, c_hm[:, None, :], c_hm[:, :, None], name=f"fox_attn_{tag}")
    mo = jnp.concatenate([a_out, _unheads_t(o_hm)], axis=1).astype(BF16)
    saved = dict(h=h, xs=xs, xc=xc, pre=pre, a=a, hs=hs, ya=ya, f=f, c_hm=c_hm, q_hm=q_hm, k_hm=k_hm,
                 v_hm=v_hm, qn=qn, kn=kn, o_hm=o_hm, lse=lse, mo=mo)
    return mo, saved


def even_mixer_bwd(dy, sv, w, tag):
    W = 512
    H = 8
    S = dy.shape[0]
    g = {}
    dmo = mm(dy, w["w_out"], "nt", name=f"ev_dmo_{tag}")
    g["w_out"] = mm(sv["mo"], dy, "tn", name=f"ev_dwout_{tag}")
    da_out, do = dmo[:, :W], dmo[:, W:]
    do_hm = _heads_t(do, H)
    delta = attn_delta(do_hm, sv["o_hm"], f"fox_delta_{tag}")
    c_hm = sv["c_hm"]
    dqn, dkn, dv_hm, dcq, dck = attn_bwd(sv["qn"], sv["kn"], sv["v_hm"], sv["lse"], do_hm, delta,
                                          c_hm[:, None, :], c_hm[:, :, None], name=f"fox_attn_bwd_{tag}")
    dq_hm, g["qn"] = qk_prep_bwd(sv["q_hm"], w["qn"], None, dqn, f"fox_qprep_bwd_{tag}")
    dk_hm, g["kn"] = qk_prep_bwd(sv["k_hm"], w["kn"], None, dkn, f"fox_kprep_bwd_{tag}")
    dc = (dcq[:, 0, :] + dck[:, :, 0]).T
    dc = jnp.pad(dc, ((0, 0), (0, LANES - H)))
    dlf = scan_real(jnp.ones_like(dc), dc, reverse=True, name=f"fox_cumsum_bwd_{tag}")

    def f_logf(f_, d_, bf):
        df, dbf = _vjp(lambda a_, b_: _log_sigmoid(a_ + b_), (f_, bf), d_)
        return [df], [dbf]

    df, dbf = rowwise(f_logf, [sv["f"], dlf], [w["bf"]], outs=[(LANES, F32)], accs=[(1, LANES)],
                      name=f"fox_logf_bwd_{tag}")
    g["bf"] = dbf[0, :H]
    def f_out(y_, h_, d_):
        dyv, dhv = _vjp(lambda a_, b_: _gelu(a_) * b_, (y_, h_), d_)
        return [dyv, dhv], []

    dya, dhs = rowwise(f_out, [sv["ya"], sv["hs"], da_out], outs=[(W, F32), (W, F32)], name=f"lru_out_bwd_{tag}")
    gs = scan_real(_shift_up(sv["a"]), dhs, reverse=True, name=f"lru_scan_bwd_{tag}")

    def f_gates(p_, x_, g_, hp_, ba, bx, lam):
        dp, dx, dba, dbx, dlam = _vjp(_lru_gates, (p_, x_, ba, bx, lam), (g_ * hp_, g_))
        return [dp, dx], [dba, dbx, dlam]

    dpre, dxc, dba, dbx, dlam = rowwise(f_gates, [sv["pre"], sv["xc"], gs, _shift_down(sv["hs"])],
                                        [w["ba"], w["bx"], w["lam"]], outs=[(2 * W, BF16), (W, F32)],
                                        accs=[(1, W)] * 3, name=f"lru_gates_bwd_{tag}")
    g["ba"], g["bx"], g["lam"] = dba[0], dbx[0], dlam[0]
    dxc2 = mm(dpre, w["w_ax"], "nt", name=f"lru_gates_mm_dx_{tag}")
    g["w_ax"] = mm(sv["xc"], dpre, "tn", name=f"lru_gates_mm_dw_{tag}")

    def f_conv(d1, d2, x0, x1, x2, x3):
        d = d1 + d2
        return [d], [jnp.sum(d, axis=0, keepdims=True)] + [jnp.sum(d * xv, axis=0, keepdims=True)
                                                           for xv in (x0, x1, x2, x3)]

    dxc_t, dcb, dw0, dw1, dw2, dw3 = rowwise(f_conv, [dxc, dxc2] + sv["xs"], outs=[(W, F32)],
                                             accs=[(1, W)] * 5, name=f"lru_conv_bwd_{tag}")
    g["conv_b"] = dcb[0]
    g["conv_w"] = jnp.concatenate([dw0, dw1, dw2, dw3], axis=0)
    ds_ = [_shift_up(dxc_t, LRU_CONV - 1 - tap) for tap in range(LRU_CONV)]
    taps = [w["conv_w"][tap][None] for tap in range(LRU_CONV)]
    dxa = rowwise(lambda a, b, c, d, w0, w1, w2, w3: ([a * w0 + b * w1 + c * w2 + d * w3], []), ds_, taps,
                  outs=[(W, F32)], name=f"lru_conv_dx_{tag}")[0]
    dz = jnp.concatenate([dxa, dya, _unheads(dq_hm), _unheads(dk_hm), _unheads(dv_hm), df], axis=1).astype(BF16)
    g["w_in"] = mm(sv["h"], dz, "tn", name=f"ev_dwin_{tag}")
    return dz, g


def odd_mixer_fwd(h, w, tag):
    S = h.shape[0]
    H, KVH = 8, 2
    z = mm(h, w["w_in"], name=f"od_in_{tag}")
    q, k, v, u = z[:, 0:512], z[:, 512:640], z[:, 640:768], z[:, 768:1280]
    tabs = _rope_tables(S)
    q_hm, k_hm, v_hm = _heads(q, H), _heads(k, KVH), _heads(v, KVH)
    qn = qk_prep_fwd(q_hm, w["qn"], tabs, f"swa_qprep_{tag}")
    kn = qk_prep_fwd(k_hm, w["kn"], tabs, f"swa_kprep_{tag}")
    sink = w["sinks"].reshape(H, 1, 1)
    o_hm, lse = attn_fwd(qn, kn, v_hm, sink, window=SWA_WINDOW, name=f"swa_attn_{tag}")
    lam, bexp = w["s5_lam"], w["s5_bexp"]
    bu = mm(u, bexp, name=f"s5_bu_{tag}")
    hs = scan_cplx(lam, bu, name=f"s5_scan_{tag}")
    yssm = mm(hs, w["s5_cexp"], name=f"s5_y_{tag}")
    zz = rowwise(lambda y_, u_, d_: ([_s5_out(y_, u_, d_)], []), [yssm, u], [w["s5_d"]], outs=[(512, F32)],
                 name=f"s5_gelu_{tag}")[0]
    gl = mm(zz, w["glu_w"], name=f"s5_glu_mm_{tag}")
    d_out = rowwise(lambda z_, g_, b_: ([_glu(z_, g_, b_)], []), [zz, gl], [w["glu_b"]], outs=[(512, F32)],
                    name=f"s5_glu_{tag}")[0]
    mo = jnp.concatenate([_unheads_t(o_hm), d_out], axis=1).astype(BF16)
    saved = dict(h=h, q_hm=q_hm, k_hm=k_hm, v_hm=v_hm, qn=qn, kn=kn, o_hm=o_hm, lse=lse, u=u, hs=hs, yssm=yssm,
                 zz=zz, gl=gl, mo=mo, tabs=tabs)
    return mo, saved


def odd_mixer_bwd(dy, sv, w, tag):
    H, KVH = 8, 2
    g = {}
    dmo = mm(dy, w["w_out"], "nt", name=f"od_dmo_{tag}")
    g["w_out"] = mm(sv["mo"], dy, "tn", name=f"od_dwout_{tag}")
    do, dd = dmo[:, :512], dmo[:, 512:]
    do_hm = _heads_t(do, H)
    delta = attn_delta(do_hm, sv["o_hm"], f"swa_delta_{tag}")
    dqn, dkn, dv_hm = attn_bwd(sv["qn"], sv["kn"], sv["v_hm"], sv["lse"], do_hm, delta, window=SWA_WINDOW,
                               name=f"swa_attn_bwd_{tag}")
    dq_hm, g["qn"] = qk_prep_bwd(sv["q_hm"], w["qn"], sv["tabs"], dqn, f"swa_qprep_bwd_{tag}")
    dk_hm, g["kn"] = qk_prep_bwd(sv["k_hm"], w["kn"], sv["tabs"], dkn, f"swa_kprep_bwd_{tag}")
    lse_t, delta_t = sv["lse"][:, 0, :].T, delta[:, 0, :].T
    g["sinks"] = rowwise(lambda l_, d_, s_: ([], [jnp.sum(-jnp.exp(s_ - l_) * d_, axis=0, keepdims=True)]),
                         [lse_t, delta_t], [w["sinks"].reshape(1, H)], accs=[(1, H)], name=f"swa_dsink_{tag}")[0][0]
    def f_glu(z_, g_, d_, b_):
        dz_, dg_, db_ = _vjp(_glu, (z_, g_, b_), d_)
        return [dz_, dg_], [db_]

    dzz1, dgl, dglb = rowwise(f_glu, [sv["zz"], sv["gl"], dd], [w["glu_b"]], outs=[(512, F32), (512, BF16)],
                              accs=[(1, 512)], name=f"s5_glu_bwd_{tag}")
    g["glu_b"] = dglb[0]
    g["glu_w"] = mm(sv["zz"], dgl, "tn", name=f"s5_glu_dw_{tag}")
    dzz2 = mm(dgl, w["glu_w"], "nt", name=f"s5_glu_dz_{tag}")

    def f_gelu(y_, u_, d1, d2, dpar):
        dy_, du_, dd_ = _vjp(_s5_out, (y_, u_, dpar), d1 + d2)
        return [dy_, du_], [dd_]

    dyssm, du1, dsd = rowwise(f_gelu, [sv["yssm"], sv["u"], dzz1, dzz2], [w["s5_d"]],
                              outs=[(512, F32), (512, F32)], accs=[(1, 512)], name=f"s5_gelu_bwd_{tag}")
    g["s5_d"] = dsd[0]
    dhs = mm(dyssm, w["s5_cexp"], "nt", name=f"s5_dh_{tag}")
    g["s5_cexp"] = mm(sv["hs"], dyssm, "tn", name=f"s5_dc_{tag}")
    gs = scan_cplx(w["s5_lam"], dhs, reverse=True, name=f"s5_scan_bwd_{tag}")
    g["s5_bexp"] = mm(sv["u"], gs, "tn", name=f"s5_db_{tag}")
    du2 = mm(gs, w["s5_bexp"], "nt", name=f"s5_du_{tag}")

    def f_dlam(g_, hp_):
        C = g_.shape[1]
        outs_r, outs_i = [], []
        for j in range(C // (2 * LANES)):
            gr, gi = g_[:, 2 * LANES * j:2 * LANES * j + LANES], g_[:, 2 * LANES * j + LANES:2 * LANES * (j + 1)]
            hr, hi = hp_[:, 2 * LANES * j:2 * LANES * j + LANES], hp_[:, 2 * LANES * j + LANES:2 * LANES * (j + 1)]
            outs_r.append(jnp.sum(gr * hr + gi * hi, axis=0, keepdims=True))
            outs_i.append(jnp.sum(gi * hr - gr * hi, axis=0, keepdims=True))
        return [], [jnp.concatenate([x for pair in zip(outs_r, outs_i) for x in pair], axis=1)]

    g["s5_lam"] = rowwise(f_dlam, [gs, _shift_down(sv["hs"])], accs=[(1, gs.shape[1])], name=f"s5_dlam_{tag}")[0]
    du = rowwise(lambda a_, b_: ([a_ + b_], []), [du1, du2], outs=[(512, F32)], name=f"s5_du_add_{tag}")[0]
    dz = jnp.concatenate([_unheads(dq_hm), _unheads(dk_hm), _unheads(dv_hm), du], axis=1).astype(BF16)
    g["w_in"] = mm(sv["h"], dz, "tn", name=f"od_dwin_{tag}")
    return dz, g


def _s5_cols(x_re, x_im):
    n = x_re.shape[0] // LANES
    return jnp.stack([x_re.reshape(n, LANES), x_im.reshape(n, LANES)], axis=1).reshape(1, 2 * n * LANES)


def _s5_uncols(x):
    n = x.shape[1] // (2 * LANES)
    y = x.reshape(n, 2, LANES)
    return y[:, 0].reshape(-1), y[:, 1].reshape(-1)


def _s5_gsel():
    return jnp.repeat(jnp.eye(S5_GROUPS, dtype=F32), S5_STATE, axis=0)


def s5_prep_fwd(lre, lim, ldt, bre, bim, cre, cim, tag):
    GP = S5_GROUPS * S5_STATE
    ins = [lre.reshape(GP, 1), lim.reshape(GP, 1), ldt.reshape(1, S5_GROUPS), _s5_gsel(),
           bre.reshape(GP, S5_GROUP), bim.reshape(GP, S5_GROUP)]
    lbr, lbi, bbr, bbi = whole(_s5_params, ins, [((GP, 1), F32)] * 2 + [((GP, S5_GROUP), F32)] * 2,
                               name=f"s5_params_{tag}")
    lam = _s5_cols(lbr[:, 0], lbi[:, 0])

    def expand_b(bb):
        return _block_diag(bb.reshape(S5_GROUPS, S5_STATE, S5_GROUP).transpose(0, 2, 1))

    n = GP // LANES
    bexp = jnp.stack([expand_b(bbr).reshape(-1, n, LANES), expand_b(bbi).reshape(-1, n, LANES)],
                     axis=2).reshape(-1, 2 * GP)
    c_r = _block_diag(cre.transpose(0, 2, 1))
    c_i = _block_diag(cim.transpose(0, 2, 1))
    cexp = jnp.stack([c_r.reshape(n, LANES, -1), -c_i.reshape(n, LANES, -1)], axis=1).reshape(2 * GP, -1)
    return lam, bexp.astype(BF16), cexp.astype(BF16), ins


def s5_prep_bwd(ins, dlam, dbexp, dcexp, tag):
    GP = S5_GROUPS * S5_STATE
    n = GP // LANES
    dlr, dli = _s5_uncols(dlam)
    db = dbexp.reshape(-1, n, 2, LANES)

    def take_b(x):
        return _block_diag_take(x, S5_GROUPS).transpose(0, 2, 1).reshape(GP, S5_GROUP)

    dbbr, dbbi = take_b(db[:, :, 0].reshape(-1, GP)), take_b(db[:, :, 1].reshape(-1, GP))
    dc = dcexp.reshape(n, 2, LANES, -1)
    dcre = _block_diag_take(dc[:, 0].reshape(GP, -1), S5_GROUPS).transpose(0, 2, 1)
    dcim = -_block_diag_take(dc[:, 1].reshape(GP, -1), S5_GROUPS).transpose(0, 2, 1)

    def f(lre, lim, ldt, gsel, bre, bim, c1, c2, c3, c4):
        d = _vjp(lambda a, b, c, e, f_: _s5_params(a, b, c, gsel, e, f_), (lre, lim, ldt, bre, bim), (c1, c2, c3, c4))
        return d

    outs = [((GP, 1), F32)] * 2 + [((1, S5_GROUPS), F32)] + [((GP, S5_GROUP), F32)] * 2
    dlre, dlim, dldt, dbre, dbim = whole(f, ins + [dlr.reshape(GP, 1), dli.reshape(GP, 1), dbbr, dbbi], outs,
                                          name=f"s5_params_bwd_{tag}")
    shp = (S5_GROUPS, S5_STATE)
    return dict(lre=dlre.reshape(shp), lim=dlim.reshape(shp), ldt=dldt.reshape(S5_GROUPS),
                bre=dbre.reshape(S5_GROUPS, S5_STATE, S5_GROUP), bim=dbim.reshape(S5_GROUPS, S5_STATE, S5_GROUP),
                cre=dcre, cim=dcim)


def _place():
    return lax.axis_index("x"), lax.axis_index("y"), lax.axis_index("c")


def _other_chips(x, y):
    return [(1 - x, y), (x, 1 - y), (1 - x, 1 - y)]


def _half(ref, h):
    n = ref.shape[0] // 2
    return ref.at[pl.ds(h * n, n)]


def _hbm_specs(n):
    return [pl.BlockSpec(memory_space=pl.ANY)] * n


def gather_chips(ws):
    n = len(ws)

    def body(*refs):
        w_refs, out_refs, (send_sems, recv_sems) = refs[:n], refs[n:2 * n], refs[2 * n:]
        x, y, c = _place()
        me, sibling = (x, y, c), (x, y, 1 - c)
        chips = _other_chips(x, y)
        mine = 2 * x + y

        def copy(k, src, dst, to):
            return pltpu.make_async_remote_copy(src_ref=src, dst_ref=dst, send_sem=send_sems.at[k],
                                                recv_sem=recv_sems.at[k], device_id=to, device_id_type=MESH)

        first, passed = [], []
        for p in range(n):
            for j, chip in enumerate(chips):
                first.append(copy(6 * p + j, _half(w_refs[p], c), _half(out_refs[p].at[mine], c), (*chip, c)))
                first[-1].start()
        for p in range(n):
            for j, chip in enumerate(chips):
                block = out_refs[p].at[2 * chip[0] + chip[1]]
                copy(6 * p + j, _half(w_refs[p], c), _half(block, c), me).wait_recv()
                passed.append(copy(6 * p + 3 + j, _half(block, c), _half(block, c), sibling))
                passed[-1].start()
        for p in range(n):
            for j, chip in enumerate(chips):
                block = out_refs[p].at[2 * chip[0] + chip[1]]
                copy(6 * p + 3 + j, _half(w_refs[p], c), _half(block, 1 - c), me).wait_recv()
        for cp in first + passed:
            cp.wait_send()

    return pl.pallas_call(
        body, out_shape=[jax.ShapeDtypeStruct((N_CHIPS,) + w.shape, w.dtype) for w in ws],
        in_specs=_hbm_specs(n), out_specs=_hbm_specs(n),
        scratch_shapes=[pltpu.SemaphoreType.DMA((6 * n,)), pltpu.SemaphoreType.DMA((6 * n,))],
        name="gather_chips",
    )(*ws)


def sibling_halves(gs):
    n = len(gs)

    def body(*refs):
        g_refs, out_refs, (send_sems, recv_sems) = refs[:n], refs[n:2 * n], refs[2 * n:]
        x, y, c = _place()
        me, sibling = (x, y, c), (x, y, 1 - c)

        def copy(p, k, to):
            return pltpu.make_async_remote_copy(src_ref=_half(g_refs[p].at[k], 1 - c), dst_ref=out_refs[p].at[k],
                                                send_sem=send_sems.at[N_CHIPS * p + k],
                                                recv_sem=recv_sems.at[N_CHIPS * p + k],
                                                device_id=to, device_id_type=MESH)

        cps = [copy(p, k, sibling) for p in range(n) for k in range(N_CHIPS)]
        for cp in cps:
            cp.start()
        for p in range(n):
            for k in range(N_CHIPS):
                copy(p, k, me).wait_recv()
        for cp in cps:
            cp.wait_send()

    return pl.pallas_call(
        body, out_shape=[jax.ShapeDtypeStruct((N_CHIPS, g.shape[1] // 2) + g.shape[2:], g.dtype) for g in gs],
        in_specs=_hbm_specs(n), out_specs=_hbm_specs(n),
        scratch_shapes=[pltpu.SemaphoreType.DMA((N_CHIPS * n,)), pltpu.SemaphoreType.DMA((N_CHIPS * n,))],
        name="sibling_halves",
    )(*gs)


def exchange_chips(ps):
    n = len(ps)

    def body(*refs):
        p_refs, out_refs, (send_sems, recv_sems) = refs[:n], refs[n:2 * n], refs[2 * n:]
        x, y, c = _place()
        me = (x, y, c)
        chips = _other_chips(x, y)

        def copy(p, j, chip, to):
            return pltpu.make_async_remote_copy(src_ref=p_refs[p].at[2 * chip[0] + chip[1]], dst_ref=out_refs[p].at[j],
                                                send_sem=send_sems.at[3 * p + j], recv_sem=recv_sems.at[3 * p + j],
                                                device_id=to, device_id_type=MESH)

        cps = [copy(p, j, chip, (*chip, c)) for p in range(n) for j, chip in enumerate(chips)]
        for cp in cps:
            cp.start()
        for p in range(n):
            for j, chip in enumerate(chips):
                copy(p, j, chip, me).wait_recv()
        for cp in cps:
            cp.wait_send()

    return pl.pallas_call(
        body, out_shape=[jax.ShapeDtypeStruct((3,) + p_.shape[1:], p_.dtype) for p_ in ps],
        in_specs=_hbm_specs(n), out_specs=_hbm_specs(n),
        scratch_shapes=[pltpu.SemaphoreType.DMA((3 * n,)), pltpu.SemaphoreType.DMA((3 * n,))],
        name="exchange_chips",
    )(*ps)


def sibling_join(rs):
    n = len(rs)

    def body(*refs):
        r_refs, out_refs, (send_sems, recv_sems) = refs[:n], refs[n:2 * n], refs[2 * n:]
        x, y, c = _place()

        def copy(p, h, to):
            return pltpu.make_async_remote_copy(src_ref=r_refs[p], dst_ref=_half(out_refs[p], h),
                                                send_sem=send_sems.at[p], recv_sem=recv_sems.at[p],
                                                device_id=to, device_id_type=MESH)

        cps = [copy(p, c, (x, y, 1 - c)) for p in range(n)]
        for cp in cps:
            cp.start()
        for p in range(n):
            copy(p, 1 - c, (x, y, c)).wait_recv()
        for cp in cps:
            cp.wait_send()

    return pl.pallas_call(
        body, out_shape=[jax.ShapeDtypeStruct((2 * r.shape[0],) + r.shape[1:], r.dtype) for r in rs],
        in_specs=_hbm_specs(n), out_specs=_hbm_specs(n),
        scratch_shapes=[pltpu.SemaphoreType.DMA((n,)), pltpu.SemaphoreType.DMA((n,))],
        name="sibling_join",
    )(*rs)


def gather_devices(v, name):
    R = v.shape[0]

    def body(v_ref, out_ref, send_sems, recv_sems, local_sem):
        x, y, c = _place()
        me, sibling = (x, y, c), (x, y, 1 - c)
        chips = _other_chips(x, y)

        def rows(px, py, pc):
            return out_ref.at[pl.ds((4 * px + 2 * py + pc) * R, R), :]

        def copy(k, block, to, src=None):
            return pltpu.make_async_remote_copy(src_ref=rows(*block) if src is None else src, dst_ref=rows(*block),
                                                send_sem=send_sems.at[k], recv_sem=recv_sems.at[k],
                                                device_id=to, device_id_type=MESH)

        mine = pltpu.make_async_copy(v_ref, rows(*me), local_sem)
        mine.start()
        first = [copy(0, me, sibling, src=v_ref)]
        first += [copy(1 + j, me, (*chip, c), src=v_ref) for j, chip in enumerate(chips)]
        for cp in first:
            cp.start()
        passed = [copy(4 + j, (*chip, c), sibling) for j, chip in enumerate(chips)]
        for j, chip in enumerate(chips):
            copy(1 + j, (*chip, c), me).wait_recv()
            passed[j].start()
        copy(0, sibling, me).wait_recv()
        for j, chip in enumerate(chips):
            copy(4 + j, (*chip, 1 - c), me).wait_recv()
        for cp in first + passed:
            cp.wait_send()
        mine.wait()

    return pl.pallas_call(
        body, out_shape=jax.ShapeDtypeStruct((N_DEV * R, LANES), v.dtype),
        in_specs=[pl.BlockSpec(memory_space=pltpu.VMEM)], out_specs=pl.BlockSpec(memory_space=pltpu.VMEM),
        scratch_shapes=[pltpu.SemaphoreType.DMA((7,)), pltpu.SemaphoreType.DMA((7,)), pltpu.SemaphoreType.DMA],
        name=name, compiler_params=_params(),
    )(v)


def _flat_rows(n, mult):
    return -(-n // (LANES * mult)) * mult


def _adam(w, g, m, v):
    m = ADAM_B1 * m + (1.0 - ADAM_B1) * g
    v = ADAM_B2 * v + (1.0 - ADAM_B2) * (g * g)
    m_hat = m / (1.0 - ADAM_B1 ** ADAM_STEP)
    v_hat = v / (1.0 - ADAM_B2 ** ADAM_STEP)
    return -ADAM_LR * (m_hat / (jnp.sqrt(v_hat) + ADAM_EPS) + ADAM_WD * w), m, v


def adam_2d(w, g, m, v, name):
    shape = w.shape
    F = shape[-1]
    if w.ndim == 3 and shape[1] % (2 * SUBLANES) == 0:
        L, R, _ = shape
        tr = R
        for t in (512, 256, 128, 64, 32, 16):
            if R % t == 0 and 7 * t * max(F, LANES) * 4 <= ROW_TILE_BYTES:
                tr = t
                break

        def body(w_ref, g_ref, m_ref, v_ref, d_ref, m2_ref, v2_ref):
            d_ref[...], m2_ref[...], v2_ref[...] = _adam(w_ref[...], g_ref[...], m_ref[...], v_ref[...])

        spec = pl.BlockSpec((None, tr, F), lambda l, i: (l, i, 0))
        return pl.pallas_call(
            body, grid=(L, R // tr), in_specs=[spec] * 4, out_specs=[spec] * 3,
            out_shape=[jax.ShapeDtypeStruct(shape, F32)] * 3, name=name, compiler_params=_params(("parallel", "parallel")),
        )(w, g, m, v)
    a = [t.reshape(-1, F) for t in (w, g, m, v)]
    d, m2, v2 = rowwise(lambda w_, g_, m_, v_: (list(_adam(w_, g_, m_, v_)), []), a, outs=[(F, F32)] * 3, name=name)
    return d.reshape(shape), m2.reshape(shape), v2.reshape(shape)


WEIGHTS = ['ffn1_norm', 'ffn1_wg', 'ffn1_wu', 'ffn1_wd', 'mix_norm', 'ffn2_norm', 'ffn2_wg', 'ffn2_wu', 'ffn2_wd',
           'ple_w', 'ple_norm', 'ple_gate_norm', 'ple_gate_w', 'ev_w_in', 'lru_conv_w', 'lru_conv_b', 'lru_wa',
           'lru_ba', 'lru_wx', 'lru_bx', 'lru_lambda', 'fox_bf', 'fox_q_norm', 'fox_k_norm', 'ev_w_out', 'od_w_in',
           'swa_q_norm', 'swa_k_norm', 'swa_sinks', 's5_lambda_re', 's5_lambda_im', 's5_log_dt', 's5_b_re',
           's5_b_im', 's5_c_re', 's5_c_im', 's5_d', 's5_glu_w', 's5_glu_b', 'od_w_out']
SHARD_AXIS = {'ffn1_wg': 2, 'ffn1_wu': 2, 'ffn1_wd': 1, 'ffn2_wg': 2, 'ffn2_wu': 2, 'ffn2_wd': 1, 'ple_w': 2,
              'ple_gate_w': 1, 'ev_w_in': 2, 'lru_conv_w': 2, 'ev_w_out': 1, 'od_w_in': 2, 's5_d': 1,
              's5_glu_w': 1, 's5_glu_b': 1, 'od_w_out': 1}
EXACT_SHARDED = ('lru_conv_w', 's5_d', 's5_glu_b')
ADAM_TRANSPOSED = ('ffn1_wg', 'ffn1_wu', 'ffn2_wg', 'ffn2_wu', 'od_w_in')
SHARDED = [n for n in WEIGHTS if n in SHARD_AXIS]
REPLICATED = [n for n in WEIGHTS if n not in SHARD_AXIS]


GROUPS = {
    'wgu': ['ffn1_wg', 'ffn1_wu', 'ffn2_wg', 'ffn2_wu'],
    'wd': ['ffn1_wd', 'ffn2_wd'],
    'w_rows': ['ple_gate_w', 'ev_w_out', 'od_w_out'],
    'ple_w': ['ple_w'], 'ev_w_in': ['ev_w_in'], 'od_w_in': ['od_w_in'], 's5_glu_w': ['s5_glu_w'],
}
REDUCED_GROUPS = list(GROUPS)


def _chip():
    return 2 * lax.axis_index("x") + lax.axis_index("y")


def gather_weights(shards):
    own = {k: jnp.concatenate([shards[n] for n in names], axis=0).astype(BF16) for k, names in GROUPS.items()}
    own['exact'] = jnp.concatenate([shards['lru_conv_w'], shards['s5_d'][:, None], shards['s5_glu_b'][:, None]], axis=1)
    keys = list(own)
    got = gather_chips([own[k] for k in keys])
    return {k: lax.dynamic_update_index_in_dim(g, own[k], _chip(), 0) for k, g in zip(keys, got)}


def _rows_by_chip(w):
    return w.reshape(w.shape[0] * w.shape[1], w.shape[2])


def _cols_by_chip(w):
    return w.transpose(1, 0, 2).reshape(w.shape[1], w.shape[0] * w.shape[2])


def _chip_rows(g):
    return g.reshape(N_CHIPS, g.shape[0] // N_CHIPS, g.shape[1])


def _chip_cols(g):
    return g.reshape(g.shape[0], N_CHIPS, g.shape[1] // N_CHIPS).transpose(1, 0, 2)


def full_weights(gw, depth):
    n_ev = (depth + 1) // 2
    ex = gw['exact']
    return dict(
        ple_gate_w=[_rows_by_chip(gw['w_rows'][:, l]) for l in range(depth)],
        ev_w_out=[_rows_by_chip(gw['w_rows'][:, depth + j]) for j in range(n_ev)],
        od_w_out=[_rows_by_chip(gw['w_rows'][:, depth + n_ev + j]) for j in range(depth // 2)],
        ple_w=[_cols_by_chip(gw['ple_w'][:, l]) for l in range(depth)],
        ev_w_in=[_cols_by_chip(gw['ev_w_in'][:, j]) for j in range(n_ev)],
        od_w_in=[_cols_by_chip(gw['od_w_in'][:, j]) for j in range(depth // 2)],
        s5_glu_w=[_rows_by_chip(gw['s5_glu_w'][:, j]) for j in range(depth // 2)],
        lru_conv_w=[_cols_by_chip(ex[:, j, 0:LRU_CONV]) for j in range(n_ev)],
        s5_d=[ex[:, j, LRU_CONV].reshape(-1) for j in range(depth // 2)],
        s5_glu_b=[ex[:, j, LRU_CONV + 1].reshape(-1) for j in range(depth // 2)],
    )


def _add_tile(rows, width):
    for t in (1024, 512, 256, 128, 64, 32, 16):
        if rows % t == 0 and 3 * t * width * 4 <= ROW_TILE_BYTES:
            return t
    return rows


def pair_add(g, t, c, name):
    C, F = g.shape[0], g.shape[-1]
    rows = math.prod(t.shape[1:-1])
    tr = _add_tile(rows, F)
    nb = rows // tr

    def body(c_ref, g_ref, t_ref, o_ref):
        o_ref[...] = (g_ref[...].astype(F32) + t_ref[...].astype(F32)).astype(o_ref.dtype)

    spec = pl.BlockSpec((None, tr, F), lambda k, i, c_ref: (k, i, 0))
    out = pl.pallas_call(
        body, out_shape=jax.ShapeDtypeStruct((C, rows, F), BF16),
        grid_spec=pltpu.PrefetchScalarGridSpec(
            num_scalar_prefetch=1, grid=(C, nb),
            in_specs=[pl.BlockSpec((None, tr, F), lambda k, i, c_ref: (k, c_ref[0] * nb + i, 0)), spec],
            out_specs=spec),
        name=name, compiler_params=_params(("parallel", "parallel")),
    )(c.reshape(1).astype(jnp.int32), g.reshape(C, 2 * rows, F), t.reshape(C, rows, F))
    return out.reshape(t.shape)


def chips_add(p, xs, chip, name):
    F = p.shape[-1]
    rows = math.prod(p.shape[1:-1])
    tr = _add_tile(rows, F)

    def body(m_ref, p_ref, a_ref, b_ref, d_ref, o_ref):
        o_ref[...] = ((p_ref[...].astype(F32) + a_ref[...].astype(F32))
                      + (b_ref[...].astype(F32) + d_ref[...].astype(F32)))

    def other(j):
        return pl.BlockSpec((None, tr, F), lambda i, m_ref: (j, i, 0))

    x3 = xs.reshape(3, rows, F)
    out = pl.pallas_call(
        body, out_shape=jax.ShapeDtypeStruct((rows, F), F32),
        grid_spec=pltpu.PrefetchScalarGridSpec(
            num_scalar_prefetch=1, grid=(rows // tr,),
            in_specs=[pl.BlockSpec((None, tr, F), lambda i, m_ref: (m_ref[0], i, 0)), other(0), other(1), other(2)],
            out_specs=pl.BlockSpec((tr, F), lambda i, m_ref: (i, 0))),
        name=name, compiler_params=_params(("parallel",)),
    )(chip.reshape(1).astype(jnp.int32), p.reshape(N_CHIPS, rows, F), x3, x3, x3)
    return out.reshape(p.shape[1:])


def reduce_sharded(groups):
    keys = list(groups)
    c = lax.axis_index("c")
    gs = [groups[k] for k in keys]
    theirs = sibling_halves(gs)
    pairs = [pair_add(g, t, c, f"pair_add_{k}") for k, g, t in zip(keys, gs, theirs)]
    got = exchange_chips(pairs)
    halves = [chips_add(p_, x_, _chip(), f"chips_add_{k}") for k, p_, x_ in zip(keys, pairs, got)]
    joined = sibling_join(halves)
    out = {}
    for k, h, j in zip(keys, halves, joined):
        out[k] = lax.dynamic_update_slice_in_dim(j, h, c * h.shape[0], axis=0)
    return out


SMALL_GRADS = REPLICATED + list(EXACT_SHARDED)


def _flatten_small(tensors, shapes):
    parts = [tensors[n].astype(F32).reshape(-1) if n in tensors else jnp.zeros((math.prod(shapes[n]),), F32)
             for n in SMALL_GRADS]
    flat = jnp.concatenate(parts)
    rows = _flat_rows(flat.shape[0], SUBLANES)
    return jnp.pad(flat, (0, rows * LANES - flat.shape[0])).reshape(rows, LANES)


def _unflatten_small(flat, shapes):
    flat = flat.reshape(-1)
    out, off = {}, 0
    for n in SMALL_GRADS:
        size = math.prod(shapes[n])
        out[n] = flat[off:off + size].reshape(shapes[n])
        off += size
    return out


def grad_groups(gwgu, gwd, G):
    def st(xs):
        return jnp.stack(xs, axis=1).astype(BF16)

    return {
        'wgu': gwgu, 'wd': gwd,
        'w_rows': st([_chip_rows(g) for n in GROUPS['w_rows'] for g in G[n]]),
        'ple_w': st([_chip_cols(g) for g in G['ple_w']]),
        'ev_w_in': st([_chip_cols(g) for g in G['ev_w_in']]),
        'od_w_in': st([_chip_cols(g) for g in G['od_w_in']]),
        's5_glu_w': st([_chip_rows(g) for g in G['s5_glu_w']]),
    }


def ungroup(red, shapes):
    out = {}
    for k, names in GROUPS.items():
        off = 0
        for n in names:
            out[n] = red[k][off:off + shapes[n][0]]
            off += shapes[n][0]
    return out


def _layer_weights(full, small, i, depth):
    j = i // 2
    w = dict(
        g1=small['ffn1_norm'][i][None], gm=small['mix_norm'][i][None], g2=small['ffn2_norm'][i][None],
        gp=small['ple_norm'][i][None], gg=small['ple_gate_norm'][i][None],
        ffn1=(i, depth + i, i), ffn2=(2 * depth + i, 3 * depth + i, depth + i),
        ple_w=full['ple_w'][i], ple_gate_w=full['ple_gate_w'][i],
    )
    if i % 2 == 0:
        w_in = full['ev_w_in'][j]
        w['mix'] = dict(
            w_in=jnp.pad(w_in, ((0, 0), (0, 2688 - w_in.shape[1]))), w_out=full['ev_w_out'][j],
            conv_w=full['lru_conv_w'][j].astype(F32), conv_b=small['lru_conv_b'][j][None],
            w_ax=jnp.concatenate([_block_diag(small['lru_wa'][j]), _block_diag(small['lru_wx'][j])],
                                 axis=1).astype(BF16),
            ba=small['lru_ba'][j][None], bx=small['lru_bx'][j][None], lam=small['lru_lambda'][j][None],
            bf=jnp.pad(small['fox_bf'][j], (0, LANES - 8))[None], qn=small['fox_q_norm'][j],
            kn=small['fox_k_norm'][j])
    else:
        lam, bexp, cexp, ins = s5_prep_fwd(small['s5_lambda_re'][j], small['s5_lambda_im'][j], small['s5_log_dt'][j],
                                           small['s5_b_re'][j], small['s5_b_im'][j], small['s5_c_re'][j],
                                           small['s5_c_im'][j], f"L{i}")
        w['mix'] = dict(
            w_in=full['od_w_in'][j], w_out=full['od_w_out'][j], qn=small['swa_q_norm'][j], kn=small['swa_k_norm'][j],
            sinks=small['swa_sinks'][j], s5_lam=lam, s5_bexp=bexp, s5_cexp=cexp, s5_ins=ins,
            s5_d=full['s5_d'][j].astype(F32)[None], glu_w=full['s5_glu_w'][j], glu_b=full['s5_glu_b'][j].astype(F32)[None])
    return w


def layer_fwd(x0, n1, p_i, w, ffnw, next_g1, i):
    tag = f"L{i}"
    sv = {}
    wgu, wd = ffnw
    x1, hm, sv['ffn1'] = ffn_fwd(n1, x0, w['gm'], wgu, wd, *w['ffn1'], f"1_{tag}")
    if i % 2 == 0:
        mo, sv['mix'] = even_mixer_fwd(hm, w['mix'], tag)
    else:
        mo, sv['mix'] = odd_mixer_fwd(hm, w['mix'], tag)
    x2, n2 = mm_add_norm(mo, w['mix']['w_out'], x1, w['g2'], f"mix_out_{tag}")
    x3, ng, sv['ffn2'] = ffn_fwd(n2, x2, w['gg'], wgu, wd, *w['ffn2'], f"2_{tag}")
    gpre = mm(ng, w['ple_gate_w'], name=f"ple_gate_{tag}")
    epre = mm(p_i, w['ple_w'], name=f"ple_emb_{tag}")
    D = x0.shape[1]
    if next_g1 is None:
        x4 = rowwise(lambda a, b, c, pn: ([_ple_out(a, b, c, pn)], []), [x3, gpre, epre], [w['gp']],
                     outs=[(D, F32)], name=f"ple_out_{tag}")[0]
        n_next = None
    else:
        def f(a, b, c, pn, gn):
            y = _ple_out(a, b, c, pn)
            return [y, _rms(y, gn)], []

        x4, n_next = rowwise(f, [x3, gpre, epre], [w['gp'], next_g1], outs=[(D, F32), (D, BF16)],
                             name=f"ple_out_{tag}")
    sv.update(x0=x0, x1=x1, x2=x2, x3=x3, ng=ng, gpre=gpre, epre=epre, p=p_i)
    return x4, n_next, sv


def layer_bwd(dx4, sv, w, ffnw, gbuf, i):
    tag = f"L{i}"
    D = dx4.shape[1]
    g = {}
    wgu, wd = ffnw

    def f_ple(a, b, c, d, pn):
        da, db, dc, dpn = _vjp(_ple_out, (a, b, c, pn), d)
        return [db, dc], [dpn]

    dgpre, depre, dgp = rowwise(f_ple, [sv['x3'], sv['gpre'], sv['epre'], dx4], [w['gp']],
                                outs=[(D, BF16), (D, BF16)], accs=[(1, D)], name=f"ple_out_bwd_{tag}")
    g['gp'] = dgp[0]
    g['ple_w'] = mm(sv['p'], depre, "tn", name=f"ple_emb_dw_{tag}")
    g['ple_gate_w'] = mm(sv['ng'], dgpre, "tn", name=f"ple_gate_dw_{tag}")
    dx3, dgg = mm_norm_bwd(dgpre, w['ple_gate_w'], sv['x3'], w['gg'], dx4, f"ple_gate_dx_{tag}")
    g['gg'] = dgg[0]
    dx2, dg2, gbuf = ffn_bwd(dx3, sv['ffn2'], sv['x2'], w['g2'], wgu, wd, *w['ffn2'], gbuf, f"2_{tag}")
    g['g2'] = dg2[0]
    if i % 2 == 0:
        dz, g['mix'] = even_mixer_bwd(dx2, sv['mix'], w['mix'], tag)
    else:
        dz, g['mix'] = odd_mixer_bwd(dx2, sv['mix'], w['mix'], tag)
    dx1, dgm = mm_norm_bwd(dz, w['mix']['w_in'], sv['x1'], w['gm'], dx2, f"mix_dh_{tag}")
    g['gm'] = dgm[0]
    dx0, dg1, gbuf = ffn_bwd(dx1, sv['ffn1'], sv['x0'], w['g1'], wgu, wd, *w['ffn1'], gbuf, f"1_{tag}")
    g['g1'] = dg1[0]
    return dx0, g, gbuf


def _collect_grads(layer_grads, depth):
    st = lambda xs: jnp.stack(xs)
    G = {}
    L = layer_grads
    G['ffn1_norm'] = st([g['g1'] for g in L])
    G['mix_norm'] = st([g['gm'] for g in L])
    G['ffn2_norm'] = st([g['g2'] for g in L])
    G['ple_norm'] = st([g['gp'] for g in L])
    G['ple_gate_norm'] = st([g['gg'] for g in L])
    G['ple_w'] = st([g['ple_w'] for g in L])
    G['ple_gate_w'] = st([g['ple_gate_w'] for g in L])
    ev = [L[i]['mix'] for i in range(0, depth, 2)]
    od = [L[i]['mix'] for i in range(1, depth, 2)]
    G['ev_w_in'] = st([m['w_in'][:, :2568] for m in ev])
    G['ev_w_out'] = st([m['w_out'] for m in ev])
    G['lru_conv_w'] = st([m['conv_w'] for m in ev])
    G['lru_conv_b'] = st([m['conv_b'] for m in ev])
    G['lru_wa'] = st([_block_diag_take(m['w_ax'][:, :512], LRU_BLOCKS) for m in ev])
    G['lru_wx'] = st([_block_diag_take(m['w_ax'][:, 512:], LRU_BLOCKS) for m in ev])
    G['lru_ba'] = st([m['ba'] for m in ev])
    G['lru_bx'] = st([m['bx'] for m in ev])
    G['lru_lambda'] = st([m['lam'] for m in ev])
    G['fox_bf'] = st([m['bf'] for m in ev])
    G['fox_q_norm'] = st([m['qn'] for m in ev])
    G['fox_k_norm'] = st([m['kn'] for m in ev])
    G['od_w_in'] = st([m['w_in'] for m in od])
    G['od_w_out'] = st([m['w_out'] for m in od])
    G['swa_q_norm'] = st([m['qn'] for m in od])
    G['swa_k_norm'] = st([m['kn'] for m in od])
    G['swa_sinks'] = st([m['sinks'] for m in od])
    G['s5_lambda_re'] = st([m['s5']['lre'] for m in od])
    G['s5_lambda_im'] = st([m['s5']['lim'] for m in od])
    G['s5_log_dt'] = st([m['s5']['ldt'] for m in od])
    G['s5_b_re'] = st([m['s5']['bre'] for m in od])
    G['s5_b_im'] = st([m['s5']['bim'] for m in od])
    G['s5_c_re'] = st([m['s5']['cre'] for m in od])
    G['s5_c_im'] = st([m['s5']['cim'] for m in od])
    G['s5_d'] = st([m['s5_d'] for m in od])
    G['s5_glu_w'] = st([m['glu_w'] for m in od])
    G['s5_glu_b'] = st([m['glu_b'] for m in od])
    return G


def local_step(x, p, target, ffnw, full, small):
    depth = p.shape[0]
    S, D = x.shape
    ws = [_layer_weights(full, small, i, depth) for i in range(depth)]
    saved = []
    xi, ni = add_norm(x, None, ws[0]['g1'], "norm1_L0")
    for i in range(depth):
        xi, ni, sv = layer_fwd(xi, ni, p[i], ws[i], ffnw, ws[i + 1]['g1'] if i + 1 < depth else None, i)
        saved.append(sv)

    def f_loss(y, t):
        e = y - t
        return [e * (1.0 / D)], [0.5 * jnp.sum(jnp.mean(e * e, axis=-1, keepdims=True), axis=0, keepdims=True)]

    dx, loss = rowwise(f_loss, [xi, target], outs=[(D, F32)], accs=[(1, 1)], name="loss")
    grads = [None] * depth
    gbuf = (None, None)
    for i in reversed(range(depth)):
        dx, grads[i], gbuf = layer_bwd(dx, saved[i], ws[i], ffnw, gbuf, i)
        if i % 2 == 1:
            m = grads[i]['mix']
            m['s5'] = s5_prep_bwd(ws[i]['mix']['s5_ins'], m['s5_lam'], m['s5_bexp'], m['s5_cexp'], f"L{i}")
    return loss[0, 0], dx, gbuf, _collect_grads(grads, depth)


def kernel(x, p, ffn1_norm, ffn1_wg, ffn1_wu, ffn1_wd, mix_norm, ffn2_norm, ffn2_wg, ffn2_wu, ffn2_wd, ple_w, ple_norm, ple_gate_norm, ple_gate_w, ev_w_in, lru_conv_w, lru_conv_b, lru_wa, lru_ba, lru_wx, lru_bx, lru_lambda, fox_bf, fox_q_norm, fox_k_norm, ev_w_out, od_w_in, swa_q_norm, swa_k_norm, swa_sinks, s5_lambda_re, s5_lambda_im, s5_log_dt, s5_b_re, s5_b_im, s5_c_re, s5_c_im, s5_d, s5_glu_w, s5_glu_b, od_w_out, loss_target, m_ffn1_norm, m_ffn1_wg, m_ffn1_wu, m_ffn1_wd, m_mix_norm, m_ffn2_norm, m_ffn2_wg, m_ffn2_wu, m_ffn2_wd, m_ple_w, m_ple_norm, m_ple_gate_norm, m_ple_gate_w, m_ev_w_in, m_lru_conv_w, m_lru_conv_b, m_lru_wa, m_lru_ba, m_lru_wx, m_lru_bx, m_lru_lambda, m_fox_bf, m_fox_q_norm, m_fox_k_norm, m_ev_w_out, m_od_w_in, m_swa_q_norm, m_swa_k_norm, m_swa_sinks, m_s5_lambda_re, m_s5_lambda_im, m_s5_log_dt, m_s5_b_re, m_s5_b_im, m_s5_c_re, m_s5_c_im, m_s5_d, m_s5_glu_w, m_s5_glu_b, m_od_w_out, v_ffn1_norm, v_ffn1_wg, v_ffn1_wu, v_ffn1_wd, v_mix_norm, v_ffn2_norm, v_ffn2_wg, v_ffn2_wu, v_ffn2_wd, v_ple_w, v_ple_norm, v_ple_gate_norm, v_ple_gate_w, v_ev_w_in, v_lru_conv_w, v_lru_conv_b, v_lru_wa, v_lru_ba, v_lru_wx, v_lru_bx, v_lru_lambda, v_fox_bf, v_fox_q_norm, v_fox_k_norm, v_ev_w_out, v_od_w_in, v_swa_q_norm, v_swa_k_norm, v_swa_sinks, v_s5_lambda_re, v_s5_lambda_im, v_s5_log_dt, v_s5_b_re, v_s5_b_im, v_s5_c_re, v_s5_c_im, v_s5_d, v_s5_glu_w, v_s5_glu_b, v_od_w_out):
    args = locals()
    wts = {n: args[n] for n in WEIGHTS}
    ms = {n: args["m_" + n] for n in WEIGHTS}
    vs = {n: args["v_" + n] for n in WEIGHTS}
    shapes = {n: wts[n].shape for n in WEIGHTS}

    depth = p.shape[0]
    gw = gather_weights({n: wts[n] for n in SHARDED})
    small = {n: wts[n] for n in REPLICATED}
    loss, dx, (gwgu, gwd), G = local_step(x[0], p[:, 0], loss_target[0], (gw['wgu'], gw['wd']),
                                          full_weights(gw, depth), small)
    loss = lax.psum(loss, ("x", "y", "c"))

    gsh = ungroup(reduce_sharded(grad_groups(gwgu, gwd, G)), shapes)
    full_shapes = {n: (G[n].shape if n in EXACT_SHARDED else shapes[n]) for n in SMALL_GRADS}
    flat_g = _flatten_small(G, full_shapes)
    g8 = gather_devices(flat_g, "gather_small_grads").reshape((N_DEV,) + flat_g.shape)
    wf, mf, vf = (_flatten_small({n: t[n] for n in REPLICATED}, full_shapes) for t in (wts, ms, vs))

    def f_small(g0, g1, g2, g3, g4, g5, g6, g7, w_, m_, v_):
        gsum = ((g0 + g1) + (g2 + g3)) + ((g4 + g5) + (g6 + g7))
        return [gsum] + list(_adam(w_, gsum, m_, v_)), []

    gs_f, ds_f, ms_f, vs_f = rowwise(f_small, [g8[d] for d in range(N_DEV)] + [wf, mf, vf],
                                     outs=[(LANES, F32)] * 4, name="adam_small")
    out_g, out_d, out_m, out_v = {}, {}, {}, {}
    for dst, flat in ((out_g, gs_f), (out_d, ds_f), (out_m, ms_f), (out_v, vs_f)):
        dst.update(_unflatten_small(flat, full_shapes))
    for n in EXACT_SHARDED:
        width = shapes[n][SHARD_AXIS[n]]
        gsh[n] = lax.dynamic_slice_in_dim(out_g[n], _chip() * width, width, axis=SHARD_AXIS[n])
    for n in SHARDED:
        out_g[n] = gsh[n]
        if n in ADAM_TRANSPOSED:
            def t(a):
                return a.transpose(0, 2, 1)
            d_, m_, v_ = adam_2d(t(wts[n]), t(gsh[n]), t(ms[n]), t(vs[n]), f"adam_{n}")
            out_d[n], out_m[n], out_v[n] = t(d_), t(m_), t(v_)
        else:
            out_d[n], out_m[n], out_v[n] = adam_2d(wts[n], gsh[n], ms[n], vs[n], f"adam_{n}")
    return (loss, dx[None], *[out_g[n] for n in WEIGHTS], *[out_d[n] for n in WEIGHTS],
            *[out_m[n] for n in WEIGHTS], *[out_v[n] for n in WEIGHTS])
```

```python
import functools
import math

import jax
import jax.numpy as jnp
from jax import lax
from jax.experimental import pallas as pl
from jax.experimental.pallas import tpu as pltpu

F32 = jnp.float32
BF16 = jnp.bfloat16
MXU_DTYPE = BF16
HI = lax.Precision.HIGHEST
MESH = pl.DeviceIdType.MESH

VMEM_LIMIT_BYTES = 56 * 1024 * 1024
ROW_TILE_BYTES = 5 * 1024 * 1024
MM_VMEM_BYTES = 40 * 1024 * 1024
MM_TILE_M = 1024
MM_TILE_N = 1408
FLAT_W = 2048
LANES = 128
SUBLANES = 8

HEAD_DIM = 64
LRU_BLOCKS = 8
LRU_CONV = 4
LRU_C = 8.0
SWA_WINDOW = 128
SWA_GROUP = 4
S5_GROUP = 16
S5_GROUPS = 32
S5_STATE = 64
S5_BLOCKS = 4
ROPE_THETA = 10000.0
EPS = 1e-6
MACARON = 0.5
NEG = -1e30

ADAM_LR = 0.001
ADAM_B1 = 0.9
ADAM_B2 = 0.999
ADAM_EPS = 1e-08
ADAM_WD = 0.01
ADAM_STEP = 10

N_CHIPS = 4
N_DEV = 8


def _pick(n, cands):
    for c in cands:
        if n % c == 0:
            return c
    return n


def _tile(n, cap, unit):
    best = None
    for t in range(unit, min(n, cap) + 1, unit):
        if n % t == 0:
            best = t
    return n if best is None else best


def _params(sem=None):
    return pltpu.CompilerParams(dimension_semantics=sem, vmem_limit_bytes=VMEM_LIMIT_BYTES)


def rowwise(fn, rows, consts=(), outs=(), accs=(), name="rowwise", periods=None):
    rows, consts = list(rows), list(consts)
    n_r, n_c, n_o, n_a = len(rows), len(consts), len(outs), len(accs)
    R = rows[0].shape[0]
    periods = list(periods) if periods is not None else [None] * n_r
    per_row = sum(max(r.shape[1], LANES) * 4 for r in rows) + sum(max(f, LANES) * 4 for f, _ in outs)
    limit = min([R] + [p for p in periods if p is not None])
    tr = limit
    for c in (1024, 512, 256, 128, 64, 32, 16):
        if c <= limit and limit % c == 0 and R % c == 0 and c * per_row <= ROW_TILE_BYTES:
            tr = c
            break

    def row_map(period):
        if period is None:
            return lambda i: (i, 0)
        nb = period // tr
        return lambda i: (i % nb, 0)

    in_specs = [pl.BlockSpec((tr, r.shape[1]), row_map(p)) for r, p in zip(rows, periods)]
    in_specs += [pl.BlockSpec(c.shape, lambda i: (0, 0)) for c in consts]
    out_shape = [jax.ShapeDtypeStruct((R, f), dt) for f, dt in outs]
    out_shape += [jax.ShapeDtypeStruct(tuple(s), F32) for s in accs]
    out_specs = [pl.BlockSpec((tr, f), lambda i: (i, 0)) for f, _ in outs]
    out_specs += [pl.BlockSpec(tuple(s), lambda i: (0, 0)) for s in accs]

    def body(*refs):
        ins = [r[...] for r in refs[:n_r + n_c]]
        o_refs = refs[n_r + n_c:n_r + n_c + n_o]
        a_refs = refs[n_r + n_c + n_o:]
        ro, ra = fn(*ins)
        for ref, val in zip(o_refs, ro):
            ref[...] = val.astype(ref.dtype)
        if n_a:
            @pl.when(pl.program_id(0) == 0)
            def _():
                for ref in a_refs:
                    ref[...] = jnp.zeros(ref.shape, ref.dtype)
            for ref, val in zip(a_refs, ra):
                ref[...] += val.astype(F32)

    res = pl.pallas_call(
        body, grid=(R // tr,), in_specs=in_specs, out_specs=out_specs, out_shape=out_shape,
        name=name, compiler_params=_params(("arbitrary",)),
    )(*rows, *consts)
    return list(res)


def whole(fn, ins, outs, name="whole"):
    n_i = len(ins)

    def body(*refs):
        vals = fn(*[r[...] for r in refs[:n_i]])
        for ref, val in zip(refs[n_i:], vals):
            ref[...] = val.astype(ref.dtype)

    res = pl.pallas_call(
        body, out_shape=[jax.ShapeDtypeStruct(tuple(s), dt) for s, dt in outs],
        in_specs=[pl.BlockSpec(memory_space=pltpu.VMEM)] * n_i,
        out_specs=[pl.BlockSpec(memory_space=pltpu.VMEM)] * len(outs),
        name=name, compiler_params=_params(),
    )(*ins)
    return list(res)


_DOT_DIMS = {
    "nn": (((1,), (0,)), ((), ())),
    "nt": (((1,), (1,)), ((), ())),
    "tn": (((0,), (0,)), ((), ())),
}


def mm(a, b, mode="nn", out_dtype=F32, name="mm"):
    if mode == "nn":
        (M, K), (K2, N) = a.shape, b.shape
    elif mode == "nt":
        (M, K), (N, K2) = a.shape, b.shape
    else:
        (K, M), (K2, N) = a.shape, b.shape
    assert K == K2, (mode, a.shape, b.shape)
    tn = _tile(N, MM_TILE_N, LANES)
    if mode == "tn":
        tm, tk = _tile(M, MM_TILE_M, LANES), _tile(K, MM_TILE_M, 2 * SUBLANES)
    else:
        tm, tk = _tile(M, MM_TILE_M, 2 * SUBLANES), _tile(K, MM_TILE_N, LANES)

    def vmem_bytes(tm_, tk_):
        return (2 * (tm_ * tk_ * a.dtype.itemsize + tk_ * tn * b.dtype.itemsize
                     + tm_ * tn * jnp.dtype(out_dtype).itemsize) + tm_ * tn * 4)

    while vmem_bytes(tm, tk) > MM_VMEM_BYTES and tk % (2 * LANES) == 0 and K % (tk // 2) == 0:
        tk //= 2
    while vmem_bytes(tm, tk) > MM_VMEM_BYTES and tm % (2 * LANES) == 0 and M % (tm // 2) == 0:
        tm //= 2
    if mode == "tn":
        a_spec = pl.BlockSpec((tk, tm), lambda i, j, k: (k, i))
    else:
        a_spec = pl.BlockSpec((tm, tk), lambda i, j, k: (i, k))
    if mode == "nt":
        b_spec = pl.BlockSpec((tn, tk), lambda i, j, k: (j, k))
    else:
        b_spec = pl.BlockSpec((tk, tn), lambda i, j, k: (k, j))
    nk = K // tk
    dims = _DOT_DIMS[mode]

    def dot(a_ref, b_ref):
        return lax.dot_general(a_ref[...].astype(MXU_DTYPE), b_ref[...].astype(MXU_DTYPE), dims,
                               preferred_element_type=F32)

    def body_one(a_ref, b_ref, o_ref):
        o_ref[...] = dot(a_ref, b_ref).astype(o_ref.dtype)

    def body_acc(a_ref, b_ref, o_ref, acc_ref):
        k = pl.program_id(2)

        @pl.when(k == 0)
        def _():
            acc_ref[...] = dot(a_ref, b_ref)

        @pl.when(k > 0)
        def _():
            acc_ref[...] += dot(a_ref, b_ref)

        @pl.when(k == nk - 1)
        def _():
            o_ref[...] = acc_ref[...].astype(o_ref.dtype)

    return pl.pallas_call(
        body_one if nk == 1 else body_acc, grid=(M // tm, N // tn, nk), in_specs=[a_spec, b_spec],
        out_specs=pl.BlockSpec((tm, tn), lambda i, j, k: (i, j)),
        out_shape=jax.ShapeDtypeStruct((M, N), out_dtype),
        scratch_shapes=[] if nk == 1 else [pltpu.VMEM((tm, tn), F32)],
        name=name, compiler_params=_params(("parallel", "parallel", "arbitrary")),
    )(a, b)


def mm_blocks(a, b, mode="nn", name="mm_blocks"):
    M = a.shape[0]
    nb = b.shape[0]
    Ka, Nb = (b.shape[1], b.shape[2]) if mode == "nn" else (b.shape[2], b.shape[1])
    tm = _tile(M, MM_TILE_M, 2 * SUBLANES)

    def body(a_ref, b_ref, o_ref):
        o_ref[...] = _dotf(a_ref[...], b_ref[...], mode)

    return pl.pallas_call(
        body, grid=(M // tm, nb),
        in_specs=[pl.BlockSpec((tm, Ka), lambda i, j: (i, j)),
                  pl.BlockSpec((None,) + b.shape[1:], lambda i, j: (j, 0, 0))],
        out_specs=pl.BlockSpec((tm, Nb), lambda i, j: (i, j)), out_shape=jax.ShapeDtypeStruct((M, nb * Nb), F32),
        name=name, compiler_params=_params(("parallel", "parallel")),
    )(a, b)


def mm_blocks_tn(a, d, nb, name="mm_blocks_tn"):
    K = a.shape[0]
    Ma, Nd = a.shape[1] // nb, d.shape[1] // nb
    tk = _tile(K, MM_TILE_M, 2 * SUBLANES)

    def body(a_ref, d_ref, o_ref):
        k = pl.program_id(1)
        r = _dotf(a_ref[...], d_ref[...], "tn")

        @pl.when(k == 0)
        def _():
            o_ref[...] = r

        @pl.when(k > 0)
        def _():
            o_ref[...] += r

    return pl.pallas_call(
        body, grid=(nb, K // tk),
        in_specs=[pl.BlockSpec((tk, Ma), lambda j, k: (k, j)), pl.BlockSpec((tk, Nd), lambda j, k: (k, j))],
        out_specs=pl.BlockSpec((None, Ma, Nd), lambda j, k: (j, 0, 0)),
        out_shape=jax.ShapeDtypeStruct((nb, Ma, Nd), F32), name=name,
        compiler_params=_params(("parallel", "arbitrary")),
    )(a, d)


def _mm_rows_tiles(M, K):
    return _tile(M, MM_TILE_M // 2, 2 * SUBLANES), _tile(K, MM_TILE_N, LANES)


def mm_add_norm(a, b, x, gain, name):
    (M, K), N = a.shape, b.shape[1]
    tm, tk = _mm_rows_tiles(M, K)
    nk = K // tk

    def body(a_ref, b_ref, x_ref, g_ref, xo_ref, n_ref, acc_ref):
        k = pl.program_id(1)
        r = _dotf(a_ref[...], b_ref[...])

        @pl.when(k == 0)
        def _():
            acc_ref[...] = r

        @pl.when(k > 0)
        def _():
            acc_ref[...] += r

        @pl.when(k == nk - 1)
        def _():
            xn = x_ref[...] + acc_ref[...]
            xo_ref[...] = xn
            n_ref[...] = _rms(xn, g_ref[...]).astype(n_ref.dtype)

    row = pl.BlockSpec((tm, N), lambda i, k: (i, 0))
    return pl.pallas_call(
        body, grid=(M // tm, nk),
        in_specs=[pl.BlockSpec((tm, tk), lambda i, k: (i, k)), pl.BlockSpec((tk, N), lambda i, k: (k, 0)), row,
                  pl.BlockSpec((1, N), lambda i, k: (0, 0))],
        out_specs=[row, row], out_shape=[jax.ShapeDtypeStruct((M, N), F32), jax.ShapeDtypeStruct((M, N), BF16)],
        scratch_shapes=[pltpu.VMEM((tm, N), F32)], name=name, compiler_params=_params(("parallel", "arbitrary")),
    )(a, b, x, gain)


def mm_norm_bwd(a, b, x, gain, dx_res, name):
    (M, K), N = a.shape, b.shape[0]
    tm, tk = _mm_rows_tiles(M, K)
    nk = K // tk

    def body(a_ref, b_ref, x_ref, g_ref, r_ref, dx_ref, dgain_ref, acc_ref):
        i, k = pl.program_id(0), pl.program_id(1)
        r = _dotf(a_ref[...], b_ref[...], "nt")

        @pl.when((i == 0) & (k == 0))
        def _():
            dgain_ref[...] = jnp.zeros(dgain_ref.shape, F32)

        @pl.when(k == 0)
        def _():
            acc_ref[...] = r

        @pl.when(k > 0)
        def _():
            acc_ref[...] += r

        @pl.when(k == nk - 1)
        def _():
            dx, dgain = _vjp(_rms, (x_ref[...], g_ref[...]), acc_ref[...])
            dx_ref[...] = r_ref[...] + dx
            dgain_ref[...] += dgain

    row = pl.BlockSpec((tm, N), lambda i, k: (i, 0))
    vec = pl.BlockSpec((1, N), lambda i, k: (0, 0))
    return pl.pallas_call(
        body, grid=(M // tm, nk),
        in_specs=[pl.BlockSpec((tm, tk), lambda i, k: (i, k)), pl.BlockSpec((N, tk), lambda i, k: (0, k)), row, vec, row],
        out_specs=[row, vec], out_shape=[jax.ShapeDtypeStruct((M, N), F32), jax.ShapeDtypeStruct((1, N), F32)],
        scratch_shapes=[pltpu.VMEM((tm, N), F32)], name=name, compiler_params=_params(("arbitrary", "arbitrary")),
    )(a, b, x, gain, dx_res)


def _roll_rows(x, d, reverse):
    return pltpu.roll(x, (SUBLANES - d) if reverse else d, 0)


def scan_real(a, b, reverse=False, name="scan_real"):
    S, W = b.shape
    cw = _pick(W, (256, 128))
    n_tiles = S // SUBLANES

    def body(a_ref, b_ref, o_ref):
        row = lax.broadcasted_iota(jnp.int32, (SUBLANES, cw), 0)
        edge = 0 if reverse else SUBLANES - 1

        def step(i, carry):
            t = (n_tiles - 1 - i) if reverse else i
            off = pl.multiple_of(t * SUBLANES, SUBLANES)
            A = a_ref[pl.ds(off, SUBLANES), :]
            B = b_ref[pl.ds(off, SUBLANES), :]
            for d in (1, 2, 4):
                m = (row < SUBLANES - d) if reverse else (row >= d)
                B = jnp.where(m, A * _roll_rows(B, d, reverse) + B, B)
                A = jnp.where(m, A * _roll_rows(A, d, reverse), A)
            o_ref[pl.ds(off, SUBLANES), :] = B + A * carry
            at_edge = row == edge
            return (jnp.sum(jnp.where(at_edge, B, 0.0), axis=0, keepdims=True)
                    + jnp.sum(jnp.where(at_edge, A, 0.0), axis=0, keepdims=True) * carry)

        lax.fori_loop(0, n_tiles, step, jnp.zeros((1, cw), F32), unroll=2)

    spec = pl.BlockSpec((S, cw), lambda j: (0, j))
    return pl.pallas_call(
        body, grid=(W // cw,), in_specs=[spec, spec], out_specs=spec,
        out_shape=jax.ShapeDtypeStruct((S, W), F32), name=name, compiler_params=_params(("parallel",)),
    )(a, b)


def scan_cplx(lam, bu, reverse=False, name="scan_cplx"):
    S, C = bu.shape
    half = LANES
    CB = _pick(C, (1024, 512, 256))
    TS = _pick(S, (1024, 512, 256, 128, 64, 32, 16, 8))
    groups = CB // (2 * half)
    n_blocks, n_tiles = S // TS, TS // SUBLANES

    def cmul(ar, ai, br, bi):
        return ar * br - ai * bi, ar * bi + ai * br

    def body(lam_ref, bu_ref, o_ref, carry_ref):
        row = lax.broadcasted_iota(jnp.int32, (SUBLANES, half), 0)

        def edge_row(v):
            return jnp.sum(jnp.where(row == (0 if reverse else SUBLANES - 1), v, 0.0), axis=0, keepdims=True)

        @pl.when(pl.program_id(1) == 0)
        def _():
            carry_ref[...] = jnp.zeros(carry_ref.shape, F32)

        consts = []
        for g in range(groups):
            lr = lam_ref[:, 2 * half * g:2 * half * g + half]
            li = lam_ref[:, 2 * half * g + half:2 * half * (g + 1)]
            if reverse:
                li = -li
            l1 = (lr, li)
            l2 = cmul(*l1, *l1)
            l4 = cmul(*l2, *l2)
            pr = jnp.zeros((SUBLANES, half), F32)
            pi = jnp.zeros((SUBLANES, half), F32)
            p = l1
            for r in range(SUBLANES):
                sel = row == ((SUBLANES - 1 - r) if reverse else r)
                pr = jnp.where(sel, p[0], pr)
                pi = jnp.where(sel, p[1], pi)
                p = cmul(*p, *l1)
            consts.append((l1, l2, l4, pr, pi, edge_row(pr), edge_row(pi)))

        def step(i, carry):
            t = (n_tiles - 1 - i) if reverse else i
            off = pl.multiple_of(t * SUBLANES, SUBLANES)
            out = []
            for g in range(groups):
                l1, l2, l4, pr, pi, p8r, p8i = consts[g]
                cr, ci = carry[2 * g], carry[2 * g + 1]
                re, im = pl.ds(2 * half * g, half), pl.ds(2 * half * g + half, half)
                Br = bu_ref[pl.ds(off, SUBLANES), re]
                Bi = bu_ref[pl.ds(off, SUBLANES), im]
                for d, (qr, qi) in ((1, l1), (2, l2), (4, l4)):
                    m = (row < SUBLANES - d) if reverse else (row >= d)
                    sr, si = _roll_rows(Br, d, reverse), _roll_rows(Bi, d, reverse)
                    nr = jnp.where(m, Br + qr * sr - qi * si, Br)
                    ni = jnp.where(m, Bi + qr * si + qi * sr, Bi)
                    Br, Bi = nr, ni
                o_ref[pl.ds(off, SUBLANES), re] = Br + pr * cr - pi * ci
                o_ref[pl.ds(off, SUBLANES), im] = Bi + pr * ci + pi * cr
                er, ei = edge_row(Br), edge_row(Bi)
                out += [er + p8r * cr - p8i * ci, ei + p8r * ci + p8i * cr]
            return tuple(out)

        carry0 = tuple(carry_ref[:, pl.ds(half * k, half)] for k in range(2 * groups))
        carry1 = lax.fori_loop(0, n_tiles, step, carry0, unroll=2)
        for k in range(2 * groups):
            carry_ref[:, pl.ds(half * k, half)] = carry1[k]

    def rows(j, t):
        return ((n_blocks - 1 - t) if reverse else t, j)

    spec = pl.BlockSpec((TS, CB), rows)
    return pl.pallas_call(
        body, grid=(C // CB, n_blocks), in_specs=[pl.BlockSpec((1, CB), lambda j, t: (0, j)), spec],
        out_specs=spec, out_shape=jax.ShapeDtypeStruct((S, C), F32), scratch_shapes=[pltpu.VMEM((1, CB), F32)],
        name=name, compiler_params=_params(("parallel", "arbitrary")),
    )(lam, bu)


ATTN_HEADS_PER_STEP = 2


def _attn_tile(S, window):
    if window is None:
        return _pick(S, (512, 256, 128))
    return max(window, _pick(S, (256, 128)))


def _attn_valid(q_blk, k_blk, T, window):
    kpos = k_blk * T + lax.broadcasted_iota(jnp.int32, (T, T), 0)
    qpos = q_blk * T + lax.broadcasted_iota(jnp.int32, (T, T), 1)
    valid = kpos <= qpos
    if window is not None:
        valid = valid & (qpos - kpos < window)
    return valid


def attn_fwd(q, k, v, sink, cq=None, ck=None, window=None, name="attn_fwd"):
    H, S, Dh = q.shape
    G = H // k.shape[0]
    HP = ATTN_HEADS_PER_STEP
    assert H % HP == 0 and (G == 1 or G % HP == 0)
    KP = HP if G == 1 else 1
    T = _attn_tile(S, window)
    nq = S // T
    nks = nq if window is None else 2
    scale = Dh ** -0.5
    bias = cq is not None

    def kv_block(i, j):
        return jnp.minimum(j, i) if window is None else jnp.maximum(i - 1 + j, 0)

    def body(*refs):
        if bias:
            q_ref, k_ref, v_ref, s_ref, cq_ref, ck_ref, o_ref, lse_ref, m_scr, l_scr, acc_scr = refs
        else:
            q_ref, k_ref, v_ref, s_ref, o_ref, lse_ref, m_scr, l_scr, acc_scr = refs
        i, j = pl.program_id(1), pl.program_id(2)

        @pl.when(j == 0)
        def _():
            m_scr[...] = jnp.zeros(m_scr.shape, F32) + s_ref[...]
            l_scr[...] = jnp.ones(l_scr.shape, F32)
            acc_scr[...] = jnp.zeros(acc_scr.shape, F32)

        def block(masked):
            valid = _attn_valid(i, kv_block(i, j), T, window) if masked else None
            for b in range(HP):
                kvb = b if G == 1 else 0
                s = _dotf(k_ref[kvb], q_ref[b], "nt") * scale
                if bias:
                    s = s + cq_ref[b] - ck_ref[b]
                if masked:
                    s = jnp.where(valid, s, NEG)
                m_old = m_scr[b]
                m_new = jnp.maximum(m_old, jnp.max(s, axis=0, keepdims=True))
                alpha = jnp.exp(m_old - m_new)
                p = jnp.exp(s - m_new)
                l_scr[b] = alpha * l_scr[b] + jnp.sum(p, axis=0, keepdims=True)
                acc_scr[b] = alpha * acc_scr[b] + _dotf(v_ref[kvb], p, "tn")
                m_scr[b] = m_new

        if window is None:
            pl.when(j < i)(lambda: block(False))
            pl.when(j == i)(lambda: block(True))
        else:
            pl.when(i - 1 + j >= 0)(lambda: block(True))

        @pl.when(j == nks - 1)
        def _():
            o_ref[...] = acc_scr[...] / l_scr[...]
            lse_ref[...] = m_scr[...] + jnp.log(l_scr[...])

    def kv_map(hp, i, j):
        return (hp if G == 1 else (hp * HP) // G, kv_block(i, j), 0)

    in_specs = [
        pl.BlockSpec((HP, T, Dh), lambda hp, i, j: (hp, i, 0)),
        pl.BlockSpec((KP, T, Dh), kv_map),
        pl.BlockSpec((KP, T, Dh), kv_map),
        pl.BlockSpec((HP, 1, 1), lambda hp, i, j: (hp, 0, 0)),
    ]
    args = [q, k, v, sink]
    if bias:
        in_specs += [pl.BlockSpec((HP, 1, T), lambda hp, i, j: (hp, 0, i)),
                     pl.BlockSpec((HP, T, 1), lambda hp, i, j: (hp, kv_block(i, j), 0))]
        args += [cq, ck]
    return pl.pallas_call(
        body, grid=(H // HP, nq, nks), in_specs=in_specs,
        out_specs=[pl.BlockSpec((HP, Dh, T), lambda hp, i, j: (hp, 0, i)),
                   pl.BlockSpec((HP, 1, T), lambda hp, i, j: (hp, 0, i))],
        out_shape=[jax.ShapeDtypeStruct((H, Dh, S), F32), jax.ShapeDtypeStruct((H, 1, S), F32)],
        scratch_shapes=[pltpu.VMEM((HP, 1, T), F32), pltpu.VMEM((HP, 1, T), F32), pltpu.VMEM((HP, Dh, T), F32)],
        name=name, compiler_params=_params(("parallel", "parallel", "arbitrary")),
    )(*args)


def attn_bwd(q, k, v, lse, do, delta, cq=None, ck=None, window=None, name="attn_bwd"):
    H, S, Dh = q.shape
    KVH = k.shape[0]
    G = H // KVH
    HP = ATTN_HEADS_PER_STEP
    assert H % HP == 0 and (G == 1 or G % HP == 0)
    pair_kv = G == 1
    KP = HP if pair_kv else 1
    T = _attn_tile(S, window)
    nq = S // T
    nqs = nq if window is None else 2
    scale = Dh ** -0.5
    bias = cq is not None
    assert not bias or G == 1

    def q_block(kb, j):
        return jnp.maximum(j, kb) if window is None else jnp.minimum(kb + j, nq - 1)

    def body(*refs):
        if bias:
            (q_ref, k_ref, v_ref, lse_ref, do_ref, dl_ref, cq_ref, ck_ref,
             dq_ref, dk_ref, dv_ref, dcq_ref, dck_ref) = refs
        else:
            q_ref, k_ref, v_ref, lse_ref, do_ref, dl_ref, dq_ref, dk_ref, dv_ref = refs
        kb, gp, j = pl.program_id(1), pl.program_id(2), pl.program_id(3)

        @pl.when((gp == 0) & (j == 0))
        def _():
            dk_ref[...] = jnp.zeros(dk_ref.shape, F32)
            dv_ref[...] = jnp.zeros(dv_ref.shape, F32)
            if bias:
                dck_ref[...] = jnp.zeros(dck_ref.shape, F32)

        @pl.when((kb == 0) & (gp == 0) & (j == 0))
        def _():
            dq_ref[...] = jnp.zeros(dq_ref.shape, F32)
            if bias:
                dcq_ref[...] = jnp.zeros(dcq_ref.shape, F32)

        def block(masked):
            qi = q_block(kb, j)
            off = pl.multiple_of(qi * T, T)
            valid = _attn_valid(qi, kb, T, window) if masked else None
            for b in range(HP):
                kvb = b if pair_kv else 0
                g = 0 if pair_kv else gp * HP + b
                qb, kk, vv = q_ref[b].astype(MXU_DTYPE), k_ref[kvb].astype(MXU_DTYPE), v_ref[kvb].astype(MXU_DTYPE)
                dob = do_ref[b].astype(MXU_DTYPE)
                s = _dotf(kk, qb, "nt") * scale
                if bias:
                    s = s + cq_ref[b] - ck_ref[b]
                if masked:
                    s = jnp.where(valid, s, NEG)
                p = jnp.exp(s - lse_ref[b])
                dv_ref[kvb] += _dotf(p, dob, "nt")
                ds = p * (_dotf(vv, dob) - dl_ref[b])
                dsb = ds.astype(MXU_DTYPE)
                dk_ref[kvb] += scale * _dotf(dsb, qb)
                dq_ref[kvb, g, pl.ds(off, T), :] += scale * _dotf(dsb, kk, "tn")
                if bias:
                    dcq_ref[kvb, g, :, pl.ds(off, T)] += jnp.sum(ds, axis=0, keepdims=True)
                    dck_ref[kvb] -= jnp.sum(ds, axis=1, keepdims=True)

        if window is None:
            pl.when(j > kb)(lambda: block(False))
            pl.when(j == kb)(lambda: block(True))
        else:
            pl.when(kb + j <= nq - 1)(lambda: block(True))

    def qmap(kvp, kb, gp, j):
        return (kvp if pair_kv else (kvp * G) // HP + gp, q_block(kb, j), 0)

    def qmap_t(kvp, kb, gp, j):
        return (kvp if pair_kv else (kvp * G) // HP + gp, 0, q_block(kb, j))

    in_specs = [
        pl.BlockSpec((HP, T, Dh), qmap),
        pl.BlockSpec((KP, T, Dh), lambda kvp, kb, gp, j: (kvp, kb, 0)),
        pl.BlockSpec((KP, T, Dh), lambda kvp, kb, gp, j: (kvp, kb, 0)),
        pl.BlockSpec((HP, 1, T), qmap_t),
        pl.BlockSpec((HP, Dh, T), qmap_t),
        pl.BlockSpec((HP, 1, T), qmap_t),
    ]
    args = [q, k, v, lse, do, delta]
    out_specs = [
        pl.BlockSpec((KP, G, S, Dh), lambda kvp, kb, gp, j: (kvp, 0, 0, 0)),
        pl.BlockSpec((KP, T, Dh), lambda kvp, kb, gp, j: (kvp, kb, 0)),
        pl.BlockSpec((KP, T, Dh), lambda kvp, kb, gp, j: (kvp, kb, 0)),
    ]
    out_shape = [jax.ShapeDtypeStruct((KVH, G, S, Dh), F32), jax.ShapeDtypeStruct((KVH, S, Dh), F32),
                 jax.ShapeDtypeStruct((KVH, S, Dh), F32)]
    if bias:
        in_specs += [pl.BlockSpec((HP, 1, T), qmap_t),
                     pl.BlockSpec((HP, T, 1), lambda kvp, kb, gp, j: (kvp, kb, 0))]
        args += [cq, ck]
        out_specs += [pl.BlockSpec((KP, G, 1, S), lambda kvp, kb, gp, j: (kvp, 0, 0, 0)),
                      pl.BlockSpec((KP, T, 1), lambda kvp, kb, gp, j: (kvp, kb, 0))]
        out_shape += [jax.ShapeDtypeStruct((KVH, G, 1, S), F32), jax.ShapeDtypeStruct((KVH, S, 1), F32)]
    res = pl.pallas_call(
        body, grid=(KVH // KP, nq, 1 if pair_kv else G // HP, nqs), in_specs=in_specs, out_specs=out_specs,
        out_shape=out_shape, name=name, compiler_params=_params(("arbitrary", "arbitrary", "arbitrary", "arbitrary")),
    )(*args)
    dq = res[0].reshape(H, S, Dh)
    if bias:
        return dq, res[1], res[2], res[3].reshape(H, 1, S), res[4]
    return dq, res[1], res[2]


def _rms(x, g):
    return x * lax.rsqrt(jnp.mean(x * x, axis=-1, keepdims=True) + EPS) * g


def _sigmoid(x):
    return 1.0 / (1.0 + jnp.exp(-x))


def _softplus(x):
    return jnp.maximum(x, 0.0) + jnp.log(1.0 + jnp.exp(-jnp.abs(x)))


def _log_sigmoid(x):
    return jnp.minimum(x, 0.0) - jnp.log(1.0 + jnp.exp(-jnp.abs(x)))


def _gelu(x):
    return 0.5 * x * (1.0 + jnp.tanh(math.sqrt(2.0 / math.pi) * (x + 0.044715 * (x * x * x))))


def _silu(x):
    return x * _sigmoid(x)


def _ffn_act(gu):
    f = gu.shape[1] // 2
    return MACARON * _silu(gu[:, :f]) * gu[:, f:]


def _qk_prep(rope):
    def f(x, *rest):
        if rope:
            cos, sin, g, rot = rest
        else:
            (g,) = rest
        y = _rms(x, g)
        if rope:
            y = y * cos + jnp.dot(y, rot, precision=HI, preferred_element_type=F32) * sin
        return y
    return f


def _lru_gates(pre, xc, ba, bx, lam):
    w = xc.shape[1]
    r = _sigmoid(pre[:, :w] + ba)
    i = _sigmoid(pre[:, w:] + bx)
    log_a = -LRU_C * r * _softplus(lam)
    a = jnp.exp(log_a)
    b = jnp.sqrt(1.0 - jnp.exp(2.0 * log_a)) * (i * xc)
    return a, b


def _lru_conv(x0, x1, x2, x3, w0, w1, w2, w3, cb):
    return cb + x0 * w0 + x1 * w1 + x2 * w2 + x3 * w3


def _s5_params(lre, lim, ldt, gsel, bre, bim):
    dt = jnp.sum(gsel * jnp.exp(ldt), axis=1, keepdims=True)
    er = jnp.exp(lre * dt)
    ang = lim * dt
    lbr, lbi = er * jnp.cos(ang), er * jnp.sin(ang)
    nr, ni = lbr - 1.0, lbi
    den = lre * lre + lim * lim
    fr, fi = (nr * lre + ni * lim) / den, (ni * lre - nr * lim) / den
    return lbr, lbi, fr * bre - fi * bim, fr * bim + fi * bre


def _s5_out(yssm, u, d):
    return _gelu(yssm + d * u)


def _glu(z, gl, gb):
    return z * _sigmoid(gl + gb)


def _ple_out(x, gpre, epre, pn):
    return x + _sigmoid(gpre) * _rms(epre, pn)


def _vjp(fn, args, cots):
    _, pull = jax.vjp(fn, *args)
    return pull(cots)


def add_norm(x, y, g, name):
    D = x.shape[1]
    if y is None:
        return x, rowwise(lambda xv, gv: ([_rms(xv, gv)], []), [x], [g], outs=[(D, BF16)], name=name)[0]
    xn, n = rowwise(lambda xv, yv, gv: ([xv + yv, _rms(xv + yv, gv)], []), [x, y], [g],
                    outs=[(D, F32), (D, BF16)], name=name)
    return xn, n


def norm_bwd(x, g, dn, dx_res, name):
    D = x.shape[1]

    def f(xv, dnv, dxv, gv):
        dx, dg = _vjp(_rms, (xv, gv), dnv)
        return [dxv + dx], [dg]

    return rowwise(f, [x, dn, dx_res], [g], outs=[(D, F32)], accs=[(1, D)], name=name)


def _swiglu(g, u):
    return MACARON * _silu(g) * u


def _dotf(a, b, mode="nn"):
    return lax.dot_general(a.astype(MXU_DTYPE), b.astype(MXU_DTYPE), _DOT_DIMS[mode], preferred_element_type=F32)


def ffn_up(n, wgu, ig, iu, name):
    S, D = n.shape
    C, _, _, Fc = wgu.shape
    tm = _tile(S, MM_TILE_M, 2 * SUBLANES)

    def body(n_ref, wg_ref, wu_ref, g_ref, u_ref, a_ref):
        g = _dotf(n_ref[...], wg_ref[...])
        u = _dotf(n_ref[...], wu_ref[...])
        g_ref[...] = g.astype(g_ref.dtype)
        u_ref[...] = u.astype(u_ref.dtype)
        a_ref[...] = _swiglu(g, u).astype(a_ref.dtype)

    hid = pl.BlockSpec((None, tm, Fc), lambda s, i: (s, i, 0))
    return pl.pallas_call(
        body, grid=(C, S // tm),
        in_specs=[pl.BlockSpec((tm, D), lambda s, i: (i, 0)),
                  pl.BlockSpec((None, None, D, Fc), lambda s, i: (s, ig, 0, 0)),
                  pl.BlockSpec((None, None, D, Fc), lambda s, i: (s, iu, 0, 0))],
        out_specs=[hid, hid, hid], out_shape=[jax.ShapeDtypeStruct((C, S, Fc), BF16)] * 3,
        name=name, compiler_params=_params(("parallel", "parallel")),
    )(n, wgu, wgu)


def ffn_down(act, wd, iw, x, gain, name):
    C, S, Fc = act.shape
    D = wd.shape[-1]
    tm = _tile(S, MM_TILE_M, 2 * SUBLANES)

    def body(a_ref, w_ref, x_ref, g_ref, xo_ref, n_ref, acc_ref):
        s = pl.program_id(1)
        r = _dotf(a_ref[...], w_ref[...])

        @pl.when(s == 0)
        def _():
            acc_ref[...] = r

        @pl.when(s > 0)
        def _():
            acc_ref[...] += r

        @pl.when(s == C - 1)
        def _():
            xn = x_ref[...] + acc_ref[...]
            xo_ref[...] = xn
            n_ref[...] = _rms(xn, g_ref[...]).astype(n_ref.dtype)

    row = pl.BlockSpec((tm, D), lambda i, s: (i, 0))
    return pl.pallas_call(
        body, grid=(S // tm, C),
        in_specs=[pl.BlockSpec((None, tm, Fc), lambda i, s: (s, i, 0)),
                  pl.BlockSpec((None, None, Fc, D), lambda i, s: (s, iw, 0, 0)), row,
                  pl.BlockSpec((1, D), lambda i, s: (0, 0))],
        out_specs=[row, row], out_shape=[jax.ShapeDtypeStruct((S, D), F32), jax.ShapeDtypeStruct((S, D), BF16)],
        scratch_shapes=[pltpu.VMEM((tm, D), F32)], name=name, compiler_params=_params(("parallel", "arbitrary")),
    )(act, wd, x, gain)


def ffn_down_bwd(dy, wd, iw, g, u, name):
    C, S, Fc = g.shape
    D = dy.shape[1]
    tm = _tile(S, MM_TILE_M, 2 * SUBLANES)

    def body(dy_ref, w_ref, g_ref, u_ref, dg_ref, du_ref):
        dact = MACARON * _dotf(dy_ref[...], w_ref[...], "nt")
        g, u = g_ref[...].astype(F32), u_ref[...].astype(F32)
        sg = _sigmoid(g)
        gs = g * sg
        dg_ref[...] = (dact * u * (sg + gs * (1.0 - sg))).astype(dg_ref.dtype)
        du_ref[...] = (dact * gs).astype(du_ref.dtype)

    hid = pl.BlockSpec((None, tm, Fc), lambda s, i: (s, i, 0))
    return pl.pallas_call(
        body, grid=(C, S // tm),
        in_specs=[pl.BlockSpec((tm, D), lambda s, i: (i, 0)),
                  pl.BlockSpec((None, None, Fc, D), lambda s, i: (s, iw, 0, 0)), hid, hid],
        out_specs=[hid, hid], out_shape=[jax.ShapeDtypeStruct((C, S, Fc), BF16)] * 2,
        name=name, compiler_params=_params(("parallel", "parallel")),
    )(dy, wd, g, u)


def ffn_dn(dg, du, wgu, ig, iu, x, gain, dx_res, name):
    C, S, Fc = dg.shape
    D = wgu.shape[2]
    tm = _tile(S, MM_TILE_M, 2 * SUBLANES)
    parts = 2 if tm % (4 * SUBLANES) == 0 else 1

    def body(dg_ref, du_ref, wg_ref, wu_ref, x_ref, g_ref, r_ref, dx_ref, dgain_ref, acc_ref):
        i, s = pl.program_id(0), pl.program_id(1)
        r = _dotf(dg_ref[...], wg_ref[...], "nt") + _dotf(du_ref[...], wu_ref[...], "nt")

        @pl.when((i == 0) & (s == 0))
        def _():
            dgain_ref[...] = jnp.zeros(dgain_ref.shape, F32)

        @pl.when(s == 0)
        def _():
            acc_ref[...] = r

        @pl.when(s > 0)
        def _():
            acc_ref[...] += r

        @pl.when(s == C - 1)
        def _():
            for part in range(parts):
                rows = pl.ds(part * (tm // parts), tm // parts)
                dx, dgain = _vjp(_rms, (x_ref[rows, :], g_ref[...]), acc_ref[rows, :])
                dx_ref[rows, :] = r_ref[rows, :] + dx
                dgain_ref[...] += dgain

    hid = pl.BlockSpec((None, tm, Fc), lambda i, s: (s, i, 0))
    row = pl.BlockSpec((tm, D), lambda i, s: (i, 0))
    vec = pl.BlockSpec((1, D), lambda i, s: (0, 0))
    return pl.pallas_call(
        body, grid=(S // tm, C),
        in_specs=[hid, hid, pl.BlockSpec((None, None, D, Fc), lambda i, s: (s, ig, 0, 0)),
                  pl.BlockSpec((None, None, D, Fc), lambda i, s: (s, iu, 0, 0)), row, vec, row],
        out_specs=[row, vec], out_shape=[jax.ShapeDtypeStruct((S, D), F32), jax.ShapeDtypeStruct((1, D), F32)],
        scratch_shapes=[pltpu.VMEM((tm, D), F32)], name=name, compiler_params=_params(("arbitrary", "arbitrary")),
    )(dg, du, wgu, wgu, x, gain, dx_res)


def ffn_dw(a, d, buf, idx, shape, blocked, name):
    C, P, M, N = shape
    S = d.shape[-2]
    tk = _tile(S, MM_TILE_M, 2 * SUBLANES)
    nk = S // tk

    def body(*refs):
        a_ref, d_ref, o_ref, acc_ref = refs[0], refs[1], refs[-2], refs[-1]
        k = pl.program_id(1)
        r = _dotf(a_ref[...], d_ref[...], "tn")

        @pl.when(k == 0)
        def _():
            acc_ref[...] = r

        @pl.when(k > 0)
        def _():
            acc_ref[...] += r

        @pl.when(k == nk - 1)
        def _():
            o_ref[...] = acc_ref[...].astype(o_ref.dtype)

    if blocked == "a":
        a_spec = pl.BlockSpec((None, tk, M), lambda s, k: (s, k, 0))
        d_spec = pl.BlockSpec((tk, N), lambda s, k: (k, 0))
    else:
        a_spec = pl.BlockSpec((tk, M), lambda s, k: (k, 0))
        d_spec = pl.BlockSpec((None, tk, N), lambda s, k: (s, k, 0))
    out_spec = pl.BlockSpec((None, None, M, N), lambda s, k: (s, idx, 0, 0))
    out_shape = jax.ShapeDtypeStruct(tuple(shape), BF16)
    scratch = [pltpu.VMEM((M, N), F32)]
    if buf is None:
        return pl.pallas_call(body, grid=(C, nk), in_specs=[a_spec, d_spec], out_specs=out_spec, out_shape=out_shape,
                              scratch_shapes=scratch, name=name,
                              compiler_params=_params(("parallel", "arbitrary")))(a, d)
    return pl.pallas_call(body, grid=(C, nk), in_specs=[a_spec, d_spec, pl.BlockSpec(memory_space=pl.ANY)],
                          out_specs=out_spec, out_shape=out_shape, input_output_aliases={2: 0},
                          scratch_shapes=scratch, name=name,
                          compiler_params=_params(("parallel", "arbitrary")))(a, d, buf)


def ffn_fwd(n, x, gain, wgu, wd, ig, iu, iw, tag):
    g, u, act = ffn_up(n, wgu, ig, iu, f"ffn_up_{tag}")
    x_new, n_new = ffn_down(act, wd, iw, x, gain, f"ffn_down_{tag}")
    return x_new, n_new, (n, g, u, act)


def ffn_bwd(dy, saved, x, gain, wgu, wd, ig, iu, iw, gbuf, tag):
    n, g, u, act = saved
    gwgu, gwd = gbuf
    dg, du = ffn_down_bwd(dy, wd, iw, g, u, f"ffn_down_bwd_{tag}")
    gwd = ffn_dw(act, dy, gwd, iw, (N_CHIPS,) + wd.shape[1:], "a", f"ffn_dwd_{tag}")
    dx, dgain = ffn_dn(dg, du, wgu, ig, iu, x, gain, dy, f"ffn_dn_{tag}")
    gwgu = ffn_dw(n, dg, gwgu, ig, (N_CHIPS,) + wgu.shape[1:], "d", f"ffn_dwg_{tag}")
    gwgu = ffn_dw(n, du, gwgu, iu, (N_CHIPS,) + wgu.shape[1:], "d", f"ffn_dwu_{tag}")
    return dx, dgain, (gwgu, gwd)


def _heads(x, H):
    S = x.shape[0]
    return x.reshape(S, H, HEAD_DIM).transpose(1, 0, 2)


def _unheads(x):
    H, S, _ = x.shape
    return x.transpose(1, 0, 2).reshape(S, H * HEAD_DIM)


def _heads_t(x, H):
    return x.T.reshape(H, HEAD_DIM, x.shape[0])


def _unheads_t(x):
    return x.reshape(x.shape[0] * x.shape[1], x.shape[2]).T


def _shift_down(x, n=1):
    return jnp.pad(x, ((n, 0), (0, 0)))[:x.shape[0]]


def _shift_up(x, n=1):
    return jnp.pad(x, ((0, n), (0, 0)))[n:]


def _block_diag(w):
    B, I, J = w.shape
    eye = jnp.eye(B, dtype=w.dtype)
    return (w[:, :, None, :] * eye[:, None, :, None]).reshape(B * I, B * J)


def _block_diag_take(x, B):
    I, J = x.shape[0] // B, x.shape[1] // B
    eye = jnp.eye(B, dtype=x.dtype)
    return jnp.sum(x.reshape(B, I, B, J) * eye[:, None, :, None], axis=2)


def _rope_tables(S):
    half = HEAD_DIM // 2
    inv = jnp.power(ROPE_THETA, -jnp.arange(half, dtype=F32) / half)
    ang = jnp.arange(S, dtype=F32)[:, None] * inv[None, :]
    cos = jnp.concatenate([jnp.cos(ang), jnp.cos(ang)], axis=1)
    sin = jnp.concatenate([jnp.sin(ang), jnp.sin(ang)], axis=1)
    r = jnp.arange(HEAD_DIM)[:, None]
    c = jnp.arange(HEAD_DIM)[None, :]
    rot = jnp.where(r == c + half, -1.0, 0.0) + jnp.where(c == r + half, 1.0, 0.0)
    return cos, sin, rot.astype(F32)


def qk_prep_fwd(x_hm, g, rope_tabs, name):
    H, S, Dh = x_hm.shape
    rows = [x_hm.reshape(H * S, Dh)]
    consts = [g.reshape(1, Dh)]
    periods = [None]
    if rope_tabs is not None:
        rows += [rope_tabs[0], rope_tabs[1]]
        consts += [rope_tabs[2]]
        periods += [S, S]
    fn = _qk_prep(rope_tabs is not None)
    y = rowwise(lambda *a: ([fn(*a)], []), rows, consts, outs=[(Dh, F32)], name=name, periods=periods)[0]
    return y.reshape(H, S, Dh)


def qk_prep_bwd(x_hm, g, rope_tabs, dy_hm, name):
    H, S, Dh = x_hm.shape
    rope = rope_tabs is not None
    rows = [x_hm.reshape(H * S, Dh), dy_hm.reshape(H * S, Dh)]
    consts = [g.reshape(1, Dh)]
    periods = [None, None]
    if rope:
        rows += [rope_tabs[0], rope_tabs[1]]
        consts += [rope_tabs[2]]
        periods += [S, S]
    fn = _qk_prep(rope)

    def f(xv, dyv, *rest):
        if rope:
            cos, sin, gv, rot = rest
            dx, dg = _vjp(lambda a, b: fn(a, cos, sin, b, rot), (xv, gv), dyv)
        else:
            (gv,) = rest
            dx, dg = _vjp(fn, (xv, gv), dyv)
        return [dx], [dg]

    dx, dg = rowwise(f, rows, consts, outs=[(Dh, F32)], accs=[(1, Dh)], name=name, periods=periods)
    return dx.reshape(H, S, Dh), dg.reshape(Dh)


def attn_delta(do_t, o_t, name):
    H, Dh, S = o_t.shape

    def body(a_ref, b_ref, o_ref):
        o_ref[...] = jnp.sum(a_ref[...] * b_ref[...], axis=0, keepdims=True)

    spec = pl.BlockSpec((None, Dh, S), lambda h: (h, 0, 0))
    return pl.pallas_call(
        body, grid=(H,), in_specs=[spec, spec], out_specs=pl.BlockSpec((None, 1, S), lambda h: (h, 0, 0)),
        out_shape=jax.ShapeDtypeStruct((H, 1, S), F32), name=name, compiler_params=_params(("parallel",)),
    )(do_t, o_t)


def even_mixer_fwd(h, w, tag):
    S = h.shape[0]
    W = 512
    H = 8
    z = mm(h, w["w_in"], name=f"ev_in_{tag}")
    xa, ya, q, k, v, f = (z[:, 0:512], z[:, 512:1024], z[:, 1024:1536], z[:, 1536:2048], z[:, 2048:2560],
                          z[:, 2560:2688])
    xs = [_shift_down(xa, LRU_CONV - 1 - tap) for tap in range(LRU_CONV)]
    taps = [w["conv_w"][tap][None] for tap in range(LRU_CONV)]
    xc = rowwise(lambda *a: ([_lru_conv(*a)], []), xs, taps + [w["conv_b"]], outs=[(W, F32)],
                 name=f"lru_conv_{tag}")[0]
    pre = mm(xc, w["w_ax"], name=f"lru_gates_mm_{tag}")
    a, b = rowwise(lambda p_, x_, ba, bx, lam: (list(_lru_gates(p_, x_, ba, bx, lam)), []), [pre, xc],
                   [w["ba"], w["bx"], w["lam"]], outs=[(W, F32), (W, F32)], name=f"lru_gates_{tag}")
    hs = scan_real(a, b, name=f"lru_scan_{tag}")
    a_out = rowwise(lambda y_, h_: ([_gelu(y_) * h_], []), [ya, hs], outs=[(W, F32)], name=f"lru_out_{tag}")[0]
    lf = rowwise(lambda f_, bf: ([_log_sigmoid(f_ + bf)], []), [f], [w["bf"]], outs=[(LANES, F32)],
                 name=f"fox_logf_{tag}")[0]
    c = scan_real(jnp.ones_like(lf), lf, name=f"fox_cumsum_{tag}")
    c_hm = c[:, :H].T
    q_hm, k_hm, v_hm = _heads(q, H), _heads(k, H), _heads(v, H)
    qn = qk_prep_fwd(q_hm, w["qn"], None, f"fox_qprep_{tag}")
    kn = qk_prep_fwd(k_hm, w["kn"], None, f"fox_kprep_{tag}")
    sink = jnp.full((H, 1, 1), NEG, F32)
    o_hm, lse = attn_fwd(qn, kn, v_hm, sink, c_hm[:, None, :], c_hm[:, :, None], name=f"fox_attn_{tag}")
    mo = jnp.concatenate([a_out, _unheads_t(o_hm)], axis=1).astype(BF16)
    saved = dict(h=h, xs=xs, xc=xc, pre=pre, a=a, hs=hs, ya=ya, f=f, c_hm=c_hm, q_hm=q_hm, k_hm=k_hm,
                 v_hm=v_hm, qn=qn, kn=kn, o_hm=o_hm, lse=lse, mo=mo)
    return mo, saved


def even_mixer_bwd(dy, sv, w, tag):
    W = 512
    H = 8
    S = dy.shape[0]
    g = {}
    dmo = mm(dy, w["w_out"], "nt", name=f"ev_dmo_{tag}")
    g["w_out"] = mm(sv["mo"], dy, "tn", name=f"ev_dwout_{tag}")
    da_out, do = dmo[:, :W], dmo[:, W:]
    do_hm = _heads_t(do, H)
    delta = attn_delta(do_hm, sv["o_hm"], f"fox_delta_{tag}")
    c_hm = sv["c_hm"]
    dqn, dkn, dv_hm, dcq, dck = attn_bwd(sv["qn"], sv["kn"], sv["v_hm"], sv["lse"], do_hm, delta,
                                          c_hm[:, None, :], c_hm[:, :, None], name=f"fox_attn_bwd_{tag}")
    dq_hm, g["qn"] = qk_prep_bwd(sv["q_hm"], w["qn"], None, dqn, f"fox_qprep_bwd_{tag}")
    dk_hm, g["kn"] = qk_prep_bwd(sv["k_hm"], w["kn"], None, dkn, f"fox_kprep_bwd_{tag}")
    dc = (dcq[:, 0, :] + dck[:, :, 0]).T
    dc = jnp.pad(dc, ((0, 0), (0, LANES - H)))
    dlf = scan_real(jnp.ones_like(dc), dc, reverse=True, name=f"fox_cumsum_bwd_{tag}")

    def f_logf(f_, d_, bf):
        df, dbf = _vjp(lambda a_, b_: _log_sigmoid(a_ + b_), (f_, bf), d_)
        return [df], [dbf]

    df, dbf = rowwise(f_logf, [sv["f"], dlf], [w["bf"]], outs=[(LANES, F32)], accs=[(1, LANES)],
                      name=f"fox_logf_bwd_{tag}")
    g["bf"] = dbf[0, :H]
    def f_out(y_, h_, d_):
        dyv, dhv = _vjp(lambda a_, b_: _gelu(a_) * b_, (y_, h_), d_)
        return [dyv, dhv], []

    dya, dhs = rowwise(f_out, [sv["ya"], sv["hs"], da_out], outs=[(W, F32), (W, F32)], name=f"lru_out_bwd_{tag}")
    gs = scan_real(_shift_up(sv["a"]), dhs, reverse=True, name=f"lru_scan_bwd_{tag}")

    def f_gates(p_, x_, g_, hp_, ba, bx, lam):
        dp, dx, dba, dbx, dlam = _vjp(_lru_gates, (p_, x_, ba, bx, lam), (g_ * hp_, g_))
        return [dp, dx], [dba, dbx, dlam]

    dpre, dxc, dba, dbx, dlam = rowwise(f_gates, [sv["pre"], sv["xc"], gs, _shift_down(sv["hs"])],
                                        [w["ba"], w["bx"], w["lam"]], outs=[(2 * W, BF16), (W, F32)],
                                        accs=[(1, W)] * 3, name=f"lru_gates_bwd_{tag}")
    g["ba"], g["bx"], g["lam"] = dba[0], dbx[0], dlam[0]
    dxc2 = mm(dpre, w["w_ax"], "nt", name=f"lru_gates_mm_dx_{tag}")
    g["w_ax"] = mm(sv["xc"], dpre, "tn", name=f"lru_gates_mm_dw_{tag}")

    def f_conv(d1, d2, x0, x1, x2, x3):
        d = d1 + d2
        return [d], [jnp.sum(d, axis=0, keepdims=True)] + [jnp.sum(d * xv, axis=0, keepdims=True)
                                                           for xv in (x0, x1, x2, x3)]

    dxc_t, dcb, dw0, dw1, dw2, dw3 = rowwise(f_conv, [dxc, dxc2] + sv["xs"], outs=[(W, F32)],
                                             accs=[(1, W)] * 5, name=f"lru_conv_bwd_{tag}")
    g["conv_b"] = dcb[0]
    g["conv_w"] = jnp.concatenate([dw0, dw1, dw2, dw3], axis=0)
    ds_ = [_shift_up(dxc_t, LRU_CONV - 1 - tap) for tap in range(LRU_CONV)]
    taps = [w["conv_w"][tap][None] for tap in range(LRU_CONV)]
    dxa = rowwise(lambda a, b, c, d, w0, w1, w2, w3: ([a * w0 + b * w1 + c * w2 + d * w3], []), ds_, taps,
                  outs=[(W, F32)], name=f"lru_conv_dx_{tag}")[0]
    dz = jnp.concatenate([dxa, dya, _unheads(dq_hm), _unheads(dk_hm), _unheads(dv_hm), df], axis=1).astype(BF16)
    g["w_in"] = mm(sv["h"], dz, "tn", name=f"ev_dwin_{tag}")
    return dz, g


def odd_mixer_fwd(h, w, tag):
    S = h.shape[0]
    H, KVH = 8, 2
    z = mm(h, w["w_in"], name=f"od_in_{tag}")
    q, k, v, u = z[:, 0:512], z[:, 512:640], z[:, 640:768], z[:, 768:1280]
    tabs = _rope_tables(S)
    q_hm, k_hm, v_hm = _heads(q, H), _heads(k, KVH), _heads(v, KVH)
    qn = qk_prep_fwd(q_hm, w["qn"], tabs, f"swa_qprep_{tag}")
    kn = qk_prep_fwd(k_hm, w["kn"], tabs, f"swa_kprep_{tag}")
    sink = w["sinks"].reshape(H, 1, 1)
    o_hm, lse = attn_fwd(qn, kn, v_hm, sink, window=SWA_WINDOW, name=f"swa_attn_{tag}")
    lam, bexp = w["s5_lam"], w["s5_bexp"]
    bu = mm_blocks(u, bexp, name=f"s5_bu_{tag}")
    hs = scan_cplx(lam, bu, name=f"s5_scan_{tag}")
    yssm = mm_blocks(hs, w["s5_cexp"], name=f"s5_y_{tag}")
    zz = rowwise(lambda y_, u_, d_: ([_s5_out(y_, u_, d_)], []), [yssm, u], [w["s5_d"]], outs=[(512, F32)],
                 name=f"s5_gelu_{tag}")[0]
    gl = mm(zz, w["glu_w"], name=f"s5_glu_mm_{tag}")
    d_out = rowwise(lambda z_, g_, b_: ([_glu(z_, g_, b_)], []), [zz, gl], [w["glu_b"]], outs=[(512, F32)],
                    name=f"s5_glu_{tag}")[0]
    mo = jnp.concatenate([_unheads_t(o_hm), d_out], axis=1).astype(BF16)
    saved = dict(h=h, q_hm=q_hm, k_hm=k_hm, v_hm=v_hm, qn=qn, kn=kn, o_hm=o_hm, lse=lse, u=u, hs=hs, yssm=yssm,
                 zz=zz, gl=gl, mo=mo, tabs=tabs)
    return mo, saved


def odd_mixer_bwd(dy, sv, w, tag):
    H, KVH = 8, 2
    g = {}
    dmo = mm(dy, w["w_out"], "nt", name=f"od_dmo_{tag}")
    g["w_out"] = mm(sv["mo"], dy, "tn", name=f"od_dwout_{tag}")
    do, dd = dmo[:, :512], dmo[:, 512:]
    do_hm = _heads_t(do, H)
    delta = attn_delta(do_hm, sv["o_hm"], f"swa_delta_{tag}")
    dqn, dkn, dv_hm = attn_bwd(sv["qn"], sv["kn"], sv["v_hm"], sv["lse"], do_hm, delta, window=SWA_WINDOW,
                               name=f"swa_attn_bwd_{tag}")
    dq_hm, g["qn"] = qk_prep_bwd(sv["q_hm"], w["qn"], sv["tabs"], dqn, f"swa_qprep_bwd_{tag}")
    dk_hm, g["kn"] = qk_prep_bwd(sv["k_hm"], w["kn"], sv["tabs"], dkn, f"swa_kprep_bwd_{tag}")
    lse_t, delta_t = sv["lse"][:, 0, :].T, delta[:, 0, :].T
    g["sinks"] = rowwise(lambda l_, d_, s_: ([], [jnp.sum(-jnp.exp(s_ - l_) * d_, axis=0, keepdims=True)]),
                         [lse_t, delta_t], [w["sinks"].reshape(1, H)], accs=[(1, H)], name=f"swa_dsink_{tag}")[0][0]
    def f_glu(z_, g_, d_, b_):
        dz_, dg_, db_ = _vjp(_glu, (z_, g_, b_), d_)
        return [dz_, dg_], [db_]

    dzz1, dgl, dglb = rowwise(f_glu, [sv["zz"], sv["gl"], dd], [w["glu_b"]], outs=[(512, F32), (512, BF16)],
                              accs=[(1, 512)], name=f"s5_glu_bwd_{tag}")
    g["glu_b"] = dglb[0]
    g["glu_w"] = mm(sv["zz"], dgl, "tn", name=f"s5_glu_dw_{tag}")
    dzz2 = mm(dgl, w["glu_w"], "nt", name=f"s5_glu_dz_{tag}")

    def f_gelu(y_, u_, d1, d2, dpar):
        dy_, du_, dd_ = _vjp(_s5_out, (y_, u_, dpar), d1 + d2)
        return [dy_, du_], [dd_]

    dyssm, du1, dsd = rowwise(f_gelu, [sv["yssm"], sv["u"], dzz1, dzz2], [w["s5_d"]],
                              outs=[(512, F32), (512, F32)], accs=[(1, 512)], name=f"s5_gelu_bwd_{tag}")
    g["s5_d"] = dsd[0]
    dhs = mm_blocks(dyssm, w["s5_cexp"], "nt", name=f"s5_dh_{tag}")
    g["s5_cexp"] = _block_diag(mm_blocks_tn(sv["hs"], dyssm, S5_BLOCKS, name=f"s5_dc_{tag}"))
    gs = scan_cplx(w["s5_lam"], dhs, reverse=True, name=f"s5_scan_bwd_{tag}")
    g["s5_bexp"] = _block_diag(mm_blocks_tn(sv["u"], gs, S5_BLOCKS, name=f"s5_db_{tag}"))
    du2 = mm_blocks(gs, w["s5_bexp"], "nt", name=f"s5_du_{tag}")

    def f_dlam(g_, hp_):
        C = g_.shape[1]
        outs_r, outs_i = [], []
        for j in range(C // (2 * LANES)):
            gr, gi = g_[:, 2 * LANES * j:2 * LANES * j + LANES], g_[:, 2 * LANES * j + LANES:2 * LANES * (j + 1)]
            hr, hi = hp_[:, 2 * LANES * j:2 * LANES * j + LANES], hp_[:, 2 * LANES * j + LANES:2 * LANES * (j + 1)]
            outs_r.append(jnp.sum(gr * hr + gi * hi, axis=0, keepdims=True))
            outs_i.append(jnp.sum(gi * hr - gr * hi, axis=0, keepdims=True))
        return [], [jnp.concatenate([x for pair in zip(outs_r, outs_i) for x in pair], axis=1)]

    g["s5_lam"] = rowwise(f_dlam, [gs, _shift_down(sv["hs"])], accs=[(1, gs.shape[1])], name=f"s5_dlam_{tag}")[0]
    du = rowwise(lambda a_, b_: ([a_ + b_], []), [du1, du2], outs=[(512, F32)], name=f"s5_du_add_{tag}")[0]
    dz = jnp.concatenate([_unheads(dq_hm), _unheads(dk_hm), _unheads(dv_hm), du], axis=1).astype(BF16)
    g["w_in"] = mm(sv["h"], dz, "tn", name=f"od_dwin_{tag}")
    return dz, g


def _s5_cols(x_re, x_im):
    n = x_re.shape[0] // LANES
    return jnp.stack([x_re.reshape(n, LANES), x_im.reshape(n, LANES)], axis=1).reshape(1, 2 * n * LANES)


def _s5_uncols(x):
    n = x.shape[1] // (2 * LANES)
    y = x.reshape(n, 2, LANES)
    return y[:, 0].reshape(-1), y[:, 1].reshape(-1)


def _s5_gsel():
    return jnp.repeat(jnp.eye(S5_GROUPS, dtype=F32), S5_STATE, axis=0)


def s5_prep_fwd(lre, lim, ldt, bre, bim, cre, cim, tag):
    GP = S5_GROUPS * S5_STATE
    ins = [lre.reshape(GP, 1), lim.reshape(GP, 1), ldt.reshape(1, S5_GROUPS), _s5_gsel(),
           bre.reshape(GP, S5_GROUP), bim.reshape(GP, S5_GROUP)]
    lbr, lbi, bbr, bbi = whole(_s5_params, ins, [((GP, 1), F32)] * 2 + [((GP, S5_GROUP), F32)] * 2,
                               name=f"s5_params_{tag}")
    lam = _s5_cols(lbr[:, 0], lbi[:, 0])

    def expand_b(bb):
        return _block_diag(bb.reshape(S5_GROUPS, S5_STATE, S5_GROUP).transpose(0, 2, 1))

    n = GP // LANES
    bexp = jnp.stack([expand_b(bbr).reshape(-1, n, LANES), expand_b(bbi).reshape(-1, n, LANES)],
                     axis=2).reshape(-1, 2 * GP)
    c_r = _block_diag(cre.transpose(0, 2, 1))
    c_i = _block_diag(cim.transpose(0, 2, 1))
    cexp = jnp.stack([c_r.reshape(n, LANES, -1), -c_i.reshape(n, LANES, -1)], axis=1).reshape(2 * GP, -1)
    cb, sb = bexp.shape[0] // S5_BLOCKS, bexp.shape[1] // S5_BLOCKS
    bexp = jnp.stack([bexp[cb * j:cb * (j + 1), sb * j:sb * (j + 1)] for j in range(S5_BLOCKS)])
    cexp = jnp.stack([cexp[sb * j:sb * (j + 1), cb * j:cb * (j + 1)] for j in range(S5_BLOCKS)])
    return lam, bexp.astype(BF16), cexp.astype(BF16), ins


def s5_prep_bwd(ins, dlam, dbexp, dcexp, tag):
    GP = S5_GROUPS * S5_STATE
    n = GP // LANES
    dlr, dli = _s5_uncols(dlam)
    db = dbexp.reshape(-1, n, 2, LANES)

    def take_b(x):
        return _block_diag_take(x, S5_GROUPS).transpose(0, 2, 1).reshape(GP, S5_GROUP)

    dbbr, dbbi = take_b(db[:, :, 0].reshape(-1, GP)), take_b(db[:, :, 1].reshape(-1, GP))
    dc = dcexp.reshape(n, 2, LANES, -1)
    dcre = _block_diag_take(dc[:, 0].reshape(GP, -1), S5_GROUPS).transpose(0, 2, 1)
    dcim = -_block_diag_take(dc[:, 1].reshape(GP, -1), S5_GROUPS).transpose(0, 2, 1)

    def f(lre, lim, ldt, gsel, bre, bim, c1, c2, c3, c4):
        d = _vjp(lambda a, b, c, e, f_: _s5_params(a, b, c, gsel, e, f_), (lre, lim, ldt, bre, bim), (c1, c2, c3, c4))
        return d

    outs = [((GP, 1), F32)] * 2 + [((1, S5_GROUPS), F32)] + [((GP, S5_GROUP), F32)] * 2
    dlre, dlim, dldt, dbre, dbim = whole(f, ins + [dlr.reshape(GP, 1), dli.reshape(GP, 1), dbbr, dbbi], outs,
                                          name=f"s5_params_bwd_{tag}")
    shp = (S5_GROUPS, S5_STATE)
    return dict(lre=dlre.reshape(shp), lim=dlim.reshape(shp), ldt=dldt.reshape(S5_GROUPS),
                bre=dbre.reshape(S5_GROUPS, S5_STATE, S5_GROUP), bim=dbim.reshape(S5_GROUPS, S5_STATE, S5_GROUP),
                cre=dcre, cim=dcim)


def _place():
    return lax.axis_index("x"), lax.axis_index("y"), lax.axis_index("c")


def _other_chips(x, y):
    return [(1 - x, y), (x, 1 - y), (1 - x, 1 - y)]


def _half(ref, h):
    n = ref.shape[0] // 2
    return ref.at[pl.ds(h * n, n)]


def _hbm_specs(n):
    return [pl.BlockSpec(memory_space=pl.ANY)] * n


def gather_chips(ws):
    n = len(ws)

    def body(*refs):
        w_refs, out_refs, (send_sems, recv_sems) = refs[:n], refs[n:2 * n], refs[2 * n:]
        x, y, c = _place()
        me, sibling = (x, y, c), (x, y, 1 - c)
        chips = _other_chips(x, y)
        mine = 2 * x + y

        def copy(k, src, dst, to):
            return pltpu.make_async_remote_copy(src_ref=src, dst_ref=dst, send_sem=send_sems.at[k],
                                                recv_sem=recv_sems.at[k], device_id=to, device_id_type=MESH)

        first, passed = [], []
        for p in range(n):
            for j, chip in enumerate(chips):
                first.append(copy(6 * p + j, _half(w_refs[p], c), _half(out_refs[p].at[mine], c), (*chip, c)))
                first[-1].start()
        for p in range(n):
            for j, chip in enumerate(chips):
                block = out_refs[p].at[2 * chip[0] + chip[1]]
                copy(6 * p + j, _half(w_refs[p], c), _half(block, c), me).wait_recv()
                passed.append(copy(6 * p + 3 + j, _half(block, c), _half(block, c), sibling))
                passed[-1].start()
        for p in range(n):
            for j, chip in enumerate(chips):
                block = out_refs[p].at[2 * chip[0] + chip[1]]
                copy(6 * p + 3 + j, _half(w_refs[p], c), _half(block, 1 - c), me).wait_recv()
        for cp in first + passed:
            cp.wait_send()

    return pl.pallas_call(
        body, out_shape=[jax.ShapeDtypeStruct((N_CHIPS,) + w.shape, w.dtype) for w in ws],
        in_specs=_hbm_specs(n), out_specs=_hbm_specs(n),
        scratch_shapes=[pltpu.SemaphoreType.DMA((6 * n,)), pltpu.SemaphoreType.DMA((6 * n,))],
        name="gather_chips",
    )(*ws)


def sibling_halves(gs):
    n = len(gs)

    def body(*refs):
        g_refs, out_refs, (send_sems, recv_sems) = refs[:n], refs[n:2 * n], refs[2 * n:]
        x, y, c = _place()
        me, sibling = (x, y, c), (x, y, 1 - c)

        def copy(p, k, to):
            return pltpu.make_async_remote_copy(src_ref=_half(g_refs[p].at[k], 1 - c), dst_ref=out_refs[p].at[k],
                                                send_sem=send_sems.at[N_CHIPS * p + k],
                                                recv_sem=recv_sems.at[N_CHIPS * p + k],
                                                device_id=to, device_id_type=MESH)

        cps = [copy(p, k, sibling) for p in range(n) for k in range(N_CHIPS)]
        for cp in cps:
            cp.start()
        for p in range(n):
            for k in range(N_CHIPS):
                copy(p, k, me).wait_recv()
        for cp in cps:
            cp.wait_send()

    return pl.pallas_call(
        body, out_shape=[jax.ShapeDtypeStruct((N_CHIPS, g.shape[1] // 2) + g.shape[2:], g.dtype) for g in gs],
        in_specs=_hbm_specs(n), out_specs=_hbm_specs(n),
        scratch_shapes=[pltpu.SemaphoreType.DMA((N_CHIPS * n,)), pltpu.SemaphoreType.DMA((N_CHIPS * n,))],
        name="sibling_halves",
    )(*gs)


def exchange_chips(ps):
    n = len(ps)

    def body(*refs):
        p_refs, out_refs, (send_sems, recv_sems) = refs[:n], refs[n:2 * n], refs[2 * n:]
        x, y, c = _place()
        me = (x, y, c)
        chips = _other_chips(x, y)

        def copy(p, j, chip, to):
            return pltpu.make_async_remote_copy(src_ref=p_refs[p].at[2 * chip[0] + chip[1]], dst_ref=out_refs[p].at[j],
                                                send_sem=send_sems.at[3 * p + j], recv_sem=recv_sems.at[3 * p + j],
                                                device_id=to, device_id_type=MESH)

        cps = [copy(p, j, chip, (*chip, c)) for p in range(n) for j, chip in enumerate(chips)]
        for cp in cps:
            cp.start()
        for p in range(n):
            for j, chip in enumerate(chips):
                copy(p, j, chip, me).wait_recv()
        for cp in cps:
            cp.wait_send()

    return pl.pallas_call(
        body, out_shape=[jax.ShapeDtypeStruct((3,) + p_.shape[1:], p_.dtype) for p_ in ps],
        in_specs=_hbm_specs(n), out_specs=_hbm_specs(n),
        scratch_shapes=[pltpu.SemaphoreType.DMA((3 * n,)), pltpu.SemaphoreType.DMA((3 * n,))],
        name="exchange_chips",
    )(*ps)


def sibling_join(rs):
    n = len(rs)

    def body(*refs):
        r_refs, out_refs, (send_sems, recv_sems) = refs[:n], refs[n:2 * n], refs[2 * n:]
        x, y, c = _place()

        def copy(p, h, to):
            return pltpu.make_async_remote_copy(src_ref=r_refs[p], dst_ref=_half(out_refs[p], h),
                                                send_sem=send_sems.at[p], recv_sem=recv_sems.at[p],
                                                device_id=to, device_id_type=MESH)

        cps = [copy(p, c, (x, y, 1 - c)) for p in range(n)]
        for cp in cps:
            cp.start()
        for p in range(n):
            copy(p, 1 - c, (x, y, c)).wait_recv()
        for cp in cps:
            cp.wait_send()

    return pl.pallas_call(
        body, out_shape=[jax.ShapeDtypeStruct((2 * r.shape[0],) + r.shape[1:], r.dtype) for r in rs],
        in_specs=_hbm_specs(n), out_specs=_hbm_specs(n),
        scratch_shapes=[pltpu.SemaphoreType.DMA((n,)), pltpu.SemaphoreType.DMA((n,))],
        name="sibling_join",
    )(*rs)


def gather_devices(v, name):
    R = v.shape[0]

    def body(v_ref, out_ref, send_sems, recv_sems, local_sem):
        x, y, c = _place()
        me, sibling = (x, y, c), (x, y, 1 - c)
        chips = _other_chips(x, y)

        def rows(px, py, pc):
            return out_ref.at[pl.ds((4 * px + 2 * py + pc) * R, R), :]

        def copy(k, block, to, src=None):
            return pltpu.make_async_remote_copy(src_ref=rows(*block) if src is None else src, dst_ref=rows(*block),
                                                send_sem=send_sems.at[k], recv_sem=recv_sems.at[k],
                                                device_id=to, device_id_type=MESH)

        mine = pltpu.make_async_copy(v_ref, rows(*me), local_sem)
        mine.start()
        first = [copy(0, me, sibling, src=v_ref)]
        first += [copy(1 + j, me, (*chip, c), src=v_ref) for j, chip in enumerate(chips)]
        for cp in first:
            cp.start()
        passed = [copy(4 + j, (*chip, c), sibling) for j, chip in enumerate(chips)]
        for j, chip in enumerate(chips):
            copy(1 + j, (*chip, c), me).wait_recv()
            passed[j].start()
        copy(0, sibling, me).wait_recv()
        for j, chip in enumerate(chips):
            copy(4 + j, (*chip, 1 - c), me).wait_recv()
        for cp in first + passed:
            cp.wait_send()
        mine.wait()

    return pl.pallas_call(
        body, out_shape=jax.ShapeDtypeStruct((N_DEV * R, LANES), v.dtype),
        in_specs=[pl.BlockSpec(memory_space=pltpu.VMEM)], out_specs=pl.BlockSpec(memory_space=pltpu.VMEM),
        scratch_shapes=[pltpu.SemaphoreType.DMA((7,)), pltpu.SemaphoreType.DMA((7,)), pltpu.SemaphoreType.DMA],
        name=name, compiler_params=_params(),
    )(v)


def _flat_rows(n, mult):
    return -(-n // (LANES * mult)) * mult


def _adam(w, g, m, v):
    m = ADAM_B1 * m + (1.0 - ADAM_B1) * g
    v = ADAM_B2 * v + (1.0 - ADAM_B2) * (g * g)
    m_hat = m / (1.0 - ADAM_B1 ** ADAM_STEP)
    v_hat = v / (1.0 - ADAM_B2 ** ADAM_STEP)
    return -ADAM_LR * (m_hat / (jnp.sqrt(v_hat) + ADAM_EPS) + ADAM_WD * w), m, v


def adam_2d(w, g, m, v, name):
    shape = w.shape
    F = shape[-1]
    if w.ndim == 3 and shape[1] % (2 * SUBLANES) == 0:
        L, R, _ = shape
        tr = R
        for t in (512, 256, 128, 64, 32, 16):
            if R % t == 0 and 7 * t * max(F, LANES) * 4 <= ROW_TILE_BYTES:
                tr = t
                break

        def body(w_ref, g_ref, m_ref, v_ref, d_ref, m2_ref, v2_ref):
            d_ref[...], m2_ref[...], v2_ref[...] = _adam(w_ref[...], g_ref[...], m_ref[...], v_ref[...])

        spec = pl.BlockSpec((None, tr, F), lambda l, i: (l, i, 0))
        return pl.pallas_call(
            body, grid=(L, R // tr), in_specs=[spec] * 4, out_specs=[spec] * 3,
            out_shape=[jax.ShapeDtypeStruct(shape, F32)] * 3, name=name, compiler_params=_params(("parallel", "parallel")),
        )(w, g, m, v)
    a = [t.reshape(-1, F) for t in (w, g, m, v)]
    d, m2, v2 = rowwise(lambda w_, g_, m_, v_: (list(_adam(w_, g_, m_, v_)), []), a, outs=[(F, F32)] * 3, name=name)
    return d.reshape(shape), m2.reshape(shape), v2.reshape(shape)


WEIGHTS = ['ffn1_norm', 'ffn1_wg', 'ffn1_wu', 'ffn1_wd', 'mix_norm', 'ffn2_norm', 'ffn2_wg', 'ffn2_wu', 'ffn2_wd',
           'ple_w', 'ple_norm', 'ple_gate_norm', 'ple_gate_w', 'ev_w_in', 'lru_conv_w', 'lru_conv_b', 'lru_wa',
           'lru_ba', 'lru_wx', 'lru_bx', 'lru_lambda', 'fox_bf', 'fox_q_norm', 'fox_k_norm', 'ev_w_out', 'od_w_in',
           'swa_q_norm', 'swa_k_norm', 'swa_sinks', 's5_lambda_re', 's5_lambda_im', 's5_log_dt', 's5_b_re',
           's5_b_im', 's5_c_re', 's5_c_im', 's5_d', 's5_glu_w', 's5_glu_b', 'od_w_out']
SHARD_AXIS = {'ffn1_wg': 2, 'ffn1_wu': 2, 'ffn1_wd': 1, 'ffn2_wg': 2, 'ffn2_wu': 2, 'ffn2_wd': 1, 'ple_w': 2,
              'ple_gate_w': 1, 'ev_w_in': 2, 'lru_conv_w': 2, 'ev_w_out': 1, 'od_w_in': 2, 's5_d': 1,
              's5_glu_w': 1, 's5_glu_b': 1, 'od_w_out': 1}
EXACT_SHARDED = ('lru_conv_w', 's5_d', 's5_glu_b')
ADAM_TRANSPOSED = ('ffn1_wg', 'ffn1_wu', 'ffn2_wg', 'ffn2_wu', 'od_w_in')
SHARDED = [n for n in WEIGHTS if n in SHARD_AXIS]
REPLICATED = [n for n in WEIGHTS if n not in SHARD_AXIS]


GROUPS = {
    'wgu': ['ffn1_wg', 'ffn1_wu', 'ffn2_wg', 'ffn2_wu'],
    'wd': ['ffn1_wd', 'ffn2_wd'],
    'w_rows': ['ple_gate_w', 'ev_w_out', 'od_w_out'],
    'ple_w': ['ple_w'], 'ev_w_in': ['ev_w_in'], 'od_w_in': ['od_w_in'], 's5_glu_w': ['s5_glu_w'],
}
REDUCED_GROUPS = list(GROUPS)


def _chip():
    return 2 * lax.axis_index("x") + lax.axis_index("y")


def gather_weights(shards):
    own = {k: jnp.concatenate([shards[n] for n in names], axis=0).astype(BF16) for k, names in GROUPS.items()}
    own['exact'] = jnp.concatenate([shards['lru_conv_w'], shards['s5_d'][:, None], shards['s5_glu_b'][:, None]], axis=1)
    keys = list(own)
    got = gather_chips([own[k] for k in keys])
    return {k: lax.dynamic_update_index_in_dim(g, own[k], _chip(), 0) for k, g in zip(keys, got)}


def _rows_by_chip(w):
    return w.reshape(w.shape[0] * w.shape[1], w.shape[2])


def _cols_by_chip(w):
    return w.transpose(1, 0, 2).reshape(w.shape[1], w.shape[0] * w.shape[2])


def _chip_rows(g):
    return g.reshape(N_CHIPS, g.shape[0] // N_CHIPS, g.shape[1])


def _chip_cols(g):
    return g.reshape(g.shape[0], N_CHIPS, g.shape[1] // N_CHIPS).transpose(1, 0, 2)


def full_weights(gw, depth):
    n_ev = (depth + 1) // 2
    ex = gw['exact']
    return dict(
        ple_gate_w=[_rows_by_chip(gw['w_rows'][:, l]) for l in range(depth)],
        ev_w_out=[_rows_by_chip(gw['w_rows'][:, depth + j]) for j in range(n_ev)],
        od_w_out=[_rows_by_chip(gw['w_rows'][:, depth + n_ev + j]) for j in range(depth // 2)],
        ple_w=[_cols_by_chip(gw['ple_w'][:, l]) for l in range(depth)],
        ev_w_in=[_cols_by_chip(gw['ev_w_in'][:, j]) for j in range(n_ev)],
        od_w_in=[_cols_by_chip(gw['od_w_in'][:, j]) for j in range(depth // 2)],
        s5_glu_w=[_rows_by_chip(gw['s5_glu_w'][:, j]) for j in range(depth // 2)],
        lru_conv_w=[_cols_by_chip(ex[:, j, 0:LRU_CONV]) for j in range(n_ev)],
        s5_d=[ex[:, j, LRU_CONV].reshape(-1) for j in range(depth // 2)],
        s5_glu_b=[ex[:, j, LRU_CONV + 1].reshape(-1) for j in range(depth // 2)],
    )


def _add_tile(rows, width):
    for t in (1024, 512, 256, 128, 64, 32, 16):
        if rows % t == 0 and 3 * t * width * 4 <= ROW_TILE_BYTES:
            return t
    return rows


def pair_add(g, t, c, name):
    C, F = g.shape[0], g.shape[-1]
    rows = math.prod(t.shape[1:-1])
    tr = _add_tile(rows, F)
    nb = rows // tr

    def body(c_ref, g_ref, t_ref, o_ref):
        o_ref[...] = (g_ref[...].astype(F32) + t_ref[...].astype(F32)).astype(o_ref.dtype)

    spec = pl.BlockSpec((None, tr, F), lambda k, i, c_ref: (k, i, 0))
    out = pl.pallas_call(
        body, out_shape=jax.ShapeDtypeStruct((C, rows, F), BF16),
        grid_spec=pltpu.PrefetchScalarGridSpec(
            num_scalar_prefetch=1, grid=(C, nb),
            in_specs=[pl.BlockSpec((None, tr, F), lambda k, i, c_ref: (k, c_ref[0] * nb + i, 0)), spec],
            out_specs=spec),
        name=name, compiler_params=_params(("parallel", "parallel")),
    )(c.reshape(1).astype(jnp.int32), g.reshape(C, 2 * rows, F), t.reshape(C, rows, F))
    return out.reshape(t.shape)


def chips_add(p, xs, chip, name):
    F = p.shape[-1]
    rows = math.prod(p.shape[1:-1])
    tr = _add_tile(rows, F)

    def body(m_ref, p_ref, a_ref, b_ref, d_ref, o_ref):
        o_ref[...] = ((p_ref[...].astype(F32) + a_ref[...].astype(F32))
                      + (b_ref[...].astype(F32) + d_ref[...].astype(F32)))

    def other(j):
        return pl.BlockSpec((None, tr, F), lambda i, m_ref: (j, i, 0))

    x3 = xs.reshape(3, rows, F)
    out = pl.pallas_call(
        body, out_shape=jax.ShapeDtypeStruct((rows, F), F32),
        grid_spec=pltpu.PrefetchScalarGridSpec(
            num_scalar_prefetch=1, grid=(rows // tr,),
            in_specs=[pl.BlockSpec((None, tr, F), lambda i, m_ref: (m_ref[0], i, 0)), other(0), other(1), other(2)],
            out_specs=pl.BlockSpec((tr, F), lambda i, m_ref: (i, 0))),
        name=name, compiler_params=_params(("parallel",)),
    )(chip.reshape(1).astype(jnp.int32), p.reshape(N_CHIPS, rows, F), x3, x3, x3)
    return out.reshape(p.shape[1:])


def reduce_sharded(groups):
    keys = list(groups)
    c = lax.axis_index("c")
    gs = [groups[k] for k in keys]
    theirs = sibling_halves(gs)
    pairs = [pair_add(g, t, c, f"pair_add_{k}") for k, g, t in zip(keys, gs, theirs)]
    got = exchange_chips(pairs)
    halves = [chips_add(p_, x_, _chip(), f"chips_add_{k}") for k, p_, x_ in zip(keys, pairs, got)]
    joined = sibling_join(halves)
    out = {}
    for k, h, j in zip(keys, halves, joined):
        out[k] = lax.dynamic_update_slice_in_dim(j, h, c * h.shape[0], axis=0)
    return out


SMALL_GRADS = REPLICATED + list(EXACT_SHARDED)


def _flatten_small(tensors, shapes):
    parts = [tensors[n].astype(F32).reshape(-1) if n in tensors else jnp.zeros((math.prod(shapes[n]),), F32)
             for n in SMALL_GRADS]
    flat = jnp.concatenate(parts)
    rows = _flat_rows(flat.shape[0], SUBLANES)
    return jnp.pad(flat, (0, rows * LANES - flat.shape[0])).reshape(rows, LANES)


def _unflatten_small(flat, shapes):
    flat = flat.reshape(-1)
    out, off = {}, 0
    for n in SMALL_GRADS:
        size = math.prod(shapes[n])
        out[n] = flat[off:off + size].reshape(shapes[n])
        off += size
    return out


def grad_groups(gwgu, gwd, G):
    def st(xs):
        return jnp.stack(xs, axis=1).astype(BF16)

    return {
        'wgu': gwgu, 'wd': gwd,
        'w_rows': st([_chip_rows(g) for n in GROUPS['w_rows'] for g in G[n]]),
        'ple_w': st([_chip_cols(g) for g in G['ple_w']]),
        'ev_w_in': st([_chip_cols(g) for g in G['ev_w_in']]),
        'od_w_in': st([_chip_cols(g) for g in G['od_w_in']]),
        's5_glu_w': st([_chip_rows(g) for g in G['s5_glu_w']]),
    }


def ungroup(red, shapes):
    out = {}
    for k, names in GROUPS.items():
        off = 0
        for n in names:
            out[n] = red[k][off:off + shapes[n][0]]
            off += shapes[n][0]
    return out


def _layer_weights(full, small, i, depth):
    j = i // 2
    w = dict(
        g1=small['ffn1_norm'][i][None], gm=small['mix_norm'][i][None], g2=small['ffn2_norm'][i][None],
        gp=small['ple_norm'][i][None], gg=small['ple_gate_norm'][i][None],
        ffn1=(i, depth + i, i), ffn2=(2 * depth + i, 3 * depth + i, depth + i),
        ple_w=full['ple_w'][i], ple_gate_w=full['ple_gate_w'][i],
    )
    if i % 2 == 0:
        w_in = full['ev_w_in'][j]
        w['mix'] = dict(
            w_in=jnp.pad(w_in, ((0, 0), (0, 2688 - w_in.shape[1]))), w_out=full['ev_w_out'][j],
            conv_w=full['lru_conv_w'][j].astype(F32), conv_b=small['lru_conv_b'][j][None],
            w_ax=jnp.concatenate([_block_diag(small['lru_wa'][j]), _block_diag(small['lru_wx'][j])],
                                 axis=1).astype(BF16),
            ba=small['lru_ba'][j][None], bx=small['lru_bx'][j][None], lam=small['lru_lambda'][j][None],
            bf=jnp.pad(small['fox_bf'][j], (0, LANES - 8))[None], qn=small['fox_q_norm'][j],
            kn=small['fox_k_norm'][j])
    else:
        lam, bexp, cexp, ins = s5_prep_fwd(small['s5_lambda_re'][j], small['s5_lambda_im'][j], small['s5_log_dt'][j],
                                           small['s5_b_re'][j], small['s5_b_im'][j], small['s5_c_re'][j],
                                           small['s5_c_im'][j], f"L{i}")
        w['mix'] = dict(
            w_in=full['od_w_in'][j], w_out=full['od_w_out'][j], qn=small['swa_q_norm'][j], kn=small['swa_k_norm'][j],
            sinks=small['swa_sinks'][j], s5_lam=lam, s5_bexp=bexp, s5_cexp=cexp, s5_ins=ins,
            s5_d=full['s5_d'][j].astype(F32)[None], glu_w=full['s5_glu_w'][j], glu_b=full['s5_glu_b'][j].astype(F32)[None])
    return w


def layer_fwd(x0, n1, p_i, w, ffnw, next_g1, i):
    tag = f"L{i}"
    sv = {}
    wgu, wd = ffnw
    x1, hm, sv['ffn1'] = ffn_fwd(n1, x0, w['gm'], wgu, wd, *w['ffn1'], f"1_{tag}")
    if i % 2 == 0:
        mo, sv['mix'] = even_mixer_fwd(hm, w['mix'], tag)
    else:
        mo, sv['mix'] = odd_mixer_fwd(hm, w['mix'], tag)
    x2, n2 = mm_add_norm(mo, w['mix']['w_out'], x1, w['g2'], f"mix_out_{tag}")
    x3, ng, sv['ffn2'] = ffn_fwd(n2, x2, w['gg'], wgu, wd, *w['ffn2'], f"2_{tag}")
    gpre = mm(ng, w['ple_gate_w'], name=f"ple_gate_{tag}")
    epre = mm(p_i, w['ple_w'], name=f"ple_emb_{tag}")
    D = x0.shape[1]
    if next_g1 is None:
        x4 = rowwise(lambda a, b, c, pn: ([_ple_out(a, b, c, pn)], []), [x3, gpre, epre], [w['gp']],
                     outs=[(D, F32)], name=f"ple_out_{tag}")[0]
        n_next = None
    else:
        def f(a, b, c, pn, gn):
            y = _ple_out(a, b, c, pn)
            return [y, _rms(y, gn)], []

        x4, n_next = rowwise(f, [x3, gpre, epre], [w['gp'], next_g1], outs=[(D, F32), (D, BF16)],
                             name=f"ple_out_{tag}")
    sv.update(x0=x0, x1=x1, x2=x2, x3=x3, ng=ng, gpre=gpre, epre=epre, p=p_i)
    return x4, n_next, sv


def layer_bwd(dx4, sv, w, ffnw, gbuf, i):
    tag = f"L{i}"
    D = dx4.shape[1]
    g = {}
    wgu, wd = ffnw

    def f_ple(a, b, c, d, pn):
        da, db, dc, dpn = _vjp(_ple_out, (a, b, c, pn), d)
        return [db, dc], [dpn]

    dgpre, depre, dgp = rowwise(f_ple, [sv['x3'], sv['gpre'], sv['epre'], dx4], [w['gp']],
                                outs=[(D, BF16), (D, BF16)], accs=[(1, D)], name=f"ple_out_bwd_{tag}")
    g['gp'] = dgp[0]
    g['ple_w'] = mm(sv['p'], depre, "tn", name=f"ple_emb_dw_{tag}")
    g['ple_gate_w'] = mm(sv['ng'], dgpre, "tn", name=f"ple_gate_dw_{tag}")
    dx3, dgg = mm_norm_bwd(dgpre, w['ple_gate_w'], sv['x3'], w['gg'], dx4, f"ple_gate_dx_{tag}")
    g['gg'] = dgg[0]
    dx2, dg2, gbuf = ffn_bwd(dx3, sv['ffn2'], sv['x2'], w['g2'], wgu, wd, *w['ffn2'], gbuf, f"2_{tag}")
    g['g2'] = dg2[0]
    if i % 2 == 0:
        dz, g['mix'] = even_mixer_bwd(dx2, sv['mix'], w['mix'], tag)
    else:
        dz, g['mix'] = odd_mixer_bwd(dx2, sv['mix'], w['mix'], tag)
    dx1, dgm = mm_norm_bwd(dz, w['mix']['w_in'], sv['x1'], w['gm'], dx2, f"mix_dh_{tag}")
    g['gm'] = dgm[0]
    dx0, dg1, gbuf = ffn_bwd(dx1, sv['ffn1'], sv['x0'], w['g1'], wgu, wd, *w['ffn1'], gbuf, f"1_{tag}")
    g['g1'] = dg1[0]
    return dx0, g, gbuf


def _collect_grads(layer_grads, depth):
    st = lambda xs: jnp.stack(xs)
    G = {}
    L = layer_grads
    G['ffn1_norm'] = st([g['g1'] for g in L])
    G['mix_norm'] = st([g['gm'] for g in L])
    G['ffn2_norm'] = st([g['g2'] for g in L])
    G['ple_norm'] = st([g['gp'] for g in L])
    G['ple_gate_norm'] = st([g['gg'] for g in L])
    G['ple_w'] = st([g['ple_w'] for g in L])
    G['ple_gate_w'] = st([g['ple_gate_w'] for g in L])
    ev = [L[i]['mix'] for i in range(0, depth, 2)]
    od = [L[i]['mix'] for i in range(1, depth, 2)]
    G['ev_w_in'] = st([m['w_in'][:, :2568] for m in ev])
    G['ev_w_out'] = st([m['w_out'] for m in ev])
    G['lru_conv_w'] = st([m['conv_w'] for m in ev])
    G['lru_conv_b'] = st([m['conv_b'] for m in ev])
    G['lru_wa'] = st([_block_diag_take(m['w_ax'][:, :512], LRU_BLOCKS) for m in ev])
    G['lru_wx'] = st([_block_diag_take(m['w_ax'][:, 512:], LRU_BLOCKS) for m in ev])
    G['lru_ba'] = st([m['ba'] for m in ev])
    G['lru_bx'] = st([m['bx'] for m in ev])
    G['lru_lambda'] = st([m['lam'] for m in ev])
    G['fox_bf'] = st([m['bf'] for m in ev])
    G['fox_q_norm'] = st([m['qn'] for m in ev])
    G['fox_k_norm'] = st([m['kn'] for m in ev])
    G['od_w_in'] = st([m['w_in'] for m in od])
    G['od_w_out'] = st([m['w_out'] for m in od])
    G['swa_q_norm'] = st([m['qn'] for m in od])
    G['swa_k_norm'] = st([m['kn'] for m in od])
    G['swa_sinks'] = st([m['sinks'] for m in od])
    G['s5_lambda_re'] = st([m['s5']['lre'] for m in od])
    G['s5_lambda_im'] = st([m['s5']['lim'] for m in od])
    G['s5_log_dt'] = st([m['s5']['ldt'] for m in od])
    G['s5_b_re'] = st([m['s5']['bre'] for m in od])
    G['s5_b_im'] = st([m['s5']['bim'] for m in od])
    G['s5_c_re'] = st([m['s5']['cre'] for m in od])
    G['s5_c_im'] = st([m['s5']['cim'] for m in od])
    G['s5_d'] = st([m['s5_d'] for m in od])
    G['s5_glu_w'] = st([m['glu_w'] for m in od])
    G['s5_glu_b'] = st([m['glu_b'] for m in od])
    return G


def local_step(x, p, target, ffnw, full, small):
    depth = p.shape[0]
    S, D = x.shape
    ws = [_layer_weights(full, small, i, depth) for i in range(depth)]
    saved = []
    xi, ni = add_norm(x, None, ws[0]['g1'], "norm1_L0")
    for i in range(depth):
        xi, ni, sv = layer_fwd(xi, ni, p[i], ws[i], ffnw, ws[i + 1]['g1'] if i + 1 < depth else None, i)
        saved.append(sv)

    def f_loss(y, t):
        e = y - t
        return [e * (1.0 / D)], [0.5 * jnp.sum(jnp.mean(e * e, axis=-1, keepdims=True), axis=0, keepdims=True)]

    dx, loss = rowwise(f_loss, [xi, target], outs=[(D, F32)], accs=[(1, 1)], name="loss")
    grads = [None] * depth
    gbuf = (None, None)
    for i in reversed(range(depth)):
        dx, grads[i], gbuf = layer_bwd(dx, saved[i], ws[i], ffnw, gbuf, i)
        if i % 2 == 1:
            m = grads[i]['mix']
            m['s5'] = s5_prep_bwd(ws[i]['mix']['s5_ins'], m['s5_lam'], m['s5_bexp'], m['s5_cexp'], f"L{i}")
    return loss[0, 0], dx, gbuf, _collect_grads(grads, depth)


def kernel(x, p, ffn1_norm, ffn1_wg, ffn1_wu, ffn1_wd, mix_norm, ffn2_norm, ffn2_wg, ffn2_wu, ffn2_wd, ple_w, ple_norm, ple_gate_norm, ple_gate_w, ev_w_in, lru_conv_w, lru_conv_b, lru_wa, lru_ba, lru_wx, lru_bx, lru_lambda, fox_bf, fox_q_norm, fox_k_norm, ev_w_out, od_w_in, swa_q_norm, swa_k_norm, swa_sinks, s5_lambda_re, s5_lambda_im, s5_log_dt, s5_b_re, s5_b_im, s5_c_re, s5_c_im, s5_d, s5_glu_w, s5_glu_b, od_w_out, loss_target, m_ffn1_norm, m_ffn1_wg, m_ffn1_wu, m_ffn1_wd, m_mix_norm, m_ffn2_norm, m_ffn2_wg, m_ffn2_wu, m_ffn2_wd, m_ple_w, m_ple_norm, m_ple_gate_norm, m_ple_gate_w, m_ev_w_in, m_lru_conv_w, m_lru_conv_b, m_lru_wa, m_lru_ba, m_lru_wx, m_lru_bx, m_lru_lambda, m_fox_bf, m_fox_q_norm, m_fox_k_norm, m_ev_w_out, m_od_w_in, m_swa_q_norm, m_swa_k_norm, m_swa_sinks, m_s5_lambda_re, m_s5_lambda_im, m_s5_log_dt, m_s5_b_re, m_s5_b_im, m_s5_c_re, m_s5_c_im, m_s5_d, m_s5_glu_w, m_s5_glu_b, m_od_w_out, v_ffn1_norm, v_ffn1_wg, v_ffn1_wu, v_ffn1_wd, v_mix_norm, v_ffn2_norm, v_ffn2_wg, v_ffn2_wu, v_ffn2_wd, v_ple_w, v_ple_norm, v_ple_gate_norm, v_ple_gate_w, v_ev_w_in, v_lru_conv_w, v_lru_conv_b, v_lru_wa, v_lru_ba, v_lru_wx, v_lru_bx, v_lru_lambda, v_fox_bf, v_fox_q_norm, v_fox_k_norm, v_ev_w_out, v_od_w_in, v_swa_q_norm, v_swa_k_norm, v_swa_sinks, v_s5_lambda_re, v_s5_lambda_im, v_s5_log_dt, v_s5_b_re, v_s5_b_im, v_s5_c_re, v_s5_c_im, v_s5_d, v_s5_glu_w, v_s5_glu_b, v_od_w_out):
    args = locals()
    wts = {n: args[n] for n in WEIGHTS}
    ms = {n: args["m_" + n] for n in WEIGHTS}
    vs = {n: args["v_" + n] for n in WEIGHTS}
    shapes = {n: wts[n].shape for n in WEIGHTS}

    depth = p.shape[0]
    gw = gather_weights({n: wts[n] for n in SHARDED})
    small = {n: wts[n] for n in REPLICATED}
    loss, dx, (gwgu, gwd), G = local_step(x[0], p[:, 0], loss_target[0], (gw['wgu'], gw['wd']),
                                          full_weights(gw, depth), small)
    loss = lax.psum(loss, ("x", "y", "c"))

    gsh = ungroup(reduce_sharded(grad_groups(gwgu, gwd, G)), shapes)
    full_shapes = {n: (G[n].shape if n in EXACT_SHARDED else shapes[n]) for n in SMALL_GRADS}
    flat_g = _flatten_small(G, full_shapes)
    g8 = gather_devices(flat_g, "gather_small_grads").reshape((N_DEV,) + flat_g.shape)
    wf, mf, vf = (_flatten_small({n: t[n] for n in REPLICATED}, full_shapes) for t in (wts, ms, vs))

    def f_small(g0, g1, g2, g3, g4, g5, g6, g7, w_, m_, v_):
        gsum = ((g0 + g1) + (g2 + g3)) + ((g4 + g5) + (g6 + g7))
        return [gsum] + list(_adam(w_, gsum, m_, v_)), []

    gs_f, ds_f, ms_f, vs_f = rowwise(f_small, [g8[d] for d in range(N_DEV)] + [wf, mf, vf],
                                     outs=[(LANES, F32)] * 4, name="adam_small")
    out_g, out_d, out_m, out_v = {}, {}, {}, {}
    for dst, flat in ((out_g, gs_f), (out_d, ds_f), (out_m, ms_f), (out_v, vs_f)):
        dst.update(_unflatten_small(flat, full_shapes))
    for n in EXACT_SHARDED:
        width = shapes[n][SHARD_AXIS[n]]
        gsh[n] = lax.dynamic_slice_in_dim(out_g[n], _chip() * width, width, axis=SHARD_AXIS[n])
    for n in SHARDED:
        out_g[n] = gsh[n]
        if n in ADAM_TRANSPOSED:
            def t(a):
                return a.transpose(0, 2, 1)
            d_, m_, v_ = adam_2d(t(wts[n]), t(gsh[n]), t(ms[n]), t(vs[n]), f"adam_{n}")
            out_d[n], out_m[n], out_v[n] = t(d_), t(m_), t(v_)
        else:
            out_d[n], out_m[n], out_v[n] = adam_2d(wts[n], gsh[n], ms[n], vs[n], f"adam_{n}")
    return (loss, dx[None], *[out_g[n] for n in WEIGHTS], *[out_d[n] for n in WEIGHTS],
            *[out_m[n] for n in WEIGHTS], *[out_v[n] for n in WEIGHTS])
```

```python
import functools
import math

import jax
import jax.numpy as jnp
from jax import lax
from jax.experimental import pallas as pl
from jax.experimental.pallas import tpu as pltpu

F32 = jnp.float32
BF16 = jnp.bfloat16
MXU_DTYPE = BF16
HI = lax.Precision.HIGHEST
MESH = pl.DeviceIdType.MESH

VMEM_LIMIT_BYTES = 56 * 1024 * 1024
ROW_TILE_BYTES = 5 * 1024 * 1024
MM_VMEM_BYTES = 40 * 1024 * 1024
MM_TILE_M = 1024
MM_TILE_N = 1408
FLAT_W = 2048
LANES = 128
SUBLANES = 8

HEAD_DIM = 64
LRU_BLOCKS = 8
LRU_CONV = 4
LRU_C = 8.0
SWA_WINDOW = 128
SWA_GROUP = 4
S5_GROUP = 16
S5_GROUPS = 32
S5_STATE = 64
S5_BLOCKS = 4
ROPE_THETA = 10000.0
EPS = 1e-6
MACARON = 0.5
NEG = -1e30

ADAM_LR = 0.001
ADAM_B1 = 0.9
ADAM_B2 = 0.999
ADAM_EPS = 1e-08
ADAM_WD = 0.01
ADAM_STEP = 10

N_CHIPS = 4
N_DEV = 8


def _pick(n, cands):
    for c in cands:
        if n % c == 0:
            return c
    return n


def _tile(n, cap, unit):
    best = None
    for t in range(unit, min(n, cap) + 1, unit):
        if n % t == 0:
            best = t
    return n if best is None else best


def _params(sem=None):
    return pltpu.CompilerParams(dimension_semantics=sem, vmem_limit_bytes=VMEM_LIMIT_BYTES)


def rowwise(fn, rows, consts=(), outs=(), accs=(), name="rowwise", periods=None):
    rows, consts = list(rows), list(consts)
    n_r, n_c, n_o, n_a = len(rows), len(consts), len(outs), len(accs)
    R = rows[0].shape[0]
    periods = list(periods) if periods is not None else [None] * n_r
    per_row = sum(max(r.shape[1], LANES) * 4 for r in rows) + sum(max(f, LANES) * 4 for f, _ in outs)
    limit = min([R] + [p for p in periods if p is not None])
    tr = limit
    for c in (1024, 512, 256, 128, 64, 32, 16):
        if c <= limit and limit % c == 0 and R % c == 0 and c * per_row <= ROW_TILE_BYTES:
            tr = c
            break

    def row_map(period):
        if period is None:
            return lambda i: (i, 0)
        nb = period // tr
        return lambda i: (i % nb, 0)

    in_specs = [pl.BlockSpec((tr, r.shape[1]), row_map(p)) for r, p in zip(rows, periods)]
    in_specs += [pl.BlockSpec(c.shape, lambda i: (0, 0)) for c in consts]
    out_shape = [jax.ShapeDtypeStruct((R, f), dt) for f, dt in outs]
    out_shape += [jax.ShapeDtypeStruct(tuple(s), F32) for s in accs]
    out_specs = [pl.BlockSpec((tr, f), lambda i: (i, 0)) for f, _ in outs]
    out_specs += [pl.BlockSpec(tuple(s), lambda i: (0, 0)) for s in accs]

    def body(*refs):
        ins = [r[...] for r in refs[:n_r + n_c]]
        o_refs = refs[n_r + n_c:n_r + n_c + n_o]
        a_refs = refs[n_r + n_c + n_o:]
        ro, ra = fn(*ins)
        for ref, val in zip(o_refs, ro):
            ref[...] = val.astype(ref.dtype)
        if n_a:
            @pl.when(pl.program_id(0) == 0)
            def _():
                for ref in a_refs:
                    ref[...] = jnp.zeros(ref.shape, ref.dtype)
            for ref, val in zip(a_refs, ra):
                ref[...] += val.astype(F32)

    res = pl.pallas_call(
        body, grid=(R // tr,), in_specs=in_specs, out_specs=out_specs, out_shape=out_shape,
        name=name, compiler_params=_params(("arbitrary",)),
    )(*rows, *consts)
    return list(res)


def whole(fn, ins, outs, name="whole"):
    n_i = len(ins)

    def body(*refs):
        vals = fn(*[r[...] for r in refs[:n_i]])
        for ref, val in zip(refs[n_i:], vals):
            ref[...] = val.astype(ref.dtype)

    res = pl.pallas_call(
        body, out_shape=[jax.ShapeDtypeStruct(tuple(s), dt) for s, dt in outs],
        in_specs=[pl.BlockSpec(memory_space=pltpu.VMEM)] * n_i,
        out_specs=[pl.BlockSpec(memory_space=pltpu.VMEM)] * len(outs),
        name=name, compiler_params=_params(),
    )(*ins)
    return list(res)


_DOT_DIMS = {
    "nn": (((1,), (0,)), ((), ())),
    "nt": (((1,), (1,)), ((), ())),
    "tn": (((0,), (0,)), ((), ())),
}


def mm(a, b, mode="nn", out_dtype=F32, name="mm"):
    if mode == "nn":
        (M, K), (K2, N) = a.shape, b.shape
    elif mode == "nt":
        (M, K), (N, K2) = a.shape, b.shape
    else:
        (K, M), (K2, N) = a.shape, b.shape
    assert K == K2, (mode, a.shape, b.shape)
    tn = _tile(N, MM_TILE_N, LANES)
    if mode == "tn":
        tm, tk = _tile(M, MM_TILE_M, LANES), _tile(K, MM_TILE_M, 2 * SUBLANES)
    else:
        tm, tk = _tile(M, MM_TILE_M, 2 * SUBLANES), _tile(K, MM_TILE_N, LANES)

    def vmem_bytes(tm_, tk_):
        return (2 * (tm_ * tk_ * a.dtype.itemsize + tk_ * tn * b.dtype.itemsize
                     + tm_ * tn * jnp.dtype(out_dtype).itemsize) + tm_ * tn * 4)

    while vmem_bytes(tm, tk) > MM_VMEM_BYTES and tk % (2 * LANES) == 0 and K % (tk // 2) == 0:
        tk //= 2
    while vmem_bytes(tm, tk) > MM_VMEM_BYTES and tm % (2 * LANES) == 0 and M % (tm // 2) == 0:
        tm //= 2
    if mode == "tn":
        a_spec = pl.BlockSpec((tk, tm), lambda i, j, k: (k, i))
    else:
        a_spec = pl.BlockSpec((tm, tk), lambda i, j, k: (i, k))
    if mode == "nt":
        b_spec = pl.BlockSpec((tn, tk), lambda i, j, k: (j, k))
    else:
        b_spec = pl.BlockSpec((tk, tn), lambda i, j, k: (k, j))
    nk = K // tk
    dims = _DOT_DIMS[mode]

    def dot(a_ref, b_ref):
        return lax.dot_general(a_ref[...].astype(MXU_DTYPE), b_ref[...].astype(MXU_DTYPE), dims,
                               preferred_element_type=F32)

    def body_one(a_ref, b_ref, o_ref):
        o_ref[...] = dot(a_ref, b_ref).astype(o_ref.dtype)

    def body_acc(a_ref, b_ref, o_ref, acc_ref):
        k = pl.program_id(2)

        @pl.when(k == 0)
        def _():
            acc_ref[...] = dot(a_ref, b_ref)

        @pl.when(k > 0)
        def _():
            acc_ref[...] += dot(a_ref, b_ref)

        @pl.when(k == nk - 1)
        def _():
            o_ref[...] = acc_ref[...].astype(o_ref.dtype)

    return pl.pallas_call(
        body_one if nk == 1 else body_acc, grid=(M // tm, N // tn, nk), in_specs=[a_spec, b_spec],
        out_specs=pl.BlockSpec((tm, tn), lambda i, j, k: (i, j)),
        out_shape=jax.ShapeDtypeStruct((M, N), out_dtype),
        scratch_shapes=[] if nk == 1 else [pltpu.VMEM((tm, tn), F32)],
        name=name, compiler_params=_params(("parallel", "parallel", "arbitrary")),
    )(a, b)


def mm_blocks(a, b, mode="nn", name="mm_blocks"):
    M = a.shape[0]
    nb = b.shape[0]
    Ka, Nb = (b.shape[1], b.shape[2]) if mode == "nn" else (b.shape[2], b.shape[1])
    tm = _tile(M, MM_TILE_M, 2 * SUBLANES)

    def body(a_ref, b_ref, o_ref):
        o_ref[...] = _dotf(a_ref[...], b_ref[...], mode)

    return pl.pallas_call(
        body, grid=(M // tm, nb),
        in_specs=[pl.BlockSpec((tm, Ka), lambda i, j: (i, j)),
                  pl.BlockSpec((None,) + b.shape[1:], lambda i, j: (j, 0, 0))],
        out_specs=pl.BlockSpec((tm, Nb), lambda i, j: (i, j)), out_shape=jax.ShapeDtypeStruct((M, nb * Nb), F32),
        name=name, compiler_params=_params(("parallel", "parallel")),
    )(a, b)


def mm_blocks_tn(a, d, nb, name="mm_blocks_tn"):
    K = a.shape[0]
    Ma, Nd = a.shape[1] // nb, d.shape[1] // nb
    tk = _tile(K, MM_TILE_M, 2 * SUBLANES)

    def body(a_ref, d_ref, o_ref):
        k = pl.program_id(1)
        r = _dotf(a_ref[...], d_ref[...], "tn")

        @pl.when(k == 0)
        def _():
            o_ref[...] = r

        @pl.when(k > 0)
        def _():
            o_ref[...] += r

    return pl.pallas_call(
        body, grid=(nb, K // tk),
        in_specs=[pl.BlockSpec((tk, Ma), lambda j, k: (k, j)), pl.BlockSpec((tk, Nd), lambda j, k: (k, j))],
        out_specs=pl.BlockSpec((None, Ma, Nd), lambda j, k: (j, 0, 0)),
        out_shape=jax.ShapeDtypeStruct((nb, Ma, Nd), F32), name=name,
        compiler_params=_params(("parallel", "arbitrary")),
    )(a, d)


def _mm_rows_tiles(M, K):
    return _tile(M, MM_TILE_M // 2, 2 * SUBLANES), _tile(K, MM_TILE_N, LANES)


def mm_add_norm(a, b, x, gain, name):
    (M, K), N = a.shape, b.shape[1]
    tm, tk = _mm_rows_tiles(M, K)
    nk = K // tk

    def body(a_ref, b_ref, x_ref, g_ref, xo_ref, n_ref, acc_ref):
        k = pl.program_id(1)
        r = _dotf(a_ref[...], b_ref[...])

        @pl.when(k == 0)
        def _():
            acc_ref[...] = r

        @pl.when(k > 0)
        def _():
            acc_ref[...] += r

        @pl.when(k == nk - 1)
        def _():
            xn = x_ref[...] + acc_ref[...]
            xo_ref[...] = xn
            n_ref[...] = _rms(xn, g_ref[...]).astype(n_ref.dtype)

    row = pl.BlockSpec((tm, N), lambda i, k: (i, 0))
    return pl.pallas_call(
        body, grid=(M // tm, nk),
        in_specs=[pl.BlockSpec((tm, tk), lambda i, k: (i, k)), pl.BlockSpec((tk, N), lambda i, k: (k, 0)), row,
                  pl.BlockSpec((1, N), lambda i, k: (0, 0))],
        out_specs=[row, row], out_shape=[jax.ShapeDtypeStruct((M, N), F32), jax.ShapeDtypeStruct((M, N), BF16)],
        scratch_shapes=[pltpu.VMEM((tm, N), F32)], name=name, compiler_params=_params(("parallel", "arbitrary")),
    )(a, b, x, gain)


def mm_norm_bwd(a, b, x, gain, dx_res, name):
    (M, K), N = a.shape, b.shape[0]
    tm, tk = _mm_rows_tiles(M, K)
    nk = K // tk

    def body(a_ref, b_ref, x_ref, g_ref, r_ref, dx_ref, dgain_ref, acc_ref):
        i, k = pl.program_id(0), pl.program_id(1)
        r = _dotf(a_ref[...], b_ref[...], "nt")

        @pl.when((i == 0) & (k == 0))
        def _():
            dgain_ref[...] = jnp.zeros(dgain_ref.shape, F32)

        @pl.when(k == 0)
        def _():
            acc_ref[...] = r

        @pl.when(k > 0)
        def _():
            acc_ref[...] += r

        @pl.when(k == nk - 1)
        def _():
            dx, dgain = _vjp(_rms, (x_ref[...], g_ref[...]), acc_ref[...])
            dx_ref[...] = r_ref[...] + dx
            dgain_ref[...] += dgain

    row = pl.BlockSpec((tm, N), lambda i, k: (i, 0))
    vec = pl.BlockSpec((1, N), lambda i, k: (0, 0))
    return pl.pallas_call(
        body, grid=(M // tm, nk),
        in_specs=[pl.BlockSpec((tm, tk), lambda i, k: (i, k)), pl.BlockSpec((N, tk), lambda i, k: (0, k)), row, vec, row],
        out_specs=[row, vec], out_shape=[jax.ShapeDtypeStruct((M, N), F32), jax.ShapeDtypeStruct((1, N), F32)],
        scratch_shapes=[pltpu.VMEM((tm, N), F32)], name=name, compiler_params=_params(("arbitrary", "arbitrary")),
    )(a, b, x, gain, dx_res)


def _roll_rows(x, d, reverse):
    return pltpu.roll(x, (SUBLANES - d) if reverse else d, 0)


def scan_real(a, b, reverse=False, name="scan_real"):
    S, W = b.shape
    cw = _pick(W, (256, 128))
    n_tiles = S // SUBLANES

    def body(a_ref, b_ref, o_ref):
        row = lax.broadcasted_iota(jnp.int32, (SUBLANES, cw), 0)
        edge = 0 if reverse else SUBLANES - 1

        def step(i, carry):
            t = (n_tiles - 1 - i) if reverse else i
            off = pl.multiple_of(t * SUBLANES, SUBLANES)
            A = a_ref[pl.ds(off, SUBLANES), :]
            B = b_ref[pl.ds(off, SUBLANES), :]
            for d in (1, 2, 4):
                m = (row < SUBLANES - d) if reverse else (row >= d)
                B = jnp.where(m, A * _roll_rows(B, d, reverse) + B, B)
                A = jnp.where(m, A * _roll_rows(A, d, reverse), A)
            o_ref[pl.ds(off, SUBLANES), :] = B + A * carry
            at_edge = row == edge
            return (jnp.sum(jnp.where(at_edge, B, 0.0), axis=0, keepdims=True)
                    + jnp.sum(jnp.where(at_edge, A, 0.0), axis=0, keepdims=True) * carry)

        lax.fori_loop(0, n_tiles, step, jnp.zeros((1, cw), F32), unroll=2)

    spec = pl.BlockSpec((S, cw), lambda j: (0, j))
    return pl.pallas_call(
        body, grid=(W // cw,), in_specs=[spec, spec], out_specs=spec,
        out_shape=jax.ShapeDtypeStruct((S, W), F32), name=name, compiler_params=_params(("parallel",)),
    )(a, b)


def scan_cplx(lam, bu, reverse=False, name="scan_cplx"):
    S, C = bu.shape
    half = LANES
    CB = _pick(C, (1024, 512, 256))
    TS = _pick(S, (1024, 512, 256, 128, 64, 32, 16, 8))
    groups = CB // (2 * half)
    n_blocks, n_tiles = S // TS, TS // SUBLANES

    def cmul(ar, ai, br, bi):
        return ar * br - ai * bi, ar * bi + ai * br

    def body(lam_ref, bu_ref, o_ref, carry_ref):
        row = lax.broadcasted_iota(jnp.int32, (SUBLANES, half), 0)

        def edge_row(v):
            return jnp.sum(jnp.where(row == (0 if reverse else SUBLANES - 1), v, 0.0), axis=0, keepdims=True)

        @pl.when(pl.program_id(1) == 0)
        def _():
            carry_ref[...] = jnp.zeros(carry_ref.shape, F32)

        consts = []
        for g in range(groups):
            lr = lam_ref[:, 2 * half * g:2 * half * g + half]
            li = lam_ref[:, 2 * half * g + half:2 * half * (g + 1)]
            if reverse:
                li = -li
            l1 = (lr, li)
            l2 = cmul(*l1, *l1)
            l4 = cmul(*l2, *l2)
            pr = jnp.zeros((SUBLANES, half), F32)
            pi = jnp.zeros((SUBLANES, half), F32)
            p = l1
            for r in range(SUBLANES):
                sel = row == ((SUBLANES - 1 - r) if reverse else r)
                pr = jnp.where(sel, p[0], pr)
                pi = jnp.where(sel, p[1], pi)
                p = cmul(*p, *l1)
            consts.append((l1, l2, l4, pr, pi, edge_row(pr), edge_row(pi)))

        def step(i, carry):
            t = (n_tiles - 1 - i) if reverse else i
            off = pl.multiple_of(t * SUBLANES, SUBLANES)
            out = []
            for g in range(groups):
                l1, l2, l4, pr, pi, p8r, p8i = consts[g]
                cr, ci = carry[2 * g], carry[2 * g + 1]
                re, im = pl.ds(2 * half * g, half), pl.ds(2 * half * g + half, half)
                Br = bu_ref[pl.ds(off, SUBLANES), re]
                Bi = bu_ref[pl.ds(off, SUBLANES), im]
                for d, (qr, qi) in ((1, l1), (2, l2), (4, l4)):
                    m = (row < SUBLANES - d) if reverse else (row >= d)
                    sr, si = _roll_rows(Br, d, reverse), _roll_rows(Bi, d, reverse)
                    nr = jnp.where(m, Br + qr * sr - qi * si, Br)
                    ni = jnp.where(m, Bi + qr * si + qi * sr, Bi)
                    Br, Bi = nr, ni
                o_ref[pl.ds(off, SUBLANES), re] = Br + pr * cr - pi * ci
                o_ref[pl.ds(off, SUBLANES), im] = Bi + pr * ci + pi * cr
                er, ei = edge_row(Br), edge_row(Bi)
                out += [er + p8r * cr - p8i * ci, ei + p8r * ci + p8i * cr]
            return tuple(out)

        carry0 = tuple(carry_ref[:, pl.ds(half * k, half)] for k in range(2 * groups))
        carry1 = lax.fori_loop(0, n_tiles, step, carry0, unroll=2)
        for k in range(2 * groups):
            carry_ref[:, pl.ds(half * k, half)] = carry1[k]

    def rows(j, t):
        return ((n_blocks - 1 - t) if reverse else t, j)

    spec = pl.BlockSpec((TS, CB), rows)
    return pl.pallas_call(
        body, grid=(C // CB, n_blocks), in_specs=[pl.BlockSpec((1, CB), lambda j, t: (0, j)), spec],
        out_specs=spec, out_shape=jax.ShapeDtypeStruct((S, C), F32), scratch_shapes=[pltpu.VMEM((1, CB), F32)],
        name=name, compiler_params=_params(("parallel", "arbitrary")),
    )(lam, bu)


ATTN_HEADS_PER_STEP = 2


def _attn_tile(S, window):
    if window is None:
        return _pick(S, (512, 256, 128))
    return max(window, _pick(S, (256, 128)))


def _attn_valid(q_blk, k_blk, T, window):
    kpos = k_blk * T + lax.broadcasted_iota(jnp.int32, (T, T), 0)
    qpos = q_blk * T + lax.broadcasted_iota(jnp.int32, (T, T), 1)
    valid = kpos <= qpos
    if window is not None:
        valid = valid & (qpos - kpos < window)
    return valid


def attn_fwd(q, k, v, sink, cq=None, ck=None, window=None, name="attn_fwd"):
    H, S, Dh = q.shape
    G = H // k.shape[0]
    HP = ATTN_HEADS_PER_STEP
    assert H % HP == 0 and (G == 1 or G % HP == 0)
    KP = HP if G == 1 else 1
    T = _attn_tile(S, window)
    nq = S // T
    scale = Dh ** -0.5
    bias = cq is not None

    if window is None:
        n_steps = nq * (nq + 1) // 2

        def pair(t):
            i = sum((t >= m * (m + 1) // 2).astype(jnp.int32) for m in range(1, nq)) if nq > 1 else 0 * t
            return i, t - (i * (i + 1)) // 2

        def kv_block(i, j):
            return j
    else:
        n_steps = 2 * nq

        def pair(t):
            return t // 2, t % 2

        def kv_block(i, j):
            return jnp.maximum(i - 1 + j, 0)

    def body(*refs):
        if bias:
            q_ref, k_ref, v_ref, s_ref, cq_ref, ck_ref, o_ref, lse_ref, m_scr, l_scr, acc_scr = refs
        else:
            q_ref, k_ref, v_ref, s_ref, o_ref, lse_ref, m_scr, l_scr, acc_scr = refs
        i, j = pair(pl.program_id(1))

        @pl.when(j == 0)
        def _():
            m_scr[...] = jnp.zeros(m_scr.shape, F32) + s_ref[...]
            l_scr[...] = jnp.ones(l_scr.shape, F32)
            acc_scr[...] = jnp.zeros(acc_scr.shape, F32)

        def block(masked):
            valid = _attn_valid(i, kv_block(i, j), T, window) if masked else None
            for b in range(HP):
                kvb = b if G == 1 else 0
                s = _dotf(k_ref[kvb], q_ref[b], "nt") * scale
                if bias:
                    s = s + cq_ref[b] - ck_ref[b]
                if masked:
                    s = jnp.where(valid, s, NEG)
                m_old = m_scr[b]
                m_new = jnp.maximum(m_old, jnp.max(s, axis=0, keepdims=True))
                alpha = jnp.exp(m_old - m_new)
                p = jnp.exp(s - m_new)
                l_scr[b] = alpha * l_scr[b] + jnp.sum(p, axis=0, keepdims=True)
                acc_scr[b] = alpha * acc_scr[b] + _dotf(v_ref[kvb], p, "tn")
                m_scr[b] = m_new

        if window is None:
            pl.when(j < i)(lambda: block(False))
            pl.when(j == i)(lambda: block(True))
        else:
            pl.when(i - 1 + j >= 0)(lambda: block(True))

        @pl.when(j == (i if window is None else 1))
        def _():
            o_ref[...] = acc_scr[...] / l_scr[...]
            lse_ref[...] = m_scr[...] + jnp.log(l_scr[...])

    def kv_map(hp, t):
        return (hp if G == 1 else (hp * HP) // G, kv_block(*pair(t)), 0)

    in_specs = [
        pl.BlockSpec((HP, T, Dh), lambda hp, t: (hp, pair(t)[0], 0)),
        pl.BlockSpec((KP, T, Dh), kv_map),
        pl.BlockSpec((KP, T, Dh), kv_map),
        pl.BlockSpec((HP, 1, 1), lambda hp, t: (hp, 0, 0)),
    ]
    args = [q, k, v, sink]
    if bias:
        in_specs += [pl.BlockSpec((HP, 1, T), lambda hp, t: (hp, 0, pair(t)[0])),
                     pl.BlockSpec((HP, T, 1), lambda hp, t: (hp, kv_block(*pair(t)), 0))]
        args += [cq, ck]
    return pl.pallas_call(
        body, grid=(H // HP, n_steps), in_specs=in_specs,
        out_specs=[pl.BlockSpec((HP, Dh, T), lambda hp, t: (hp, 0, pair(t)[0])),
                   pl.BlockSpec((HP, 1, T), lambda hp, t: (hp, 0, pair(t)[0]))],
        out_shape=[jax.ShapeDtypeStruct((H, Dh, S), F32), jax.ShapeDtypeStruct((H, 1, S), F32)],
        scratch_shapes=[pltpu.VMEM((HP, 1, T), F32), pltpu.VMEM((HP, 1, T), F32), pltpu.VMEM((HP, Dh, T), F32)],
        name=name, compiler_params=_params(("parallel", "arbitrary")),
    )(*args)


def attn_bwd(q, k, v, lse, do, delta, cq=None, ck=None, window=None, name="attn_bwd"):
    H, S, Dh = q.shape
    KVH = k.shape[0]
    G = H // KVH
    HP = ATTN_HEADS_PER_STEP
    assert H % HP == 0 and (G == 1 or G % HP == 0)
    pair_kv = G == 1
    KP = HP if pair_kv else 1
    T = _attn_tile(S, window)
    nq = S // T
    scale = Dh ** -0.5
    bias = cq is not None
    assert not bias or G == 1

    if window is None:
        assert pair_kv
        n_a, n_j = nq * (nq + 1) // 2, 1

        def start(m):
            return m * nq - (m * (m - 1)) // 2

        def blocks(a, j):
            kb = sum((a >= start(m)).astype(jnp.int32) for m in range(1, nq)) if nq > 1 else 0 * a
            return kb, kb + a - start(kb)
    else:
        n_a, n_j = nq, 2

        def blocks(a, j):
            return a, jnp.minimum(a + j, nq - 1)

    def body(*refs):
        if bias:
            (q_ref, k_ref, v_ref, lse_ref, do_ref, dl_ref, cq_ref, ck_ref,
             dq_ref, dk_ref, dv_ref, dcq_ref, dck_ref) = refs
        else:
            q_ref, k_ref, v_ref, lse_ref, do_ref, dl_ref, dq_ref, dk_ref, dv_ref = refs
        a, gp, j = pl.program_id(1), pl.program_id(2), pl.program_id(3)
        kb, qi = blocks(a, j)
        first = (qi == kb) if window is None else (j == 0)

        @pl.when((gp == 0) & first)
        def _():
            dk_ref[...] = jnp.zeros(dk_ref.shape, F32)
            dv_ref[...] = jnp.zeros(dv_ref.shape, F32)
            if bias:
                dck_ref[...] = jnp.zeros(dck_ref.shape, F32)

        @pl.when((a == 0) & (gp == 0) & (j == 0))
        def _():
            dq_ref[...] = jnp.zeros(dq_ref.shape, F32)
            if bias:
                dcq_ref[...] = jnp.zeros(dcq_ref.shape, F32)

        def block(masked):
            off = pl.multiple_of(qi * T, T)
            valid = _attn_valid(qi, kb, T, window) if masked else None
            for b in range(HP):
                kvb = b if pair_kv else 0
                g = 0 if pair_kv else gp * HP + b
                qb, kk, vv = q_ref[b].astype(MXU_DTYPE), k_ref[kvb].astype(MXU_DTYPE), v_ref[kvb].astype(MXU_DTYPE)
                dob = do_ref[b].astype(MXU_DTYPE)
                s = _dotf(kk, qb, "nt") * scale
                if bias:
                    s = s + cq_ref[b] - ck_ref[b]
                if masked:
                    s = jnp.where(valid, s, NEG)
                p = jnp.exp(s - lse_ref[b])
                dv_ref[kvb] += _dotf(p, dob, "nt")
                ds = p * (_dotf(vv, dob) - dl_ref[b])
                dsb = ds.astype(MXU_DTYPE)
                dk_ref[kvb] += scale * _dotf(dsb, qb)
                dq_ref[kvb, g, pl.ds(off, T), :] += scale * _dotf(dsb, kk, "tn")
                if bias:
                    dcq_ref[kvb, g, :, pl.ds(off, T)] += jnp.sum(ds, axis=0, keepdims=True)
                    dck_ref[kvb] -= jnp.sum(ds, axis=1, keepdims=True)

        if window is None:
            pl.when(qi > kb)(lambda: block(False))
            pl.when(qi == kb)(lambda: block(True))
        else:
            pl.when(kb + j <= nq - 1)(lambda: block(True))

    def qmap(kvp, a, gp, j):
        return (kvp if pair_kv else (kvp * G) // HP + gp, blocks(a, j)[1], 0)

    def qmap_t(kvp, a, gp, j):
        return (kvp if pair_kv else (kvp * G) // HP + gp, 0, blocks(a, j)[1])

    def kmap(kvp, a, gp, j):
        return (kvp, blocks(a, j)[0], 0)

    in_specs = [
        pl.BlockSpec((HP, T, Dh), qmap),
        pl.BlockSpec((KP, T, Dh), kmap),
        pl.BlockSpec((KP, T, Dh), kmap),
        pl.BlockSpec((HP, 1, T), qmap_t),
        pl.BlockSpec((HP, Dh, T), qmap_t),
        pl.BlockSpec((HP, 1, T), qmap_t),
    ]
    args = [q, k, v, lse, do, delta]
    out_specs = [
        pl.BlockSpec((KP, G, S, Dh), lambda kvp, a, gp, j: (kvp, 0, 0, 0)),
        pl.BlockSpec((KP, T, Dh), kmap),
        pl.BlockSpec((KP, T, Dh), kmap),
    ]
    out_shape = [jax.ShapeDtypeStruct((KVH, G, S, Dh), F32), jax.ShapeDtypeStruct((KVH, S, Dh), F32),
                 jax.ShapeDtypeStruct((KVH, S, Dh), F32)]
    if bias:
        in_specs += [pl.BlockSpec((HP, 1, T), qmap_t), pl.BlockSpec((HP, T, 1), kmap)]
        args += [cq, ck]
        out_specs += [pl.BlockSpec((KP, G, 1, S), lambda kvp, a, gp, j: (kvp, 0, 0, 0)),
                      pl.BlockSpec((KP, T, 1), kmap)]
        out_shape += [jax.ShapeDtypeStruct((KVH, G, 1, S), F32), jax.ShapeDtypeStruct((KVH, S, 1), F32)]
    res = pl.pallas_call(
        body, grid=(KVH // KP, n_a, 1 if pair_kv else G // HP, n_j), in_specs=in_specs, out_specs=out_specs,
        out_shape=out_shape, name=name, compiler_params=_params(("arbitrary", "arbitrary", "arbitrary", "arbitrary")),
    )(*args)
    dq = res[0].reshape(H, S, Dh)
    if bias:
        return dq, res[1], res[2], res[3].reshape(H, 1, S), res[4]
    return dq, res[1], res[2]


def _rms(x, g):
    return x * lax.rsqrt(jnp.mean(x * x, axis=-1, keepdims=True) + EPS) * g


def _sigmoid(x):
    return 1.0 / (1.0 + jnp.exp(-x))


def _softplus(x):
    return jnp.maximum(x, 0.0) + jnp.log(1.0 + jnp.exp(-jnp.abs(x)))


def _log_sigmoid(x):
    return jnp.minimum(x, 0.0) - jnp.log(1.0 + jnp.exp(-jnp.abs(x)))


def _gelu(x):
    return 0.5 * x * (1.0 + jnp.tanh(math.sqrt(2.0 / math.pi) * (x + 0.044715 * (x * x * x))))


def _silu(x):
    return x * _sigmoid(x)


def _ffn_act(gu):
    f = gu.shape[1] // 2
    return MACARON * _silu(gu[:, :f]) * gu[:, f:]


def _qk_prep(rope):
    def f(x, *rest):
        if rope:
            cos, sin, g, rot = rest
        else:
            (g,) = rest
        y = _rms(x, g)
        if rope:
            y = y * cos + jnp.dot(y, rot, precision=HI, preferred_element_type=F32) * sin
        return y
    return f


def _lru_gates(pre, xc, ba, bx, lam):
    w = xc.shape[1]
    r = _sigmoid(pre[:, :w] + ba)
    i = _sigmoid(pre[:, w:] + bx)
    log_a = -LRU_C * r * _softplus(lam)
    a = jnp.exp(log_a)
    b = jnp.sqrt(1.0 - jnp.exp(2.0 * log_a)) * (i * xc)
    return a, b


def _lru_conv(x0, x1, x2, x3, w0, w1, w2, w3, cb):
    return cb + x0 * w0 + x1 * w1 + x2 * w2 + x3 * w3


def _s5_params(lre, lim, ldt, gsel, bre, bim):
    dt = jnp.sum(gsel * jnp.exp(ldt), axis=1, keepdims=True)
    er = jnp.exp(lre * dt)
    ang = lim * dt
    lbr, lbi = er * jnp.cos(ang), er * jnp.sin(ang)
    nr, ni = lbr - 1.0, lbi
    den = lre * lre + lim * lim
    fr, fi = (nr * lre + ni * lim) / den, (ni * lre - nr * lim) / den
    return lbr, lbi, fr * bre - fi * bim, fr * bim + fi * bre


def _s5_out(yssm, u, d):
    return _gelu(yssm + d * u)


def _glu(z, gl, gb):
    return z * _sigmoid(gl + gb)


def _ple_out(x, gpre, epre, pn):
    return x + _sigmoid(gpre) * _rms(epre, pn)


def _vjp(fn, args, cots):
    _, pull = jax.vjp(fn, *args)
    return pull(cots)


def add_norm(x, y, g, name):
    D = x.shape[1]
    if y is None:
        return x, rowwise(lambda xv, gv: ([_rms(xv, gv)], []), [x], [g], outs=[(D, BF16)], name=name)[0]
    xn, n = rowwise(lambda xv, yv, gv: ([xv + yv, _rms(xv + yv, gv)], []), [x, y], [g],
                    outs=[(D, F32), (D, BF16)], name=name)
    return xn, n


def norm_bwd(x, g, dn, dx_res, name):
    D = x.shape[1]

    def f(xv, dnv, dxv, gv):
        dx, dg = _vjp(_rms, (xv, gv), dnv)
        return [dxv + dx], [dg]

    return rowwise(f, [x, dn, dx_res], [g], outs=[(D, F32)], accs=[(1, D)], name=name)


def _swiglu(g, u):
    return MACARON * _silu(g) * u


def _dotf(a, b, mode="nn"):
    return lax.dot_general(a.astype(MXU_DTYPE), b.astype(MXU_DTYPE), _DOT_DIMS[mode], preferred_element_type=F32)


def ffn_up(n, wgu, ig, iu, name):
    S, D = n.shape
    C, _, _, Fc = wgu.shape
    tm = _tile(S, MM_TILE_M, 2 * SUBLANES)

    def body(n_ref, wg_ref, wu_ref, g_ref, u_ref, a_ref):
        g = _dotf(n_ref[...], wg_ref[...])
        u = _dotf(n_ref[...], wu_ref[...])
        g_ref[...] = g.astype(g_ref.dtype)
        u_ref[...] = u.astype(u_ref.dtype)
        a_ref[...] = _swiglu(g, u).astype(a_ref.dtype)

    hid = pl.BlockSpec((None, tm, Fc), lambda s, i: (s, i, 0))
    return pl.pallas_call(
        body, grid=(C, S // tm),
        in_specs=[pl.BlockSpec((tm, D), lambda s, i: (i, 0)),
                  pl.BlockSpec((None, None, D, Fc), lambda s, i: (s, ig, 0, 0)),
                  pl.BlockSpec((None, None, D, Fc), lambda s, i: (s, iu, 0, 0))],
        out_specs=[hid, hid, hid], out_shape=[jax.ShapeDtypeStruct((C, S, Fc), BF16)] * 3,
        name=name, compiler_params=_params(("parallel", "parallel")),
    )(n, wgu, wgu)


def ffn_down(act, wd, iw, x, gain, name):
    C, S, Fc = act.shape
    D = wd.shape[-1]
    tm = _tile(S, MM_TILE_M, 2 * SUBLANES)

    def body(a_ref, w_ref, x_ref, g_ref, xo_ref, n_ref, acc_ref):
        s = pl.program_id(1)
        r = _dotf(a_ref[...], w_ref[...])

        @pl.when(s == 0)
        def _():
            acc_ref[...] = r

        @pl.when(s > 0)
        def _():
            acc_ref[...] += r

        @pl.when(s == C - 1)
        def _():
            xn = x_ref[...] + acc_ref[...]
            xo_ref[...] = xn
            n_ref[...] = _rms(xn, g_ref[...]).astype(n_ref.dtype)

    row = pl.BlockSpec((tm, D), lambda i, s: (i, 0))
    return pl.pallas_call(
        body, grid=(S // tm, C),
        in_specs=[pl.BlockSpec((None, tm, Fc), lambda i, s: (s, i, 0)),
                  pl.BlockSpec((None, None, Fc, D), lambda i, s: (s, iw, 0, 0)), row,
                  pl.BlockSpec((1, D), lambda i, s: (0, 0))],
        out_specs=[row, row], out_shape=[jax.ShapeDtypeStruct((S, D), F32), jax.ShapeDtypeStruct((S, D), BF16)],
        scratch_shapes=[pltpu.VMEM((tm, D), F32)], name=name, compiler_params=_params(("parallel", "arbitrary")),
    )(act, wd, x, gain)


def ffn_down_bwd(dy, wd, iw, g, u, name):
    C, S, Fc = g.shape
    D = dy.shape[1]
    tm = _tile(S, MM_TILE_M, 2 * SUBLANES)

    def body(dy_ref, w_ref, g_ref, u_ref, dg_ref, du_ref):
        dact = MACARON * _dotf(dy_ref[...], w_ref[...], "nt")
        g, u = g_ref[...].astype(F32), u_ref[...].astype(F32)
        sg = _sigmoid(g)
        gs = g * sg
        dg_ref[...] = (dact * u * (sg + gs * (1.0 - sg))).astype(dg_ref.dtype)
        du_ref[...] = (dact * gs).astype(du_ref.dtype)

    hid = pl.BlockSpec((None, tm, Fc), lambda s, i: (s, i, 0))
    return pl.pallas_call(
        body, grid=(C, S // tm),
        in_specs=[pl.BlockSpec((tm, D), lambda s, i: (i, 0)),
                  pl.BlockSpec((None, None, Fc, D), lambda s, i: (s, iw, 0, 0)), hid, hid],
        out_specs=[hid, hid], out_shape=[jax.ShapeDtypeStruct((C, S, Fc), BF16)] * 2,
        name=name, compiler_params=_params(("parallel", "parallel")),
    )(dy, wd, g, u)


def ffn_dn(dg, du, wgu, ig, iu, x, gain, dx_res, name):
    C, S, Fc = dg.shape
    D = wgu.shape[2]
    tm = _tile(S, MM_TILE_M, 2 * SUBLANES)
    parts = 2 if tm % (4 * SUBLANES) == 0 else 1

    def body(dg_ref, du_ref, wg_ref, wu_ref, x_ref, g_ref, r_ref, dx_ref, dgain_ref, acc_ref):
        i, s = pl.program_id(0), pl.program_id(1)
        r = _dotf(dg_ref[...], wg_ref[...], "nt") + _dotf(du_ref[...], wu_ref[...], "nt")

        @pl.when((i == 0) & (s == 0))
        def _():
            dgain_ref[...] = jnp.zeros(dgain_ref.shape, F32)

        @pl.when(s == 0)
        def _():
            acc_ref[...] = r

        @pl.when(s > 0)
        def _():
            acc_ref[...] += r

        @pl.when(s == C - 1)
        def _():
            for part in range(parts):
                rows = pl.ds(part * (tm // parts), tm // parts)
                dx, dgain = _vjp(_rms, (x_ref[rows, :], g_ref[...]), acc_ref[rows, :])
                dx_ref[rows, :] = r_ref[rows, :] + dx
                dgain_ref[...] += dgain

    hid = pl.BlockSpec((None, tm, Fc), lambda i, s: (s, i, 0))
    row = pl.BlockSpec((tm, D), lambda i, s: (i, 0))
    vec = pl.BlockSpec((1, D), lambda i, s: (0, 0))
    return pl.pallas_call(
        body, grid=(S // tm, C),
        in_specs=[hid, hid, pl.BlockSpec((None, None, D, Fc), lambda i, s: (s, ig, 0, 0)),
                  pl.BlockSpec((None, None, D, Fc), lambda i, s: (s, iu, 0, 0)), row, vec, row],
        out_specs=[row, vec], out_shape=[jax.ShapeDtypeStruct((S, D), F32), jax.ShapeDtypeStruct((1, D), F32)],
        scratch_shapes=[pltpu.VMEM((tm, D), F32)], name=name, compiler_params=_params(("arbitrary", "arbitrary")),
    )(dg, du, wgu, wgu, x, gain, dx_res)


def ffn_dw(a, d, buf, idx, shape, blocked, name):
    C, P, M, N = shape
    S = d.shape[-2]
    tk = _tile(S, MM_TILE_M, 2 * SUBLANES)
    nk = S // tk

    def body(*refs):
        a_ref, d_ref, o_ref, acc_ref = refs[0], refs[1], refs[-2], refs[-1]
        k = pl.program_id(1)
        r = _dotf(a_ref[...], d_ref[...], "tn")

        @pl.when(k == 0)
        def _():
            acc_ref[...] = r

        @pl.when(k > 0)
        def _():
            acc_ref[...] += r

        @pl.when(k == nk - 1)
        def _():
            o_ref[...] = acc_ref[...].astype(o_ref.dtype)

    if blocked == "a":
        a_spec = pl.BlockSpec((None, tk, M), lambda s, k: (s, k, 0))
        d_spec = pl.BlockSpec((tk, N), lambda s, k: (k, 0))
    else:
        a_spec = pl.BlockSpec((tk, M), lambda s, k: (k, 0))
        d_spec = pl.BlockSpec((None, tk, N), lambda s, k: (s, k, 0))
    out_spec = pl.BlockSpec((None, None, M, N), lambda s, k: (s, idx, 0, 0))
    out_shape = jax.ShapeDtypeStruct(tuple(shape), BF16)
    scratch = [pltpu.VMEM((M, N), F32)]
    if buf is None:
        return pl.pallas_call(body, grid=(C, nk), in_specs=[a_spec, d_spec], out_specs=out_spec, out_shape=out_shape,
                              scratch_shapes=scratch, name=name,
                              compiler_params=_params(("parallel", "arbitrary")))(a, d)
    return pl.pallas_call(body, grid=(C, nk), in_specs=[a_spec, d_spec, pl.BlockSpec(memory_space=pl.ANY)],
                          out_specs=out_spec, out_shape=out_shape, input_output_aliases={2: 0},
                          scratch_shapes=scratch, name=name,
                          compiler_params=_params(("parallel", "arbitrary")))(a, d, buf)


def ffn_fwd(n, x, gain, wgu, wd, ig, iu, iw, tag):
    g, u, act = ffn_up(n, wgu, ig, iu, f"ffn_up_{tag}")
    x_new, n_new = ffn_down(act, wd, iw, x, gain, f"ffn_down_{tag}")
    return x_new, n_new, (n, g, u, act)


def ffn_bwd(dy, saved, x, gain, wgu, wd, ig, iu, iw, gbuf, tag):
    n, g, u, act = saved
    gwgu, gwd = gbuf
    dg, du = ffn_down_bwd(dy, wd, iw, g, u, f"ffn_down_bwd_{tag}")
    gwd = ffn_dw(act, dy, gwd, iw, (N_CHIPS,) + wd.shape[1:], "a", f"ffn_dwd_{tag}")
    dx, dgain = ffn_dn(dg, du, wgu, ig, iu, x, gain, dy, f"ffn_dn_{tag}")
    gwgu = ffn_dw(n, dg, gwgu, ig, (N_CHIPS,) + wgu.shape[1:], "d", f"ffn_dwg_{tag}")
    gwgu = ffn_dw(n, du, gwgu, iu, (N_CHIPS,) + wgu.shape[1:], "d", f"ffn_dwu_{tag}")
    return dx, dgain, (gwgu, gwd)


def _heads(x, H):
    S = x.shape[0]
    return x.reshape(S, H, HEAD_DIM).transpose(1, 0, 2)


def _unheads(x):
    H, S, _ = x.shape
    return x.transpose(1, 0, 2).reshape(S, H * HEAD_DIM)


def _heads_t(x, H):
    return x.T.reshape(H, HEAD_DIM, x.shape[0])


def _unheads_t(x):
    return x.reshape(x.shape[0] * x.shape[1], x.shape[2]).T


def _shift_down(x, n=1):
    return jnp.pad(x, ((n, 0), (0, 0)))[:x.shape[0]]


def _shift_up(x, n=1):
    return jnp.pad(x, ((0, n), (0, 0)))[n:]


def _block_diag(w):
    B, I, J = w.shape
    eye = jnp.eye(B, dtype=w.dtype)
    return (w[:, :, None, :] * eye[:, None, :, None]).reshape(B * I, B * J)


def _block_diag_take(x, B):
    I, J = x.shape[0] // B, x.shape[1] // B
    eye = jnp.eye(B, dtype=x.dtype)
    return jnp.sum(x.reshape(B, I, B, J) * eye[:, None, :, None], axis=2)


def _rope_tables(S):
    half = HEAD_DIM // 2
    inv = jnp.power(ROPE_THETA, -jnp.arange(half, dtype=F32) / half)
    ang = jnp.arange(S, dtype=F32)[:, None] * inv[None, :]
    cos = jnp.concatenate([jnp.cos(ang), jnp.cos(ang)], axis=1)
    sin = jnp.concatenate([jnp.sin(ang), jnp.sin(ang)], axis=1)
    r = jnp.arange(HEAD_DIM)[:, None]
    c = jnp.arange(HEAD_DIM)[None, :]
    rot = jnp.where(r == c + half, -1.0, 0.0) + jnp.where(c == r + half, 1.0, 0.0)
    return cos, sin, rot.astype(F32)


def qk_prep_fwd(x_hm, g, rope_tabs, name):
    H, S, Dh = x_hm.shape
    rows = [x_hm.reshape(H * S, Dh)]
    consts = [g.reshape(1, Dh)]
    periods = [None]
    if rope_tabs is not None:
        rows += [rope_tabs[0], rope_tabs[1]]
        consts += [rope_tabs[2]]
        periods += [S, S]
    fn = _qk_prep(rope_tabs is not None)
    y = rowwise(lambda *a: ([fn(*a)], []), rows, consts, outs=[(Dh, F32)], name=name, periods=periods)[0]
    return y.reshape(H, S, Dh)


def qk_prep_bwd(x_hm, g, rope_tabs, dy_hm, name):
    H, S, Dh = x_hm.shape
    rope = rope_tabs is not None
    rows = [x_hm.reshape(H * S, Dh), dy_hm.reshape(H * S, Dh)]
    consts = [g.reshape(1, Dh)]
    periods = [None, None]
    if rope:
        rows += [rope_tabs[0], rope_tabs[1]]
        consts += [rope_tabs[2]]
        periods += [S, S]
    fn = _qk_prep(rope)

    def f(xv, dyv, *rest):
        if rope:
            cos, sin, gv, rot = rest
            dx, dg = _vjp(lambda a, b: fn(a, cos, sin, b, rot), (xv, gv), dyv)
        else:
            (gv,) = rest
            dx, dg = _vjp(fn, (xv, gv), dyv)
        return [dx], [dg]

    dx, dg = rowwise(f, rows, consts, outs=[(Dh, F32)], accs=[(1, Dh)], name=name, periods=periods)
    return dx.reshape(H, S, Dh), dg.reshape(Dh)


def attn_delta(do_t, o_t, name):
    H, Dh, S = o_t.shape

    def body(a_ref, b_ref, o_ref):
        o_ref[...] = jnp.sum(a_ref[...] * b_ref[...], axis=0, keepdims=True)

    spec = pl.BlockSpec((None, Dh, S), lambda h: (h, 0, 0))
    return pl.pallas_call(
        body, grid=(H,), in_specs=[spec, spec], out_specs=pl.BlockSpec((None, 1, S), lambda h: (h, 0, 0)),
        out_shape=jax.ShapeDtypeStruct((H, 1, S), F32), name=name, compiler_params=_params(("parallel",)),
    )(do_t, o_t)


def even_mixer_fwd(h, w, tag):
    S = h.shape[0]
    W = 512
    H = 8
    z = mm(h, w["w_in"], name=f"ev_in_{tag}")
    xa, ya, q, k, v, f = (z[:, 0:512], z[:, 512:1024], z[:, 1024:1536], z[:, 1536:2048], z[:, 2048:2560],
                          z[:, 2560:2688])
    xs = [_shift_down(xa, LRU_CONV - 1 - tap) for tap in range(LRU_CONV)]
    taps = [w["conv_w"][tap][None] for tap in range(LRU_CONV)]
    xc = rowwise(lambda *a: ([_lru_conv(*a)], []), xs, taps + [w["conv_b"]], outs=[(W, F32)],
                 name=f"lru_conv_{tag}")[0]
    pre = mm(xc, w["w_ax"], name=f"lru_gates_mm_{tag}")
    a, b = rowwise(lambda p_, x_, ba, bx, lam: (list(_lru_gates(p_, x_, ba, bx, lam)), []), [pre, xc],
                   [w["ba"], w["bx"], w["lam"]], outs=[(W, F32), (W, F32)], name=f"lru_gates_{tag}")
    hs = scan_real(a, b, name=f"lru_scan_{tag}")
    a_out = rowwise(lambda y_, h_: ([_gelu(y_) * h_], []), [ya, hs], outs=[(W, F32)], name=f"lru_out_{tag}")[0]
    lf = rowwise(lambda f_, bf: ([_log_sigmoid(f_ + bf)], []), [f], [w["bf"]], outs=[(LANES, F32)],
                 name=f"fox_logf_{tag}")[0]
    c = scan_real(jnp.ones_like(lf), lf, name=f"fox_cumsum_{tag}")
    c_hm = c[:, :H].T
    q_hm, k_hm, v_hm = _heads(q, H), _heads(k, H), _heads(v, H)
    qn = qk_prep_fwd(q_hm, w["qn"], None, f"fox_qprep_{tag}")
    kn = qk_prep_fwd(k_hm, w["kn"], None, f"fox_kprep_{tag}")
    sink = jnp.full((H, 1, 1), NEG, F32)
    o_hm, lse = attn_fwd(qn, kn, v_hm, sink, c_hm[:, None, :], c_hm[:, :, None], name=f"fox_attn_{tag}")
    mo = jnp.concatenate([a_out, _unheads_t(o_hm)], axis=1).astype(BF16)
    saved = dict(h=h, xs=xs, xc=xc, pre=pre, a=a, hs=hs, ya=ya, f=f, c_hm=c_hm, q_hm=q_hm, k_hm=k_hm,
                 v_hm=v_hm, qn=qn, kn=kn, o_hm=o_hm, lse=lse, mo=mo)
    return mo, saved


def even_mixer_bwd(dy, sv, w, tag):
    W = 512
    H = 8
    S = dy.shape[0]
    g = {}
    dmo = mm(dy, w["w_out"], "nt", name=f"ev_dmo_{tag}")
    g["w_out"] = mm(sv["mo"], dy, "tn", name=f"ev_dwout_{tag}")
    da_out, do = dmo[:, :W], dmo[:, W:]
    do_hm = _heads_t(do, H)
    delta = attn_delta(do_hm, sv["o_hm"], f"fox_delta_{tag}")
    c_hm = sv["c_hm"]
    dqn, dkn, dv_hm, dcq, dck = attn_bwd(sv["qn"], sv["kn"], sv["v_hm"], sv["lse"], do_hm, delta,
                                          c_hm[:, None, :], c_hm[:, :, None], name=f"fox_attn_bwd_{tag}")
    dq_hm, g["qn"] = qk_prep_bwd(sv["q_hm"], w["qn"], None, dqn, f"fox_qprep_bwd_{tag}")
    dk_hm, g["kn"] = qk_prep_bwd(sv["k_hm"], w["kn"], None, dkn, f"fox_kprep_bwd_{tag}")
    dc = (dcq[:, 0, :] + dck[:, :, 0]).T
    dc = jnp.pad(dc, ((0, 0), (0, LANES - H)))
    dlf = scan_real(jnp.ones_like(dc), dc, reverse=True, name=f"fox_cumsum_bwd_{tag}")

    def f_logf(f_, d_, bf):
        df, dbf = _vjp(lambda a_, b_: _log_sigmoid(a_ + b_), (f_, bf), d_)
        return [df], [dbf]

    df, dbf = rowwise(f_logf, [sv["f"], dlf], [w["bf"]], outs=[(LANES, F32)], accs=[(1, LANES)],
                      name=f"fox_logf_bwd_{tag}")
    g["bf"] = dbf[0, :H]
    def f_out(y_, h_, d_):
        dyv, dhv = _vjp(lambda a_, b_: _gelu(a_) * b_, (y_, h_), d_)
        return [dyv, dhv], []

    dya, dhs = rowwise(f_out, [sv["ya"], sv["hs"], da_out], outs=[(W, F32), (W, F32)], name=f"lru_out_bwd_{tag}")
    gs = scan_real(_shift_up(sv["a"]), dhs, reverse=True, name=f"lru_scan_bwd_{tag}")

    def f_gates(p_, x_, g_, hp_, ba, bx, lam):
        dp, dx, dba, dbx, dlam = _vjp(_lru_gates, (p_, x_, ba, bx, lam), (g_ * hp_, g_))
        return [dp, dx], [dba, dbx, dlam]

    dpre, dxc, dba, dbx, dlam = rowwise(f_gates, [sv["pre"], sv["xc"], gs, _shift_down(sv["hs"])],
                                        [w["ba"], w["bx"], w["lam"]], outs=[(2 * W, BF16), (W, F32)],
                                        accs=[(1, W)] * 3, name=f"lru_gates_bwd_{tag}")
    g["ba"], g["bx"], g["lam"] = dba[0], dbx[0], dlam[0]
    dxc2 = mm(dpre, w["w_ax"], "nt", name=f"lru_gates_mm_dx_{tag}")
    g["w_ax"] = mm(sv["xc"], dpre, "tn", name=f"lru_gates_mm_dw_{tag}")

    def f_conv(d1, d2, x0, x1, x2, x3):
        d = d1 + d2
        return [d], [jnp.sum(d, axis=0, keepdims=True)] + [jnp.sum(d * xv, axis=0, keepdims=True)
                                                           for xv in (x0, x1, x2, x3)]

    dxc_t, dcb, dw0, dw1, dw2, dw3 = rowwise(f_conv, [dxc, dxc2] + sv["xs"], outs=[(W, F32)],
                                             accs=[(1, W)] * 5, name=f"lru_conv_bwd_{tag}")
    g["conv_b"] = dcb[0]
    g["conv_w"] = jnp.concatenate([dw0, dw1, dw2, dw3], axis=0)
    ds_ = [_shift_up(dxc_t, LRU_CONV - 1 - tap) for tap in range(LRU_CONV)]
    taps = [w["conv_w"][tap][None] for tap in range(LRU_CONV)]
    dxa = rowwise(lambda a, b, c, d, w0, w1, w2, w3: ([a * w0 + b * w1 + c * w2 + d * w3], []), ds_, taps,
                  outs=[(W, F32)], name=f"lru_conv_dx_{tag}")[0]
    dz = jnp.concatenate([dxa, dya, _unheads(dq_hm), _unheads(dk_hm), _unheads(dv_hm), df], axis=1).astype(BF16)
    g["w_in"] = mm(sv["h"], dz, "tn", name=f"ev_dwin_{tag}")
    return dz, g


def odd_mixer_fwd(h, w, tag):
    S = h.shape[0]
    H, KVH = 8, 2
    z = mm(h, w["w_in"], name=f"od_in_{tag}")
    q, k, v, u = z[:, 0:512], z[:, 512:640], z[:, 640:768], z[:, 768:1280]
    tabs = _rope_tables(S)
    q_hm, k_hm, v_hm = _heads(q, H), _heads(k, KVH), _heads(v, KVH)
    qn = qk_prep_fwd(q_hm, w["qn"], tabs, f"swa_qprep_{tag}")
    kn = qk_prep_fwd(k_hm, w["kn"], tabs, f"swa_kprep_{tag}")
    sink = w["sinks"].reshape(H, 1, 1)
    o_hm, lse = attn_fwd(qn, kn, v_hm, sink, window=SWA_WINDOW, name=f"swa_attn_{tag}")
    lam, bexp = w["s5_lam"], w["s5_bexp"]
    bu = mm_blocks(u, bexp, name=f"s5_bu_{tag}")
    hs = scan_cplx(lam, bu, name=f"s5_scan_{tag}")
    yssm = mm_blocks(hs, w["s5_cexp"], name=f"s5_y_{tag}")
    zz = rowwise(lambda y_, u_, d_: ([_s5_out(y_, u_, d_)], []), [yssm, u], [w["s5_d"]], outs=[(512, F32)],
                 name=f"s5_gelu_{tag}")[0]
    gl = mm(zz, w["glu_w"], name=f"s5_glu_mm_{tag}")
    d_out = rowwise(lambda z_, g_, b_: ([_glu(z_, g_, b_)], []), [zz, gl], [w["glu_b"]], outs=[(512, F32)],
                    name=f"s5_glu_{tag}")[0]
    mo = jnp.concatenate([_unheads_t(o_hm), d_out], axis=1).astype(BF16)
    saved = dict(h=h, q_hm=q_hm, k_hm=k_hm, v_hm=v_hm, qn=qn, kn=kn, o_hm=o_hm, lse=lse, u=u, hs=hs, yssm=yssm,
                 zz=zz, gl=gl, mo=mo, tabs=tabs)
    return mo, saved


def odd_mixer_bwd(dy, sv, w, tag):
    H, KVH = 8, 2
    g = {}
    dmo = mm(dy, w["w_out"], "nt", name=f"od_dmo_{tag}")
    g["w_out"] = mm(sv["mo"], dy, "tn", name=f"od_dwout_{tag}")
    do, dd = dmo[:, :512], dmo[:, 512:]
    do_hm = _heads_t(do, H)
    delta = attn_delta(do_hm, sv["o_hm"], f"swa_delta_{tag}")
    dqn, dkn, dv_hm = attn_bwd(sv["qn"], sv["kn"], sv["v_hm"], sv["lse"], do_hm, delta, window=SWA_WINDOW,
                               name=f"swa_attn_bwd_{tag}")
    dq_hm, g["qn"] = qk_prep_bwd(sv["q_hm"], w["qn"], sv["tabs"], dqn, f"swa_qprep_bwd_{tag}")
    dk_hm, g["kn"] = qk_prep_bwd(sv["k_hm"], w["kn"], sv["tabs"], dkn, f"swa_kprep_bwd_{tag}")
    lse_t, delta_t = sv["lse"][:, 0, :].T, delta[:, 0, :].T
    g["sinks"] = rowwise(lambda l_, d_, s_: ([], [jnp.sum(-jnp.exp(s_ - l_) * d_, axis=0, keepdims=True)]),
                         [lse_t, delta_t], [w["sinks"].reshape(1, H)], accs=[(1, H)], name=f"swa_dsink_{tag}")[0][0]
    def f_glu(z_, g_, d_, b_):
        dz_, dg_, db_ = _vjp(_glu, (z_, g_, b_), d_)
        return [dz_, dg_], [db_]

    dzz1, dgl, dglb = rowwise(f_glu, [sv["zz"], sv["gl"], dd], [w["glu_b"]], outs=[(512, F32), (512, BF16)],
                              accs=[(1, 512)], name=f"s5_glu_bwd_{tag}")
    g["glu_b"] = dglb[0]
    g["glu_w"] = mm(sv["zz"], dgl, "tn", name=f"s5_glu_dw_{tag}")
    dzz2 = mm(dgl, w["glu_w"], "nt", name=f"s5_glu_dz_{tag}")

    def f_gelu(y_, u_, d1, d2, dpar):
        dy_, du_, dd_ = _vjp(_s5_out, (y_, u_, dpar), d1 + d2)
        return [dy_, du_], [dd_]

    dyssm, du1, dsd = rowwise(f_gelu, [sv["yssm"], sv["u"], dzz1, dzz2], [w["s5_d"]],
                              outs=[(512, F32), (512, F32)], accs=[(1, 512)], name=f"s5_gelu_bwd_{tag}")
    g["s5_d"] = dsd[0]
    dhs = mm_blocks(dyssm, w["s5_cexp"], "nt", name=f"s5_dh_{tag}")
    g["s5_cexp"] = _block_diag(mm_blocks_tn(sv["hs"], dyssm, S5_BLOCKS, name=f"s5_dc_{tag}"))
    gs = scan_cplx(w["s5_lam"], dhs, reverse=True, name=f"s5_scan_bwd_{tag}")
    g["s5_bexp"] = _block_diag(mm_blocks_tn(sv["u"], gs, S5_BLOCKS, name=f"s5_db_{tag}"))
    du2 = mm_blocks(gs, w["s5_bexp"], "nt", name=f"s5_du_{tag}")

    def f_dlam(g_, hp_):
        C = g_.shape[1]
        outs_r, outs_i = [], []
        for j in range(C // (2 * LANES)):
            gr, gi = g_[:, 2 * LANES * j:2 * LANES * j + LANES], g_[:, 2 * LANES * j + LANES:2 * LANES * (j + 1)]
            hr, hi = hp_[:, 2 * LANES * j:2 * LANES * j + LANES], hp_[:, 2 * LANES * j + LANES:2 * LANES * (j + 1)]
            outs_r.append(jnp.sum(gr * hr + gi * hi, axis=0, keepdims=True))
            outs_i.append(jnp.sum(gi * hr - gr * hi, axis=0, keepdims=True))
        return [], [jnp.concatenate([x for pair in zip(outs_r, outs_i) for x in pair], axis=1)]

    g["s5_lam"] = rowwise(f_dlam, [gs, _shift_down(sv["hs"])], accs=[(1, gs.shape[1])], name=f"s5_dlam_{tag}")[0]
    du = rowwise(lambda a_, b_: ([a_ + b_], []), [du1, du2], outs=[(512, F32)], name=f"s5_du_add_{tag}")[0]
    dz = jnp.concatenate([_unheads(dq_hm), _unheads(dk_hm), _unheads(dv_hm), du], axis=1).astype(BF16)
    g["w_in"] = mm(sv["h"], dz, "tn", name=f"od_dwin_{tag}")
    return dz, g


def _s5_cols(x_re, x_im):
    n = x_re.shape[0] // LANES
    return jnp.stack([x_re.reshape(n, LANES), x_im.reshape(n, LANES)], axis=1).reshape(1, 2 * n * LANES)


def _s5_uncols(x):
    n = x.shape[1] // (2 * LANES)
    y = x.reshape(n, 2, LANES)
    return y[:, 0].reshape(-1), y[:, 1].reshape(-1)


def _s5_gsel():
    return jnp.repeat(jnp.eye(S5_GROUPS, dtype=F32), S5_STATE, axis=0)


def s5_prep_fwd(lre, lim, ldt, bre, bim, cre, cim, tag):
    GP = S5_GROUPS * S5_STATE
    ins = [lre.reshape(GP, 1), lim.reshape(GP, 1), ldt.reshape(1, S5_GROUPS), _s5_gsel(),
           bre.reshape(GP, S5_GROUP), bim.reshape(GP, S5_GROUP)]
    lbr, lbi, bbr, bbi = whole(_s5_params, ins, [((GP, 1), F32)] * 2 + [((GP, S5_GROUP), F32)] * 2,
                               name=f"s5_params_{tag}")
    lam = _s5_cols(lbr[:, 0], lbi[:, 0])

    def expand_b(bb):
        return _block_diag(bb.reshape(S5_GROUPS, S5_STATE, S5_GROUP).transpose(0, 2, 1))

    n = GP // LANES
    bexp = jnp.stack([expand_b(bbr).reshape(-1, n, LANES), expand_b(bbi).reshape(-1, n, LANES)],
                     axis=2).reshape(-1, 2 * GP)
    c_r = _block_diag(cre.transpose(0, 2, 1))
    c_i = _block_diag(cim.transpose(0, 2, 1))
    cexp = jnp.stack([c_r.reshape(n, LANES, -1), -c_i.reshape(n, LANES, -1)], axis=1).reshape(2 * GP, -1)
    cb, sb = bexp.shape[0] // S5_BLOCKS, bexp.shape[1] // S5_BLOCKS
    bexp = jnp.stack([bexp[cb * j:cb * (j + 1), sb * j:sb * (j + 1)] for j in range(S5_BLOCKS)])
    cexp = jnp.stack([cexp[sb * j:sb * (j + 1), cb * j:cb * (j + 1)] for j in range(S5_BLOCKS)])
    return lam, bexp.astype(BF16), cexp.astype(BF16), ins


def s5_prep_bwd(ins, dlam, dbexp, dcexp, tag):
    GP = S5_GROUPS * S5_STATE
    n = GP // LANES
    dlr, dli = _s5_uncols(dlam)
    db = dbexp.reshape(-1, n, 2, LANES)

    def take_b(x):
        return _block_diag_take(x, S5_GROUPS).transpose(0, 2, 1).reshape(GP, S5_GROUP)

    dbbr, dbbi = take_b(db[:, :, 0].reshape(-1, GP)), take_b(db[:, :, 1].reshape(-1, GP))
    dc = dcexp.reshape(n, 2, LANES, -1)
    dcre = _block_diag_take(dc[:, 0].reshape(GP, -1), S5_GROUPS).transpose(0, 2, 1)
    dcim = -_block_diag_take(dc[:, 1].reshape(GP, -1), S5_GROUPS).transpose(0, 2, 1)

    def f(lre, lim, ldt, gsel, bre, bim, c1, c2, c3, c4):
        d = _vjp(lambda a, b, c, e, f_: _s5_params(a, b, c, gsel, e, f_), (lre, lim, ldt, bre, bim), (c1, c2, c3, c4))
        return d

    outs = [((GP, 1), F32)] * 2 + [((1, S5_GROUPS), F32)] + [((GP, S5_GROUP), F32)] * 2
    dlre, dlim, dldt, dbre, dbim = whole(f, ins + [dlr.reshape(GP, 1), dli.reshape(GP, 1), dbbr, dbbi], outs,
                                          name=f"s5_params_bwd_{tag}")
    shp = (S5_GROUPS, S5_STATE)
    return dict(lre=dlre.reshape(shp), lim=dlim.reshape(shp), ldt=dldt.reshape(S5_GROUPS),
                bre=dbre.reshape(S5_GROUPS, S5_STATE, S5_GROUP), bim=dbim.reshape(S5_GROUPS, S5_STATE, S5_GROUP),
                cre=dcre, cim=dcim)


def _place():
    return lax.axis_index("x"), lax.axis_index("y"), lax.axis_index("c")


def _other_chips(x, y):
    return [(1 - x, y), (x, 1 - y), (1 - x, 1 - y)]


def _half(ref, h):
    n = ref.shape[0] // 2
    return ref.at[pl.ds(h * n, n)]


def _hbm_specs(n):
    return [pl.BlockSpec(memory_space=pl.ANY)] * n


def gather_chips(ws):
    n = len(ws)

    def body(*refs):
        w_refs, out_refs, (send_sems, recv_sems) = refs[:n], refs[n:2 * n], refs[2 * n:]
        x, y, c = _place()
        me, sibling = (x, y, c), (x, y, 1 - c)
        chips = _other_chips(x, y)
        mine = 2 * x + y

        def copy(k, src, dst, to):
            return pltpu.make_async_remote_copy(src_ref=src, dst_ref=dst, send_sem=send_sems.at[k],
                                                recv_sem=recv_sems.at[k], device_id=to, device_id_type=MESH)

        first, passed = [], []
        for p in range(n):
            for j, chip in enumerate(chips):
                first.append(copy(6 * p + j, _half(w_refs[p], c), _half(out_refs[p].at[mine], c), (*chip, c)))
                first[-1].start()
        for p in range(n):
            for j, chip in enumerate(chips):
                block = out_refs[p].at[2 * chip[0] + chip[1]]
                copy(6 * p + j, _half(w_refs[p], c), _half(block, c), me).wait_recv()
                passed.append(copy(6 * p + 3 + j, _half(block, c), _half(block, c), sibling))
                passed[-1].start()
        for p in range(n):
            for j, chip in enumerate(chips):
                block = out_refs[p].at[2 * chip[0] + chip[1]]
                copy(6 * p + 3 + j, _half(w_refs[p], c), _half(block, 1 - c), me).wait_recv()
        for cp in first + passed:
            cp.wait_send()

    return pl.pallas_call(
        body, out_shape=[jax.ShapeDtypeStruct((N_CHIPS,) + w.shape, w.dtype) for w in ws],
        in_specs=_hbm_specs(n), out_specs=_hbm_specs(n),
        scratch_shapes=[pltpu.SemaphoreType.DMA((6 * n,)), pltpu.SemaphoreType.DMA((6 * n,))],
        name="gather_chips",
    )(*ws)


def sibling_halves(gs):
    n = len(gs)

    def body(*refs):
        g_refs, out_refs, (send_sems, recv_sems) = refs[:n], refs[n:2 * n], refs[2 * n:]
        x, y, c = _place()
        me, sibling = (x, y, c), (x, y, 1 - c)

        def copy(p, k, to):
            return pltpu.make_async_remote_copy(src_ref=_half(g_refs[p].at[k], 1 - c), dst_ref=out_refs[p].at[k],
                                                send_sem=send_sems.at[N_CHIPS * p + k],
                                                recv_sem=recv_sems.at[N_CHIPS * p + k],
                                                device_id=to, device_id_type=MESH)

        cps = [copy(p, k, sibling) for p in range(n) for k in range(N_CHIPS)]
        for cp in cps:
            cp.start()
        for p in range(n):
            for k in range(N_CHIPS):
                copy(p, k, me).wait_recv()
        for cp in cps:
            cp.wait_send()

    return pl.pallas_call(
        body, out_shape=[jax.ShapeDtypeStruct((N_CHIPS, g.shape[1] // 2) + g.shape[2:], g.dtype) for g in gs],
        in_specs=_hbm_specs(n), out_specs=_hbm_specs(n),
        scratch_shapes=[pltpu.SemaphoreType.DMA((N_CHIPS * n,)), pltpu.SemaphoreType.DMA((N_CHIPS * n,))],
        name="sibling_halves",
    )(*gs)


def exchange_chips(ps):
    n = len(ps)

    def body(*refs):
        p_refs, out_refs, (send_sems, recv_sems) = refs[:n], refs[n:2 * n], refs[2 * n:]
        x, y, c = _place()
        me = (x, y, c)
        chips = _other_chips(x, y)

        def copy(p, j, chip, to):
            return pltpu.make_async_remote_copy(src_ref=p_refs[p].at[2 * chip[0] + chip[1]], dst_ref=out_refs[p].at[j],
                                                send_sem=send_sems.at[3 * p + j], recv_sem=recv_sems.at[3 * p + j],
                                                device_id=to, device_id_type=MESH)

        cps = [copy(p, j, chip, (*chip, c)) for p in range(n) for j, chip in enumerate(chips)]
        for cp in cps:
            cp.start()
        for p in range(n):
            for j, chip in enumerate(chips):
                copy(p, j, chip, me).wait_recv()
        for cp in cps:
            cp.wait_send()

    return pl.pallas_call(
        body, out_shape=[jax.ShapeDtypeStruct((3,) + p_.shape[1:], p_.dtype) for p_ in ps],
        in_specs=_hbm_specs(n), out_specs=_hbm_specs(n),
        scratch_shapes=[pltpu.SemaphoreType.DMA((3 * n,)), pltpu.SemaphoreType.DMA((3 * n,))],
        name="exchange_chips",
    )(*ps)


def sibling_join(rs):
    n = len(rs)

    def body(*refs):
        r_refs, out_refs, (send_sems, recv_sems) = refs[:n], refs[n:2 * n], refs[2 * n:]
        x, y, c = _place()

        def copy(p, h, to):
            return pltpu.make_async_remote_copy(src_ref=r_refs[p], dst_ref=_half(out_refs[p], h),
                                                send_sem=send_sems.at[p], recv_sem=recv_sems.at[p],
                                                device_id=to, device_id_type=MESH)

        cps = [copy(p, c, (x, y, 1 - c)) for p in range(n)]
        for cp in cps:
            cp.start()
        for p in range(n):
            copy(p, 1 - c, (x, y, c)).wait_recv()
        for cp in cps:
            cp.wait_send()

    return pl.pallas_call(
        body, out_shape=[jax.ShapeDtypeStruct((2 * r.shape[0],) + r.shape[1:], r.dtype) for r in rs],
        in_specs=_hbm_specs(n), out_specs=_hbm_specs(n),
        scratch_shapes=[pltpu.SemaphoreType.DMA((n,)), pltpu.SemaphoreType.DMA((n,))],
        name="sibling_join",
    )(*rs)


def gather_devices(v, name):
    R = v.shape[0]

    def body(v_ref, out_ref, send_sems, recv_sems, local_sem):
        x, y, c = _place()
        me, sibling = (x, y, c), (x, y, 1 - c)
        chips = _other_chips(x, y)

        def rows(px, py, pc):
            return out_ref.at[pl.ds((4 * px + 2 * py + pc) * R, R), :]

        def copy(k, block, to, src=None):
            return pltpu.make_async_remote_copy(src_ref=rows(*block) if src is None else src, dst_ref=rows(*block),
                                                send_sem=send_sems.at[k], recv_sem=recv_sems.at[k],
                                                device_id=to, device_id_type=MESH)

        mine = pltpu.make_async_copy(v_ref, rows(*me), local_sem)
        mine.start()
        first = [copy(0, me, sibling, src=v_ref)]
        first += [copy(1 + j, me, (*chip, c), src=v_ref) for j, chip in enumerate(chips)]
        for cp in first:
            cp.start()
        passed = [copy(4 + j, (*chip, c), sibling) for j, chip in enumerate(chips)]
        for j, chip in enumerate(chips):
            copy(1 + j, (*chip, c), me).wait_recv()
            passed[j].start()
        copy(0, sibling, me).wait_recv()
        for j, chip in enumerate(chips):
            copy(4 + j, (*chip, 1 - c), me).wait_recv()
        for cp in first + passed:
            cp.wait_send()
        mine.wait()

    return pl.pallas_call(
        body, out_shape=jax.ShapeDtypeStruct((N_DEV * R, LANES), v.dtype),
        in_specs=[pl.BlockSpec(memory_space=pltpu.VMEM)], out_specs=pl.BlockSpec(memory_space=pltpu.VMEM),
        scratch_shapes=[pltpu.SemaphoreType.DMA((7,)), pltpu.SemaphoreType.DMA((7,)), pltpu.SemaphoreType.DMA],
        name=name, compiler_params=_params(),
    )(v)


def _flat_rows(n, mult):
    return -(-n // (LANES * mult)) * mult


def _adam(w, g, m, v):
    m = ADAM_B1 * m + (1.0 - ADAM_B1) * g
    v = ADAM_B2 * v + (1.0 - ADAM_B2) * (g * g)
    m_hat = m / (1.0 - ADAM_B1 ** ADAM_STEP)
    v_hat = v / (1.0 - ADAM_B2 ** ADAM_STEP)
    return -ADAM_LR * (m_hat / (jnp.sqrt(v_hat) + ADAM_EPS) + ADAM_WD * w), m, v


def adam_2d(w, g, m, v, name):
    shape = w.shape
    F = shape[-1]
    if w.ndim == 3 and shape[1] % (2 * SUBLANES) == 0:
        L, R, _ = shape
        tr = R
        for t in (512, 256, 128, 64, 32, 16):
            if R % t == 0 and 7 * t * max(F, LANES) * 4 <= ROW_TILE_BYTES:
                tr = t
                break

        def body(w_ref, g_ref, m_ref, v_ref, d_ref, m2_ref, v2_ref):
            d_ref[...], m2_ref[...], v2_ref[...] = _adam(w_ref[...], g_ref[...], m_ref[...], v_ref[...])

        spec = pl.BlockSpec((None, tr, F), lambda l, i: (l, i, 0))
        return pl.pallas_call(
            body, grid=(L, R // tr), in_specs=[spec] * 4, out_specs=[spec] * 3,
            out_shape=[jax.ShapeDtypeStruct(shape, F32)] * 3, name=name, compiler_params=_params(("parallel", "parallel")),
        )(w, g, m, v)
    a = [t.reshape(-1, F) for t in (w, g, m, v)]
    d, m2, v2 = rowwise(lambda w_, g_, m_, v_: (list(_adam(w_, g_, m_, v_)), []), a, outs=[(F, F32)] * 3, name=name)
    return d.reshape(shape), m2.reshape(shape), v2.reshape(shape)


WEIGHTS = ['ffn1_norm', 'ffn1_wg', 'ffn1_wu', 'ffn1_wd', 'mix_norm', 'ffn2_norm', 'ffn2_wg', 'ffn2_wu', 'ffn2_wd',
           'ple_w', 'ple_norm', 'ple_gate_norm', 'ple_gate_w', 'ev_w_in', 'lru_conv_w', 'lru_conv_b', 'lru_wa',
           'lru_ba', 'lru_wx', 'lru_bx', 'lru_lambda', 'fox_bf', 'fox_q_norm', 'fox_k_norm', 'ev_w_out', 'od_w_in',
           'swa_q_norm', 'swa_k_norm', 'swa_sinks', 's5_lambda_re', 's5_lambda_im', 's5_log_dt', 's5_b_re',
           's5_b_im', 's5_c_re', 's5_c_im', 's5_d', 's5_glu_w', 's5_glu_b', 'od_w_out']
SHARD_AXIS = {'ffn1_wg': 2, 'ffn1_wu': 2, 'ffn1_wd': 1, 'ffn2_wg': 2, 'ffn2_wu': 2, 'ffn2_wd': 1, 'ple_w': 2,
              'ple_gate_w': 1, 'ev_w_in': 2, 'lru_conv_w': 2, 'ev_w_out': 1, 'od_w_in': 2, 's5_d': 1,
              's5_glu_w': 1, 's5_glu_b': 1, 'od_w_out': 1}
EXACT_SHARDED = ('lru_conv_w', 's5_d', 's5_glu_b')
ADAM_TRANSPOSED = ('ffn1_wg', 'ffn1_wu', 'ffn2_wg', 'ffn2_wu', 'od_w_in')
SHARDED = [n for n in WEIGHTS if n in SHARD_AXIS]
REPLICATED = [n for n in WEIGHTS if n not in SHARD_AXIS]


GROUPS = {
    'wgu': ['ffn1_wg', 'ffn1_wu', 'ffn2_wg', 'ffn2_wu'],
    'wd': ['ffn1_wd', 'ffn2_wd'],
    'w_rows': ['ple_gate_w', 'ev_w_out', 'od_w_out'],
    'ple_w': ['ple_w'], 'ev_w_in': ['ev_w_in'], 'od_w_in': ['od_w_in'], 's5_glu_w': ['s5_glu_w'],
}
REDUCED_GROUPS = list(GROUPS)


def _chip():
    return 2 * lax.axis_index("x") + lax.axis_index("y")


def gather_weights(shards):
    own = {k: jnp.concatenate([shards[n] for n in names], axis=0).astype(BF16) for k, names in GROUPS.items()}
    own['exact'] = jnp.concatenate([shards['lru_conv_w'], shards['s5_d'][:, None], shards['s5_glu_b'][:, None]], axis=1)
    keys = list(own)
    got = gather_chips([own[k] for k in keys])
    return {k: lax.dynamic_update_index_in_dim(g, own[k], _chip(), 0) for k, g in zip(keys, got)}


def _rows_by_chip(w):
    return w.reshape(w.shape[0] * w.shape[1], w.shape[2])


def _cols_by_chip(w):
    return w.transpose(1, 0, 2).reshape(w.shape[1], w.shape[0] * w.shape[2])


def _chip_rows(g):
    return g.reshape(N_CHIPS, g.shape[0] // N_CHIPS, g.shape[1])


def _chip_cols(g):
    return g.reshape(g.shape[0], N_CHIPS, g.shape[1] // N_CHIPS).transpose(1, 0, 2)


def full_weights(gw, depth):
    n_ev = (depth + 1) // 2
    ex = gw['exact']
    return dict(
        ple_gate_w=[_rows_by_chip(gw['w_rows'][:, l]) for l in range(depth)],
        ev_w_out=[_rows_by_chip(gw['w_rows'][:, depth + j]) for j in range(n_ev)],
        od_w_out=[_rows_by_chip(gw['w_rows'][:, depth + n_ev + j]) for j in range(depth // 2)],
        ple_w=[_cols_by_chip(gw['ple_w'][:, l]) for l in range(depth)],
        ev_w_in=[_cols_by_chip(gw['ev_w_in'][:, j]) for j in range(n_ev)],
        od_w_in=[_cols_by_chip(gw['od_w_in'][:, j]) for j in range(depth // 2)],
        s5_glu_w=[_rows_by_chip(gw['s5_glu_w'][:, j]) for j in range(depth // 2)],
        lru_conv_w=[_cols_by_chip(ex[:, j, 0:LRU_CONV]) for j in range(n_ev)],
        s5_d=[ex[:, j, LRU_CONV].reshape(-1) for j in range(depth // 2)],
        s5_glu_b=[ex[:, j, LRU_CONV + 1].reshape(-1) for j in range(depth // 2)],
    )


def _add_tile(rows, width):
    for t in (1024, 512, 256, 128, 64, 32, 16):
        if rows % t == 0 and 3 * t * width * 4 <= ROW_TILE_BYTES:
            return t
    return rows


def pair_add(g, t, c, name):
    C, F = g.shape[0], g.shape[-1]
    rows = math.prod(t.shape[1:-1])
    tr = _add_tile(rows, F)
    nb = rows // tr

    def body(c_ref, g_ref, t_ref, o_ref):
        o_ref[...] = (g_ref[...].astype(F32) + t_ref[...].astype(F32)).astype(o_ref.dtype)

    spec = pl.BlockSpec((None, tr, F), lambda k, i, c_ref: (k, i, 0))
    out = pl.pallas_call(
        body, out_shape=jax.ShapeDtypeStruct((C, rows, F), BF16),
        grid_spec=pltpu.PrefetchScalarGridSpec(
            num_scalar_prefetch=1, grid=(C, nb),
            in_specs=[pl.BlockSpec((None, tr, F), lambda k, i, c_ref: (k, c_ref[0] * nb + i, 0)), spec],
            out_specs=spec),
        name=name, compiler_params=_params(("parallel", "parallel")),
    )(c.reshape(1).astype(jnp.int32), g.reshape(C, 2 * rows, F), t.reshape(C, rows, F))
    return out.reshape(t.shape)


def chips_add(p, xs, chip, name):
    F = p.shape[-1]
    rows = math.prod(p.shape[1:-1])
    tr = _add_tile(rows, F)

    def body(m_ref, p_ref, a_ref, b_ref, d_ref, o_ref):
        o_ref[...] = ((p_ref[...].astype(F32) + a_ref[...].astype(F32))
                      + (b_ref[...].astype(F32) + d_ref[...].astype(F32)))

    def other(j):
        return pl.BlockSpec((None, tr, F), lambda i, m_ref: (j, i, 0))

    x3 = xs.reshape(3, rows, F)
    out = pl.pallas_call(
        body, out_shape=jax.ShapeDtypeStruct((rows, F), F32),
        grid_spec=pltpu.PrefetchScalarGridSpec(
            num_scalar_prefetch=1, grid=(rows // tr,),
            in_specs=[pl.BlockSpec((None, tr, F), lambda i, m_ref: (m_ref[0], i, 0)), other(0), other(1), other(2)],
            out_specs=pl.BlockSpec((tr, F), lambda i, m_ref: (i, 0))),
        name=name, compiler_params=_params(("parallel",)),
    )(chip.reshape(1).astype(jnp.int32), p.reshape(N_CHIPS, rows, F), x3, x3, x3)
    return out.reshape(p.shape[1:])


def reduce_sharded(groups):
    keys = list(groups)
    c = lax.axis_index("c")
    gs = [groups[k] for k in keys]
    theirs = sibling_halves(gs)
    pairs = [pair_add(g, t, c, f"pair_add_{k}") for k, g, t in zip(keys, gs, theirs)]
    got = exchange_chips(pairs)
    halves = [chips_add(p_, x_, _chip(), f"chips_add_{k}") for k, p_, x_ in zip(keys, pairs, got)]
    joined = sibling_join(halves)
    out = {}
    for k, h, j in zip(keys, halves, joined):
        out[k] = lax.dynamic_update_slice_in_dim(j, h, c * h.shape[0], axis=0)
    return out


SMALL_GRADS = REPLICATED + list(EXACT_SHARDED)


def _flatten_small(tensors, shapes):
    parts = [tensors[n].astype(F32).reshape(-1) if n in tensors else jnp.zeros((math.prod(shapes[n]),), F32)
             for n in SMALL_GRADS]
    flat = jnp.concatenate(parts)
    rows = _flat_rows(flat.shape[0], SUBLANES)
    return jnp.pad(flat, (0, rows * LANES - flat.shape[0])).reshape(rows, LANES)


def _unflatten_small(flat, shapes):
    flat = flat.reshape(-1)
    out, off = {}, 0
    for n in SMALL_GRADS:
        size = math.prod(shapes[n])
        out[n] = flat[off:off + size].reshape(shapes[n])
        off += size
    return out


def grad_groups(gwgu, gwd, G):
    def st(xs):
        return jnp.stack(xs, axis=1).astype(BF16)

    return {
        'wgu': gwgu, 'wd': gwd,
        'w_rows': st([_chip_rows(g) for n in GROUPS['w_rows'] for g in G[n]]),
        'ple_w': st([_chip_cols(g) for g in G['ple_w']]),
        'ev_w_in': st([_chip_cols(g) for g in G['ev_w_in']]),
        'od_w_in': st([_chip_cols(g) for g in G['od_w_in']]),
        's5_glu_w': st([_chip_rows(g) for g in G['s5_glu_w']]),
    }


def ungroup(red, shapes):
    out = {}
    for k, names in GROUPS.items():
        off = 0
        for n in names:
            out[n] = red[k][off:off + shapes[n][0]]
            off += shapes[n][0]
    return out


def _layer_weights(full, small, i, depth):
    j = i // 2
    w = dict(
        g1=small['ffn1_norm'][i][None], gm=small['mix_norm'][i][None], g2=small['ffn2_norm'][i][None],
        gp=small['ple_norm'][i][None], gg=small['ple_gate_norm'][i][None],
        ffn1=(i, depth + i, i), ffn2=(2 * depth + i, 3 * depth + i, depth + i),
        ple_w=full['ple_w'][i], ple_gate_w=full['ple_gate_w'][i],
    )
    if i % 2 == 0:
        w_in = full['ev_w_in'][j]
        w['mix'] = dict(
            w_in=jnp.pad(w_in, ((0, 0), (0, 2688 - w_in.shape[1]))), w_out=full['ev_w_out'][j],
            conv_w=full['lru_conv_w'][j].astype(F32), conv_b=small['lru_conv_b'][j][None],
            w_ax=jnp.concatenate([_block_diag(small['lru_wa'][j]), _block_diag(small['lru_wx'][j])],
                                 axis=1).astype(BF16),
            ba=small['lru_ba'][j][None], bx=small['lru_bx'][j][None], lam=small['lru_lambda'][j][None],
            bf=jnp.pad(small['fox_bf'][j], (0, LANES - 8))[None], qn=small['fox_q_norm'][j],
            kn=small['fox_k_norm'][j])
    else:
        lam, bexp, cexp, ins = s5_prep_fwd(small['s5_lambda_re'][j], small['s5_lambda_im'][j], small['s5_log_dt'][j],
                                           small['s5_b_re'][j], small['s5_b_im'][j], small['s5_c_re'][j],
                                           small['s5_c_im'][j], f"L{i}")
        w['mix'] = dict(
            w_in=full['od_w_in'][j], w_out=full['od_w_out'][j], qn=small['swa_q_norm'][j], kn=small['swa_k_norm'][j],
            sinks=small['swa_sinks'][j], s5_lam=lam, s5_bexp=bexp, s5_cexp=cexp, s5_ins=ins,
            s5_d=full['s5_d'][j].astype(F32)[None], glu_w=full['s5_glu_w'][j], glu_b=full['s5_glu_b'][j].astype(F32)[None])
    return w


def layer_fwd(x0, n1, p_i, w, ffnw, next_g1, i):
    tag = f"L{i}"
    sv = {}
    wgu, wd = ffnw
    x1, hm, sv['ffn1'] = ffn_fwd(n1, x0, w['gm'], wgu, wd, *w['ffn1'], f"1_{tag}")
    if i % 2 == 0:
        mo, sv['mix'] = even_mixer_fwd(hm, w['mix'], tag)
    else:
        mo, sv['mix'] = odd_mixer_fwd(hm, w['mix'], tag)
    x2, n2 = mm_add_norm(mo, w['mix']['w_out'], x1, w['g2'], f"mix_out_{tag}")
    x3, ng, sv['ffn2'] = ffn_fwd(n2, x2, w['gg'], wgu, wd, *w['ffn2'], f"2_{tag}")
    gpre = mm(ng, w['ple_gate_w'], name=f"ple_gate_{tag}")
    epre = mm(p_i, w['ple_w'], name=f"ple_emb_{tag}")
    D = x0.shape[1]
    if next_g1 is None:
        x4 = rowwise(lambda a, b, c, pn: ([_ple_out(a, b, c, pn)], []), [x3, gpre, epre], [w['gp']],
                     outs=[(D, F32)], name=f"ple_out_{tag}")[0]
        n_next = None
    else:
        def f(a, b, c, pn, gn):
            y = _ple_out(a, b, c, pn)
            return [y, _rms(y, gn)], []

        x4, n_next = rowwise(f, [x3, gpre, epre], [w['gp'], next_g1], outs=[(D, F32), (D, BF16)],
                             name=f"ple_out_{tag}")
    sv.update(x0=x0, x1=x1, x2=x2, x3=x3, ng=ng, gpre=gpre, epre=epre, p=p_i)
    return x4, n_next, sv


def layer_bwd(dx4, sv, w, ffnw, gbuf, i):
    tag = f"L{i}"
    D = dx4.shape[1]
    g = {}
    wgu, wd = ffnw

    def f_ple(a, b, c, d, pn):
        da, db, dc, dpn = _vjp(_ple_out, (a, b, c, pn), d)
        return [db, dc], [dpn]

    dgpre, depre, dgp = rowwise(f_ple, [sv['x3'], sv['gpre'], sv['epre'], dx4], [w['gp']],
                                outs=[(D, BF16), (D, BF16)], accs=[(1, D)], name=f"ple_out_bwd_{tag}")
    g['gp'] = dgp[0]
    g['ple_w'] = mm(sv['p'], depre, "tn", name=f"ple_emb_dw_{tag}")
    g['ple_gate_w'] = mm(sv['ng'], dgpre, "tn", name=f"ple_gate_dw_{tag}")
    dx3, dgg = mm_norm_bwd(dgpre, w['ple_gate_w'], sv['x3'], w['gg'], dx4, f"ple_gate_dx_{tag}")
    g['gg'] = dgg[0]
    dx2, dg2, gbuf = ffn_bwd(dx3, sv['ffn2'], sv['x2'], w['g2'], wgu, wd, *w['ffn2'], gbuf, f"2_{tag}")
    g['g2'] = dg2[0]
    if i % 2 == 0:
        dz, g['mix'] = even_mixer_bwd(dx2, sv['mix'], w['mix'], tag)
    else:
        dz, g['mix'] = odd_mixer_bwd(dx2, sv['mix'], w['mix'], tag)
    dx1, dgm = mm_norm_bwd(dz, w['mix']['w_in'], sv['x1'], w['gm'], dx2, f"mix_dh_{tag}")
    g['gm'] = dgm[0]
    dx0, dg1, gbuf = ffn_bwd(dx1, sv['ffn1'], sv['x0'], w['g1'], wgu, wd, *w['ffn1'], gbuf, f"1_{tag}")
    g['g1'] = dg1[0]
    return dx0, g, gbuf


def _collect_grads(layer_grads, depth):
    st = lambda xs: jnp.stack(xs)
    G = {}
    L = layer_grads
    G['ffn1_norm'] = st([g['g1'] for g in L])
    G['mix_norm'] = st([g['gm'] for g in L])
    G['ffn2_norm'] = st([g['g2'] for g in L])
    G['ple_norm'] = st([g['gp'] for g in L])
    G['ple_gate_norm'] = st([g['gg'] for g in L])
    G['ple_w'] = st([g['ple_w'] for g in L])
    G['ple_gate_w'] = st([g['ple_gate_w'] for g in L])
    ev = [L[i]['mix'] for i in range(0, depth, 2)]
    od = [L[i]['mix'] for i in range(1, depth, 2)]
    G['ev_w_in'] = st([m['w_in'][:, :2568] for m in ev])
    G['ev_w_out'] = st([m['w_out'] for m in ev])
    G['lru_conv_w'] = st([m['conv_w'] for m in ev])
    G['lru_conv_b'] = st([m['conv_b'] for m in ev])
    G['lru_wa'] = st([_block_diag_take(m['w_ax'][:, :512], LRU_BLOCKS) for m in ev])
    G['lru_wx'] = st([_block_diag_take(m['w_ax'][:, 512:], LRU_BLOCKS) for m in ev])
    G['lru_ba'] = st([m['ba'] for m in ev])
    G['lru_bx'] = st([m['bx'] for m in ev])
    G['lru_lambda'] = st([m['lam'] for m in ev])
    G['fox_bf'] = st([m['bf'] for m in ev])
    G['fox_q_norm'] = st([m['qn'] for m in ev])
    G['fox_k_norm'] = st([m['kn'] for m in ev])
    G['od_w_in'] = st([m['w_in'] for m in od])
    G['od_w_out'] = st([m['w_out'] for m in od])
    G['swa_q_norm'] = st([m['qn'] for m in od])
    G['swa_k_norm'] = st([m['kn'] for m in od])
    G['swa_sinks'] = st([m['sinks'] for m in od])
    G['s5_lambda_re'] = st([m['s5']['lre'] for m in od])
    G['s5_lambda_im'] = st([m['s5']['lim'] for m in od])
    G['s5_log_dt'] = st([m['s5']['ldt'] for m in od])
    G['s5_b_re'] = st([m['s5']['bre'] for m in od])
    G['s5_b_im'] = st([m['s5']['bim'] for m in od])
    G['s5_c_re'] = st([m['s5']['cre'] for m in od])
    G['s5_c_im'] = st([m['s5']['cim'] for m in od])
    G['s5_d'] = st([m['s5_d'] for m in od])
    G['s5_glu_w'] = st([m['glu_w'] for m in od])
    G['s5_glu_b'] = st([m['glu_b'] for m in od])
    return G


def local_step(x, p, target, ffnw, full, small):
    depth = p.shape[0]
    S, D = x.shape
    ws = [_layer_weights(full, small, i, depth) for i in range(depth)]
    saved = []
    xi, ni = add_norm(x, None, ws[0]['g1'], "norm1_L0")
    for i in range(depth):
        xi, ni, sv = layer_fwd(xi, ni, p[i], ws[i], ffnw, ws[i + 1]['g1'] if i + 1 < depth else None, i)
        saved.append(sv)

    def f_loss(y, t):
        e = y - t
        return [e * (1.0 / D)], [0.5 * jnp.sum(jnp.mean(e * e, axis=-1, keepdims=True), axis=0, keepdims=True)]

    dx, loss = rowwise(f_loss, [xi, target], outs=[(D, F32)], accs=[(1, 1)], name="loss")
    grads = [None] * depth
    gbuf = (None, None)
    for i in reversed(range(depth)):
        dx, grads[i], gbuf = layer_bwd(dx, saved[i], ws[i], ffnw, gbuf, i)
        if i % 2 == 1:
            m = grads[i]['mix']
            m['s5'] = s5_prep_bwd(ws[i]['mix']['s5_ins'], m['s5_lam'], m['s5_bexp'], m['s5_cexp'], f"L{i}")
    return loss[0, 0], dx, gbuf, _collect_grads(grads, depth)


def kernel(x, p, ffn1_norm, ffn1_wg, ffn1_wu, ffn1_wd, mix_norm, ffn2_norm, ffn2_wg, ffn2_wu, ffn2_wd, ple_w, ple_norm, ple_gate_norm, ple_gate_w, ev_w_in, lru_conv_w, lru_conv_b, lru_wa, lru_ba, lru_wx, lru_bx, lru_lambda, fox_bf, fox_q_norm, fox_k_norm, ev_w_out, od_w_in, swa_q_norm, swa_k_norm, swa_sinks, s5_lambda_re, s5_lambda_im, s5_log_dt, s5_b_re, s5_b_im, s5_c_re, s5_c_im, s5_d, s5_glu_w, s5_glu_b, od_w_out, loss_target, m_ffn1_norm, m_ffn1_wg, m_ffn1_wu, m_ffn1_wd, m_mix_norm, m_ffn2_norm, m_ffn2_wg, m_ffn2_wu, m_ffn2_wd, m_ple_w, m_ple_norm, m_ple_gate_norm, m_ple_gate_w, m_ev_w_in, m_lru_conv_w, m_lru_conv_b, m_lru_wa, m_lru_ba, m_lru_wx, m_lru_bx, m_lru_lambda, m_fox_bf, m_fox_q_norm, m_fox_k_norm, m_ev_w_out, m_od_w_in, m_swa_q_norm, m_swa_k_norm, m_swa_sinks, m_s5_lambda_re, m_s5_lambda_im, m_s5_log_dt, m_s5_b_re, m_s5_b_im, m_s5_c_re, m_s5_c_im, m_s5_d, m_s5_glu_w, m_s5_glu_b, m_od_w_out, v_ffn1_norm, v_ffn1_wg, v_ffn1_wu, v_ffn1_wd, v_mix_norm, v_ffn2_norm, v_ffn2_wg, v_ffn2_wu, v_ffn2_wd, v_ple_w, v_ple_norm, v_ple_gate_norm, v_ple_gate_w, v_ev_w_in, v_lru_conv_w, v_lru_conv_b, v_lru_wa, v_lru_ba, v_lru_wx, v_lru_bx, v_lru_lambda, v_fox_bf, v_fox_q_norm, v_fox_k_norm, v_ev_w_out, v_od_w_in, v_swa_q_norm, v_swa_k_norm, v_swa_sinks, v_s5_lambda_re, v_s5_lambda_im, v_s5_log_dt, v_s5_b_re, v_s5_b_im, v_s5_c_re, v_s5_c_im, v_s5_d, v_s5_glu_w, v_s5_glu_b, v_od_w_out):
    args = locals()
    wts = {n: args[n] for n in WEIGHTS}
    ms = {n: args["m_" + n] for n in WEIGHTS}
    vs = {n: args["v_" + n] for n in WEIGHTS}
    shapes = {n: wts[n].shape for n in WEIGHTS}

    depth = p.shape[0]
    gw = gather_weights({n: wts[n] for n in SHARDED})
    small = {n: wts[n] for n in REPLICATED}
    loss, dx, (gwgu, gwd), G = local_step(x[0], p[:, 0], loss_target[0], (gw['wgu'], gw['wd']),
                                          full_weights(gw, depth), small)
    loss = lax.psum(loss, ("x", "y", "c"))

    gsh = ungroup(reduce_sharded(grad_groups(gwgu, gwd, G)), shapes)
    full_shapes = {n: (G[n].shape if n in EXACT_SHARDED else shapes[n]) for n in SMALL_GRADS}
    flat_g = _flatten_small(G, full_shapes)
    g8 = gather_devices(flat_g, "gather_small_grads").reshape((N_DEV,) + flat_g.shape)
    wf, mf, vf = (_flatten_small({n: t[n] for n in REPLICATED}, full_shapes) for t in (wts, ms, vs))

    def f_small(g0, g1, g2, g3, g4, g5, g6, g7, w_, m_, v_):
        gsum = ((g0 + g1) + (g2 + g3)) + ((g4 + g5) + (g6 + g7))
        return [gsum] + list(_adam(w_, gsum, m_, v_)), []

    gs_f, ds_f, ms_f, vs_f = rowwise(f_small, [g8[d] for d in range(N_DEV)] + [wf, mf, vf],
                                     outs=[(LANES, F32)] * 4, name="adam_small")
    out_g, out_d, out_m, out_v = {}, {}, {}, {}
    for dst, flat in ((out_g, gs_f), (out_d, ds_f), (out_m, ms_f), (out_v, vs_f)):
        dst.update(_unflatten_small(flat, full_shapes))
    for n in EXACT_SHARDED:
        width = shapes[n][SHARD_AXIS[n]]
        gsh[n] = lax.dynamic_slice_in_dim(out_g[n], _chip() * width, width, axis=SHARD_AXIS[n])
    for n in SHARDED:
        out_g[n] = gsh[n]
        if n in ADAM_TRANSPOSED:
            def t(a):
                return a.transpose(0, 2, 1)
            d_, m_, v_ = adam_2d(t(wts[n]), t(gsh[n]), t(ms[n]), t(vs[n]), f"adam_{n}")
            out_d[n], out_m[n], out_v[n] = t(d_), t(m_), t(v_)
        else:
            out_d[n], out_m[n], out_v[n] = adam_2d(wts[n], gsh[n], ms[n], vs[n], f"adam_{n}")
    return (loss, dx[None], *[out_g[n] for n in WEIGHTS], *[out_d[n] for n in WEIGHTS],
            *[out_m[n] for n in WEIGHTS], *[out_v[n] for n in WEIGHTS])
```

```python
import functools
import math

import jax
import jax.numpy as jnp
from jax import lax
from jax.experimental import pallas as pl
from jax.experimental.pallas import tpu as pltpu

F32 = jnp.float32
BF16 = jnp.bfloat16
MXU_DTYPE = BF16
HI = lax.Precision.HIGHEST
MESH = pl.DeviceIdType.MESH

VMEM_LIMIT_BYTES = 56 * 1024 * 1024
ROW_TILE_BYTES = 8 * 1024 * 1024
MM_VMEM_BYTES = 40 * 1024 * 1024
MM_TILE_M = 1024
MM_TILE_N = 1408
FLAT_W = 2048
LANES = 128
SUBLANES = 8

HEAD_DIM = 64
LRU_BLOCKS = 8
LRU_CONV = 4
LRU_C = 8.0
SWA_WINDOW = 128
SWA_GROUP = 4
S5_GROUP = 16
S5_GROUPS = 32
S5_STATE = 64
S5_BLOCKS = 4
ROPE_THETA = 10000.0
EPS = 1e-6
MACARON = 0.5
NEG = -1e30

ADAM_LR = 0.001
ADAM_B1 = 0.9
ADAM_B2 = 0.999
ADAM_EPS = 1e-08
ADAM_WD = 0.01
ADAM_STEP = 10

N_CHIPS = 4
N_DEV = 8


def _pick(n, cands):
    for c in cands:
        if n % c == 0:
            return c
    return n


def _tile(n, cap, unit):
    best = None
    for t in range(unit, min(n, cap) + 1, unit):
        if n % t == 0:
            best = t
    return n if best is None else best


def _params(sem=None):
    return pltpu.CompilerParams(dimension_semantics=sem, vmem_limit_bytes=VMEM_LIMIT_BYTES)


def rowwise(fn, rows, consts=(), outs=(), accs=(), name="rowwise", periods=None):
    rows, consts = list(rows), list(consts)
    n_r, n_c, n_o, n_a = len(rows), len(consts), len(outs), len(accs)
    R = rows[0].shape[0]
    periods = list(periods) if periods is not None else [None] * n_r
    per_row = sum(max(r.shape[1], LANES) * 4 for r in rows) + sum(max(f, LANES) * 4 for f, _ in outs)
    limit = min([R] + [p for p in periods if p is not None])
    tr = limit
    for c in (1024, 512, 256, 128, 64, 32, 16):
        if c <= limit and limit % c == 0 and R % c == 0 and c * per_row <= ROW_TILE_BYTES:
            tr = c
            break

    def row_map(period):
        if period is None:
            return lambda i: (i, 0)
        nb = period // tr
        return lambda i: (i % nb, 0)

    in_specs = [pl.BlockSpec((tr, r.shape[1]), row_map(p)) for r, p in zip(rows, periods)]
    in_specs += [pl.BlockSpec(c.shape, lambda i: (0, 0)) for c in consts]
    out_shape = [jax.ShapeDtypeStruct((R, f), dt) for f, dt in outs]
    out_shape += [jax.ShapeDtypeStruct(tuple(s), F32) for s in accs]
    out_specs = [pl.BlockSpec((tr, f), lambda i: (i, 0)) for f, _ in outs]
    out_specs += [pl.BlockSpec(tuple(s), lambda i: (0, 0)) for s in accs]

    def body(*refs):
        ins = [r[...] for r in refs[:n_r + n_c]]
        o_refs = refs[n_r + n_c:n_r + n_c + n_o]
        a_refs = refs[n_r + n_c + n_o:]
        ro, ra = fn(*ins)
        for ref, val in zip(o_refs, ro):
            ref[...] = val.astype(ref.dtype)
        if n_a:
            @pl.when(pl.program_id(0) == 0)
            def _():
                for ref in a_refs:
                    ref[...] = jnp.zeros(ref.shape, ref.dtype)
            for ref, val in zip(a_refs, ra):
                ref[...] += val.astype(F32)

    res = pl.pallas_call(
        body, grid=(R // tr,), in_specs=in_specs, out_specs=out_specs, out_shape=out_shape,
        name=name, compiler_params=_params(("arbitrary",)),
    )(*rows, *consts)
    return list(res)


def whole(fn, ins, outs, name="whole"):
    n_i = len(ins)

    def body(*refs):
        vals = fn(*[r[...] for r in refs[:n_i]])
        for ref, val in zip(refs[n_i:], vals):
            ref[...] = val.astype(ref.dtype)

    res = pl.pallas_call(
        body, out_shape=[jax.ShapeDtypeStruct(tuple(s), dt) for s, dt in outs],
        in_specs=[pl.BlockSpec(memory_space=pltpu.VMEM)] * n_i,
        out_specs=[pl.BlockSpec(memory_space=pltpu.VMEM)] * len(outs),
        name=name, compiler_params=_params(),
    )(*ins)
    return list(res)


_DOT_DIMS = {
    "nn": (((1,), (0,)), ((), ())),
    "nt": (((1,), (1,)), ((), ())),
    "tn": (((0,), (0,)), ((), ())),
}


def mm(a, b, mode="nn", out_dtype=F32, name="mm"):
    if mode == "nn":
        (M, K), (K2, N) = a.shape, b.shape
    elif mode == "nt":
        (M, K), (N, K2) = a.shape, b.shape
    else:
        (K, M), (K2, N) = a.shape, b.shape
    assert K == K2, (mode, a.shape, b.shape)
    tn = _tile(N, MM_TILE_N, LANES)
    if mode == "tn":
        tm, tk = _tile(M, MM_TILE_M, LANES), _tile(K, MM_TILE_M, 2 * SUBLANES)
    else:
        tm, tk = _tile(M, MM_TILE_M, 2 * SUBLANES), _tile(K, MM_TILE_N, LANES)

    def vmem_bytes(tm_, tk_):
        return (2 * (tm_ * tk_ * a.dtype.itemsize + tk_ * tn * b.dtype.itemsize
                     + tm_ * tn * jnp.dtype(out_dtype).itemsize) + tm_ * tn * 4)

    while vmem_bytes(tm, tk) > MM_VMEM_BYTES and tk % (2 * LANES) == 0 and K % (tk // 2) == 0:
        tk //= 2
    while vmem_bytes(tm, tk) > MM_VMEM_BYTES and tm % (2 * LANES) == 0 and M % (tm // 2) == 0:
        tm //= 2
    if mode == "tn":
        a_spec = pl.BlockSpec((tk, tm), lambda i, j, k: (k, i))
    else:
        a_spec = pl.BlockSpec((tm, tk), lambda i, j, k: (i, k))
    if mode == "nt":
        b_spec = pl.BlockSpec((tn, tk), lambda i, j, k: (j, k))
    else:
        b_spec = pl.BlockSpec((tk, tn), lambda i, j, k: (k, j))
    nk = K // tk
    dims = _DOT_DIMS[mode]

    def dot(a_ref, b_ref):
        return lax.dot_general(a_ref[...].astype(MXU_DTYPE), b_ref[...].astype(MXU_DTYPE), dims,
                               preferred_element_type=F32)

    def body_one(a_ref, b_ref, o_ref):
        o_ref[...] = dot(a_ref, b_ref).astype(o_ref.dtype)

    def body_acc(a_ref, b_ref, o_ref, acc_ref):
        k = pl.program_id(2)

        @pl.when(k == 0)
        def _():
            acc_ref[...] = dot(a_ref, b_ref)

        @pl.when(k > 0)
        def _():
            acc_ref[...] += dot(a_ref, b_ref)

        @pl.when(k == nk - 1)
        def _():
            o_ref[...] = acc_ref[...].astype(o_ref.dtype)

    return pl.pallas_call(
        body_one if nk == 1 else body_acc, grid=(M // tm, N // tn, nk), in_specs=[a_spec, b_spec],
        out_specs=pl.BlockSpec((tm, tn), lambda i, j, k: (i, j)),
        out_shape=jax.ShapeDtypeStruct((M, N), out_dtype),
        scratch_shapes=[] if nk == 1 else [pltpu.VMEM((tm, tn), F32)],
        name=name, compiler_params=_params(("parallel", "parallel", "arbitrary")),
    )(a, b)


def mm_blocks(a, b, mode="nn", name="mm_blocks"):
    M = a.shape[0]
    nb = b.shape[0]
    Ka, Nb = (b.shape[1], b.shape[2]) if mode == "nn" else (b.shape[2], b.shape[1])
    tm = _tile(M, MM_TILE_M, 2 * SUBLANES)

    def body(a_ref, b_ref, o_ref):
        o_ref[...] = _dotf(a_ref[...], b_ref[...], mode)

    return pl.pallas_call(
        body, grid=(M // tm, nb),
        in_specs=[pl.BlockSpec((tm, Ka), lambda i, j: (i, j)),
                  pl.BlockSpec((None,) + b.shape[1:], lambda i, j: (j, 0, 0))],
        out_specs=pl.BlockSpec((tm, Nb), lambda i, j: (i, j)), out_shape=jax.ShapeDtypeStruct((M, nb * Nb), F32),
        name=name, compiler_params=_params(("parallel", "parallel")),
    )(a, b)


def mm_blocks_tn(a, d, nb, name="mm_blocks_tn"):
    K = a.shape[0]
    Ma, Nd = a.shape[1] // nb, d.shape[1] // nb
    tk = _tile(K, MM_TILE_M, 2 * SUBLANES)

    def body(a_ref, d_ref, o_ref):
        k = pl.program_id(1)
        r = _dotf(a_ref[...], d_ref[...], "tn")

        @pl.when(k == 0)
        def _():
            o_ref[...] = r

        @pl.when(k > 0)
        def _():
            o_ref[...] += r

    return pl.pallas_call(
        body, grid=(nb, K // tk),
        in_specs=[pl.BlockSpec((tk, Ma), lambda j, k: (k, j)), pl.BlockSpec((tk, Nd), lambda j, k: (k, j))],
        out_specs=pl.BlockSpec((None, Ma, Nd), lambda j, k: (j, 0, 0)),
        out_shape=jax.ShapeDtypeStruct((nb, Ma, Nd), F32), name=name,
        compiler_params=_params(("parallel", "arbitrary")),
    )(a, d)


def _mm_rows_tiles(M, K):
    return _tile(M, MM_TILE_M // 2, 2 * SUBLANES), _tile(K, MM_TILE_N, LANES)


def mm_add_norm(a, b, x, gain, name):
    (M, K), N = a.shape, b.shape[1]
    tm, tk = _mm_rows_tiles(M, K)
    nk = K // tk

    def body(a_ref, b_ref, x_ref, g_ref, xo_ref, n_ref, acc_ref):
        k = pl.program_id(1)
        r = _dotf(a_ref[...], b_ref[...])

        @pl.when(k == 0)
        def _():
            acc_ref[...] = r

        @pl.when(k > 0)
        def _():
            acc_ref[...] += r

        @pl.when(k == nk - 1)
        def _():
            xn = x_ref[...] + acc_ref[...]
            xo_ref[...] = xn
            n_ref[...] = _rms(xn, g_ref[...]).astype(n_ref.dtype)

    row = pl.BlockSpec((tm, N), lambda i, k: (i, 0))
    return pl.pallas_call(
        body, grid=(M // tm, nk),
        in_specs=[pl.BlockSpec((tm, tk), lambda i, k: (i, k)), pl.BlockSpec((tk, N), lambda i, k: (k, 0)), row,
                  pl.BlockSpec((1, N), lambda i, k: (0, 0))],
        out_specs=[row, row], out_shape=[jax.ShapeDtypeStruct((M, N), F32), jax.ShapeDtypeStruct((M, N), BF16)],
        scratch_shapes=[pltpu.VMEM((tm, N), F32)], name=name, compiler_params=_params(("parallel", "arbitrary")),
    )(a, b, x, gain)


def mm_norm_bwd(a, b, x, gain, dx_res, name):
    (M, K), N = a.shape, b.shape[0]
    tm, tk = _mm_rows_tiles(M, K)
    nk = K // tk

    def body(a_ref, b_ref, x_ref, g_ref, r_ref, dx_ref, dgain_ref, acc_ref):
        i, k = pl.program_id(0), pl.program_id(1)
        r = _dotf(a_ref[...], b_ref[...], "nt")

        @pl.when((i == 0) & (k == 0))
        def _():
            dgain_ref[...] = jnp.zeros(dgain_ref.shape, F32)

        @pl.when(k == 0)
        def _():
            acc_ref[...] = r

        @pl.when(k > 0)
        def _():
            acc_ref[...] += r

        @pl.when(k == nk - 1)
        def _():
            dx, dgain = _vjp(_rms, (x_ref[...], g_ref[...]), acc_ref[...])
            dx_ref[...] = r_ref[...] + dx
            dgain_ref[...] += dgain

    row = pl.BlockSpec((tm, N), lambda i, k: (i, 0))
    vec = pl.BlockSpec((1, N), lambda i, k: (0, 0))
    return pl.pallas_call(
        body, grid=(M // tm, nk),
        in_specs=[pl.BlockSpec((tm, tk), lambda i, k: (i, k)), pl.BlockSpec((N, tk), lambda i, k: (0, k)), row, vec, row],
        out_specs=[row, vec], out_shape=[jax.ShapeDtypeStruct((M, N), F32), jax.ShapeDtypeStruct((1, N), F32)],
        scratch_shapes=[pltpu.VMEM((tm, N), F32)], name=name, compiler_params=_params(("arbitrary", "arbitrary")),
    )(a, b, x, gain, dx_res)


def _roll_rows(x, d, reverse):
    return pltpu.roll(x, (SUBLANES - d) if reverse else d, 0)


def scan_real(a, b, reverse=False, name="scan_real"):
    S, W = b.shape
    cw = _pick(W, (256, 128))
    n_tiles = S // SUBLANES

    def body(a_ref, b_ref, o_ref):
        row = lax.broadcasted_iota(jnp.int32, (SUBLANES, cw), 0)
        edge = 0 if reverse else SUBLANES - 1

        def step(i, carry):
            t = (n_tiles - 1 - i) if reverse else i
            off = pl.multiple_of(t * SUBLANES, SUBLANES)
            A = a_ref[pl.ds(off, SUBLANES), :]
            B = b_ref[pl.ds(off, SUBLANES), :]
            for d in (1, 2, 4):
                m = (row < SUBLANES - d) if reverse else (row >= d)
                B = jnp.where(m, A * _roll_rows(B, d, reverse) + B, B)
                A = jnp.where(m, A * _roll_rows(A, d, reverse), A)
            o_ref[pl.ds(off, SUBLANES), :] = B + A * carry
            at_edge = row == edge
            return (jnp.sum(jnp.where(at_edge, B, 0.0), axis=0, keepdims=True)
                    + jnp.sum(jnp.where(at_edge, A, 0.0), axis=0, keepdims=True) * carry)

        lax.fori_loop(0, n_tiles, step, jnp.zeros((1, cw), F32), unroll=2)

    spec = pl.BlockSpec((S, cw), lambda j: (0, j))
    return pl.pallas_call(
        body, grid=(W // cw,), in_specs=[spec, spec], out_specs=spec,
        out_shape=jax.ShapeDtypeStruct((S, W), F32), name=name, compiler_params=_params(("parallel",)),
    )(a, b)


def scan_cplx(lam, bu, reverse=False, name="scan_cplx"):
    S, C = bu.shape
    half = LANES
    CB = _pick(C, (1024, 512, 256))
    TS = _pick(S, (1024, 512, 256, 128, 64, 32, 16, 8))
    groups = CB // (2 * half)
    n_blocks, n_tiles = S // TS, TS // SUBLANES

    def cmul(ar, ai, br, bi):
        return ar * br - ai * bi, ar * bi + ai * br

    def body(lam_ref, bu_ref, o_ref, carry_ref):
        row = lax.broadcasted_iota(jnp.int32, (SUBLANES, half), 0)

        def edge_row(v):
            return jnp.sum(jnp.where(row == (0 if reverse else SUBLANES - 1), v, 0.0), axis=0, keepdims=True)

        @pl.when(pl.program_id(1) == 0)
        def _():
            carry_ref[...] = jnp.zeros(carry_ref.shape, F32)

        consts = []
        for g in range(groups):
            lr = lam_ref[:, 2 * half * g:2 * half * g + half]
            li = lam_ref[:, 2 * half * g + half:2 * half * (g + 1)]
            if reverse:
                li = -li
            l1 = (lr, li)
            l2 = cmul(*l1, *l1)
            l4 = cmul(*l2, *l2)
            pr = jnp.zeros((SUBLANES, half), F32)
            pi = jnp.zeros((SUBLANES, half), F32)
            p = l1
            for r in range(SUBLANES):
                sel = row == ((SUBLANES - 1 - r) if reverse else r)
                pr = jnp.where(sel, p[0], pr)
                pi = jnp.where(sel, p[1], pi)
                p = cmul(*p, *l1)
            consts.append((l1, l2, l4, pr, pi, edge_row(pr), edge_row(pi)))

        def step(i, carry):
            t = (n_tiles - 1 - i) if reverse else i
            off = pl.multiple_of(t * SUBLANES, SUBLANES)
            out = []
            for g in range(groups):
                l1, l2, l4, pr, pi, p8r, p8i = consts[g]
                cr, ci = carry[2 * g], carry[2 * g + 1]
                re, im = pl.ds(2 * half * g, half), pl.ds(2 * half * g + half, half)
                Br = bu_ref[pl.ds(off, SUBLANES), re]
                Bi = bu_ref[pl.ds(off, SUBLANES), im]
                for d, (qr, qi) in ((1, l1), (2, l2), (4, l4)):
                    m = (row < SUBLANES - d) if reverse else (row >= d)
                    sr, si = _roll_rows(Br, d, reverse), _roll_rows(Bi, d, reverse)
                    nr = jnp.where(m, Br + qr * sr - qi * si, Br)
                    ni = jnp.where(m, Bi + qr * si + qi * sr, Bi)
                    Br, Bi = nr, ni
                o_ref[pl.ds(off, SUBLANES), re] = Br + pr * cr - pi * ci
                o_ref[pl.ds(off, SUBLANES), im] = Bi + pr * ci + pi * cr
                er, ei = edge_row(Br), edge_row(Bi)
                out += [er + p8r * cr - p8i * ci, ei + p8r * ci + p8i * cr]
            return tuple(out)

        carry0 = tuple(carry_ref[:, pl.ds(half * k, half)] for k in range(2 * groups))
        carry1 = lax.fori_loop(0, n_tiles, step, carry0, unroll=2)
        for k in range(2 * groups):
            carry_ref[:, pl.ds(half * k, half)] = carry1[k]

    def rows(j, t):
        return ((n_blocks - 1 - t) if reverse else t, j)

    spec = pl.BlockSpec((TS, CB), rows)
    return pl.pallas_call(
        body, grid=(C // CB, n_blocks), in_specs=[pl.BlockSpec((1, CB), lambda j, t: (0, j)), spec],
        out_specs=spec, out_shape=jax.ShapeDtypeStruct((S, C), F32), scratch_shapes=[pltpu.VMEM((1, CB), F32)],
        name=name, compiler_params=_params(("parallel", "arbitrary")),
    )(lam, bu)


ATTN_HEADS_PER_STEP = 2


def _attn_tile(S, window):
    if window is None:
        return _pick(S, (512, 256, 128))
    return max(window, _pick(S, (256, 128)))


def _attn_valid(q_blk, k_blk, T, window):
    kpos = k_blk * T + lax.broadcasted_iota(jnp.int32, (T, T), 0)
    qpos = q_blk * T + lax.broadcasted_iota(jnp.int32, (T, T), 1)
    valid = kpos <= qpos
    if window is not None:
        valid = valid & (qpos - kpos < window)
    return valid


def attn_fwd(q, k, v, sink, cq=None, ck=None, window=None, name="attn_fwd"):
    H, S, Dh = q.shape
    G = H // k.shape[0]
    HP = ATTN_HEADS_PER_STEP
    assert H % HP == 0 and (G == 1 or G % HP == 0)
    KP = HP if G == 1 else 1
    T = _attn_tile(S, window)
    nq = S // T
    scale = Dh ** -0.5
    bias = cq is not None

    if window is None:
        n_steps = nq * (nq + 1) // 2

        def pair(t):
            i = sum((t >= m * (m + 1) // 2).astype(jnp.int32) for m in range(1, nq)) if nq > 1 else 0 * t
            return i, t - (i * (i + 1)) // 2

        def kv_block(i, j):
            return j
    else:
        n_steps = 2 * nq

        def pair(t):
            return t // 2, t % 2

        def kv_block(i, j):
            return jnp.maximum(i - 1 + j, 0)

    def body(*refs):
        if bias:
            q_ref, k_ref, v_ref, s_ref, cq_ref, ck_ref, o_ref, lse_ref, m_scr, l_scr, acc_scr = refs
        else:
            q_ref, k_ref, v_ref, s_ref, o_ref, lse_ref, m_scr, l_scr, acc_scr = refs
        i, j = pair(pl.program_id(1))

        @pl.when(j == 0)
        def _():
            m_scr[...] = jnp.zeros(m_scr.shape, F32) + s_ref[...]
            l_scr[...] = jnp.ones(l_scr.shape, F32)
            acc_scr[...] = jnp.zeros(acc_scr.shape, F32)

        def block(masked):
            valid = _attn_valid(i, kv_block(i, j), T, window) if masked else None
            for b in range(HP):
                kvb = b if G == 1 else 0
                s = _dotf(k_ref[kvb], q_ref[b], "nt") * scale
                if bias:
                    s = s + cq_ref[b] - ck_ref[b]
                if masked:
                    s = jnp.where(valid, s, NEG)
                m_old = m_scr[b]
                m_new = jnp.maximum(m_old, jnp.max(s, axis=0, keepdims=True))
                alpha = jnp.exp(m_old - m_new)
                p = jnp.exp(s - m_new)
                l_scr[b] = alpha * l_scr[b] + jnp.sum(p, axis=0, keepdims=True)
                acc_scr[b] = alpha * acc_scr[b] + _dotf(v_ref[kvb], p, "tn")
                m_scr[b] = m_new

        if window is None:
            pl.when(j < i)(lambda: block(False))
            pl.when(j == i)(lambda: block(True))
        else:
            pl.when(i - 1 + j >= 0)(lambda: block(True))

        @pl.when(j == (i if window is None else 1))
        def _():
            o_ref[...] = acc_scr[...] / l_scr[...]
            lse_ref[...] = m_scr[...] + jnp.log(l_scr[...])

    def kv_map(hp, t):
        return (hp if G == 1 else (hp * HP) // G, kv_block(*pair(t)), 0)

    in_specs = [
        pl.BlockSpec((HP, T, Dh), lambda hp, t: (hp, pair(t)[0], 0)),
        pl.BlockSpec((KP, T, Dh), kv_map),
        pl.BlockSpec((KP, T, Dh), kv_map),
        pl.BlockSpec((HP, 1, 1), lambda hp, t: (hp, 0, 0)),
    ]
    args = [q, k, v, sink]
    if bias:
        in_specs += [pl.BlockSpec((HP, 1, T), lambda hp, t: (hp, 0, pair(t)[0])),
                     pl.BlockSpec((HP, T, 1), lambda hp, t: (hp, kv_block(*pair(t)), 0))]
        args += [cq, ck]
    return pl.pallas_call(
        body, grid=(H // HP, n_steps), in_specs=in_specs,
        out_specs=[pl.BlockSpec((HP, Dh, T), lambda hp, t: (hp, 0, pair(t)[0])),
                   pl.BlockSpec((HP, 1, T), lambda hp, t: (hp, 0, pair(t)[0]))],
        out_shape=[jax.ShapeDtypeStruct((H, Dh, S), F32), jax.ShapeDtypeStruct((H, 1, S), F32)],
        scratch_shapes=[pltpu.VMEM((HP, 1, T), F32), pltpu.VMEM((HP, 1, T), F32), pltpu.VMEM((HP, Dh, T), F32)],
        name=name, compiler_params=_params(("parallel", "arbitrary")),
    )(*args)


def attn_bwd(q, k, v, lse, do, delta, cq=None, ck=None, window=None, name="attn_bwd"):
    H, S, Dh = q.shape
    KVH = k.shape[0]
    G = H // KVH
    HP = ATTN_HEADS_PER_STEP
    assert H % HP == 0 and (G == 1 or G % HP == 0)
    pair_kv = G == 1
    KP = HP if pair_kv else 1
    T = _attn_tile(S, window)
    nq = S // T
    scale = Dh ** -0.5
    bias = cq is not None
    assert not bias or G == 1

    if window is None:
        assert pair_kv
        n_a, n_j = nq * (nq + 1) // 2, 1

        def start(m):
            return m * nq - (m * (m - 1)) // 2

        def blocks(a, j):
            kb = sum((a >= start(m)).astype(jnp.int32) for m in range(1, nq)) if nq > 1 else 0 * a
            return kb, kb + a - start(kb)
    else:
        n_a, n_j = nq, 2

        def blocks(a, j):
            return a, jnp.minimum(a + j, nq - 1)

    def body(*refs):
        if bias:
            (q_ref, k_ref, v_ref, lse_ref, do_ref, dl_ref, cq_ref, ck_ref,
             dq_ref, dk_ref, dv_ref, dcq_ref, dck_ref) = refs
        else:
            q_ref, k_ref, v_ref, lse_ref, do_ref, dl_ref, dq_ref, dk_ref, dv_ref = refs
        a, gp, j = pl.program_id(1), pl.program_id(2), pl.program_id(3)
        kb, qi = blocks(a, j)
        first = (qi == kb) if window is None else (j == 0)

        @pl.when((gp == 0) & first)
        def _():
            dk_ref[...] = jnp.zeros(dk_ref.shape, F32)
            dv_ref[...] = jnp.zeros(dv_ref.shape, F32)
            if bias:
                dck_ref[...] = jnp.zeros(dck_ref.shape, F32)

        @pl.when((a == 0) & (gp == 0) & (j == 0))
        def _():
            dq_ref[...] = jnp.zeros(dq_ref.shape, F32)
            if bias:
                dcq_ref[...] = jnp.zeros(dcq_ref.shape, F32)

        def block(masked):
            off = pl.multiple_of(qi * T, T)
            valid = _attn_valid(qi, kb, T, window) if masked else None
            for b in range(HP):
                kvb = b if pair_kv else 0
                g = 0 if pair_kv else gp * HP + b
                qb, kk, vv = q_ref[b].astype(MXU_DTYPE), k_ref[kvb].astype(MXU_DTYPE), v_ref[kvb].astype(MXU_DTYPE)
                dob = do_ref[b].astype(MXU_DTYPE)
                s = _dotf(kk, qb, "nt") * scale
                if bias:
                    s = s + cq_ref[b] - ck_ref[b]
                if masked:
                    s = jnp.where(valid, s, NEG)
                p = jnp.exp(s - lse_ref[b])
                dv_ref[kvb] += _dotf(p, dob, "nt")
                ds = p * (_dotf(vv, dob) - dl_ref[b])
                dsb = ds.astype(MXU_DTYPE)
                dk_ref[kvb] += scale * _dotf(dsb, qb)
                dq_ref[kvb, g, pl.ds(off, T), :] += scale * _dotf(dsb, kk, "tn")
                if bias:
                    dcq_ref[kvb, g, :, pl.ds(off, T)] += jnp.sum(ds, axis=0, keepdims=True)
                    dck_ref[kvb] -= jnp.sum(ds, axis=1, keepdims=True)

        if window is None:
            pl.when(qi > kb)(lambda: block(False))
            pl.when(qi == kb)(lambda: block(True))
        else:
            pl.when(kb + j <= nq - 1)(lambda: block(True))

    def qmap(kvp, a, gp, j):
        return (kvp if pair_kv else (kvp * G) // HP + gp, blocks(a, j)[1], 0)

    def qmap_t(kvp, a, gp, j):
        return (kvp if pair_kv else (kvp * G) // HP + gp, 0, blocks(a, j)[1])

    def kmap(kvp, a, gp, j):
        return (kvp, blocks(a, j)[0], 0)

    in_specs = [
        pl.BlockSpec((HP, T, Dh), qmap),
        pl.BlockSpec((KP, T, Dh), kmap),
        pl.BlockSpec((KP, T, Dh), kmap),
        pl.BlockSpec((HP, 1, T), qmap_t),
        pl.BlockSpec((HP, Dh, T), qmap_t),
        pl.BlockSpec((HP, 1, T), qmap_t),
    ]
    args = [q, k, v, lse, do, delta]
    out_specs = [
        pl.BlockSpec((KP, G, S, Dh), lambda kvp, a, gp, j: (kvp, 0, 0, 0)),
        pl.BlockSpec((KP, T, Dh), kmap),
        pl.BlockSpec((KP, T, Dh), kmap),
    ]
    out_shape = [jax.ShapeDtypeStruct((KVH, G, S, Dh), F32), jax.ShapeDtypeStruct((KVH, S, Dh), F32),
                 jax.ShapeDtypeStruct((KVH, S, Dh), F32)]
    if bias:
        in_specs += [pl.BlockSpec((HP, 1, T), qmap_t), pl.BlockSpec((HP, T, 1), kmap)]
        args += [cq, ck]
        out_specs += [pl.BlockSpec((KP, G, 1, S), lambda kvp, a, gp, j: (kvp, 0, 0, 0)),
                      pl.BlockSpec((KP, T, 1), kmap)]
        out_shape += [jax.ShapeDtypeStruct((KVH, G, 1, S), F32), jax.ShapeDtypeStruct((KVH, S, 1), F32)]
    res = pl.pallas_call(
        body, grid=(KVH // KP, n_a, 1 if pair_kv else G // HP, n_j), in_specs=in_specs, out_specs=out_specs,
        out_shape=out_shape, name=name, compiler_params=_params(("arbitrary", "arbitrary", "arbitrary", "arbitrary")),
    )(*args)
    dq = res[0].reshape(H, S, Dh)
    if bias:
        return dq, res[1], res[2], res[3].reshape(H, 1, S), res[4]
    return dq, res[1], res[2]


def _rms(x, g):
    return x * lax.rsqrt(jnp.mean(x * x, axis=-1, keepdims=True) + EPS) * g


def _sigmoid(x):
    return 1.0 / (1.0 + jnp.exp(-x))


def _softplus(x):
    return jnp.maximum(x, 0.0) + jnp.log(1.0 + jnp.exp(-jnp.abs(x)))


def _log_sigmoid(x):
    return jnp.minimum(x, 0.0) - jnp.log(1.0 + jnp.exp(-jnp.abs(x)))


def _gelu(x):
    return 0.5 * x * (1.0 + jnp.tanh(math.sqrt(2.0 / math.pi) * (x + 0.044715 * (x * x * x))))


def _silu(x):
    return x * _sigmoid(x)


def _ffn_act(gu):
    f = gu.shape[1] // 2
    return MACARON * _silu(gu[:, :f]) * gu[:, f:]


def _qk_prep(rope):
    def f(x, *rest):
        if rope:
            cos, sin, g, rot = rest
        else:
            (g,) = rest
        y = _rms(x, g)
        if rope:
            y = y * cos + jnp.dot(y, rot, precision=HI, preferred_element_type=F32) * sin
        return y
    return f


def _lru_gates(pre, xc, ba, bx, lam):
    w = xc.shape[1]
    r = _sigmoid(pre[:, :w] + ba)
    i = _sigmoid(pre[:, w:] + bx)
    log_a = -LRU_C * r * _softplus(lam)
    a = jnp.exp(log_a)
    b = jnp.sqrt(1.0 - jnp.exp(2.0 * log_a)) * (i * xc)
    return a, b


def _lru_conv(x0, x1, x2, x3, w0, w1, w2, w3, cb):
    return cb + x0 * w0 + x1 * w1 + x2 * w2 + x3 * w3


def _s5_params(lre, lim, ldt, gsel, bre, bim):
    dt = jnp.sum(gsel * jnp.exp(ldt), axis=1, keepdims=True)
    er = jnp.exp(lre * dt)
    ang = lim * dt
    lbr, lbi = er * jnp.cos(ang), er * jnp.sin(ang)
    nr, ni = lbr - 1.0, lbi
    den = lre * lre + lim * lim
    fr, fi = (nr * lre + ni * lim) / den, (ni * lre - nr * lim) / den
    return lbr, lbi, fr * bre - fi * bim, fr * bim + fi * bre


def _s5_out(yssm, u, d):
    return _gelu(yssm + d * u)


def _glu(z, gl, gb):
    return z * _sigmoid(gl + gb)


def _ple_out(x, gpre, epre, pn):
    return x + _sigmoid(gpre) * _rms(epre, pn)


def _vjp(fn, args, cots):
    _, pull = jax.vjp(fn, *args)
    return pull(cots)


def add_norm(x, y, g, name):
    D = x.shape[1]
    if y is None:
        return x, rowwise(lambda xv, gv: ([_rms(xv, gv)], []), [x], [g], outs=[(D, BF16)], name=name)[0]
    xn, n = rowwise(lambda xv, yv, gv: ([xv + yv, _rms(xv + yv, gv)], []), [x, y], [g],
                    outs=[(D, F32), (D, BF16)], name=name)
    return xn, n


def norm_bwd(x, g, dn, dx_res, name):
    D = x.shape[1]

    def f(xv, dnv, dxv, gv):
        dx, dg = _vjp(_rms, (xv, gv), dnv)
        return [dxv + dx], [dg]

    return rowwise(f, [x, dn, dx_res], [g], outs=[(D, F32)], accs=[(1, D)], name=name)


def _swiglu(g, u):
    return MACARON * _silu(g) * u


def _dotf(a, b, mode="nn"):
    return lax.dot_general(a.astype(MXU_DTYPE), b.astype(MXU_DTYPE), _DOT_DIMS[mode], preferred_element_type=F32)


def ffn_up(n, wgu, ig, iu, name):
    S, D = n.shape
    C, _, _, Fc = wgu.shape
    tm = _tile(S, MM_TILE_M, 2 * SUBLANES)

    def body(n_ref, wg_ref, wu_ref, g_ref, u_ref, a_ref):
        g = _dotf(n_ref[...], wg_ref[...])
        u = _dotf(n_ref[...], wu_ref[...])
        g_ref[...] = g.astype(g_ref.dtype)
        u_ref[...] = u.astype(u_ref.dtype)
        a_ref[...] = _swiglu(g, u).astype(a_ref.dtype)

    hid = pl.BlockSpec((None, tm, Fc), lambda s, i: (s, i, 0))
    return pl.pallas_call(
        body, grid=(C, S // tm),
        in_specs=[pl.BlockSpec((tm, D), lambda s, i: (i, 0)),
                  pl.BlockSpec((None, None, D, Fc), lambda s, i: (s, ig, 0, 0)),
                  pl.BlockSpec((None, None, D, Fc), lambda s, i: (s, iu, 0, 0))],
        out_specs=[hid, hid, hid], out_shape=[jax.ShapeDtypeStruct((C, S, Fc), BF16)] * 3,
        name=name, compiler_params=_params(("parallel", "parallel")),
    )(n, wgu, wgu)


def ffn_down(act, wd, iw, x, gain, name):
    C, S, Fc = act.shape
    D = wd.shape[-1]
    tm = _tile(S, MM_TILE_M, 2 * SUBLANES)

    def body(a_ref, w_ref, x_ref, g_ref, xo_ref, n_ref, acc_ref):
        s = pl.program_id(1)
        r = _dotf(a_ref[...], w_ref[...])

        @pl.when(s == 0)
        def _():
            acc_ref[...] = r

        @pl.when(s > 0)
        def _():
            acc_ref[...] += r

        @pl.when(s == C - 1)
        def _():
            xn = x_ref[...] + acc_ref[...]
            xo_ref[...] = xn
            n_ref[...] = _rms(xn, g_ref[...]).astype(n_ref.dtype)

    row = pl.BlockSpec((tm, D), lambda i, s: (i, 0))
    return pl.pallas_call(
        body, grid=(S // tm, C),
        in_specs=[pl.BlockSpec((None, tm, Fc), lambda i, s: (s, i, 0)),
                  pl.BlockSpec((None, None, Fc, D), lambda i, s: (s, iw, 0, 0)), row,
                  pl.BlockSpec((1, D), lambda i, s: (0, 0))],
        out_specs=[row, row], out_shape=[jax.ShapeDtypeStruct((S, D), F32), jax.ShapeDtypeStruct((S, D), BF16)],
        scratch_shapes=[pltpu.VMEM((tm, D), F32)], name=name, compiler_params=_params(("parallel", "arbitrary")),
    )(act, wd, x, gain)


def ffn_down_bwd(dy, wd, iw, g, u, name):
    C, S, Fc = g.shape
    D = dy.shape[1]
    tm = _tile(S, MM_TILE_M, 2 * SUBLANES)

    def body(dy_ref, w_ref, g_ref, u_ref, dg_ref, du_ref):
        dact = MACARON * _dotf(dy_ref[...], w_ref[...], "nt")
        g, u = g_ref[...].astype(F32), u_ref[...].astype(F32)
        sg = _sigmoid(g)
        gs = g * sg
        dg_ref[...] = (dact * u * (sg + gs * (1.0 - sg))).astype(dg_ref.dtype)
        du_ref[...] = (dact * gs).astype(du_ref.dtype)

    hid = pl.BlockSpec((None, tm, Fc), lambda s, i: (s, i, 0))
    return pl.pallas_call(
        body, grid=(C, S // tm),
        in_specs=[pl.BlockSpec((tm, D), lambda s, i: (i, 0)),
                  pl.BlockSpec((None, None, Fc, D), lambda s, i: (s, iw, 0, 0)), hid, hid],
        out_specs=[hid, hid], out_shape=[jax.ShapeDtypeStruct((C, S, Fc), BF16)] * 2,
        name=name, compiler_params=_params(("parallel", "parallel")),
    )(dy, wd, g, u)


def ffn_dn(dg, du, wgu, ig, iu, x, gain, dx_res, name):
    C, S, Fc = dg.shape
    D = wgu.shape[2]
    tm = _tile(S, MM_TILE_M, 2 * SUBLANES)
    parts = 2 if tm % (4 * SUBLANES) == 0 else 1

    def body(dg_ref, du_ref, wg_ref, wu_ref, x_ref, g_ref, r_ref, dx_ref, dgain_ref, acc_ref):
        i, s = pl.program_id(0), pl.program_id(1)
        r = _dotf(dg_ref[...], wg_ref[...], "nt") + _dotf(du_ref[...], wu_ref[...], "nt")

        @pl.when((i == 0) & (s == 0))
        def _():
            dgain_ref[...] = jnp.zeros(dgain_ref.shape, F32)

        @pl.when(s == 0)
        def _():
            acc_ref[...] = r

        @pl.when(s > 0)
        def _():
            acc_ref[...] += r

        @pl.when(s == C - 1)
        def _():
            for part in range(parts):
                rows = pl.ds(part * (tm // parts), tm // parts)
                dx, dgain = _vjp(_rms, (x_ref[rows, :], g_ref[...]), acc_ref[rows, :])
                dx_ref[rows, :] = r_ref[rows, :] + dx
                dgain_ref[...] += dgain

    hid = pl.BlockSpec((None, tm, Fc), lambda i, s: (s, i, 0))
    row = pl.BlockSpec((tm, D), lambda i, s: (i, 0))
    vec = pl.BlockSpec((1, D), lambda i, s: (0, 0))
    return pl.pallas_call(
        body, grid=(S // tm, C),
        in_specs=[hid, hid, pl.BlockSpec((None, None, D, Fc), lambda i, s: (s, ig, 0, 0)),
                  pl.BlockSpec((None, None, D, Fc), lambda i, s: (s, iu, 0, 0)), row, vec, row],
        out_specs=[row, vec], out_shape=[jax.ShapeDtypeStruct((S, D), F32), jax.ShapeDtypeStruct((1, D), F32)],
        scratch_shapes=[pltpu.VMEM((tm, D), F32)], name=name, compiler_params=_params(("arbitrary", "arbitrary")),
    )(dg, du, wgu, wgu, x, gain, dx_res)


def ffn_dw(a, d, buf, idx, shape, blocked, name):
    C, P, M, N = shape
    S = d.shape[-2]
    tk = _tile(S, MM_TILE_M, 2 * SUBLANES)
    nk = S // tk

    def body(*refs):
        a_ref, d_ref, o_ref, acc_ref = refs[0], refs[1], refs[-2], refs[-1]
        k = pl.program_id(1)
        r = _dotf(a_ref[...], d_ref[...], "tn")

        @pl.when(k == 0)
        def _():
            acc_ref[...] = r

        @pl.when(k > 0)
        def _():
            acc_ref[...] += r

        @pl.when(k == nk - 1)
        def _():
            o_ref[...] = acc_ref[...].astype(o_ref.dtype)

    if blocked == "a":
        a_spec = pl.BlockSpec((None, tk, M), lambda s, k: (s, k, 0))
        d_spec = pl.BlockSpec((tk, N), lambda s, k: (k, 0))
    else:
        a_spec = pl.BlockSpec((tk, M), lambda s, k: (k, 0))
        d_spec = pl.BlockSpec((None, tk, N), lambda s, k: (s, k, 0))
    out_spec = pl.BlockSpec((None, None, M, N), lambda s, k: (s, idx, 0, 0))
    out_shape = jax.ShapeDtypeStruct(tuple(shape), BF16)
    scratch = [pltpu.VMEM((M, N), F32)]
    if buf is None:
        return pl.pallas_call(body, grid=(C, nk), in_specs=[a_spec, d_spec], out_specs=out_spec, out_shape=out_shape,
                              scratch_shapes=scratch, name=name,
                              compiler_params=_params(("parallel", "arbitrary")))(a, d)
    return pl.pallas_call(body, grid=(C, nk), in_specs=[a_spec, d_spec, pl.BlockSpec(memory_space=pl.ANY)],
                          out_specs=out_spec, out_shape=out_shape, input_output_aliases={2: 0},
                          scratch_shapes=scratch, name=name,
                          compiler_params=_params(("parallel", "arbitrary")))(a, d, buf)


def ffn_fwd(n, x, gain, wgu, wd, ig, iu, iw, tag):
    g, u, act = ffn_up(n, wgu, ig, iu, f"ffn_up_{tag}")
    x_new, n_new = ffn_down(act, wd, iw, x, gain, f"ffn_down_{tag}")
    return x_new, n_new, (n, g, u, act)


def ffn_bwd(dy, saved, x, gain, wgu, wd, ig, iu, iw, gbuf, tag):
    n, g, u, act = saved
    gwgu, gwd = gbuf
    dg, du = ffn_down_bwd(dy, wd, iw, g, u, f"ffn_down_bwd_{tag}")
    gwd = ffn_dw(act, dy, gwd, iw, (N_CHIPS,) + wd.shape[1:], "a", f"ffn_dwd_{tag}")
    dx, dgain = ffn_dn(dg, du, wgu, ig, iu, x, gain, dy, f"ffn_dn_{tag}")
    gwgu = ffn_dw(n, dg, gwgu, ig, (N_CHIPS,) + wgu.shape[1:], "d", f"ffn_dwg_{tag}")
    gwgu = ffn_dw(n, du, gwgu, iu, (N_CHIPS,) + wgu.shape[1:], "d", f"ffn_dwu_{tag}")
    return dx, dgain, (gwgu, gwd)


def _heads(x, H):
    S = x.shape[0]
    return x.reshape(S, H, HEAD_DIM).transpose(1, 0, 2)


def _unheads(x):
    H, S, _ = x.shape
    return x.transpose(1, 0, 2).reshape(S, H * HEAD_DIM)


def _heads_t(x, H):
    return x.T.reshape(H, HEAD_DIM, x.shape[0])


def _unheads_t(x):
    return x.reshape(x.shape[0] * x.shape[1], x.shape[2]).T


def _shift_down(x, n=1):
    return jnp.pad(x, ((n, 0), (0, 0)))[:x.shape[0]]


def _shift_up(x, n=1):
    return jnp.pad(x, ((0, n), (0, 0)))[n:]


def _block_diag(w):
    B, I, J = w.shape
    eye = jnp.eye(B, dtype=w.dtype)
    return (w[:, :, None, :] * eye[:, None, :, None]).reshape(B * I, B * J)


def _block_diag_take(x, B):
    I, J = x.shape[0] // B, x.shape[1] // B
    eye = jnp.eye(B, dtype=x.dtype)
    return jnp.sum(x.reshape(B, I, B, J) * eye[:, None, :, None], axis=2)


def _rope_tables(S):
    half = HEAD_DIM // 2
    inv = jnp.power(ROPE_THETA, -jnp.arange(half, dtype=F32) / half)
    ang = jnp.arange(S, dtype=F32)[:, None] * inv[None, :]
    cos = jnp.concatenate([jnp.cos(ang), jnp.cos(ang)], axis=1)
    sin = jnp.concatenate([jnp.sin(ang), jnp.sin(ang)], axis=1)
    r = jnp.arange(HEAD_DIM)[:, None]
    c = jnp.arange(HEAD_DIM)[None, :]
    rot = jnp.where(r == c + half, -1.0, 0.0) + jnp.where(c == r + half, 1.0, 0.0)
    return cos, sin, rot.astype(F32)


def qk_prep_fwd(x_hm, g, rope_tabs, name):
    H, S, Dh = x_hm.shape
    rows = [x_hm.reshape(H * S, Dh)]
    consts = [g.reshape(1, Dh)]
    periods = [None]
    if rope_tabs is not None:
        rows += [rope_tabs[0], rope_tabs[1]]
        consts += [rope_tabs[2]]
        periods += [S, S]
    fn = _qk_prep(rope_tabs is not None)
    y = rowwise(lambda *a: ([fn(*a)], []), rows, consts, outs=[(Dh, F32)], name=name, periods=periods)[0]
    return y.reshape(H, S, Dh)


def qk_prep_bwd(x_hm, g, rope_tabs, dy_hm, name):
    H, S, Dh = x_hm.shape
    rope = rope_tabs is not None
    rows = [x_hm.reshape(H * S, Dh), dy_hm.reshape(H * S, Dh)]
    consts = [g.reshape(1, Dh)]
    periods = [None, None]
    if rope:
        rows += [rope_tabs[0], rope_tabs[1]]
        consts += [rope_tabs[2]]
        periods += [S, S]
    fn = _qk_prep(rope)

    def f(xv, dyv, *rest):
        if rope:
            cos, sin, gv, rot = rest
            dx, dg = _vjp(lambda a, b: fn(a, cos, sin, b, rot), (xv, gv), dyv)
        else:
            (gv,) = rest
            dx, dg = _vjp(fn, (xv, gv), dyv)
        return [dx], [dg]

    dx, dg = rowwise(f, rows, consts, outs=[(Dh, F32)], accs=[(1, Dh)], name=name, periods=periods)
    return dx.reshape(H, S, Dh), dg.reshape(Dh)


def attn_delta(do_t, o_t, name):
    H, Dh, S = o_t.shape

    def body(a_ref, b_ref, o_ref):
        o_ref[...] = jnp.sum(a_ref[...] * b_ref[...], axis=0, keepdims=True)

    spec = pl.BlockSpec((None, Dh, S), lambda h: (h, 0, 0))
    return pl.pallas_call(
        body, grid=(H,), in_specs=[spec, spec], out_specs=pl.BlockSpec((None, 1, S), lambda h: (h, 0, 0)),
        out_shape=jax.ShapeDtypeStruct((H, 1, S), F32), name=name, compiler_params=_params(("parallel",)),
    )(do_t, o_t)


def even_mixer_fwd(h, w, tag):
    S = h.shape[0]
    W = 512
    H = 8
    z = mm(h, w["w_in"], name=f"ev_in_{tag}")
    xa, ya, q, k, v, f = (z[:, 0:512], z[:, 512:1024], z[:, 1024:1536], z[:, 1536:2048], z[:, 2048:2560],
                          z[:, 2560:2688])
    xs = [_shift_down(xa, LRU_CONV - 1 - tap) for tap in range(LRU_CONV)]
    taps = [w["conv_w"][tap][None] for tap in range(LRU_CONV)]
    xc = rowwise(lambda *a: ([_lru_conv(*a)], []), xs, taps + [w["conv_b"]], outs=[(W, F32)],
                 name=f"lru_conv_{tag}")[0]
    pre = mm(xc, w["w_ax"], name=f"lru_gates_mm_{tag}")
    a, b = rowwise(lambda p_, x_, ba, bx, lam: (list(_lru_gates(p_, x_, ba, bx, lam)), []), [pre, xc],
                   [w["ba"], w["bx"], w["lam"]], outs=[(W, F32), (W, F32)], name=f"lru_gates_{tag}")
    hs = scan_real(a, b, name=f"lru_scan_{tag}")
    a_out = rowwise(lambda y_, h_: ([_gelu(y_) * h_], []), [ya, hs], outs=[(W, F32)], name=f"lru_out_{tag}")[0]
    lf = rowwise(lambda f_, bf: ([_log_sigmoid(f_ + bf)], []), [f], [w["bf"]], outs=[(LANES, F32)],
                 name=f"fox_logf_{tag}")[0]
    c = scan_real(jnp.ones_like(lf), lf, name=f"fox_cumsum_{tag}")
    c_hm = c[:, :H].T
    q_hm, k_hm, v_hm = _heads(q, H), _heads(k, H), _heads(v, H)
    qn = qk_prep_fwd(q_hm, w["qn"], None, f"fox_qprep_{tag}")
    kn = qk_prep_fwd(k_hm, w["kn"], None, f"fox_kprep_{tag}")
    sink = jnp.full((H, 1, 1), NEG, F32)
    o_hm, lse = attn_fwd(qn, kn, v_hm, sink, c_hm[:, None, :], c_hm[:, :, None], name=f"fox_attn_{tag}")
    mo = jnp.concatenate([a_out, _unheads_t(o_hm)], axis=1).astype(BF16)
    saved = dict(h=h, xs=xs, xc=xc, pre=pre, a=a, hs=hs, ya=ya, f=f, c_hm=c_hm, q_hm=q_hm, k_hm=k_hm,
                 v_hm=v_hm, qn=qn, kn=kn, o_hm=o_hm, lse=lse, mo=mo)
    return mo, saved


def even_mixer_bwd(dy, sv, w, tag):
    W = 512
    H = 8
    S = dy.shape[0]
    g = {}
    dmo = mm(dy, w["w_out"], "nt", name=f"ev_dmo_{tag}")
    g["w_out"] = mm(sv["mo"], dy, "tn", name=f"ev_dwout_{tag}")
    da_out, do = dmo[:, :W], dmo[:, W:]
    do_hm = _heads_t(do, H)
    delta = attn_delta(do_hm, sv["o_hm"], f"fox_delta_{tag}")
    c_hm = sv["c_hm"]
    dqn, dkn, dv_hm, dcq, dck = attn_bwd(sv["qn"], sv["kn"], sv["v_hm"], sv["lse"], do_hm, delta,
                                          c_hm[:, None, :], c_hm[:, :, None], name=f"fox_attn_bwd_{tag}")
    dq_hm, g["qn"] = qk_prep_bwd(sv["q_hm"], w["qn"], None, dqn, f"fox_qprep_bwd_{tag}")
    dk_hm, g["kn"] = qk_prep_bwd(sv["k_hm"], w["kn"], None, dkn, f"fox_kprep_bwd_{tag}")
    dc = (dcq[:, 0, :] + dck[:, :, 0]).T
    dc = jnp.pad(dc, ((0, 0), (0, LANES - H)))
    dlf = scan_real(jnp.ones_like(dc), dc, reverse=True, name=f"fox_cumsum_bwd_{tag}")

    def f_logf(f_, d_, bf):
        df, dbf = _vjp(lambda a_, b_: _log_sigmoid(a_ + b_), (f_, bf), d_)
        return [df], [dbf]

    df, dbf = rowwise(f_logf, [sv["f"], dlf], [w["bf"]], outs=[(LANES, F32)], accs=[(1, LANES)],
                      name=f"fox_logf_bwd_{tag}")
    g["bf"] = dbf[0, :H]
    def f_out(y_, h_, d_):
        dyv, dhv = _vjp(lambda a_, b_: _gelu(a_) * b_, (y_, h_), d_)
        return [dyv, dhv], []

    dya, dhs = rowwise(f_out, [sv["ya"], sv["hs"], da_out], outs=[(W, F32), (W, F32)], name=f"lru_out_bwd_{tag}")
    gs = scan_real(_shift_up(sv["a"]), dhs, reverse=True, name=f"lru_scan_bwd_{tag}")

    def f_gates(p_, x_, g_, hp_, ba, bx, lam):
        dp, dx, dba, dbx, dlam = _vjp(_lru_gates, (p_, x_, ba, bx, lam), (g_ * hp_, g_))
        return [dp, dx], [dba, dbx, dlam]

    dpre, dxc, dba, dbx, dlam = rowwise(f_gates, [sv["pre"], sv["xc"], gs, _shift_down(sv["hs"])],
                                        [w["ba"], w["bx"], w["lam"]], outs=[(2 * W, BF16), (W, F32)],
                                        accs=[(1, W)] * 3, name=f"lru_gates_bwd_{tag}")
    g["ba"], g["bx"], g["lam"] = dba[0], dbx[0], dlam[0]
    dxc2 = mm(dpre, w["w_ax"], "nt", name=f"lru_gates_mm_dx_{tag}")
    g["w_ax"] = mm(sv["xc"], dpre, "tn", name=f"lru_gates_mm_dw_{tag}")

    def f_conv(d1, d2, x0, x1, x2, x3):
        d = d1 + d2
        return [d], [jnp.sum(d, axis=0, keepdims=True)] + [jnp.sum(d * xv, axis=0, keepdims=True)
                                                           for xv in (x0, x1, x2, x3)]

    dxc_t, dcb, dw0, dw1, dw2, dw3 = rowwise(f_conv, [dxc, dxc2] + sv["xs"], outs=[(W, F32)],
                                             accs=[(1, W)] * 5, name=f"lru_conv_bwd_{tag}")
    g["conv_b"] = dcb[0]
    g["conv_w"] = jnp.concatenate([dw0, dw1, dw2, dw3], axis=0)
    ds_ = [_shift_up(dxc_t, LRU_CONV - 1 - tap) for tap in range(LRU_CONV)]
    taps = [w["conv_w"][tap][None] for tap in range(LRU_CONV)]
    dxa = rowwise(lambda a, b, c, d, w0, w1, w2, w3: ([a * w0 + b * w1 + c * w2 + d * w3], []), ds_, taps,
                  outs=[(W, F32)], name=f"lru_conv_dx_{tag}")[0]
    dz = jnp.concatenate([dxa, dya, _unheads(dq_hm), _unheads(dk_hm), _unheads(dv_hm), df], axis=1).astype(BF16)
    g["w_in"] = mm(sv["h"], dz, "tn", name=f"ev_dwin_{tag}")
    return dz, g


def odd_mixer_fwd(h, w, tag):
    S = h.shape[0]
    H, KVH = 8, 2
    z = mm(h, w["w_in"], name=f"od_in_{tag}")
    q, k, v, u = z[:, 0:512], z[:, 512:640], z[:, 640:768], z[:, 768:1280]
    tabs = _rope_tables(S)
    q_hm, k_hm, v_hm = _heads(q, H), _heads(k, KVH), _heads(v, KVH)
    qn = qk_prep_fwd(q_hm, w["qn"], tabs, f"swa_qprep_{tag}")
    kn = qk_prep_fwd(k_hm, w["kn"], tabs, f"swa_kprep_{tag}")
    sink = w["sinks"].reshape(H, 1, 1)
    o_hm, lse = attn_fwd(qn, kn, v_hm, sink, window=SWA_WINDOW, name=f"swa_attn_{tag}")
    lam, bexp = w["s5_lam"], w["s5_bexp"]
    bu = mm_blocks(u, bexp, name=f"s5_bu_{tag}")
    hs = scan_cplx(lam, bu, name=f"s5_scan_{tag}")
    yssm = mm_blocks(hs, w["s5_cexp"], name=f"s5_y_{tag}")
    zz = rowwise(lambda y_, u_, d_: ([_s5_out(y_, u_, d_)], []), [yssm, u], [w["s5_d"]], outs=[(512, F32)],
                 name=f"s5_gelu_{tag}")[0]
    gl = mm(zz, w["glu_w"], name=f"s5_glu_mm_{tag}")
    d_out = rowwise(lambda z_, g_, b_: ([_glu(z_, g_, b_)], []), [zz, gl], [w["glu_b"]], outs=[(512, F32)],
                    name=f"s5_glu_{tag}")[0]
    mo = jnp.concatenate([_unheads_t(o_hm), d_out], axis=1).astype(BF16)
    saved = dict(h=h, q_hm=q_hm, k_hm=k_hm, v_hm=v_hm, qn=qn, kn=kn, o_hm=o_hm, lse=lse, u=u, hs=hs, yssm=yssm,
                 zz=zz, gl=gl, mo=mo, tabs=tabs)
    return mo, saved


def odd_mixer_bwd(dy, sv, w, tag):
    H, KVH = 8, 2
    g = {}
    dmo = mm(dy, w["w_out"], "nt", name=f"od_dmo_{tag}")
    g["w_out"] = mm(sv["mo"], dy, "tn", name=f"od_dwout_{tag}")
    do, dd = dmo[:, :512], dmo[:, 512:]
    do_hm = _heads_t(do, H)
    delta = attn_delta(do_hm, sv["o_hm"], f"swa_delta_{tag}")
    dqn, dkn, dv_hm = attn_bwd(sv["qn"], sv["kn"], sv["v_hm"], sv["lse"], do_hm, delta, window=SWA_WINDOW,
                               name=f"swa_attn_bwd_{tag}")
    dq_hm, g["qn"] = qk_prep_bwd(sv["q_hm"], w["qn"], sv["tabs"], dqn, f"swa_qprep_bwd_{tag}")
    dk_hm, g["kn"] = qk_prep_bwd(sv["k_hm"], w["kn"], sv["tabs"], dkn, f"swa_kprep_bwd_{tag}")
    lse_t, delta_t = sv["lse"][:, 0, :].T, delta[:, 0, :].T
    g["sinks"] = rowwise(lambda l_, d_, s_: ([], [jnp.sum(-jnp.exp(s_ - l_) * d_, axis=0, keepdims=True)]),
                         [lse_t, delta_t], [w["sinks"].reshape(1, H)], accs=[(1, H)], name=f"swa_dsink_{tag}")[0][0]
    def f_glu(z_, g_, d_, b_):
        dz_, dg_, db_ = _vjp(_glu, (z_, g_, b_), d_)
        return [dz_, dg_], [db_]

    dzz1, dgl, dglb = rowwise(f_glu, [sv["zz"], sv["gl"], dd], [w["glu_b"]], outs=[(512, F32), (512, BF16)],
                              accs=[(1, 512)], name=f"s5_glu_bwd_{tag}")
    g["glu_b"] = dglb[0]
    g["glu_w"] = mm(sv["zz"], dgl, "tn", name=f"s5_glu_dw_{tag}")
    dzz2 = mm(dgl, w["glu_w"], "nt", name=f"s5_glu_dz_{tag}")

    def f_gelu(y_, u_, d1, d2, dpar):
        dy_, du_, dd_ = _vjp(_s5_out, (y_, u_, dpar), d1 + d2)
        return [dy_, du_], [dd_]

    dyssm, du1, dsd = rowwise(f_gelu, [sv["yssm"], sv["u"], dzz1, dzz2], [w["s5_d"]],
                              outs=[(512, F32), (512, F32)], accs=[(1, 512)], name=f"s5_gelu_bwd_{tag}")
    g["s5_d"] = dsd[0]
    dhs = mm_blocks(dyssm, w["s5_cexp"], "nt", name=f"s5_dh_{tag}")
    g["s5_cexp"] = _block_diag(mm_blocks_tn(sv["hs"], dyssm, S5_BLOCKS, name=f"s5_dc_{tag}"))
    gs = scan_cplx(w["s5_lam"], dhs, reverse=True, name=f"s5_scan_bwd_{tag}")
    g["s5_bexp"] = _block_diag(mm_blocks_tn(sv["u"], gs, S5_BLOCKS, name=f"s5_db_{tag}"))
    du2 = mm_blocks(gs, w["s5_bexp"], "nt", name=f"s5_du_{tag}")

    def f_dlam(g_, hp_):
        C = g_.shape[1]
        outs_r, outs_i = [], []
        for j in range(C // (2 * LANES)):
            gr, gi = g_[:, 2 * LANES * j:2 * LANES * j + LANES], g_[:, 2 * LANES * j + LANES:2 * LANES * (j + 1)]
            hr, hi = hp_[:, 2 * LANES * j:2 * LANES * j + LANES], hp_[:, 2 * LANES * j + LANES:2 * LANES * (j + 1)]
            outs_r.append(jnp.sum(gr * hr + gi * hi, axis=0, keepdims=True))
            outs_i.append(jnp.sum(gi * hr - gr * hi, axis=0, keepdims=True))
        return [], [jnp.concatenate([x for pair in zip(outs_r, outs_i) for x in pair], axis=1)]

    g["s5_lam"] = rowwise(f_dlam, [gs, _shift_down(sv["hs"])], accs=[(1, gs.shape[1])], name=f"s5_dlam_{tag}")[0]
    du = rowwise(lambda a_, b_: ([a_ + b_], []), [du1, du2], outs=[(512, F32)], name=f"s5_du_add_{tag}")[0]
    dz = jnp.concatenate([_unheads(dq_hm), _unheads(dk_hm), _unheads(dv_hm), du], axis=1).astype(BF16)
    g["w_in"] = mm(sv["h"], dz, "tn", name=f"od_dwin_{tag}")
    return dz, g


def _s5_cols(x_re, x_im):
    n = x_re.shape[0] // LANES
    return jnp.stack([x_re.reshape(n, LANES), x_im.reshape(n, LANES)], axis=1).reshape(1, 2 * n * LANES)


def _s5_uncols(x):
    n = x.shape[1] // (2 * LANES)
    y = x.reshape(n, 2, LANES)
    return y[:, 0].reshape(-1), y[:, 1].reshape(-1)


def _s5_gsel():
    return jnp.repeat(jnp.eye(S5_GROUPS, dtype=F32), S5_STATE, axis=0)


def s5_prep_fwd(lre, lim, ldt, bre, bim, cre, cim, tag):
    GP = S5_GROUPS * S5_STATE
    ins = [lre.reshape(GP, 1), lim.reshape(GP, 1), ldt.reshape(1, S5_GROUPS), _s5_gsel(),
           bre.reshape(GP, S5_GROUP), bim.reshape(GP, S5_GROUP)]
    lbr, lbi, bbr, bbi = whole(_s5_params, ins, [((GP, 1), F32)] * 2 + [((GP, S5_GROUP), F32)] * 2,
                               name=f"s5_params_{tag}")
    lam = _s5_cols(lbr[:, 0], lbi[:, 0])

    def expand_b(bb):
        return _block_diag(bb.reshape(S5_GROUPS, S5_STATE, S5_GROUP).transpose(0, 2, 1))

    n = GP // LANES
    bexp = jnp.stack([expand_b(bbr).reshape(-1, n, LANES), expand_b(bbi).reshape(-1, n, LANES)],
                     axis=2).reshape(-1, 2 * GP)
    c_r = _block_diag(cre.transpose(0, 2, 1))
    c_i = _block_diag(cim.transpose(0, 2, 1))
    cexp = jnp.stack([c_r.reshape(n, LANES, -1), -c_i.reshape(n, LANES, -1)], axis=1).reshape(2 * GP, -1)
    cb, sb = bexp.shape[0] // S5_BLOCKS, bexp.shape[1] // S5_BLOCKS
    bexp = jnp.stack([bexp[cb * j:cb * (j + 1), sb * j:sb * (j + 1)] for j in range(S5_BLOCKS)])
    cexp = jnp.stack([cexp[sb * j:sb * (j + 1), cb * j:cb * (j + 1)] for j in range(S5_BLOCKS)])
    return lam, bexp.astype(BF16), cexp.astype(BF16), ins


def s5_prep_bwd(ins, dlam, dbexp, dcexp, tag):
    GP = S5_GROUPS * S5_STATE
    n = GP // LANES
    dlr, dli = _s5_uncols(dlam)
    db = dbexp.reshape(-1, n, 2, LANES)

    def take_b(x):
        return _block_diag_take(x, S5_GROUPS).transpose(0, 2, 1).reshape(GP, S5_GROUP)

    dbbr, dbbi = take_b(db[:, :, 0].reshape(-1, GP)), take_b(db[:, :, 1].reshape(-1, GP))
    dc = dcexp.reshape(n, 2, LANES, -1)
    dcre = _block_diag_take(dc[:, 0].reshape(GP, -1), S5_GROUPS).transpose(0, 2, 1)
    dcim = -_block_diag_take(dc[:, 1].reshape(GP, -1), S5_GROUPS).transpose(0, 2, 1)

    def f(lre, lim, ldt, gsel, bre, bim, c1, c2, c3, c4):
        d = _vjp(lambda a, b, c, e, f_: _s5_params(a, b, c, gsel, e, f_), (lre, lim, ldt, bre, bim), (c1, c2, c3, c4))
        return d

    outs = [((GP, 1), F32)] * 2 + [((1, S5_GROUPS), F32)] + [((GP, S5_GROUP), F32)] * 2
    dlre, dlim, dldt, dbre, dbim = whole(f, ins + [dlr.reshape(GP, 1), dli.reshape(GP, 1), dbbr, dbbi], outs,
                                          name=f"s5_params_bwd_{tag}")
    shp = (S5_GROUPS, S5_STATE)
    return dict(lre=dlre.reshape(shp), lim=dlim.reshape(shp), ldt=dldt.reshape(S5_GROUPS),
                bre=dbre.reshape(S5_GROUPS, S5_STATE, S5_GROUP), bim=dbim.reshape(S5_GROUPS, S5_STATE, S5_GROUP),
                cre=dcre, cim=dcim)


def _place():
    return lax.axis_index("x"), lax.axis_index("y"), lax.axis_index("c")


def _other_chips(x, y):
    return [(1 - x, y), (x, 1 - y), (1 - x, 1 - y)]


def _half(ref, h):
    n = ref.shape[0] // 2
    return ref.at[pl.ds(h * n, n)]


def _hbm_specs(n):
    return [pl.BlockSpec(memory_space=pl.ANY)] * n


def gather_chips(ws):
    n = len(ws)

    def body(*refs):
        w_refs, out_refs, (send_sems, recv_sems) = refs[:n], refs[n:2 * n], refs[2 * n:]
        x, y, c = _place()
        me, sibling = (x, y, c), (x, y, 1 - c)
        chips = _other_chips(x, y)
        mine = 2 * x + y

        def copy(k, src, dst, to):
            return pltpu.make_async_remote_copy(src_ref=src, dst_ref=dst, send_sem=send_sems.at[k],
                                                recv_sem=recv_sems.at[k], device_id=to, device_id_type=MESH)

        first, passed = [], []
        for p in range(n):
            for j, chip in enumerate(chips):
                first.append(copy(6 * p + j, _half(w_refs[p], c), _half(out_refs[p].at[mine], c), (*chip, c)))
                first[-1].start()
        for p in range(n):
            for j, chip in enumerate(chips):
                block = out_refs[p].at[2 * chip[0] + chip[1]]
                copy(6 * p + j, _half(w_refs[p], c), _half(block, c), me).wait_recv()
                passed.append(copy(6 * p + 3 + j, _half(block, c), _half(block, c), sibling))
                passed[-1].start()
        for p in range(n):
            for j, chip in enumerate(chips):
                block = out_refs[p].at[2 * chip[0] + chip[1]]
                copy(6 * p + 3 + j, _half(w_refs[p], c), _half(block, 1 - c), me).wait_recv()
        for cp in first + passed:
            cp.wait_send()

    return pl.pallas_call(
        body, out_shape=[jax.ShapeDtypeStruct((N_CHIPS,) + w.shape, w.dtype) for w in ws],
        in_specs=_hbm_specs(n), out_specs=_hbm_specs(n),
        scratch_shapes=[pltpu.SemaphoreType.DMA((6 * n,)), pltpu.SemaphoreType.DMA((6 * n,))],
        name="gather_chips",
    )(*ws)


def sibling_halves(gs):
    n = len(gs)

    def body(*refs):
        g_refs, out_refs, (send_sems, recv_sems) = refs[:n], refs[n:2 * n], refs[2 * n:]
        x, y, c = _place()
        me, sibling = (x, y, c), (x, y, 1 - c)

        def copy(p, k, to):
            return pltpu.make_async_remote_copy(src_ref=_half(g_refs[p].at[k], 1 - c), dst_ref=out_refs[p].at[k],
                                                send_sem=send_sems.at[N_CHIPS * p + k],
                                                recv_sem=recv_sems.at[N_CHIPS * p + k],
                                                device_id=to, device_id_type=MESH)

        cps = [copy(p, k, sibling) for p in range(n) for k in range(N_CHIPS)]
        for cp in cps:
            cp.start()
        for p in range(n):
            for k in range(N_CHIPS):
                copy(p, k, me).wait_recv()
        for cp in cps:
            cp.wait_send()

    return pl.pallas_call(
        body, out_shape=[jax.ShapeDtypeStruct((N_CHIPS, g.shape[1] // 2) + g.shape[2:], g.dtype) for g in gs],
        in_specs=_hbm_specs(n), out_specs=_hbm_specs(n),
        scratch_shapes=[pltpu.SemaphoreType.DMA((N_CHIPS * n,)), pltpu.SemaphoreType.DMA((N_CHIPS * n,))],
        name="sibling_halves",
    )(*gs)


def exchange_chips(ps):
    n = len(ps)

    def body(*refs):
        p_refs, out_refs, (send_sems, recv_sems) = refs[:n], refs[n:2 * n], refs[2 * n:]
        x, y, c = _place()
        me = (x, y, c)
        chips = _other_chips(x, y)

        def copy(p, j, chip, to):
            return pltpu.make_async_remote_copy(src_ref=p_refs[p].at[2 * chip[0] + chip[1]], dst_ref=out_refs[p].at[j],
                                                send_sem=send_sems.at[3 * p + j], recv_sem=recv_sems.at[3 * p + j],
                                                device_id=to, device_id_type=MESH)

        cps = [copy(p, j, chip, (*chip, c)) for p in range(n) for j, chip in enumerate(chips)]
        for cp in cps:
            cp.start()
        for p in range(n):
            for j, chip in enumerate(chips):
                copy(p, j, chip, me).wait_recv()
        for cp in cps:
            cp.wait_send()

    return pl.pallas_call(
        body, out_shape=[jax.ShapeDtypeStruct((3,) + p_.shape[1:], p_.dtype) for p_ in ps],
        in_specs=_hbm_specs(n), out_specs=_hbm_specs(n),
        scratch_shapes=[pltpu.SemaphoreType.DMA((3 * n,)), pltpu.SemaphoreType.DMA((3 * n,))],
        name="exchange_chips",
    )(*ps)


def sibling_join(rs):
    n = len(rs)

    def body(*refs):
        r_refs, out_refs, (send_sems, recv_sems) = refs[:n], refs[n:2 * n], refs[2 * n:]
        x, y, c = _place()

        def copy(p, h, to):
            return pltpu.make_async_remote_copy(src_ref=r_refs[p], dst_ref=_half(out_refs[p], h),
                                                send_sem=send_sems.at[p], recv_sem=recv_sems.at[p],
                                                device_id=to, device_id_type=MESH)

        cps = [copy(p, c, (x, y, 1 - c)) for p in range(n)]
        for cp in cps:
            cp.start()
        for p in range(n):
            copy(p, 1 - c, (x, y, c)).wait_recv()
        for cp in cps:
            cp.wait_send()

    return pl.pallas_call(
        body, out_shape=[jax.ShapeDtypeStruct((2 * r.shape[0],) + r.shape[1:], r.dtype) for r in rs],
        in_specs=_hbm_specs(n), out_specs=_hbm_specs(n),
        scratch_shapes=[pltpu.SemaphoreType.DMA((n,)), pltpu.SemaphoreType.DMA((n,))],
        name="sibling_join",
    )(*rs)


def gather_devices(v, name):
    R = v.shape[0]

    def body(v_ref, out_ref, send_sems, recv_sems, local_sem):
        x, y, c = _place()
        me, sibling = (x, y, c), (x, y, 1 - c)
        chips = _other_chips(x, y)

        def rows(px, py, pc):
            return out_ref.at[pl.ds((4 * px + 2 * py + pc) * R, R), :]

        def copy(k, block, to, src=None):
            return pltpu.make_async_remote_copy(src_ref=rows(*block) if src is None else src, dst_ref=rows(*block),
                                                send_sem=send_sems.at[k], recv_sem=recv_sems.at[k],
                                                device_id=to, device_id_type=MESH)

        mine = pltpu.make_async_copy(v_ref, rows(*me), local_sem)
        mine.start()
        first = [copy(0, me, sibling, src=v_ref)]
        first += [copy(1 + j, me, (*chip, c), src=v_ref) for j, chip in enumerate(chips)]
        for cp in first:
            cp.start()
        passed = [copy(4 + j, (*chip, c), sibling) for j, chip in enumerate(chips)]
        for j, chip in enumerate(chips):
            copy(1 + j, (*chip, c), me).wait_recv()
            passed[j].start()
        copy(0, sibling, me).wait_recv()
        for j, chip in enumerate(chips):
            copy(4 + j, (*chip, 1 - c), me).wait_recv()
        for cp in first + passed:
            cp.wait_send()
        mine.wait()

    return pl.pallas_call(
        body, out_shape=jax.ShapeDtypeStruct((N_DEV * R, LANES), v.dtype),
        in_specs=[pl.BlockSpec(memory_space=pltpu.VMEM)], out_specs=pl.BlockSpec(memory_space=pltpu.VMEM),
        scratch_shapes=[pltpu.SemaphoreType.DMA((7,)), pltpu.SemaphoreType.DMA((7,)), pltpu.SemaphoreType.DMA],
        name=name, compiler_params=_params(),
    )(v)


def _flat_rows(n, mult):
    return -(-n // (LANES * mult)) * mult


def _adam(w, g, m, v):
    m = ADAM_B1 * m + (1.0 - ADAM_B1) * g
    v = ADAM_B2 * v + (1.0 - ADAM_B2) * (g * g)
    m_hat = m / (1.0 - ADAM_B1 ** ADAM_STEP)
    v_hat = v / (1.0 - ADAM_B2 ** ADAM_STEP)
    return -ADAM_LR * (m_hat / (jnp.sqrt(v_hat) + ADAM_EPS) + ADAM_WD * w), m, v


def adam_2d(w, g, m, v, name):
    shape = w.shape
    F = shape[-1]
    if w.ndim == 3 and shape[1] % (2 * SUBLANES) == 0:
        L, R, _ = shape
        tr = R
        for t in (512, 256, 128, 64, 32, 16):
            if R % t == 0 and 7 * t * max(F, LANES) * 4 <= ROW_TILE_BYTES:
                tr = t
                break

        def body(w_ref, g_ref, m_ref, v_ref, d_ref, m2_ref, v2_ref):
            d_ref[...], m2_ref[...], v2_ref[...] = _adam(w_ref[...], g_ref[...], m_ref[...], v_ref[...])

        spec = pl.BlockSpec((None, tr, F), lambda l, i: (l, i, 0))
        return pl.pallas_call(
            body, grid=(L, R // tr), in_specs=[spec] * 4, out_specs=[spec] * 3,
            out_shape=[jax.ShapeDtypeStruct(shape, F32)] * 3, name=name, compiler_params=_params(("parallel", "parallel")),
        )(w, g, m, v)
    a = [t.reshape(-1, F) for t in (w, g, m, v)]
    d, m2, v2 = rowwise(lambda w_, g_, m_, v_: (list(_adam(w_, g_, m_, v_)), []), a, outs=[(F, F32)] * 3, name=name)
    return d.reshape(shape), m2.reshape(shape), v2.reshape(shape)


WEIGHTS = ['ffn1_norm', 'ffn1_wg', 'ffn1_wu', 'ffn1_wd', 'mix_norm', 'ffn2_norm', 'ffn2_wg', 'ffn2_wu', 'ffn2_wd',
           'ple_w', 'ple_norm', 'ple_gate_norm', 'ple_gate_w', 'ev_w_in', 'lru_conv_w', 'lru_conv_b', 'lru_wa',
           'lru_ba', 'lru_wx', 'lru_bx', 'lru_lambda', 'fox_bf', 'fox_q_norm', 'fox_k_norm', 'ev_w_out', 'od_w_in',
           'swa_q_norm', 'swa_k_norm', 'swa_sinks', 's5_lambda_re', 's5_lambda_im', 's5_log_dt', 's5_b_re',
           's5_b_im', 's5_c_re', 's5_c_im', 's5_d', 's5_glu_w', 's5_glu_b', 'od_w_out']
SHARD_AXIS = {'ffn1_wg': 2, 'ffn1_wu': 2, 'ffn1_wd': 1, 'ffn2_wg': 2, 'ffn2_wu': 2, 'ffn2_wd': 1, 'ple_w': 2,
              'ple_gate_w': 1, 'ev_w_in': 2, 'lru_conv_w': 2, 'ev_w_out': 1, 'od_w_in': 2, 's5_d': 1,
              's5_glu_w': 1, 's5_glu_b': 1, 'od_w_out': 1}
EXACT_SHARDED = ('lru_conv_w', 's5_d', 's5_glu_b')
ADAM_TRANSPOSED = ('ffn1_wg', 'ffn1_wu', 'ffn2_wg', 'ffn2_wu', 'od_w_in')
SHARDED = [n for n in WEIGHTS if n in SHARD_AXIS]
REPLICATED = [n for n in WEIGHTS if n not in SHARD_AXIS]


GROUPS = {
    'wgu': ['ffn1_wg', 'ffn1_wu', 'ffn2_wg', 'ffn2_wu'],
    'wd': ['ffn1_wd', 'ffn2_wd'],
    'w_rows': ['ple_gate_w', 'ev_w_out', 'od_w_out'],
    'ple_w': ['ple_w'], 'ev_w_in': ['ev_w_in'], 'od_w_in': ['od_w_in'], 's5_glu_w': ['s5_glu_w'],
}
REDUCED_GROUPS = list(GROUPS)


def _chip():
    return 2 * lax.axis_index("x") + lax.axis_index("y")


def gather_weights(shards):
    own = {k: jnp.concatenate([shards[n] for n in names], axis=0).astype(BF16) for k, names in GROUPS.items()}
    own['exact'] = jnp.concatenate([shards['lru_conv_w'], shards['s5_d'][:, None], shards['s5_glu_b'][:, None]], axis=1)
    keys = list(own)
    got = gather_chips([own[k] for k in keys])
    return {k: lax.dynamic_update_index_in_dim(g, own[k], _chip(), 0) for k, g in zip(keys, got)}


def _rows_by_chip(w):
    return w.reshape(w.shape[0] * w.shape[1], w.shape[2])


def _cols_by_chip(w):
    return w.transpose(1, 0, 2).reshape(w.shape[1], w.shape[0] * w.shape[2])


def _chip_rows(g):
    return g.reshape(N_CHIPS, g.shape[0] // N_CHIPS, g.shape[1])


def _chip_cols(g):
    return g.reshape(g.shape[0], N_CHIPS, g.shape[1] // N_CHIPS).transpose(1, 0, 2)


def full_weights(gw, depth):
    n_ev = (depth + 1) // 2
    ex = gw['exact']
    return dict(
        ple_gate_w=[_rows_by_chip(gw['w_rows'][:, l]) for l in range(depth)],
        ev_w_out=[_rows_by_chip(gw['w_rows'][:, depth + j]) for j in range(n_ev)],
        od_w_out=[_rows_by_chip(gw['w_rows'][:, depth + n_ev + j]) for j in range(depth // 2)],
        ple_w=[_cols_by_chip(gw['ple_w'][:, l]) for l in range(depth)],
        ev_w_in=[_cols_by_chip(gw['ev_w_in'][:, j]) for j in range(n_ev)],
        od_w_in=[_cols_by_chip(gw['od_w_in'][:, j]) for j in range(depth // 2)],
        s5_glu_w=[_rows_by_chip(gw['s5_glu_w'][:, j]) for j in range(depth // 2)],
        lru_conv_w=[_cols_by_chip(ex[:, j, 0:LRU_CONV]) for j in range(n_ev)],
        s5_d=[ex[:, j, LRU_CONV].reshape(-1) for j in range(depth // 2)],
        s5_glu_b=[ex[:, j, LRU_CONV + 1].reshape(-1) for j in range(depth // 2)],
    )


def _add_tile(rows, width):
    for t in (1024, 512, 256, 128, 64, 32, 16):
        if rows % t == 0 and 3 * t * width * 4 <= ROW_TILE_BYTES:
            return t
    return rows


def pair_add(g, t, c, name):
    C, F = g.shape[0], g.shape[-1]
    rows = math.prod(t.shape[1:-1])
    tr = _add_tile(rows, F)
    nb = rows // tr

    def body(c_ref, g_ref, t_ref, o_ref):
        o_ref[...] = (g_ref[...].astype(F32) + t_ref[...].astype(F32)).astype(o_ref.dtype)

    spec = pl.BlockSpec((None, tr, F), lambda k, i, c_ref: (k, i, 0))
    out = pl.pallas_call(
        body, out_shape=jax.ShapeDtypeStruct((C, rows, F), BF16),
        grid_spec=pltpu.PrefetchScalarGridSpec(
            num_scalar_prefetch=1, grid=(C, nb),
            in_specs=[pl.BlockSpec((None, tr, F), lambda k, i, c_ref: (k, c_ref[0] * nb + i, 0)), spec],
            out_specs=spec),
        name=name, compiler_params=_params(("parallel", "parallel")),
    )(c.reshape(1).astype(jnp.int32), g.reshape(C, 2 * rows, F), t.reshape(C, rows, F))
    return out.reshape(t.shape)


def chips_add(p, xs, chip, name):
    F = p.shape[-1]
    rows = math.prod(p.shape[1:-1])
    tr = _add_tile(rows, F)

    def body(m_ref, p_ref, a_ref, b_ref, d_ref, o_ref):
        o_ref[...] = ((p_ref[...].astype(F32) + a_ref[...].astype(F32))
                      + (b_ref[...].astype(F32) + d_ref[...].astype(F32)))

    def other(j):
        return pl.BlockSpec((None, tr, F), lambda i, m_ref: (j, i, 0))

    x3 = xs.reshape(3, rows, F)
    out = pl.pallas_call(
        body, out_shape=jax.ShapeDtypeStruct((rows, F), F32),
        grid_spec=pltpu.PrefetchScalarGridSpec(
            num_scalar_prefetch=1, grid=(rows // tr,),
            in_specs=[pl.BlockSpec((None, tr, F), lambda i, m_ref: (m_ref[0], i, 0)), other(0), other(1), other(2)],
            out_specs=pl.BlockSpec((tr, F), lambda i, m_ref: (i, 0))),
        name=name, compiler_params=_params(("parallel",)),
    )(chip.reshape(1).astype(jnp.int32), p.reshape(N_CHIPS, rows, F), x3, x3, x3)
    return out.reshape(p.shape[1:])


def reduce_sharded(groups):
    keys = list(groups)
    c = lax.axis_index("c")
    gs = [groups[k] for k in keys]
    theirs = sibling_halves(gs)
    pairs = [pair_add(g, t, c, f"pair_add_{k}") for k, g, t in zip(keys, gs, theirs)]
    got = exchange_chips(pairs)
    halves = [chips_add(p_, x_, _chip(), f"chips_add_{k}") for k, p_, x_ in zip(keys, pairs, got)]
    joined = sibling_join(halves)
    out = {}
    for k, h, j in zip(keys, halves, joined):
        out[k] = lax.dynamic_update_slice_in_dim(j, h, c * h.shape[0], axis=0)
    return out


SMALL_GRADS = REPLICATED + list(EXACT_SHARDED)


def _flatten_small(tensors, shapes):
    parts = [tensors[n].astype(F32).reshape(-1) if n in tensors else jnp.zeros((math.prod(shapes[n]),), F32)
             for n in SMALL_GRADS]
    flat = jnp.concatenate(parts)
    rows = _flat_rows(flat.shape[0], SUBLANES)
    return jnp.pad(flat, (0, rows * LANES - flat.shape[0])).reshape(rows, LANES)


def _unflatten_small(flat, shapes):
    flat = flat.reshape(-1)
    out, off = {}, 0
    for n in SMALL_GRADS:
        size = math.prod(shapes[n])
        out[n] = flat[off:off + size].reshape(shapes[n])
        off += size
    return out


def grad_groups(gwgu, gwd, G):
    def st(xs):
        return jnp.stack(xs, axis=1).astype(BF16)

    return {
        'wgu': gwgu, 'wd': gwd,
        'w_rows': st([_chip_rows(g) for n in GROUPS['w_rows'] for g in G[n]]),
        'ple_w': st([_chip_cols(g) for g in G['ple_w']]),
        'ev_w_in': st([_chip_cols(g) for g in G['ev_w_in']]),
        'od_w_in': st([_chip_cols(g) for g in G['od_w_in']]),
        's5_glu_w': st([_chip_rows(g) for g in G['s5_glu_w']]),
    }


def ungroup(red, shapes):
    out = {}
    for k, names in GROUPS.items():
        off = 0
        for n in names:
            out[n] = red[k][off:off + shapes[n][0]]
            off += shapes[n][0]
    return out


def _layer_weights(full, small, i, depth):
    j = i // 2
    w = dict(
        g1=small['ffn1_norm'][i][None], gm=small['mix_norm'][i][None], g2=small['ffn2_norm'][i][None],
        gp=small['ple_norm'][i][None], gg=small['ple_gate_norm'][i][None],
        ffn1=(i, depth + i, i), ffn2=(2 * depth + i, 3 * depth + i, depth + i),
        ple_w=full['ple_w'][i], ple_gate_w=full['ple_gate_w'][i],
    )
    if i % 2 == 0:
        w_in = full['ev_w_in'][j]
        w['mix'] = dict(
            w_in=jnp.pad(w_in, ((0, 0), (0, 2688 - w_in.shape[1]))), w_out=full['ev_w_out'][j],
            conv_w=full['lru_conv_w'][j].astype(F32), conv_b=small['lru_conv_b'][j][None],
            w_ax=jnp.concatenate([_block_diag(small['lru_wa'][j]), _block_diag(small['lru_wx'][j])],
                                 axis=1).astype(BF16),
            ba=small['lru_ba'][j][None], bx=small['lru_bx'][j][None], lam=small['lru_lambda'][j][None],
            bf=jnp.pad(small['fox_bf'][j], (0, LANES - 8))[None], qn=small['fox_q_norm'][j],
            kn=small['fox_k_norm'][j])
    else:
        lam, bexp, cexp, ins = s5_prep_fwd(small['s5_lambda_re'][j], small['s5_lambda_im'][j], small['s5_log_dt'][j],
                                           small['s5_b_re'][j], small['s5_b_im'][j], small['s5_c_re'][j],
                                           small['s5_c_im'][j], f"L{i}")
        w['mix'] = dict(
            w_in=full['od_w_in'][j], w_out=full['od_w_out'][j], qn=small['swa_q_norm'][j], kn=small['swa_k_norm'][j],
            sinks=small['swa_sinks'][j], s5_lam=lam, s5_bexp=bexp, s5_cexp=cexp, s5_ins=ins,
            s5_d=full['s5_d'][j].astype(F32)[None], glu_w=full['s5_glu_w'][j], glu_b=full['s5_glu_b'][j].astype(F32)[None])
    return w


def layer_fwd(x0, n1, p_i, w, ffnw, next_g1, i):
    tag = f"L{i}"
    sv = {}
    wgu, wd = ffnw
    x1, hm, sv['ffn1'] = ffn_fwd(n1, x0, w['gm'], wgu, wd, *w['ffn1'], f"1_{tag}")
    if i % 2 == 0:
        mo, sv['mix'] = even_mixer_fwd(hm, w['mix'], tag)
    else:
        mo, sv['mix'] = odd_mixer_fwd(hm, w['mix'], tag)
    x2, n2 = mm_add_norm(mo, w['mix']['w_out'], x1, w['g2'], f"mix_out_{tag}")
    x3, ng, sv['ffn2'] = ffn_fwd(n2, x2, w['gg'], wgu, wd, *w['ffn2'], f"2_{tag}")
    gpre = mm(ng, w['ple_gate_w'], name=f"ple_gate_{tag}")
    epre = mm(p_i, w['ple_w'], name=f"ple_emb_{tag}")
    D = x0.shape[1]
    if next_g1 is None:
        x4 = rowwise(lambda a, b, c, pn: ([_ple_out(a, b, c, pn)], []), [x3, gpre, epre], [w['gp']],
                     outs=[(D, F32)], name=f"ple_out_{tag}")[0]
        n_next = None
    else:
        def f(a, b, c, pn, gn):
            y = _ple_out(a, b, c, pn)
            return [y, _rms(y, gn)], []

        x4, n_next = rowwise(f, [x3, gpre, epre], [w['gp'], next_g1], outs=[(D, F32), (D, BF16)],
                             name=f"ple_out_{tag}")
    sv.update(x0=x0, x1=x1, x2=x2, x3=x3, ng=ng, gpre=gpre, epre=epre, p=p_i)
    return x4, n_next, sv


def layer_bwd(dx4, sv, w, ffnw, gbuf, i):
    tag = f"L{i}"
    D = dx4.shape[1]
    g = {}
    wgu, wd = ffnw

    def f_ple(a, b, c, d, pn):
        da, db, dc, dpn = _vjp(_ple_out, (a, b, c, pn), d)
        return [db, dc], [dpn]

    dgpre, depre, dgp = rowwise(f_ple, [sv['x3'], sv['gpre'], sv['epre'], dx4], [w['gp']],
                                outs=[(D, BF16), (D, BF16)], accs=[(1, D)], name=f"ple_out_bwd_{tag}")
    g['gp'] = dgp[0]
    g['ple_w'] = mm(sv['p'], depre, "tn", name=f"ple_emb_dw_{tag}")
    g['ple_gate_w'] = mm(sv['ng'], dgpre, "tn", name=f"ple_gate_dw_{tag}")
    dx3, dgg = mm_norm_bwd(dgpre, w['ple_gate_w'], sv['x3'], w['gg'], dx4, f"ple_gate_dx_{tag}")
    g['gg'] = dgg[0]
    dx2, dg2, gbuf = ffn_bwd(dx3, sv['ffn2'], sv['x2'], w['g2'], wgu, wd, *w['ffn2'], gbuf, f"2_{tag}")
    g['g2'] = dg2[0]
    if i % 2 == 0:
        dz, g['mix'] = even_mixer_bwd(dx2, sv['mix'], w['mix'], tag)
    else:
        dz, g['mix'] = odd_mixer_bwd(dx2, sv['mix'], w['mix'], tag)
    dx1, dgm = mm_norm_bwd(dz, w['mix']['w_in'], sv['x1'], w['gm'], dx2, f"mix_dh_{tag}")
    g['gm'] = dgm[0]
    dx0, dg1, gbuf = ffn_bwd(dx1, sv['ffn1'], sv['x0'], w['g1'], wgu, wd, *w['ffn1'], gbuf, f"1_{tag}")
    g['g1'] = dg1[0]
    return dx0, g, gbuf


def _collect_grads(layer_grads, depth):
    st = lambda xs: jnp.stack(xs)
    G = {}
    L = layer_grads
    G['ffn1_norm'] = st([g['g1'] for g in L])
    G['mix_norm'] = st([g['gm'] for g in L])
    G['ffn2_norm'] = st([g['g2'] for g in L])
    G['ple_norm'] = st([g['gp'] for g in L])
    G['ple_gate_norm'] = st([g['gg'] for g in L])
    G['ple_w'] = st([g['ple_w'] for g in L])
    G['ple_gate_w'] = st([g['ple_gate_w'] for g in L])
    ev = [L[i]['mix'] for i in range(0, depth, 2)]
    od = [L[i]['mix'] for i in range(1, depth, 2)]
    G['ev_w_in'] = st([m['w_in'][:, :2568] for m in ev])
    G['ev_w_out'] = st([m['w_out'] for m in ev])
    G['lru_conv_w'] = st([m['conv_w'] for m in ev])
    G['lru_conv_b'] = st([m['conv_b'] for m in ev])
    G['lru_wa'] = st([_block_diag_take(m['w_ax'][:, :512], LRU_BLOCKS) for m in ev])
    G['lru_wx'] = st([_block_diag_take(m['w_ax'][:, 512:], LRU_BLOCKS) for m in ev])
    G['lru_ba'] = st([m['ba'] for m in ev])
    G['lru_bx'] = st([m['bx'] for m in ev])
    G['lru_lambda'] = st([m['lam'] for m in ev])
    G['fox_bf'] = st([m['bf'] for m in ev])
    G['fox_q_norm'] = st([m['qn'] for m in ev])
    G['fox_k_norm'] = st([m['kn'] for m in ev])
    G['od_w_in'] = st([m['w_in'] for m in od])
    G['od_w_out'] = st([m['w_out'] for m in od])
    G['swa_q_norm'] = st([m['qn'] for m in od])
    G['swa_k_norm'] = st([m['kn'] for m in od])
    G['swa_sinks'] = st([m['sinks'] for m in od])
    G['s5_lambda_re'] = st([m['s5']['lre'] for m in od])
    G['s5_lambda_im'] = st([m['s5']['lim'] for m in od])
    G['s5_log_dt'] = st([m['s5']['ldt'] for m in od])
    G['s5_b_re'] = st([m['s5']['bre'] for m in od])
    G['s5_b_im'] = st([m['s5']['bim'] for m in od])
    G['s5_c_re'] = st([m['s5']['cre'] for m in od])
    G['s5_c_im'] = st([m['s5']['cim'] for m in od])
    G['s5_d'] = st([m['s5_d'] for m in od])
    G['s5_glu_w'] = st([m['glu_w'] for m in od])
    G['s5_glu_b'] = st([m['glu_b'] for m in od])
    return G


def local_step(x, p, target, ffnw, full, small):
    depth = p.shape[0]
    S, D = x.shape
    ws = [_layer_weights(full, small, i, depth) for i in range(depth)]
    saved = []
    xi, ni = add_norm(x, None, ws[0]['g1'], "norm1_L0")
    for i in range(depth):
        xi, ni, sv = layer_fwd(xi, ni, p[i], ws[i], ffnw, ws[i + 1]['g1'] if i + 1 < depth else None, i)
        saved.append(sv)

    def f_loss(y, t):
        e = y - t
        return [e * (1.0 / D)], [0.5 * jnp.sum(jnp.mean(e * e, axis=-1, keepdims=True), axis=0, keepdims=True)]

    dx, loss = rowwise(f_loss, [xi, target], outs=[(D, F32)], accs=[(1, 1)], name="loss")
    grads = [None] * depth
    gbuf = (None, None)
    for i in reversed(range(depth)):
        dx, grads[i], gbuf = layer_bwd(dx, saved[i], ws[i], ffnw, gbuf, i)
        if i % 2 == 1:
            m = grads[i]['mix']
            m['s5'] = s5_prep_bwd(ws[i]['mix']['s5_ins'], m['s5_lam'], m['s5_bexp'], m['s5_cexp'], f"L{i}")
    return loss[0, 0], dx, gbuf, _collect_grads(grads, depth)


def kernel(x, p, ffn1_norm, ffn1_wg, ffn1_wu, ffn1_wd, mix_norm, ffn2_norm, ffn2_wg, ffn2_wu, ffn2_wd, ple_w, ple_norm, ple_gate_norm, ple_gate_w, ev_w_in, lru_conv_w, lru_conv_b, lru_wa, lru_ba, lru_wx, lru_bx, lru_lambda, fox_bf, fox_q_norm, fox_k_norm, ev_w_out, od_w_in, swa_q_norm, swa_k_norm, swa_sinks, s5_lambda_re, s5_lambda_im, s5_log_dt, s5_b_re, s5_b_im, s5_c_re, s5_c_im, s5_d, s5_glu_w, s5_glu_b, od_w_out, loss_target, m_ffn1_norm, m_ffn1_wg, m_ffn1_wu, m_ffn1_wd, m_mix_norm, m_ffn2_norm, m_ffn2_wg, m_ffn2_wu, m_ffn2_wd, m_ple_w, m_ple_norm, m_ple_gate_norm, m_ple_gate_w, m_ev_w_in, m_lru_conv_w, m_lru_conv_b, m_lru_wa, m_lru_ba, m_lru_wx, m_lru_bx, m_lru_lambda, m_fox_bf, m_fox_q_norm, m_fox_k_norm, m_ev_w_out, m_od_w_in, m_swa_q_norm, m_swa_k_norm, m_swa_sinks, m_s5_lambda_re, m_s5_lambda_im, m_s5_log_dt, m_s5_b_re, m_s5_b_im, m_s5_c_re, m_s5_c_im, m_s5_d, m_s5_glu_w, m_s5_glu_b, m_od_w_out, v_ffn1_norm, v_ffn1_wg, v_ffn1_wu, v_ffn1_wd, v_mix_norm, v_ffn2_norm, v_ffn2_wg, v_ffn2_wu, v_ffn2_wd, v_ple_w, v_ple_norm, v_ple_gate_norm, v_ple_gate_w, v_ev_w_in, v_lru_conv_w, v_lru_conv_b, v_lru_wa, v_lru_ba, v_lru_wx, v_lru_bx, v_lru_lambda, v_fox_bf, v_fox_q_norm, v_fox_k_norm, v_ev_w_out, v_od_w_in, v_swa_q_norm, v_swa_k_norm, v_swa_sinks, v_s5_lambda_re, v_s5_lambda_im, v_s5_log_dt, v_s5_b_re, v_s5_b_im, v_s5_c_re, v_s5_c_im, v_s5_d, v_s5_glu_w, v_s5_glu_b, v_od_w_out):
    args = locals()
    wts = {n: args[n] for n in WEIGHTS}
    ms = {n: args["m_" + n] for n in WEIGHTS}
    vs = {n: args["v_" + n] for n in WEIGHTS}
    shapes = {n: wts[n].shape for n in WEIGHTS}

    depth = p.shape[0]
    gw = gather_weights({n: wts[n] for n in SHARDED})
    small = {n: wts[n] for n in REPLICATED}
    loss, dx, (gwgu, gwd), G = local_step(x[0], p[:, 0], loss_target[0], (gw['wgu'], gw['wd']),
                                          full_weights(gw, depth), small)
    loss = lax.psum(loss, ("x", "y", "c"))

    gsh = ungroup(reduce_sharded(grad_groups(gwgu, gwd, G)), shapes)
    full_shapes = {n: (G[n].shape if n in EXACT_SHARDED else shapes[n]) for n in SMALL_GRADS}
    flat_g = _flatten_small(G, full_shapes)
    g8 = gather_devices(flat_g, "gather_small_grads").reshape((N_DEV,) + flat_g.shape)
    wf, mf, vf = (_flatten_small({n: t[n] for n in REPLICATED}, full_shapes) for t in (wts, ms, vs))

    def f_small(g0, g1, g2, g3, g4, g5, g6, g7, w_, m_, v_):
        gsum = ((g0 + g1) + (g2 + g3)) + ((g4 + g5) + (g6 + g7))
        return [gsum] + list(_adam(w_, gsum, m_, v_)), []

    gs_f, ds_f, ms_f, vs_f = rowwise(f_small, [g8[d] for d in range(N_DEV)] + [wf, mf, vf],
                                     outs=[(LANES, F32)] * 4, name="adam_small")
    out_g, out_d, out_m, out_v = {}, {}, {}, {}
    for dst, flat in ((out_g, gs_f), (out_d, ds_f), (out_m, ms_f), (out_v, vs_f)):
        dst.update(_unflatten_small(flat, full_shapes))
    for n in EXACT_SHARDED:
        width = shapes[n][SHARD_AXIS[n]]
        gsh[n] = lax.dynamic_slice_in_dim(out_g[n], _chip() * width, width, axis=SHARD_AXIS[n])
    for n in SHARDED:
        out_g[n] = gsh[n]
        if n in ADAM_TRANSPOSED:
            def t(a):
                return a.transpose(0, 2, 1)
            d_, m_, v_ = adam_2d(t(wts[n]), t(gsh[n]), t(ms[n]), t(vs[n]), f"adam_{n}")
            out_d[n], out_m[n], out_v[n] = t(d_), t(m_), t(v_)
        else:
            out_d[n], out_m[n], out_v[n] = adam_2d(wts[n], gsh[n], ms[n], vs[n], f"adam_{n}")
    return (loss, dx[None], *[out_g[n] for n in WEIGHTS], *[out_d[n] for n in WEIGHTS],
            *[out_m[n] for n in WEIGHTS], *[out_v[n] for n in WEIGHTS])
```

```python
import functools
import math

import jax
import jax.numpy as jnp
from jax import lax
from jax.experimental import pallas as pl
from jax.experimental.pallas import tpu as pltpu

F32 = jnp.float32
BF16 = jnp.bfloat16
MXU_DTYPE = BF16
HI = lax.Precision.HIGHEST
MESH = pl.DeviceIdType.MESH

VMEM_LIMIT_BYTES = 56 * 1024 * 1024
ROW_TILE_BYTES = 12 * 1024 * 1024
MM_VMEM_BYTES = 40 * 1024 * 1024
MM_TILE_M = 1024
MM_TILE_N = 1408
FLAT_W = 2048
LANES = 128
SUBLANES = 8

HEAD_DIM = 64
LRU_BLOCKS = 8
LRU_CONV = 4
LRU_C = 8.0
SWA_WINDOW = 128
SWA_GROUP = 4
S5_GROUP = 16
S5_GROUPS = 32
S5_STATE = 64
S5_BLOCKS = 4
ROPE_THETA = 10000.0
EPS = 1e-6
MACARON = 0.5
NEG = -1e30

ADAM_LR = 0.001
ADAM_B1 = 0.9
ADAM_B2 = 0.999
ADAM_EPS = 1e-08
ADAM_WD = 0.01
ADAM_STEP = 10

N_CHIPS = 4
N_DEV = 8


def _pick(n, cands):
    for c in cands:
        if n % c == 0:
            return c
    return n


def _tile(n, cap, unit):
    best = None
    for t in range(unit, min(n, cap) + 1, unit):
        if n % t == 0:
            best = t
    return n if best is None else best


def _params(sem=None):
    return pltpu.CompilerParams(dimension_semantics=sem, vmem_limit_bytes=VMEM_LIMIT_BYTES)


def rowwise(fn, rows, consts=(), outs=(), accs=(), name="rowwise", periods=None):
    rows, consts = list(rows), list(consts)
    n_r, n_c, n_o, n_a = len(rows), len(consts), len(outs), len(accs)
    R = rows[0].shape[0]
    periods = list(periods) if periods is not None else [None] * n_r
    per_row = sum(max(r.shape[1], LANES) * 4 for r in rows) + sum(max(f, LANES) * 4 for f, _ in outs)
    limit = min([R] + [p for p in periods if p is not None])
    tr = limit
    for c in (1024, 512, 256, 128, 64, 32, 16):
        if c <= limit and limit % c == 0 and R % c == 0 and c * per_row <= ROW_TILE_BYTES:
            tr = c
            break

    def row_map(period):
        if period is None:
            return lambda i: (i, 0)
        nb = period // tr
        return lambda i: (i % nb, 0)

    in_specs = [pl.BlockSpec((tr, r.shape[1]), row_map(p)) for r, p in zip(rows, periods)]
    in_specs += [pl.BlockSpec(c.shape, lambda i: (0, 0)) for c in consts]
    out_shape = [jax.ShapeDtypeStruct((R, f), dt) for f, dt in outs]
    out_shape += [jax.ShapeDtypeStruct(tuple(s), F32) for s in accs]
    out_specs = [pl.BlockSpec((tr, f), lambda i: (i, 0)) for f, _ in outs]
    out_specs += [pl.BlockSpec(tuple(s), lambda i: (0, 0)) for s in accs]

    def body(*refs):
        ins = [r[...] for r in refs[:n_r + n_c]]
        o_refs = refs[n_r + n_c:n_r + n_c + n_o]
        a_refs = refs[n_r + n_c + n_o:]
        ro, ra = fn(*ins)
        for ref, val in zip(o_refs, ro):
            ref[...] = val.astype(ref.dtype)
        if n_a:
            @pl.when(pl.program_id(0) == 0)
            def _():
                for ref in a_refs:
                    ref[...] = jnp.zeros(ref.shape, ref.dtype)
            for ref, val in zip(a_refs, ra):
                ref[...] += val.astype(F32)

    res = pl.pallas_call(
        body, grid=(R // tr,), in_specs=in_specs, out_specs=out_specs, out_shape=out_shape,
        name=name, compiler_params=_params(("arbitrary",)),
    )(*rows, *consts)
    return list(res)


def whole(fn, ins, outs, name="whole"):
    n_i = len(ins)

    def body(*refs):
        vals = fn(*[r[...] for r in refs[:n_i]])
        for ref, val in zip(refs[n_i:], vals):
            ref[...] = val.astype(ref.dtype)

    res = pl.pallas_call(
        body, out_shape=[jax.ShapeDtypeStruct(tuple(s), dt) for s, dt in outs],
        in_specs=[pl.BlockSpec(memory_space=pltpu.VMEM)] * n_i,
        out_specs=[pl.BlockSpec(memory_space=pltpu.VMEM)] * len(outs),
        name=name, compiler_params=_params(),
    )(*ins)
    return list(res)


_DOT_DIMS = {
    "nn": (((1,), (0,)), ((), ())),
    "nt": (((1,), (1,)), ((), ())),
    "tn": (((0,), (0,)), ((), ())),
}


def mm(a, b, mode="nn", out_dtype=F32, name="mm"):
    if mode == "nn":
        (M, K), (K2, N) = a.shape, b.shape
    elif mode == "nt":
        (M, K), (N, K2) = a.shape, b.shape
    else:
        (K, M), (K2, N) = a.shape, b.shape
    assert K == K2, (mode, a.shape, b.shape)
    tn = _tile(N, MM_TILE_N, LANES)
    if mode == "tn":
        tm, tk = _tile(M, MM_TILE_M, LANES), _tile(K, MM_TILE_M, 2 * SUBLANES)
    else:
        tm, tk = _tile(M, MM_TILE_M, 2 * SUBLANES), _tile(K, MM_TILE_N, LANES)

    def vmem_bytes(tm_, tk_):
        return (2 * (tm_ * tk_ * a.dtype.itemsize + tk_ * tn * b.dtype.itemsize
                     + tm_ * tn * jnp.dtype(out_dtype).itemsize) + tm_ * tn * 4)

    while vmem_bytes(tm, tk) > MM_VMEM_BYTES and tk % (2 * LANES) == 0 and K % (tk // 2) == 0:
        tk //= 2
    while vmem_bytes(tm, tk) > MM_VMEM_BYTES and tm % (2 * LANES) == 0 and M % (tm // 2) == 0:
        tm //= 2
    if mode == "tn":
        a_spec = pl.BlockSpec((tk, tm), lambda i, j, k: (k, i))
    else:
        a_spec = pl.BlockSpec((tm, tk), lambda i, j, k: (i, k))
    if mode == "nt":
        b_spec = pl.BlockSpec((tn, tk), lambda i, j, k: (j, k))
    else:
        b_spec = pl.BlockSpec((tk, tn), lambda i, j, k: (k, j))
    nk = K // tk
    dims = _DOT_DIMS[mode]

    def dot(a_ref, b_ref):
        return lax.dot_general(a_ref[...].astype(MXU_DTYPE), b_ref[...].astype(MXU_DTYPE), dims,
                               preferred_element_type=F32)

    def body_one(a_ref, b_ref, o_ref):
        o_ref[...] = dot(a_ref, b_ref).astype(o_ref.dtype)

    def body_acc(a_ref, b_ref, o_ref, acc_ref):
        k = pl.program_id(2)

        @pl.when(k == 0)
        def _():
            acc_ref[...] = dot(a_ref, b_ref)

        @pl.when(k > 0)
        def _():
            acc_ref[...] += dot(a_ref, b_ref)

        @pl.when(k == nk - 1)
        def _():
            o_ref[...] = acc_ref[...].astype(o_ref.dtype)

    return pl.pallas_call(
        body_one if nk == 1 else body_acc, grid=(M // tm, N // tn, nk), in_specs=[a_spec, b_spec],
        out_specs=pl.BlockSpec((tm, tn), lambda i, j, k: (i, j)),
        out_shape=jax.ShapeDtypeStruct((M, N), out_dtype),
        scratch_shapes=[] if nk == 1 else [pltpu.VMEM((tm, tn), F32)],
        name=name, compiler_params=_params(("parallel", "parallel", "arbitrary")),
    )(a, b)


def mm_blocks(a, b, mode="nn", name="mm_blocks"):
    M = a.shape[0]
    nb = b.shape[0]
    Ka, Nb = (b.shape[1], b.shape[2]) if mode == "nn" else (b.shape[2], b.shape[1])
    tm = _tile(M, MM_TILE_M, 2 * SUBLANES)

    def body(a_ref, b_ref, o_ref):
        o_ref[...] = _dotf(a_ref[...], b_ref[...], mode)

    return pl.pallas_call(
        body, grid=(M // tm, nb),
        in_specs=[pl.BlockSpec((tm, Ka), lambda i, j: (i, j)),
                  pl.BlockSpec((None,) + b.shape[1:], lambda i, j: (j, 0, 0))],
        out_specs=pl.BlockSpec((tm, Nb), lambda i, j: (i, j)), out_shape=jax.ShapeDtypeStruct((M, nb * Nb), F32),
        name=name, compiler_params=_params(("parallel", "parallel")),
    )(a, b)


def mm_blocks_tn(a, d, nb, name="mm_blocks_tn"):
    K = a.shape[0]
    Ma, Nd = a.shape[1] // nb, d.shape[1] // nb
    tk = _tile(K, MM_TILE_M, 2 * SUBLANES)

    def body(a_ref, d_ref, o_ref):
        k = pl.program_id(1)
        r = _dotf(a_ref[...], d_ref[...], "tn")

        @pl.when(k == 0)
        def _():
            o_ref[...] = r

        @pl.when(k > 0)
        def _():
            o_ref[...] += r

    return pl.pallas_call(
        body, grid=(nb, K // tk),
        in_specs=[pl.BlockSpec((tk, Ma), lambda j, k: (k, j)), pl.BlockSpec((tk, Nd), lambda j, k: (k, j))],
        out_specs=pl.BlockSpec((None, Ma, Nd), lambda j, k: (j, 0, 0)),
        out_shape=jax.ShapeDtypeStruct((nb, Ma, Nd), F32), name=name,
        compiler_params=_params(("parallel", "arbitrary")),
    )(a, d)


def _mm_rows_tiles(M, K):
    return _tile(M, MM_TILE_M // 2, 2 * SUBLANES), _tile(K, MM_TILE_N, LANES)


def mm_add_norm(a, b, x, gain, name):
    (M, K), N = a.shape, b.shape[1]
    tm, tk = _mm_rows_tiles(M, K)
    nk = K // tk

    def body(a_ref, b_ref, x_ref, g_ref, xo_ref, n_ref, acc_ref):
        k = pl.program_id(1)
        r = _dotf(a_ref[...], b_ref[...])

        @pl.when(k == 0)
        def _():
            acc_ref[...] = r

        @pl.when(k > 0)
        def _():
            acc_ref[...] += r

        @pl.when(k == nk - 1)
        def _():
            xn = x_ref[...] + acc_ref[...]
            xo_ref[...] = xn
            n_ref[...] = _rms(xn, g_ref[...]).astype(n_ref.dtype)

    row = pl.BlockSpec((tm, N), lambda i, k: (i, 0))
    return pl.pallas_call(
        body, grid=(M // tm, nk),
        in_specs=[pl.BlockSpec((tm, tk), lambda i, k: (i, k)), pl.BlockSpec((tk, N), lambda i, k: (k, 0)), row,
                  pl.BlockSpec((1, N), lambda i, k: (0, 0))],
        out_specs=[row, row], out_shape=[jax.ShapeDtypeStruct((M, N), F32), jax.ShapeDtypeStruct((M, N), BF16)],
        scratch_shapes=[pltpu.VMEM((tm, N), F32)], name=name, compiler_params=_params(("parallel", "arbitrary")),
    )(a, b, x, gain)


def mm_norm_bwd(a, b, x, gain, dx_res, name):
    (M, K), N = a.shape, b.shape[0]
    tm, tk = _mm_rows_tiles(M, K)
    nk = K // tk

    def body(a_ref, b_ref, x_ref, g_ref, r_ref, dx_ref, dgain_ref, acc_ref):
        i, k = pl.program_id(0), pl.program_id(1)
        r = _dotf(a_ref[...], b_ref[...], "nt")

        @pl.when((i == 0) & (k == 0))
        def _():
            dgain_ref[...] = jnp.zeros(dgain_ref.shape, F32)

        @pl.when(k == 0)
        def _():
            acc_ref[...] = r

        @pl.when(k > 0)
        def _():
            acc_ref[...] += r

        @pl.when(k == nk - 1)
        def _():
            dx, dgain = _vjp(_rms, (x_ref[...], g_ref[...]), acc_ref[...])
            dx_ref[...] = r_ref[...] + dx
            dgain_ref[...] += dgain

    row = pl.BlockSpec((tm, N), lambda i, k: (i, 0))
    vec = pl.BlockSpec((1, N), lambda i, k: (0, 0))
    return pl.pallas_call(
        body, grid=(M // tm, nk),
        in_specs=[pl.BlockSpec((tm, tk), lambda i, k: (i, k)), pl.BlockSpec((N, tk), lambda i, k: (0, k)), row, vec, row],
        out_specs=[row, vec], out_shape=[jax.ShapeDtypeStruct((M, N), F32), jax.ShapeDtypeStruct((1, N), F32)],
        scratch_shapes=[pltpu.VMEM((tm, N), F32)], name=name, compiler_params=_params(("arbitrary", "arbitrary")),
    )(a, b, x, gain, dx_res)


def _roll_rows(x, d, reverse):
    return pltpu.roll(x, (SUBLANES - d) if reverse else d, 0)


def scan_real(a, b, reverse=False, name="scan_real"):
    S, W = b.shape
    cw = _pick(W, (256, 128))
    n_tiles = S // SUBLANES

    def body(a_ref, b_ref, o_ref):
        row = lax.broadcasted_iota(jnp.int32, (SUBLANES, cw), 0)
        edge = 0 if reverse else SUBLANES - 1

        def step(i, carry):
            t = (n_tiles - 1 - i) if reverse else i
            off = pl.multiple_of(t * SUBLANES, SUBLANES)
            A = a_ref[pl.ds(off, SUBLANES), :]
            B = b_ref[pl.ds(off, SUBLANES), :]
            for d in (1, 2, 4):
                m = (row < SUBLANES - d) if reverse else (row >= d)
                B = jnp.where(m, A * _roll_rows(B, d, reverse) + B, B)
                A = jnp.where(m, A * _roll_rows(A, d, reverse), A)
            o_ref[pl.ds(off, SUBLANES), :] = B + A * carry
            at_edge = row == edge
            return (jnp.sum(jnp.where(at_edge, B, 0.0), axis=0, keepdims=True)
                    + jnp.sum(jnp.where(at_edge, A, 0.0), axis=0, keepdims=True) * carry)

        lax.fori_loop(0, n_tiles, step, jnp.zeros((1, cw), F32), unroll=2)

    spec = pl.BlockSpec((S, cw), lambda j: (0, j))
    return pl.pallas_call(
        body, grid=(W // cw,), in_specs=[spec, spec], out_specs=spec,
        out_shape=jax.ShapeDtypeStruct((S, W), F32), name=name, compiler_params=_params(("parallel",)),
    )(a, b)


def scan_cplx(lam, bu, reverse=False, name="scan_cplx"):
    S, C = bu.shape
    half = LANES
    CB = _pick(C, (1024, 512, 256))
    TS = _pick(S, (1024, 512, 256, 128, 64, 32, 16, 8))
    groups = CB // (2 * half)
    n_blocks, n_tiles = S // TS, TS // SUBLANES

    def cmul(ar, ai, br, bi):
        return ar * br - ai * bi, ar * bi + ai * br

    def body(lam_ref, bu_ref, o_ref, carry_ref):
        row = lax.broadcasted_iota(jnp.int32, (SUBLANES, half), 0)

        def edge_row(v):
            return jnp.sum(jnp.where(row == (0 if reverse else SUBLANES - 1), v, 0.0), axis=0, keepdims=True)

        @pl.when(pl.program_id(1) == 0)
        def _():
            carry_ref[...] = jnp.zeros(carry_ref.shape, F32)

        consts = []
        for g in range(groups):
            lr = lam_ref[:, 2 * half * g:2 * half * g + half]
            li = lam_ref[:, 2 * half * g + half:2 * half * (g + 1)]
            if reverse:
                li = -li
            l1 = (lr, li)
            l2 = cmul(*l1, *l1)
            l4 = cmul(*l2, *l2)
            pr = jnp.zeros((SUBLANES, half), F32)
            pi = jnp.zeros((SUBLANES, half), F32)
            p = l1
            for r in range(SUBLANES):
                sel = row == ((SUBLANES - 1 - r) if reverse else r)
                pr = jnp.where(sel, p[0], pr)
                pi = jnp.where(sel, p[1], pi)
                p = cmul(*p, *l1)
            consts.append((l1, l2, l4, pr, pi, edge_row(pr), edge_row(pi)))

        def step(i, carry):
            t = (n_tiles - 1 - i) if reverse else i
            off = pl.multiple_of(t * SUBLANES, SUBLANES)
            out = []
            for g in range(groups):
                l1, l2, l4, pr, pi, p8r, p8i = consts[g]
                cr, ci = carry[2 * g], carry[2 * g + 1]
                re, im = pl.ds(2 * half * g, half), pl.ds(2 * half * g + half, half)
                Br = bu_ref[pl.ds(off, SUBLANES), re]
                Bi = bu_ref[pl.ds(off, SUBLANES), im]
                for d, (qr, qi) in ((1, l1), (2, l2), (4, l4)):
                    m = (row < SUBLANES - d) if reverse else (row >= d)
                    sr, si = _roll_rows(Br, d, reverse), _roll_rows(Bi, d, reverse)
                    nr = jnp.where(m, Br + qr * sr - qi * si, Br)
                    ni = jnp.where(m, Bi + qr * si + qi * sr, Bi)
                    Br, Bi = nr, ni
                o_ref[pl.ds(off, SUBLANES), re] = Br + pr * cr - pi * ci
                o_ref[pl.ds(off, SUBLANES), im] = Bi + pr * ci + pi * cr
                er, ei = edge_row(Br), edge_row(Bi)
                out += [er + p8r * cr - p8i * ci, ei + p8r * ci + p8i * cr]
            return tuple(out)

        carry0 = tuple(carry_ref[:, pl.ds(half * k, half)] for k in range(2 * groups))
        carry1 = lax.fori_loop(0, n_tiles, step, carry0, unroll=2)
        for k in range(2 * groups):
            carry_ref[:, pl.ds(half * k, half)] = carry1[k]

    def rows(j, t):
        return ((n_blocks - 1 - t) if reverse else t, j)

    spec = pl.BlockSpec((TS, CB), rows)
    return pl.pallas_call(
        body, grid=(C // CB, n_blocks), in_specs=[pl.BlockSpec((1, CB), lambda j, t: (0, j)), spec],
        out_specs=spec, out_shape=jax.ShapeDtypeStruct((S, C), F32), scratch_shapes=[pltpu.VMEM((1, CB), F32)],
        name=name, compiler_params=_params(("parallel", "arbitrary")),
    )(lam, bu)


ATTN_HEADS_PER_STEP = 2


def _attn_tile(S, window):
    if window is None:
        return _pick(S, (512, 256, 128))
    return max(window, _pick(S, (256, 128)))


def _attn_valid(q_blk, k_blk, T, window):
    kpos = k_blk * T + lax.broadcasted_iota(jnp.int32, (T, T), 0)
    qpos = q_blk * T + lax.broadcasted_iota(jnp.int32, (T, T), 1)
    valid = kpos <= qpos
    if window is not None:
        valid = valid & (qpos - kpos < window)
    return valid


def attn_fwd(q, k, v, sink, cq=None, ck=None, window=None, name="attn_fwd"):
    H, S, Dh = q.shape
    G = H // k.shape[0]
    HP = ATTN_HEADS_PER_STEP
    assert H % HP == 0 and (G == 1 or G % HP == 0)
    KP = HP if G == 1 else 1
    T = _attn_tile(S, window)
    nq = S // T
    scale = Dh ** -0.5
    bias = cq is not None

    if window is None:
        n_steps = nq * (nq + 1) // 2

        def pair(t):
            i = sum((t >= m * (m + 1) // 2).astype(jnp.int32) for m in range(1, nq)) if nq > 1 else 0 * t
            return i, t - (i * (i + 1)) // 2

        def kv_block(i, j):
            return j
    else:
        n_steps = 2 * nq

        def pair(t):
            return t // 2, t % 2

        def kv_block(i, j):
            return jnp.maximum(i - 1 + j, 0)

    def body(*refs):
        if bias:
            q_ref, k_ref, v_ref, s_ref, cq_ref, ck_ref, o_ref, lse_ref, m_scr, l_scr, acc_scr = refs
        else:
            q_ref, k_ref, v_ref, s_ref, o_ref, lse_ref, m_scr, l_scr, acc_scr = refs
        i, j = pair(pl.program_id(1))

        @pl.when(j == 0)
        def _():
            m_scr[...] = jnp.zeros(m_scr.shape, F32) + s_ref[...]
            l_scr[...] = jnp.ones(l_scr.shape, F32)
            acc_scr[...] = jnp.zeros(acc_scr.shape, F32)

        def block(masked):
            valid = _attn_valid(i, kv_block(i, j), T, window) if masked else None
            for b in range(HP):
                kvb = b if G == 1 else 0
                s = _dotf(k_ref[kvb], q_ref[b], "nt") * scale
                if bias:
                    s = s + cq_ref[b] - ck_ref[b]
                if masked:
                    s = jnp.where(valid, s, NEG)
                m_old = m_scr[b]
                m_new = jnp.maximum(m_old, jnp.max(s, axis=0, keepdims=True))
                alpha = jnp.exp(m_old - m_new)
                p = jnp.exp(s - m_new)
                l_scr[b] = alpha * l_scr[b] + jnp.sum(p, axis=0, keepdims=True)
                acc_scr[b] = alpha * acc_scr[b] + _dotf(v_ref[kvb], p, "tn")
                m_scr[b] = m_new

        if window is None:
            pl.when(j < i)(lambda: block(False))
            pl.when(j == i)(lambda: block(True))
        else:
            pl.when(i - 1 + j >= 0)(lambda: block(True))

        @pl.when(j == (i if window is None else 1))
        def _():
            o_ref[...] = acc_scr[...] / l_scr[...]
            lse_ref[...] = m_scr[...] + jnp.log(l_scr[...])

    def kv_map(hp, t):
        return (hp if G == 1 else (hp * HP) // G, kv_block(*pair(t)), 0)

    in_specs = [
        pl.BlockSpec((HP, T, Dh), lambda hp, t: (hp, pair(t)[0], 0)),
        pl.BlockSpec((KP, T, Dh), kv_map),
        pl.BlockSpec((KP, T, Dh), kv_map),
        pl.BlockSpec((HP, 1, 1), lambda hp, t: (hp, 0, 0)),
    ]
    args = [q, k, v, sink]
    if bias:
        in_specs += [pl.BlockSpec((HP, 1, T), lambda hp, t: (hp, 0, pair(t)[0])),
                     pl.BlockSpec((HP, T, 1), lambda hp, t: (hp, kv_block(*pair(t)), 0))]
        args += [cq, ck]
    return pl.pallas_call(
        body, grid=(H // HP, n_steps), in_specs=in_specs,
        out_specs=[pl.BlockSpec((HP, Dh, T), lambda hp, t: (hp, 0, pair(t)[0])),
                   pl.BlockSpec((HP, 1, T), lambda hp, t: (hp, 0, pair(t)[0]))],
        out_shape=[jax.ShapeDtypeStruct((H, Dh, S), F32), jax.ShapeDtypeStruct((H, 1, S), F32)],
        scratch_shapes=[pltpu.VMEM((HP, 1, T), F32), pltpu.VMEM((HP, 1, T), F32), pltpu.VMEM((HP, Dh, T), F32)],
        name=name, compiler_params=_params(("parallel", "arbitrary")),
    )(*args)


def attn_bwd(q, k, v, lse, do, delta, cq=None, ck=None, window=None, name="attn_bwd"):
    H, S, Dh = q.shape
    KVH = k.shape[0]
    G = H // KVH
    HP = ATTN_HEADS_PER_STEP
    assert H % HP == 0 and (G == 1 or G % HP == 0)
    pair_kv = G == 1
    KP = HP if pair_kv else 1
    T = _attn_tile(S, window)
    nq = S // T
    scale = Dh ** -0.5
    bias = cq is not None
    assert not bias or G == 1

    if window is None:
        assert pair_kv
        n_a, n_j = nq * (nq + 1) // 2, 1

        def start(m):
            return m * nq - (m * (m - 1)) // 2

        def blocks(a, j):
            kb = sum((a >= start(m)).astype(jnp.int32) for m in range(1, nq)) if nq > 1 else 0 * a
            return kb, kb + a - start(kb)
    else:
        n_a, n_j = nq, 2

        def blocks(a, j):
            return a, jnp.minimum(a + j, nq - 1)

    def body(*refs):
        if bias:
            (q_ref, k_ref, v_ref, lse_ref, do_ref, dl_ref, cq_ref, ck_ref,
             dq_ref, dk_ref, dv_ref, dcq_ref, dck_ref) = refs
        else:
            q_ref, k_ref, v_ref, lse_ref, do_ref, dl_ref, dq_ref, dk_ref, dv_ref = refs
        a, gp, j = pl.program_id(1), pl.program_id(2), pl.program_id(3)
        kb, qi = blocks(a, j)
        first = (qi == kb) if window is None else (j == 0)

        @pl.when((gp == 0) & first)
        def _():
            dk_ref[...] = jnp.zeros(dk_ref.shape, F32)
            dv_ref[...] = jnp.zeros(dv_ref.shape, F32)
            if bias:
                dck_ref[...] = jnp.zeros(dck_ref.shape, F32)

        @pl.when((a == 0) & (gp == 0) & (j == 0))
        def _():
            dq_ref[...] = jnp.zeros(dq_ref.shape, F32)
            if bias:
                dcq_ref[...] = jnp.zeros(dcq_ref.shape, F32)

        def block(masked):
            off = pl.multiple_of(qi * T, T)
            valid = _attn_valid(qi, kb, T, window) if masked else None
            for b in range(HP):
                kvb = b if pair_kv else 0
                g = 0 if pair_kv else gp * HP + b
                qb, kk, vv = q_ref[b].astype(MXU_DTYPE), k_ref[kvb].astype(MXU_DTYPE), v_ref[kvb].astype(MXU_DTYPE)
                dob = do_ref[b].astype(MXU_DTYPE)
                s = _dotf(kk, qb, "nt") * scale
                if bias:
                    s = s + cq_ref[b] - ck_ref[b]
                if masked:
                    s = jnp.where(valid, s, NEG)
                p = jnp.exp(s - lse_ref[b])
                dv_ref[kvb] += _dotf(p, dob, "nt")
                ds = p * (_dotf(vv, dob) - dl_ref[b])
                dsb = ds.astype(MXU_DTYPE)
                dk_ref[kvb] += scale * _dotf(dsb, qb)
                dq_ref[kvb, g, pl.ds(off, T), :] += scale * _dotf(dsb, kk, "tn")
                if bias:
                    dcq_ref[kvb, g, :, pl.ds(off, T)] += jnp.sum(ds, axis=0, keepdims=True)
                    dck_ref[kvb] -= jnp.sum(ds, axis=1, keepdims=True)

        if window is None:
            pl.when(qi > kb)(lambda: block(False))
            pl.when(qi == kb)(lambda: block(True))
        else:
            pl.when(kb + j <= nq - 1)(lambda: block(True))

    def qmap(kvp, a, gp, j):
        return (kvp if pair_kv else (kvp * G) // HP + gp, blocks(a, j)[1], 0)

    def qmap_t(kvp, a, gp, j):
        return (kvp if pair_kv else (kvp * G) // HP + gp, 0, blocks(a, j)[1])

    def kmap(kvp, a, gp, j):
        return (kvp, blocks(a, j)[0], 0)

    in_specs = [
        pl.BlockSpec((HP, T, Dh), qmap),
        pl.BlockSpec((KP, T, Dh), kmap),
        pl.BlockSpec((KP, T, Dh), kmap),
        pl.BlockSpec((HP, 1, T), qmap_t),
        pl.BlockSpec((HP, Dh, T), qmap_t),
        pl.BlockSpec((HP, 1, T), qmap_t),
    ]
    args = [q, k, v, lse, do, delta]
    out_specs = [
        pl.BlockSpec((KP, G, S, Dh), lambda kvp, a, gp, j: (kvp, 0, 0, 0)),
        pl.BlockSpec((KP, T, Dh), kmap),
        pl.BlockSpec((KP, T, Dh), kmap),
    ]
    out_shape = [jax.ShapeDtypeStruct((KVH, G, S, Dh), F32), jax.ShapeDtypeStruct((KVH, S, Dh), F32),
                 jax.ShapeDtypeStruct((KVH, S, Dh), F32)]
    if bias:
        in_specs += [pl.BlockSpec((HP, 1, T), qmap_t), pl.BlockSpec((HP, T, 1), kmap)]
        args += [cq, ck]
        out_specs += [pl.BlockSpec((KP, G, 1, S), lambda kvp, a, gp, j: (kvp, 0, 0, 0)),
                      pl.BlockSpec((KP, T, 1), kmap)]
        out_shape += [jax.ShapeDtypeStruct((KVH, G, 1, S), F32), jax.ShapeDtypeStruct((KVH, S, 1), F32)]
    res = pl.pallas_call(
        body, grid=(KVH // KP, n_a, 1 if pair_kv else G // HP, n_j), in_specs=in_specs, out_specs=out_specs,
        out_shape=out_shape, name=name, compiler_params=_params(("arbitrary", "arbitrary", "arbitrary", "arbitrary")),
    )(*args)
    dq = res[0].reshape(H, S, Dh)
    if bias:
        return dq, res[1], res[2], res[3].reshape(H, 1, S), res[4]
    return dq, res[1], res[2]


def _rms(x, g):
    return x * lax.rsqrt(jnp.mean(x * x, axis=-1, keepdims=True) + EPS) * g


def _sigmoid(x):
    return 1.0 / (1.0 + jnp.exp(-x))


def _softplus(x):
    return jnp.maximum(x, 0.0) + jnp.log(1.0 + jnp.exp(-jnp.abs(x)))


def _log_sigmoid(x):
    return jnp.minimum(x, 0.0) - jnp.log(1.0 + jnp.exp(-jnp.abs(x)))


def _gelu(x):
    return 0.5 * x * (1.0 + jnp.tanh(math.sqrt(2.0 / math.pi) * (x + 0.044715 * (x * x * x))))


def _silu(x):
    return x * _sigmoid(x)


def _ffn_act(gu):
    f = gu.shape[1] // 2
    return MACARON * _silu(gu[:, :f]) * gu[:, f:]


def _qk_prep(rope):
    def f(x, *rest):
        if rope:
            cos, sin, g, rot = rest
        else:
            (g,) = rest
        y = _rms(x, g)
        if rope:
            y = y * cos + jnp.dot(y, rot, precision=HI, preferred_element_type=F32) * sin
        return y
    return f


def _lru_gates(pre, xc, ba, bx, lam):
    w = xc.shape[1]
    r = _sigmoid(pre[:, :w] + ba)
    i = _sigmoid(pre[:, w:] + bx)
    log_a = -LRU_C * r * _softplus(lam)
    a = jnp.exp(log_a)
    b = jnp.sqrt(1.0 - jnp.exp(2.0 * log_a)) * (i * xc)
    return a, b


def _lru_conv(x0, x1, x2, x3, w0, w1, w2, w3, cb):
    return cb + x0 * w0 + x1 * w1 + x2 * w2 + x3 * w3


def _s5_params(lre, lim, ldt, gsel, bre, bim):
    dt = jnp.sum(gsel * jnp.exp(ldt), axis=1, keepdims=True)
    er = jnp.exp(lre * dt)
    ang = lim * dt
    lbr, lbi = er * jnp.cos(ang), er * jnp.sin(ang)
    nr, ni = lbr - 1.0, lbi
    den = lre * lre + lim * lim
    fr, fi = (nr * lre + ni * lim) / den, (ni * lre - nr * lim) / den
    return lbr, lbi, fr * bre - fi * bim, fr * bim + fi * bre


def _s5_out(yssm, u, d):
    return _gelu(yssm + d * u)


def _glu(z, gl, gb):
    return z * _sigmoid(gl + gb)


def _ple_out(x, gpre, epre, pn):
    return x + _sigmoid(gpre) * _rms(epre, pn)


def _vjp(fn, args, cots):
    _, pull = jax.vjp(fn, *args)
    return pull(cots)


def add_norm(x, y, g, name):
    D = x.shape[1]
    if y is None:
        return x, rowwise(lambda xv, gv: ([_rms(xv, gv)], []), [x], [g], outs=[(D, BF16)], name=name)[0]
    xn, n = rowwise(lambda xv, yv, gv: ([xv + yv, _rms(xv + yv, gv)], []), [x, y], [g],
                    outs=[(D, F32), (D, BF16)], name=name)
    return xn, n


def norm_bwd(x, g, dn, dx_res, name):
    D = x.shape[1]

    def f(xv, dnv, dxv, gv):
        dx, dg = _vjp(_rms, (xv, gv), dnv)
        return [dxv + dx], [dg]

    return rowwise(f, [x, dn, dx_res], [g], outs=[(D, F32)], accs=[(1, D)], name=name)


def _swiglu(g, u):
    return MACARON * _silu(g) * u


def _dotf(a, b, mode="nn"):
    return lax.dot_general(a.astype(MXU_DTYPE), b.astype(MXU_DTYPE), _DOT_DIMS[mode], preferred_element_type=F32)


def ffn_up(n, wgu, ig, iu, name):
    S, D = n.shape
    C, _, _, Fc = wgu.shape
    tm = _tile(S, MM_TILE_M, 2 * SUBLANES)

    def body(n_ref, wg_ref, wu_ref, g_ref, u_ref, a_ref):
        g = _dotf(n_ref[...], wg_ref[...])
        u = _dotf(n_ref[...], wu_ref[...])
        g_ref[...] = g.astype(g_ref.dtype)
        u_ref[...] = u.astype(u_ref.dtype)
        a_ref[...] = _swiglu(g, u).astype(a_ref.dtype)

    hid = pl.BlockSpec((None, tm, Fc), lambda s, i: (s, i, 0))
    return pl.pallas_call(
        body, grid=(C, S // tm),
        in_specs=[pl.BlockSpec((tm, D), lambda s, i: (i, 0)),
                  pl.BlockSpec((None, None, D, Fc), lambda s, i: (s, ig, 0, 0)),
                  pl.BlockSpec((None, None, D, Fc), lambda s, i: (s, iu, 0, 0))],
        out_specs=[hid, hid, hid], out_shape=[jax.ShapeDtypeStruct((C, S, Fc), BF16)] * 3,
        name=name, compiler_params=_params(("parallel", "parallel")),
    )(n, wgu, wgu)


def ffn_down(act, wd, iw, x, gain, name):
    C, S, Fc = act.shape
    D = wd.shape[-1]
    tm = _tile(S, MM_TILE_M, 2 * SUBLANES)

    def body(a_ref, w_ref, x_ref, g_ref, xo_ref, n_ref, acc_ref):
        s = pl.program_id(1)
        r = _dotf(a_ref[...], w_ref[...])

        @pl.when(s == 0)
        def _():
            acc_ref[...] = r

        @pl.when(s > 0)
        def _():
            acc_ref[...] += r

        @pl.when(s == C - 1)
        def _():
            xn = x_ref[...] + acc_ref[...]
            xo_ref[...] = xn
            n_ref[...] = _rms(xn, g_ref[...]).astype(n_ref.dtype)

    row = pl.BlockSpec((tm, D), lambda i, s: (i, 0))
    return pl.pallas_call(
        body, grid=(S // tm, C),
        in_specs=[pl.BlockSpec((None, tm, Fc), lambda i, s: (s, i, 0)),
                  pl.BlockSpec((None, None, Fc, D), lambda i, s: (s, iw, 0, 0)), row,
                  pl.BlockSpec((1, D), lambda i, s: (0, 0))],
        out_specs=[row, row], out_shape=[jax.ShapeDtypeStruct((S, D), F32), jax.ShapeDtypeStruct((S, D), BF16)],
        scratch_shapes=[pltpu.VMEM((tm, D), F32)], name=name, compiler_params=_params(("parallel", "arbitrary")),
    )(act, wd, x, gain)


def ffn_down_bwd(dy, wd, iw, g, u, name):
    C, S, Fc = g.shape
    D = dy.shape[1]
    tm = _tile(S, MM_TILE_M, 2 * SUBLANES)

    def body(dy_ref, w_ref, g_ref, u_ref, dg_ref, du_ref):
        dact = MACARON * _dotf(dy_ref[...], w_ref[...], "nt")
        g, u = g_ref[...].astype(F32), u_ref[...].astype(F32)
        sg = _sigmoid(g)
        gs = g * sg
        dg_ref[...] = (dact * u * (sg + gs * (1.0 - sg))).astype(dg_ref.dtype)
        du_ref[...] = (dact * gs).astype(du_ref.dtype)

    hid = pl.BlockSpec((None, tm, Fc), lambda s, i: (s, i, 0))
    return pl.pallas_call(
        body, grid=(C, S // tm),
        in_specs=[pl.BlockSpec((tm, D), lambda s, i: (i, 0)),
                  pl.BlockSpec((None, None, Fc, D), lambda s, i: (s, iw, 0, 0)), hid, hid],
        out_specs=[hid, hid], out_shape=[jax.ShapeDtypeStruct((C, S, Fc), BF16)] * 2,
        name=name, compiler_params=_params(("parallel", "parallel")),
    )(dy, wd, g, u)


def ffn_dn(dg, du, wgu, ig, iu, x, gain, dx_res, name):
    C, S, Fc = dg.shape
    D = wgu.shape[2]
    tm = _tile(S, MM_TILE_M, 2 * SUBLANES)
    parts = 2 if tm % (4 * SUBLANES) == 0 else 1

    def body(dg_ref, du_ref, wg_ref, wu_ref, x_ref, g_ref, r_ref, dx_ref, dgain_ref, acc_ref):
        i, s = pl.program_id(0), pl.program_id(1)
        r = _dotf(dg_ref[...], wg_ref[...], "nt") + _dotf(du_ref[...], wu_ref[...], "nt")

        @pl.when((i == 0) & (s == 0))
        def _():
            dgain_ref[...] = jnp.zeros(dgain_ref.shape, F32)

        @pl.when(s == 0)
        def _():
            acc_ref[...] = r

        @pl.when(s > 0)
        def _():
            acc_ref[...] += r

        @pl.when(s == C - 1)
        def _():
            for part in range(parts):
                rows = pl.ds(part * (tm // parts), tm // parts)
                dx, dgain = _vjp(_rms, (x_ref[rows, :], g_ref[...]), acc_ref[rows, :])
                dx_ref[rows, :] = r_ref[rows, :] + dx
                dgain_ref[...] += dgain

    hid = pl.BlockSpec((None, tm, Fc), lambda i, s: (s, i, 0))
    row = pl.BlockSpec((tm, D), lambda i, s: (i, 0))
    vec = pl.BlockSpec((1, D), lambda i, s: (0, 0))
    return pl.pallas_call(
        body, grid=(S // tm, C),
        in_specs=[hid, hid, pl.BlockSpec((None, None, D, Fc), lambda i, s: (s, ig, 0, 0)),
                  pl.BlockSpec((None, None, D, Fc), lambda i, s: (s, iu, 0, 0)), row, vec, row],
        out_specs=[row, vec], out_shape=[jax.ShapeDtypeStruct((S, D), F32), jax.ShapeDtypeStruct((1, D), F32)],
        scratch_shapes=[pltpu.VMEM((tm, D), F32)], name=name, compiler_params=_params(("arbitrary", "arbitrary")),
    )(dg, du, wgu, wgu, x, gain, dx_res)


def ffn_dw(a, d, buf, idx, shape, blocked, name):
    C, P, M, N = shape
    S = d.shape[-2]
    tk = _tile(S, MM_TILE_M, 2 * SUBLANES)
    nk = S // tk

    def body(*refs):
        a_ref, d_ref, o_ref, acc_ref = refs[0], refs[1], refs[-2], refs[-1]
        k = pl.program_id(1)
        r = _dotf(a_ref[...], d_ref[...], "tn")

        @pl.when(k == 0)
        def _():
            acc_ref[...] = r

        @pl.when(k > 0)
        def _():
            acc_ref[...] += r

        @pl.when(k == nk - 1)
        def _():
            o_ref[...] = acc_ref[...].astype(o_ref.dtype)

    if blocked == "a":
        a_spec = pl.BlockSpec((None, tk, M), lambda s, k: (s, k, 0))
        d_spec = pl.BlockSpec((tk, N), lambda s, k: (k, 0))
    else:
        a_spec = pl.BlockSpec((tk, M), lambda s, k: (k, 0))
        d_spec = pl.BlockSpec((None, tk, N), lambda s, k: (s, k, 0))
    out_spec = pl.BlockSpec((None, None, M, N), lambda s, k: (s, idx, 0, 0))
    out_shape = jax.ShapeDtypeStruct(tuple(shape), BF16)
    scratch = [pltpu.VMEM((M, N), F32)]
    if buf is None:
        return pl.pallas_call(body, grid=(C, nk), in_specs=[a_spec, d_spec], out_specs=out_spec, out_shape=out_shape,
                              scratch_shapes=scratch, name=name,
                              compiler_params=_params(("parallel", "arbitrary")))(a, d)
    return pl.pallas_call(body, grid=(C, nk), in_specs=[a_spec, d_spec, pl.BlockSpec(memory_space=pl.ANY)],
                          out_specs=out_spec, out_shape=out_shape, input_output_aliases={2: 0},
                          scratch_shapes=scratch, name=name,
                          compiler_params=_params(("parallel", "arbitrary")))(a, d, buf)


def ffn_fwd(n, x, gain, wgu, wd, ig, iu, iw, tag):
    g, u, act = ffn_up(n, wgu, ig, iu, f"ffn_up_{tag}")
    x_new, n_new = ffn_down(act, wd, iw, x, gain, f"ffn_down_{tag}")
    return x_new, n_new, (n, g, u, act)


def ffn_bwd(dy, saved, x, gain, wgu, wd, ig, iu, iw, gbuf, tag):
    n, g, u, act = saved
    gwgu, gwd = gbuf
    dg, du = ffn_down_bwd(dy, wd, iw, g, u, f"ffn_down_bwd_{tag}")
    gwd = ffn_dw(act, dy, gwd, iw, (N_CHIPS,) + wd.shape[1:], "a", f"ffn_dwd_{tag}")
    dx, dgain = ffn_dn(dg, du, wgu, ig, iu, x, gain, dy, f"ffn_dn_{tag}")
    gwgu = ffn_dw(n, dg, gwgu, ig, (N_CHIPS,) + wgu.shape[1:], "d", f"ffn_dwg_{tag}")
    gwgu = ffn_dw(n, du, gwgu, iu, (N_CHIPS,) + wgu.shape[1:], "d", f"ffn_dwu_{tag}")
    return dx, dgain, (gwgu, gwd)


def _heads(x, H):
    S = x.shape[0]
    return x.reshape(S, H, HEAD_DIM).transpose(1, 0, 2)


def _unheads(x):
    H, S, _ = x.shape
    return x.transpose(1, 0, 2).reshape(S, H * HEAD_DIM)


def _heads_t(x, H):
    return x.T.reshape(H, HEAD_DIM, x.shape[0])


def _unheads_t(x):
    return x.reshape(x.shape[0] * x.shape[1], x.shape[2]).T


def _shift_down(x, n=1):
    return jnp.pad(x, ((n, 0), (0, 0)))[:x.shape[0]]


def _shift_up(x, n=1):
    return jnp.pad(x, ((0, n), (0, 0)))[n:]


def _block_diag(w):
    B, I, J = w.shape
    eye = jnp.eye(B, dtype=w.dtype)
    return (w[:, :, None, :] * eye[:, None, :, None]).reshape(B * I, B * J)


def _block_diag_take(x, B):
    I, J = x.shape[0] // B, x.shape[1] // B
    eye = jnp.eye(B, dtype=x.dtype)
    return jnp.sum(x.reshape(B, I, B, J) * eye[:, None, :, None], axis=2)


def _rope_tables(S):
    half = HEAD_DIM // 2
    inv = jnp.power(ROPE_THETA, -jnp.arange(half, dtype=F32) / half)
    ang = jnp.arange(S, dtype=F32)[:, None] * inv[None, :]
    cos = jnp.concatenate([jnp.cos(ang), jnp.cos(ang)], axis=1)
    sin = jnp.concatenate([jnp.sin(ang), jnp.sin(ang)], axis=1)
    r = jnp.arange(HEAD_DIM)[:, None]
    c = jnp.arange(HEAD_DIM)[None, :]
    rot = jnp.where(r == c + half, -1.0, 0.0) + jnp.where(c == r + half, 1.0, 0.0)
    return cos, sin, rot.astype(F32)


def qk_prep_fwd(x_hm, g, rope_tabs, name):
    H, S, Dh = x_hm.shape
    rows = [x_hm.reshape(H * S, Dh)]
    consts = [g.reshape(1, Dh)]
    periods = [None]
    if rope_tabs is not None:
        rows += [rope_tabs[0], rope_tabs[1]]
        consts += [rope_tabs[2]]
        periods += [S, S]
    fn = _qk_prep(rope_tabs is not None)
    y = rowwise(lambda *a: ([fn(*a)], []), rows, consts, outs=[(Dh, F32)], name=name, periods=periods)[0]
    return y.reshape(H, S, Dh)


def qk_prep_bwd(x_hm, g, rope_tabs, dy_hm, name):
    H, S, Dh = x_hm.shape
    rope = rope_tabs is not None
    rows = [x_hm.reshape(H * S, Dh), dy_hm.reshape(H * S, Dh)]
    consts = [g.reshape(1, Dh)]
    periods = [None, None]
    if rope:
        rows += [rope_tabs[0], rope_tabs[1]]
        consts += [rope_tabs[2]]
        periods += [S, S]
    fn = _qk_prep(rope)

    def f(xv, dyv, *rest):
        if rope:
            cos, sin, gv, rot = rest
            dx, dg = _vjp(lambda a, b: fn(a, cos, sin, b, rot), (xv, gv), dyv)
        else:
            (gv,) = rest
            dx, dg = _vjp(fn, (xv, gv), dyv)
        return [dx], [dg]

    dx, dg = rowwise(f, rows, consts, outs=[(Dh, F32)], accs=[(1, Dh)], name=name, periods=periods)
    return dx.reshape(H, S, Dh), dg.reshape(Dh)


def attn_delta(do_t, o_t, name):
    H, Dh, S = o_t.shape

    def body(a_ref, b_ref, o_ref):
        o_ref[...] = jnp.sum(a_ref[...] * b_ref[...], axis=0, keepdims=True)

    spec = pl.BlockSpec((None, Dh, S), lambda h: (h, 0, 0))
    return pl.pallas_call(
        body, grid=(H,), in_specs=[spec, spec], out_specs=pl.BlockSpec((None, 1, S), lambda h: (h, 0, 0)),
        out_shape=jax.ShapeDtypeStruct((H, 1, S), F32), name=name, compiler_params=_params(("parallel",)),
    )(do_t, o_t)


def even_mixer_fwd(h, w, tag):
    S = h.shape[0]
    W = 512
    H = 8
    z = mm(h, w["w_in"], name=f"ev_in_{tag}")
    xa, ya, q, k, v, f = (z[:, 0:512], z[:, 512:1024], z[:, 1024:1536], z[:, 1536:2048], z[:, 2048:2560],
                          z[:, 2560:2688])
    xs = [_shift_down(xa, LRU_CONV - 1 - tap) for tap in range(LRU_CONV)]
    taps = [w["conv_w"][tap][None] for tap in range(LRU_CONV)]
    xc = rowwise(lambda *a: ([_lru_conv(*a)], []), xs, taps + [w["conv_b"]], outs=[(W, F32)],
                 name=f"lru_conv_{tag}")[0]
    pre = mm(xc, w["w_ax"], name=f"lru_gates_mm_{tag}")
    a, b = rowwise(lambda p_, x_, ba, bx, lam: (list(_lru_gates(p_, x_, ba, bx, lam)), []), [pre, xc],
                   [w["ba"], w["bx"], w["lam"]], outs=[(W, F32), (W, F32)], name=f"lru_gates_{tag}")
    hs = scan_real(a, b, name=f"lru_scan_{tag}")
    a_out = rowwise(lambda y_, h_: ([_gelu(y_) * h_], []), [ya, hs], outs=[(W, F32)], name=f"lru_out_{tag}")[0]
    lf = rowwise(lambda f_, bf: ([_log_sigmoid(f_ + bf)], []), [f], [w["bf"]], outs=[(LANES, F32)],
                 name=f"fox_logf_{tag}")[0]
    c = scan_real(jnp.ones_like(lf), lf, name=f"fox_cumsum_{tag}")
    c_hm = c[:, :H].T
    q_hm, k_hm, v_hm = _heads(q, H), _heads(k, H), _heads(v, H)
    qn = qk_prep_fwd(q_hm, w["qn"], None, f"fox_qprep_{tag}")
    kn = qk_prep_fwd(k_hm, w["kn"], None, f"fox_kprep_{tag}")
    sink = jnp.full((H, 1, 1), NEG, F32)
    o_hm, lse = attn_fwd(qn, kn, v_hm, sink, c_hm[:, None, :], c_hm[:, :, None], name=f"fox_attn_{tag}")
    mo = jnp.concatenate([a_out, _unheads_t(o_hm)], axis=1).astype(BF16)
    saved = dict(h=h, xs=xs, xc=xc, pre=pre, a=a, hs=hs, ya=ya, f=f, c_hm=c_hm, q_hm=q_hm, k_hm=k_hm,
                 v_hm=v_hm, qn=qn, kn=kn, o_hm=o_hm, lse=lse, mo=mo)
    return mo, saved


def even_mixer_bwd(dy, sv, w, tag):
    W = 512
    H = 8
    S = dy.shape[0]
    g = {}
    dmo = mm(dy, w["w_out"], "nt", name=f"ev_dmo_{tag}")
    g["w_out"] = mm(sv["mo"], dy, "tn", name=f"ev_dwout_{tag}")
    da_out, do = dmo[:, :W], dmo[:, W:]
    do_hm = _heads_t(do, H)
    delta = attn_delta(do_hm, sv["o_hm"], f"fox_delta_{tag}")
    c_hm = sv["c_hm"]
    dqn, dkn, dv_hm, dcq, dck = attn_bwd(sv["qn"], sv["kn"], sv["v_hm"], sv["lse"], do_hm, delta,
                                          c_hm[:, None, :], c_hm[:, :, None], name=f"fox_attn_bwd_{tag}")
    dq_hm, g["qn"] = qk_prep_bwd(sv["q_hm"], w["qn"], None, dqn, f"fox_qprep_bwd_{tag}")
    dk_hm, g["kn"] = qk_prep_bwd(sv["k_hm"], w["kn"], None, dkn, f"fox_kprep_bwd_{tag}")
    dc = (dcq[:, 0, :] + dck[:, :, 0]).T
    dc = jnp.pad(dc, ((0, 0), (0, LANES - H)))
    dlf = scan_real(jnp.ones_like(dc), dc, reverse=True, name=f"fox_cumsum_bwd_{tag}")

    def f_logf(f_, d_, bf):
        df, dbf = _vjp(lambda a_, b_: _log_sigmoid(a_ + b_), (f_, bf), d_)
        return [df], [dbf]

    df, dbf = rowwise(f_logf, [sv["f"], dlf], [w["bf"]], outs=[(LANES, F32)], accs=[(1, LANES)],
                      name=f"fox_logf_bwd_{tag}")
    g["bf"] = dbf[0, :H]
    def f_out(y_, h_, d_):
        dyv, dhv = _vjp(lambda a_, b_: _gelu(a_) * b_, (y_, h_), d_)
        return [dyv, dhv], []

    dya, dhs = rowwise(f_out, [sv["ya"], sv["hs"], da_out], outs=[(W, F32), (W, F32)], name=f"lru_out_bwd_{tag}")
    gs = scan_real(_shift_up(sv["a"]), dhs, reverse=True, name=f"lru_scan_bwd_{tag}")

    def f_gates(p_, x_, g_, hp_, ba, bx, lam):
        dp, dx, dba, dbx, dlam = _vjp(_lru_gates, (p_, x_, ba, bx, lam), (g_ * hp_, g_))
        return [dp, dx], [dba, dbx, dlam]

    dpre, dxc, dba, dbx, dlam = rowwise(f_gates, [sv["pre"], sv["xc"], gs, _shift_down(sv["hs"])],
                                        [w["ba"], w["bx"], w["lam"]], outs=[(2 * W, BF16), (W, F32)],
                                        accs=[(1, W)] * 3, name=f"lru_gates_bwd_{tag}")
    g["ba"], g["bx"], g["lam"] = dba[0], dbx[0], dlam[0]
    dxc2 = mm(dpre, w["w_ax"], "nt", name=f"lru_gates_mm_dx_{tag}")
    g["w_ax"] = mm(sv["xc"], dpre, "tn", name=f"lru_gates_mm_dw_{tag}")

    def f_conv(d1, d2, x0, x1, x2, x3):
        d = d1 + d2
        return [d], [jnp.sum(d, axis=0, keepdims=True)] + [jnp.sum(d * xv, axis=0, keepdims=True)
                                                           for xv in (x0, x1, x2, x3)]

    dxc_t, dcb, dw0, dw1, dw2, dw3 = rowwise(f_conv, [dxc, dxc2] + sv["xs"], outs=[(W, F32)],
                                             accs=[(1, W)] * 5, name=f"lru_conv_bwd_{tag}")
    g["conv_b"] = dcb[0]
    g["conv_w"] = jnp.concatenate([dw0, dw1, dw2, dw3], axis=0)
    ds_ = [_shift_up(dxc_t, LRU_CONV - 1 - tap) for tap in range(LRU_CONV)]
    taps = [w["conv_w"][tap][None] for tap in range(LRU_CONV)]
    dxa = rowwise(lambda a, b, c, d, w0, w1, w2, w3: ([a * w0 + b * w1 + c * w2 + d * w3], []), ds_, taps,
                  outs=[(W, F32)], name=f"lru_conv_dx_{tag}")[0]
    dz = jnp.concatenate([dxa, dya, _unheads(dq_hm), _unheads(dk_hm), _unheads(dv_hm), df], axis=1).astype(BF16)
    g["w_in"] = mm(sv["h"], dz, "tn", name=f"ev_dwin_{tag}")
    return dz, g


def odd_mixer_fwd(h, w, tag):
    S = h.shape[0]
    H, KVH = 8, 2
    z = mm(h, w["w_in"], name=f"od_in_{tag}")
    q, k, v, u = z[:, 0:512], z[:, 512:640], z[:, 640:768], z[:, 768:1280]
    tabs = _rope_tables(S)
    q_hm, k_hm, v_hm = _heads(q, H), _heads(k, KVH), _heads(v, KVH)
    qn = qk_prep_fwd(q_hm, w["qn"], tabs, f"swa_qprep_{tag}")
    kn = qk_prep_fwd(k_hm, w["kn"], tabs, f"swa_kprep_{tag}")
    sink = w["sinks"].reshape(H, 1, 1)
    o_hm, lse = attn_fwd(qn, kn, v_hm, sink, window=SWA_WINDOW, name=f"swa_attn_{tag}")
    lam, bexp = w["s5_lam"], w["s5_bexp"]
    bu = mm_blocks(u, bexp, name=f"s5_bu_{tag}")
    hs = scan_cplx(lam, bu, name=f"s5_scan_{tag}")
    yssm = mm_blocks(hs, w["s5_cexp"], name=f"s5_y_{tag}")
    zz = rowwise(lambda y_, u_, d_: ([_s5_out(y_, u_, d_)], []), [yssm, u], [w["s5_d"]], outs=[(512, F32)],
                 name=f"s5_gelu_{tag}")[0]
    gl = mm(zz, w["glu_w"], name=f"s5_glu_mm_{tag}")
    d_out = rowwise(lambda z_, g_, b_: ([_glu(z_, g_, b_)], []), [zz, gl], [w["glu_b"]], outs=[(512, F32)],
                    name=f"s5_glu_{tag}")[0]
    mo = jnp.concatenate([_unheads_t(o_hm), d_out], axis=1).astype(BF16)
    saved = dict(h=h, q_hm=q_hm, k_hm=k_hm, v_hm=v_hm, qn=qn, kn=kn, o_hm=o_hm, lse=lse, u=u, hs=hs, yssm=yssm,
                 zz=zz, gl=gl, mo=mo, tabs=tabs)
    return mo, saved


def odd_mixer_bwd(dy, sv, w, tag):
    H, KVH = 8, 2
    g = {}
    dmo = mm(dy, w["w_out"], "nt", name=f"od_dmo_{tag}")
    g["w_out"] = mm(sv["mo"], dy, "tn", name=f"od_dwout_{tag}")
    do, dd = dmo[:, :512], dmo[:, 512:]
    do_hm = _heads_t(do, H)
    delta = attn_delta(do_hm, sv["o_hm"], f"swa_delta_{tag}")
    dqn, dkn, dv_hm = attn_bwd(sv["qn"], sv["kn"], sv["v_hm"], sv["lse"], do_hm, delta, window=SWA_WINDOW,
                               name=f"swa_attn_bwd_{tag}")
    dq_hm, g["qn"] = qk_prep_bwd(sv["q_hm"], w["qn"], sv["tabs"], dqn, f"swa_qprep_bwd_{tag}")
    dk_hm, g["kn"] = qk_prep_bwd(sv["k_hm"], w["kn"], sv["tabs"], dkn, f"swa_kprep_bwd_{tag}")
    lse_t, delta_t = sv["lse"][:, 0, :].T, delta[:, 0, :].T
    g["sinks"] = rowwise(lambda l_, d_, s_: ([], [jnp.sum(-jnp.exp(s_ - l_) * d_, axis=0, keepdims=True)]),
                         [lse_t, delta_t], [w["sinks"].reshape(1, H)], accs=[(1, H)], name=f"swa_dsink_{tag}")[0][0]
    def f_glu(z_, g_, d_, b_):
        dz_, dg_, db_ = _vjp(_glu, (z_, g_, b_), d_)
        return [dz_, dg_], [db_]

    dzz1, dgl, dglb = rowwise(f_glu, [sv["zz"], sv["gl"], dd], [w["glu_b"]], outs=[(512, F32), (512, BF16)],
                              accs=[(1, 512)], name=f"s5_glu_bwd_{tag}")
    g["glu_b"] = dglb[0]
    g["glu_w"] = mm(sv["zz"], dgl, "tn", name=f"s5_glu_dw_{tag}")
    dzz2 = mm(dgl, w["glu_w"], "nt", name=f"s5_glu_dz_{tag}")

    def f_gelu(y_, u_, d1, d2, dpar):
        dy_, du_, dd_ = _vjp(_s5_out, (y_, u_, dpar), d1 + d2)
        return [dy_, du_], [dd_]

    dyssm, du1, dsd = rowwise(f_gelu, [sv["yssm"], sv["u"], dzz1, dzz2], [w["s5_d"]],
                              outs=[(512, F32), (512, F32)], accs=[(1, 512)], name=f"s5_gelu_bwd_{tag}")
    g["s5_d"] = dsd[0]
    dhs = mm_blocks(dyssm, w["s5_cexp"], "nt", name=f"s5_dh_{tag}")
    g["s5_cexp"] = _block_diag(mm_blocks_tn(sv["hs"], dyssm, S5_BLOCKS, name=f"s5_dc_{tag}"))
    gs = scan_cplx(w["s5_lam"], dhs, reverse=True, name=f"s5_scan_bwd_{tag}")
    g["s5_bexp"] = _block_diag(mm_blocks_tn(sv["u"], gs, S5_BLOCKS, name=f"s5_db_{tag}"))
    du2 = mm_blocks(gs, w["s5_bexp"], "nt", name=f"s5_du_{tag}")

    def f_dlam(g_, hp_):
        C = g_.shape[1]
        outs_r, outs_i = [], []
        for j in range(C // (2 * LANES)):
            gr, gi = g_[:, 2 * LANES * j:2 * LANES * j + LANES], g_[:, 2 * LANES * j + LANES:2 * LANES * (j + 1)]
            hr, hi = hp_[:, 2 * LANES * j:2 * LANES * j + LANES], hp_[:, 2 * LANES * j + LANES:2 * LANES * (j + 1)]
            outs_r.append(jnp.sum(gr * hr + gi * hi, axis=0, keepdims=True))
            outs_i.append(jnp.sum(gi * hr - gr * hi, axis=0, keepdims=True))
        return [], [jnp.concatenate([x for pair in zip(outs_r, outs_i) for x in pair], axis=1)]

    g["s5_lam"] = rowwise(f_dlam, [gs, _shift_down(sv["hs"])], accs=[(1, gs.shape[1])], name=f"s5_dlam_{tag}")[0]
    du = rowwise(lambda a_, b_: ([a_ + b_], []), [du1, du2], outs=[(512, F32)], name=f"s5_du_add_{tag}")[0]
    dz = jnp.concatenate([_unheads(dq_hm), _unheads(dk_hm), _unheads(dv_hm), du], axis=1).astype(BF16)
    g["w_in"] = mm(sv["h"], dz, "tn", name=f"od_dwin_{tag}")
    return dz, g


def _s5_cols(x_re, x_im):
    n = x_re.shape[0] // LANES
    return jnp.stack([x_re.reshape(n, LANES), x_im.reshape(n, LANES)], axis=1).reshape(1, 2 * n * LANES)


def _s5_uncols(x):
    n = x.shape[1] // (2 * LANES)
    y = x.reshape(n, 2, LANES)
    return y[:, 0].reshape(-1), y[:, 1].reshape(-1)


def _s5_gsel():
    return jnp.repeat(jnp.eye(S5_GROUPS, dtype=F32), S5_STATE, axis=0)


def s5_prep_fwd(lre, lim, ldt, bre, bim, cre, cim, tag):
    GP = S5_GROUPS * S5_STATE
    ins = [lre.reshape(GP, 1), lim.reshape(GP, 1), ldt.reshape(1, S5_GROUPS), _s5_gsel(),
           bre.reshape(GP, S5_GROUP), bim.reshape(GP, S5_GROUP)]
    lbr, lbi, bbr, bbi = whole(_s5_params, ins, [((GP, 1), F32)] * 2 + [((GP, S5_GROUP), F32)] * 2,
                               name=f"s5_params_{tag}")
    lam = _s5_cols(lbr[:, 0], lbi[:, 0])

    def expand_b(bb):
        return _block_diag(bb.reshape(S5_GROUPS, S5_STATE, S5_GROUP).transpose(0, 2, 1))

    n = GP // LANES
    bexp = jnp.stack([expand_b(bbr).reshape(-1, n, LANES), expand_b(bbi).reshape(-1, n, LANES)],
                     axis=2).reshape(-1, 2 * GP)
    c_r = _block_diag(cre.transpose(0, 2, 1))
    c_i = _block_diag(cim.transpose(0, 2, 1))
    cexp = jnp.stack([c_r.reshape(n, LANES, -1), -c_i.reshape(n, LANES, -1)], axis=1).reshape(2 * GP, -1)
    cb, sb = bexp.shape[0] // S5_BLOCKS, bexp.shape[1] // S5_BLOCKS
    bexp = jnp.stack([bexp[cb * j:cb * (j + 1), sb * j:sb * (j + 1)] for j in range(S5_BLOCKS)])
    cexp = jnp.stack([cexp[sb * j:sb * (j + 1), cb * j:cb * (j + 1)] for j in range(S5_BLOCKS)])
    return lam, bexp.astype(BF16), cexp.astype(BF16), ins


def s5_prep_bwd(ins, dlam, dbexp, dcexp, tag):
    GP = S5_GROUPS * S5_STATE
    n = GP // LANES
    dlr, dli = _s5_uncols(dlam)
    db = dbexp.reshape(-1, n, 2, LANES)

    def take_b(x):
        return _block_diag_take(x, S5_GROUPS).transpose(0, 2, 1).reshape(GP, S5_GROUP)

    dbbr, dbbi = take_b(db[:, :, 0].reshape(-1, GP)), take_b(db[:, :, 1].reshape(-1, GP))
    dc = dcexp.reshape(n, 2, LANES, -1)
    dcre = _block_diag_take(dc[:, 0].reshape(GP, -1), S5_GROUPS).transpose(0, 2, 1)
    dcim = -_block_diag_take(dc[:, 1].reshape(GP, -1), S5_GROUPS).transpose(0, 2, 1)

    def f(lre, lim, ldt, gsel, bre, bim, c1, c2, c3, c4):
        d = _vjp(lambda a, b, c, e, f_: _s5_params(a, b, c, gsel, e, f_), (lre, lim, ldt, bre, bim), (c1, c2, c3, c4))
        return d

    outs = [((GP, 1), F32)] * 2 + [((1, S5_GROUPS), F32)] + [((GP, S5_GROUP), F32)] * 2
    dlre, dlim, dldt, dbre, dbim = whole(f, ins + [dlr.reshape(GP, 1), dli.reshape(GP, 1), dbbr, dbbi], outs,
                                          name=f"s5_params_bwd_{tag}")
    shp = (S5_GROUPS, S5_STATE)
    return dict(lre=dlre.reshape(shp), lim=dlim.reshape(shp), ldt=dldt.reshape(S5_GROUPS),
                bre=dbre.reshape(S5_GROUPS, S5_STATE, S5_GROUP), bim=dbim.reshape(S5_GROUPS, S5_STATE, S5_GROUP),
                cre=dcre, cim=dcim)


def _place():
    return lax.axis_index("x"), lax.axis_index("y"), lax.axis_index("c")


def _other_chips(x, y):
    return [(1 - x, y), (x, 1 - y), (1 - x, 1 - y)]


def _half(ref, h):
    n = ref.shape[0] // 2
    return ref.at[pl.ds(h * n, n)]


def _hbm_specs(n):
    return [pl.BlockSpec(memory_space=pl.ANY)] * n


def gather_chips(ws):
    n = len(ws)

    def body(*refs):
        w_refs, out_refs, (send_sems, recv_sems) = refs[:n], refs[n:2 * n], refs[2 * n:]
        x, y, c = _place()
        me, sibling = (x, y, c), (x, y, 1 - c)
        chips = _other_chips(x, y)
        mine = 2 * x + y

        def copy(k, src, dst, to):
            return pltpu.make_async_remote_copy(src_ref=src, dst_ref=dst, send_sem=send_sems.at[k],
                                                recv_sem=recv_sems.at[k], device_id=to, device_id_type=MESH)

        first, passed = [], []
        for p in range(n):
            for j, chip in enumerate(chips):
                first.append(copy(6 * p + j, _half(w_refs[p], c), _half(out_refs[p].at[mine], c), (*chip, c)))
                first[-1].start()
        for p in range(n):
            for j, chip in enumerate(chips):
                block = out_refs[p].at[2 * chip[0] + chip[1]]
                copy(6 * p + j, _half(w_refs[p], c), _half(block, c), me).wait_recv()
                passed.append(copy(6 * p + 3 + j, _half(block, c), _half(block, c), sibling))
                passed[-1].start()
        for p in range(n):
            for j, chip in enumerate(chips):
                block = out_refs[p].at[2 * chip[0] + chip[1]]
                copy(6 * p + 3 + j, _half(w_refs[p], c), _half(block, 1 - c), me).wait_recv()
        for cp in first + passed:
            cp.wait_send()

    return pl.pallas_call(
        body, out_shape=[jax.ShapeDtypeStruct((N_CHIPS,) + w.shape, w.dtype) for w in ws],
        in_specs=_hbm_specs(n), out_specs=_hbm_specs(n),
        scratch_shapes=[pltpu.SemaphoreType.DMA((6 * n,)), pltpu.SemaphoreType.DMA((6 * n,))],
        name="gather_chips",
    )(*ws)


def sibling_halves(gs):
    n = len(gs)

    def body(*refs):
        g_refs, out_refs, (send_sems, recv_sems) = refs[:n], refs[n:2 * n], refs[2 * n:]
        x, y, c = _place()
        me, sibling = (x, y, c), (x, y, 1 - c)

        def copy(p, k, to):
            return pltpu.make_async_remote_copy(src_ref=_half(g_refs[p].at[k], 1 - c), dst_ref=out_refs[p].at[k],
                                                send_sem=send_sems.at[N_CHIPS * p + k],
                                                recv_sem=recv_sems.at[N_CHIPS * p + k],
                                                device_id=to, device_id_type=MESH)

        cps = [copy(p, k, sibling) for p in range(n) for k in range(N_CHIPS)]
        for cp in cps:
            cp.start()
        for p in range(n):
            for k in range(N_CHIPS):
                copy(p, k, me).wait_recv()
        for cp in cps:
            cp.wait_send()

    return pl.pallas_call(
        body, out_shape=[jax.ShapeDtypeStruct((N_CHIPS, g.shape[1] // 2) + g.shape[2:], g.dtype) for g in gs],
        in_specs=_hbm_specs(n), out_specs=_hbm_specs(n),
        scratch_shapes=[pltpu.SemaphoreType.DMA((N_CHIPS * n,)), pltpu.SemaphoreType.DMA((N_CHIPS * n,))],
        name="sibling_halves",
    )(*gs)


def exchange_chips(ps):
    n = len(ps)

    def body(*refs):
        p_refs, out_refs, (send_sems, recv_sems) = refs[:n], refs[n:2 * n], refs[2 * n:]
        x, y, c = _place()
        me = (x, y, c)
        chips = _other_chips(x, y)

        def copy(p, j, chip, to):
            return pltpu.make_async_remote_copy(src_ref=p_refs[p].at[2 * chip[0] + chip[1]], dst_ref=out_refs[p].at[j],
                                                send_sem=send_sems.at[3 * p + j], recv_sem=recv_sems.at[3 * p + j],
                                                device_id=to, device_id_type=MESH)

        cps = [copy(p, j, chip, (*chip, c)) for p in range(n) for j, chip in enumerate(chips)]
        for cp in cps:
            cp.start()
        for p in range(n):
            for j, chip in enumerate(chips):
                copy(p, j, chip, me).wait_recv()
        for cp in cps:
            cp.wait_send()

    return pl.pallas_call(
        body, out_shape=[jax.ShapeDtypeStruct((3,) + p_.shape[1:], p_.dtype) for p_ in ps],
        in_specs=_hbm_specs(n), out_specs=_hbm_specs(n),
        scratch_shapes=[pltpu.SemaphoreType.DMA((3 * n,)), pltpu.SemaphoreType.DMA((3 * n,))],
        name="exchange_chips",
    )(*ps)


def sibling_join(rs):
    n = len(rs)

    def body(*refs):
        r_refs, out_refs, (send_sems, recv_sems) = refs[:n], refs[n:2 * n], refs[2 * n:]
        x, y, c = _place()

        def copy(p, h, to):
            return pltpu.make_async_remote_copy(src_ref=r_refs[p], dst_ref=_half(out_refs[p], h),
                                                send_sem=send_sems.at[p], recv_sem=recv_sems.at[p],
                                                device_id=to, device_id_type=MESH)

        cps = [copy(p, c, (x, y, 1 - c)) for p in range(n)]
        for cp in cps:
            cp.start()
        for p in range(n):
            copy(p, 1 - c, (x, y, c)).wait_recv()
        for cp in cps:
            cp.wait_send()

    return pl.pallas_call(
        body, out_shape=[jax.ShapeDtypeStruct((2 * r.shape[0],) + r.shape[1:], r.dtype) for r in rs],
        in_specs=_hbm_specs(n), out_specs=_hbm_specs(n),
        scratch_shapes=[pltpu.SemaphoreType.DMA((n,)), pltpu.SemaphoreType.DMA((n,))],
        name="sibling_join",
    )(*rs)


def gather_devices(v, name):
    R = v.shape[0]

    def body(v_ref, out_ref, send_sems, recv_sems, local_sem):
        x, y, c = _place()
        me, sibling = (x, y, c), (x, y, 1 - c)
        chips = _other_chips(x, y)

        def rows(px, py, pc):
            return out_ref.at[pl.ds((4 * px + 2 * py + pc) * R, R), :]

        def copy(k, block, to, src=None):
            return pltpu.make_async_remote_copy(src_ref=rows(*block) if src is None else src, dst_ref=rows(*block),
                                                send_sem=send_sems.at[k], recv_sem=recv_sems.at[k],
                                                device_id=to, device_id_type=MESH)

        mine = pltpu.make_async_copy(v_ref, rows(*me), local_sem)
        mine.start()
        first = [copy(0, me, sibling, src=v_ref)]
        first += [copy(1 + j, me, (*chip, c), src=v_ref) for j, chip in enumerate(chips)]
        for cp in first:
            cp.start()
        passed = [copy(4 + j, (*chip, c), sibling) for j, chip in enumerate(chips)]
        for j, chip in enumerate(chips):
            copy(1 + j, (*chip, c), me).wait_recv()
            passed[j].start()
        copy(0, sibling, me).wait_recv()
        for j, chip in enumerate(chips):
            copy(4 + j, (*chip, 1 - c), me).wait_recv()
        for cp in first + passed:
            cp.wait_send()
        mine.wait()

    return pl.pallas_call(
        body, out_shape=jax.ShapeDtypeStruct((N_DEV * R, LANES), v.dtype),
        in_specs=[pl.BlockSpec(memory_space=pltpu.VMEM)], out_specs=pl.BlockSpec(memory_space=pltpu.VMEM),
        scratch_shapes=[pltpu.SemaphoreType.DMA((7,)), pltpu.SemaphoreType.DMA((7,)), pltpu.SemaphoreType.DMA],
        name=name, compiler_params=_params(),
    )(v)


def _flat_rows(n, mult):
    return -(-n // (LANES * mult)) * mult


def _adam(w, g, m, v):
    m = ADAM_B1 * m + (1.0 - ADAM_B1) * g
    v = ADAM_B2 * v + (1.0 - ADAM_B2) * (g * g)
    m_hat = m / (1.0 - ADAM_B1 ** ADAM_STEP)
    v_hat = v / (1.0 - ADAM_B2 ** ADAM_STEP)
    return -ADAM_LR * (m_hat / (jnp.sqrt(v_hat) + ADAM_EPS) + ADAM_WD * w), m, v


def adam_2d(w, g, m, v, name):
    shape = w.shape
    F = shape[-1]
    if w.ndim == 3 and shape[1] % (2 * SUBLANES) == 0:
        L, R, _ = shape
        tr = R
        for t in (512, 256, 128, 64, 32, 16):
            if R % t == 0 and 7 * t * max(F, LANES) * 4 <= ROW_TILE_BYTES:
                tr = t
                break

        def body(w_ref, g_ref, m_ref, v_ref, d_ref, m2_ref, v2_ref):
            d_ref[...], m2_ref[...], v2_ref[...] = _adam(w_ref[...], g_ref[...], m_ref[...], v_ref[...])

        spec = pl.BlockSpec((None, tr, F), lambda l, i: (l, i, 0))
        return pl.pallas_call(
            body, grid=(L, R // tr), in_specs=[spec] * 4, out_specs=[spec] * 3,
            out_shape=[jax.ShapeDtypeStruct(shape, F32)] * 3, name=name, compiler_params=_params(("parallel", "parallel")),
        )(w, g, m, v)
    a = [t.reshape(-1, F) for t in (w, g, m, v)]
    d, m2, v2 = rowwise(lambda w_, g_, m_, v_: (list(_adam(w_, g_, m_, v_)), []), a, outs=[(F, F32)] * 3, name=name)
    return d.reshape(shape), m2.reshape(shape), v2.reshape(shape)


WEIGHTS = ['ffn1_norm', 'ffn1_wg', 'ffn1_wu', 'ffn1_wd', 'mix_norm', 'ffn2_norm', 'ffn2_wg', 'ffn2_wu', 'ffn2_wd',
           'ple_w', 'ple_norm', 'ple_gate_norm', 'ple_gate_w', 'ev_w_in', 'lru_conv_w', 'lru_conv_b', 'lru_wa',
           'lru_ba', 'lru_wx', 'lru_bx', 'lru_lambda', 'fox_bf', 'fox_q_norm', 'fox_k_norm', 'ev_w_out', 'od_w_in',
           'swa_q_norm', 'swa_k_norm', 'swa_sinks', 's5_lambda_re', 's5_lambda_im', 's5_log_dt', 's5_b_re',
           's5_b_im', 's5_c_re', 's5_c_im', 's5_d', 's5_glu_w', 's5_glu_b', 'od_w_out']
SHARD_AXIS = {'ffn1_wg': 2, 'ffn1_wu': 2, 'ffn1_wd': 1, 'ffn2_wg': 2, 'ffn2_wu': 2, 'ffn2_wd': 1, 'ple_w': 2,
              'ple_gate_w': 1, 'ev_w_in': 2, 'lru_conv_w': 2, 'ev_w_out': 1, 'od_w_in': 2, 's5_d': 1,
              's5_glu_w': 1, 's5_glu_b': 1, 'od_w_out': 1}
EXACT_SHARDED = ('lru_conv_w', 's5_d', 's5_glu_b')
ADAM_TRANSPOSED = ('ffn1_wg', 'ffn1_wu', 'ffn2_wg', 'ffn2_wu', 'od_w_in')
SHARDED = [n for n in WEIGHTS if n in SHARD_AXIS]
REPLICATED = [n for n in WEIGHTS if n not in SHARD_AXIS]


GROUPS = {
    'wgu': ['ffn1_wg', 'ffn1_wu', 'ffn2_wg', 'ffn2_wu'],
    'wd': ['ffn1_wd', 'ffn2_wd'],
    'w_rows': ['ple_gate_w', 'ev_w_out', 'od_w_out'],
    'ple_w': ['ple_w'], 'ev_w_in': ['ev_w_in'], 'od_w_in': ['od_w_in'], 's5_glu_w': ['s5_glu_w'],
}
REDUCED_GROUPS = list(GROUPS)


def _chip():
    return 2 * lax.axis_index("x") + lax.axis_index("y")


def gather_weights(shards):
    own = {k: jnp.concatenate([shards[n] for n in names], axis=0).astype(BF16) for k, names in GROUPS.items()}
    own['exact'] = jnp.concatenate([shards['lru_conv_w'], shards['s5_d'][:, None], shards['s5_glu_b'][:, None]], axis=1)
    keys = list(own)
    got = gather_chips([own[k] for k in keys])
    return {k: lax.dynamic_update_index_in_dim(g, own[k], _chip(), 0) for k, g in zip(keys, got)}


def _rows_by_chip(w):
    return w.reshape(w.shape[0] * w.shape[1], w.shape[2])


def _cols_by_chip(w):
    return w.transpose(1, 0, 2).reshape(w.shape[1], w.shape[0] * w.shape[2])


def _chip_rows(g):
    return g.reshape(N_CHIPS, g.shape[0] // N_CHIPS, g.shape[1])


def _chip_cols(g):
    return g.reshape(g.shape[0], N_CHIPS, g.shape[1] // N_CHIPS).transpose(1, 0, 2)


def full_weights(gw, depth):
    n_ev = (depth + 1) // 2
    ex = gw['exact']
    return dict(
        ple_gate_w=[_rows_by_chip(gw['w_rows'][:, l]) for l in range(depth)],
        ev_w_out=[_rows_by_chip(gw['w_rows'][:, depth + j]) for j in range(n_ev)],
        od_w_out=[_rows_by_chip(gw['w_rows'][:, depth + n_ev + j]) for j in range(depth // 2)],
        ple_w=[_cols_by_chip(gw['ple_w'][:, l]) for l in range(depth)],
        ev_w_in=[_cols_by_chip(gw['ev_w_in'][:, j]) for j in range(n_ev)],
        od_w_in=[_cols_by_chip(gw['od_w_in'][:, j]) for j in range(depth // 2)],
        s5_glu_w=[_rows_by_chip(gw['s5_glu_w'][:, j]) for j in range(depth // 2)],
        lru_conv_w=[_cols_by_chip(ex[:, j, 0:LRU_CONV]) for j in range(n_ev)],
        s5_d=[ex[:, j, LRU_CONV].reshape(-1) for j in range(depth // 2)],
        s5_glu_b=[ex[:, j, LRU_CONV + 1].reshape(-1) for j in range(depth // 2)],
    )


def _add_tile(rows, width):
    for t in (1024, 512, 256, 128, 64, 32, 16):
        if rows % t == 0 and 3 * t * width * 4 <= ROW_TILE_BYTES:
            return t
    return rows


def pair_add(g, t, c, name):
    C, F = g.shape[0], g.shape[-1]
    rows = math.prod(t.shape[1:-1])
    tr = _add_tile(rows, F)
    nb = rows // tr

    def body(c_ref, g_ref, t_ref, o_ref):
        o_ref[...] = (g_ref[...].astype(F32) + t_ref[...].astype(F32)).astype(o_ref.dtype)

    spec = pl.BlockSpec((None, tr, F), lambda k, i, c_ref: (k, i, 0))
    out = pl.pallas_call(
        body, out_shape=jax.ShapeDtypeStruct((C, rows, F), BF16),
        grid_spec=pltpu.PrefetchScalarGridSpec(
            num_scalar_prefetch=1, grid=(C, nb),
            in_specs=[pl.BlockSpec((None, tr, F), lambda k, i, c_ref: (k, c_ref[0] * nb + i, 0)), spec],
            out_specs=spec),
        name=name, compiler_params=_params(("parallel", "parallel")),
    )(c.reshape(1).astype(jnp.int32), g.reshape(C, 2 * rows, F), t.reshape(C, rows, F))
    return out.reshape(t.shape)


def chips_add(p, xs, chip, name):
    F = p.shape[-1]
    rows = math.prod(p.shape[1:-1])
    tr = _add_tile(rows, F)

    def body(m_ref, p_ref, a_ref, b_ref, d_ref, o_ref):
        o_ref[...] = ((p_ref[...].astype(F32) + a_ref[...].astype(F32))
                      + (b_ref[...].astype(F32) + d_ref[...].astype(F32)))

    def other(j):
        return pl.BlockSpec((None, tr, F), lambda i, m_ref: (j, i, 0))

    x3 = xs.reshape(3, rows, F)
    out = pl.pallas_call(
        body, out_shape=jax.ShapeDtypeStruct((rows, F), F32),
        grid_spec=pltpu.PrefetchScalarGridSpec(
            num_scalar_prefetch=1, grid=(rows // tr,),
            in_specs=[pl.BlockSpec((None, tr, F), lambda i, m_ref: (m_ref[0], i, 0)), other(0), other(1), other(2)],
            out_specs=pl.BlockSpec((tr, F), lambda i, m_ref: (i, 0))),
        name=name, compiler_params=_params(("parallel",)),
    )(chip.reshape(1).astype(jnp.int32), p.reshape(N_CHIPS, rows, F), x3, x3, x3)
    return out.reshape(p.shape[1:])


def reduce_sharded(groups):
    keys = list(groups)
    c = lax.axis_index("c")
    gs = [groups[k] for k in keys]
    theirs = sibling_halves(gs)
    pairs = [pair_add(g, t, c, f"pair_add_{k}") for k, g, t in zip(keys, gs, theirs)]
    got = exchange_chips(pairs)
    halves = [chips_add(p_, x_, _chip(), f"chips_add_{k}") for k, p_, x_ in zip(keys, pairs, got)]
    joined = sibling_join(halves)
    out = {}
    for k, h, j in zip(keys, halves, joined):
        out[k] = lax.dynamic_update_slice_in_dim(j, h, c * h.shape[0], axis=0)
    return out


SMALL_GRADS = REPLICATED + list(EXACT_SHARDED)


def _flatten_small(tensors, shapes):
    parts = [tensors[n].astype(F32).reshape(-1) if n in tensors else jnp.zeros((math.prod(shapes[n]),), F32)
             for n in SMALL_GRADS]
    flat = jnp.concatenate(parts)
    rows = _flat_rows(flat.shape[0], SUBLANES)
    return jnp.pad(flat, (0, rows * LANES - flat.shape[0])).reshape(rows, LANES)


def _unflatten_small(flat, shapes):
    flat = flat.reshape(-1)
    out, off = {}, 0
    for n in SMALL_GRADS:
        size = math.prod(shapes[n])
        out[n] = flat[off:off + size].reshape(shapes[n])
        off += size
    return out


def grad_groups(gwgu, gwd, G):
    def st(xs):
        return jnp.stack(xs, axis=1).astype(BF16)

    return {
        'wgu': gwgu, 'wd': gwd,
        'w_rows': st([_chip_rows(g) for n in GROUPS['w_rows'] for g in G[n]]),
        'ple_w': st([_chip_cols(g) for g in G['ple_w']]),
        'ev_w_in': st([_chip_cols(g) for g in G['ev_w_in']]),
        'od_w_in': st([_chip_cols(g) for g in G['od_w_in']]),
        's5_glu_w': st([_chip_rows(g) for g in G['s5_glu_w']]),
    }


def ungroup(red, shapes):
    out = {}
    for k, names in GROUPS.items():
        off = 0
        for n in names:
            out[n] = red[k][off:off + shapes[n][0]]
            off += shapes[n][0]
    return out


def _layer_weights(full, small, i, depth):
    j = i // 2
    w = dict(
        g1=small['ffn1_norm'][i][None], gm=small['mix_norm'][i][None], g2=small['ffn2_norm'][i][None],
        gp=small['ple_norm'][i][None], gg=small['ple_gate_norm'][i][None],
        ffn1=(i, depth + i, i), ffn2=(2 * depth + i, 3 * depth + i, depth + i),
        ple_w=full['ple_w'][i], ple_gate_w=full['ple_gate_w'][i],
    )
    if i % 2 == 0:
        w_in = full['ev_w_in'][j]
        w['mix'] = dict(
            w_in=jnp.pad(w_in, ((0, 0), (0, 2688 - w_in.shape[1]))), w_out=full['ev_w_out'][j],
            conv_w=full['lru_conv_w'][j].astype(F32), conv_b=small['lru_conv_b'][j][None],
            w_ax=jnp.concatenate([_block_diag(small['lru_wa'][j]), _block_diag(small['lru_wx'][j])],
                                 axis=1).astype(BF16),
            ba=small['lru_ba'][j][None], bx=small['lru_bx'][j][None], lam=small['lru_lambda'][j][None],
            bf=jnp.pad(small['fox_bf'][j], (0, LANES - 8))[None], qn=small['fox_q_norm'][j],
            kn=small['fox_k_norm'][j])
    else:
        lam, bexp, cexp, ins = s5_prep_fwd(small['s5_lambda_re'][j], small['s5_lambda_im'][j], small['s5_log_dt'][j],
                                           small['s5_b_re'][j], small['s5_b_im'][j], small['s5_c_re'][j],
                                           small['s5_c_im'][j], f"L{i}")
        w['mix'] = dict(
            w_in=full['od_w_in'][j], w_out=full['od_w_out'][j], qn=small['swa_q_norm'][j], kn=small['swa_k_norm'][j],
            sinks=small['swa_sinks'][j], s5_lam=lam, s5_bexp=bexp, s5_cexp=cexp, s5_ins=ins,
            s5_d=full['s5_d'][j].astype(F32)[None], glu_w=full['s5_glu_w'][j], glu_b=full['s5_glu_b'][j].astype(F32)[None])
    return w


def layer_fwd(x0, n1, p_i, w, ffnw, next_g1, i):
    tag = f"L{i}"
    sv = {}
    wgu, wd = ffnw
    x1, hm, sv['ffn1'] = ffn_fwd(n1, x0, w['gm'], wgu, wd, *w['ffn1'], f"1_{tag}")
    if i % 2 == 0:
        mo, sv['mix'] = even_mixer_fwd(hm, w['mix'], tag)
    else:
        mo, sv['mix'] = odd_mixer_fwd(hm, w['mix'], tag)
    x2, n2 = mm_add_norm(mo, w['mix']['w_out'], x1, w['g2'], f"mix_out_{tag}")
    x3, ng, sv['ffn2'] = ffn_fwd(n2, x2, w['gg'], wgu, wd, *w['ffn2'], f"2_{tag}")
    gpre = mm(ng, w['ple_gate_w'], name=f"ple_gate_{tag}")
    epre = mm(p_i, w['ple_w'], name=f"ple_emb_{tag}")
    D = x0.shape[1]
    if next_g1 is None:
        x4 = rowwise(lambda a, b, c, pn: ([_ple_out(a, b, c, pn)], []), [x3, gpre, epre], [w['gp']],
                     outs=[(D, F32)], name=f"ple_out_{tag}")[0]
        n_next = None
    else:
        def f(a, b, c, pn, gn):
            y = _ple_out(a, b, c, pn)
            return [y, _rms(y, gn)], []

        x4, n_next = rowwise(f, [x3, gpre, epre], [w['gp'], next_g1], outs=[(D, F32), (D, BF16)],
                             name=f"ple_out_{tag}")
    sv.update(x0=x0, x1=x1, x2=x2, x3=x3, ng=ng, gpre=gpre, epre=epre, p=p_i)
    return x4, n_next, sv


def layer_bwd(dx4, sv, w, ffnw, gbuf, i):
    tag = f"L{i}"
    D = dx4.shape[1]
    g = {}
    wgu, wd = ffnw

    def f_ple(a, b, c, d, pn):
        da, db, dc, dpn = _vjp(_ple_out, (a, b, c, pn), d)
        return [db, dc], [dpn]

    dgpre, depre, dgp = rowwise(f_ple, [sv['x3'], sv['gpre'], sv['epre'], dx4], [w['gp']],
                                outs=[(D, BF16), (D, BF16)], accs=[(1, D)], name=f"ple_out_bwd_{tag}")
    g['gp'] = dgp[0]
    g['ple_w'] = mm(sv['p'], depre, "tn", name=f"ple_emb_dw_{tag}")
    g['ple_gate_w'] = mm(sv['ng'], dgpre, "tn", name=f"ple_gate_dw_{tag}")
    dx3, dgg = mm_norm_bwd(dgpre, w['ple_gate_w'], sv['x3'], w['gg'], dx4, f"ple_gate_dx_{tag}")
    g['gg'] = dgg[0]
    dx2, dg2, gbuf = ffn_bwd(dx3, sv['ffn2'], sv['x2'], w['g2'], wgu, wd, *w['ffn2'], gbuf, f"2_{tag}")
    g['g2'] = dg2[0]
    if i % 2 == 0:
        dz, g['mix'] = even_mixer_bwd(dx2, sv['mix'], w['mix'], tag)
    else:
        dz, g['mix'] = odd_mixer_bwd(dx2, sv['mix'], w['mix'], tag)
    dx1, dgm = mm_norm_bwd(dz, w['mix']['w_in'], sv['x1'], w['gm'], dx2, f"mix_dh_{tag}")
    g['gm'] = dgm[0]
    dx0, dg1, gbuf = ffn_bwd(dx1, sv['ffn1'], sv['x0'], w['g1'], wgu, wd, *w['ffn1'], gbuf, f"1_{tag}")
    g['g1'] = dg1[0]
    return dx0, g, gbuf


def _collect_grads(layer_grads, depth):
    st = lambda xs: jnp.stack(xs)
    G = {}
    L = layer_grads
    G['ffn1_norm'] = st([g['g1'] for g in L])
    G['mix_norm'] = st([g['gm'] for g in L])
    G['ffn2_norm'] = st([g['g2'] for g in L])
    G['ple_norm'] = st([g['gp'] for g in L])
    G['ple_gate_norm'] = st([g['gg'] for g in L])
    G['ple_w'] = st([g['ple_w'] for g in L])
    G['ple_gate_w'] = st([g['ple_gate_w'] for g in L])
    ev = [L[i]['mix'] for i in range(0, depth, 2)]
    od = [L[i]['mix'] for i in range(1, depth, 2)]
    G['ev_w_in'] = st([m['w_in'][:, :2568] for m in ev])
    G['ev_w_out'] = st([m['w_out'] for m in ev])
    G['lru_conv_w'] = st([m['conv_w'] for m in ev])
    G['lru_conv_b'] = st([m['conv_b'] for m in ev])
    G['lru_wa'] = st([_block_diag_take(m['w_ax'][:, :512], LRU_BLOCKS) for m in ev])
    G['lru_wx'] = st([_block_diag_take(m['w_ax'][:, 512:], LRU_BLOCKS) for m in ev])
    G['lru_ba'] = st([m['ba'] for m in ev])
    G['lru_bx'] = st([m['bx'] for m in ev])
    G['lru_lambda'] = st([m['lam'] for m in ev])
    G['fox_bf'] = st([m['bf'] for m in ev])
    G['fox_q_norm'] = st([m['qn'] for m in ev])
    G['fox_k_norm'] = st([m['kn'] for m in ev])
    G['od_w_in'] = st([m['w_in'] for m in od])
    G['od_w_out'] = st([m['w_out'] for m in od])
    G['swa_q_norm'] = st([m['qn'] for m in od])
    G['swa_k_norm'] = st([m['kn'] for m in od])
    G['swa_sinks'] = st([m['sinks'] for m in od])
    G['s5_lambda_re'] = st([m['s5']['lre'] for m in od])
    G['s5_lambda_im'] = st([m['s5']['lim'] for m in od])
    G['s5_log_dt'] = st([m['s5']['ldt'] for m in od])
    G['s5_b_re'] = st([m['s5']['bre'] for m in od])
    G['s5_b_im'] = st([m['s5']['bim'] for m in od])
    G['s5_c_re'] = st([m['s5']['cre'] for m in od])
    G['s5_c_im'] = st([m['s5']['cim'] for m in od])
    G['s5_d'] = st([m['s5_d'] for m in od])
    G['s5_glu_w'] = st([m['glu_w'] for m in od])
    G['s5_glu_b'] = st([m['glu_b'] for m in od])
    return G


def local_step(x, p, target, ffnw, full, small):
    depth = p.shape[0]
    S, D = x.shape
    ws = [_layer_weights(full, small, i, depth) for i in range(depth)]
    saved = []
    xi, ni = add_norm(x, None, ws[0]['g1'], "norm1_L0")
    for i in range(depth):
        xi, ni, sv = layer_fwd(xi, ni, p[i], ws[i], ffnw, ws[i + 1]['g1'] if i + 1 < depth else None, i)
        saved.append(sv)

    def f_loss(y, t):
        e = y - t
        return [e * (1.0 / D)], [0.5 * jnp.sum(jnp.mean(e * e, axis=-1, keepdims=True), axis=0, keepdims=True)]

    dx, loss = rowwise(f_loss, [xi, target], outs=[(D, F32)], accs=[(1, 1)], name="loss")
    grads = [None] * depth
    gbuf = (None, None)
    for i in reversed(range(depth)):
        dx, grads[i], gbuf = layer_bwd(dx, saved[i], ws[i], ffnw, gbuf, i)
        if i % 2 == 1:
            m = grads[i]['mix']
            m['s5'] = s5_prep_bwd(ws[i]['mix']['s5_ins'], m['s5_lam'], m['s5_bexp'], m['s5_cexp'], f"L{i}")
    return loss[0, 0], dx, gbuf, _collect_grads(grads, depth)


def kernel(x, p, ffn1_norm, ffn1_wg, ffn1_wu, ffn1_wd, mix_norm, ffn2_norm, ffn2_wg, ffn2_wu, ffn2_wd, ple_w, ple_norm, ple_gate_norm, ple_gate_w, ev_w_in, lru_conv_w, lru_conv_b, lru_wa, lru_ba, lru_wx, lru_bx, lru_lambda, fox_bf, fox_q_norm, fox_k_norm, ev_w_out, od_w_in, swa_q_norm, swa_k_norm, swa_sinks, s5_lambda_re, s5_lambda_im, s5_log_dt, s5_b_re, s5_b_im, s5_c_re, s5_c_im, s5_d, s5_glu_w, s5_glu_b, od_w_out, loss_target, m_ffn1_norm, m_ffn1_wg, m_ffn1_wu, m_ffn1_wd, m_mix_norm, m_ffn2_norm, m_ffn2_wg, m_ffn2_wu, m_ffn2_wd, m_ple_w, m_ple_norm, m_ple_gate_norm, m_ple_gate_w, m_ev_w_in, m_lru_conv_w, m_lru_conv_b, m_lru_wa, m_lru_ba, m_lru_wx, m_lru_bx, m_lru_lambda, m_fox_bf, m_fox_q_norm, m_fox_k_norm, m_ev_w_out, m_od_w_in, m_swa_q_norm, m_swa_k_norm, m_swa_sinks, m_s5_lambda_re, m_s5_lambda_im, m_s5_log_dt, m_s5_b_re, m_s5_b_im, m_s5_c_re, m_s5_c_im, m_s5_d, m_s5_glu_w, m_s5_glu_b, m_od_w_out, v_ffn1_norm, v_ffn1_wg, v_ffn1_wu, v_ffn1_wd, v_mix_norm, v_ffn2_norm, v_ffn2_wg, v_ffn2_wu, v_ffn2_wd, v_ple_w, v_ple_norm, v_ple_gate_norm, v_ple_gate_w, v_ev_w_in, v_lru_conv_w, v_lru_conv_b, v_lru_wa, v_lru_ba, v_lru_wx, v_lru_bx, v_lru_lambda, v_fox_bf, v_fox_q_norm, v_fox_k_norm, v_ev_w_out, v_od_w_in, v_swa_q_norm, v_swa_k_norm, v_swa_sinks, v_s5_lambda_re, v_s5_lambda_im, v_s5_log_dt, v_s5_b_re, v_s5_b_im, v_s5_c_re, v_s5_c_im, v_s5_d, v_s5_glu_w, v_s5_glu_b, v_od_w_out):
    args = locals()
    wts = {n: args[n] for n in WEIGHTS}
    ms = {n: args["m_" + n] for n in WEIGHTS}
    vs = {n: args["v_" + n] for n in WEIGHTS}
    shapes = {n: wts[n].shape for n in WEIGHTS}

    depth = p.shape[0]
    gw = gather_weights({n: wts[n] for n in SHARDED})
    small = {n: wts[n] for n in REPLICATED}
    loss, dx, (gwgu, gwd), G = local_step(x[0], p[:, 0], loss_target[0], (gw['wgu'], gw['wd']),
                                          full_weights(gw, depth), small)
    loss = lax.psum(loss, ("x", "y", "c"))

    gsh = ungroup(reduce_sharded(grad_groups(gwgu, gwd, G)), shapes)
    full_shapes = {n: (G[n].shape if n in EXACT_SHARDED else shapes[n]) for n in SMALL_GRADS}
    flat_g = _flatten_small(G, full_shapes)
    g8 = gather_devices(flat_g, "gather_small_grads").reshape((N_DEV,) + flat_g.shape)
    wf, mf, vf = (_flatten_small({n: t[n] for n in REPLICATED}, full_shapes) for t in (wts, ms, vs))

    def f_small(g0, g1, g2, g3, g4, g5, g6, g7, w_, m_, v_):
        gsum = ((g0 + g1) + (g2 + g3)) + ((g4 + g5) + (g6 + g7))
        return [gsum] + list(_adam(w_, gsum, m_, v_)), []

    gs_f, ds_f, ms_f, vs_f = rowwise(f_small, [g8[d] for d in range(N_DEV)] + [wf, mf, vf],
                                     outs=[(LANES, F32)] * 4, name="adam_small")
    out_g, out_d, out_m, out_v = {}, {}, {}, {}
    for dst, flat in ((out_g, gs_f), (out_d, ds_f), (out_m, ms_f), (out_v, vs_f)):
        dst.update(_unflatten_small(flat, full_shapes))
    for n in EXACT_SHARDED:
        width = shapes[n][SHARD_AXIS[n]]
        gsh[n] = lax.dynamic_slice_in_dim(out_g[n], _chip() * width, width, axis=SHARD_AXIS[n])
    for n in SHARDED:
        out_g[n] = gsh[n]
        if n in ADAM_TRANSPOSED:
            def t(a):
                return a.transpose(0, 2, 1)
            d_, m_, v_ = adam_2d(t(wts[n]), t(gsh[n]), t(ms[n]), t(vs[n]), f"adam_{n}")
            out_d[n], out_m[n], out_v[n] = t(d_), t(m_), t(v_)
        else:
            out_d[n], out_m[n], out_v[n] = adam_2d(wts[n], gsh[n], ms[n], vs[n], f"adam_{n}")
    return (loss, dx[None], *[out_g[n] for n in WEIGHTS], *[out_d[n] for n in WEIGHTS],
            *[out_m[n] for n in WEIGHTS], *[out_v[n] for n in WEIGHTS])
```
